```python
import math
import jax, jax.numpy as jnp
from jax import lax
import numpy as np


D_MODEL = 1024
BATCH = 4
SEQ = 4096
DEPTH = 2

HEAD_DIM = 128
FOX_HEADS = 4
GDN_HEADS = 4
MOBA_HEADS = D_MODEL // HEAD_DIM
FOX_WIDTH = FOX_HEADS * HEAD_DIM
GDN_WIDTH = GDN_HEADS * HEAD_DIM
MOBA_WIDTH = MOBA_HEADS * HEAD_DIM
Q_BLOCK = 128
GDN_CHUNK = 64
CONV_WIDTH = 4
MOBA_BLOCK = 256
MOBA_TOPK = 3
MOBA_Q_CHUNK = 16
D_FF_DENSE = 2816
N_EXPERTS = 8
TOP_K_EXPERTS = 2
D_FF_EXPERT = 3584
ROPE_THETA = 10000.0
EPS = 1e-6
N_EVEN = (DEPTH + 1) // 2
N_ODD = DEPTH // 2
IN_WIDTHS = (FOX_WIDTH, FOX_WIDTH, FOX_WIDTH, FOX_HEADS, GDN_WIDTH, GDN_WIDTH, GDN_WIDTH, GDN_HEADS, GDN_HEADS, GDN_WIDTH)
IN_COLS = sum(IN_WIDTHS)

kernel_name = 'hybrid_fox_gdn_moba_moe'

F32 = jnp.float32


def rms_norm(x, gain):
    xf = x.astype(F32)
    y = xf * lax.rsqrt(jnp.mean(xf * xf, axis=-1, keepdims=True) + EPS)
    return (y * gain.astype(F32)).astype(x.dtype)


def l2_normalize(x):
    xf = x.astype(F32)
    return xf * lax.rsqrt(jnp.sum(xf * xf, axis=-1, keepdims=True) + EPS)


def to_heads(t, n):
    B, S, _ = t.shape
    return t.reshape(B, S, n, HEAD_DIM).transpose(0, 2, 1, 3)


def from_heads(t):
    B, H, S, D = t.shape
    return t.transpose(0, 2, 1, 3).reshape(B, S, H * D)


def rope(x, pos):
    half = HEAD_DIM // 2
    inv = jnp.power(ROPE_THETA, -jnp.arange(half, dtype=F32) / half)
    ang = pos.astype(F32)[:, None] * inv[None, :]
    cos, sin = jnp.cos(ang), jnp.sin(ang)
    xf = x.astype(F32)
    x1, x2 = xf[..., :half], xf[..., half:]
    return jnp.concatenate([x1 * cos - x2 * sin, x2 * cos + x1 * sin], axis=-1).astype(x.dtype)


def forgetting_attention(q, k, v, log_f):
    B, H, S, D = q.shape
    F = jnp.cumsum(log_f, axis=-1)
    scale = D ** -0.5
    kpos = jnp.arange(S)

    def block(i):
        start = i * Q_BLOCK
        qb = lax.dynamic_slice_in_dim(q, start, Q_BLOCK, axis=2)
        Fq = lax.dynamic_slice_in_dim(F, start, Q_BLOCK, axis=2)
        qpos = start + jnp.arange(Q_BLOCK)
        s = jnp.einsum('bhqd,bhkd->bhqk', qb, k, preferred_element_type=F32) * scale
        s = s + Fq[..., :, None] - F[..., None, :]
        s = jnp.where(kpos[None, :] <= qpos[:, None], s, -jnp.inf)
        p = jax.nn.softmax(s, axis=-1)
        return jnp.einsum('bhqk,bhkd->bhqd', p.astype(v.dtype), v)

    out = lax.map(block, jnp.arange(S // Q_BLOCK))
    return jnp.moveaxis(out, 0, 2).reshape(B, H, S, D)


def causal_depthwise_conv(x, w):
    K, C = w.shape
    return lax.conv_general_dilated(x, w[:, None, :], window_strides=(1,), padding=[(K - 1, 0)],
                                    dimension_numbers=('NWC', 'WIO', 'NWC'), feature_group_count=C)


def gated_delta_rule(q, k, v, g, beta):
    B, H, S, Dk = q.shape
    Dv = v.shape[-1]
    C = GDN_CHUNK
    N = S // C
    q = q * Dk ** -0.5
    q, k, v = (t.reshape(B, H, N, C, t.shape[-1]) for t in (q, k, v))
    g, beta = (t.reshape(B, H, N, C) for t in (g, beta))
    gc = jnp.cumsum(g, axis=-1)
    idx = jnp.arange(C)
    strict = idx[:, None] > idx[None, :]
    incl = idx[:, None] >= idx[None, :]
    diff = gc[..., :, None] - gc[..., None, :]
    decay = jnp.where(incl, jnp.exp(jnp.where(incl, diff, 0.0)), 0.0)
    kb = k * beta[..., None]
    M = jnp.where(strict, jnp.einsum('bhncd,bhnsd->bhncs', kb, k) * decay, 0.0)
    A = jnp.eye(C, dtype=F32) + M
    rhs = jnp.concatenate([v * beta[..., None], kb * jnp.exp(gc)[..., None]], axis=-1)
    sol = lax.linalg.triangular_solve(A, rhs, left_side=True, lower=True, unit_diagonal=True)
    u, w = sol[..., :Dv], sol[..., Dv:]
    attn = jnp.einsum('bhncd,bhnsd->bhncs', q, k) * decay
    qg = q * jnp.exp(gc)[..., None]
    kg = k * jnp.exp(gc[..., -1:] - gc)[..., None]
    glast = jnp.exp(gc[..., -1])

    def step(state, xs):
        u_c, w_c, attn_c, qg_c, kg_c, gl_c = xs
        v_new = u_c - jnp.einsum('bhcd,bhde->bhce', w_c, state)
        o = jnp.einsum('bhcd,bhde->bhce', qg_c, state) + jnp.einsum('bhcs,bhse->bhce', attn_c, v_new)
        state = state * gl_c[..., None, None] + jnp.einsum('bhcd,bhce->bhde', kg_c, v_new)
        return state, o

    xs = tuple(jnp.moveaxis(t, 2, 0) for t in (u, w, attn, qg, kg, glast))
    _, o = lax.scan(step, jnp.zeros((B, H, Dk, Dv), F32), xs)
    return jnp.moveaxis(o, 0, 2).reshape(B, H, S, Dv)


def gdn_branch(gq, gk, gv, ga, gb, gg, conv_w, a_log, dt_bias, o_norm):
    B, S, _ = gq.shape
    qkv = jnp.concatenate([gq, gk, gv], axis=-1).astype(F32)
    qkv = jax.nn.silu(causal_depthwise_conv(qkv, conv_w.astype(F32)))
    q, k, v = jnp.split(qkv, 3, axis=-1)
    q = l2_normalize(to_heads(q, GDN_HEADS))
    k = l2_normalize(to_heads(k, GDN_HEADS))
    v = to_heads(v, GDN_HEADS)
    beta = jax.nn.sigmoid(gb.astype(F32)).transpose(0, 2, 1)
    g = (-jnp.exp(a_log.astype(F32)) * jax.nn.softplus(ga.astype(F32) + dt_bias.astype(F32))).transpose(0, 2, 1)
    o = gated_delta_rule(q, k, v, g, beta)
    o = rms_norm(o.transpose(0, 2, 1, 3), o_norm) * jax.nn.silu(gg.astype(F32).reshape(B, S, GDN_HEADS, HEAD_DIM))
    return o.reshape(B, S, GDN_WIDTH).astype(gq.dtype)


def even_mixer(h, w_in, fox_f_bias, fox_q_norm, fox_k_norm, gdn_conv, gdn_a_log, gdn_dt_bias, gdn_o_norm, w_out):
    z = h @ w_in
    offsets = np.cumsum(IN_WIDTHS)[:-1].tolist()
    fq, fk, fv, ff, gq, gk, gv, ga, gb, gg = jnp.split(z, offsets, axis=-1)
    q = rms_norm(to_heads(fq, FOX_HEADS), fox_q_norm)
    k = rms_norm(to_heads(fk, FOX_HEADS), fox_k_norm)
    v = to_heads(fv, FOX_HEADS)
    log_f = jax.nn.log_sigmoid(ff.astype(F32) + fox_f_bias.astype(F32)).transpose(0, 2, 1)
    fox_out = from_heads(forgetting_attention(q, k, v, log_f))
    gdn_out = gdn_branch(gq, gk, gv, ga, gb, gg, gdn_conv, gdn_a_log, gdn_dt_bias, gdn_o_norm)
    return jnp.concatenate([fox_out, gdn_out.astype(fox_out.dtype)], axis=-1) @ w_out


def moba_attention(q, k, v):
    B, H, S, D = q.shape
    nblk = -(-S // MOBA_BLOCK)
    pad = nblk * MOBA_BLOCK - S
    if pad:
        k = jnp.pad(k, ((0, 0), (0, 0), (0, pad), (0, 0)))
        v = jnp.pad(v, ((0, 0), (0, 0), (0, pad), (0, 0)))
    topk = min(MOBA_TOPK, nblk)
    scale = D ** -0.5
    kb = k.reshape(B, H, nblk, MOBA_BLOCK, D)
    vb = v.reshape(B, H, nblk, MOBA_BLOCK, D)
    kmean = jnp.mean(kb.astype(F32), axis=3)
    k_flat = kb.reshape(B * H * nblk, MOBA_BLOCK, D)
    v_flat = vb.reshape(B * H * nblk, MOBA_BLOCK, D)
    bh_off = (jnp.arange(B * H) * nblk).reshape(B, H, 1, 1)
    blk_ids = jnp.arange(nblk)
    inner = jnp.arange(MOBA_Q_CHUNK)
    kin = jnp.arange(MOBA_BLOCK)
    n_sel = topk * MOBA_BLOCK

    def chunk(ci):
        start = ci * MOBA_Q_CHUNK
        qc = lax.dynamic_slice_in_dim(q, start, MOBA_Q_CHUNK, axis=2)
        cur = start // MOBA_BLOCK
        gate = jnp.einsum('bhqd,bhnd->bhqn', qc.astype(F32), kmean)
        gate = jnp.where(blk_ids < cur, gate, -jnp.inf)
        _, sel = lax.top_k(gate, topk)
        valid = sel < cur
        ks = k_flat[bh_off + sel]
        vs = v_flat[bh_off + sel]
        s_sel = jnp.einsum('bhqd,bhqjsd->bhqjs', qc, ks, preferred_element_type=F32) * scale
        s_sel = jnp.where(valid[..., None], s_sel, -jnp.inf).reshape(B, H, MOBA_Q_CHUNK, n_sel)
        k_own = kb[:, :, cur]
        v_own = vb[:, :, cur]
        qpos_in = start - cur * MOBA_BLOCK + inner
        s_own = jnp.einsum('bhqd,bhsd->bhqs', qc, k_own, preferred_element_type=F32) * scale
        s_own = jnp.where(kin[None, :] <= qpos_in[:, None], s_own, -jnp.inf)
        p = jax.nn.softmax(jnp.concatenate([s_sel, s_own], axis=-1), axis=-1).astype(v.dtype)
        p_sel = p[..., :n_sel].reshape(B, H, MOBA_Q_CHUNK, topk, MOBA_BLOCK)
        return (jnp.einsum('bhqjs,bhqjsd->bhqd', p_sel, vs)
                + jnp.einsum('bhqs,bhsd->bhqd', p[..., n_sel:], v_own))

    out = lax.map(chunk, jnp.arange(S // MOBA_Q_CHUNK))
    return jnp.moveaxis(out, 0, 2).reshape(B, H, S, D)


def odd_mixer(h, w_qkv, q_norm, k_norm, w_out, pos):
    fq, fk, fv = jnp.split(h @ w_qkv, 3, axis=-1)
    q = rope(rms_norm(to_heads(fq, MOBA_HEADS), q_norm), pos)
    k = rope(rms_norm(to_heads(fk, MOBA_HEADS), k_norm), pos)
    v = to_heads(fv, MOBA_HEADS)
    return from_heads(moba_attention(q, k, v)) @ w_out


def swiglu(h, w_gate_up, w_down):
    gt, up = jnp.split(h @ w_gate_up, 2, axis=-1)
    return (jax.nn.silu(gt) * up) @ w_down


def moe_swiglu(h, w_router, w_gate_up, w_down):
    B, S, D = h.shape
    t = h.reshape(B * S, D)
    logits = (t @ w_router).astype(F32)
    top_v, top_i = lax.top_k(logits, TOP_K_EXPERTS)
    gates = jax.nn.softmax(top_v, axis=-1)
    combine = jnp.sum(jax.nn.one_hot(top_i, N_EXPERTS, dtype=F32) * gates[..., None], axis=1)
    out = jnp.zeros_like(t)
    for e in range(N_EXPERTS):
        out = out + combine[:, e:e + 1].astype(t.dtype) * swiglu(t, w_gate_up[e], w_down[e])
    return out.reshape(B, S, D)


def setup_inputs(seed: int = 0) -> dict:
    key = jax.random.key(seed)
    ks = jax.random.split(key, 32)

    def nrm(k, shape, scale):
        return jax.random.normal(k, shape, F32) * scale

    NE, NO = N_EVEN, N_ODD
    dinv = D_MODEL ** -0.5
    dt = jnp.exp(jax.random.uniform(ks[8], (NE, GDN_HEADS), F32, math.log(1e-3), math.log(1e-1)))
    return {
        'x': nrm(ks[0], (BATCH, SEQ, D_MODEL), 1.0),
        'e_norm1': 1.0 + nrm(ks[1], (NE, D_MODEL), 0.02),
        'e_w_in': nrm(ks[2], (NE, D_MODEL, IN_COLS), dinv),
        'e_fox_f_bias': 2.0 + nrm(ks[3], (NE, FOX_HEADS), 0.5),
        'e_fox_q_norm': 1.0 + nrm(ks[4], (NE, HEAD_DIM), 0.02),
        'e_fox_k_norm': 1.0 + nrm(ks[5], (NE, HEAD_DIM), 0.02),
        'e_gdn_conv': nrm(ks[6], (NE, CONV_WIDTH, 3 * GDN_WIDTH), CONV_WIDTH ** -0.5),
        'e_gdn_a_log': jnp.log(jax.random.uniform(ks[7], (NE, GDN_HEADS), F32, 1.0, 16.0)),
        'e_gdn_dt_bias': dt + jnp.log(-jnp.expm1(-dt)),
        'e_gdn_o_norm': 1.0 + nrm(ks[9], (NE, HEAD_DIM), 0.02),
        'e_w_out': nrm(ks[10], (NE, FOX_WIDTH + GDN_WIDTH, D_MODEL), (FOX_WIDTH + GDN_WIDTH) ** -0.5),
        'e_norm2': 1.0 + nrm(ks[11], (NE, D_MODEL), 0.02),
        'e_ffn_w_gate_up': nrm(ks[12], (NE, D_MODEL, 2 * D_FF_DENSE), dinv),
        'e_ffn_w_down': nrm(ks[13], (NE, D_FF_DENSE, D_MODEL), D_FF_DENSE ** -0.5),
        'o_norm1': 1.0 + nrm(ks[14], (NO, D_MODEL), 0.02),
        'o_w_qkv': nrm(ks[15], (NO, D_MODEL, 3 * MOBA_WIDTH), dinv),
        'o_q_norm': 1.0 + nrm(ks[16], (NO, HEAD_DIM), 0.02),
        'o_k_norm': 1.0 + nrm(ks[17], (NO, HEAD_DIM), 0.02),
        'o_w_out': nrm(ks[18], (NO, MOBA_WIDTH, D_MODEL), MOBA_WIDTH ** -0.5),
        'o_norm2': 1.0 + nrm(ks[19], (NO, D_MODEL), 0.02),
        'o_router': nrm(ks[20], (NO, D_MODEL, N_EXPERTS), dinv),
        'o_exp_w_gate_up': nrm(ks[21], (NO, N_EXPERTS, D_MODEL, 2 * D_FF_EXPERT), dinv),
        'o_exp_w_down': nrm(ks[22], (NO, N_EXPERTS, D_FF_EXPERT, D_MODEL), D_FF_EXPERT ** -0.5),
    }


def reference(x, e_norm1, e_w_in, e_fox_f_bias, e_fox_q_norm, e_fox_k_norm, e_gdn_conv, e_gdn_a_log,
              e_gdn_dt_bias, e_gdn_o_norm, e_w_out, e_norm2, e_ffn_w_gate_up, e_ffn_w_down,
              o_norm1, o_w_qkv, o_q_norm, o_k_norm, o_w_out, o_norm2, o_router, o_exp_w_gate_up, o_exp_w_down):
    pos = jnp.arange(x.shape[1])
    h = x
    for layer in range(DEPTH):
        i = layer // 2
        if layer % 2 == 0:
            h = h + even_mixer(rms_norm(h, e_norm1[i]), e_w_in[i], e_fox_f_bias[i], e_fox_q_norm[i], e_fox_k_norm[i],
                               e_gdn_conv[i], e_gdn_a_log[i], e_gdn_dt_bias[i], e_gdn_o_norm[i], e_w_out[i])
            h = h + swiglu(rms_norm(h, e_norm2[i]), e_ffn_w_gate_up[i], e_ffn_w_down[i])
        else:
            h = h + odd_mixer(rms_norm(h, o_norm1[i]), o_w_qkv[i], o_q_norm[i], o_k_norm[i], o_w_out[i], pos)
            h = h + moe_swiglu(rms_norm(h, o_norm2[i]), o_router[i], o_exp_w_gate_up[i], o_exp_w_down[i])
    return h
```

```python
import functools

import jax
import jax.numpy as jnp
from jax import lax
from jax.experimental import pallas as pl
from jax.experimental.pallas import tpu as pltpu

F32 = jnp.float32
BF16 = jnp.bfloat16

HEAD_DIM = 128
FOX_HEADS = 4
GDN_HEADS = 4
CONV_WIDTH = 4
MOBA_BLOCK = 256
MOBA_TOPK = 3
N_EXPERTS = 8
ROPE_THETA = 10000.0
EPS = 1e-6

LANES = 128
GDN_CHUNK = 128
INV_BLOCK = 16
NEG = -1e30
VMEM_LIMIT_BYTES = 56 * 1024 * 1024

FOX_Q0, FOX_K0, FOX_V0 = 0, 4, 8
GDN_Q0, GDN_K0, GDN_V0, GDN_G0 = 12, 16, 20, 24
LANE_F, LANE_A, LANE_B = 0, 4, 8


def _params(*sem):
    return pltpu.CompilerParams(dimension_semantics=sem, vmem_limit_bytes=VMEM_LIMIT_BYTES)


def _rms(x, gain):
    return x * lax.rsqrt(jnp.mean(x * x, axis=-1, keepdims=True) + EPS) * gain


def _dot_nt(a, b, **kw):
    return lax.dot_general(a, b, (((1,), (1,)), ((), ())), preferred_element_type=F32, **kw)


def _pick_lane(x, lane_idx):
    lane = lax.broadcasted_iota(jnp.int32, x.shape, 1)
    return jnp.sum(jnp.where(lane == lane_idx, x, 0.0), axis=-1, keepdims=True)


def _norm_mm_kernel(x_ref, g_ref, w_ref, *rest, has_aux):
    if has_aux:
        waux_ref, o_ref, oaux_ref, xn_ref = rest
    else:
        o_ref, xn_ref = rest

    @pl.when(pl.program_id(1) == 0)
    def _():
        xn = _rms(x_ref[...], g_ref[...]).astype(BF16)
        xn_ref[...] = xn
        if has_aux:
            oaux_ref[...] = jnp.dot(xn, waux_ref[...], preferred_element_type=F32)

    o_ref[...] = jnp.dot(xn_ref[...], w_ref[...], preferred_element_type=F32).astype(o_ref.dtype)


def norm_matmul(x, gain, w, w_aux=None, *, tm, tn, out_dtype=BF16):
    M, K = x.shape
    N = w.shape[1]
    has_aux = w_aux is not None
    in_specs = [
        pl.BlockSpec((tm, K), lambda i, j: (i, 0)),
        pl.BlockSpec((1, K), lambda i, j: (0, 0)),
        pl.BlockSpec((K, tn), lambda i, j: (0, j)),
    ]
    out_shape = [jax.ShapeDtypeStruct((M, N), out_dtype)]
    out_specs = [pl.BlockSpec((tm, tn), lambda i, j: (i, j))]
    args = [x, gain.reshape(1, K), w]
    if has_aux:
        in_specs.append(pl.BlockSpec((K, LANES), lambda i, j: (0, 0)))
        out_shape.append(jax.ShapeDtypeStruct((M, LANES), F32))
        out_specs.append(pl.BlockSpec((tm, LANES), lambda i, j: (i, 0)))
        args.append(w_aux)
    res = pl.pallas_call(
        functools.partial(_norm_mm_kernel, has_aux=has_aux),
        grid=(M // tm, N // tn),
        in_specs=in_specs,
        out_specs=out_specs,
        out_shape=out_shape,
        scratch_shapes=[pltpu.VMEM((tm, K), BF16)],
        compiler_params=_params("parallel", "arbitrary"),
        name="norm_matmul",
    )(*args)
    return res if has_aux else res[0]


def _mm_res_kernel(*refs, n_in):
    res_ref = refs[0]
    o_ref = refs[1 + 2 * n_in]
    acc = res_ref[...]
    for t in range(n_in):
        acc = acc + jnp.dot(refs[1 + 2 * t][...], refs[2 + 2 * t][...], preferred_element_type=F32)
    o_ref[...] = acc


def matmul_residual(res, pairs, *, tm, tn):
    M, N = res.shape
    in_specs = [pl.BlockSpec((tm, tn), lambda i, j: (i, j))]
    args = [res]
    for a, w in pairs:
        K = a.shape[1]
        in_specs.append(pl.BlockSpec((tm, K), lambda i, j: (i, 0)))
        in_specs.append(pl.BlockSpec((K, tn), lambda i, j: (0, j)))
        args += [a, w]
    return pl.pallas_call(
        functools.partial(_mm_res_kernel, n_in=len(pairs)),
        grid=(M // tm, N // tn),
        in_specs=in_specs,
        out_specs=pl.BlockSpec((tm, tn), lambda i, j: (i, j)),
        out_shape=jax.ShapeDtypeStruct((M, N), F32),
        compiler_params=_params("parallel", "arbitrary"),
        name="matmul_residual",
    )(*args)


def _gate_kernel(zs_ref, par_ref, col_ref, row_ref, *, S):
    C = GDN_CHUNK
    bias = par_ref[0:1, :]
    neg_a = -jnp.exp(par_ref[1:2, :])
    r = lax.broadcasted_iota(jnp.int32, (C, C), 0)
    c = lax.broadcasted_iota(jnp.int32, (C, C), 1)
    tril = (r >= c).astype(F32)
    lane = lax.broadcasted_iota(jnp.int32, (C, LANES), 1)

    def body(n, carry):
        sl = pl.ds(pl.multiple_of(n * C, C), C)
        z = zs_ref[sl, :]
        t = z + bias
        soft = jnp.log(1.0 + jnp.exp(-jnp.abs(t)))
        log_f = jnp.minimum(t, 0.0) - soft
        g = neg_a * (jnp.maximum(t, 0.0) + soft)
        beta = 1.0 / (1.0 + jnp.exp(-z))
        u = jnp.where(lane < LANE_A, log_f, jnp.where(lane < LANE_B, g, 0.0))
        cs = jnp.dot(tril, u, preferred_element_type=F32, precision=lax.Precision.HIGHEST)
        cs = cs + jnp.where(lane < LANE_A, carry, 0.0)
        out = jnp.where(lane < LANE_B, cs, beta)
        col_ref[sl, :] = out
        row_ref[:, sl] = out.T[0:16, :]
        return cs[C - 1:C, :]

    lax.fori_loop(0, S // C, body, jnp.zeros((1, LANES), F32))


def even_gates(zs, par):
    B, S, _ = zs.shape
    return pl.pallas_call(
        functools.partial(_gate_kernel, S=S),
        grid=(B,),
        in_specs=[
            pl.BlockSpec((None, S, LANES), lambda b: (b, 0, 0)),
            pl.BlockSpec((8, LANES), lambda b: (0, 0)),
        ],
        out_specs=[
            pl.BlockSpec((None, S, LANES), lambda b: (b, 0, 0)),
            pl.BlockSpec((None, 16, S), lambda b: (b, 0, 0)),
        ],
        out_shape=[jax.ShapeDtypeStruct((B, S, LANES), F32), jax.ShapeDtypeStruct((B, 16, S), F32)],
        compiler_params=_params("parallel"),
        name="even_gates",
    )(zs, par)


def _softmax_step(carry, s, v):
    m, l, acc = carry
    m_new = jnp.maximum(m, jnp.max(s, axis=-1, keepdims=True))
    a = jnp.exp(m - m_new)
    p = jnp.exp(s - m_new)
    l = a * l + jnp.sum(p, axis=-1, keepdims=True)
    acc = a * acc + jnp.dot(p.astype(BF16), v, preferred_element_type=F32)
    return m_new, l, acc


def _fox_kernel(q_ref, k_ref, v_ref, col_ref, row_ref, qg_ref, kg_ref, o_ref, kn_ref, *, S, t):
    h = pl.program_id(1)
    i = pl.program_id(2)

    @pl.when(i == 0)
    def _():
        def norm_keys(n, _):
            sl = pl.ds(pl.multiple_of(n * t, t), t)
            kn_ref[sl, :] = _rms(k_ref[sl, :].astype(F32), kg_ref[...]).astype(BF16)
            return 0

        lax.fori_loop(0, S // t, norm_keys, 0)

    qn = (_rms(q_ref[...].astype(F32), qg_ref[...]) * HEAD_DIM ** -0.5).astype(BF16)
    fq = _pick_lane(col_ref[...], LANE_F + h)

    def scores(j):
        sl = pl.ds(pl.multiple_of(j * t, t), t)
        s = _dot_nt(qn, kn_ref[sl, :])
        return s + (fq - row_ref[:, sl]), v_ref[sl, :]

    def full_block(j, carry):
        s, v = scores(j)
        return _softmax_step(carry, s, v)

    init = (jnp.full((t, 1), NEG, F32), jnp.zeros((t, 1), F32), jnp.zeros((t, HEAD_DIM), F32))
    carry = lax.fori_loop(0, i, full_block, init)
    s, v = scores(i)
    r = lax.broadcasted_iota(jnp.int32, (t, t), 0)
    c = lax.broadcasted_iota(jnp.int32, (t, t), 1)
    _, l, acc = _softmax_step(carry, jnp.where(c <= r, s, NEG), v)
    o_ref[...] = (acc / l).astype(o_ref.dtype)


def fox_attention(z, col, row, q_gain, k_gain, *, t=256):
    B, S, _ = z.shape
    return pl.pallas_call(
        functools.partial(_fox_kernel, S=S, t=t),
        grid=(B, FOX_HEADS, S // t),
        in_specs=[
            pl.BlockSpec((None, t, HEAD_DIM), lambda b, h, i: (b, i, FOX_Q0 + h)),
            pl.BlockSpec((None, S, HEAD_DIM), lambda b, h, i: (b, 0, FOX_K0 + h)),
            pl.BlockSpec((None, S, HEAD_DIM), lambda b, h, i: (b, 0, FOX_V0 + h)),
            pl.BlockSpec((None, t, LANES), lambda b, h, i: (b, i, 0)),
            pl.BlockSpec((None, None, 1, S), lambda b, h, i: (b, LANE_F + h, 0, 0)),
            pl.BlockSpec((1, HEAD_DIM), lambda b, h, i: (0, 0)),
            pl.BlockSpec((1, HEAD_DIM), lambda b, h, i: (0, 0)),
        ],
        out_specs=pl.BlockSpec((None, t, HEAD_DIM), lambda b, h, i: (b, i, h)),
        out_shape=jax.ShapeDtypeStruct((B, S, FOX_HEADS * HEAD_DIM), BF16),
        scratch_shapes=[pltpu.VMEM((S, HEAD_DIM), BF16)],
        compiler_params=_params("parallel", "parallel", "arbitrary"),
        name="fox_attention",
    )(z, z, z, col, row, q_gain.reshape(1, -1), k_gain.reshape(1, -1))


def _unit_lower_inverse(m):
    n = m.shape[0]
    r = lax.broadcasted_iota(jnp.int32, (n, n), 0)
    c = lax.broadcasted_iota(jnp.int32, (n, n), 1)
    eye = (r == c).astype(F32)

    def same_block(b):
        return (r // b) == (c // b)

    p = jnp.where(same_block(INV_BLOCK), m, 0.0)
    inv = eye - p
    k = 2
    while k < INV_BLOCK:
        pb = p.astype(BF16)
        p = jnp.dot(pb, pb, preferred_element_type=F32)
        inv = jnp.dot(inv.astype(BF16), (eye + p).astype(BF16), preferred_element_type=F32)
        k *= 2
    b = INV_BLOCK
    while b < n:
        off = jnp.where(same_block(2 * b), jnp.where(same_block(b), 0.0, m), 0.0).astype(BF16)
        ib = inv.astype(BF16)
        left = jnp.dot(ib, off, preferred_element_type=F32).astype(BF16)
        inv = inv - jnp.dot(left, ib, preferred_element_type=F32)
        b *= 2
    return inv


def _gdn_kernel(q_ref, k_ref, v_ref, gg_ref, wq_ref, wk_ref, wv_ref, col_ref, row_ref, on_ref,
                o_ref, pad_ref, qs_ref, ks_ref, vs_ref, *, S, rows):
    h = pl.program_id(1)
    C = GDN_CHUNK
    D = HEAD_DIM

    def conv_silu(x_ref, w_ref, dst_ref, mode):
        pad_ref[0:8, :] = jnp.zeros((8, D), F32)
        for n in range(S // rows):
            pad_ref[8 + n * rows:8 + (n + 1) * rows, :] = x_ref[n * rows:(n + 1) * rows, :].astype(F32)
        for n in range(S // rows):
            y = jnp.zeros((rows, D), F32)
            for tap in range(CONV_WIDTH):
                start = 8 + n * rows - (CONV_WIDTH - 1) + tap
                y = y + w_ref[tap:tap + 1, :] * pad_ref[start:start + rows, :]
            y = y / (1.0 + jnp.exp(-y))
            if mode != "v":
                y = y * lax.rsqrt(jnp.sum(y * y, axis=-1, keepdims=True) + EPS)
            if mode == "q":
                y = y * D ** -0.5
            dst_ref[n * rows:(n + 1) * rows, :] = y

    conv_silu(q_ref, wq_ref, qs_ref, "q")
    conv_silu(k_ref, wk_ref, ks_ref, "k")
    conv_silu(v_ref, wv_ref, vs_ref, "v")

    r = lax.broadcasted_iota(jnp.int32, (C, C), 0)
    c = lax.broadcasted_iota(jnp.int32, (C, C), 1)
    incl = r >= c
    strict = r > c

    def chunk(n, state):
        sl = pl.ds(pl.multiple_of(n * C, C), C)
        q = qs_ref[sl, :]
        k = ks_ref[sl, :]
        v = vs_ref[sl, :]
        tab = col_ref[sl, :]
        gcol = _pick_lane(tab, LANE_A + h)
        beta = _pick_lane(tab, LANE_B + h)
        grow = row_ref[:, sl]
        glast = gcol[C - 1:C, :]
        decay = jnp.where(incl, jnp.exp(jnp.where(incl, gcol - grow, 0.0)), 0.0)
        eg = jnp.exp(gcol)
        kb = k * beta
        kbf = k.astype(BF16)
        m = jnp.where(strict, _dot_nt(kb.astype(BF16), kbf) * decay, 0.0)
        tinv = _unit_lower_inverse(m).astype(BF16)
        rhs = jnp.concatenate([v * beta, kb * eg], axis=-1).astype(BF16)
        sol = jnp.dot(tinv, rhs, preferred_element_type=F32)
        u = sol[:, :D]
        w = sol[:, D:]
        attn = _dot_nt(q.astype(BF16), kbf) * decay
        qg = q * eg
        kg = k * jnp.exp(glast - gcol)
        sb = state.astype(BF16)
        v_new = u - jnp.dot(w.astype(BF16), sb, preferred_element_type=F32)
        vb = v_new.astype(BF16)
        o = (jnp.dot(qg.astype(BF16), sb, preferred_element_type=F32)
             + jnp.dot(attn.astype(BF16), vb, preferred_element_type=F32))
        state = state * jnp.exp(glast) + jnp.dot(kg.T.astype(BF16), vb, preferred_element_type=F32)
        gate = gg_ref[sl, :].astype(F32)
        o_ref[sl, :] = (_rms(o, on_ref[...]) * (gate / (1.0 + jnp.exp(-gate)))).astype(o_ref.dtype)
        return state

    lax.fori_loop(0, S // C, chunk, jnp.zeros((D, D), F32))


def gdn_mixer(z, conv_w, col, row, o_gain):
    B, S, _ = z.shape
    D = HEAD_DIM
    rows = min(S, 512)
    seq = lambda off: pl.BlockSpec((None, S, D), lambda b, h: (b, 0, off + h))
    cw = lambda off: pl.BlockSpec((CONV_WIDTH, D), lambda b, h: (0, off + h))
    return pl.pallas_call(
        functools.partial(_gdn_kernel, S=S, rows=rows),
        grid=(B, GDN_HEADS),
        in_specs=[
            seq(GDN_Q0), seq(GDN_K0), seq(GDN_V0), seq(GDN_G0),
            cw(0), cw(GDN_HEADS), cw(2 * GDN_HEADS),
            pl.BlockSpec((None, S, LANES), lambda b, h: (b, 0, 0)),
            pl.BlockSpec((None, None, 1, S), lambda b, h: (b, LANE_A + h, 0, 0)),
            pl.BlockSpec((1, D), lambda b, h: (0, 0)),
        ],
        out_specs=pl.BlockSpec((None, S, D), lambda b, h: (b, 0, h)),
        out_shape=jax.ShapeDtypeStruct((B, S, GDN_HEADS * D), BF16),
        scratch_shapes=[pltpu.VMEM((S + 8, D), F32)] + [pltpu.VMEM((S, D), F32)] * 3,
        compiler_params=_params("parallel", "parallel"),
        name="gdn_mixer",
    )(z, z, z, z, conv_w, conv_w, conv_w, col, row, o_gain.reshape(1, D))


def _rope(x, cos, sin_signed):
    return x * cos + pltpu.roll(x, HEAD_DIM // 2, 1) * sin_signed


def _moba_kernel(q_ref, k_ref, v_ref, cos_ref, sin_ref, qg_ref, kg_ref, o_ref, kn_ref, kmean_ref, *, S):
    i = pl.program_id(2)
    t = MOBA_BLOCK
    nblk = S // t

    @pl.when(i == 0)
    def _():
        kmean_ref[...] = jnp.zeros(kmean_ref.shape, F32)

        def prep_keys(n, _):
            sl = pl.ds(pl.multiple_of(n * t, t), t)
            k = _rope(_rms(k_ref[sl, :].astype(F32), kg_ref[...]), cos_ref[sl, :], sin_ref[sl, :])
            kn_ref[sl, :] = k.astype(BF16)
            kmean_ref[pl.ds(n, 1), :] = jnp.mean(k, axis=0, keepdims=True)
            return 0

        lax.fori_loop(0, nblk, prep_keys, 0)

    q = _rope(_rms(q_ref[...].astype(F32), qg_ref[...]), cos_ref[pl.ds(pl.multiple_of(i * t, t), t), :],
              sin_ref[pl.ds(pl.multiple_of(i * t, t), t), :])
    lane = lax.broadcasted_iota(jnp.int32, (t, LANES), 1)
    gate = _dot_nt(q, kmean_ref[...], precision=lax.Precision.HIGHEST)
    gate = jnp.where(lane < i, gate, -jnp.inf)
    sel_bias = jnp.full((t, LANES), NEG, F32)
    for _ in range(MOBA_TOPK):
        top = jnp.max(gate, axis=-1, keepdims=True)
        first = jnp.min(jnp.where(gate == top, lane, LANES), axis=-1, keepdims=True)
        pick = lane == first
        sel_bias = jnp.where(pick & (first < i), 0.0, sel_bias)
        gate = jnp.where(pick, -jnp.inf, gate)

    qs = (q * HEAD_DIM ** -0.5).astype(BF16)

    def block(j):
        sl = pl.ds(pl.multiple_of(j * t, t), t)
        return _dot_nt(qs, kn_ref[sl, :]), v_ref[sl, :]

    s, v = block(i)
    r = lax.broadcasted_iota(jnp.int32, (t, t), 0)
    c = lax.broadcasted_iota(jnp.int32, (t, t), 1)
    init = (jnp.full((t, 1), NEG, F32), jnp.zeros((t, 1), F32), jnp.zeros((t, HEAD_DIM), F32))
    carry = _softmax_step(init, jnp.where(c <= r, s, NEG), v)

    def past_block(j, carry):
        s, v = block(j)
        bias = jnp.max(jnp.where(lane == j, sel_bias, NEG), axis=-1, keepdims=True)
        return _softmax_step(carry, s + bias, v)

    _, l, acc = lax.fori_loop(0, i, past_block, carry)
    o_ref[...] = (acc / l).astype(o_ref.dtype)


def moba_attention(z, cos, sin_signed, q_gain, k_gain):
    B, S, W = z.shape
    H = W // (3 * HEAD_DIM)
    t = MOBA_BLOCK
    return pl.pallas_call(
        functools.partial(_moba_kernel, S=S),
        grid=(B, H, S // t),
        in_specs=[
            pl.BlockSpec((None, t, HEAD_DIM), lambda b, h, i: (b, i, h)),
            pl.BlockSpec((None, S, HEAD_DIM), lambda b, h, i: (b, 0, H + h)),
            pl.BlockSpec((None, S, HEAD_DIM), lambda b, h, i: (b, 0, 2 * H + h)),
            pl.BlockSpec((S, HEAD_DIM), lambda b, h, i: (0, 0)),
            pl.BlockSpec((S, HEAD_DIM), lambda b, h, i: (0, 0)),
            pl.BlockSpec((1, HEAD_DIM), lambda b, h, i: (0, 0)),
            pl.BlockSpec((1, HEAD_DIM), lambda b, h, i: (0, 0)),
        ],
        out_specs=pl.BlockSpec((None, t, HEAD_DIM), lambda b, h, i: (b, i, h)),
        out_shape=jax.ShapeDtypeStruct((B, S, H * HEAD_DIM), BF16),
        scratch_shapes=[pltpu.VMEM((S, HEAD_DIM), BF16), pltpu.VMEM((LANES, HEAD_DIM), F32)],
        compiler_params=_params("parallel", "parallel", "arbitrary"),
        name="moba_attention",
    )(z, z, z, cos, sin_signed, q_gain.reshape(1, -1), k_gain.reshape(1, -1))


def _router_kernel(x_ref, g_ref, w_ref, o_ref):
    xn = _rms(x_ref[...], g_ref[...])
    logits = jnp.dot(xn, w_ref[...], preferred_element_type=F32, precision=lax.Precision.HIGHEST)
    lane = lax.broadcasted_iota(jnp.int32, logits.shape, 1)
    logits = jnp.where(lane < N_EXPERTS, logits, -jnp.inf)
    top1 = jnp.max(logits, axis=-1, keepdims=True)
    idx1 = jnp.min(jnp.where(logits == top1, lane, LANES), axis=-1, keepdims=True)
    rest = jnp.where(lane == idx1, -jnp.inf, logits)
    top2 = jnp.max(rest, axis=-1, keepdims=True)
    idx2 = jnp.min(jnp.where(rest == top2, lane, LANES), axis=-1, keepdims=True)
    e2 = jnp.exp(top2 - top1)
    denom = 1.0 + e2
    o_ref[...] = jnp.where(lane == idx1, 1.0 / denom, jnp.where(lane == idx2, e2 / denom, 0.0))


def moe_router(x, gain, w_router, *, tm):
    M, K = x.shape
    w = jnp.zeros((K, LANES), F32).at[:, :N_EXPERTS].set(w_router)
    return pl.pallas_call(
        _router_kernel,
        grid=(M // tm,),
        in_specs=[
            pl.BlockSpec((tm, K), lambda i: (i, 0)),
            pl.BlockSpec((1, K), lambda i: (0, 0)),
            pl.BlockSpec((K, LANES), lambda i: (0, 0)),
        ],
        out_specs=pl.BlockSpec((tm, LANES), lambda i: (i, 0)),
        out_shape=jax.ShapeDtypeStruct((M, LANES), F32),
        compiler_params=_params("parallel"),
        name="moe_router",
    )(x, gain.reshape(1, K), w)


def _ffn_kernel(x_ref, g_ref, wg_ref, wu_ref, wd_ref, *rest, has_combine):
    if has_combine:
        cmb_ref, o_ref, xn_ref, acc_ref = rest
    else:
        o_ref, xn_ref, acc_ref = rest
    e = pl.program_id(1)
    j = pl.program_id(2)

    @pl.when((e == 0) & (j == 0))
    def _():
        x = x_ref[...]
        xn_ref[...] = _rms(x, g_ref[...]).astype(BF16)
        acc_ref[...] = x

    xn = xn_ref[...]
    gt = jnp.dot(xn, wg_ref[...], preferred_element_type=F32)
    up = jnp.dot(xn, wu_ref[...], preferred_element_type=F32)
    act = gt / (1.0 + jnp.exp(-gt)) * up
    if has_combine:
        act = act * _pick_lane(cmb_ref[...], e)
    acc_ref[...] += jnp.dot(act.astype(BF16), wd_ref[...], preferred_element_type=F32)

    @pl.when((e == pl.num_programs(1) - 1) & (j == pl.num_programs(2) - 1))
    def _():
        o_ref[...] = acc_ref[...]


def ffn_residual(x, gain, w_gate_up, w_down, combine=None, *, tm, tf):
    M, K = x.shape
    E, F, _ = w_down.shape
    nf = F // tf
    has_combine = combine is not None
    in_specs = [
        pl.BlockSpec((tm, K), lambda i, e, j: (i, 0)),
        pl.BlockSpec((1, K), lambda i, e, j: (0, 0)),
        pl.BlockSpec((None, K, tf), lambda i, e, j: (e, 0, j)),
        pl.BlockSpec((None, K, tf), lambda i, e, j: (e, 0, nf + j)),
        pl.BlockSpec((None, tf, K), lambda i, e, j: (e, j, 0)),
    ]
    args = [x, gain.reshape(1, K), w_gate_up, w_gate_up, w_down]
    if has_combine:
        in_specs.append(pl.BlockSpec((tm, LANES), lambda i, e, j: (i, 0)))
        args.append(combine)
    return pl.pallas_call(
        functools.partial(_ffn_kernel, has_combine=has_combine),
        grid=(M // tm, E, nf),
        in_specs=in_specs,
        out_specs=pl.BlockSpec((tm, K), lambda i, e, j: (i, 0)),
        out_shape=jax.ShapeDtypeStruct((M, K), F32),
        scratch_shapes=[pltpu.VMEM((tm, K), BF16), pltpu.VMEM((tm, K), F32)],
        compiler_params=_params("parallel", "arbitrary", "arbitrary"),
        name="ffn_residual",
    )(*args)


def _even_mix(h, B, S, norm1, w_in, f_bias, q_norm, k_norm, conv_w, a_log, dt_bias, o_norm, w_out):
    M, D = h.shape
    fw, gw = FOX_HEADS * HEAD_DIM, GDN_HEADS * HEAD_DIM
    o_ff = 3 * fw
    o_gq = o_ff + FOX_HEADS
    o_ga = o_gq + 3 * gw
    o_gb = o_ga + GDN_HEADS
    o_gg = o_gb + GDN_HEADS
    w_big = jnp.concatenate([w_in[:, :o_ff], w_in[:, o_gq:o_ga], w_in[:, o_gg:]], axis=1).astype(BF16)
    w_small = jnp.concatenate([w_in[:, o_ff:o_gq], w_in[:, o_ga:o_gg],
                               jnp.zeros((D, LANES - FOX_HEADS - 2 * GDN_HEADS), F32)], axis=1).astype(BF16)
    z, zs = norm_matmul(h, norm1, w_big, w_small, tm=512, tn=512)
    par = jnp.zeros((8, LANES), F32)
    par = par.at[0, LANE_F:LANE_F + FOX_HEADS].set(f_bias).at[0, LANE_A:LANE_A + GDN_HEADS].set(dt_bias)
    par = par.at[1, LANE_A:LANE_A + GDN_HEADS].set(a_log)
    col, row = even_gates(zs.reshape(B, S, LANES), par)
    row = row.reshape(B, 16, 1, S)
    z = z.reshape(B, S, -1)
    fox = fox_attention(z, col, row, q_norm, k_norm)
    gdn = gdn_mixer(z, conv_w, col, row, o_norm)
    w_out = w_out.astype(BF16)
    return matmul_residual(h, [(fox.reshape(M, fw), w_out[:fw]), (gdn.reshape(M, gw), w_out[fw:])], tm=512, tn=512)


def _odd_mix(h, B, S, norm1, w_qkv, q_norm, k_norm, w_out):
    M, D = h.shape
    z = norm_matmul(h, norm1, w_qkv.astype(BF16), tm=512, tn=512).reshape(B, S, -1)
    half = HEAD_DIM // 2
    inv = jnp.power(ROPE_THETA, -jnp.arange(half, dtype=F32) / half)
    ang = jnp.arange(S, dtype=F32)[:, None] * inv[None, :]
    cos, sin = jnp.cos(ang), jnp.sin(ang)
    cos_full = jnp.concatenate([cos, cos], axis=-1)
    sin_signed = jnp.concatenate([-sin, sin], axis=-1)
    att = moba_attention(z, cos_full, sin_signed, q_norm, k_norm)
    return matmul_residual(h, [(att.reshape(M, -1), w_out.astype(BF16))], tm=512, tn=512)


def _odd_moe(h, norm2, w_router, w_gate_up, w_down):
    combine = moe_router(h, norm2, w_router, tm=512)
    return ffn_residual(h, norm2, w_gate_up.astype(BF16), w_down.astype(BF16), combine, tm=1024, tf=512)


def kernel(x, e_norm1, e_w_in, e_fox_f_bias, e_fox_q_norm, e_fox_k_norm, e_gdn_conv, e_gdn_a_log,
           e_gdn_dt_bias, e_gdn_o_norm, e_w_out, e_norm2, e_ffn_w_gate_up, e_ffn_w_down,
           o_norm1, o_w_qkv, o_q_norm, o_k_norm, o_w_out, o_norm2, o_router, o_exp_w_gate_up, o_exp_w_down):
    B, S, D = x.shape
    h = x.reshape(B * S, D)
    depth = e_norm1.shape[0] + o_norm1.shape[0]
    for layer in range(depth):
        i = layer // 2
        if layer % 2 == 0:
            h = _even_mix(h, B, S, e_norm1[i], e_w_in[i], e_fox_f_bias[i], e_fox_q_norm[i], e_fox_k_norm[i],
                          e_gdn_conv[i], e_gdn_a_log[i], e_gdn_dt_bias[i], e_gdn_o_norm[i], e_w_out[i])
            h = ffn_residual(h, e_norm2[i], e_ffn_w_gate_up[i].astype(BF16)[None],
                             e_ffn_w_down[i].astype(BF16)[None], tm=512, tf=256)
        else:
            h = _odd_mix(h, B, S, o_norm1[i], o_w_qkv[i], o_q_norm[i], o_k_norm[i], o_w_out[i])
            h = _odd_moe(h, o_norm2[i], o_router[i], o_exp_w_gate_up[i], o_exp_w_down[i])
    return h.reshape(B, S, D)
```

```python
import functools

import jax
import jax.numpy as jnp
from jax import lax
from jax.experimental import pallas as pl
from jax.experimental.pallas import tpu as pltpu

F32 = jnp.float32
BF16 = jnp.bfloat16

HEAD_DIM = 128
FOX_HEADS = 4
GDN_HEADS = 4
CONV_WIDTH = 4
MOBA_BLOCK = 256
MOBA_TOPK = 3
N_EXPERTS = 8
ROPE_THETA = 10000.0
EPS = 1e-6

LANES = 128
GDN_CHUNK = 128
INV_BLOCK = 16
NEG = -(2.0 ** 100)
LOG2E = 1.4426950408889634
ATT_BLOCK = 256
V_ROWS = HEAD_DIM + 16
VMEM_LIMIT_BYTES = 56 * 1024 * 1024

FOX_Q0, FOX_K0, FOX_V0 = 0, 4, 8
GDN_Q0, GDN_K0, GDN_V0, GDN_G0 = 12, 16, 20, 24
LANE_F, LANE_A, LANE_B = 0, 4, 8


def _params(*sem):
    return pltpu.CompilerParams(dimension_semantics=sem, vmem_limit_bytes=VMEM_LIMIT_BYTES)


def _rms(x, gain):
    return x * lax.rsqrt(jnp.mean(x * x, axis=-1, keepdims=True) + EPS) * gain


def _dot_nt(a, b, **kw):
    return lax.dot_general(a, b, (((1,), (1,)), ((), ())), preferred_element_type=F32, **kw)


def _pick_lane(x, lane_idx):
    lane = lax.broadcasted_iota(jnp.int32, x.shape, 1)
    return jnp.sum(jnp.where(lane == lane_idx, x, 0.0), axis=-1, keepdims=True)


def _norm_mm_kernel(x_ref, g_ref, w_ref, *rest, has_aux):
    if has_aux:
        waux_ref, o_ref, oaux_ref, xn_ref = rest
    else:
        o_ref, xn_ref = rest

    @pl.when(pl.program_id(1) == 0)
    def _():
        xn = _rms(x_ref[...], g_ref[...]).astype(BF16)
        xn_ref[...] = xn
        if has_aux:
            oaux_ref[...] = jnp.dot(xn, waux_ref[...], preferred_element_type=F32)

    o_ref[...] = jnp.dot(xn_ref[...], w_ref[...], preferred_element_type=F32).astype(o_ref.dtype)


def norm_matmul(x, gain, w, w_aux=None, *, tm, tn, out_dtype=BF16):
    M, K = x.shape
    N = w.shape[1]
    has_aux = w_aux is not None
    in_specs = [
        pl.BlockSpec((tm, K), lambda i, j: (i, 0)),
        pl.BlockSpec((1, K), lambda i, j: (0, 0)),
        pl.BlockSpec((K, tn), lambda i, j: (0, j)),
    ]
    out_shape = [jax.ShapeDtypeStruct((M, N), out_dtype)]
    out_specs = [pl.BlockSpec((tm, tn), lambda i, j: (i, j))]
    args = [x, gain.reshape(1, K), w]
    if has_aux:
        in_specs.append(pl.BlockSpec((K, LANES), lambda i, j: (0, 0)))
        out_shape.append(jax.ShapeDtypeStruct((M, LANES), F32))
        out_specs.append(pl.BlockSpec((tm, LANES), lambda i, j: (i, 0)))
        args.append(w_aux)
    res = pl.pallas_call(
        functools.partial(_norm_mm_kernel, has_aux=has_aux),
        grid=(M // tm, N // tn),
        in_specs=in_specs,
        out_specs=out_specs,
        out_shape=out_shape,
        scratch_shapes=[pltpu.VMEM((tm, K), BF16)],
        compiler_params=_params("parallel", "arbitrary"),
        name="norm_matmul",
    )(*args)
    return res if has_aux else res[0]


def _mm_res_kernel(*refs, n_in):
    res_ref = refs[0]
    o_ref = refs[1 + 2 * n_in]
    acc = res_ref[...]
    for t in range(n_in):
        acc = acc + jnp.dot(refs[1 + 2 * t][...], refs[2 + 2 * t][...], preferred_element_type=F32)
    o_ref[...] = acc


def matmul_residual(res, pairs, *, tm, tn):
    M, N = res.shape
    in_specs = [pl.BlockSpec((tm, tn), lambda i, j: (i, j))]
    args = [res]
    for a, w in pairs:
        K = a.shape[1]
        in_specs.append(pl.BlockSpec((tm, K), lambda i, j: (i, 0)))
        in_specs.append(pl.BlockSpec((K, tn), lambda i, j: (0, j)))
        args += [a, w]
    return pl.pallas_call(
        functools.partial(_mm_res_kernel, n_in=len(pairs)),
        grid=(M // tm, N // tn),
        in_specs=in_specs,
        out_specs=pl.BlockSpec((tm, tn), lambda i, j: (i, j)),
        out_shape=jax.ShapeDtypeStruct((M, N), F32),
        compiler_params=_params("parallel", "arbitrary"),
        name="matmul_residual",
    )(*args)


def _gate_kernel(zs_ref, par_ref, col_ref, row_ref, *, S):
    C = GDN_CHUNK
    bias = par_ref[0:1, :]
    neg_a = -jnp.exp(par_ref[1:2, :])
    r = lax.broadcasted_iota(jnp.int32, (C, C), 0)
    c = lax.broadcasted_iota(jnp.int32, (C, C), 1)
    tril = (r >= c).astype(F32)
    lane = lax.broadcasted_iota(jnp.int32, (C, LANES), 1)

    def body(n, carry):
        sl = pl.ds(pl.multiple_of(n * C, C), C)
        z = zs_ref[sl, :]
        t = z + bias
        soft = jnp.log(1.0 + jnp.exp(-jnp.abs(t)))
        log_f = jnp.minimum(t, 0.0) - soft
        g = neg_a * (jnp.maximum(t, 0.0) + soft)
        beta = 1.0 / (1.0 + jnp.exp(-z))
        u = jnp.where(lane < LANE_A, log_f, jnp.where(lane < LANE_B, g, 0.0))
        cs = jnp.dot(tril, u, preferred_element_type=F32, precision=lax.Precision.HIGHEST)
        cs = cs + jnp.where(lane < LANE_A, carry, 0.0)
        out = jnp.where(lane < LANE_B, cs, beta)
        col_ref[sl, :] = out
        row_ref[:, sl] = out.T[0:16, :]
        return cs[C - 1:C, :]

    lax.fori_loop(0, S // C, body, jnp.zeros((1, LANES), F32))


def even_gates(zs, par):
    B, S, _ = zs.shape
    return pl.pallas_call(
        functools.partial(_gate_kernel, S=S),
        grid=(B,),
        in_specs=[
            pl.BlockSpec((None, S, LANES), lambda b: (b, 0, 0)),
            pl.BlockSpec((8, LANES), lambda b: (0, 0)),
        ],
        out_specs=[
            pl.BlockSpec((None, S, LANES), lambda b: (b, 0, 0)),
            pl.BlockSpec((None, 16, S), lambda b: (b, 0, 0)),
        ],
        out_shape=[jax.ShapeDtypeStruct((B, S, LANES), F32), jax.ShapeDtypeStruct((B, 16, S), F32)],
        compiler_params=_params("parallel"),
        name="even_gates",
    )(zs, par)


def _split3(x):
    hi = x.astype(BF16).astype(F32)
    mid = (x - hi).astype(BF16).astype(F32)
    lo = (x - hi - mid).astype(BF16).astype(F32)
    return hi, mid, lo


def _attend(qa, ka_ref, vt_ref, m_ref, acc_ref, key0, nkeys, keep=None, first=False):
    ks = pl.ds(pl.multiple_of(key0, ATT_BLOCK), nkeys)
    st = _dot_nt(ka_ref[ks, :], qa)
    if keep is not None:
        st = jnp.where(keep, st, NEG)
    m_new = jnp.max(st, axis=0, keepdims=True)
    if not first:
        m_old = m_ref[...]
        m_new = jnp.maximum(m_old, m_new)
    p = jnp.exp2(st - m_new).astype(BF16)
    pv = jnp.dot(vt_ref[:, ks], p, preferred_element_type=F32)
    if first:
        acc_ref[...] = pv
    else:
        acc_ref[...] = acc_ref[...] * jnp.exp2(m_old - m_new) + pv
    m_ref[...] = m_new


def _attend_past(n_blocks, attend):
    def group4(g, _):
        attend(g * (4 * ATT_BLOCK), 4 * ATT_BLOCK)
        return 0

    lax.fori_loop(0, n_blocks // 4, group4, 0)

    @pl.when((n_blocks & 2) != 0)
    def _():
        attend((n_blocks // 4) * (4 * ATT_BLOCK), 2 * ATT_BLOCK)

    @pl.when((n_blocks & 1) != 0)
    def _():
        attend((n_blocks // 2) * (2 * ATT_BLOCK), ATT_BLOCK)


def _attend_finish(acc_ref, o_ref):
    acc = acc_ref[...]
    out_t = acc[:HEAD_DIM, :] * (1.0 / acc[HEAD_DIM:HEAD_DIM + 1, :])
    o_ref[...] = out_t.T.astype(o_ref.dtype)


def _store_vt(vt_ref, v_ref, sl):
    vt_ref[0:HEAD_DIM, sl] = v_ref[sl, :].astype(F32).T.astype(BF16)
    vt_ref[HEAD_DIM:V_ROWS, sl] = jnp.ones((V_ROWS - HEAD_DIM, ATT_BLOCK), BF16)


def _causal_keep():
    key = lax.broadcasted_iota(jnp.int32, (ATT_BLOCK, ATT_BLOCK), 0)
    qry = lax.broadcasted_iota(jnp.int32, (ATT_BLOCK, ATT_BLOCK), 1)
    return key <= qry


def _fox_kernel(q_ref, k_ref, v_ref, col_ref, qg_ref, kg_ref, o_ref, ka_ref, vt_ref, m_ref, acc_ref, *, S):
    h = pl.program_id(1)
    i = pl.program_id(2)
    t = ATT_BLOCK
    lane = lax.broadcasted_iota(jnp.int32, (t, LANES), 1)

    @pl.when(i == 0)
    def _():
        def prep_keys(n, _):
            sl = pl.ds(pl.multiple_of(n * t, t), t)
            ka_ref[sl, 0:HEAD_DIM] = _rms(k_ref[sl, :].astype(F32), kg_ref[...]).astype(BF16)
            hi, mid, lo = _split3(-LOG2E * _pick_lane(col_ref[sl, :], LANE_F + h))
            ka_ref[sl, HEAD_DIM:] = jnp.where(
                lane < 3, 1.0, jnp.where(lane == 3, hi, jnp.where(lane == 4, mid, jnp.where(lane == 5, lo, 0.0)))
            ).astype(BF16)
            _store_vt(vt_ref, v_ref, sl)
            return 0

        lax.fori_loop(0, S // t, prep_keys, 0)

    qsl = pl.ds(pl.multiple_of(i * t, t), t)
    qn = (_rms(q_ref[...].astype(F32), qg_ref[...]) * (LOG2E * HEAD_DIM ** -0.5)).astype(BF16)
    hi, mid, lo = _split3(LOG2E * _pick_lane(col_ref[qsl, :], LANE_F + h))
    aug = jnp.where(lane == 0, hi, jnp.where(lane == 1, mid, jnp.where(lane == 2, lo, jnp.where(lane < 6, 1.0, 0.0))))
    qa = jnp.concatenate([qn, aug.astype(BF16)], axis=-1)
    attend = functools.partial(_attend, qa, ka_ref, vt_ref, m_ref, acc_ref)
    attend(i * t, t, keep=_causal_keep(), first=True)
    _attend_past(i, attend)
    _attend_finish(acc_ref, o_ref)


def fox_attention(z, col, q_gain, k_gain):
    B, S, _ = z.shape
    t = ATT_BLOCK
    return pl.pallas_call(
        functools.partial(_fox_kernel, S=S),
        grid=(B, FOX_HEADS, S // t),
        in_specs=[
            pl.BlockSpec((None, t, HEAD_DIM), lambda b, h, i: (b, i, FOX_Q0 + h)),
            pl.BlockSpec((None, S, HEAD_DIM), lambda b, h, i: (b, 0, FOX_K0 + h)),
            pl.BlockSpec((None, S, HEAD_DIM), lambda b, h, i: (b, 0, FOX_V0 + h)),
            pl.BlockSpec((None, S, LANES), lambda b, h, i: (b, 0, 0)),
            pl.BlockSpec((1, HEAD_DIM), lambda b, h, i: (0, 0)),
            pl.BlockSpec((1, HEAD_DIM), lambda b, h, i: (0, 0)),
        ],
        out_specs=pl.BlockSpec((None, t, HEAD_DIM), lambda b, h, i: (b, i, h)),
        out_shape=jax.ShapeDtypeStruct((B, S, FOX_HEADS * HEAD_DIM), BF16),
        scratch_shapes=[pltpu.VMEM((S, 2 * HEAD_DIM), BF16), pltpu.VMEM((V_ROWS, S), BF16),
                        pltpu.VMEM((1, t), F32), pltpu.VMEM((V_ROWS, t), F32)],
        compiler_params=_params("parallel", "parallel", "arbitrary"),
        name="fox_attention",
    )(z, z, z, col, q_gain.reshape(1, -1), k_gain.reshape(1, -1))


def _unit_lower_inverse(m):
    n = m.shape[0]
    r = lax.broadcasted_iota(jnp.int32, (n, n), 0)
    c = lax.broadcasted_iota(jnp.int32, (n, n), 1)
    eye = (r == c).astype(F32)

    def same_block(b):
        return (r // b) == (c // b)

    p = jnp.where(same_block(INV_BLOCK), m, 0.0)
    inv = eye - p
    k = 2
    while k < INV_BLOCK:
        pb = p.astype(BF16)
        p = jnp.dot(pb, pb, preferred_element_type=F32)
        inv = jnp.dot(inv.astype(BF16), (eye + p).astype(BF16), preferred_element_type=F32)
        k *= 2
    b = INV_BLOCK
    while b < n:
        off = jnp.where(same_block(2 * b), jnp.where(same_block(b), 0.0, m), 0.0).astype(BF16)
        ib = inv.astype(BF16)
        left = jnp.dot(ib, off, preferred_element_type=F32).astype(BF16)
        inv = inv - jnp.dot(left, ib, preferred_element_type=F32)
        b *= 2
    return inv


def _gdn_kernel(q_ref, k_ref, v_ref, gg_ref, wq_ref, wk_ref, wv_ref, col_ref, row_ref, on_ref,
                o_ref, pad_ref, qs_ref, ks_ref, vs_ref, *, S, rows):
    h = pl.program_id(1)
    C = GDN_CHUNK
    D = HEAD_DIM

    def conv_silu(x_ref, w_ref, dst_ref, mode):
        pad_ref[0:8, :] = jnp.zeros((8, D), F32)
        for n in range(S // rows):
            pad_ref[8 + n * rows:8 + (n + 1) * rows, :] = x_ref[n * rows:(n + 1) * rows, :].astype(F32)
        for n in range(S // rows):
            y = jnp.zeros((rows, D), F32)
            for tap in range(CONV_WIDTH):
                start = 8 + n * rows - (CONV_WIDTH - 1) + tap
                y = y + w_ref[tap:tap + 1, :] * pad_ref[start:start + rows, :]
            y = y / (1.0 + jnp.exp(-y))
            if mode != "v":
                y = y * lax.rsqrt(jnp.sum(y * y, axis=-1, keepdims=True) + EPS)
            if mode == "q":
                y = y * D ** -0.5
            dst_ref[n * rows:(n + 1) * rows, :] = y

    conv_silu(q_ref, wq_ref, qs_ref, "q")
    conv_silu(k_ref, wk_ref, ks_ref, "k")
    conv_silu(v_ref, wv_ref, vs_ref, "v")

    r = lax.broadcasted_iota(jnp.int32, (C, C), 0)
    c = lax.broadcasted_iota(jnp.int32, (C, C), 1)
    incl = r >= c
    strict = r > c

    def chunk(n, state):
        sl = pl.ds(pl.multiple_of(n * C, C), C)
        q = qs_ref[sl, :]
        k = ks_ref[sl, :]
        v = vs_ref[sl, :]
        tab = col_ref[sl, :]
        gcol = _pick_lane(tab, LANE_A + h)
        beta = _pick_lane(tab, LANE_B + h)
        grow = row_ref[:, sl]
        glast = gcol[C - 1:C, :]
        decay = jnp.where(incl, jnp.exp(jnp.where(incl, gcol - grow, 0.0)), 0.0)
        eg = jnp.exp(gcol)
        kb = k * beta
        kbf = k.astype(BF16)
        m = jnp.where(strict, _dot_nt(kb.astype(BF16), kbf) * decay, 0.0)
        tinv = _unit_lower_inverse(m).astype(BF16)
        rhs = jnp.concatenate([v * beta, kb * eg], axis=-1).astype(BF16)
        sol = jnp.dot(tinv, rhs, preferred_element_type=F32)
        u = sol[:, :D]
        w = sol[:, D:]
        attn = _dot_nt(q.astype(BF16), kbf) * decay
        qg = q * eg
        kg = k * jnp.exp(glast - gcol)
        sb = state.astype(BF16)
        v_new = u - jnp.dot(w.astype(BF16), sb, preferred_element_type=F32)
        vb = v_new.astype(BF16)
        o = (jnp.dot(qg.astype(BF16), sb, preferred_element_type=F32)
             + jnp.dot(attn.astype(BF16), vb, preferred_element_type=F32))
        state = state * jnp.exp(glast) + jnp.dot(kg.T.astype(BF16), vb, preferred_element_type=F32)
        gate = gg_ref[sl, :].astype(F32)
        o_ref[sl, :] = (_rms(o, on_ref[...]) * (gate / (1.0 + jnp.exp(-gate)))).astype(o_ref.dtype)
        return state

    lax.fori_loop(0, S // C, chunk, jnp.zeros((D, D), F32))


def gdn_mixer(z, conv_w, col, row, o_gain):
    B, S, _ = z.shape
    D = HEAD_DIM
    rows = min(S, 512)
    seq = lambda off: pl.BlockSpec((None, S, D), lambda b, h: (b, 0, off + h))
    cw = lambda off: pl.BlockSpec((CONV_WIDTH, D), lambda b, h: (0, off + h))
    return pl.pallas_call(
        functools.partial(_gdn_kernel, S=S, rows=rows),
        grid=(B, GDN_HEADS),
        in_specs=[
            seq(GDN_Q0), seq(GDN_K0), seq(GDN_V0), seq(GDN_G0),
            cw(0), cw(GDN_HEADS), cw(2 * GDN_HEADS),
            pl.BlockSpec((None, S, LANES), lambda b, h: (b, 0, 0)),
            pl.BlockSpec((None, None, 1, S), lambda b, h: (b, LANE_A + h, 0, 0)),
            pl.BlockSpec((1, D), lambda b, h: (0, 0)),
        ],
        out_specs=pl.BlockSpec((None, S, D), lambda b, h: (b, 0, h)),
        out_shape=jax.ShapeDtypeStruct((B, S, GDN_HEADS * D), BF16),
        scratch_shapes=[pltpu.VMEM((S + 8, D), F32)] + [pltpu.VMEM((S, D), F32)] * 3,
        compiler_params=_params("parallel", "parallel"),
        name="gdn_mixer",
    )(z, z, z, z, conv_w, conv_w, conv_w, col, row, o_gain.reshape(1, D))


def _rope(x, cos, sin_signed):
    return x * cos + pltpu.roll(x, HEAD_DIM // 2, 1) * sin_signed


def _moba_kernel(q_ref, k_ref, v_ref, cos_ref, sin_ref, qg_ref, kg_ref, o_ref,
                 ka_ref, vt_ref, kmean_ref, m_ref, acc_ref, *, S):
    i = pl.program_id(2)
    t = ATT_BLOCK
    lane = lax.broadcasted_iota(jnp.int32, (t, LANES), 1)

    @pl.when(i == 0)
    def _():
        kmean_ref[...] = jnp.zeros(kmean_ref.shape, F32)

        def prep_keys(n, _):
            sl = pl.ds(pl.multiple_of(n * t, t), t)
            k = _rope(_rms(k_ref[sl, :].astype(F32), kg_ref[...]), cos_ref[sl, :], sin_ref[sl, :])
            ka_ref[sl, 0:HEAD_DIM] = k.astype(BF16)
            ka_ref[sl, HEAD_DIM:] = jnp.where(lane == n, 1.0, 0.0).astype(BF16)
            kmean_ref[pl.ds(n, 1), :] = jnp.mean(k, axis=0, keepdims=True)
            _store_vt(vt_ref, v_ref, sl)
            return 0

        lax.fori_loop(0, S // t, prep_keys, 0)

    qsl = pl.ds(pl.multiple_of(i * t, t), t)
    q = _rope(_rms(q_ref[...].astype(F32), qg_ref[...]), cos_ref[qsl, :], sin_ref[qsl, :])
    gate = _dot_nt(q, kmean_ref[...], precision=lax.Precision.HIGHEST)
    gate = jnp.where(lane < i, gate, -jnp.inf)
    sel_bias = jnp.full((t, LANES), NEG, F32)
    for _ in range(MOBA_TOPK):
        top = jnp.max(gate, axis=-1, keepdims=True)
        first = jnp.min(jnp.where(gate == top, lane, LANES), axis=-1, keepdims=True)
        pick = lane == first
        sel_bias = jnp.where(pick & (first < i), 0.0, sel_bias)
        gate = jnp.where(pick, -jnp.inf, gate)

    qs = (q * (LOG2E * HEAD_DIM ** -0.5)).astype(BF16)
    attend = functools.partial(_attend, jnp.concatenate([qs, jnp.zeros((t, LANES), BF16)], axis=-1),
                               ka_ref, vt_ref, m_ref, acc_ref)
    attend(i * t, t, keep=_causal_keep(), first=True)
    attend = functools.partial(_attend, jnp.concatenate([qs, sel_bias.astype(BF16)], axis=-1),
                               ka_ref, vt_ref, m_ref, acc_ref)
    _attend_past(i, attend)
    _attend_finish(acc_ref, o_ref)


def moba_attention(z, cos, sin_signed, q_gain, k_gain):
    B, S, W = z.shape
    H = W // (3 * HEAD_DIM)
    t = ATT_BLOCK
    assert t == MOBA_BLOCK
    return pl.pallas_call(
        functools.partial(_moba_kernel, S=S),
        grid=(B, H, S // t),
        in_specs=[
            pl.BlockSpec((None, t, HEAD_DIM), lambda b, h, i: (b, i, h)),
            pl.BlockSpec((None, S, HEAD_DIM), lambda b, h, i: (b, 0, H + h)),
            pl.BlockSpec((None, S, HEAD_DIM), lambda b, h, i: (b, 0, 2 * H + h)),
            pl.BlockSpec((S, HEAD_DIM), lambda b, h, i: (0, 0)),
            pl.BlockSpec((S, HEAD_DIM), lambda b, h, i: (0, 0)),
            pl.BlockSpec((1, HEAD_DIM), lambda b, h, i: (0, 0)),
            pl.BlockSpec((1, HEAD_DIM), lambda b, h, i: (0, 0)),
        ],
        out_specs=pl.BlockSpec((None, t, HEAD_DIM), lambda b, h, i: (b, i, h)),
        out_shape=jax.ShapeDtypeStruct((B, S, H * HEAD_DIM), BF16),
        scratch_shapes=[pltpu.VMEM((S, 2 * HEAD_DIM), BF16), pltpu.VMEM((V_ROWS, S), BF16),
                        pltpu.VMEM((LANES, HEAD_DIM), F32), pltpu.VMEM((1, t), F32), pltpu.VMEM((V_ROWS, t), F32)],
        compiler_params=_params("parallel", "parallel", "arbitrary"),
        name="moba_attention",
    )(z, z, z, cos, sin_signed, q_gain.reshape(1, -1), k_gain.reshape(1, -1))


def _router_kernel(x_ref, g_ref, w_ref, o_ref):
    xn = _rms(x_ref[...], g_ref[...])
    logits = jnp.dot(xn, w_ref[...], preferred_element_type=F32, precision=lax.Precision.HIGHEST)
    lane = lax.broadcasted_iota(jnp.int32, logits.shape, 1)
    logits = jnp.where(lane < N_EXPERTS, logits, -jnp.inf)
    top1 = jnp.max(logits, axis=-1, keepdims=True)
    idx1 = jnp.min(jnp.where(logits == top1, lane, LANES), axis=-1, keepdims=True)
    rest = jnp.where(lane == idx1, -jnp.inf, logits)
    top2 = jnp.max(rest, axis=-1, keepdims=True)
    idx2 = jnp.min(jnp.where(rest == top2, lane, LANES), axis=-1, keepdims=True)
    e2 = jnp.exp(top2 - top1)
    denom = 1.0 + e2
    o_ref[...] = jnp.where(lane == idx1, 1.0 / denom, jnp.where(lane == idx2, e2 / denom, 0.0))


def moe_router(x, gain, w_router, *, tm):
    M, K = x.shape
    w = jnp.zeros((K, LANES), F32).at[:, :N_EXPERTS].set(w_router)
    return pl.pallas_call(
        _router_kernel,
        grid=(M // tm,),
        in_specs=[
            pl.BlockSpec((tm, K), lambda i: (i, 0)),
            pl.BlockSpec((1, K), lambda i: (0, 0)),
            pl.BlockSpec((K, LANES), lambda i: (0, 0)),
        ],
        out_specs=pl.BlockSpec((tm, LANES), lambda i: (i, 0)),
        out_shape=jax.ShapeDtypeStruct((M, LANES), F32),
        compiler_params=_params("parallel"),
        name="moe_router",
    )(x, gain.reshape(1, K), w)


def _ffn_kernel(x_ref, g_ref, wg_ref, wu_ref, wd_ref, *rest, has_combine):
    if has_combine:
        cmb_ref, o_ref, xn_ref, acc_ref = rest
    else:
        o_ref, xn_ref, acc_ref = rest
    e = pl.program_id(1)
    j = pl.program_id(2)

    @pl.when((e == 0) & (j == 0))
    def _():
        x = x_ref[...]
        xn_ref[...] = _rms(x, g_ref[...]).astype(BF16)
        acc_ref[...] = x

    xn = xn_ref[...]
    gt = jnp.dot(xn, wg_ref[...], preferred_element_type=F32)
    up = jnp.dot(xn, wu_ref[...], preferred_element_type=F32)
    act = gt / (1.0 + jnp.exp(-gt)) * up
    if has_combine:
        act = act * _pick_lane(cmb_ref[...], e)
    acc_ref[...] += jnp.dot(act.astype(BF16), wd_ref[...], preferred_element_type=F32)

    @pl.when((e == pl.num_programs(1) - 1) & (j == pl.num_programs(2) - 1))
    def _():
        o_ref[...] = acc_ref[...]


def ffn_residual(x, gain, w_gate_up, w_down, combine=None, *, tm, tf):
    M, K = x.shape
    E, F, _ = w_down.shape
    nf = F // tf
    has_combine = combine is not None
    in_specs = [
        pl.BlockSpec((tm, K), lambda i, e, j: (i, 0)),
        pl.BlockSpec((1, K), lambda i, e, j: (0, 0)),
        pl.BlockSpec((None, K, tf), lambda i, e, j: (e, 0, j)),
        pl.BlockSpec((None, K, tf), lambda i, e, j: (e, 0, nf + j)),
        pl.BlockSpec((None, tf, K), lambda i, e, j: (e, j, 0)),
    ]
    args = [x, gain.reshape(1, K), w_gate_up, w_gate_up, w_down]
    if has_combine:
        in_specs.append(pl.BlockSpec((tm, LANES), lambda i, e, j: (i, 0)))
        args.append(combine)
    return pl.pallas_call(
        functools.partial(_ffn_kernel, has_combine=has_combine),
        grid=(M // tm, E, nf),
        in_specs=in_specs,
        out_specs=pl.BlockSpec((tm, K), lambda i, e, j: (i, 0)),
        out_shape=jax.ShapeDtypeStruct((M, K), F32),
        scratch_shapes=[pltpu.VMEM((tm, K), BF16), pltpu.VMEM((tm, K), F32)],
        compiler_params=_params("parallel", "arbitrary", "arbitrary"),
        name="ffn_residual",
    )(*args)


def _even_mix(h, B, S, norm1, w_in, f_bias, q_norm, k_norm, conv_w, a_log, dt_bias, o_norm, w_out):
    M, D = h.shape
    fw, gw = FOX_HEADS * HEAD_DIM, GDN_HEADS * HEAD_DIM
    o_ff = 3 * fw
    o_gq = o_ff + FOX_HEADS
    o_ga = o_gq + 3 * gw
    o_gb = o_ga + GDN_HEADS
    o_gg = o_gb + GDN_HEADS
    w_big = jnp.concatenate([w_in[:, :o_ff], w_in[:, o_gq:o_ga], w_in[:, o_gg:]], axis=1).astype(BF16)
    w_small = jnp.concatenate([w_in[:, o_ff:o_gq], w_in[:, o_ga:o_gg],
                               jnp.zeros((D, LANES - FOX_HEADS - 2 * GDN_HEADS), F32)], axis=1).astype(BF16)
    z, zs = norm_matmul(h, norm1, w_big, w_small, tm=512, tn=512)
    par = jnp.zeros((8, LANES), F32)
    par = par.at[0, LANE_F:LANE_F + FOX_HEADS].set(f_bias).at[0, LANE_A:LANE_A + GDN_HEADS].set(dt_bias)
    par = par.at[1, LANE_A:LANE_A + GDN_HEADS].set(a_log)
    col, row = even_gates(zs.reshape(B, S, LANES), par)
    row = row.reshape(B, 16, 1, S)
    z = z.reshape(B, S, -1)
    fox = fox_attention(z, col, q_norm, k_norm)
    gdn = gdn_mixer(z, conv_w, col, row, o_norm)
    w_out = w_out.astype(BF16)
    return matmul_residual(h, [(fox.reshape(M, fw), w_out[:fw]), (gdn.reshape(M, gw), w_out[fw:])], tm=512, tn=512)


def _odd_mix(h, B, S, norm1, w_qkv, q_norm, k_norm, w_out):
    M, D = h.shape
    z = norm_matmul(h, norm1, w_qkv.astype(BF16), tm=512, tn=512).reshape(B, S, -1)
    half = HEAD_DIM // 2
    inv = jnp.power(ROPE_THETA, -jnp.arange(half, dtype=F32) / half)
    ang = jnp.arange(S, dtype=F32)[:, None] * inv[None, :]
    cos, sin = jnp.cos(ang), jnp.sin(ang)
    cos_full = jnp.concatenate([cos, cos], axis=-1)
    sin_signed = jnp.concatenate([-sin, sin], axis=-1)
    att = moba_attention(z, cos_full, sin_signed, q_norm, k_norm)
    return matmul_residual(h, [(att.reshape(M, -1), w_out.astype(BF16))], tm=512, tn=512)


def _odd_moe(h, norm2, w_router, w_gate_up, w_down):
    combine = moe_router(h, norm2, w_router, tm=512)
    return ffn_residual(h, norm2, w_gate_up.astype(BF16), w_down.astype(BF16), combine, tm=1024, tf=512)


def kernel(x, e_norm1, e_w_in, e_fox_f_bias, e_fox_q_norm, e_fox_k_norm, e_gdn_conv, e_gdn_a_log,
           e_gdn_dt_bias, e_gdn_o_norm, e_w_out, e_norm2, e_ffn_w_gate_up, e_ffn_w_down,
           o_norm1, o_w_qkv, o_q_norm, o_k_norm, o_w_out, o_norm2, o_router, o_exp_w_gate_up, o_exp_w_down):
    B, S, D = x.shape
    h = x.reshape(B * S, D)
    depth = e_norm1.shape[0] + o_norm1.shape[0]
    for layer in range(depth):
        i = layer // 2
        if layer % 2 == 0:
            h = _even_mix(h, B, S, e_norm1[i], e_w_in[i], e_fox_f_bias[i], e_fox_q_norm[i], e_fox_k_norm[i],
                          e_gdn_conv[i], e_gdn_a_log[i], e_gdn_dt_bias[i], e_gdn_o_norm[i], e_w_out[i])
            h = ffn_residual(h, e_norm2[i], e_ffn_w_gate_up[i].astype(BF16)[None],
                             e_ffn_w_down[i].astype(BF16)[None], tm=512, tf=256)
        else:
            h = _odd_mix(h, B, S, o_norm1[i], o_w_qkv[i], o_q_norm[i], o_k_norm[i], o_w_out[i])
            h = _odd_moe(h, o_norm2[i], o_router[i], o_exp_w_gate_up[i], o_exp_w_down[i])
    return h.reshape(B, S, D)
```

```python
import functools

import jax
import jax.numpy as jnp
from jax import lax
from jax.experimental import pallas as pl
from jax.experimental.pallas import tpu as pltpu

F32 = jnp.float32
BF16 = jnp.bfloat16

HEAD_DIM = 128
FOX_HEADS = 4
GDN_HEADS = 4
CONV_WIDTH = 4
MOBA_BLOCK = 256
MOBA_TOPK = 3
N_EXPERTS = 8
ROPE_THETA = 10000.0
EPS = 1e-6

LANES = 128
GDN_CHUNK = 128
INV_BLOCK = 16
NEG = -(2.0 ** 100)
LOG2E = 1.4426950408889634
ATT_BLOCK = 256
V_ROWS = HEAD_DIM + 16
PERM_TILE = 512
EXPERT_TILE = 1024
VMEM_LIMIT_BYTES = 56 * 1024 * 1024

FOX_Q0, FOX_K0, FOX_V0 = 0, 4, 8
GDN_Q0, GDN_K0, GDN_V0, GDN_G0 = 12, 16, 20, 24
LANE_F, LANE_A, LANE_B = 0, 4, 8


def _params(*sem):
    return pltpu.CompilerParams(dimension_semantics=sem, vmem_limit_bytes=VMEM_LIMIT_BYTES)


def _rms(x, gain):
    return x * lax.rsqrt(jnp.mean(x * x, axis=-1, keepdims=True) + EPS) * gain


def _dot_nt(a, b, **kw):
    return lax.dot_general(a, b, (((1,), (1,)), ((), ())), preferred_element_type=F32, **kw)


def _pick_lane(x, lane_idx):
    lane = lax.broadcasted_iota(jnp.int32, x.shape, 1)
    return jnp.sum(jnp.where(lane == lane_idx, x, 0.0), axis=-1, keepdims=True)


def _norm_mm_kernel(x_ref, g_ref, w_ref, *rest, has_aux):
    if has_aux:
        waux_ref, o_ref, oaux_ref, xn_ref = rest
    else:
        o_ref, xn_ref = rest

    @pl.when(pl.program_id(1) == 0)
    def _():
        xn = _rms(x_ref[...], g_ref[...]).astype(BF16)
        xn_ref[...] = xn
        if has_aux:
            oaux_ref[...] = jnp.dot(xn, waux_ref[...], preferred_element_type=F32)

    o_ref[...] = jnp.dot(xn_ref[...], w_ref[...], preferred_element_type=F32).astype(o_ref.dtype)


def norm_matmul(x, gain, w, w_aux=None, *, tm, tn, out_dtype=BF16):
    M, K = x.shape
    N = w.shape[1]
    has_aux = w_aux is not None
    in_specs = [
        pl.BlockSpec((tm, K), lambda i, j: (i, 0)),
        pl.BlockSpec((1, K), lambda i, j: (0, 0)),
        pl.BlockSpec((K, tn), lambda i, j: (0, j)),
    ]
    out_shape = [jax.ShapeDtypeStruct((M, N), out_dtype)]
    out_specs = [pl.BlockSpec((tm, tn), lambda i, j: (i, j))]
    args = [x, gain.reshape(1, K), w]
    if has_aux:
        in_specs.append(pl.BlockSpec((K, LANES), lambda i, j: (0, 0)))
        out_shape.append(jax.ShapeDtypeStruct((M, LANES), F32))
        out_specs.append(pl.BlockSpec((tm, LANES), lambda i, j: (i, 0)))
        args.append(w_aux)
    res = pl.pallas_call(
        functools.partial(_norm_mm_kernel, has_aux=has_aux),
        grid=(M // tm, N // tn),
        in_specs=in_specs,
        out_specs=out_specs,
        out_shape=out_shape,
        scratch_shapes=[pltpu.VMEM((tm, K), BF16)],
        compiler_params=_params("parallel", "arbitrary"),
        name="norm_matmul",
    )(*args)
    return res if has_aux else res[0]


def _mm_res_kernel(*refs, n_in):
    res_ref = refs[0]
    o_ref = refs[1 + 2 * n_in]
    acc = res_ref[...]
    for t in range(n_in):
        acc = acc + jnp.dot(refs[1 + 2 * t][...], refs[2 + 2 * t][...], preferred_element_type=F32)
    o_ref[...] = acc


def matmul_residual(res, pairs, *, tm, tn):
    M, N = res.shape
    in_specs = [pl.BlockSpec((tm, tn), lambda i, j: (i, j))]
    args = [res]
    for a, w in pairs:
        K = a.shape[1]
        in_specs.append(pl.BlockSpec((tm, K), lambda i, j: (i, 0)))
        in_specs.append(pl.BlockSpec((K, tn), lambda i, j: (0, j)))
        args += [a, w]
    return pl.pallas_call(
        functools.partial(_mm_res_kernel, n_in=len(pairs)),
        grid=(M // tm, N // tn),
        in_specs=in_specs,
        out_specs=pl.BlockSpec((tm, tn), lambda i, j: (i, j)),
        out_shape=jax.ShapeDtypeStruct((M, N), F32),
        compiler_params=_params("parallel", "arbitrary"),
        name="matmul_residual",
    )(*args)


def _gate_kernel(zs_ref, par_ref, col_ref, row_ref, *, S):
    C = GDN_CHUNK
    bias = par_ref[0:1, :]
    neg_a = -jnp.exp(par_ref[1:2, :])
    r = lax.broadcasted_iota(jnp.int32, (C, C), 0)
    c = lax.broadcasted_iota(jnp.int32, (C, C), 1)
    tril = (r >= c).astype(F32)
    lane = lax.broadcasted_iota(jnp.int32, (C, LANES), 1)

    def body(n, carry):
        sl = pl.ds(pl.multiple_of(n * C, C), C)
        z = zs_ref[sl, :]
        t = z + bias
        soft = jnp.log(1.0 + jnp.exp(-jnp.abs(t)))
        log_f = jnp.minimum(t, 0.0) - soft
        g = neg_a * (jnp.maximum(t, 0.0) + soft)
        beta = 1.0 / (1.0 + jnp.exp(-z))
        u = jnp.where(lane < LANE_A, log_f, jnp.where(lane < LANE_B, g, 0.0))
        cs = jnp.dot(tril, u, preferred_element_type=F32, precision=lax.Precision.HIGHEST)
        cs = cs + jnp.where(lane < LANE_A, carry, 0.0)
        out = jnp.where(lane < LANE_B, cs, beta)
        col_ref[sl, :] = out
        row_ref[:, sl] = out.T[0:16, :]
        return cs[C - 1:C, :]

    lax.fori_loop(0, S // C, body, jnp.zeros((1, LANES), F32))


def even_gates(zs, par):
    B, S, _ = zs.shape
    return pl.pallas_call(
        functools.partial(_gate_kernel, S=S),
        grid=(B,),
        in_specs=[
            pl.BlockSpec((None, S, LANES), lambda b: (b, 0, 0)),
            pl.BlockSpec((8, LANES), lambda b: (0, 0)),
        ],
        out_specs=[
            pl.BlockSpec((None, S, LANES), lambda b: (b, 0, 0)),
            pl.BlockSpec((None, 16, S), lambda b: (b, 0, 0)),
        ],
        out_shape=[jax.ShapeDtypeStruct((B, S, LANES), F32), jax.ShapeDtypeStruct((B, 16, S), F32)],
        compiler_params=_params("parallel"),
        name="even_gates",
    )(zs, par)


def _split3(x):
    hi = x.astype(BF16).astype(F32)
    mid = (x - hi).astype(BF16).astype(F32)
    lo = (x - hi - mid).astype(BF16).astype(F32)
    return hi, mid, lo


def _attend(qa, ka_ref, vt_ref, m_ref, acc_ref, key0, nkeys, keep=None, first=False):
    ks = pl.ds(pl.multiple_of(key0, ATT_BLOCK), nkeys)
    st = _dot_nt(ka_ref[ks, :], qa)
    if keep is not None:
        st = jnp.where(keep, st, NEG)
    m_new = jnp.max(st, axis=0, keepdims=True)
    if not first:
        m_old = m_ref[...]
        m_new = jnp.maximum(m_old, m_new)
    p = jnp.exp2(st - m_new).astype(BF16)
    pv = jnp.dot(vt_ref[:, ks], p, preferred_element_type=F32)
    if first:
        acc_ref[...] = pv
    else:
        acc_ref[...] = acc_ref[...] * jnp.exp2(m_old - m_new) + pv
    m_ref[...] = m_new


def _attend_past(n_blocks, attend):
    def group4(g, _):
        attend(g * (4 * ATT_BLOCK), 4 * ATT_BLOCK)
        return 0

    lax.fori_loop(0, n_blocks // 4, group4, 0)

    @pl.when((n_blocks & 2) != 0)
    def _():
        attend((n_blocks // 4) * (4 * ATT_BLOCK), 2 * ATT_BLOCK)

    @pl.when((n_blocks & 1) != 0)
    def _():
        attend((n_blocks // 2) * (2 * ATT_BLOCK), ATT_BLOCK)


def _attend_finish(acc_ref, o_ref):
    acc = acc_ref[...]
    out_t = acc[:HEAD_DIM, :] * (1.0 / acc[HEAD_DIM:HEAD_DIM + 1, :])
    o_ref[...] = out_t.T.astype(o_ref.dtype)


def _store_vt(vt_ref, v_ref, sl):
    vt_ref[0:HEAD_DIM, sl] = v_ref[sl, :].astype(F32).T.astype(BF16)
    vt_ref[HEAD_DIM:V_ROWS, sl] = jnp.ones((V_ROWS - HEAD_DIM, ATT_BLOCK), BF16)


def _causal_keep():
    key = lax.broadcasted_iota(jnp.int32, (ATT_BLOCK, ATT_BLOCK), 0)
    qry = lax.broadcasted_iota(jnp.int32, (ATT_BLOCK, ATT_BLOCK), 1)
    return key <= qry


def _fox_kernel(q_ref, k_ref, v_ref, col_ref, qg_ref, kg_ref, o_ref, ka_ref, vt_ref, m_ref, acc_ref, *, S):
    h = pl.program_id(1)
    i = pl.program_id(2)
    t = ATT_BLOCK
    lane = lax.broadcasted_iota(jnp.int32, (t, LANES), 1)

    @pl.when(i == 0)
    def _():
        def prep_keys(n, _):
            sl = pl.ds(pl.multiple_of(n * t, t), t)
            ka_ref[sl, 0:HEAD_DIM] = _rms(k_ref[sl, :].astype(F32), kg_ref[...]).astype(BF16)
            hi, mid, lo = _split3(-LOG2E * _pick_lane(col_ref[sl, :], LANE_F + h))
            ka_ref[sl, HEAD_DIM:] = jnp.where(
                lane < 3, 1.0, jnp.where(lane == 3, hi, jnp.where(lane == 4, mid, jnp.where(lane == 5, lo, 0.0)))
            ).astype(BF16)
            _store_vt(vt_ref, v_ref, sl)
            return 0

        lax.fori_loop(0, S // t, prep_keys, 0)

    qsl = pl.ds(pl.multiple_of(i * t, t), t)
    qn = (_rms(q_ref[...].astype(F32), qg_ref[...]) * (LOG2E * HEAD_DIM ** -0.5)).astype(BF16)
    hi, mid, lo = _split3(LOG2E * _pick_lane(col_ref[qsl, :], LANE_F + h))
    aug = jnp.where(lane == 0, hi, jnp.where(lane == 1, mid, jnp.where(lane == 2, lo, jnp.where(lane < 6, 1.0, 0.0))))
    qa = jnp.concatenate([qn, aug.astype(BF16)], axis=-1)
    attend = functools.partial(_attend, qa, ka_ref, vt_ref, m_ref, acc_ref)
    attend(i * t, t, keep=_causal_keep(), first=True)
    _attend_past(i, attend)
    _attend_finish(acc_ref, o_ref)


def fox_attention(z, col, q_gain, k_gain):
    B, S, _ = z.shape
    t = ATT_BLOCK
    return pl.pallas_call(
        functools.partial(_fox_kernel, S=S),
        grid=(B, FOX_HEADS, S // t),
        in_specs=[
            pl.BlockSpec((None, t, HEAD_DIM), lambda b, h, i: (b, i, FOX_Q0 + h)),
            pl.BlockSpec((None, S, HEAD_DIM), lambda b, h, i: (b, 0, FOX_K0 + h)),
            pl.BlockSpec((None, S, HEAD_DIM), lambda b, h, i: (b, 0, FOX_V0 + h)),
            pl.BlockSpec((None, S, LANES), lambda b, h, i: (b, 0, 0)),
            pl.BlockSpec((1, HEAD_DIM), lambda b, h, i: (0, 0)),
            pl.BlockSpec((1, HEAD_DIM), lambda b, h, i: (0, 0)),
        ],
        out_specs=pl.BlockSpec((None, t, HEAD_DIM), lambda b, h, i: (b, i, h)),
        out_shape=jax.ShapeDtypeStruct((B, S, FOX_HEADS * HEAD_DIM), BF16),
        scratch_shapes=[pltpu.VMEM((S, 2 * HEAD_DIM), BF16), pltpu.VMEM((V_ROWS, S), BF16),
                        pltpu.VMEM((1, t), F32), pltpu.VMEM((V_ROWS, t), F32)],
        compiler_params=_params("parallel", "parallel", "arbitrary"),
        name="fox_attention",
    )(z, z, z, col, q_gain.reshape(1, -1), k_gain.reshape(1, -1))


def _unit_lower_inverse(m):
    n = m.shape[0]
    r = lax.broadcasted_iota(jnp.int32, (n, n), 0)
    c = lax.broadcasted_iota(jnp.int32, (n, n), 1)
    eye = (r == c).astype(F32)

    def same_block(b):
        return (r // b) == (c // b)

    p = jnp.where(same_block(INV_BLOCK), m, 0.0)
    inv = eye - p
    k = 2
    while k < INV_BLOCK:
        pb = p.astype(BF16)
        p = jnp.dot(pb, pb, preferred_element_type=F32)
        inv = jnp.dot(inv.astype(BF16), (eye + p).astype(BF16), preferred_element_type=F32)
        k *= 2
    b = INV_BLOCK
    while b < n:
        off = jnp.where(same_block(2 * b), jnp.where(same_block(b), 0.0, m), 0.0).astype(BF16)
        ib = inv.astype(BF16)
        left = jnp.dot(ib, off, preferred_element_type=F32).astype(BF16)
        inv = inv - jnp.dot(left, ib, preferred_element_type=F32)
        b *= 2
    return inv


def _gdn_kernel(q_ref, k_ref, v_ref, gg_ref, wq_ref, wk_ref, wv_ref, col_ref, row_ref, on_ref,
                o_ref, pad_ref, qs_ref, ks_ref, vs_ref, *, S, rows):
    h = pl.program_id(1)
    C = GDN_CHUNK
    D = HEAD_DIM

    def conv_silu(x_ref, w_ref, dst_ref, mode):
        pad_ref[0:8, :] = jnp.zeros((8, D), F32)
        for n in range(S // rows):
            pad_ref[8 + n * rows:8 + (n + 1) * rows, :] = x_ref[n * rows:(n + 1) * rows, :].astype(F32)
        for n in range(S // rows):
            y = jnp.zeros((rows, D), F32)
            for tap in range(CONV_WIDTH):
                start = 8 + n * rows - (CONV_WIDTH - 1) + tap
                y = y + w_ref[tap:tap + 1, :] * pad_ref[start:start + rows, :]
            y = y / (1.0 + jnp.exp(-y))
            if mode != "v":
                y = y * lax.rsqrt(jnp.sum(y * y, axis=-1, keepdims=True) + EPS)
            if mode == "q":
                y = y * D ** -0.5
            dst_ref[n * rows:(n + 1) * rows, :] = y

    conv_silu(q_ref, wq_ref, qs_ref, "q")
    conv_silu(k_ref, wk_ref, ks_ref, "k")
    conv_silu(v_ref, wv_ref, vs_ref, "v")

    r = lax.broadcasted_iota(jnp.int32, (C, C), 0)
    c = lax.broadcasted_iota(jnp.int32, (C, C), 1)
    incl = r >= c
    strict = r > c

    def chunk(n, state):
        sl = pl.ds(pl.multiple_of(n * C, C), C)
        q = qs_ref[sl, :]
        k = ks_ref[sl, :]
        v = vs_ref[sl, :]
        tab = col_ref[sl, :]
        gcol = _pick_lane(tab, LANE_A + h)
        beta = _pick_lane(tab, LANE_B + h)
        grow = row_ref[:, sl]
        glast = gcol[C - 1:C, :]
        decay = jnp.where(incl, jnp.exp(jnp.where(incl, gcol - grow, 0.0)), 0.0)
        eg = jnp.exp(gcol)
        kb = k * beta
        kbf = k.astype(BF16)
        m = jnp.where(strict, _dot_nt(kb.astype(BF16), kbf) * decay, 0.0)
        tinv = _unit_lower_inverse(m).astype(BF16)
        rhs = jnp.concatenate([v * beta, kb * eg], axis=-1).astype(BF16)
        sol = jnp.dot(tinv, rhs, preferred_element_type=F32)
        u = sol[:, :D]
        w = sol[:, D:]
        attn = _dot_nt(q.astype(BF16), kbf) * decay
        qg = q * eg
        kg = k * jnp.exp(glast - gcol)
        sb = state.astype(BF16)
        v_new = u - jnp.dot(w.astype(BF16), sb, preferred_element_type=F32)
        vb = v_new.astype(BF16)
        o = (jnp.dot(qg.astype(BF16), sb, preferred_element_type=F32)
             + jnp.dot(attn.astype(BF16), vb, preferred_element_type=F32))
        state = state * jnp.exp(glast) + jnp.dot(kg.T.astype(BF16), vb, preferred_element_type=F32)
        gate = gg_ref[sl, :].astype(F32)
        o_ref[sl, :] = (_rms(o, on_ref[...]) * (gate / (1.0 + jnp.exp(-gate)))).astype(o_ref.dtype)
        return state

    lax.fori_loop(0, S // C, chunk, jnp.zeros((D, D), F32))


def gdn_mixer(z, conv_w, col, row, o_gain):
    B, S, _ = z.shape
    D = HEAD_DIM
    rows = min(S, 512)
    seq = lambda off: pl.BlockSpec((None, S, D), lambda b, h: (b, 0, off + h))
    cw = lambda off: pl.BlockSpec((CONV_WIDTH, D), lambda b, h: (0, off + h))
    return pl.pallas_call(
        functools.partial(_gdn_kernel, S=S, rows=rows),
        grid=(B, GDN_HEADS),
        in_specs=[
            seq(GDN_Q0), seq(GDN_K0), seq(GDN_V0), seq(GDN_G0),
            cw(0), cw(GDN_HEADS), cw(2 * GDN_HEADS),
            pl.BlockSpec((None, S, LANES), lambda b, h: (b, 0, 0)),
            pl.BlockSpec((None, None, 1, S), lambda b, h: (b, LANE_A + h, 0, 0)),
            pl.BlockSpec((1, D), lambda b, h: (0, 0)),
        ],
        out_specs=pl.BlockSpec((None, S, D), lambda b, h: (b, 0, h)),
        out_shape=jax.ShapeDtypeStruct((B, S, GDN_HEADS * D), BF16),
        scratch_shapes=[pltpu.VMEM((S + 8, D), F32)] + [pltpu.VMEM((S, D), F32)] * 3,
        compiler_params=_params("parallel", "parallel"),
        name="gdn_mixer",
    )(z, z, z, z, conv_w, conv_w, conv_w, col, row, o_gain.reshape(1, D))


def _rope(x, cos, sin_signed):
    return x * cos + pltpu.roll(x, HEAD_DIM // 2, 1) * sin_signed


def _moba_kernel(q_ref, k_ref, v_ref, cos_ref, sin_ref, qg_ref, kg_ref, o_ref,
                 ka_ref, vt_ref, kmean_ref, m_ref, acc_ref, *, S):
    i = pl.program_id(2)
    t = ATT_BLOCK
    lane = lax.broadcasted_iota(jnp.int32, (t, LANES), 1)

    @pl.when(i == 0)
    def _():
        kmean_ref[...] = jnp.zeros(kmean_ref.shape, F32)

        def prep_keys(n, _):
            sl = pl.ds(pl.multiple_of(n * t, t), t)
            k = _rope(_rms(k_ref[sl, :].astype(F32), kg_ref[...]), cos_ref[sl, :], sin_ref[sl, :])
            ka_ref[sl, 0:HEAD_DIM] = k.astype(BF16)
            ka_ref[sl, HEAD_DIM:] = jnp.where(lane == n, 1.0, 0.0).astype(BF16)
            kmean_ref[pl.ds(n, 1), :] = jnp.mean(k, axis=0, keepdims=True)
            _store_vt(vt_ref, v_ref, sl)
            return 0

        lax.fori_loop(0, S // t, prep_keys, 0)

    qsl = pl.ds(pl.multiple_of(i * t, t), t)
    q = _rope(_rms(q_ref[...].astype(F32), qg_ref[...]), cos_ref[qsl, :], sin_ref[qsl, :])
    gate = _dot_nt(q, kmean_ref[...], precision=lax.Precision.HIGHEST)
    gate = jnp.where(lane < i, gate, -jnp.inf)
    sel_bias = jnp.full((t, LANES), NEG, F32)
    for _ in range(MOBA_TOPK):
        top = jnp.max(gate, axis=-1, keepdims=True)
        first = jnp.min(jnp.where(gate == top, lane, LANES), axis=-1, keepdims=True)
        pick = lane == first
        sel_bias = jnp.where(pick & (first < i), 0.0, sel_bias)
        gate = jnp.where(pick, -jnp.inf, gate)

    qs = (q * (LOG2E * HEAD_DIM ** -0.5)).astype(BF16)
    attend = functools.partial(_attend, jnp.concatenate([qs, jnp.zeros((t, LANES), BF16)], axis=-1),
                               ka_ref, vt_ref, m_ref, acc_ref)
    attend(i * t, t, keep=_causal_keep(), first=True)
    attend = functools.partial(_attend, jnp.concatenate([qs, sel_bias.astype(BF16)], axis=-1),
                               ka_ref, vt_ref, m_ref, acc_ref)
    _attend_past(i, attend)
    _attend_finish(acc_ref, o_ref)


def moba_attention(z, cos, sin_signed, q_gain, k_gain):
    B, S, W = z.shape
    H = W // (3 * HEAD_DIM)
    t = ATT_BLOCK
    assert t == MOBA_BLOCK
    return pl.pallas_call(
        functools.partial(_moba_kernel, S=S),
        grid=(B, H, S // t),
        in_specs=[
            pl.BlockSpec((None, t, HEAD_DIM), lambda b, h, i: (b, i, h)),
            pl.BlockSpec((None, S, HEAD_DIM), lambda b, h, i: (b, 0, H + h)),
            pl.BlockSpec((None, S, HEAD_DIM), lambda b, h, i: (b, 0, 2 * H + h)),
            pl.BlockSpec((S, HEAD_DIM), lambda b, h, i: (0, 0)),
            pl.BlockSpec((S, HEAD_DIM), lambda b, h, i: (0, 0)),
            pl.BlockSpec((1, HEAD_DIM), lambda b, h, i: (0, 0)),
            pl.BlockSpec((1, HEAD_DIM), lambda b, h, i: (0, 0)),
        ],
        out_specs=pl.BlockSpec((None, t, HEAD_DIM), lambda b, h, i: (b, i, h)),
        out_shape=jax.ShapeDtypeStruct((B, S, H * HEAD_DIM), BF16),
        scratch_shapes=[pltpu.VMEM((S, 2 * HEAD_DIM), BF16), pltpu.VMEM((V_ROWS, S), BF16),
                        pltpu.VMEM((LANES, HEAD_DIM), F32), pltpu.VMEM((1, t), F32), pltpu.VMEM((V_ROWS, t), F32)],
        compiler_params=_params("parallel", "parallel", "arbitrary"),
        name="moba_attention",
    )(z, z, z, cos, sin_signed, q_gain.reshape(1, -1), k_gain.reshape(1, -1))


META_IDX, META_RANK, META_GATE = 0, 2, 4


def _router_kernel(x_ref, g_ref, w_ref, xn_ref, meta_ref, base_ref, cnt_ref, carry_ref):
    i = pl.program_id(0)
    tm = x_ref.shape[0]

    @pl.when(i == 0)
    def _():
        carry_ref[...] = jnp.zeros(carry_ref.shape, F32)

    xn = _rms(x_ref[...], g_ref[...])
    xn_ref[...] = xn.astype(BF16)
    logits = jnp.dot(xn, w_ref[...], preferred_element_type=F32, precision=lax.Precision.HIGHEST)
    lane = lax.broadcasted_iota(jnp.int32, logits.shape, 1)
    logits = jnp.where(lane < N_EXPERTS, logits, -jnp.inf)
    top1 = jnp.max(logits, axis=-1, keepdims=True)
    idx1 = jnp.min(jnp.where(logits == top1, lane, LANES), axis=-1, keepdims=True)
    rest = jnp.where(lane == idx1, -jnp.inf, logits)
    top2 = jnp.max(rest, axis=-1, keepdims=True)
    idx2 = jnp.min(jnp.where(rest == top2, lane, LANES), axis=-1, keepdims=True)
    e2 = jnp.exp(top2 - top1)
    denom = 1.0 + e2
    chosen = jnp.where((lane == idx1) | (lane == idx2), 1.0, 0.0)
    r = lax.broadcasted_iota(jnp.int32, (tm, tm), 0)
    c = lax.broadcasted_iota(jnp.int32, (tm, tm), 1)
    ahead = jnp.dot(jnp.where(r > c, 1.0, 0.0).astype(BF16), chosen.astype(BF16), preferred_element_type=F32)
    carry = carry_ref[...]
    rank = ahead + carry
    rank1 = jnp.sum(jnp.where(lane == idx1, rank, 0.0), axis=-1, keepdims=True)
    rank2 = jnp.sum(jnp.where(lane == idx2, rank, 0.0), axis=-1, keepdims=True)
    vals = (idx1.astype(F32), idx2.astype(F32), rank1, rank2, 1.0 / denom, e2 / denom)
    meta = jnp.zeros(logits.shape, F32)
    for n, v in enumerate(vals):
        meta = jnp.where(lane == n, v, meta)
    meta_ref[...] = meta
    base_ref[...] = jnp.broadcast_to(carry, base_ref.shape)
    carry = carry + jnp.sum(chosen, axis=0, keepdims=True)
    carry_ref[...] = carry
    cnt_ref[...] = jnp.broadcast_to(carry, cnt_ref.shape)


def moe_router(x, gain, w_router, *, tm):
    M, K = x.shape
    w = jnp.zeros((K, LANES), F32).at[:, :N_EXPERTS].set(w_router)
    return pl.pallas_call(
        _router_kernel,
        grid=(M // tm,),
        in_specs=[
            pl.BlockSpec((tm, K), lambda i: (i, 0)),
            pl.BlockSpec((1, K), lambda i: (0, 0)),
            pl.BlockSpec((K, LANES), lambda i: (0, 0)),
        ],
        out_specs=[
            pl.BlockSpec((tm, K), lambda i: (i, 0)),
            pl.BlockSpec((tm, LANES), lambda i: (i, 0)),
            pl.BlockSpec((None, 8, LANES), lambda i: (i, 0, 0)),
            pl.BlockSpec((8, LANES), lambda i: (0, 0)),
        ],
        out_shape=[
            jax.ShapeDtypeStruct((M, K), BF16),
            jax.ShapeDtypeStruct((M, LANES), F32),
            jax.ShapeDtypeStruct((M // tm, 8, LANES), F32),
            jax.ShapeDtypeStruct((8, LANES), F32),
        ],
        scratch_shapes=[pltpu.VMEM((1, LANES), F32)],
        compiler_params=_params("arbitrary"),
        name="moe_router",
    )(x, gain.reshape(1, K), w)


def _permute_in_kernel(s_ref, t_ref, flag_ref, x_ref, dest_ref, zero_ref, o_ref, acc_ref):
    del t_ref, zero_ref
    k = pl.program_id(0)
    p = PERM_TILE

    @pl.when(flag_ref[3 * k] != 0)
    def _():
        row = (lax.broadcasted_iota(jnp.int32, (p, p), 0) + s_ref[k] * p).astype(F32)
        hit = (dest_ref[0:1, :] == row) | (dest_ref[1:2, :] == row)
        got = jnp.dot(jnp.where(hit, 1.0, 0.0).astype(BF16), x_ref[...], preferred_element_type=F32)

        @pl.when(flag_ref[3 * k + 1] != 0)
        def _():
            acc_ref[...] = got

        @pl.when(flag_ref[3 * k + 1] == 0)
        def _():
            acc_ref[...] += got

        @pl.when(flag_ref[3 * k + 2] != 0)
        def _():
            o_ref[...] = acc_ref[...].astype(o_ref.dtype)


def permute_in(xn, dest_rows, items, n_rows):
    M, K = xn.shape
    p = PERM_TILE
    s_item, t_item, flags = items
    grid_spec = pltpu.PrefetchScalarGridSpec(
        num_scalar_prefetch=3,
        grid=(s_item.shape[0],),
        in_specs=[
            pl.BlockSpec((p, K), lambda k, s, t, f: (t[k], 0)),
            pl.BlockSpec((None, 8, p), lambda k, s, t, f: (t[k], 0, 0)),
            pl.BlockSpec(memory_space=pl.ANY),
        ],
        out_specs=pl.BlockSpec((p, K), lambda k, s, t, f: (s[k], 0)),
        scratch_shapes=[pltpu.VMEM((p, K), F32)],
    )
    return pl.pallas_call(
        _permute_in_kernel,
        grid_spec=grid_spec,
        out_shape=jax.ShapeDtypeStruct((n_rows, K), BF16),
        input_output_aliases={5: 0},
        compiler_params=_params("arbitrary"),
        name="moe_permute_in",
    )(s_item, t_item, flags, xn, dest_rows, jnp.zeros((n_rows, K), BF16))


def _permute_out_kernel(s_ref, flag_ref, h_ref, y_ref, meta_ref, o_ref):
    ti = pl.program_id(0)
    c = pl.program_id(1)
    k = ti * pl.num_programs(1) + c
    p = PERM_TILE

    @pl.when(c == 0)
    def _():
        o_ref[...] = h_ref[...]

    @pl.when(flag_ref[k] != 0)
    def _():
        lo = (s_ref[k] * p).astype(F32)
        row = lax.broadcasted_iota(jnp.int32, (p, p), 1).astype(F32) + lo
        d1 = meta_ref[:, 0:1]
        d2 = meta_ref[:, 1:2]
        hit = (d1 == row) | (d2 == row)
        got = jnp.dot(jnp.where(hit, 1.0, 0.0).astype(BF16), y_ref[...], preferred_element_type=F32)
        gate = (jnp.where((d1 >= lo) & (d1 < lo + p), meta_ref[:, 2:3], 0.0)
                + jnp.where((d2 >= lo) & (d2 < lo + p), meta_ref[:, 3:4], 0.0))
        o_ref[...] += gate * got


def permute_out(h, y, dest_gate, items):
    M, K = h.shape
    p = PERM_TILE
    s_item, flags = items
    n_cand = s_item.shape[0] // (M // p)
    grid_spec = pltpu.PrefetchScalarGridSpec(
        num_scalar_prefetch=2,
        grid=(M // p, n_cand),
        in_specs=[
            pl.BlockSpec((p, K), lambda ti, c, s, f: (ti, 0)),
            pl.BlockSpec((p, K), lambda ti, c, s, f: (s[ti * n_cand + c], 0)),
            pl.BlockSpec((p, LANES), lambda ti, c, s, f: (ti, 0)),
        ],
        out_specs=pl.BlockSpec((p, K), lambda ti, c, s, f: (ti, 0)),
    )
    return pl.pallas_call(
        _permute_out_kernel,
        grid_spec=grid_spec,
        out_shape=jax.ShapeDtypeStruct((M, K), F32),
        compiler_params=_params("parallel", "arbitrary"),
        name="moe_permute_out",
    )(s_item, flags, h, y, dest_gate)


def _expert_ffn_kernel(te_ref, nu_ref, x_ref, wg_ref, wu_ref, wd_ref, o_ref, acc_ref):
    del te_ref
    i = pl.program_id(0)
    j = pl.program_id(1)

    @pl.when(i < nu_ref[0])
    def _():
        x = x_ref[...]
        gt = jnp.dot(x, wg_ref[...], preferred_element_type=F32)
        up = jnp.dot(x, wu_ref[...], preferred_element_type=F32)
        act = (gt / (1.0 + jnp.exp(-gt)) * up).astype(BF16)
        part = jnp.dot(act, wd_ref[...], preferred_element_type=F32)

        @pl.when(j == 0)
        def _():
            acc_ref[...] = part

        @pl.when(j > 0)
        def _():
            acc_ref[...] += part

        @pl.when(j == pl.num_programs(1) - 1)
        def _():
            o_ref[...] = acc_ref[...].astype(o_ref.dtype)


def expert_ffn(xs, tile_expert, n_used, w_gate_up, w_down, *, tm, tf):
    R, K = xs.shape
    E, F, _ = w_down.shape
    nf = F // tf
    used = lambda i, nu: jnp.minimum(i, nu[0] - 1)
    grid_spec = pltpu.PrefetchScalarGridSpec(
        num_scalar_prefetch=2,
        grid=(R // tm, nf),
        in_specs=[
            pl.BlockSpec((tm, K), lambda i, j, te, nu: (used(i, nu), 0)),
            pl.BlockSpec((None, K, tf), lambda i, j, te, nu: (te[used(i, nu)], 0, jnp.where(i < nu[0], j, nf - 1))),
            pl.BlockSpec((None, K, tf),
                         lambda i, j, te, nu: (te[used(i, nu)], 0, nf + jnp.where(i < nu[0], j, nf - 1))),
            pl.BlockSpec((None, tf, K), lambda i, j, te, nu: (te[used(i, nu)], jnp.where(i < nu[0], j, nf - 1), 0)),
        ],
        out_specs=pl.BlockSpec((tm, K), lambda i, j, te, nu: (used(i, nu), 0)),
        scratch_shapes=[pltpu.VMEM((tm, K), F32)],
    )
    return pl.pallas_call(
        _expert_ffn_kernel,
        grid_spec=grid_spec,
        out_shape=jax.ShapeDtypeStruct((R, K), BF16),
        compiler_params=_params("arbitrary", "arbitrary"),
        name="moe_expert_ffn",
    )(tile_expert, n_used, xs, w_gate_up, w_gate_up, w_down)


def _fill_invalid(valid, *arrays):
    n = valid.shape[0]
    pos = jnp.where(valid, jnp.arange(n), -1)
    last = lax.cummax(pos)
    last = jnp.where(last < 0, jnp.argmax(valid), last)
    return [a[last] for a in arrays]


def moe_residual(h, gain, w_router, w_gate_up, w_down):
    M, K = h.shape
    p = PERM_TILE
    tm = EXPERT_TILE
    ntt = M // p
    n_rows = 2 * M + N_EXPERTS * tm
    xn, meta, base, cnt = moe_router(h, gain, w_router, tm=p)

    idx = meta[:, META_IDX:META_IDX + 2].astype(jnp.int32)
    rank = meta[:, META_RANK:META_RANK + 2].astype(jnp.int32)
    counts = cnt[0, :N_EXPERTS].astype(jnp.int32)
    padded = (counts + tm - 1) // tm * tm
    ends = jnp.cumsum(padded)
    offsets = ends - padded
    n_used = (ends[-1] // tm).reshape(1)
    tile_expert = jnp.minimum(jnp.searchsorted(ends, jnp.arange(n_rows // tm) * tm, side="right"),
                              N_EXPERTS - 1).astype(jnp.int32)
    dest = (offsets[idx] + rank).astype(F32)
    dest_rows = jnp.zeros((ntt, 8, p), F32).at[:, 0:2, :].set(dest.reshape(ntt, p, 2).transpose(0, 2, 1))
    dest_gate = jnp.zeros((M, LANES), F32).at[:, 0:2].set(dest).at[:, 2:4].set(meta[:, META_GATE:META_GATE + 2])

    start = base[:, 0, :N_EXPERTS].astype(jnp.int32)
    stop = jnp.concatenate([start[1:], counts[None, :]], axis=0)
    first = (offsets[None, :] + start) // p
    last = (offsets[None, :] + stop - 1) // p
    nonempty = stop > start
    s_cand = jnp.stack([first, first + 1], axis=-1)
    v_cand = jnp.stack([nonempty, nonempty & (last > first)], axis=-1)
    t_cand = jnp.broadcast_to(jnp.arange(ntt)[:, None, None], s_cand.shape)

    s_in = s_cand.transpose(1, 0, 2).reshape(-1)
    t_in = t_cand.transpose(1, 0, 2).reshape(-1)
    v_in = v_cand.transpose(1, 0, 2).reshape(-1)
    s_in, t_in = _fill_invalid(v_in, s_in, t_in)
    before = jnp.concatenate([jnp.full((1,), -1, s_in.dtype), lax.cummax(jnp.where(v_in, s_in, -1))[:-1]])
    big = jnp.iinfo(jnp.int32).max
    after = jnp.concatenate([jnp.flip(lax.cummin(jnp.flip(jnp.where(v_in, s_in, big))))[1:],
                             jnp.full((1,), big, s_in.dtype)])
    flags_in = jnp.stack([v_in, v_in & (before != s_in), v_in & (after != s_in)], axis=-1)
    xs = permute_in(xn, dest_rows, (s_in.astype(jnp.int32), t_in.astype(jnp.int32),
                                    flags_in.astype(jnp.int32).reshape(-1)), n_rows)

    ys = expert_ffn(xs, tile_expert, n_used.astype(jnp.int32), w_gate_up, w_down, tm=tm, tf=512)

    s_out = s_cand.reshape(-1)
    v_out = v_cand.reshape(-1)
    (s_out,) = _fill_invalid(v_out, s_out)
    return permute_out(h, ys, dest_gate, (s_out.astype(jnp.int32), v_out.astype(jnp.int32)))


def _ffn_kernel(x_ref, g_ref, wg_ref, wu_ref, wd_ref, o_ref, xn_ref, acc_ref):
    j = pl.program_id(1)

    @pl.when(j == 0)
    def _():
        x = x_ref[...]
        xn_ref[...] = _rms(x, g_ref[...]).astype(BF16)
        acc_ref[...] = x

    xn = xn_ref[...]
    gt = jnp.dot(xn, wg_ref[...], preferred_element_type=F32)
    up = jnp.dot(xn, wu_ref[...], preferred_element_type=F32)
    act = gt / (1.0 + jnp.exp(-gt)) * up
    acc_ref[...] += jnp.dot(act.astype(BF16), wd_ref[...], preferred_element_type=F32)

    @pl.when(j == pl.num_programs(1) - 1)
    def _():
        o_ref[...] = acc_ref[...]


def ffn_residual(x, gain, w_gate_up, w_down, *, tm, tf):
    M, K = x.shape
    F = w_down.shape[0]
    nf = F // tf
    return pl.pallas_call(
        _ffn_kernel,
        grid=(M // tm, nf),
        in_specs=[
            pl.BlockSpec((tm, K), lambda i, j: (i, 0)),
            pl.BlockSpec((1, K), lambda i, j: (0, 0)),
            pl.BlockSpec((K, tf), lambda i, j: (0, j)),
            pl.BlockSpec((K, tf), lambda i, j: (0, nf + j)),
            pl.BlockSpec((tf, K), lambda i, j: (j, 0)),
        ],
        out_specs=pl.BlockSpec((tm, K), lambda i, j: (i, 0)),
        out_shape=jax.ShapeDtypeStruct((M, K), F32),
        scratch_shapes=[pltpu.VMEM((tm, K), BF16), pltpu.VMEM((tm, K), F32)],
        compiler_params=_params("parallel", "arbitrary"),
        name="ffn_residual",
    )(x, gain.reshape(1, K), w_gate_up, w_gate_up, w_down)


def _even_mix(h, B, S, norm1, w_in, f_bias, q_norm, k_norm, conv_w, a_log, dt_bias, o_norm, w_out):
    M, D = h.shape
    fw, gw = FOX_HEADS * HEAD_DIM, GDN_HEADS * HEAD_DIM
    o_ff = 3 * fw
    o_gq = o_ff + FOX_HEADS
    o_ga = o_gq + 3 * gw
    o_gb = o_ga + GDN_HEADS
    o_gg = o_gb + GDN_HEADS
    w_big = jnp.concatenate([w_in[:, :o_ff], w_in[:, o_gq:o_ga], w_in[:, o_gg:]], axis=1).astype(BF16)
    w_small = jnp.concatenate([w_in[:, o_ff:o_gq], w_in[:, o_ga:o_gg],
                               jnp.zeros((D, LANES - FOX_HEADS - 2 * GDN_HEADS), F32)], axis=1).astype(BF16)
    z, zs = norm_matmul(h, norm1, w_big, w_small, tm=512, tn=512)
    par = jnp.zeros((8, LANES), F32)
    par = par.at[0, LANE_F:LANE_F + FOX_HEADS].set(f_bias).at[0, LANE_A:LANE_A + GDN_HEADS].set(dt_bias)
    par = par.at[1, LANE_A:LANE_A + GDN_HEADS].set(a_log)
    col, row = even_gates(zs.reshape(B, S, LANES), par)
    row = row.reshape(B, 16, 1, S)
    z = z.reshape(B, S, -1)
    fox = fox_attention(z, col, q_norm, k_norm)
    gdn = gdn_mixer(z, conv_w, col, row, o_norm)
    w_out = w_out.astype(BF16)
    return matmul_residual(h, [(fox.reshape(M, fw), w_out[:fw]), (gdn.reshape(M, gw), w_out[fw:])], tm=512, tn=512)


def _odd_mix(h, B, S, norm1, w_qkv, q_norm, k_norm, w_out):
    M, D = h.shape
    z = norm_matmul(h, norm1, w_qkv.astype(BF16), tm=512, tn=512).reshape(B, S, -1)
    half = HEAD_DIM // 2
    inv = jnp.power(ROPE_THETA, -jnp.arange(half, dtype=F32) / half)
    ang = jnp.arange(S, dtype=F32)[:, None] * inv[None, :]
    cos, sin = jnp.cos(ang), jnp.sin(ang)
    cos_full = jnp.concatenate([cos, cos], axis=-1)
    sin_signed = jnp.concatenate([-sin, sin], axis=-1)
    att = moba_attention(z, cos_full, sin_signed, q_norm, k_norm)
    return matmul_residual(h, [(att.reshape(M, -1), w_out.astype(BF16))], tm=512, tn=512)


def _odd_moe(h, norm2, w_router, w_gate_up, w_down):
    return moe_residual(h, norm2, w_router, w_gate_up.astype(BF16), w_down.astype(BF16))


def kernel(x, e_norm1, e_w_in, e_fox_f_bias, e_fox_q_norm, e_fox_k_norm, e_gdn_conv, e_gdn_a_log,
           e_gdn_dt_bias, e_gdn_o_norm, e_w_out, e_norm2, e_ffn_w_gate_up, e_ffn_w_down,
           o_norm1, o_w_qkv, o_q_norm, o_k_norm, o_w_out, o_norm2, o_router, o_exp_w_gate_up, o_exp_w_down):
    B, S, D = x.shape
    h = x.reshape(B * S, D)
    depth = e_norm1.shape[0] + o_norm1.shape[0]
    for layer in range(depth):
        i = layer // 2
        if layer % 2 == 0:
            h = _even_mix(h, B, S, e_norm1[i], e_w_in[i], e_fox_f_bias[i], e_fox_q_norm[i], e_fox_k_norm[i],
                          e_gdn_conv[i], e_gdn_a_log[i], e_gdn_dt_bias[i], e_gdn_o_norm[i], e_w_out[i])
            h = ffn_residual(h, e_norm2[i], e_ffn_w_gate_up[i].astype(BF16), e_ffn_w_down[i].astype(BF16),
                             tm=512, tf=256)
        else:
            h = _odd_mix(h, B, S, o_norm1[i], o_w_qkv[i], o_q_norm[i], o_k_norm[i], o_w_out[i])
            h = _odd_moe(h, o_norm2[i], o_router[i], o_exp_w_gate_up[i], o_exp_w_down[i])
    return h.reshape(B, S, D)
```

```python
import functools

import jax
import jax.numpy as jnp
from jax import lax
from jax.experimental import pallas as pl
from jax.experimental.pallas import tpu as pltpu

F32 = jnp.float32
BF16 = jnp.bfloat16

HEAD_DIM = 128
FOX_HEADS = 4
GDN_HEADS = 4
CONV_WIDTH = 4
MOBA_BLOCK = 256
MOBA_TOPK = 3
N_EXPERTS = 8
ROPE_THETA = 10000.0
EPS = 1e-6

LANES = 128
GDN_CHUNK = 128
INV_BLOCK = 16
GDN_HEADS_PER_STEP = 4
NEG = -(2.0 ** 100)
LOG2E = 1.4426950408889634
ATT_BLOCK = 256
V_ROWS = HEAD_DIM + 16
PERM_TILE = 512
ROW_TILE = 1024
EXPERT_TILE = 1024
VMEM_LIMIT_BYTES = 56 * 1024 * 1024

FOX_Q0, FOX_K0, FOX_V0 = 0, 4, 8
GDN_Q0, GDN_K0, GDN_V0, GDN_G0 = 12, 16, 20, 24
LANE_F, LANE_A, LANE_B = 0, 4, 8


def _params(*sem):
    return pltpu.CompilerParams(dimension_semantics=sem, vmem_limit_bytes=VMEM_LIMIT_BYTES)


def _rms(x, gain):
    return x * lax.rsqrt(jnp.mean(x * x, axis=-1, keepdims=True) + EPS) * gain


def _dot_nt(a, b, **kw):
    return lax.dot_general(a, b, (((1,), (1,)), ((), ())), preferred_element_type=F32, **kw)


def _pick_lane(x, lane_idx):
    lane = lax.broadcasted_iota(jnp.int32, x.shape, 1)
    return jnp.sum(jnp.where(lane == lane_idx, x, 0.0), axis=-1, keepdims=True)


def _norm_mm_kernel(x_ref, g_ref, w_ref, *rest, has_aux):
    if has_aux:
        waux_ref, o_ref, oaux_ref, xn_ref = rest
    else:
        o_ref, xn_ref = rest

    @pl.when(pl.program_id(1) == 0)
    def _():
        xn = _rms(x_ref[...], g_ref[...]).astype(BF16)
        xn_ref[...] = xn
        if has_aux:
            oaux_ref[...] = jnp.dot(xn, waux_ref[...], preferred_element_type=F32)

    o_ref[...] = jnp.dot(xn_ref[...], w_ref[...], preferred_element_type=F32).astype(o_ref.dtype)


def norm_matmul(x, gain, w, w_aux=None, *, tm, tn, out_dtype=BF16):
    M, K = x.shape
    N = w.shape[1]
    has_aux = w_aux is not None
    in_specs = [
        pl.BlockSpec((tm, K), lambda i, j: (i, 0)),
        pl.BlockSpec((1, K), lambda i, j: (0, 0)),
        pl.BlockSpec((K, tn), lambda i, j: (0, j)),
    ]
    out_shape = [jax.ShapeDtypeStruct((M, N), out_dtype)]
    out_specs = [pl.BlockSpec((tm, tn), lambda i, j: (i, j))]
    args = [x, gain.reshape(1, K), w]
    if has_aux:
        in_specs.append(pl.BlockSpec((K, LANES), lambda i, j: (0, 0)))
        out_shape.append(jax.ShapeDtypeStruct((M, LANES), F32))
        out_specs.append(pl.BlockSpec((tm, LANES), lambda i, j: (i, 0)))
        args.append(w_aux)
    res = pl.pallas_call(
        functools.partial(_norm_mm_kernel, has_aux=has_aux),
        grid=(M // tm, N // tn),
        in_specs=in_specs,
        out_specs=out_specs,
        out_shape=out_shape,
        scratch_shapes=[pltpu.VMEM((tm, K), BF16)],
        compiler_params=_params("parallel", "arbitrary"),
        name="norm_matmul",
    )(*args)
    return res if has_aux else res[0]


def _mm_res_kernel(*refs, n_in):
    res_ref = refs[0]
    o_ref = refs[1 + 2 * n_in]
    acc = res_ref[...]
    for t in range(n_in):
        acc = acc + jnp.dot(refs[1 + 2 * t][...], refs[2 + 2 * t][...], preferred_element_type=F32)
    o_ref[...] = acc


def matmul_residual(res, pairs, *, tm, tn):
    M, N = res.shape
    in_specs = [pl.BlockSpec((tm, tn), lambda i, j: (i, j))]
    args = [res]
    for a, w in pairs:
        K = a.shape[1]
        in_specs.append(pl.BlockSpec((tm, K), lambda i, j: (i, 0)))
        in_specs.append(pl.BlockSpec((K, tn), lambda i, j: (0, j)))
        args += [a, w]
    return pl.pallas_call(
        functools.partial(_mm_res_kernel, n_in=len(pairs)),
        grid=(M // tm, N // tn),
        in_specs=in_specs,
        out_specs=pl.BlockSpec((tm, tn), lambda i, j: (i, j)),
        out_shape=jax.ShapeDtypeStruct((M, N), F32),
        compiler_params=_params("parallel", "arbitrary"),
        name="matmul_residual",
    )(*args)


def _gate_kernel(zs_ref, par_ref, col_ref, row_ref, *, S):
    C = GDN_CHUNK
    bias = par_ref[0:1, :]
    neg_a = -jnp.exp(par_ref[1:2, :])
    r = lax.broadcasted_iota(jnp.int32, (C, C), 0)
    c = lax.broadcasted_iota(jnp.int32, (C, C), 1)
    tril = (r >= c).astype(F32)
    lane = lax.broadcasted_iota(jnp.int32, (C, LANES), 1)

    def body(n, carry):
        sl = pl.ds(pl.multiple_of(n * C, C), C)
        z = zs_ref[sl, :]
        t = z + bias
        soft = jnp.log(1.0 + jnp.exp(-jnp.abs(t)))
        log_f = jnp.minimum(t, 0.0) - soft
        g = neg_a * (jnp.maximum(t, 0.0) + soft)
        beta = 1.0 / (1.0 + jnp.exp(-z))
        u = jnp.where(lane < LANE_A, log_f, jnp.where(lane < LANE_B, g, 0.0))
        cs = jnp.dot(tril, u, preferred_element_type=F32, precision=lax.Precision.HIGHEST)
        cs = cs + jnp.where(lane < LANE_A, carry, 0.0)
        out = jnp.where(lane < LANE_B, cs, beta)
        col_ref[sl, :] = out
        row_ref[:, sl] = out.T[0:16, :]
        return cs[C - 1:C, :]

    lax.fori_loop(0, S // C, body, jnp.zeros((1, LANES), F32))


def even_gates(zs, par):
    B, S, _ = zs.shape
    return pl.pallas_call(
        functools.partial(_gate_kernel, S=S),
        grid=(B,),
        in_specs=[
            pl.BlockSpec((None, S, LANES), lambda b: (b, 0, 0)),
            pl.BlockSpec((8, LANES), lambda b: (0, 0)),
        ],
        out_specs=[
            pl.BlockSpec((None, S, LANES), lambda b: (b, 0, 0)),
            pl.BlockSpec((None, 16, S), lambda b: (b, 0, 0)),
        ],
        out_shape=[jax.ShapeDtypeStruct((B, S, LANES), F32), jax.ShapeDtypeStruct((B, 16, S), F32)],
        compiler_params=_params("parallel"),
        name="even_gates",
    )(zs, par)


def _split3(x):
    hi = x.astype(BF16).astype(F32)
    mid = (x - hi).astype(BF16).astype(F32)
    lo = (x - hi - mid).astype(BF16).astype(F32)
    return hi, mid, lo


def _interleave(gens):
    out = [None] * len(gens)
    live = list(range(len(gens)))
    while live:
        for n in list(live):
            try:
                next(gens[n])
            except StopIteration as stop:
                out[n] = stop.value
                live.remove(n)
    return out


def _attend(state, c, qa, ka_ref, vt_ref, key0, nkeys, keep=None):
    ks = pl.ds(pl.multiple_of(key0, ATT_BLOCK), nkeys)
    st = _dot_nt(ka_ref[ks, :], qa)
    yield
    if keep is not None:
        st = jnp.where(keep, st, NEG)
    m_new = jnp.max(st, axis=0, keepdims=True)
    if state[c] is not None:
        m_old, acc_old = state[c]
        m_new = jnp.maximum(m_old, m_new)
    p = jnp.exp2(st - m_new).astype(BF16)
    pv = jnp.dot(vt_ref[:, ks], p, preferred_element_type=F32)
    state[c] = (m_new, pv if state[c] is None else acc_old * jnp.exp2(m_old - m_new) + pv)


def _attend_tile_pair(i, qa_diag, qa_past, ka_ref, vt_ref, m_ref, acc_ref, o_ref):
    t = ATT_BLOCK
    step = functools.partial(_attend, ka_ref=ka_ref, vt_ref=vt_ref)

    def save(state):
        m_ref[...] = jnp.stack([state[0][0], state[1][0]])
        acc_ref[...] = jnp.stack([state[0][1], state[1][1]])

    def past_blocks(key0, n_steps):
        state = {c: (m_ref[c], acc_ref[c]) for c in range(2)}
        _interleave([step(state, c, qa_past[c], key0=key0 + s * 2 * t, nkeys=2 * t)
                     for s in range(n_steps) for c in range(2)])
        save(state)

    state = {0: None, 1: None}
    keep = _causal_keep()
    _interleave([step(state, 0, qa_diag[0], key0=(2 * i) * t, nkeys=t, keep=keep),
                 step(state, 1, qa_diag[1], key0=(2 * i + 1) * t, nkeys=t, keep=keep),
                 step(state, 1, qa_past[1], key0=(2 * i) * t, nkeys=t)])
    save(state)

    def four_blocks(g, _):
        past_blocks(g * (4 * t), 2)
        return 0

    lax.fori_loop(0, i // 2, four_blocks, 0)

    @pl.when((i & 1) != 0)
    def _():
        past_blocks((i // 2) * (4 * t), 1)

    for c in range(2):
        acc = acc_ref[c]
        out_t = acc[:HEAD_DIM, :] * (1.0 / acc[HEAD_DIM:HEAD_DIM + 1, :])
        o_ref[c * t:(c + 1) * t, :] = out_t.T.astype(o_ref.dtype)


def _store_vt(vt_ref, v_ref, sl):
    vt_ref[0:HEAD_DIM, sl] = v_ref[sl, :].astype(F32).T.astype(BF16)
    vt_ref[HEAD_DIM:V_ROWS, sl] = jnp.ones((V_ROWS - HEAD_DIM, ATT_BLOCK), BF16)


def _causal_keep():
    key = lax.broadcasted_iota(jnp.int32, (ATT_BLOCK, ATT_BLOCK), 0)
    qry = lax.broadcasted_iota(jnp.int32, (ATT_BLOCK, ATT_BLOCK), 1)
    return key <= qry


_ATT_SCRATCH = lambda S: [pltpu.VMEM((S, 2 * HEAD_DIM), BF16), pltpu.VMEM((V_ROWS, S), BF16),
                          pltpu.VMEM((2, 1, ATT_BLOCK), F32), pltpu.VMEM((2, V_ROWS, ATT_BLOCK), F32)]


def _fox_kernel(q_ref, k_ref, v_ref, col_ref, qg_ref, kg_ref, o_ref, ka_ref, vt_ref, m_ref, acc_ref, *, S):
    h = pl.program_id(1)
    i = pl.program_id(2)
    t = ATT_BLOCK
    lane = lax.broadcasted_iota(jnp.int32, (t, LANES), 1)

    @pl.when(i == 0)
    def _():
        def prep_keys(n, _):
            sl = pl.ds(pl.multiple_of(n * t, t), t)
            ka_ref[sl, 0:HEAD_DIM] = _rms(k_ref[sl, :].astype(F32), kg_ref[...]).astype(BF16)
            hi, mid, lo = _split3(-LOG2E * _pick_lane(col_ref[sl, :], LANE_F + h))
            ka_ref[sl, HEAD_DIM:] = jnp.where(
                lane < 3, 1.0, jnp.where(lane == 3, hi, jnp.where(lane == 4, mid, jnp.where(lane == 5, lo, 0.0)))
            ).astype(BF16)
            _store_vt(vt_ref, v_ref, sl)
            return 0

        lax.fori_loop(0, S // t, prep_keys, 0)

    qa = []
    for c in range(2):
        qn = (_rms(q_ref[c * t:(c + 1) * t, :].astype(F32), qg_ref[...]) * (LOG2E * HEAD_DIM ** -0.5)).astype(BF16)
        qsl = pl.ds(pl.multiple_of((2 * i + c) * t, t), t)
        hi, mid, lo = _split3(LOG2E * _pick_lane(col_ref[qsl, :], LANE_F + h))
        aug = jnp.where(lane == 0, hi,
                        jnp.where(lane == 1, mid, jnp.where(lane == 2, lo, jnp.where(lane < 6, 1.0, 0.0))))
        qa.append(jnp.concatenate([qn, aug.astype(BF16)], axis=-1))
    _attend_tile_pair(i, qa, qa, ka_ref, vt_ref, m_ref, acc_ref, o_ref)


def fox_attention(z, col, q_gain, k_gain):
    B, S, _ = z.shape
    t = 2 * ATT_BLOCK
    return pl.pallas_call(
        functools.partial(_fox_kernel, S=S),
        grid=(B, FOX_HEADS, S // t),
        in_specs=[
            pl.BlockSpec((None, t, HEAD_DIM), lambda b, h, i: (b, i, FOX_Q0 + h)),
            pl.BlockSpec((None, S, HEAD_DIM), lambda b, h, i: (b, 0, FOX_K0 + h)),
            pl.BlockSpec((None, S, HEAD_DIM), lambda b, h, i: (b, 0, FOX_V0 + h)),
            pl.BlockSpec((None, S, LANES), lambda b, h, i: (b, 0, 0)),
            pl.BlockSpec((1, HEAD_DIM), lambda b, h, i: (0, 0)),
            pl.BlockSpec((1, HEAD_DIM), lambda b, h, i: (0, 0)),
        ],
        out_specs=pl.BlockSpec((None, t, HEAD_DIM), lambda b, h, i: (b, i, h)),
        out_shape=jax.ShapeDtypeStruct((B, S, FOX_HEADS * HEAD_DIM), BF16),
        scratch_shapes=_ATT_SCRATCH(S),
        compiler_params=_params("parallel", "parallel", "arbitrary"),
        name="fox_attention",
    )(z, z, z, col, q_gain.reshape(1, -1), k_gain.reshape(1, -1))


def _unit_lower_inverse(m):
    n = m.shape[0]
    r = lax.broadcasted_iota(jnp.int32, (n, n), 0)
    c = lax.broadcasted_iota(jnp.int32, (n, n), 1)
    eye = (r == c).astype(F32)

    def same_block(b):
        return (r // b) == (c // b)

    p = jnp.where(same_block(INV_BLOCK), m, 0.0)
    inv = eye - p
    k = 2
    while k < INV_BLOCK:
        pb = p.astype(BF16)
        p = jnp.dot(pb, pb, preferred_element_type=F32)
        yield
        inv = jnp.dot(inv.astype(BF16), (eye + p).astype(BF16), preferred_element_type=F32)
        yield
        k *= 2
    b = INV_BLOCK
    while b < n:
        off = jnp.where(same_block(2 * b), jnp.where(same_block(b), 0.0, m), 0.0).astype(BF16)
        ib = inv.astype(BF16)
        left = jnp.dot(ib, off, preferred_element_type=F32).astype(BF16)
        yield
        inv = inv - jnp.dot(left, ib, preferred_element_type=F32)
        yield
        b *= 2
    return inv


def _gdn_kernel(q_ref, k_ref, v_ref, gg_ref, wq_ref, wk_ref, wv_ref, col_ref, row_ref, on_ref,
                o_ref, pad_ref, qs_ref, ks_ref, vs_ref, *, S, rows, hb):
    h0 = pl.program_id(1) * hb
    C = GDN_CHUNK
    D = HEAD_DIM

    def conv_silu(x_ref, w_ref, dst_ref, hh, mode):
        cols = slice(hh * D, (hh + 1) * D)
        pad_ref[0:8, :] = jnp.zeros((8, D), F32)

        def fill(n, _):
            src = pl.ds(pl.multiple_of(n * rows, rows), rows)
            pad_ref[pl.ds(pl.multiple_of(8 + n * rows, 8), rows), :] = x_ref[src, cols].astype(F32)
            return 0

        lax.fori_loop(0, S // rows, fill, 0)
        w = w_ref[:, cols]

        def conv(n, _):
            base = pl.multiple_of(n * rows, rows)
            win = pad_ref[pl.ds(base, rows + 8), :]
            y = jnp.zeros((rows, D), F32)
            for tap in range(CONV_WIDTH):
                lead = 8 - (CONV_WIDTH - 1) + tap
                y = y + w[tap:tap + 1, :] * pltpu.roll(win, rows + 8 - lead, 0)[0:rows, :]
            y = y / (1.0 + jnp.exp(-y))
            if mode != "v":
                y = y * lax.rsqrt(jnp.sum(y * y, axis=-1, keepdims=True) + EPS)
            if mode == "q":
                y = y * D ** -0.5
            dst_ref[hh, pl.ds(base, rows), :] = y.astype(dst_ref.dtype)
            return 0

        lax.fori_loop(0, S // rows, conv, 0)

    for hh in range(hb):
        conv_silu(q_ref, wq_ref, qs_ref, hh, "q")
        conv_silu(k_ref, wk_ref, ks_ref, hh, "k")
        conv_silu(v_ref, wv_ref, vs_ref, hh, "v")

    r = lax.broadcasted_iota(jnp.int32, (C, C), 0)
    c = lax.broadcasted_iota(jnp.int32, (C, C), 1)
    incl = r >= c
    strict = r > c

    def head_chunk(hh, sl, tab, state):
        q = qs_ref[hh, sl, :].astype(F32)
        k = ks_ref[hh, sl, :].astype(F32)
        v = vs_ref[hh, sl, :].astype(F32)
        gcol = _pick_lane(tab, LANE_A + h0 + hh)
        beta = _pick_lane(tab, LANE_B + h0 + hh)
        grow = row_ref[hh, :, sl]
        glast = gcol[C - 1:C, :]
        decay = jnp.where(incl, jnp.exp(jnp.where(incl, gcol - grow, 0.0)), 0.0)
        eg = jnp.exp(gcol)
        kb = k * beta
        kbf = k.astype(BF16)
        m = jnp.where(strict, _dot_nt(kb.astype(BF16), kbf) * decay, 0.0)
        attn = _dot_nt(q.astype(BF16), kbf) * decay
        yield
        tinv = (yield from _unit_lower_inverse(m)).astype(BF16)
        rhs = jnp.concatenate([v * beta, kb * eg], axis=-1).astype(BF16)
        sol = jnp.dot(tinv, rhs, preferred_element_type=F32)
        yield
        u = sol[:, :D]
        w = sol[:, D:]
        qg = q * eg
        kg = k * jnp.exp(glast - gcol)
        sb = state.astype(BF16)
        v_new = u - jnp.dot(w.astype(BF16), sb, preferred_element_type=F32)
        o_state = jnp.dot(qg.astype(BF16), sb, preferred_element_type=F32)
        yield
        vb = v_new.astype(BF16)
        o = o_state + jnp.dot(attn.astype(BF16), vb, preferred_element_type=F32)
        state = state * jnp.exp(glast) + jnp.dot(kg.T.astype(BF16), vb, preferred_element_type=F32)
        yield
        gate = gg_ref[sl, hh * D:(hh + 1) * D].astype(F32)
        return state, (_rms(o, on_ref[...]) * (gate / (1.0 + jnp.exp(-gate)))).astype(o_ref.dtype)

    def chunk(n, states):
        sl = pl.ds(pl.multiple_of(n * C, C), C)
        tab = col_ref[sl, :]
        res = _interleave([head_chunk(hh, sl, tab, states[hh]) for hh in range(hb)])
        o_ref[sl, :] = jnp.concatenate([o for _, o in res], axis=-1)
        return tuple(s for s, _ in res)

    lax.fori_loop(0, S // C, chunk, tuple(jnp.zeros((D, D), F32) for _ in range(hb)))


def gdn_mixer(z, conv_w, col, row, o_gain):
    B, S, _ = z.shape
    D = HEAD_DIM
    hb = GDN_HEADS_PER_STEP
    rows = 256
    seq = lambda off: pl.BlockSpec((None, S, hb * D), lambda b, h: (b, 0, off // hb + h),
                                   pipeline_mode=pl.Buffered(1))
    cw = lambda off: pl.BlockSpec((CONV_WIDTH, hb * D), lambda b, h: (0, off // hb + h))
    return pl.pallas_call(
        functools.partial(_gdn_kernel, S=S, rows=rows, hb=hb),
        grid=(B, GDN_HEADS // hb),
        in_specs=[
            seq(GDN_Q0), seq(GDN_K0), seq(GDN_V0), seq(GDN_G0),
            cw(0), cw(GDN_HEADS), cw(2 * GDN_HEADS),
            pl.BlockSpec((None, S, LANES), lambda b, h: (b, 0, 0)),
            pl.BlockSpec((None, hb, 1, S), lambda b, h: (b, LANE_A // hb + h, 0, 0)),
            pl.BlockSpec((1, D), lambda b, h: (0, 0)),
        ],
        out_specs=pl.BlockSpec((None, S, hb * D), lambda b, h: (b, 0, h)),
        out_shape=jax.ShapeDtypeStruct((B, S, GDN_HEADS * D), BF16),
        scratch_shapes=[pltpu.VMEM((S + 8, D), F32)] + [pltpu.VMEM((hb, S, D), BF16)] * 3,
        compiler_params=_params("parallel", "parallel"),
        name="gdn_mixer",
    )(z, z, z, z, conv_w, conv_w, conv_w, col, row, o_gain.reshape(1, D))


def _rope(x, cos, sin_signed):
    return x * cos + pltpu.roll(x, HEAD_DIM // 2, 1) * sin_signed


def _moba_kernel(q_ref, k_ref, v_ref, cos_ref, sin_ref, qg_ref, kg_ref, o_ref,
                 ka_ref, vt_ref, m_ref, acc_ref, kmean_ref, *, S):
    i = pl.program_id(2)
    t = ATT_BLOCK
    lane = lax.broadcasted_iota(jnp.int32, (t, LANES), 1)

    @pl.when(i == 0)
    def _():
        kmean_ref[...] = jnp.zeros(kmean_ref.shape, F32)

        def prep_keys(n, _):
            sl = pl.ds(pl.multiple_of(n * t, t), t)
            k = _rope(_rms(k_ref[sl, :].astype(F32), kg_ref[...]), cos_ref[sl, :], sin_ref[sl, :])
            ka_ref[sl, 0:HEAD_DIM] = k.astype(BF16)
            ka_ref[sl, HEAD_DIM:] = jnp.where(lane == n, 1.0, 0.0).astype(BF16)
            kmean_ref[pl.ds(n, 1), :] = jnp.mean(k, axis=0, keepdims=True)
            _store_vt(vt_ref, v_ref, sl)
            return 0

        lax.fori_loop(0, S // t, prep_keys, 0)

    qa_diag, qa_past = [], []
    for c in range(2):
        cur = 2 * i + c
        qsl = pl.ds(pl.multiple_of(cur * t, t), t)
        q = _rope(_rms(q_ref[c * t:(c + 1) * t, :].astype(F32), qg_ref[...]), cos_ref[qsl, :], sin_ref[qsl, :])
        gate = _dot_nt(q, kmean_ref[...], precision=lax.Precision.HIGHEST)
        gate = jnp.where(lane < cur, gate, -jnp.inf)
        sel_bias = jnp.full((t, LANES), NEG, F32)
        for _ in range(MOBA_TOPK):
            top = jnp.max(gate, axis=-1, keepdims=True)
            first = jnp.min(jnp.where(gate == top, lane, LANES), axis=-1, keepdims=True)
            pick = lane == first
            sel_bias = jnp.where(pick & (first < cur), 0.0, sel_bias)
            gate = jnp.where(pick, -jnp.inf, gate)
        qs = (q * (LOG2E * HEAD_DIM ** -0.5)).astype(BF16)
        qa_diag.append(jnp.concatenate([qs, jnp.zeros((t, LANES), BF16)], axis=-1))
        qa_past.append(jnp.concatenate([qs, sel_bias.astype(BF16)], axis=-1))
    _attend_tile_pair(i, qa_diag, qa_past, ka_ref, vt_ref, m_ref, acc_ref, o_ref)


def moba_attention(z, cos, sin_signed, q_gain, k_gain):
    B, S, W = z.shape
    H = W // (3 * HEAD_DIM)
    assert ATT_BLOCK == MOBA_BLOCK
    t = 2 * ATT_BLOCK
    return pl.pallas_call(
        functools.partial(_moba_kernel, S=S),
        grid=(B, H, S // t),
        in_specs=[
            pl.BlockSpec((None, t, HEAD_DIM), lambda b, h, i: (b, i, h)),
            pl.BlockSpec((None, S, HEAD_DIM), lambda b, h, i: (b, 0, H + h)),
            pl.BlockSpec((None, S, HEAD_DIM), lambda b, h, i: (b, 0, 2 * H + h)),
            pl.BlockSpec((S, HEAD_DIM), lambda b, h, i: (0, 0)),
            pl.BlockSpec((S, HEAD_DIM), lambda b, h, i: (0, 0)),
            pl.BlockSpec((1, HEAD_DIM), lambda b, h, i: (0, 0)),
            pl.BlockSpec((1, HEAD_DIM), lambda b, h, i: (0, 0)),
        ],
        out_specs=pl.BlockSpec((None, t, HEAD_DIM), lambda b, h, i: (b, i, h)),
        out_shape=jax.ShapeDtypeStruct((B, S, H * HEAD_DIM), BF16),
        scratch_shapes=_ATT_SCRATCH(S) + [pltpu.VMEM((LANES, HEAD_DIM), F32)],
        compiler_params=_params("parallel", "parallel", "arbitrary"),
        name="moba_attention",
    )(z, z, z, cos, sin_signed, q_gain.reshape(1, -1), k_gain.reshape(1, -1))


META_IDX, META_RANK, META_GATE = 0, 2, 4


def _router_kernel(x_ref, g_ref, w_ref, xn_ref, meta_ref, base_ref, cnt_ref, carry_ref):
    i = pl.program_id(0)
    tm = x_ref.shape[0]

    @pl.when(i == 0)
    def _():
        carry_ref[...] = jnp.zeros(carry_ref.shape, F32)

    xn = _rms(x_ref[...], g_ref[...])
    xn_ref[...] = xn.astype(BF16)
    logits = jnp.dot(xn, w_ref[...], preferred_element_type=F32, precision=lax.Precision.HIGHEST)
    lane = lax.broadcasted_iota(jnp.int32, logits.shape, 1)
    logits = jnp.where(lane < N_EXPERTS, logits, -jnp.inf)
    top1 = jnp.max(logits, axis=-1, keepdims=True)
    idx1 = jnp.min(jnp.where(logits == top1, lane, LANES), axis=-1, keepdims=True)
    rest = jnp.where(lane == idx1, -jnp.inf, logits)
    top2 = jnp.max(rest, axis=-1, keepdims=True)
    idx2 = jnp.min(jnp.where(rest == top2, lane, LANES), axis=-1, keepdims=True)
    e2 = jnp.exp(top2 - top1)
    denom = 1.0 + e2
    chosen = jnp.where((lane == idx1) | (lane == idx2), 1.0, 0.0)
    r = lax.broadcasted_iota(jnp.int32, (tm, tm), 0)
    c = lax.broadcasted_iota(jnp.int32, (tm, tm), 1)
    ahead = jnp.dot(jnp.where(r > c, 1.0, 0.0).astype(BF16), chosen.astype(BF16), preferred_element_type=F32)
    carry = carry_ref[...]
    rank = ahead + carry
    rank1 = jnp.sum(jnp.where(lane == idx1, rank, 0.0), axis=-1, keepdims=True)
    rank2 = jnp.sum(jnp.where(lane == idx2, rank, 0.0), axis=-1, keepdims=True)
    vals = (idx1.astype(F32), idx2.astype(F32), rank1, rank2, 1.0 / denom, e2 / denom)
    meta = jnp.zeros(logits.shape, F32)
    for n, v in enumerate(vals):
        meta = jnp.where(lane == n, v, meta)
    meta_ref[...] = meta
    base_ref[...] = jnp.broadcast_to(carry, base_ref.shape)
    carry = carry + jnp.sum(chosen, axis=0, keepdims=True)
    carry_ref[...] = carry
    cnt_ref[...] = jnp.broadcast_to(carry, cnt_ref.shape)


def moe_router(x, gain, w_router, *, tm):
    M, K = x.shape
    w = jnp.zeros((K, LANES), F32).at[:, :N_EXPERTS].set(w_router)
    return pl.pallas_call(
        _router_kernel,
        grid=(M // tm,),
        in_specs=[
            pl.BlockSpec((tm, K), lambda i: (i, 0)),
            pl.BlockSpec((1, K), lambda i: (0, 0)),
            pl.BlockSpec((K, LANES), lambda i: (0, 0)),
        ],
        out_specs=[
            pl.BlockSpec((tm, K), lambda i: (i, 0)),
            pl.BlockSpec((tm, LANES), lambda i: (i, 0)),
            pl.BlockSpec((None, 8, LANES), lambda i: (i, 0, 0)),
            pl.BlockSpec((8, LANES), lambda i: (0, 0)),
        ],
        out_shape=[
            jax.ShapeDtypeStruct((M, K), BF16),
            jax.ShapeDtypeStruct((M, LANES), F32),
            jax.ShapeDtypeStruct((M // tm, 8, LANES), F32),
            jax.ShapeDtypeStruct((8, LANES), F32),
        ],
        scratch_shapes=[pltpu.VMEM((1, LANES), F32)],
        compiler_params=_params("arbitrary"),
        name="moe_router",
    )(x, gain.reshape(1, K), w)


def _permute_in_kernel(s_ref, t_ref, flag_ref, x_ref, dest_ref, zero_ref, o_ref, acc_ref):
    del t_ref, zero_ref
    k = pl.program_id(0)
    p = PERM_TILE

    @pl.when(flag_ref[3 * k] != 0)
    def _():
        row = (lax.broadcasted_iota(jnp.int32, (p, p), 0) + s_ref[k] * p).astype(F32)
        hit = (dest_ref[0:1, :] == row) | (dest_ref[1:2, :] == row)
        got = jnp.dot(jnp.where(hit, 1.0, 0.0).astype(BF16), x_ref[...], preferred_element_type=F32)

        @pl.when(flag_ref[3 * k + 1] != 0)
        def _():
            acc_ref[...] = got

        @pl.when(flag_ref[3 * k + 1] == 0)
        def _():
            acc_ref[...] += got

        @pl.when(flag_ref[3 * k + 2] != 0)
        def _():
            o_ref[...] = acc_ref[...].astype(o_ref.dtype)


def permute_in(xn, dest_rows, items, n_rows):
    M, K = xn.shape
    p = PERM_TILE
    s_item, t_item, flags = items
    grid_spec = pltpu.PrefetchScalarGridSpec(
        num_scalar_prefetch=3,
        grid=(s_item.shape[0],),
        in_specs=[
            pl.BlockSpec((p, K), lambda k, s, t, f: (t[k], 0)),
            pl.BlockSpec((None, 8, p), lambda k, s, t, f: (t[k], 0, 0)),
            pl.BlockSpec(memory_space=pl.ANY),
        ],
        out_specs=pl.BlockSpec((p, K), lambda k, s, t, f: (s[k], 0)),
        scratch_shapes=[pltpu.VMEM((p, K), F32)],
    )
    return pl.pallas_call(
        _permute_in_kernel,
        grid_spec=grid_spec,
        out_shape=jax.ShapeDtypeStruct((n_rows, K), BF16),
        input_output_aliases={5: 0},
        compiler_params=_params("arbitrary"),
        name="moe_permute_in",
    )(s_item, t_item, flags, xn, dest_rows, jnp.zeros((n_rows, K), BF16))


def _permute_out_kernel(s_ref, flag_ref, h_ref, y_ref, meta_ref, o_ref):
    ti = pl.program_id(0)
    c = pl.program_id(1)
    k = ti * pl.num_programs(1) + c
    p = PERM_TILE

    @pl.when(c == 0)
    def _():
        o_ref[...] = h_ref[...]

    @pl.when(flag_ref[k] != 0)
    def _():
        lo = (s_ref[k] * p).astype(F32)
        row = lax.broadcasted_iota(jnp.int32, (p, p), 1).astype(F32) + lo
        d1 = meta_ref[:, 0:1]
        d2 = meta_ref[:, 1:2]
        hit = (d1 == row) | (d2 == row)
        got = jnp.dot(jnp.where(hit, 1.0, 0.0).astype(BF16), y_ref[...], preferred_element_type=F32)
        gate = (jnp.where((d1 >= lo) & (d1 < lo + p), meta_ref[:, 2:3], 0.0)
                + jnp.where((d2 >= lo) & (d2 < lo + p), meta_ref[:, 3:4], 0.0))
        o_ref[...] += gate * got


def permute_out(h, y, dest_gate, items):
    M, K = h.shape
    p = PERM_TILE
    s_item, flags = items
    n_cand = s_item.shape[0] // (M // p)
    grid_spec = pltpu.PrefetchScalarGridSpec(
        num_scalar_prefetch=2,
        grid=(M // p, n_cand),
        in_specs=[
            pl.BlockSpec((p, K), lambda ti, c, s, f: (ti, 0)),
            pl.BlockSpec((p, K), lambda ti, c, s, f: (s[ti * n_cand + c], 0)),
            pl.BlockSpec((p, LANES), lambda ti, c, s, f: (ti, 0)),
        ],
        out_specs=pl.BlockSpec((p, K), lambda ti, c, s, f: (ti, 0)),
    )
    return pl.pallas_call(
        _permute_out_kernel,
        grid_spec=grid_spec,
        out_shape=jax.ShapeDtypeStruct((M, K), F32),
        compiler_params=_params("parallel", "arbitrary"),
        name="moe_permute_out",
    )(s_item, flags, h, y, dest_gate)


def _expert_ffn_kernel(te_ref, nu_ref, x_ref, wg_ref, wu_ref, wd_ref, o_ref, acc_ref):
    del te_ref
    i = pl.program_id(0)
    j = pl.program_id(1)

    @pl.when(i < nu_ref[0])
    def _():
        x = x_ref[...]
        gt = jnp.dot(x, wg_ref[...], preferred_element_type=F32)
        up = jnp.dot(x, wu_ref[...], preferred_element_type=F32)
        act = (gt / (1.0 + jnp.exp(-gt)) * up).astype(BF16)
        part = jnp.dot(act, wd_ref[...], preferred_element_type=F32)

        @pl.when(j == 0)
        def _():
            acc_ref[...] = part

        @pl.when(j > 0)
        def _():
            acc_ref[...] += part

        @pl.when(j == pl.num_programs(1) - 1)
        def _():
            o_ref[...] = acc_ref[...].astype(o_ref.dtype)


def expert_ffn(xs, tile_expert, n_used, w_gate_up, w_down, *, tm, tf):
    R, K = xs.shape
    E, F, _ = w_down.shape
    nf = F // tf
    used = lambda i, nu: jnp.minimum(i, nu[0] - 1)
    grid_spec = pltpu.PrefetchScalarGridSpec(
        num_scalar_prefetch=2,
        grid=(R // tm, nf),
        in_specs=[
            pl.BlockSpec((tm, K), lambda i, j, te, nu: (used(i, nu), 0)),
            pl.BlockSpec((None, K, tf), lambda i, j, te, nu: (te[used(i, nu)], 0, jnp.where(i < nu[0], j, nf - 1))),
            pl.BlockSpec((None, K, tf),
                         lambda i, j, te, nu: (te[used(i, nu)], 0, nf + jnp.where(i < nu[0], j, nf - 1))),
            pl.BlockSpec((None, tf, K), lambda i, j, te, nu: (te[used(i, nu)], jnp.where(i < nu[0], j, nf - 1), 0)),
        ],
        out_specs=pl.BlockSpec((tm, K), lambda i, j, te, nu: (used(i, nu), 0)),
        scratch_shapes=[pltpu.VMEM((tm, K), F32)],
    )
    return pl.pallas_call(
        _expert_ffn_kernel,
        grid_spec=grid_spec,
        out_shape=jax.ShapeDtypeStruct((R, K), BF16),
        compiler_params=_params("arbitrary", "arbitrary"),
        name="moe_expert_ffn",
    )(tile_expert, n_used, xs, w_gate_up, w_gate_up, w_down)


def _prefix_max(x, fill, reverse=False):
    n = x.shape[0]
    j = jnp.arange(n)
    seen = (j[None, :] >= j[:, None]) if reverse else (j[None, :] <= j[:, None])
    return jnp.max(jnp.where(seen, x[None, :], fill), axis=1)


def _fill_invalid(valid, *arrays):
    n = valid.shape[0]
    j = jnp.arange(n)
    last = _prefix_max(jnp.where(valid, j, -1), -1)
    last = jnp.where(last < 0, jnp.min(jnp.where(valid, j, n)), last)
    pick = last[:, None] == j[None, :]
    return [jnp.sum(jnp.where(pick, a[None, :], 0), axis=1) for a in arrays]


def moe_residual(h, gain, w_router, w_gate_up, w_down):
    M, K = h.shape
    p = PERM_TILE
    tm = EXPERT_TILE
    ntt = M // p
    n_rows = 2 * M + N_EXPERTS * tm
    xn, meta, base, cnt = moe_router(h, gain, w_router, tm=p)

    idx = meta[:, META_IDX:META_IDX + 2].astype(jnp.int32)
    rank = meta[:, META_RANK:META_RANK + 2].astype(jnp.int32)
    counts = cnt[0, :N_EXPERTS].astype(jnp.int32)
    padded = (counts + tm - 1) // tm * tm
    ends = jnp.cumsum(padded)
    offsets = ends - padded
    n_used = (ends[-1] // tm).reshape(1)
    tile_row = jnp.arange(n_rows // tm) * tm
    tile_expert = jnp.minimum(jnp.sum(ends[None, :] <= tile_row[:, None], axis=1), N_EXPERTS - 1).astype(jnp.int32)
    offset_of = jnp.sum(jnp.where(idx[:, :, None] == jnp.arange(N_EXPERTS), offsets, 0), axis=-1)
    dest = (offset_of + rank).astype(F32)
    dest_rows = jnp.zeros((ntt, 8, p), F32).at[:, 0:2, :].set(dest.reshape(ntt, p, 2).transpose(0, 2, 1))
    dest_gate = jnp.zeros((M, LANES), F32).at[:, 0:2].set(dest).at[:, 2:4].set(meta[:, META_GATE:META_GATE + 2])

    start = base[:, 0, :N_EXPERTS].astype(jnp.int32)
    stop = jnp.concatenate([start[1:], counts[None, :]], axis=0)
    first = (offsets[None, :] + start) // p
    last = (offsets[None, :] + stop - 1) // p
    nonempty = stop > start
    s_cand = jnp.stack([first, first + 1], axis=-1)
    v_cand = jnp.stack([nonempty, nonempty & (last > first)], axis=-1)
    t_cand = jnp.broadcast_to(jnp.arange(ntt)[:, None, None], s_cand.shape)

    s_in = s_cand.transpose(1, 0, 2).reshape(-1)
    t_in = t_cand.transpose(1, 0, 2).reshape(-1)
    v_in = v_cand.transpose(1, 0, 2).reshape(-1)
    s_in, t_in = _fill_invalid(v_in, s_in, t_in)
    before = jnp.concatenate([jnp.full((1,), -1, s_in.dtype), _prefix_max(jnp.where(v_in, s_in, -1), -1)[:-1]])
    big = n_rows // p
    after = jnp.concatenate([-_prefix_max(jnp.where(v_in, -s_in, -big), -big, reverse=True)[1:],
                             jnp.full((1,), big, s_in.dtype)])
    flags_in = jnp.stack([v_in, v_in & (before != s_in), v_in & (after != s_in)], axis=-1)
    xs = permute_in(xn, dest_rows, (s_in.astype(jnp.int32), t_in.astype(jnp.int32),
                                    flags_in.astype(jnp.int32).reshape(-1)), n_rows)

    ys = expert_ffn(xs, tile_expert, n_used.astype(jnp.int32), w_gate_up, w_down, tm=tm, tf=512)

    s_out = s_cand.reshape(-1)
    v_out = v_cand.reshape(-1)
    (s_out,) = _fill_invalid(v_out, s_out)
    return permute_out(h, ys, dest_gate, (s_out.astype(jnp.int32), v_out.astype(jnp.int32)))


def _ffn_kernel(x_ref, g_ref, wg_ref, wu_ref, wd_ref, o_ref, xn_ref, acc_ref):
    j = pl.program_id(1)

    @pl.when(j == 0)
    def _():
        x = x_ref[...]
        xn_ref[...] = _rms(x, g_ref[...]).astype(BF16)
        acc_ref[...] = x

    xn = xn_ref[...]
    gt = jnp.dot(xn, wg_ref[...], preferred_element_type=F32)
    up = jnp.dot(xn, wu_ref[...], preferred_element_type=F32)
    act = gt / (1.0 + jnp.exp(-gt)) * up
    acc_ref[...] += jnp.dot(act.astype(BF16), wd_ref[...], preferred_element_type=F32)

    @pl.when(j == pl.num_programs(1) - 1)
    def _():
        o_ref[...] = acc_ref[...]


def ffn_residual(x, gain, w_gate_up, w_down, *, tm, tf):
    M, K = x.shape
    F = w_down.shape[0]
    nf = F // tf
    return pl.pallas_call(
        _ffn_kernel,
        grid=(M // tm, nf),
        in_specs=[
            pl.BlockSpec((tm, K), lambda i, j: (i, 0)),
            pl.BlockSpec((1, K), lambda i, j: (0, 0)),
            pl.BlockSpec((K, tf), lambda i, j: (0, j)),
            pl.BlockSpec((K, tf), lambda i, j: (0, nf + j)),
            pl.BlockSpec((tf, K), lambda i, j: (j, 0)),
        ],
        out_specs=pl.BlockSpec((tm, K), lambda i, j: (i, 0)),
        out_shape=jax.ShapeDtypeStruct((M, K), F32),
        scratch_shapes=[pltpu.VMEM((tm, K), BF16), pltpu.VMEM((tm, K), F32)],
        compiler_params=_params("parallel", "arbitrary"),
        name="ffn_residual",
    )(x, gain.reshape(1, K), w_gate_up, w_gate_up, w_down)


def _even_mix(h, B, S, norm1, w_in, f_bias, q_norm, k_norm, conv_w, a_log, dt_bias, o_norm, w_out):
    M, D = h.shape
    fw, gw = FOX_HEADS * HEAD_DIM, GDN_HEADS * HEAD_DIM
    o_ff = 3 * fw
    o_gq = o_ff + FOX_HEADS
    o_ga = o_gq + 3 * gw
    o_gb = o_ga + GDN_HEADS
    o_gg = o_gb + GDN_HEADS
    w_big = jnp.concatenate([w_in[:, :o_ff], w_in[:, o_gq:o_ga], w_in[:, o_gg:]], axis=1).astype(BF16)
    w_small = jnp.concatenate([w_in[:, o_ff:o_gq], w_in[:, o_ga:o_gg],
                               jnp.zeros((D, LANES - FOX_HEADS - 2 * GDN_HEADS), F32)], axis=1).astype(BF16)
    z, zs = norm_matmul(h, norm1, w_big, w_small, tm=ROW_TILE, tn=896)
    par = jnp.zeros((8, LANES), F32)
    par = par.at[0, LANE_F:LANE_F + FOX_HEADS].set(f_bias).at[0, LANE_A:LANE_A + GDN_HEADS].set(dt_bias)
    par = par.at[1, LANE_A:LANE_A + GDN_HEADS].set(a_log)
    col, row = even_gates(zs.reshape(B, S, LANES), par)
    row = row.reshape(B, 16, 1, S)
    z = z.reshape(B, S, -1)
    fox = fox_attention(z, col, q_norm, k_norm)
    gdn = gdn_mixer(z, conv_w, col, row, o_norm)
    w_out = w_out.astype(BF16)
    return matmul_residual(h, [(fox.reshape(M, fw), w_out[:fw]), (gdn.reshape(M, gw), w_out[fw:])], tm=ROW_TILE, tn=1024)


def _odd_mix(h, B, S, norm1, w_qkv, q_norm, k_norm, w_out):
    M, D = h.shape
    z = norm_matmul(h, norm1, w_qkv.astype(BF16), tm=ROW_TILE, tn=1024).reshape(B, S, -1)
    half = HEAD_DIM // 2
    inv = jnp.power(ROPE_THETA, -jnp.arange(half, dtype=F32) / half)
    ang = jnp.arange(S, dtype=F32)[:, None] * inv[None, :]
    cos, sin = jnp.cos(ang), jnp.sin(ang)
    cos_full = jnp.concatenate([cos, cos], axis=-1)
    sin_signed = jnp.concatenate([-sin, sin], axis=-1)
    att = moba_attention(z, cos_full, sin_signed, q_norm, k_norm)
    return matmul_residual(h, [(att.reshape(M, -1), w_out.astype(BF16))], tm=ROW_TILE, tn=1024)


def _odd_moe(h, norm2, w_router, w_gate_up, w_down):
    return moe_residual(h, norm2, w_router, w_gate_up.astype(BF16), w_down.astype(BF16))


def kernel(x, e_norm1, e_w_in, e_fox_f_bias, e_fox_q_norm, e_fox_k_norm, e_gdn_conv, e_gdn_a_log,
           e_gdn_dt_bias, e_gdn_o_norm, e_w_out, e_norm2, e_ffn_w_gate_up, e_ffn_w_down,
           o_norm1, o_w_qkv, o_q_norm, o_k_norm, o_w_out, o_norm2, o_router, o_exp_w_gate_up, o_exp_w_down):
    B, S, D = x.shape
    h = x.reshape(B * S, D)
    depth = e_norm1.shape[0] + o_norm1.shape[0]
    for layer in range(depth):
        i = layer // 2
        if layer % 2 == 0:
            h = _even_mix(h, B, S, e_norm1[i], e_w_in[i], e_fox_f_bias[i], e_fox_q_norm[i], e_fox_k_norm[i],
                          e_gdn_conv[i], e_gdn_a_log[i], e_gdn_dt_bias[i], e_gdn_o_norm[i], e_w_out[i])
            h = ffn_residual(h, e_norm2[i], e_ffn_w_gate_up[i].astype(BF16), e_ffn_w_down[i].astype(BF16),
                             tm=ROW_TILE, tf=256)
        else:
            h = _odd_mix(h, B, S, o_norm1[i], o_w_qkv[i], o_q_norm[i], o_k_norm[i], o_w_out[i])
            h = _odd_moe(h, o_norm2[i], o_router[i], o_exp_w_gate_up[i], o_exp_w_down[i])
    return h.reshape(B, S, D)
```

```python
import functools

import jax
import jax.numpy as jnp
from jax import lax
from jax.experimental import pallas as pl
from jax.experimental.pallas import tpu as pltpu

F32 = jnp.float32
BF16 = jnp.bfloat16

HEAD_DIM = 128
FOX_HEADS = 4
GDN_HEADS = 4
CONV_WIDTH = 4
MOBA_BLOCK = 256
MOBA_TOPK = 3
N_EXPERTS = 8
ROPE_THETA = 10000.0
EPS = 1e-6

LANES = 128
GDN_CHUNK = 128
INV_BLOCK = 16
GDN_HEADS_PER_STEP = 4
NEG = -(2.0 ** 100)
LOG2E = 1.4426950408889634
ATT_BLOCK = 256
V_ROWS = HEAD_DIM + 16
PERM_TILE = 512
ROW_TILE = 1024
EXPERT_TILE = 1024
VMEM_LIMIT_BYTES = 56 * 1024 * 1024

FOX_Q0, FOX_K0, FOX_V0 = 0, 4, 8
GDN_Q0, GDN_K0, GDN_V0, GDN_G0 = 12, 16, 20, 24
LANE_F, LANE_A, LANE_B = 0, 4, 8


def _params(*sem):
    return pltpu.CompilerParams(dimension_semantics=sem, vmem_limit_bytes=VMEM_LIMIT_BYTES)


def _rms(x, gain):
    return x * lax.rsqrt(jnp.mean(x * x, axis=-1, keepdims=True) + EPS) * gain


def _dot_nt(a, b, **kw):
    return lax.dot_general(a, b, (((1,), (1,)), ((), ())), preferred_element_type=F32, **kw)


def _pick_lane(x, lane_idx):
    lane = lax.broadcasted_iota(jnp.int32, x.shape, 1)
    return jnp.sum(jnp.where(lane == lane_idx, x, 0.0), axis=-1, keepdims=True)


def _norm_mm_kernel(x_ref, g_ref, w_ref, *rest, has_aux):
    if has_aux:
        waux_ref, o_ref, oaux_ref, xn_ref = rest
    else:
        o_ref, xn_ref = rest

    @pl.when(pl.program_id(1) == 0)
    def _():
        xn = _rms(x_ref[...], g_ref[...]).astype(BF16)
        xn_ref[...] = xn
        if has_aux:
            oaux_ref[...] = jnp.dot(xn, waux_ref[...], preferred_element_type=F32)

    o_ref[...] = jnp.dot(xn_ref[...], w_ref[...], preferred_element_type=F32).astype(o_ref.dtype)


def norm_matmul(x, gain, w, w_aux=None, *, tm, tn, out_dtype=BF16):
    M, K = x.shape
    N = w.shape[1]
    has_aux = w_aux is not None
    in_specs = [
        pl.BlockSpec((tm, K), lambda i, j: (i, 0)),
        pl.BlockSpec((1, K), lambda i, j: (0, 0)),
        pl.BlockSpec((K, tn), lambda i, j: (0, j)),
    ]
    out_shape = [jax.ShapeDtypeStruct((M, N), out_dtype)]
    out_specs = [pl.BlockSpec((tm, tn), lambda i, j: (i, j))]
    args = [x, gain.reshape(1, K), w]
    if has_aux:
        in_specs.append(pl.BlockSpec((K, LANES), lambda i, j: (0, 0)))
        out_shape.append(jax.ShapeDtypeStruct((M, LANES), F32))
        out_specs.append(pl.BlockSpec((tm, LANES), lambda i, j: (i, 0)))
        args.append(w_aux)
    res = pl.pallas_call(
        functools.partial(_norm_mm_kernel, has_aux=has_aux),
        grid=(M // tm, N // tn),
        in_specs=in_specs,
        out_specs=out_specs,
        out_shape=out_shape,
        scratch_shapes=[pltpu.VMEM((tm, K), BF16)],
        compiler_params=_params("parallel", "arbitrary"),
        name="norm_matmul",
    )(*args)
    return res if has_aux else res[0]


def _mm_res_kernel(*refs, n_in):
    res_ref = refs[0]
    o_ref = refs[1 + 2 * n_in]
    acc = res_ref[...]
    for t in range(n_in):
        acc = acc + jnp.dot(refs[1 + 2 * t][...], refs[2 + 2 * t][...], preferred_element_type=F32)
    o_ref[...] = acc


def matmul_residual(res, pairs, *, tm, tn):
    M, N = res.shape
    in_specs = [pl.BlockSpec((tm, tn), lambda i, j: (i, j))]
    args = [res]
    for a, w in pairs:
        K = a.shape[1]
        in_specs.append(pl.BlockSpec((tm, K), lambda i, j: (i, 0)))
        in_specs.append(pl.BlockSpec((K, tn), lambda i, j: (0, j)))
        args += [a, w]
    return pl.pallas_call(
        functools.partial(_mm_res_kernel, n_in=len(pairs)),
        grid=(M // tm, N // tn),
        in_specs=in_specs,
        out_specs=pl.BlockSpec((tm, tn), lambda i, j: (i, j)),
        out_shape=jax.ShapeDtypeStruct((M, N), F32),
        compiler_params=_params("parallel", "arbitrary"),
        name="matmul_residual",
    )(*args)


def _gate_kernel(zs_ref, par_ref, col_ref, row_ref, *, S):
    C = GDN_CHUNK
    bias = par_ref[0:1, :]
    neg_a = -jnp.exp(par_ref[1:2, :])
    r = lax.broadcasted_iota(jnp.int32, (C, C), 0)
    c = lax.broadcasted_iota(jnp.int32, (C, C), 1)
    tril = (r >= c).astype(F32)
    lane = lax.broadcasted_iota(jnp.int32, (C, LANES), 1)

    def body(n, carry):
        sl = pl.ds(pl.multiple_of(n * C, C), C)
        z = zs_ref[sl, :]
        t = z + bias
        soft = jnp.log(1.0 + jnp.exp(-jnp.abs(t)))
        log_f = jnp.minimum(t, 0.0) - soft
        g = neg_a * (jnp.maximum(t, 0.0) + soft)
        beta = 1.0 / (1.0 + jnp.exp(-z))
        u = jnp.where(lane < LANE_A, log_f, jnp.where(lane < LANE_B, g, 0.0))
        cs = jnp.dot(tril, u, preferred_element_type=F32, precision=lax.Precision.HIGHEST)
        cs = cs + jnp.where(lane < LANE_A, carry, 0.0)
        out = jnp.where(lane < LANE_B, cs, beta)
        col_ref[sl, :] = out
        row_ref[:, sl] = out.T[0:16, :]
        return cs[C - 1:C, :]

    lax.fori_loop(0, S // C, body, jnp.zeros((1, LANES), F32))


def even_gates(zs, par):
    B, S, _ = zs.shape
    return pl.pallas_call(
        functools.partial(_gate_kernel, S=S),
        grid=(B,),
        in_specs=[
            pl.BlockSpec((None, S, LANES), lambda b: (b, 0, 0)),
            pl.BlockSpec((8, LANES), lambda b: (0, 0)),
        ],
        out_specs=[
            pl.BlockSpec((None, S, LANES), lambda b: (b, 0, 0)),
            pl.BlockSpec((None, 16, S), lambda b: (b, 0, 0)),
        ],
        out_shape=[jax.ShapeDtypeStruct((B, S, LANES), F32), jax.ShapeDtypeStruct((B, 16, S), F32)],
        compiler_params=_params("parallel"),
        name="even_gates",
    )(zs, par)


def _split3(x):
    hi = x.astype(BF16).astype(F32)
    mid = (x - hi).astype(BF16).astype(F32)
    lo = (x - hi - mid).astype(BF16).astype(F32)
    return hi, mid, lo


def _interleave(gens):
    out = [None] * len(gens)
    live = list(range(len(gens)))
    while live:
        for n in list(live):
            try:
                next(gens[n])
            except StopIteration as stop:
                out[n] = stop.value
                live.remove(n)
    return out


def _attend(state, c, qa, ka_ref, vt_ref, key0, nkeys, keep=None):
    ks = pl.ds(pl.multiple_of(key0, ATT_BLOCK), nkeys)
    st = _dot_nt(ka_ref[ks, :], qa)
    yield
    if keep is not None:
        st = jnp.where(keep, st, NEG)
    m_new = jnp.max(st, axis=0, keepdims=True)
    if state[c] is not None:
        m_old, acc_old = state[c]
        m_new = jnp.maximum(m_old, m_new)
    p = jnp.exp2(st - m_new).astype(BF16)
    pv = jnp.dot(vt_ref[:, ks], p, preferred_element_type=F32)
    state[c] = (m_new, pv if state[c] is None else acc_old * jnp.exp2(m_old - m_new) + pv)


def _attend_tile_pair(i, qa_diag, qa_past, ka_ref, vt_ref, m_ref, acc_ref, o_ref):
    t = ATT_BLOCK
    step = functools.partial(_attend, ka_ref=ka_ref, vt_ref=vt_ref)

    def save(state):
        m_ref[...] = jnp.stack([state[0][0], state[1][0]])
        acc_ref[...] = jnp.stack([state[0][1], state[1][1]])

    def past_blocks(key0, n_steps):
        state = {c: (m_ref[c], acc_ref[c]) for c in range(2)}
        _interleave([step(state, c, qa_past[c], key0=key0 + s * 2 * t, nkeys=2 * t)
                     for s in range(n_steps) for c in range(2)])
        save(state)

    state = {0: None, 1: None}
    keep = _causal_keep()
    _interleave([step(state, 0, qa_diag[0], key0=(2 * i) * t, nkeys=t, keep=keep),
                 step(state, 1, qa_diag[1], key0=(2 * i + 1) * t, nkeys=t, keep=keep),
                 step(state, 1, qa_past[1], key0=(2 * i) * t, nkeys=t)])
    save(state)

    def four_blocks(g, _):
        past_blocks(g * (4 * t), 2)
        return 0

    lax.fori_loop(0, i // 2, four_blocks, 0)

    @pl.when((i & 1) != 0)
    def _():
        past_blocks((i // 2) * (4 * t), 1)

    for c in range(2):
        acc = acc_ref[c]
        out_t = acc[:HEAD_DIM, :] * (1.0 / acc[HEAD_DIM:HEAD_DIM + 1, :])
        o_ref[c * t:(c + 1) * t, :] = out_t.T.astype(o_ref.dtype)


def _store_vt(vt_ref, v_ref, sl):
    vt_ref[0:HEAD_DIM, sl] = v_ref[sl, :].astype(F32).T.astype(BF16)
    vt_ref[HEAD_DIM:V_ROWS, sl] = jnp.ones((V_ROWS - HEAD_DIM, ATT_BLOCK), BF16)


def _causal_keep():
    key = lax.broadcasted_iota(jnp.int32, (ATT_BLOCK, ATT_BLOCK), 0)
    qry = lax.broadcasted_iota(jnp.int32, (ATT_BLOCK, ATT_BLOCK), 1)
    return key <= qry


_ATT_SCRATCH = lambda S: [pltpu.VMEM((S, 2 * HEAD_DIM), BF16), pltpu.VMEM((V_ROWS, S), BF16),
                          pltpu.VMEM((2, 1, ATT_BLOCK), F32), pltpu.VMEM((2, V_ROWS, ATT_BLOCK), F32)]


def _fox_kernel(q_ref, k_ref, v_ref, col_ref, qg_ref, kg_ref, o_ref, ka_ref, vt_ref, m_ref, acc_ref, *, S):
    h = pl.program_id(1)
    i = pl.program_id(2)
    t = ATT_BLOCK
    lane = lax.broadcasted_iota(jnp.int32, (t, LANES), 1)

    @pl.when(i == 0)
    def _():
        def prep_keys(n, _):
            sl = pl.ds(pl.multiple_of(n * t, t), t)
            ka_ref[sl, 0:HEAD_DIM] = _rms(k_ref[sl, :].astype(F32), kg_ref[...]).astype(BF16)
            hi, mid, lo = _split3(-LOG2E * _pick_lane(col_ref[sl, :], LANE_F + h))
            ka_ref[sl, HEAD_DIM:] = jnp.where(
                lane < 3, 1.0, jnp.where(lane == 3, hi, jnp.where(lane == 4, mid, jnp.where(lane == 5, lo, 0.0)))
            ).astype(BF16)
            _store_vt(vt_ref, v_ref, sl)
            return 0

        lax.fori_loop(0, S // t, prep_keys, 0)

    qa = []
    for c in range(2):
        qn = (_rms(q_ref[c * t:(c + 1) * t, :].astype(F32), qg_ref[...]) * (LOG2E * HEAD_DIM ** -0.5)).astype(BF16)
        qsl = pl.ds(pl.multiple_of((2 * i + c) * t, t), t)
        hi, mid, lo = _split3(LOG2E * _pick_lane(col_ref[qsl, :], LANE_F + h))
        aug = jnp.where(lane == 0, hi,
                        jnp.where(lane == 1, mid, jnp.where(lane == 2, lo, jnp.where(lane < 6, 1.0, 0.0))))
        qa.append(jnp.concatenate([qn, aug.astype(BF16)], axis=-1))
    _attend_tile_pair(i, qa, qa, ka_ref, vt_ref, m_ref, acc_ref, o_ref)


def fox_attention(z, col, q_gain, k_gain):
    B, S, _ = z.shape
    t = 2 * ATT_BLOCK
    return pl.pallas_call(
        functools.partial(_fox_kernel, S=S),
        grid=(B, FOX_HEADS, S // t),
        in_specs=[
            pl.BlockSpec((None, t, HEAD_DIM), lambda b, h, i: (b, i, FOX_Q0 + h)),
            pl.BlockSpec((None, S, HEAD_DIM), lambda b, h, i: (b, 0, FOX_K0 + h)),
            pl.BlockSpec((None, S, HEAD_DIM), lambda b, h, i: (b, 0, FOX_V0 + h)),
            pl.BlockSpec((None, S, LANES), lambda b, h, i: (b, 0, 0)),
            pl.BlockSpec((1, HEAD_DIM), lambda b, h, i: (0, 0)),
            pl.BlockSpec((1, HEAD_DIM), lambda b, h, i: (0, 0)),
        ],
        out_specs=pl.BlockSpec((None, t, HEAD_DIM), lambda b, h, i: (b, i, h)),
        out_shape=jax.ShapeDtypeStruct((B, S, FOX_HEADS * HEAD_DIM), BF16),
        scratch_shapes=_ATT_SCRATCH(S),
        compiler_params=_params("parallel", "parallel", "arbitrary"),
        name="fox_attention",
    )(z, z, z, col, q_gain.reshape(1, -1), k_gain.reshape(1, -1))


def _unit_lower_inverse(m):
    n = m.shape[0]
    r = lax.broadcasted_iota(jnp.int32, (n, n), 0)
    c = lax.broadcasted_iota(jnp.int32, (n, n), 1)
    eye = (r == c).astype(F32)

    def same_block(b):
        return (r // b) == (c // b)

    p = jnp.where(same_block(INV_BLOCK), m, 0.0)
    inv = eye - p
    k = 2
    while k < INV_BLOCK:
        pb = p.astype(BF16)
        p = jnp.dot(pb, pb, preferred_element_type=F32)
        yield
        inv = jnp.dot(inv.astype(BF16), (eye + p).astype(BF16), preferred_element_type=F32)
        yield
        k *= 2
    b = INV_BLOCK
    while b < n:
        off = jnp.where(same_block(2 * b), jnp.where(same_block(b), 0.0, m), 0.0).astype(BF16)
        ib = inv.astype(BF16)
        left = jnp.dot(ib, off, preferred_element_type=F32).astype(BF16)
        yield
        inv = inv - jnp.dot(left, ib, preferred_element_type=F32)
        yield
        b *= 2
    return inv


def _gdn_kernel(q_ref, k_ref, v_ref, gg_ref, wq_ref, wk_ref, wv_ref, col_ref, row_ref, on_ref,
                o_ref, pad_ref, qs_ref, ks_ref, vs_ref, *, S, rows, hb):
    h0 = pl.program_id(1) * hb
    C = GDN_CHUNK
    D = HEAD_DIM

    def conv_silu(x_ref, w_ref, dst_ref, hh, mode):
        cols = slice(hh * D, (hh + 1) * D)
        pad_ref[0:8, :] = jnp.zeros((8, D), F32)

        def fill(n, _):
            src = pl.ds(pl.multiple_of(n * rows, rows), rows)
            pad_ref[pl.ds(pl.multiple_of(8 + n * rows, 8), rows), :] = x_ref[src, cols].astype(F32)
            return 0

        lax.fori_loop(0, S // rows, fill, 0)
        w = w_ref[:, cols]

        def conv(n, _):
            base = pl.multiple_of(n * rows, rows)
            win = pad_ref[pl.ds(base, rows + 8), :]
            y = jnp.zeros((rows, D), F32)
            for tap in range(CONV_WIDTH):
                lead = 8 - (CONV_WIDTH - 1) + tap
                y = y + w[tap:tap + 1, :] * pltpu.roll(win, rows + 8 - lead, 0)[0:rows, :]
            y = y / (1.0 + jnp.exp(-y))
            if mode != "v":
                y = y * lax.rsqrt(jnp.sum(y * y, axis=-1, keepdims=True) + EPS)
            if mode == "q":
                y = y * D ** -0.5
            dst_ref[hh, pl.ds(base, rows), :] = y.astype(dst_ref.dtype)
            return 0

        lax.fori_loop(0, S // rows, conv, 0)

    for hh in range(hb):
        conv_silu(q_ref, wq_ref, qs_ref, hh, "q")
        conv_silu(k_ref, wk_ref, ks_ref, hh, "k")
        conv_silu(v_ref, wv_ref, vs_ref, hh, "v")

    r = lax.broadcasted_iota(jnp.int32, (C, C), 0)
    c = lax.broadcasted_iota(jnp.int32, (C, C), 1)
    incl = r >= c
    strict = r > c

    def head_chunk(hh, sl, tab, state):
        q = qs_ref[hh, sl, :].astype(F32)
        k = ks_ref[hh, sl, :].astype(F32)
        v = vs_ref[hh, sl, :].astype(F32)
        gcol = _pick_lane(tab, LANE_A + h0 + hh)
        beta = _pick_lane(tab, LANE_B + h0 + hh)
        grow = row_ref[hh, :, sl]
        glast = gcol[C - 1:C, :]
        decay = jnp.where(incl, jnp.exp(jnp.where(incl, gcol - grow, 0.0)), 0.0)
        eg = jnp.exp(gcol)
        kb = k * beta
        kbf = k.astype(BF16)
        m = jnp.where(strict, _dot_nt(kb.astype(BF16), kbf) * decay, 0.0)
        attn = _dot_nt(q.astype(BF16), kbf) * decay
        yield
        tinv = (yield from _unit_lower_inverse(m)).astype(BF16)
        rhs = jnp.concatenate([v * beta, kb * eg], axis=-1).astype(BF16)
        sol = jnp.dot(tinv, rhs, preferred_element_type=F32)
        yield
        u = sol[:, :D]
        w = sol[:, D:]
        qg = q * eg
        kg = k * jnp.exp(glast - gcol)
        sb = state.astype(BF16)
        v_new = u - jnp.dot(w.astype(BF16), sb, preferred_element_type=F32)
        o_state = jnp.dot(qg.astype(BF16), sb, preferred_element_type=F32)
        yield
        vb = v_new.astype(BF16)
        o = o_state + jnp.dot(attn.astype(BF16), vb, preferred_element_type=F32)
        state = state * jnp.exp(glast) + jnp.dot(kg.T.astype(BF16), vb, preferred_element_type=F32)
        yield
        gate = gg_ref[sl, hh * D:(hh + 1) * D].astype(F32)
        return state, (_rms(o, on_ref[...]) * (gate / (1.0 + jnp.exp(-gate)))).astype(o_ref.dtype)

    def chunk(n, states):
        sl = pl.ds(pl.multiple_of(n * C, C), C)
        tab = col_ref[sl, :]
        res = _interleave([head_chunk(hh, sl, tab, states[hh]) for hh in range(hb)])
        o_ref[sl, :] = jnp.concatenate([o for _, o in res], axis=-1)
        return tuple(s for s, _ in res)

    lax.fori_loop(0, S // C, chunk, tuple(jnp.zeros((D, D), F32) for _ in range(hb)))


def gdn_mixer(z, conv_w, col, row, o_gain):
    B, S, _ = z.shape
    D = HEAD_DIM
    hb = GDN_HEADS_PER_STEP
    rows = 256
    seq = lambda off: pl.BlockSpec((None, S, hb * D), lambda b, h: (b, 0, off // hb + h),
                                   pipeline_mode=pl.Buffered(1))
    cw = lambda off: pl.BlockSpec((CONV_WIDTH, hb * D), lambda b, h: (0, off // hb + h))
    return pl.pallas_call(
        functools.partial(_gdn_kernel, S=S, rows=rows, hb=hb),
        grid=(B, GDN_HEADS // hb),
        in_specs=[
            seq(GDN_Q0), seq(GDN_K0), seq(GDN_V0), seq(GDN_G0),
            cw(0), cw(GDN_HEADS), cw(2 * GDN_HEADS),
            pl.BlockSpec((None, S, LANES), lambda b, h: (b, 0, 0)),
            pl.BlockSpec((None, hb, 1, S), lambda b, h: (b, LANE_A // hb + h, 0, 0)),
            pl.BlockSpec((1, D), lambda b, h: (0, 0)),
        ],
        out_specs=pl.BlockSpec((None, S, hb * D), lambda b, h: (b, 0, h)),
        out_shape=jax.ShapeDtypeStruct((B, S, GDN_HEADS * D), BF16),
        scratch_shapes=[pltpu.VMEM((S + 8, D), F32)] + [pltpu.VMEM((hb, S, D), BF16)] * 3,
        compiler_params=_params("parallel", "parallel"),
        name="gdn_mixer",
    )(z, z, z, z, conv_w, conv_w, conv_w, col, row, o_gain.reshape(1, D))


def _rope(x, cos, sin_signed):
    return x * cos + pltpu.roll(x, HEAD_DIM // 2, 1) * sin_signed


def _moba_kernel(q_ref, k_ref, v_ref, cos_ref, sin_ref, qg_ref, kg_ref, o_ref,
                 ka_ref, vt_ref, m_ref, acc_ref, kmean_ref, *, S):
    i = pl.program_id(2)
    t = ATT_BLOCK
    lane = lax.broadcasted_iota(jnp.int32, (t, LANES), 1)

    @pl.when(i == 0)
    def _():
        kmean_ref[...] = jnp.zeros(kmean_ref.shape, F32)

        def prep_keys(n, _):
            sl = pl.ds(pl.multiple_of(n * t, t), t)
            k = _rope(_rms(k_ref[sl, :].astype(F32), kg_ref[...]), cos_ref[sl, :], sin_ref[sl, :])
            ka_ref[sl, 0:HEAD_DIM] = k.astype(BF16)
            ka_ref[sl, HEAD_DIM:] = jnp.where(lane == n, 1.0, 0.0).astype(BF16)
            kmean_ref[pl.ds(n, 1), :] = jnp.mean(k, axis=0, keepdims=True)
            _store_vt(vt_ref, v_ref, sl)
            return 0

        lax.fori_loop(0, S // t, prep_keys, 0)

    lane_f = lane.astype(F32)
    kmean = kmean_ref[...]
    km_hi = kmean.astype(BF16)
    km_split = jnp.concatenate([km_hi, (kmean - km_hi.astype(F32)).astype(BF16)], axis=0)
    qa_diag, qa_past = [], []
    for c in range(2):
        cur = 2 * i + c
        qsl = pl.ds(pl.multiple_of(cur * t, t), t)
        q = _rope(_rms(q_ref[c * t:(c + 1) * t, :].astype(F32), qg_ref[...]), cos_ref[qsl, :], sin_ref[qsl, :])
        q_hi = q.astype(BF16)
        q_lo = (q - q_hi.astype(F32)).astype(BF16)
        part = _dot_nt(q_hi, km_split)
        gate = part[:, :LANES] + part[:, LANES:] + _dot_nt(q_lo, km_split[:LANES, :])
        gate = jnp.where(lane < cur, gate, -jnp.inf)
        sel_bias = jnp.full((t, LANES), NEG, F32)
        for _ in range(MOBA_TOPK):
            top = jnp.max(gate, axis=-1, keepdims=True)
            first = jnp.min(jnp.where(gate == top, lane_f, float(LANES)), axis=-1, keepdims=True)
            pick = lane_f == first
            sel_bias = jnp.where(pick & (first < cur.astype(F32)), 0.0, sel_bias)
            gate = jnp.where(pick, -jnp.inf, gate)
        qs = (q * (LOG2E * HEAD_DIM ** -0.5)).astype(BF16)
        qa_diag.append(jnp.concatenate([qs, jnp.zeros((t, LANES), BF16)], axis=-1))
        qa_past.append(jnp.concatenate([qs, sel_bias.astype(BF16)], axis=-1))
    _attend_tile_pair(i, qa_diag, qa_past, ka_ref, vt_ref, m_ref, acc_ref, o_ref)


def moba_attention(z, cos, sin_signed, q_gain, k_gain):
    B, S, W = z.shape
    H = W // (3 * HEAD_DIM)
    assert ATT_BLOCK == MOBA_BLOCK
    t = 2 * ATT_BLOCK
    return pl.pallas_call(
        functools.partial(_moba_kernel, S=S),
        grid=(B, H, S // t),
        in_specs=[
            pl.BlockSpec((None, t, HEAD_DIM), lambda b, h, i: (b, i, h)),
            pl.BlockSpec((None, S, HEAD_DIM), lambda b, h, i: (b, 0, H + h)),
            pl.BlockSpec((None, S, HEAD_DIM), lambda b, h, i: (b, 0, 2 * H + h)),
            pl.BlockSpec((S, HEAD_DIM), lambda b, h, i: (0, 0)),
            pl.BlockSpec((S, HEAD_DIM), lambda b, h, i: (0, 0)),
            pl.BlockSpec((1, HEAD_DIM), lambda b, h, i: (0, 0)),
            pl.BlockSpec((1, HEAD_DIM), lambda b, h, i: (0, 0)),
        ],
        out_specs=pl.BlockSpec((None, t, HEAD_DIM), lambda b, h, i: (b, i, h)),
        out_shape=jax.ShapeDtypeStruct((B, S, H * HEAD_DIM), BF16),
        scratch_shapes=_ATT_SCRATCH(S) + [pltpu.VMEM((LANES, HEAD_DIM), F32)],
        compiler_params=_params("parallel", "parallel", "arbitrary"),
        name="moba_attention",
    )(z, z, z, cos, sin_signed, q_gain.reshape(1, -1), k_gain.reshape(1, -1))


META_IDX, META_RANK, META_GATE = 0, 2, 4


def _router_kernel(x_ref, g_ref, w_ref, xn_ref, meta_ref, base_ref, cnt_ref, carry_ref):
    i = pl.program_id(0)
    tm = x_ref.shape[0]

    @pl.when(i == 0)
    def _():
        carry_ref[...] = jnp.zeros(carry_ref.shape, F32)

    xn = _rms(x_ref[...], g_ref[...])
    xn_ref[...] = xn.astype(BF16)
    logits = jnp.dot(xn, w_ref[...], preferred_element_type=F32, precision=lax.Precision.HIGHEST)
    lane = lax.broadcasted_iota(jnp.int32, logits.shape, 1)
    logits = jnp.where(lane < N_EXPERTS, logits, -jnp.inf)
    top1 = jnp.max(logits, axis=-1, keepdims=True)
    lane_f = lane.astype(F32)
    idx1 = jnp.min(jnp.where(logits == top1, lane_f, float(LANES)), axis=-1, keepdims=True)
    rest = jnp.where(lane_f == idx1, -jnp.inf, logits)
    top2 = jnp.max(rest, axis=-1, keepdims=True)
    idx2 = jnp.min(jnp.where(rest == top2, lane_f, float(LANES)), axis=-1, keepdims=True)
    e2 = jnp.exp(top2 - top1)
    denom = 1.0 + e2
    chosen = jnp.where((lane_f == idx1) | (lane_f == idx2), 1.0, 0.0)
    r = lax.broadcasted_iota(jnp.int32, (tm, tm), 0)
    c = lax.broadcasted_iota(jnp.int32, (tm, tm), 1)
    ahead = jnp.dot(jnp.where(r > c, 1.0, 0.0).astype(BF16), chosen.astype(BF16), preferred_element_type=F32)
    carry = carry_ref[...]
    rank = ahead + carry
    rank1 = jnp.sum(jnp.where(lane_f == idx1, rank, 0.0), axis=-1, keepdims=True)
    rank2 = jnp.sum(jnp.where(lane_f == idx2, rank, 0.0), axis=-1, keepdims=True)
    vals = (idx1, idx2, rank1, rank2, 1.0 / denom, e2 / denom)
    meta = jnp.zeros(logits.shape, F32)
    for n, v in enumerate(vals):
        meta = jnp.where(lane == n, v, meta)
    meta_ref[...] = meta
    base_ref[...] = jnp.broadcast_to(carry, base_ref.shape)
    carry = carry + jnp.sum(chosen, axis=0, keepdims=True)
    carry_ref[...] = carry
    cnt_ref[...] = jnp.broadcast_to(carry, cnt_ref.shape)


def moe_router(x, gain, w_router, *, tm):
    M, K = x.shape
    w = jnp.zeros((K, LANES), F32).at[:, :N_EXPERTS].set(w_router)
    return pl.pallas_call(
        _router_kernel,
        grid=(M // tm,),
        in_specs=[
            pl.BlockSpec((tm, K), lambda i: (i, 0)),
            pl.BlockSpec((1, K), lambda i: (0, 0)),
            pl.BlockSpec((K, LANES), lambda i: (0, 0)),
        ],
        out_specs=[
            pl.BlockSpec((tm, K), lambda i: (i, 0)),
            pl.BlockSpec((tm, LANES), lambda i: (i, 0)),
            pl.BlockSpec((None, 8, LANES), lambda i: (i, 0, 0)),
            pl.BlockSpec((8, LANES), lambda i: (0, 0)),
        ],
        out_shape=[
            jax.ShapeDtypeStruct((M, K), BF16),
            jax.ShapeDtypeStruct((M, LANES), F32),
            jax.ShapeDtypeStruct((M // tm, 8, LANES), F32),
            jax.ShapeDtypeStruct((8, LANES), F32),
        ],
        scratch_shapes=[pltpu.VMEM((1, LANES), F32)],
        compiler_params=_params("arbitrary"),
        name="moe_router",
    )(x, gain.reshape(1, K), w)


def _permute_in_kernel(s_ref, t_ref, flag_ref, x_ref, dest_ref, zero_ref, o_ref, acc_ref):
    del t_ref, zero_ref
    k = pl.program_id(0)
    p = PERM_TILE

    @pl.when(flag_ref[3 * k] != 0)
    def _():
        row = (lax.broadcasted_iota(jnp.int32, (p, p), 0) + s_ref[k] * p).astype(F32)
        hit = (dest_ref[0:1, :] == row) | (dest_ref[1:2, :] == row)
        got = jnp.dot(jnp.where(hit, 1.0, 0.0).astype(BF16), x_ref[...], preferred_element_type=F32)

        @pl.when(flag_ref[3 * k + 1] != 0)
        def _():
            acc_ref[...] = got

        @pl.when(flag_ref[3 * k + 1] == 0)
        def _():
            acc_ref[...] += got

        @pl.when(flag_ref[3 * k + 2] != 0)
        def _():
            o_ref[...] = acc_ref[...].astype(o_ref.dtype)


def permute_in(xn, dest_rows, items, n_rows):
    M, K = xn.shape
    p = PERM_TILE
    s_item, t_item, flags = items
    grid_spec = pltpu.PrefetchScalarGridSpec(
        num_scalar_prefetch=3,
        grid=(s_item.shape[0],),
        in_specs=[
            pl.BlockSpec((p, K), lambda k, s, t, f: (t[k], 0)),
            pl.BlockSpec((None, 8, p), lambda k, s, t, f: (t[k], 0, 0)),
            pl.BlockSpec(memory_space=pl.ANY),
        ],
        out_specs=pl.BlockSpec((p, K), lambda k, s, t, f: (s[k], 0)),
        scratch_shapes=[pltpu.VMEM((p, K), F32)],
    )
    return pl.pallas_call(
        _permute_in_kernel,
        grid_spec=grid_spec,
        out_shape=jax.ShapeDtypeStruct((n_rows, K), BF16),
        input_output_aliases={5: 0},
        compiler_params=_params("arbitrary"),
        name="moe_permute_in",
    )(s_item, t_item, flags, xn, dest_rows, jnp.zeros((n_rows, K), BF16))


def _permute_out_kernel(s_ref, t_ref, flag_ref, h_ref, y_ref, meta_ref, o_ref):
    del t_ref
    k = pl.program_id(0)
    p = PERM_TILE

    @pl.when(flag_ref[2 * k + 1] != 0)
    def _():
        o_ref[...] = h_ref[...]

    @pl.when(flag_ref[2 * k] != 0)
    def _():
        lo = (s_ref[k] * p).astype(F32)
        row = lax.broadcasted_iota(jnp.int32, (p, p), 1).astype(F32) + lo
        d1 = meta_ref[:, 0:1]
        d2 = meta_ref[:, 1:2]
        hit = (d1 == row) | (d2 == row)
        got = jnp.dot(jnp.where(hit, 1.0, 0.0).astype(BF16), y_ref[...], preferred_element_type=F32)
        gate = (jnp.where((d1 >= lo) & (d1 < lo + p), meta_ref[:, 2:3], 0.0)
                + jnp.where((d2 >= lo) & (d2 < lo + p), meta_ref[:, 3:4], 0.0))
        o_ref[...] += gate * got


def permute_out(h, y, dest_gate, items):
    M, K = h.shape
    p = PERM_TILE
    s_item, t_item, flags = items
    grid_spec = pltpu.PrefetchScalarGridSpec(
        num_scalar_prefetch=3,
        grid=(s_item.shape[0],),
        in_specs=[
            pl.BlockSpec((p, K), lambda k, s, t, f: (t[k], 0)),
            pl.BlockSpec((p, K), lambda k, s, t, f: (s[k], 0)),
            pl.BlockSpec((p, LANES), lambda k, s, t, f: (t[k], 0)),
        ],
        out_specs=pl.BlockSpec((p, K), lambda k, s, t, f: (t[k], 0)),
    )
    return pl.pallas_call(
        _permute_out_kernel,
        grid_spec=grid_spec,
        out_shape=jax.ShapeDtypeStruct((M, K), F32),
        compiler_params=_params("arbitrary"),
        name="moe_permute_out",
    )(s_item, t_item, flags, h, y, dest_gate)


def _expert_ffn_kernel(te_ref, nu_ref, x_ref, wg_ref, wu_ref, wd_ref, o_ref, acc_ref):
    del te_ref
    i = pl.program_id(0)
    j = pl.program_id(1)

    @pl.when(i < nu_ref[0])
    def _():
        x = x_ref[...]
        gt = jnp.dot(x, wg_ref[...], preferred_element_type=F32)
        up = jnp.dot(x, wu_ref[...], preferred_element_type=F32)
        act = (gt / (1.0 + jnp.exp(-gt)) * up).astype(BF16)
        part = jnp.dot(act, wd_ref[...], preferred_element_type=F32)

        @pl.when(j == 0)
        def _():
            acc_ref[...] = part

        @pl.when(j > 0)
        def _():
            acc_ref[...] += part

        @pl.when(j == pl.num_programs(1) - 1)
        def _():
            o_ref[...] = acc_ref[...].astype(o_ref.dtype)


def expert_ffn(xs, tile_expert, n_used, w_gate_up, w_down, *, tm, tf):
    R, K = xs.shape
    E, F, _ = w_down.shape
    nf = F // tf
    used = lambda i, nu: jnp.minimum(i, nu[0] - 1)
    grid_spec = pltpu.PrefetchScalarGridSpec(
        num_scalar_prefetch=2,
        grid=(R // tm, nf),
        in_specs=[
            pl.BlockSpec((tm, K), lambda i, j, te, nu: (used(i, nu), 0)),
            pl.BlockSpec((None, K, tf), lambda i, j, te, nu: (te[used(i, nu)], 0, jnp.where(i < nu[0], j, nf - 1))),
            pl.BlockSpec((None, K, tf),
                         lambda i, j, te, nu: (te[used(i, nu)], 0, nf + jnp.where(i < nu[0], j, nf - 1))),
            pl.BlockSpec((None, tf, K), lambda i, j, te, nu: (te[used(i, nu)], jnp.where(i < nu[0], j, nf - 1), 0)),
        ],
        out_specs=pl.BlockSpec((tm, K), lambda i, j, te, nu: (used(i, nu), 0)),
        scratch_shapes=[pltpu.VMEM((tm, K), F32)],
    )
    return pl.pallas_call(
        _expert_ffn_kernel,
        grid_spec=grid_spec,
        out_shape=jax.ShapeDtypeStruct((R, K), BF16),
        compiler_params=_params("arbitrary", "arbitrary"),
        name="moe_expert_ffn",
    )(tile_expert, n_used, xs, w_gate_up, w_gate_up, w_down)


def _compact(valid, n_out, *arrays):
    n = valid.shape[0]
    rank = jnp.cumsum(valid.astype(jnp.int32)) - 1
    n_valid = rank[n - 1] + 1
    slot = jnp.arange(n_out)
    pick = valid[None, :] & (rank[None, :] == jnp.minimum(slot, n_valid - 1)[:, None])
    return slot < n_valid, [jnp.sum(jnp.where(pick, a[None, :], 0), axis=1).astype(jnp.int32) for a in arrays]


def _changes(valid, x):
    prev = jnp.concatenate([jnp.full((1,), -1, x.dtype), x[:-1]])
    nxt = jnp.concatenate([x[1:], jnp.full((1,), -1, x.dtype)])
    nxt_valid = jnp.concatenate([valid[1:], jnp.zeros((1,), bool)])
    return valid & (x != prev), valid & ((x != nxt) | ~nxt_valid)


def moe_residual(h, gain, w_router, w_gate_up, w_down):
    M, K = h.shape
    p = PERM_TILE
    tm = EXPERT_TILE
    ntt = M // p
    n_rows = 2 * M + N_EXPERTS * tm
    xn, meta, base, cnt = moe_router(h, gain, w_router, tm=p)

    idx = meta[:, META_IDX:META_IDX + 2].astype(jnp.int32)
    rank = meta[:, META_RANK:META_RANK + 2].astype(jnp.int32)
    counts = cnt[0, :N_EXPERTS].astype(jnp.int32)
    padded = (counts + tm - 1) // tm * tm
    ends = jnp.cumsum(padded)
    offsets = ends - padded
    n_used = (ends[-1] // tm).reshape(1)
    tile_row = jnp.arange(n_rows // tm) * tm
    tile_expert = jnp.minimum(jnp.sum(ends[None, :] <= tile_row[:, None], axis=1), N_EXPERTS - 1).astype(jnp.int32)
    offset_of = jnp.sum(jnp.where(idx[:, :, None] == jnp.arange(N_EXPERTS), offsets, 0), axis=-1)
    dest = (offset_of + rank).astype(F32)
    dest_rows = jnp.zeros((ntt, 8, p), F32).at[:, 0:2, :].set(dest.reshape(ntt, p, 2).transpose(0, 2, 1))
    dest_gate = jnp.zeros((M, LANES), F32).at[:, 0:2].set(dest).at[:, 2:4].set(meta[:, META_GATE:META_GATE + 2])

    start = base[:, 0, :N_EXPERTS].astype(jnp.int32)
    stop = jnp.concatenate([start[1:], counts[None, :]], axis=0)
    first = (offsets[None, :] + start) // p
    last = (offsets[None, :] + stop - 1) // p
    nonempty = stop > start
    s_cand = jnp.stack([first, first + 1], axis=-1)
    v_cand = jnp.stack([nonempty, nonempty & (last > first)], axis=-1)
    t_cand = jnp.broadcast_to(jnp.arange(ntt)[:, None, None], s_cand.shape)

    n_items = ntt * N_EXPERTS + n_rows // p - N_EXPERTS

    v_in, (s_in, t_in) = _compact(v_cand.transpose(1, 0, 2).reshape(-1), n_items,
                                  s_cand.transpose(1, 0, 2).reshape(-1), t_cand.transpose(1, 0, 2).reshape(-1))
    first_in, last_in = _changes(v_in, s_in)
    flags_in = jnp.stack([v_in, first_in, last_in], axis=-1).astype(jnp.int32).reshape(-1)
    xs = permute_in(xn, dest_rows, (s_in, t_in, flags_in), n_rows)

    ys = expert_ffn(xs, tile_expert, n_used.astype(jnp.int32), w_gate_up, w_down, tm=tm, tf=896)

    v_out, (s_out, t_out) = _compact(v_cand.reshape(-1), n_items, s_cand.reshape(-1), t_cand.reshape(-1))
    first_out, _ = _changes(v_out, t_out)
    flags_out = jnp.stack([v_out, first_out], axis=-1).astype(jnp.int32).reshape(-1)
    return permute_out(h, ys, dest_gate, (s_out, t_out, flags_out))


def _ffn_kernel(x_ref, g_ref, wg_ref, wu_ref, wd_ref, o_ref, xn_ref, acc_ref):
    j = pl.program_id(1)

    @pl.when(j == 0)
    def _():
        x = x_ref[...]
        xn_ref[...] = _rms(x, g_ref[...]).astype(BF16)
        acc_ref[...] = x

    xn = xn_ref[...]
    gt = jnp.dot(xn, wg_ref[...], preferred_element_type=F32)
    up = jnp.dot(xn, wu_ref[...], preferred_element_type=F32)
    act = gt / (1.0 + jnp.exp(-gt)) * up
    acc_ref[...] += jnp.dot(act.astype(BF16), wd_ref[...], preferred_element_type=F32)

    @pl.when(j == pl.num_programs(1) - 1)
    def _():
        o_ref[...] = acc_ref[...]


def ffn_residual(x, gain, w_gate_up, w_down, *, tm, tf):
    M, K = x.shape
    F = w_down.shape[0]
    nf = F // tf
    return pl.pallas_call(
        _ffn_kernel,
        grid=(M // tm, nf),
        in_specs=[
            pl.BlockSpec((tm, K), lambda i, j: (i, 0)),
            pl.BlockSpec((1, K), lambda i, j: (0, 0)),
            pl.BlockSpec((K, tf), lambda i, j: (0, j)),
            pl.BlockSpec((K, tf), lambda i, j: (0, nf + j)),
            pl.BlockSpec((tf, K), lambda i, j: (j, 0)),
        ],
        out_specs=pl.BlockSpec((tm, K), lambda i, j: (i, 0)),
        out_shape=jax.ShapeDtypeStruct((M, K), F32),
        scratch_shapes=[pltpu.VMEM((tm, K), BF16), pltpu.VMEM((tm, K), F32)],
        compiler_params=_params("parallel", "arbitrary"),
        name="ffn_residual",
    )(x, gain.reshape(1, K), w_gate_up, w_gate_up, w_down)


def _even_mix(h, B, S, norm1, w_in, f_bias, q_norm, k_norm, conv_w, a_log, dt_bias, o_norm, w_out):
    M, D = h.shape
    fw, gw = FOX_HEADS * HEAD_DIM, GDN_HEADS * HEAD_DIM
    o_ff = 3 * fw
    o_gq = o_ff + FOX_HEADS
    o_ga = o_gq + 3 * gw
    o_gb = o_ga + GDN_HEADS
    o_gg = o_gb + GDN_HEADS
    w_big = jnp.concatenate([w_in[:, :o_ff], w_in[:, o_gq:o_ga], w_in[:, o_gg:]], axis=1).astype(BF16)
    w_small = jnp.concatenate([w_in[:, o_ff:o_gq], w_in[:, o_ga:o_gg],
                               jnp.zeros((D, LANES - FOX_HEADS - 2 * GDN_HEADS), F32)], axis=1).astype(BF16)
    z, zs = norm_matmul(h, norm1, w_big, w_small, tm=ROW_TILE, tn=896)
    par = jnp.zeros((8, LANES), F32)
    par = par.at[0, LANE_F:LANE_F + FOX_HEADS].set(f_bias).at[0, LANE_A:LANE_A + GDN_HEADS].set(dt_bias)
    par = par.at[1, LANE_A:LANE_A + GDN_HEADS].set(a_log)
    col, row = even_gates(zs.reshape(B, S, LANES), par)
    row = row.reshape(B, 16, 1, S)
    z = z.reshape(B, S, -1)
    fox = fox_attention(z, col, q_norm, k_norm)
    gdn = gdn_mixer(z, conv_w, col, row, o_norm)
    w_out = w_out.astype(BF16)
    return matmul_residual(h, [(fox.reshape(M, fw), w_out[:fw]), (gdn.reshape(M, gw), w_out[fw:])], tm=ROW_TILE, tn=1024)


def _odd_mix(h, B, S, norm1, w_qkv, q_norm, k_norm, w_out):
    M, D = h.shape
    z = norm_matmul(h, norm1, w_qkv.astype(BF16), tm=ROW_TILE, tn=1024).reshape(B, S, -1)
    half = HEAD_DIM // 2
    inv = jnp.power(ROPE_THETA, -jnp.arange(half, dtype=F32) / half)
    ang = jnp.arange(S, dtype=F32)[:, None] * inv[None, :]
    cos, sin = jnp.cos(ang), jnp.sin(ang)
    cos_full = jnp.concatenate([cos, cos], axis=-1)
    sin_signed = jnp.concatenate([-sin, sin], axis=-1)
    att = moba_attention(z, cos_full, sin_signed, q_norm, k_norm)
    return matmul_residual(h, [(att.reshape(M, -1), w_out.astype(BF16))], tm=ROW_TILE, tn=1024)


def _odd_moe(h, norm2, w_router, w_gate_up, w_down):
    return moe_residual(h, norm2, w_router, w_gate_up.astype(BF16), w_down.astype(BF16))


def kernel(x, e_norm1, e_w_in, e_fox_f_bias, e_fox_q_norm, e_fox_k_norm, e_gdn_conv, e_gdn_a_log,
           e_gdn_dt_bias, e_gdn_o_norm, e_w_out, e_norm2, e_ffn_w_gate_up, e_ffn_w_down,
           o_norm1, o_w_qkv, o_q_norm, o_k_norm, o_w_out, o_norm2, o_router, o_exp_w_gate_up, o_exp_w_down):
    B, S, D = x.shape
    h = x.reshape(B * S, D)
    depth = e_norm1.shape[0] + o_norm1.shape[0]
    for layer in range(depth):
        i = layer // 2
        if layer % 2 == 0:
            h = _even_mix(h, B, S, e_norm1[i], e_w_in[i], e_fox_f_bias[i], e_fox_q_norm[i], e_fox_k_norm[i],
                          e_gdn_conv[i], e_gdn_a_log[i], e_gdn_dt_bias[i], e_gdn_o_norm[i], e_w_out[i])
            h = ffn_residual(h, e_norm2[i], e_ffn_w_gate_up[i].astype(BF16), e_ffn_w_down[i].astype(BF16),
                             tm=ROW_TILE // 2, tf=1408)
        else:
            h = _odd_mix(h, B, S, o_norm1[i], o_w_qkv[i], o_q_norm[i], o_k_norm[i], o_w_out[i])
            h = _odd_moe(h, o_norm2[i], o_router[i], o_exp_w_gate_up[i], o_exp_w_down[i])
    return h.reshape(B, S, D)
```

```python
import functools

import jax
import jax.numpy as jnp
from jax import lax
from jax.experimental import pallas as pl
from jax.experimental.pallas import tpu as pltpu

F32 = jnp.float32
BF16 = jnp.bfloat16

HEAD_DIM = 128
FOX_HEADS = 4
GDN_HEADS = 4
CONV_WIDTH = 4
MOBA_BLOCK = 256
MOBA_TOPK = 3
N_EXPERTS = 8
ROPE_THETA = 10000.0
EPS = 1e-6

LANES = 128
GDN_CHUNK = 128
INV_BLOCK = 16
GDN_HEADS_PER_STEP = 4
NEG = -(2.0 ** 100)
LOG2E = 1.4426950408889634
ATT_BLOCK = 256
ATT_HEADS = 2
V_ROWS = HEAD_DIM + 16
PERM_TILE = 512
ROW_TILE = 1024
EXPERT_TILE = 1024
VMEM_LIMIT_BYTES = 56 * 1024 * 1024

FOX_Q0, FOX_K0, FOX_V0 = 0, 4, 8
GDN_Q0, GDN_K0, GDN_V0, GDN_G0 = 12, 16, 20, 24
LANE_F, LANE_A, LANE_B = 0, 4, 8


def _params(*sem):
    return pltpu.CompilerParams(dimension_semantics=sem, vmem_limit_bytes=VMEM_LIMIT_BYTES)


def _rms(x, gain):
    return x * lax.rsqrt(jnp.mean(x * x, axis=-1, keepdims=True) + EPS) * gain


def _dot_nt(a, b, **kw):
    return lax.dot_general(a, b, (((1,), (1,)), ((), ())), preferred_element_type=F32, **kw)


def _pick_lane(x, lane_idx):
    lane = lax.broadcasted_iota(jnp.int32, x.shape, 1)
    return jnp.sum(jnp.where(lane == lane_idx, x, 0.0), axis=-1, keepdims=True)


def _norm_mm_kernel(x_ref, g_ref, w_ref, *rest, has_aux):
    if has_aux:
        waux_ref, o_ref, oaux_ref, xn_ref = rest
    else:
        o_ref, xn_ref = rest

    @pl.when(pl.program_id(1) == 0)
    def _():
        xn = _rms(x_ref[...], g_ref[...]).astype(BF16)
        xn_ref[...] = xn
        if has_aux:
            oaux_ref[...] = jnp.dot(xn, waux_ref[...], preferred_element_type=F32)

    o_ref[...] = jnp.dot(xn_ref[...], w_ref[...], preferred_element_type=F32).astype(o_ref.dtype)


def norm_matmul(x, gain, w, w_aux=None, *, tm, tn, out_dtype=BF16):
    M, K = x.shape
    N = w.shape[1]
    has_aux = w_aux is not None
    in_specs = [
        pl.BlockSpec((tm, K), lambda i, j: (i, 0)),
        pl.BlockSpec((1, K), lambda i, j: (0, 0)),
        pl.BlockSpec((K, tn), lambda i, j: (0, j)),
    ]
    out_shape = [jax.ShapeDtypeStruct((M, N), out_dtype)]
    out_specs = [pl.BlockSpec((tm, tn), lambda i, j: (i, j))]
    args = [x, gain.reshape(1, K), w]
    if has_aux:
        in_specs.append(pl.BlockSpec((K, LANES), lambda i, j: (0, 0)))
        out_shape.append(jax.ShapeDtypeStruct((M, LANES), F32))
        out_specs.append(pl.BlockSpec((tm, LANES), lambda i, j: (i, 0)))
        args.append(w_aux)
    res = pl.pallas_call(
        functools.partial(_norm_mm_kernel, has_aux=has_aux),
        grid=(M // tm, N // tn),
        in_specs=in_specs,
        out_specs=out_specs,
        out_shape=out_shape,
        scratch_shapes=[pltpu.VMEM((tm, K), BF16)],
        compiler_params=_params("parallel", "arbitrary"),
        name="norm_matmul",
    )(*args)
    return res if has_aux else res[0]


def _mm_res_kernel(*refs, n_in):
    res_ref = refs[0]
    o_ref = refs[1 + 2 * n_in]
    acc = res_ref[...]
    for t in range(n_in):
        acc = acc + jnp.dot(refs[1 + 2 * t][...], refs[2 + 2 * t][...], preferred_element_type=F32)
    o_ref[...] = acc


def matmul_residual(res, pairs, *, tm, tn):
    M, N = res.shape
    in_specs = [pl.BlockSpec((tm, tn), lambda i, j: (i, j))]
    args = [res]
    for a, w in pairs:
        K = a.shape[1]
        in_specs.append(pl.BlockSpec((tm, K), lambda i, j: (i, 0)))
        in_specs.append(pl.BlockSpec((K, tn), lambda i, j: (0, j)))
        args += [a, w]
    return pl.pallas_call(
        functools.partial(_mm_res_kernel, n_in=len(pairs)),
        grid=(M // tm, N // tn),
        in_specs=in_specs,
        out_specs=pl.BlockSpec((tm, tn), lambda i, j: (i, j)),
        out_shape=jax.ShapeDtypeStruct((M, N), F32),
        compiler_params=_params("parallel", "arbitrary"),
        name="matmul_residual",
    )(*args)


def _gate_kernel(zs_ref, par_ref, col_ref, row_ref, *, S):
    C = GDN_CHUNK
    bias = par_ref[0:1, :]
    neg_a = -jnp.exp(par_ref[1:2, :])
    r = lax.broadcasted_iota(jnp.int32, (C, C), 0)
    c = lax.broadcasted_iota(jnp.int32, (C, C), 1)
    tril = (r >= c).astype(F32)
    lane = lax.broadcasted_iota(jnp.int32, (C, LANES), 1)

    def body(n, carry):
        sl = pl.ds(pl.multiple_of(n * C, C), C)
        z = zs_ref[sl, :]
        t = z + bias
        soft = jnp.log(1.0 + jnp.exp(-jnp.abs(t)))
        log_f = jnp.minimum(t, 0.0) - soft
        g = neg_a * (jnp.maximum(t, 0.0) + soft)
        beta = 1.0 / (1.0 + jnp.exp(-z))
        u = jnp.where(lane < LANE_A, log_f, jnp.where(lane < LANE_B, g, 0.0))
        cs = jnp.dot(tril, u, preferred_element_type=F32, precision=lax.Precision.HIGHEST)
        cs = cs + jnp.where(lane < LANE_A, carry, 0.0)
        out = jnp.where(lane < LANE_B, cs, beta)
        col_ref[sl, :] = out
        row_ref[:, sl] = out.T[0:16, :]
        return cs[C - 1:C, :]

    lax.fori_loop(0, S // C, body, jnp.zeros((1, LANES), F32))


def even_gates(zs, par):
    B, S, _ = zs.shape
    return pl.pallas_call(
        functools.partial(_gate_kernel, S=S),
        grid=(B,),
        in_specs=[
            pl.BlockSpec((None, S, LANES), lambda b: (b, 0, 0)),
            pl.BlockSpec((8, LANES), lambda b: (0, 0)),
        ],
        out_specs=[
            pl.BlockSpec((None, S, LANES), lambda b: (b, 0, 0)),
            pl.BlockSpec((None, 16, S), lambda b: (b, 0, 0)),
        ],
        out_shape=[jax.ShapeDtypeStruct((B, S, LANES), F32), jax.ShapeDtypeStruct((B, 16, S), F32)],
        compiler_params=_params("parallel"),
        name="even_gates",
    )(zs, par)


def _split3(x):
    hi = x.astype(BF16).astype(F32)
    mid = (x - hi).astype(BF16).astype(F32)
    lo = (x - hi - mid).astype(BF16).astype(F32)
    return hi, mid, lo


def _interleave(gens):
    out = [None] * len(gens)
    live = list(range(len(gens)))
    while live:
        for n in list(live):
            try:
                next(gens[n])
            except StopIteration as stop:
                out[n] = stop.value
                live.remove(n)
    return out


def _attend(state, c, qa, ka_ref, vt_ref, key0, nkeys, keep=None):
    ks = pl.ds(pl.multiple_of(key0, ATT_BLOCK), nkeys)
    st = _dot_nt(ka_ref[ks, :], qa)
    yield
    if keep is not None:
        st = jnp.where(keep, st, NEG)
    m_new = jnp.max(st, axis=0, keepdims=True)
    if state[c] is not None:
        m_old, acc_old = state[c]
        m_new = jnp.maximum(m_old, m_new)
    p = jnp.exp2(st - m_new).astype(BF16)
    pv = jnp.dot(vt_ref[:, ks], p, preferred_element_type=F32)
    state[c] = (m_new, pv if state[c] is None else acc_old * jnp.exp2(m_old - m_new) + pv)


def _attend_tile_pair(i, prep, ka_ref, vt_ref, m_ref, acc_ref, o_ref):
    t = ATT_BLOCK
    chains = [(hd, c) for c in range(2) for hd in range(ATT_HEADS)]
    num = lambda hd, c: 2 * hd + c
    qa = _interleave(prep)

    def step(state, hd, c, which, **kw):
        return _attend(state, num(hd, c), qa[num(hd, c)][which], ka_ref.at[hd], vt_ref.at[hd], **kw)

    def save(state):
        m_ref[...] = jnp.stack([state[n][0] for n in range(len(chains))])
        acc_ref[...] = jnp.stack([state[n][1] for n in range(len(chains))])

    def past_blocks(key0, n_steps):
        state = {n: (m_ref[n], acc_ref[n]) for n in range(len(chains))}
        _interleave([step(state, hd, c, 1, key0=key0 + s * 2 * t, nkeys=2 * t)
                     for s in range(n_steps) for hd, c in chains])
        save(state)

    state = {n: None for n in range(len(chains))}
    keep = _causal_keep()
    _interleave([step(state, hd, c, 0, key0=(2 * i + c) * t, nkeys=t, keep=keep) for hd, c in chains]
                + [step(state, hd, 1, 1, key0=(2 * i) * t, nkeys=t) for hd in range(ATT_HEADS)])
    save(state)

    def four_blocks(g, _):
        past_blocks(g * (4 * t), 2)
        return 0

    lax.fori_loop(0, i // 2, four_blocks, 0)

    @pl.when((i & 1) != 0)
    def _():
        past_blocks((i // 2) * (4 * t), 1)

    for hd, c in chains:
        acc = acc_ref[num(hd, c)]
        out_t = acc[:HEAD_DIM, :] * (1.0 / acc[HEAD_DIM:HEAD_DIM + 1, :])
        o_ref[c * t:(c + 1) * t, hd * HEAD_DIM:(hd + 1) * HEAD_DIM] = out_t.T.astype(o_ref.dtype)


def _store_vt(vt_ref, v, sl):
    vt_ref[0:HEAD_DIM, sl] = v.astype(F32).T.astype(BF16)
    vt_ref[HEAD_DIM:V_ROWS, sl] = jnp.ones((V_ROWS - HEAD_DIM, ATT_BLOCK), BF16)


def _causal_keep():
    key = lax.broadcasted_iota(jnp.int32, (ATT_BLOCK, ATT_BLOCK), 0)
    qry = lax.broadcasted_iota(jnp.int32, (ATT_BLOCK, ATT_BLOCK), 1)
    return key <= qry


_ATT_SCRATCH = lambda S: [pltpu.VMEM((ATT_HEADS, S, 2 * HEAD_DIM), BF16), pltpu.VMEM((ATT_HEADS, V_ROWS, S), BF16),
                          pltpu.VMEM((2 * ATT_HEADS, 1, ATT_BLOCK), F32),
                          pltpu.VMEM((2 * ATT_HEADS, V_ROWS, ATT_BLOCK), F32)]


def _fox_kernel(q_ref, k_ref, v_ref, col_ref, qg_ref, kg_ref, o_ref, ka_ref, vt_ref, m_ref, acc_ref, *, S):
    h0 = pl.program_id(1) * ATT_HEADS
    i = pl.program_id(2)
    t = ATT_BLOCK
    D = HEAD_DIM
    lane = lax.broadcasted_iota(jnp.int32, (t, LANES), 1)

    @pl.when(i == 0)
    def _():
        def prep_keys(n, _):
            sl = pl.ds(pl.multiple_of(n * t, t), t)
            for hd in range(ATT_HEADS):
                cols = slice(hd * D, (hd + 1) * D)
                ka_ref[hd, sl, 0:D] = _rms(k_ref[sl, cols].astype(F32), kg_ref[...]).astype(BF16)
                hi, mid, lo = _split3(-LOG2E * _pick_lane(col_ref[sl, :], LANE_F + h0 + hd))
                ka_ref[hd, sl, D:] = jnp.where(
                    lane < 3, 1.0, jnp.where(lane == 3, hi, jnp.where(lane == 4, mid, jnp.where(lane == 5, lo, 0.0)))
                ).astype(BF16)
                _store_vt(vt_ref.at[hd], v_ref[sl, cols], sl)
            return 0

        lax.fori_loop(0, S // t, prep_keys, 0)

    def prep(hd, c):
        q = q_ref[c * t:(c + 1) * t, hd * D:(hd + 1) * D].astype(F32)
        qn = (_rms(q, qg_ref[...]) * (LOG2E * D ** -0.5)).astype(BF16)
        yield
        qsl = pl.ds(pl.multiple_of((2 * i + c) * t, t), t)
        hi, mid, lo = _split3(LOG2E * _pick_lane(col_ref[qsl, :], LANE_F + h0 + hd))
        aug = jnp.where(lane == 0, hi,
                        jnp.where(lane == 1, mid, jnp.where(lane == 2, lo, jnp.where(lane < 6, 1.0, 0.0))))
        qa = jnp.concatenate([qn, aug.astype(BF16)], axis=-1)
        return qa, qa

    _attend_tile_pair(i, [prep(hd, c) for hd in range(ATT_HEADS) for c in range(2)],
                      ka_ref, vt_ref, m_ref, acc_ref, o_ref)


def fox_attention(z, col, q_gain, k_gain):
    B, S, _ = z.shape
    t = 2 * ATT_BLOCK
    w = ATT_HEADS * HEAD_DIM
    return pl.pallas_call(
        functools.partial(_fox_kernel, S=S),
        grid=(B, FOX_HEADS // ATT_HEADS, S // t),
        in_specs=[
            pl.BlockSpec((None, t, w), lambda b, h, i: (b, i, FOX_Q0 // ATT_HEADS + h)),
            pl.BlockSpec((None, S, w), lambda b, h, i: (b, 0, FOX_K0 // ATT_HEADS + h)),
            pl.BlockSpec((None, S, w), lambda b, h, i: (b, 0, FOX_V0 // ATT_HEADS + h)),
            pl.BlockSpec((None, S, LANES), lambda b, h, i: (b, 0, 0)),
            pl.BlockSpec((1, HEAD_DIM), lambda b, h, i: (0, 0)),
            pl.BlockSpec((1, HEAD_DIM), lambda b, h, i: (0, 0)),
        ],
        out_specs=pl.BlockSpec((None, t, w), lambda b, h, i: (b, i, h)),
        out_shape=jax.ShapeDtypeStruct((B, S, FOX_HEADS * HEAD_DIM), BF16),
        scratch_shapes=_ATT_SCRATCH(S),
        compiler_params=_params("parallel", "parallel", "arbitrary"),
        name="fox_attention",
    )(z, z, z, col, q_gain.reshape(1, -1), k_gain.reshape(1, -1))


def _unit_lower_inverse(m):
    n = m.shape[0]
    r = lax.broadcasted_iota(jnp.int32, (n, n), 0)
    c = lax.broadcasted_iota(jnp.int32, (n, n), 1)
    eye = (r == c).astype(F32)

    def same_block(b):
        return (r // b) == (c // b)

    p = jnp.where(same_block(INV_BLOCK), m, 0.0)
    inv = eye - p
    k = 2
    while k < INV_BLOCK:
        pb = p.astype(BF16)
        p = jnp.dot(pb, pb, preferred_element_type=F32)
        yield
        inv = jnp.dot(inv.astype(BF16), (eye + p).astype(BF16), preferred_element_type=F32)
        yield
        k *= 2
    b = INV_BLOCK
    while b < n:
        off = jnp.where(same_block(2 * b), jnp.where(same_block(b), 0.0, m), 0.0).astype(BF16)
        ib = inv.astype(BF16)
        left = jnp.dot(ib, off, preferred_element_type=F32).astype(BF16)
        yield
        inv = inv - jnp.dot(left, ib, preferred_element_type=F32)
        yield
        b *= 2
    return inv


def _gdn_kernel(q_ref, k_ref, v_ref, gg_ref, wq_ref, wk_ref, wv_ref, col_ref, row_ref, on_ref,
                o_ref, pad_ref, qs_ref, ks_ref, vs_ref, *, S, rows, hb):
    h0 = pl.program_id(1) * hb
    C = GDN_CHUNK
    D = HEAD_DIM

    def conv_silu(x_ref, w_ref, dst_ref, hh, mode):
        cols = slice(hh * D, (hh + 1) * D)
        pad_ref[0:8, :] = jnp.zeros((8, D), F32)

        def fill(n, _):
            src = pl.ds(pl.multiple_of(n * rows, rows), rows)
            pad_ref[pl.ds(pl.multiple_of(8 + n * rows, 8), rows), :] = x_ref[src, cols].astype(F32)
            return 0

        lax.fori_loop(0, S // rows, fill, 0)
        w = w_ref[:, cols]

        def conv(n, _):
            base = pl.multiple_of(n * rows, rows)
            win = pad_ref[pl.ds(base, rows + 8), :]
            y = jnp.zeros((rows, D), F32)
            for tap in range(CONV_WIDTH):
                lead = 8 - (CONV_WIDTH - 1) + tap
                y = y + w[tap:tap + 1, :] * pltpu.roll(win, rows + 8 - lead, 0)[0:rows, :]
            y = y / (1.0 + jnp.exp(-y))
            if mode != "v":
                y = y * lax.rsqrt(jnp.sum(y * y, axis=-1, keepdims=True) + EPS)
            if mode == "q":
                y = y * D ** -0.5
            dst_ref[hh, pl.ds(base, rows), :] = y.astype(dst_ref.dtype)
            return 0

        lax.fori_loop(0, S // rows, conv, 0)

    for hh in range(hb):
        conv_silu(q_ref, wq_ref, qs_ref, hh, "q")
        conv_silu(k_ref, wk_ref, ks_ref, hh, "k")
        conv_silu(v_ref, wv_ref, vs_ref, hh, "v")

    r = lax.broadcasted_iota(jnp.int32, (C, C), 0)
    c = lax.broadcasted_iota(jnp.int32, (C, C), 1)
    incl = r >= c
    strict = r > c

    def head_chunk(hh, sl, tab, state):
        q = qs_ref[hh, sl, :].astype(F32)
        k = ks_ref[hh, sl, :].astype(F32)
        v = vs_ref[hh, sl, :].astype(F32)
        gcol = _pick_lane(tab, LANE_A + h0 + hh)
        beta = _pick_lane(tab, LANE_B + h0 + hh)
        grow = row_ref[hh, :, sl]
        glast = gcol[C - 1:C, :]
        decay = jnp.where(incl, jnp.exp(jnp.where(incl, gcol - grow, 0.0)), 0.0)
        eg = jnp.exp(gcol)
        kb = k * beta
        kbf = k.astype(BF16)
        m = jnp.where(strict, _dot_nt(kb.astype(BF16), kbf) * decay, 0.0)
        attn = _dot_nt(q.astype(BF16), kbf) * decay
        yield
        tinv = (yield from _unit_lower_inverse(m)).astype(BF16)
        rhs = jnp.concatenate([v * beta, kb * eg], axis=-1).astype(BF16)
        sol = jnp.dot(tinv, rhs, preferred_element_type=F32)
        yield
        u = sol[:, :D]
        w = sol[:, D:]
        qg = q * eg
        kg = k * jnp.exp(glast - gcol)
        sb = state.astype(BF16)
        v_new = u - jnp.dot(w.astype(BF16), sb, preferred_element_type=F32)
        o_state = jnp.dot(qg.astype(BF16), sb, preferred_element_type=F32)
        yield
        vb = v_new.astype(BF16)
        o = o_state + jnp.dot(attn.astype(BF16), vb, preferred_element_type=F32)
        state = state * jnp.exp(glast) + jnp.dot(kg.T.astype(BF16), vb, preferred_element_type=F32)
        yield
        gate = gg_ref[sl, hh * D:(hh + 1) * D].astype(F32)
        return state, (_rms(o, on_ref[...]) * (gate / (1.0 + jnp.exp(-gate)))).astype(o_ref.dtype)

    def chunk(n, states):
        sl = pl.ds(pl.multiple_of(n * C, C), C)
        tab = col_ref[sl, :]
        res = _interleave([head_chunk(hh, sl, tab, states[hh]) for hh in range(hb)])
        o_ref[sl, :] = jnp.concatenate([o for _, o in res], axis=-1)
        return tuple(s for s, _ in res)

    lax.fori_loop(0, S // C, chunk, tuple(jnp.zeros((D, D), F32) for _ in range(hb)))


def gdn_mixer(z, conv_w, col, row, o_gain):
    B, S, _ = z.shape
    D = HEAD_DIM
    hb = GDN_HEADS_PER_STEP
    rows = 256
    seq = lambda off: pl.BlockSpec((None, S, hb * D), lambda b, h: (b, 0, off // hb + h),
                                   pipeline_mode=pl.Buffered(1))
    cw = lambda off: pl.BlockSpec((CONV_WIDTH, hb * D), lambda b, h: (0, off // hb + h))
    return pl.pallas_call(
        functools.partial(_gdn_kernel, S=S, rows=rows, hb=hb),
        grid=(B, GDN_HEADS // hb),
        in_specs=[
            seq(GDN_Q0), seq(GDN_K0), seq(GDN_V0), seq(GDN_G0),
            cw(0), cw(GDN_HEADS), cw(2 * GDN_HEADS),
            pl.BlockSpec((None, S, LANES), lambda b, h: (b, 0, 0)),
            pl.BlockSpec((None, hb, 1, S), lambda b, h: (b, LANE_A // hb + h, 0, 0)),
            pl.BlockSpec((1, D), lambda b, h: (0, 0)),
        ],
        out_specs=pl.BlockSpec((None, S, hb * D), lambda b, h: (b, 0, h)),
        out_shape=jax.ShapeDtypeStruct((B, S, GDN_HEADS * D), BF16),
        scratch_shapes=[pltpu.VMEM((S + 8, D), F32)] + [pltpu.VMEM((hb, S, D), BF16)] * 3,
        compiler_params=_params("parallel", "parallel"),
        name="gdn_mixer",
    )(z, z, z, z, conv_w, conv_w, conv_w, col, row, o_gain.reshape(1, D))


def _rope(x, cos, sin_signed):
    return x * cos + pltpu.roll(x, HEAD_DIM // 2, 1) * sin_signed


def _moba_kernel(q_ref, k_ref, v_ref, cos_ref, sin_ref, qg_ref, kg_ref, o_ref,
                 ka_ref, vt_ref, m_ref, acc_ref, kmean_ref, *, S):
    i = pl.program_id(2)
    t = ATT_BLOCK
    D = HEAD_DIM
    lane = lax.broadcasted_iota(jnp.int32, (t, LANES), 1)

    @pl.when(i == 0)
    def _():
        kmean_ref[...] = jnp.zeros(kmean_ref.shape, F32)

        def prep_keys(n, _):
            sl = pl.ds(pl.multiple_of(n * t, t), t)
            for hd in range(ATT_HEADS):
                cols = slice(hd * D, (hd + 1) * D)
                k = _rope(_rms(k_ref[sl, cols].astype(F32), kg_ref[...]), cos_ref[sl, :], sin_ref[sl, :])
                ka_ref[hd, sl, 0:D] = k.astype(BF16)
                ka_ref[hd, sl, D:] = jnp.where(lane == n, 1.0, 0.0).astype(BF16)
                kmean_ref[hd, pl.ds(n, 1), :] = jnp.mean(k, axis=0, keepdims=True)
                _store_vt(vt_ref.at[hd], v_ref[sl, cols], sl)
            return 0

        lax.fori_loop(0, S // t, prep_keys, 0)

    lane_f = lane.astype(F32)

    def prep(hd, c):
        cur = 2 * i + c
        qsl = pl.ds(pl.multiple_of(cur * t, t), t)
        q = _rope(_rms(q_ref[c * t:(c + 1) * t, hd * D:(hd + 1) * D].astype(F32), qg_ref[...]),
                  cos_ref[qsl, :], sin_ref[qsl, :])
        yield
        kmean = kmean_ref[hd]
        km_hi = kmean.astype(BF16)
        km_split = jnp.concatenate([km_hi, (kmean - km_hi.astype(F32)).astype(BF16)], axis=0)
        q_hi = q.astype(BF16)
        q_lo = (q - q_hi.astype(F32)).astype(BF16)
        part = _dot_nt(q_hi, km_split)
        gate = part[:, :LANES] + part[:, LANES:] + _dot_nt(q_lo, km_split[:LANES, :])
        gate = jnp.where(lane < cur, gate, -jnp.inf)
        yield
        sel_bias = jnp.full((t, LANES), NEG, F32)
        for _ in range(MOBA_TOPK):
            top = jnp.max(gate, axis=-1, keepdims=True)
            yield
            first = jnp.min(jnp.where(gate == top, lane_f, float(LANES)), axis=-1, keepdims=True)
            yield
            pick = lane_f == first
            sel_bias = jnp.where(pick & (first < cur.astype(F32)), 0.0, sel_bias)
            gate = jnp.where(pick, -jnp.inf, gate)
        qs = (q * (LOG2E * D ** -0.5)).astype(BF16)
        return (jnp.concatenate([qs, jnp.zeros((t, LANES), BF16)], axis=-1),
                jnp.concatenate([qs, sel_bias.astype(BF16)], axis=-1))

    _attend_tile_pair(i, [prep(hd, c) for hd in range(ATT_HEADS) for c in range(2)],
                      ka_ref, vt_ref, m_ref, acc_ref, o_ref)


def moba_attention(z, cos, sin_signed, q_gain, k_gain):
    B, S, W = z.shape
    H = W // (3 * HEAD_DIM)
    assert ATT_BLOCK == MOBA_BLOCK
    t = 2 * ATT_BLOCK
    w = ATT_HEADS * HEAD_DIM
    hp = H // ATT_HEADS
    return pl.pallas_call(
        functools.partial(_moba_kernel, S=S),
        grid=(B, hp, S // t),
        in_specs=[
            pl.BlockSpec((None, t, w), lambda b, h, i: (b, i, h)),
            pl.BlockSpec((None, S, w), lambda b, h, i: (b, 0, hp + h)),
            pl.BlockSpec((None, S, w), lambda b, h, i: (b, 0, 2 * hp + h)),
            pl.BlockSpec((S, HEAD_DIM), lambda b, h, i: (0, 0)),
            pl.BlockSpec((S, HEAD_DIM), lambda b, h, i: (0, 0)),
            pl.BlockSpec((1, HEAD_DIM), lambda b, h, i: (0, 0)),
            pl.BlockSpec((1, HEAD_DIM), lambda b, h, i: (0, 0)),
        ],
        out_specs=pl.BlockSpec((None, t, w), lambda b, h, i: (b, i, h)),
        out_shape=jax.ShapeDtypeStruct((B, S, H * HEAD_DIM), BF16),
        scratch_shapes=_ATT_SCRATCH(S) + [pltpu.VMEM((ATT_HEADS, LANES, HEAD_DIM), F32)],
        compiler_params=_params("parallel", "parallel", "arbitrary"),
        name="moba_attention",
    )(z, z, z, cos, sin_signed, q_gain.reshape(1, -1), k_gain.reshape(1, -1))


META_IDX, META_RANK, META_GATE = 0, 2, 4


def _router_kernel(x_ref, g_ref, w_ref, xn_ref, meta_ref, base_ref, cnt_ref, carry_ref):
    i = pl.program_id(0)
    tm = x_ref.shape[0]

    @pl.when(i == 0)
    def _():
        carry_ref[...] = jnp.zeros(carry_ref.shape, F32)

    xn = _rms(x_ref[...], g_ref[...])
    xn_ref[...] = xn.astype(BF16)
    logits = jnp.dot(xn, w_ref[...], preferred_element_type=F32, precision=lax.Precision.HIGHEST)
    lane = lax.broadcasted_iota(jnp.int32, logits.shape, 1)
    logits = jnp.where(lane < N_EXPERTS, logits, -jnp.inf)
    top1 = jnp.max(logits, axis=-1, keepdims=True)
    lane_f = lane.astype(F32)
    idx1 = jnp.min(jnp.where(logits == top1, lane_f, float(LANES)), axis=-1, keepdims=True)
    rest = jnp.where(lane_f == idx1, -jnp.inf, logits)
    top2 = jnp.max(rest, axis=-1, keepdims=True)
    idx2 = jnp.min(jnp.where(rest == top2, lane_f, float(LANES)), axis=-1, keepdims=True)
    e2 = jnp.exp(top2 - top1)
    denom = 1.0 + e2
    chosen = jnp.where((lane_f == idx1) | (lane_f == idx2), 1.0, 0.0)
    r = lax.broadcasted_iota(jnp.int32, (tm, tm), 0)
    c = lax.broadcasted_iota(jnp.int32, (tm, tm), 1)
    ahead = jnp.dot(jnp.where(r > c, 1.0, 0.0).astype(BF16), chosen.astype(BF16), preferred_element_type=F32)
    carry = carry_ref[...]
    rank = ahead + carry
    rank1 = jnp.sum(jnp.where(lane_f == idx1, rank, 0.0), axis=-1, keepdims=True)
    rank2 = jnp.sum(jnp.where(lane_f == idx2, rank, 0.0), axis=-1, keepdims=True)
    vals = (idx1, idx2, rank1, rank2, 1.0 / denom, e2 / denom)
    meta = jnp.zeros(logits.shape, F32)
    for n, v in enumerate(vals):
        meta = jnp.where(lane == n, v, meta)
    meta_ref[...] = meta
    base_ref[...] = jnp.broadcast_to(carry, base_ref.shape)
    carry = carry + jnp.sum(chosen, axis=0, keepdims=True)
    carry_ref[...] = carry
    cnt_ref[...] = jnp.broadcast_to(carry, cnt_ref.shape)


def moe_router(x, gain, w_router, *, tm):
    M, K = x.shape
    w = jnp.zeros((K, LANES), F32).at[:, :N_EXPERTS].set(w_router)
    return pl.pallas_call(
        _router_kernel,
        grid=(M // tm,),
        in_specs=[
            pl.BlockSpec((tm, K), lambda i: (i, 0)),
            pl.BlockSpec((1, K), lambda i: (0, 0)),
            pl.BlockSpec((K, LANES), lambda i: (0, 0)),
        ],
        out_specs=[
            pl.BlockSpec((tm, K), lambda i: (i, 0)),
            pl.BlockSpec((tm, LANES), lambda i: (i, 0)),
            pl.BlockSpec((None, 8, LANES), lambda i: (i, 0, 0)),
            pl.BlockSpec((8, LANES), lambda i: (0, 0)),
        ],
        out_shape=[
            jax.ShapeDtypeStruct((M, K), BF16),
            jax.ShapeDtypeStruct((M, LANES), F32),
            jax.ShapeDtypeStruct((M // tm, 8, LANES), F32),
            jax.ShapeDtypeStruct((8, LANES), F32),
        ],
        scratch_shapes=[pltpu.VMEM((1, LANES), F32)],
        compiler_params=_params("arbitrary"),
        name="moe_router",
    )(x, gain.reshape(1, K), w)


def _permute_in_kernel(s_ref, t_ref, flag_ref, x_ref, dest_ref, zero_ref, o_ref, acc_ref):
    del t_ref, zero_ref
    k = pl.program_id(0)
    p = PERM_TILE

    @pl.when(flag_ref[3 * k] != 0)
    def _():
        row = (lax.broadcasted_iota(jnp.int32, (p, p), 0) + s_ref[k] * p).astype(F32)
        hit = (dest_ref[0:1, :] == row) | (dest_ref[1:2, :] == row)
        got = jnp.dot(jnp.where(hit, 1.0, 0.0).astype(BF16), x_ref[...], preferred_element_type=F32)

        @pl.when(flag_ref[3 * k + 1] != 0)
        def _():
            acc_ref[...] = got

        @pl.when(flag_ref[3 * k + 1] == 0)
        def _():
            acc_ref[...] += got

        @pl.when(flag_ref[3 * k + 2] != 0)
        def _():
            o_ref[...] = acc_ref[...].astype(o_ref.dtype)


def permute_in(xn, dest_rows, items, n_rows):
    M, K = xn.shape
    p = PERM_TILE
    s_item, t_item, flags = items
    grid_spec = pltpu.PrefetchScalarGridSpec(
        num_scalar_prefetch=3,
        grid=(s_item.shape[0],),
        in_specs=[
            pl.BlockSpec((p, K), lambda k, s, t, f: (t[k], 0)),
            pl.BlockSpec((None, 8, p), lambda k, s, t, f: (t[k], 0, 0)),
            pl.BlockSpec(memory_space=pl.ANY),
        ],
        out_specs=pl.BlockSpec((p, K), lambda k, s, t, f: (s[k], 0)),
        scratch_shapes=[pltpu.VMEM((p, K), F32)],
    )
    return pl.pallas_call(
        _permute_in_kernel,
        grid_spec=grid_spec,
        out_shape=jax.ShapeDtypeStruct((n_rows, K), BF16),
        input_output_aliases={5: 0},
        compiler_params=_params("arbitrary"),
        name="moe_permute_in",
    )(s_item, t_item, flags, xn, dest_rows, jnp.zeros((n_rows, K), BF16))


def _permute_out_kernel(s_ref, t_ref, flag_ref, h_ref, y_ref, meta_ref, o_ref):
    del t_ref
    k = pl.program_id(0)
    p = PERM_TILE

    @pl.when(flag_ref[2 * k + 1] != 0)
    def _():
        o_ref[...] = h_ref[...]

    @pl.when(flag_ref[2 * k] != 0)
    def _():
        lo = (s_ref[k] * p).astype(F32)
        row = lax.broadcasted_iota(jnp.int32, (p, p), 1).astype(F32) + lo
        d1 = meta_ref[:, 0:1]
        d2 = meta_ref[:, 1:2]
        hit = (d1 == row) | (d2 == row)
        got = jnp.dot(jnp.where(hit, 1.0, 0.0).astype(BF16), y_ref[...], preferred_element_type=F32)
        gate = (jnp.where((d1 >= lo) & (d1 < lo + p), meta_ref[:, 2:3], 0.0)
                + jnp.where((d2 >= lo) & (d2 < lo + p), meta_ref[:, 3:4], 0.0))
        o_ref[...] += gate * got


def permute_out(h, y, dest_gate, items):
    M, K = h.shape
    p = PERM_TILE
    s_item, t_item, flags = items
    grid_spec = pltpu.PrefetchScalarGridSpec(
        num_scalar_prefetch=3,
        grid=(s_item.shape[0],),
        in_specs=[
            pl.BlockSpec((p, K), lambda k, s, t, f: (t[k], 0)),
            pl.BlockSpec((p, K), lambda k, s, t, f: (s[k], 0)),
            pl.BlockSpec((p, LANES), lambda k, s, t, f: (t[k], 0)),
        ],
        out_specs=pl.BlockSpec((p, K), lambda k, s, t, f: (t[k], 0)),
    )
    return pl.pallas_call(
        _permute_out_kernel,
        grid_spec=grid_spec,
        out_shape=jax.ShapeDtypeStruct((M, K), F32),
        compiler_params=_params("arbitrary"),
        name="moe_permute_out",
    )(s_item, t_item, flags, h, y, dest_gate)


def _expert_ffn_kernel(te_ref, nu_ref, x_ref, wg_ref, wu_ref, wd_ref, o_ref, acc_ref):
    del te_ref
    i = pl.program_id(0)
    j = pl.program_id(1)

    @pl.when(i < nu_ref[0])
    def _():
        x = x_ref[...]
        gt = jnp.dot(x, wg_ref[...], preferred_element_type=F32)
        up = jnp.dot(x, wu_ref[...], preferred_element_type=F32)
        act = (gt / (1.0 + jnp.exp(-gt)) * up).astype(BF16)
        part = jnp.dot(act, wd_ref[...], preferred_element_type=F32)

        @pl.when(j == 0)
        def _():
            acc_ref[...] = part

        @pl.when(j > 0)
        def _():
            acc_ref[...] += part

        @pl.when(j == pl.num_programs(1) - 1)
        def _():
            o_ref[...] = acc_ref[...].astype(o_ref.dtype)


def expert_ffn(xs, tile_expert, n_used, w_gate_up, w_down, *, tm, tf):
    R, K = xs.shape
    E, F, _ = w_down.shape
    nf = F // tf
    used = lambda i, nu: jnp.minimum(i, nu[0] - 1)
    grid_spec = pltpu.PrefetchScalarGridSpec(
        num_scalar_prefetch=2,
        grid=(R // tm, nf),
        in_specs=[
            pl.BlockSpec((tm, K), lambda i, j, te, nu: (used(i, nu), 0)),
            pl.BlockSpec((None, K, tf), lambda i, j, te, nu: (te[used(i, nu)], 0, jnp.where(i < nu[0], j, nf - 1))),
            pl.BlockSpec((None, K, tf),
                         lambda i, j, te, nu: (te[used(i, nu)], 0, nf + jnp.where(i < nu[0], j, nf - 1))),
            pl.BlockSpec((None, tf, K), lambda i, j, te, nu: (te[used(i, nu)], jnp.where(i < nu[0], j, nf - 1), 0)),
        ],
        out_specs=pl.BlockSpec((tm, K), lambda i, j, te, nu: (used(i, nu), 0)),
        scratch_shapes=[pltpu.VMEM((tm, K), F32)],
    )
    return pl.pallas_call(
        _expert_ffn_kernel,
        grid_spec=grid_spec,
        out_shape=jax.ShapeDtypeStruct((R, K), BF16),
        compiler_params=_params("arbitrary", "arbitrary"),
        name="moe_expert_ffn",
    )(tile_expert, n_used, xs, w_gate_up, w_gate_up, w_down)


def _compact(valid, n_out, *arrays):
    n = valid.shape[0]
    rank = jnp.cumsum(valid.astype(jnp.int32)) - 1
    n_valid = rank[n - 1] + 1
    slot = jnp.arange(n_out)
    pick = valid[None, :] & (rank[None, :] == jnp.minimum(slot, n_valid - 1)[:, None])
    return slot < n_valid, [jnp.sum(jnp.where(pick, a[None, :], 0), axis=1).astype(jnp.int32) for a in arrays]


def _changes(valid, x):
    prev = jnp.concatenate([jnp.full((1,), -1, x.dtype), x[:-1]])
    nxt = jnp.concatenate([x[1:], jnp.full((1,), -1, x.dtype)])
    nxt_valid = jnp.concatenate([valid[1:], jnp.zeros((1,), bool)])
    return valid & (x != prev), valid & ((x != nxt) | ~nxt_valid)


def moe_residual(h, gain, w_router, w_gate_up, w_down):
    M, K = h.shape
    p = PERM_TILE
    tm = EXPERT_TILE
    ntt = M // p
    n_rows = 2 * M + N_EXPERTS * tm
    xn, meta, base, cnt = moe_router(h, gain, w_router, tm=p)

    idx = meta[:, META_IDX:META_IDX + 2].astype(jnp.int32)
    rank = meta[:, META_RANK:META_RANK + 2].astype(jnp.int32)
    counts = cnt[0, :N_EXPERTS].astype(jnp.int32)
    padded = (counts + tm - 1) // tm * tm
    ends = jnp.cumsum(padded)
    offsets = ends - padded
    n_used = (ends[-1] // tm).reshape(1)
    tile_row = jnp.arange(n_rows // tm) * tm
    tile_expert = jnp.minimum(jnp.sum(ends[None, :] <= tile_row[:, None], axis=1), N_EXPERTS - 1).astype(jnp.int32)
    offset_of = jnp.sum(jnp.where(idx[:, :, None] == jnp.arange(N_EXPERTS), offsets, 0), axis=-1)
    dest = (offset_of + rank).astype(F32)
    dest_rows = jnp.zeros((ntt, 8, p), F32).at[:, 0:2, :].set(dest.reshape(ntt, p, 2).transpose(0, 2, 1))
    dest_gate = jnp.zeros((M, LANES), F32).at[:, 0:2].set(dest).at[:, 2:4].set(meta[:, META_GATE:META_GATE + 2])

    start = base[:, 0, :N_EXPERTS].astype(jnp.int32)
    stop = jnp.concatenate([start[1:], counts[None, :]], axis=0)
    first = (offsets[None, :] + start) // p
    last = (offsets[None, :] + stop - 1) // p
    nonempty = stop > start
    s_cand = jnp.stack([first, first + 1], axis=-1)
    v_cand = jnp.stack([nonempty, nonempty & (last > first)], axis=-1)
    t_cand = jnp.broadcast_to(jnp.arange(ntt)[:, None, None], s_cand.shape)

    n_items = ntt * N_EXPERTS + n_rows // p - N_EXPERTS

    v_in, (s_in, t_in) = _compact(v_cand.transpose(1, 0, 2).reshape(-1), n_items,
                                  s_cand.transpose(1, 0, 2).reshape(-1), t_cand.transpose(1, 0, 2).reshape(-1))
    first_in, last_in = _changes(v_in, s_in)
    flags_in = jnp.stack([v_in, first_in, last_in], axis=-1).astype(jnp.int32).reshape(-1)
    xs = permute_in(xn, dest_rows, (s_in, t_in, flags_in), n_rows)

    ys = expert_ffn(xs, tile_expert, n_used.astype(jnp.int32), w_gate_up, w_down, tm=tm, tf=896)

    v_out, (s_out, t_out) = _compact(v_cand.reshape(-1), n_items, s_cand.reshape(-1), t_cand.reshape(-1))
    first_out, _ = _changes(v_out, t_out)
    flags_out = jnp.stack([v_out, first_out], axis=-1).astype(jnp.int32).reshape(-1)
    return permute_out(h, ys, dest_gate, (s_out, t_out, flags_out))


def _ffn_kernel(x_ref, g_ref, wg_ref, wu_ref, wd_ref, o_ref, xn_ref, acc_ref):
    j = pl.program_id(1)

    @pl.when(j == 0)
    def _():
        x = x_ref[...]
        xn_ref[...] = _rms(x, g_ref[...]).astype(BF16)
        acc_ref[...] = x

    xn = xn_ref[...]
    gt = jnp.dot(xn, wg_ref[...], preferred_element_type=F32)
    up = jnp.dot(xn, wu_ref[...], preferred_element_type=F32)
    act = gt / (1.0 + jnp.exp(-gt)) * up
    acc_ref[...] += jnp.dot(act.astype(BF16), wd_ref[...], preferred_element_type=F32)

    @pl.when(j == pl.num_programs(1) - 1)
    def _():
        o_ref[...] = acc_ref[...]


def ffn_residual(x, gain, w_gate_up, w_down, *, tm, tf):
    M, K = x.shape
    F = w_down.shape[0]
    nf = F // tf
    return pl.pallas_call(
        _ffn_kernel,
        grid=(M // tm, nf),
        in_specs=[
            pl.BlockSpec((tm, K), lambda i, j: (i, 0)),
            pl.BlockSpec((1, K), lambda i, j: (0, 0)),
            pl.BlockSpec((K, tf), lambda i, j: (0, j)),
            pl.BlockSpec((K, tf), lambda i, j: (0, nf + j)),
            pl.BlockSpec((tf, K), lambda i, j: (j, 0)),
        ],
        out_specs=pl.BlockSpec((tm, K), lambda i, j: (i, 0)),
        out_shape=jax.ShapeDtypeStruct((M, K), F32),
        scratch_shapes=[pltpu.VMEM((tm, K), BF16), pltpu.VMEM((tm, K), F32)],
        compiler_params=_params("parallel", "arbitrary"),
        name="ffn_residual",
    )(x, gain.reshape(1, K), w_gate_up, w_gate_up, w_down)


def _even_mix(h, B, S, norm1, w_in, f_bias, q_norm, k_norm, conv_w, a_log, dt_bias, o_norm, w_out):
    M, D = h.shape
    fw, gw = FOX_HEADS * HEAD_DIM, GDN_HEADS * HEAD_DIM
    o_ff = 3 * fw
    o_gq = o_ff + FOX_HEADS
    o_ga = o_gq + 3 * gw
    o_gb = o_ga + GDN_HEADS
    o_gg = o_gb + GDN_HEADS
    w_big = jnp.concatenate([w_in[:, :o_ff], w_in[:, o_gq:o_ga], w_in[:, o_gg:]], axis=1).astype(BF16)
    w_small = jnp.concatenate([w_in[:, o_ff:o_gq], w_in[:, o_ga:o_gg],
                               jnp.zeros((D, LANES - FOX_HEADS - 2 * GDN_HEADS), F32)], axis=1).astype(BF16)
    z, zs = norm_matmul(h, norm1, w_big, w_small, tm=ROW_TILE, tn=896)
    par = jnp.zeros((8, LANES), F32)
    par = par.at[0, LANE_F:LANE_F + FOX_HEADS].set(f_bias).at[0, LANE_A:LANE_A + GDN_HEADS].set(dt_bias)
    par = par.at[1, LANE_A:LANE_A + GDN_HEADS].set(a_log)
    col, row = even_gates(zs.reshape(B, S, LANES), par)
    row = row.reshape(B, 16, 1, S)
    z = z.reshape(B, S, -1)
    fox = fox_attention(z, col, q_norm, k_norm)
    gdn = gdn_mixer(z, conv_w, col, row, o_norm)
    w_out = w_out.astype(BF16)
    return matmul_residual(h, [(fox.reshape(M, fw), w_out[:fw]), (gdn.reshape(M, gw), w_out[fw:])], tm=ROW_TILE, tn=1024)


def _odd_mix(h, B, S, norm1, w_qkv, q_norm, k_norm, w_out):
    M, D = h.shape
    z = norm_matmul(h, norm1, w_qkv.astype(BF16), tm=ROW_TILE, tn=1024).reshape(B, S, -1)
    half = HEAD_DIM // 2
    inv = jnp.power(ROPE_THETA, -jnp.arange(half, dtype=F32) / half)
    ang = jnp.arange(S, dtype=F32)[:, None] * inv[None, :]
    cos, sin = jnp.cos(ang), jnp.sin(ang)
    cos_full = jnp.concatenate([cos, cos], axis=-1)
    sin_signed = jnp.concatenate([-sin, sin], axis=-1)
    att = moba_attention(z, cos_full, sin_signed, q_norm, k_norm)
    return matmul_residual(h, [(att.reshape(M, -1), w_out.astype(BF16))], tm=ROW_TILE, tn=1024)


def _odd_moe(h, norm2, w_router, w_gate_up, w_down):
    return moe_residual(h, norm2, w_router, w_gate_up.astype(BF16), w_down.astype(BF16))


def kernel(x, e_norm1, e_w_in, e_fox_f_bias, e_fox_q_norm, e_fox_k_norm, e_gdn_conv, e_gdn_a_log,
           e_gdn_dt_bias, e_gdn_o_norm, e_w_out, e_norm2, e_ffn_w_gate_up, e_ffn_w_down,
           o_norm1, o_w_qkv, o_q_norm, o_k_norm, o_w_out, o_norm2, o_router, o_exp_w_gate_up, o_exp_w_down):
    B, S, D = x.shape
    h = x.reshape(B * S, D)
    depth = e_norm1.shape[0] + o_norm1.shape[0]
    for layer in range(depth):
        i = layer // 2
        if layer % 2 == 0:
            h = _even_mix(h, B, S, e_norm1[i], e_w_in[i], e_fox_f_bias[i], e_fox_q_norm[i], e_fox_k_norm[i],
                          e_gdn_conv[i], e_gdn_a_log[i], e_gdn_dt_bias[i], e_gdn_o_norm[i], e_w_out[i])
            h = ffn_residual(h, e_norm2[i], e_ffn_w_gate_up[i].astype(BF16), e_ffn_w_down[i].astype(BF16),
                             tm=ROW_TILE // 2, tf=1408)
        else:
            h = _odd_mix(h, B, S, o_norm1[i], o_w_qkv[i], o_q_norm[i], o_k_norm[i], o_w_out[i])
            h = _odd_moe(h, o_norm2[i], o_router[i], o_exp_w_gate_up[i], o_exp_w_down[i])
    return h.reshape(B, S, D)
```

```python
import functools

import jax
import jax.numpy as jnp
from jax import lax
from jax.experimental import pallas as pl
from jax.experimental.pallas import tpu as pltpu

F32 = jnp.float32
BF16 = jnp.bfloat16

HEAD_DIM = 128
FOX_HEADS = 4
GDN_HEADS = 4
CONV_WIDTH = 4
MOBA_BLOCK = 256
MOBA_TOPK = 3
N_EXPERTS = 8
ROPE_THETA = 10000.0
EPS = 1e-6

LANES = 128
GDN_CHUNK = 128
INV_BLOCK = 16
GDN_HEADS_PER_STEP = 4
NEG = -(2.0 ** 100)
LOG2E = 1.4426950408889634
ATT_BLOCK = 256
ATT_HEADS = 2
V_ROWS = HEAD_DIM + 16
PERM_TILE = 512
ROW_TILE = 1024
EXPERT_TILE = 1024
VMEM_LIMIT_BYTES = 56 * 1024 * 1024

FOX_Q0, FOX_K0, FOX_V0 = 0, 4, 8
GDN_Q0, GDN_K0, GDN_V0, GDN_G0 = 12, 16, 20, 24
LANE_F, LANE_A, LANE_B = 0, 4, 8


def _params(*sem):
    return pltpu.CompilerParams(dimension_semantics=sem, vmem_limit_bytes=VMEM_LIMIT_BYTES)


def _rms(x, gain):
    return x * lax.rsqrt(jnp.mean(x * x, axis=-1, keepdims=True) + EPS) * gain


def _dot_nt(a, b, **kw):
    return lax.dot_general(a, b, (((1,), (1,)), ((), ())), preferred_element_type=F32, **kw)


def _pick_lane(x, lane_idx):
    lane = lax.broadcasted_iota(jnp.int32, x.shape, 1)
    return jnp.sum(jnp.where(lane == lane_idx, x, 0.0), axis=-1, keepdims=True)


def _norm_mm_kernel(x_ref, g_ref, w_ref, *rest, has_aux):
    if has_aux:
        waux_ref, o_ref, oaux_ref, xn_ref = rest
    else:
        o_ref, xn_ref = rest

    @pl.when(pl.program_id(1) == 0)
    def _():
        xn = _rms(x_ref[...], g_ref[...]).astype(BF16)
        xn_ref[...] = xn
        if has_aux:
            oaux_ref[...] = jnp.dot(xn, waux_ref[...], preferred_element_type=F32)

    o_ref[...] = jnp.dot(xn_ref[...], w_ref[...], preferred_element_type=F32).astype(o_ref.dtype)


def norm_matmul(x, gain, w, w_aux=None, *, tm, tn, out_dtype=BF16):
    M, K = x.shape
    N = w.shape[1]
    has_aux = w_aux is not None
    in_specs = [
        pl.BlockSpec((tm, K), lambda i, j: (i, 0)),
        pl.BlockSpec((1, K), lambda i, j: (0, 0)),
        pl.BlockSpec((K, tn), lambda i, j: (0, j)),
    ]
    out_shape = [jax.ShapeDtypeStruct((M, N), out_dtype)]
    out_specs = [pl.BlockSpec((tm, tn), lambda i, j: (i, j))]
    args = [x, gain.reshape(1, K), w]
    if has_aux:
        in_specs.append(pl.BlockSpec((K, LANES), lambda i, j: (0, 0)))
        out_shape.append(jax.ShapeDtypeStruct((M, LANES), F32))
        out_specs.append(pl.BlockSpec((tm, LANES), lambda i, j: (i, 0)))
        args.append(w_aux)
    res = pl.pallas_call(
        functools.partial(_norm_mm_kernel, has_aux=has_aux),
        grid=(M // tm, N // tn),
        in_specs=in_specs,
        out_specs=out_specs,
        out_shape=out_shape,
        scratch_shapes=[pltpu.VMEM((tm, K), BF16)],
        compiler_params=_params("parallel", "arbitrary"),
        name="norm_matmul",
    )(*args)
    return res if has_aux else res[0]


def _mm_res_kernel(*refs, n_in):
    res_ref = refs[0]
    o_ref = refs[1 + 2 * n_in]
    acc = res_ref[...]
    for t in range(n_in):
        acc = acc + jnp.dot(refs[1 + 2 * t][...], refs[2 + 2 * t][...], preferred_element_type=F32)
    o_ref[...] = acc


def matmul_residual(res, pairs, *, tm, tn):
    M, N = res.shape
    in_specs = [pl.BlockSpec((tm, tn), lambda i, j: (i, j))]
    args = [res]
    for a, w in pairs:
        K = a.shape[1]
        in_specs.append(pl.BlockSpec((tm, K), lambda i, j: (i, 0)))
        in_specs.append(pl.BlockSpec((K, tn), lambda i, j: (0, j)))
        args += [a, w]
    return pl.pallas_call(
        functools.partial(_mm_res_kernel, n_in=len(pairs)),
        grid=(M // tm, N // tn),
        in_specs=in_specs,
        out_specs=pl.BlockSpec((tm, tn), lambda i, j: (i, j)),
        out_shape=jax.ShapeDtypeStruct((M, N), F32),
        compiler_params=_params("parallel", "arbitrary"),
        name="matmul_residual",
    )(*args)


def _gate_kernel(zs_ref, par_ref, col_ref, row_ref, *, S):
    C = GDN_CHUNK
    bias = par_ref[0:1, :]
    neg_a = -jnp.exp(par_ref[1:2, :])
    r = lax.broadcasted_iota(jnp.int32, (C, C), 0)
    c = lax.broadcasted_iota(jnp.int32, (C, C), 1)
    tril = (r >= c).astype(F32)
    lane = lax.broadcasted_iota(jnp.int32, (C, LANES), 1)

    def body(n, carry):
        sl = pl.ds(pl.multiple_of(n * C, C), C)
        z = zs_ref[sl, :]
        t = z + bias
        soft = jnp.log(1.0 + jnp.exp(-jnp.abs(t)))
        log_f = jnp.minimum(t, 0.0) - soft
        g = neg_a * (jnp.maximum(t, 0.0) + soft)
        beta = 1.0 / (1.0 + jnp.exp(-z))
        u = jnp.where(lane < LANE_A, log_f, jnp.where(lane < LANE_B, g, 0.0))
        cs = jnp.dot(tril, u, preferred_element_type=F32, precision=lax.Precision.HIGHEST)
        cs = cs + jnp.where(lane < LANE_A, carry, 0.0)
        out = jnp.where(lane < LANE_B, cs, beta)
        col_ref[sl, :] = out
        row_ref[:, sl] = out.T[0:16, :]
        return cs[C - 1:C, :]

    lax.fori_loop(0, S // C, body, jnp.zeros((1, LANES), F32))


def even_gates(zs, par):
    B, S, _ = zs.shape
    return pl.pallas_call(
        functools.partial(_gate_kernel, S=S),
        grid=(B,),
        in_specs=[
            pl.BlockSpec((None, S, LANES), lambda b: (b, 0, 0)),
            pl.BlockSpec((8, LANES), lambda b: (0, 0)),
        ],
        out_specs=[
            pl.BlockSpec((None, S, LANES), lambda b: (b, 0, 0)),
            pl.BlockSpec((None, 16, S), lambda b: (b, 0, 0)),
        ],
        out_shape=[jax.ShapeDtypeStruct((B, S, LANES), F32), jax.ShapeDtypeStruct((B, 16, S), F32)],
        compiler_params=_params("parallel"),
        name="even_gates",
    )(zs, par)


def _split3(x):
    hi = x.astype(BF16).astype(F32)
    mid = (x - hi).astype(BF16).astype(F32)
    lo = (x - hi - mid).astype(BF16).astype(F32)
    return hi, mid, lo


def _interleave(gens):
    out = [None] * len(gens)
    live = list(range(len(gens)))
    while live:
        for n in list(live):
            try:
                next(gens[n])
            except StopIteration as stop:
                out[n] = stop.value
                live.remove(n)
    return out


def _attend(state, c, qa, ka_ref, vt_ref, key0, nkeys, keep=None):
    ks = pl.ds(pl.multiple_of(key0, ATT_BLOCK), nkeys)
    st = _dot_nt(ka_ref[ks, :], qa)
    yield
    if keep is not None:
        st = jnp.where(keep, st, NEG)
    m_new = jnp.max(st, axis=0, keepdims=True)
    if state[c] is not None:
        m_old, acc_old = state[c]
        m_new = jnp.maximum(m_old, m_new)
    p = jnp.exp2(st - m_new).astype(BF16)
    pv = jnp.dot(vt_ref[:, ks], p, preferred_element_type=F32)
    state[c] = (m_new, pv if state[c] is None else acc_old * jnp.exp2(m_old - m_new) + pv)


def _attend_tile_pair(i, prep, ka_ref, vt_ref, m_ref, acc_ref, o_ref):
    t = ATT_BLOCK
    chains = [(hd, c) for c in range(2) for hd in range(ATT_HEADS)]
    num = lambda hd, c: 2 * hd + c
    qa = _interleave(prep)

    def step(state, hd, c, which, **kw):
        return _attend(state, num(hd, c), qa[num(hd, c)][which], ka_ref.at[hd], vt_ref.at[hd], **kw)

    def save(state):
        m_ref[...] = jnp.stack([state[n][0] for n in range(len(chains))])
        acc_ref[...] = jnp.stack([state[n][1] for n in range(len(chains))])

    def past_blocks(key0, n_steps):
        state = {n: (m_ref[n], acc_ref[n]) for n in range(len(chains))}
        _interleave([step(state, hd, c, 1, key0=key0 + s * 2 * t, nkeys=2 * t)
                     for s in range(n_steps) for hd, c in chains])
        save(state)

    state = {n: None for n in range(len(chains))}
    keep = _causal_keep()
    _interleave([step(state, hd, c, 0, key0=(2 * i + c) * t, nkeys=t, keep=keep) for hd, c in chains]
                + [step(state, hd, 1, 1, key0=(2 * i) * t, nkeys=t) for hd in range(ATT_HEADS)])
    save(state)

    def four_blocks(g, _):
        past_blocks(g * (4 * t), 2)
        return 0

    lax.fori_loop(0, i // 2, four_blocks, 0)

    @pl.when((i & 1) != 0)
    def _():
        past_blocks((i // 2) * (4 * t), 1)

    for hd, c in chains:
        acc = acc_ref[num(hd, c)]
        out_t = acc[:HEAD_DIM, :] * (1.0 / acc[HEAD_DIM:HEAD_DIM + 1, :])
        o_ref[c * t:(c + 1) * t, hd * HEAD_DIM:(hd + 1) * HEAD_DIM] = out_t.T.astype(o_ref.dtype)


def _store_vt(vt_ref, v, sl):
    vt_ref[0:HEAD_DIM, sl] = v.astype(F32).T.astype(BF16)
    vt_ref[HEAD_DIM:V_ROWS, sl] = jnp.ones((V_ROWS - HEAD_DIM, ATT_BLOCK), BF16)


def _causal_keep():
    key = lax.broadcasted_iota(jnp.int32, (ATT_BLOCK, ATT_BLOCK), 0)
    qry = lax.broadcasted_iota(jnp.int32, (ATT_BLOCK, ATT_BLOCK), 1)
    return key <= qry


_ATT_SCRATCH = lambda S: [pltpu.VMEM((ATT_HEADS, S, 2 * HEAD_DIM), BF16), pltpu.VMEM((ATT_HEADS, V_ROWS, S), BF16),
                          pltpu.VMEM((2 * ATT_HEADS, 1, ATT_BLOCK), F32),
                          pltpu.VMEM((2 * ATT_HEADS, V_ROWS, ATT_BLOCK), F32)]


def _fox_kernel(q_ref, k_ref, v_ref, col_ref, qg_ref, kg_ref, o_ref, ka_ref, vt_ref, m_ref, acc_ref, *, S):
    h0 = pl.program_id(1) * ATT_HEADS
    i = pl.program_id(2)
    t = ATT_BLOCK
    D = HEAD_DIM
    lane = lax.broadcasted_iota(jnp.int32, (t, LANES), 1)

    @pl.when(i == 0)
    def _():
        def prep_keys(n, _):
            sl = pl.ds(pl.multiple_of(n * t, t), t)
            for hd in range(ATT_HEADS):
                cols = slice(hd * D, (hd + 1) * D)
                ka_ref[hd, sl, 0:D] = _rms(k_ref[sl, cols].astype(F32), kg_ref[...]).astype(BF16)
                hi, mid, lo = _split3(-LOG2E * _pick_lane(col_ref[sl, :], LANE_F + h0 + hd))
                ka_ref[hd, sl, D:] = jnp.where(
                    lane < 3, 1.0, jnp.where(lane == 3, hi, jnp.where(lane == 4, mid, jnp.where(lane == 5, lo, 0.0)))
                ).astype(BF16)
                _store_vt(vt_ref.at[hd], v_ref[sl, cols], sl)
            return 0

        lax.fori_loop(0, S // t, prep_keys, 0)

    def prep(hd, c):
        q = q_ref[c * t:(c + 1) * t, hd * D:(hd + 1) * D].astype(F32)
        qn = (_rms(q, qg_ref[...]) * (LOG2E * D ** -0.5)).astype(BF16)
        yield
        qsl = pl.ds(pl.multiple_of((2 * i + c) * t, t), t)
        hi, mid, lo = _split3(LOG2E * _pick_lane(col_ref[qsl, :], LANE_F + h0 + hd))
        aug = jnp.where(lane == 0, hi,
                        jnp.where(lane == 1, mid, jnp.where(lane == 2, lo, jnp.where(lane < 6, 1.0, 0.0))))
        qa = jnp.concatenate([qn, aug.astype(BF16)], axis=-1)
        return qa, qa

    _attend_tile_pair(i, [prep(hd, c) for hd in range(ATT_HEADS) for c in range(2)],
                      ka_ref, vt_ref, m_ref, acc_ref, o_ref)


def fox_attention(z, col, q_gain, k_gain):
    B, S, _ = z.shape
    t = 2 * ATT_BLOCK
    w = ATT_HEADS * HEAD_DIM
    return pl.pallas_call(
        functools.partial(_fox_kernel, S=S),
        grid=(B, FOX_HEADS // ATT_HEADS, S // t),
        in_specs=[
            pl.BlockSpec((None, t, w), lambda b, h, i: (b, i, FOX_Q0 // ATT_HEADS + h)),
            pl.BlockSpec((None, S, w), lambda b, h, i: (b, 0, FOX_K0 // ATT_HEADS + h)),
            pl.BlockSpec((None, S, w), lambda b, h, i: (b, 0, FOX_V0 // ATT_HEADS + h)),
            pl.BlockSpec((None, S, LANES), lambda b, h, i: (b, 0, 0)),
            pl.BlockSpec((1, HEAD_DIM), lambda b, h, i: (0, 0)),
            pl.BlockSpec((1, HEAD_DIM), lambda b, h, i: (0, 0)),
        ],
        out_specs=pl.BlockSpec((None, t, w), lambda b, h, i: (b, i, h)),
        out_shape=jax.ShapeDtypeStruct((B, S, FOX_HEADS * HEAD_DIM), BF16),
        scratch_shapes=_ATT_SCRATCH(S),
        compiler_params=_params("parallel", "parallel", "arbitrary"),
        name="fox_attention",
    )(z, z, z, col, q_gain.reshape(1, -1), k_gain.reshape(1, -1))


def _unit_lower_inverse(m):
    n = m.shape[0]
    r = lax.broadcasted_iota(jnp.int32, (n, n), 0)
    c = lax.broadcasted_iota(jnp.int32, (n, n), 1)
    eye = (r == c).astype(F32)

    def same_block(b):
        return (r // b) == (c // b)

    p = jnp.where(same_block(INV_BLOCK), m, 0.0)
    inv = eye - p
    k = 2
    while k < INV_BLOCK:
        pb = p.astype(BF16)
        p = jnp.dot(pb, pb, preferred_element_type=F32)
        yield
        inv = jnp.dot(inv.astype(BF16), (eye + p).astype(BF16), preferred_element_type=F32)
        yield
        k *= 2
    b = INV_BLOCK
    while b < n:
        off = jnp.where(same_block(2 * b), jnp.where(same_block(b), 0.0, m), 0.0).astype(BF16)
        ib = inv.astype(BF16)
        left = jnp.dot(ib, off, preferred_element_type=F32).astype(BF16)
        yield
        inv = inv - jnp.dot(left, ib, preferred_element_type=F32)
        yield
        b *= 2
    return inv


def _gdn_kernel(q_ref, k_ref, v_ref, gg_ref, wq_ref, wk_ref, wv_ref, col_ref, row_ref, on_ref,
                o_ref, pad_ref, qs_ref, ks_ref, vs_ref, *, S, rows, hb):
    h0 = pl.program_id(1) * hb
    C = GDN_CHUNK
    D = HEAD_DIM

    def conv_silu(x_ref, w_ref, dst_ref, hh, mode):
        cols = slice(hh * D, (hh + 1) * D)
        pad_ref[0:8, :] = jnp.zeros((8, D), F32)

        def fill(n, _):
            src = pl.ds(pl.multiple_of(n * rows, rows), rows)
            pad_ref[pl.ds(pl.multiple_of(8 + n * rows, 8), rows), :] = x_ref[src, cols].astype(F32)
            return 0

        lax.fori_loop(0, S // rows, fill, 0)
        w = w_ref[:, cols]

        def conv(n, _):
            base = pl.multiple_of(n * rows, rows)
            win = pad_ref[pl.ds(base, rows + 8), :]
            y = jnp.zeros((rows, D), F32)
            for tap in range(CONV_WIDTH):
                lead = 8 - (CONV_WIDTH - 1) + tap
                y = y + w[tap:tap + 1, :] * pltpu.roll(win, rows + 8 - lead, 0)[0:rows, :]
            y = y / (1.0 + jnp.exp(-y))
            if mode != "v":
                y = y * lax.rsqrt(jnp.sum(y * y, axis=-1, keepdims=True) + EPS)
            if mode == "q":
                y = y * D ** -0.5
            dst_ref[hh, pl.ds(base, rows), :] = y.astype(dst_ref.dtype)
            return 0

        lax.fori_loop(0, S // rows, conv, 0)

    for hh in range(hb):
        conv_silu(q_ref, wq_ref, qs_ref, hh, "q")
        conv_silu(k_ref, wk_ref, ks_ref, hh, "k")
        conv_silu(v_ref, wv_ref, vs_ref, hh, "v")

    r = lax.broadcasted_iota(jnp.int32, (C, C), 0)
    c = lax.broadcasted_iota(jnp.int32, (C, C), 1)
    incl = r >= c
    strict = r > c

    def head_chunk(hh, sl, tab, state):
        q = qs_ref[hh, sl, :].astype(F32)
        k = ks_ref[hh, sl, :].astype(F32)
        v = vs_ref[hh, sl, :].astype(F32)
        gcol = _pick_lane(tab, LANE_A + h0 + hh)
        beta = _pick_lane(tab, LANE_B + h0 + hh)
        grow = row_ref[hh, :, sl]
        glast = gcol[C - 1:C, :]
        decay = jnp.where(incl, jnp.exp(jnp.where(incl, gcol - grow, 0.0)), 0.0)
        eg = jnp.exp(gcol)
        kb = k * beta
        kbf = k.astype(BF16)
        m = jnp.where(strict, _dot_nt(kb.astype(BF16), kbf) * decay, 0.0)
        attn = _dot_nt(q.astype(BF16), kbf) * decay
        yield
        tinv = (yield from _unit_lower_inverse(m)).astype(BF16)
        rhs = jnp.concatenate([v * beta, kb * eg], axis=-1).astype(BF16)
        sol = jnp.dot(tinv, rhs, preferred_element_type=F32)
        yield
        u = sol[:, :D]
        w = sol[:, D:]
        qg = q * eg
        kg = k * jnp.exp(glast - gcol)
        sb = state.astype(BF16)
        v_new = u - jnp.dot(w.astype(BF16), sb, preferred_element_type=F32)
        o_state = jnp.dot(qg.astype(BF16), sb, preferred_element_type=F32)
        yield
        vb = v_new.astype(BF16)
        o = o_state + jnp.dot(attn.astype(BF16), vb, preferred_element_type=F32)
        state = state * jnp.exp(glast) + jnp.dot(kg.T.astype(BF16), vb, preferred_element_type=F32)
        yield
        gate = gg_ref[sl, hh * D:(hh + 1) * D].astype(F32)
        return state, (_rms(o, on_ref[...]) * (gate / (1.0 + jnp.exp(-gate)))).astype(o_ref.dtype)

    def chunk(n, states):
        sl = pl.ds(pl.multiple_of(n * C, C), C)
        tab = col_ref[sl, :]
        res = _interleave([head_chunk(hh, sl, tab, states[hh]) for hh in range(hb)])
        o_ref[sl, :] = jnp.concatenate([o for _, o in res], axis=-1)
        return tuple(s for s, _ in res)

    lax.fori_loop(0, S // C, chunk, tuple(jnp.zeros((D, D), F32) for _ in range(hb)))


def gdn_mixer(z, conv_w, col, row, o_gain):
    B, S, _ = z.shape
    D = HEAD_DIM
    hb = GDN_HEADS_PER_STEP
    rows = 256
    seq = lambda off: pl.BlockSpec((None, S, hb * D), lambda b, h: (b, 0, off // hb + h),
                                   pipeline_mode=pl.Buffered(1))
    cw = lambda off: pl.BlockSpec((CONV_WIDTH, hb * D), lambda b, h: (0, off // hb + h))
    return pl.pallas_call(
        functools.partial(_gdn_kernel, S=S, rows=rows, hb=hb),
        grid=(B, GDN_HEADS // hb),
        in_specs=[
            seq(GDN_Q0), seq(GDN_K0), seq(GDN_V0), seq(GDN_G0),
            cw(0), cw(GDN_HEADS), cw(2 * GDN_HEADS),
            pl.BlockSpec((None, S, LANES), lambda b, h: (b, 0, 0)),
            pl.BlockSpec((None, hb, 1, S), lambda b, h: (b, LANE_A // hb + h, 0, 0)),
            pl.BlockSpec((1, D), lambda b, h: (0, 0)),
        ],
        out_specs=pl.BlockSpec((None, S, hb * D), lambda b, h: (b, 0, h)),
        out_shape=jax.ShapeDtypeStruct((B, S, GDN_HEADS * D), BF16),
        scratch_shapes=[pltpu.VMEM((S + 8, D), F32)] + [pltpu.VMEM((hb, S, D), BF16)] * 3,
        compiler_params=_params("parallel", "parallel"),
        name="gdn_mixer",
    )(z, z, z, z, conv_w, conv_w, conv_w, col, row, o_gain.reshape(1, D))


def _rope(x, cos, sin_signed):
    return x * cos + pltpu.roll(x, HEAD_DIM // 2, 1) * sin_signed


def _moba_kernel(q_ref, k_ref, v_ref, cos_ref, sin_ref, qg_ref, kg_ref, o_ref,
                 ka_ref, vt_ref, m_ref, acc_ref, kmean_ref, *, S):
    i = pl.program_id(2)
    t = ATT_BLOCK
    D = HEAD_DIM
    lane = lax.broadcasted_iota(jnp.int32, (t, LANES), 1)

    @pl.when(i == 0)
    def _():
        kmean_ref[...] = jnp.zeros(kmean_ref.shape, F32)

        def prep_keys(n, _):
            sl = pl.ds(pl.multiple_of(n * t, t), t)
            for hd in range(ATT_HEADS):
                cols = slice(hd * D, (hd + 1) * D)
                k = _rope(_rms(k_ref[sl, cols].astype(F32), kg_ref[...]), cos_ref[sl, :], sin_ref[sl, :])
                ka_ref[hd, sl, 0:D] = k.astype(BF16)
                ka_ref[hd, sl, D:] = jnp.where(lane == n, 1.0, 0.0).astype(BF16)
                kmean_ref[hd, pl.ds(n, 1), :] = jnp.mean(k, axis=0, keepdims=True)
                _store_vt(vt_ref.at[hd], v_ref[sl, cols], sl)
            return 0

        lax.fori_loop(0, S // t, prep_keys, 0)

    lane_f = lane.astype(F32)

    def prep(hd, c):
        cur = 2 * i + c
        qsl = pl.ds(pl.multiple_of(cur * t, t), t)
        q = _rope(_rms(q_ref[c * t:(c + 1) * t, hd * D:(hd + 1) * D].astype(F32), qg_ref[...]),
                  cos_ref[qsl, :], sin_ref[qsl, :])
        yield
        kmean = kmean_ref[hd]
        km_hi = kmean.astype(BF16)
        km_split = jnp.concatenate([km_hi, (kmean - km_hi.astype(F32)).astype(BF16)], axis=0)
        q_hi = q.astype(BF16)
        q_lo = (q - q_hi.astype(F32)).astype(BF16)
        part = _dot_nt(q_hi, km_split)
        gate = part[:, :LANES] + part[:, LANES:] + _dot_nt(q_lo, km_split[:LANES, :])
        gate = jnp.where(lane < cur, gate, -jnp.inf)
        yield
        sel_bias = jnp.full((t, LANES), NEG, F32)
        for _ in range(MOBA_TOPK):
            top = jnp.max(gate, axis=-1, keepdims=True)
            yield
            first = jnp.min(jnp.where(gate == top, lane_f, float(LANES)), axis=-1, keepdims=True)
            yield
            pick = lane_f == first
            sel_bias = jnp.where(pick & (first < cur.astype(F32)), 0.0, sel_bias)
            gate = jnp.where(pick, -jnp.inf, gate)
        qs = (q * (LOG2E * D ** -0.5)).astype(BF16)
        return (jnp.concatenate([qs, jnp.zeros((t, LANES), BF16)], axis=-1),
                jnp.concatenate([qs, sel_bias.astype(BF16)], axis=-1))

    _attend_tile_pair(i, [prep(hd, c) for hd in range(ATT_HEADS) for c in range(2)],
                      ka_ref, vt_ref, m_ref, acc_ref, o_ref)


def moba_attention(z, cos, sin_signed, q_gain, k_gain):
    B, S, W = z.shape
    H = W // (3 * HEAD_DIM)
    assert ATT_BLOCK == MOBA_BLOCK
    t = 2 * ATT_BLOCK
    w = ATT_HEADS * HEAD_DIM
    hp = H // ATT_HEADS
    return pl.pallas_call(
        functools.partial(_moba_kernel, S=S),
        grid=(B, hp, S // t),
        in_specs=[
            pl.BlockSpec((None, t, w), lambda b, h, i: (b, i, h)),
            pl.BlockSpec((None, S, w), lambda b, h, i: (b, 0, hp + h)),
            pl.BlockSpec((None, S, w), lambda b, h, i: (b, 0, 2 * hp + h)),
            pl.BlockSpec((S, HEAD_DIM), lambda b, h, i: (0, 0)),
            pl.BlockSpec((S, HEAD_DIM), lambda b, h, i: (0, 0)),
            pl.BlockSpec((1, HEAD_DIM), lambda b, h, i: (0, 0)),
            pl.BlockSpec((1, HEAD_DIM), lambda b, h, i: (0, 0)),
        ],
        out_specs=pl.BlockSpec((None, t, w), lambda b, h, i: (b, i, h)),
        out_shape=jax.ShapeDtypeStruct((B, S, H * HEAD_DIM), BF16),
        scratch_shapes=_ATT_SCRATCH(S) + [pltpu.VMEM((ATT_HEADS, LANES, HEAD_DIM), F32)],
        compiler_params=_params("parallel", "parallel", "arbitrary"),
        name="moba_attention",
    )(z, z, z, cos, sin_signed, q_gain.reshape(1, -1), k_gain.reshape(1, -1))


META_IDX, META_RANK, META_GATE = 0, 2, 4


def _router_kernel(x_ref, g_ref, w_ref, xn_ref, meta_ref, rows_ref, base_ref, cnt_ref, carry_ref):
    i = pl.program_id(0)
    tm = x_ref.shape[0]

    @pl.when(i == 0)
    def _():
        carry_ref[...] = jnp.zeros(carry_ref.shape, F32)

    xn = _rms(x_ref[...], g_ref[...])
    xn_ref[...] = xn.astype(BF16)
    logits = jnp.dot(xn, w_ref[...], preferred_element_type=F32, precision=lax.Precision.HIGHEST)
    lane = lax.broadcasted_iota(jnp.int32, logits.shape, 1)
    logits = jnp.where(lane < N_EXPERTS, logits, -jnp.inf)
    top1 = jnp.max(logits, axis=-1, keepdims=True)
    lane_f = lane.astype(F32)
    idx1 = jnp.min(jnp.where(logits == top1, lane_f, float(LANES)), axis=-1, keepdims=True)
    rest = jnp.where(lane_f == idx1, -jnp.inf, logits)
    top2 = jnp.max(rest, axis=-1, keepdims=True)
    idx2 = jnp.min(jnp.where(rest == top2, lane_f, float(LANES)), axis=-1, keepdims=True)
    e2 = jnp.exp(top2 - top1)
    denom = 1.0 + e2
    chosen = jnp.where((lane_f == idx1) | (lane_f == idx2), 1.0, 0.0)
    r = lax.broadcasted_iota(jnp.int32, (tm, tm), 0)
    c = lax.broadcasted_iota(jnp.int32, (tm, tm), 1)
    ahead = jnp.dot(jnp.where(r > c, 1.0, 0.0).astype(BF16), chosen.astype(BF16), preferred_element_type=F32)
    carry = carry_ref[...]
    rank = ahead + carry
    rank1 = jnp.sum(jnp.where(lane_f == idx1, rank, 0.0), axis=-1, keepdims=True)
    rank2 = jnp.sum(jnp.where(lane_f == idx2, rank, 0.0), axis=-1, keepdims=True)
    vals = (idx1, idx2, rank1, rank2, 1.0 / denom, e2 / denom)
    meta = jnp.zeros(logits.shape, F32)
    for n, v in enumerate(vals):
        meta = jnp.where(lane == n, v, meta)
    meta_ref[...] = meta
    rows_ref[...] = meta.T[0:8, :]
    base_ref[...] = jnp.broadcast_to(carry, base_ref.shape)
    carry = carry + jnp.sum(chosen, axis=0, keepdims=True)
    carry_ref[...] = carry
    cnt_ref[...] = jnp.broadcast_to(carry, cnt_ref.shape)


def moe_router(x, gain, w_router, *, tm):
    M, K = x.shape
    w = jnp.zeros((K, LANES), F32).at[:, :N_EXPERTS].set(w_router)
    return pl.pallas_call(
        _router_kernel,
        grid=(M // tm,),
        in_specs=[
            pl.BlockSpec((tm, K), lambda i: (i, 0)),
            pl.BlockSpec((1, K), lambda i: (0, 0)),
            pl.BlockSpec((K, LANES), lambda i: (0, 0)),
        ],
        out_specs=[
            pl.BlockSpec((tm, K), lambda i: (i, 0)),
            pl.BlockSpec((tm, LANES), lambda i: (i, 0)),
            pl.BlockSpec((None, 8, tm), lambda i: (i, 0, 0)),
            pl.BlockSpec((None, 8, LANES), lambda i: (i, 0, 0)),
            pl.BlockSpec((8, LANES), lambda i: (0, 0)),
        ],
        out_shape=[
            jax.ShapeDtypeStruct((M, K), BF16),
            jax.ShapeDtypeStruct((M, LANES), F32),
            jax.ShapeDtypeStruct((M // tm, 8, tm), F32),
            jax.ShapeDtypeStruct((M // tm, 8, LANES), F32),
            jax.ShapeDtypeStruct((8, LANES), F32),
        ],
        scratch_shapes=[pltpu.VMEM((1, LANES), F32)],
        compiler_params=_params("arbitrary"),
        name="moe_router",
    )(x, gain.reshape(1, K), w)


def _sorted_rows(idx, rank, off_ref):
    first = jnp.zeros(idx.shape, F32)
    for e in range(N_EXPERTS):
        first = jnp.where(idx == float(e), off_ref[e].astype(F32), first)
    return first + rank


def _permute_in_kernel(s_ref, t_ref, flag_ref, off_ref, x_ref, tab_ref, zero_ref, o_ref, acc_ref):
    del t_ref, zero_ref
    k = pl.program_id(0)
    p = PERM_TILE

    @pl.when(flag_ref[3 * k] != 0)
    def _():
        row = (lax.broadcasted_iota(jnp.int32, (p, p), 0) + s_ref[k] * p).astype(F32)
        dest = _sorted_rows(tab_ref[META_IDX:META_IDX + 2, :], tab_ref[META_RANK:META_RANK + 2, :], off_ref)
        hit = (dest[0:1, :] == row) | (dest[1:2, :] == row)
        got = jnp.dot(jnp.where(hit, 1.0, 0.0).astype(BF16), x_ref[...], preferred_element_type=F32)

        @pl.when(flag_ref[3 * k + 1] != 0)
        def _():
            acc_ref[...] = got

        @pl.when(flag_ref[3 * k + 1] == 0)
        def _():
            acc_ref[...] += got

        @pl.when(flag_ref[3 * k + 2] != 0)
        def _():
            o_ref[...] = acc_ref[...].astype(o_ref.dtype)


def permute_in(xn, meta_rows, offsets, items, n_rows):
    M, K = xn.shape
    p = PERM_TILE
    s_item, t_item, flags = items
    grid_spec = pltpu.PrefetchScalarGridSpec(
        num_scalar_prefetch=4,
        grid=(s_item.shape[0],),
        in_specs=[
            pl.BlockSpec((p, K), lambda k, s, t, f, o: (t[k], 0)),
            pl.BlockSpec((None, 8, p), lambda k, s, t, f, o: (t[k], 0, 0)),
            pl.BlockSpec(memory_space=pl.ANY),
        ],
        out_specs=pl.BlockSpec((p, K), lambda k, s, t, f, o: (s[k], 0)),
        scratch_shapes=[pltpu.VMEM((p, K), F32)],
    )
    return pl.pallas_call(
        _permute_in_kernel,
        grid_spec=grid_spec,
        out_shape=jax.ShapeDtypeStruct((n_rows, K), BF16),
        input_output_aliases={6: 0},
        compiler_params=_params("arbitrary"),
        name="moe_permute_in",
    )(s_item, t_item, flags, offsets, xn, meta_rows, jnp.zeros((n_rows, K), BF16))


def _permute_out_kernel(s_ref, t_ref, flag_ref, off_ref, h_ref, y_ref, meta_ref, o_ref):
    del t_ref
    k = pl.program_id(0)
    p = PERM_TILE

    @pl.when(flag_ref[2 * k + 1] != 0)
    def _():
        o_ref[...] = h_ref[...]

    @pl.when(flag_ref[2 * k] != 0)
    def _():
        lo = (s_ref[k] * p).astype(F32)
        row = lax.broadcasted_iota(jnp.int32, (p, p), 1).astype(F32) + lo
        dest = _sorted_rows(meta_ref[:, META_IDX:META_IDX + 2], meta_ref[:, META_RANK:META_RANK + 2], off_ref)
        d1 = dest[:, 0:1]
        d2 = dest[:, 1:2]
        hit = (d1 == row) | (d2 == row)
        got = jnp.dot(jnp.where(hit, 1.0, 0.0).astype(BF16), y_ref[...], preferred_element_type=F32)
        gate = (jnp.where((d1 >= lo) & (d1 < lo + p), meta_ref[:, META_GATE:META_GATE + 1], 0.0)
                + jnp.where((d2 >= lo) & (d2 < lo + p), meta_ref[:, META_GATE + 1:META_GATE + 2], 0.0))
        o_ref[...] += gate * got


def permute_out(h, y, meta, offsets, items):
    M, K = h.shape
    p = PERM_TILE
    s_item, t_item, flags = items
    grid_spec = pltpu.PrefetchScalarGridSpec(
        num_scalar_prefetch=4,
        grid=(s_item.shape[0],),
        in_specs=[
            pl.BlockSpec((p, K), lambda k, s, t, f, o: (t[k], 0)),
            pl.BlockSpec((p, K), lambda k, s, t, f, o: (s[k], 0)),
            pl.BlockSpec((p, LANES), lambda k, s, t, f, o: (t[k], 0)),
        ],
        out_specs=pl.BlockSpec((p, K), lambda k, s, t, f, o: (t[k], 0)),
    )
    return pl.pallas_call(
        _permute_out_kernel,
        grid_spec=grid_spec,
        out_shape=jax.ShapeDtypeStruct((M, K), F32),
        compiler_params=_params("arbitrary"),
        name="moe_permute_out",
    )(s_item, t_item, flags, offsets, h, y, meta)


def _expert_ffn_kernel(te_ref, tv_ref, x_ref, wg_ref, wu_ref, wd_ref, o_ref, acc_ref):
    del te_ref
    i = pl.program_id(0)
    j = pl.program_id(1)
    tm = x_ref.shape[0]
    valid = tv_ref[i]

    def swiglu_rows(rows):
        x = x_ref[0:rows, :]
        gt = jnp.dot(x, wg_ref[...], preferred_element_type=F32)
        up = jnp.dot(x, wu_ref[...], preferred_element_type=F32)
        act = (gt / (1.0 + jnp.exp(-gt)) * up).astype(BF16)
        part = jnp.dot(act, wd_ref[...], preferred_element_type=F32)

        @pl.when(j == 0)
        def _():
            acc_ref[0:rows, :] = part

        @pl.when(j > 0)
        def _():
            acc_ref[0:rows, :] += part

        @pl.when(j == pl.num_programs(1) - 1)
        def _():
            o_ref[0:rows, :] = acc_ref[0:rows, :].astype(o_ref.dtype)
            if rows < tm:
                o_ref[rows:tm, :] = jnp.zeros((tm - rows, o_ref.shape[1]), o_ref.dtype)

    @pl.when(valid > tm // 2)
    def _():
        swiglu_rows(tm)

    @pl.when((valid > 0) & (valid <= tm // 2))
    def _():
        swiglu_rows(tm // 2)

    @pl.when((valid == 0) & (j == 0))
    def _():
        o_ref[...] = jnp.zeros(o_ref.shape, o_ref.dtype)


def expert_ffn(xs, tile_expert, tile_valid, w_gate_up, w_down, *, tm, tf):
    R, K = xs.shape
    E, F, _ = w_down.shape
    nf = F // tf
    live = lambda i, tv: tv[i] > 0
    col = lambda i, j, tv: jnp.where(live(i, tv), j, nf - 1)
    grid_spec = pltpu.PrefetchScalarGridSpec(
        num_scalar_prefetch=2,
        grid=(R // tm, nf),
        in_specs=[
            pl.BlockSpec((tm, K), lambda i, j, te, tv: (te[R // tm + i], 0)),
            pl.BlockSpec((None, K, tf), lambda i, j, te, tv: (te[i], 0, col(i, j, tv))),
            pl.BlockSpec((None, K, tf), lambda i, j, te, tv: (te[i], 0, nf + col(i, j, tv))),
            pl.BlockSpec((None, tf, K), lambda i, j, te, tv: (te[i], col(i, j, tv), 0)),
        ],
        out_specs=pl.BlockSpec((tm, K), lambda i, j, te, tv: (i, 0)),
        scratch_shapes=[pltpu.VMEM((tm, K), F32)],
    )
    return pl.pallas_call(
        _expert_ffn_kernel,
        grid_spec=grid_spec,
        out_shape=jax.ShapeDtypeStruct((R, K), BF16),
        compiler_params=_params("arbitrary", "arbitrary"),
        name="moe_expert_ffn",
    )(tile_expert, tile_valid, xs, w_gate_up, w_gate_up, w_down)


def _compact(valid, n_out, *arrays):
    n = valid.shape[0]
    rank = jnp.cumsum(valid.astype(jnp.int32)) - 1
    n_valid = rank[n - 1] + 1
    slot = jnp.arange(n_out)
    pick = valid[None, :] & (rank[None, :] == jnp.minimum(slot, n_valid - 1)[:, None])
    return slot < n_valid, [jnp.sum(jnp.where(pick, a[None, :], 0), axis=1).astype(jnp.int32) for a in arrays]


def _changes(valid, x):
    prev = jnp.concatenate([jnp.full((1,), -1, x.dtype), x[:-1]])
    nxt = jnp.concatenate([x[1:], jnp.full((1,), -1, x.dtype)])
    nxt_valid = jnp.concatenate([valid[1:], jnp.zeros((1,), bool)])
    return valid & (x != prev), valid & ((x != nxt) | ~nxt_valid)


def moe_residual(h, gain, w_router, w_gate_up, w_down):
    M, K = h.shape
    p = PERM_TILE
    tm = EXPERT_TILE
    ntt = M // p
    n_rows = 2 * M + N_EXPERTS * tm
    xn, meta, meta_rows, base, cnt = moe_router(h, gain, w_router, tm=p)

    counts = cnt[0, :N_EXPERTS].astype(jnp.int32)
    padded = (counts + tm - 1) // tm * tm
    ends = jnp.cumsum(padded)
    offsets = ends - padded
    n_tiles = n_rows // tm
    tile_row = jnp.arange(n_tiles) * tm
    n_used = ends[-1] // tm
    last_used = jnp.minimum(tile_row // tm, n_used - 1)
    expert_of = lambda row: jnp.minimum(jnp.sum(ends[None, :] <= row[:, None], axis=1), N_EXPERTS - 1)
    tile_expert = expert_of(last_used * tm)
    row_in_expert = tile_row - jnp.sum(jnp.where(tile_expert[:, None] == jnp.arange(N_EXPERTS), offsets, 0), axis=1)
    own_count = jnp.sum(jnp.where(tile_expert[:, None] == jnp.arange(N_EXPERTS), counts, 0), axis=1)
    tile_valid = jnp.where(tile_row // tm < n_used, jnp.clip(own_count - row_in_expert, 0, tm), 0)
    tile_tables = jnp.concatenate([tile_expert, last_used]).astype(jnp.int32)

    start = base[:, 0, :N_EXPERTS].astype(jnp.int32)
    stop = jnp.concatenate([start[1:], counts[None, :]], axis=0)
    first = (offsets[None, :] + start) // p
    last = (offsets[None, :] + stop - 1) // p
    nonempty = stop > start
    s_cand = jnp.stack([first, first + 1], axis=-1)
    v_cand = jnp.stack([nonempty, nonempty & (last > first)], axis=-1)
    t_cand = jnp.broadcast_to(jnp.arange(ntt)[:, None, None], s_cand.shape)

    n_items = ntt * N_EXPERTS + n_rows // p - N_EXPERTS

    v_in, (s_in, t_in) = _compact(v_cand.transpose(1, 0, 2).reshape(-1), n_items,
                                  s_cand.transpose(1, 0, 2).reshape(-1), t_cand.transpose(1, 0, 2).reshape(-1))
    first_in, last_in = _changes(v_in, s_in)
    flags_in = jnp.stack([v_in, first_in, last_in], axis=-1).astype(jnp.int32).reshape(-1)
    xs = permute_in(xn, meta_rows, offsets.astype(jnp.int32), (s_in, t_in, flags_in), n_rows)

    ys = expert_ffn(xs, tile_tables, tile_valid.astype(jnp.int32), w_gate_up, w_down, tm=tm, tf=896)

    v_out, (s_out, t_out) = _compact(v_cand.reshape(-1), n_items, s_cand.reshape(-1), t_cand.reshape(-1))
    first_out, _ = _changes(v_out, t_out)
    flags_out = jnp.stack([v_out, first_out], axis=-1).astype(jnp.int32).reshape(-1)
    return permute_out(h, ys, meta, offsets.astype(jnp.int32), (s_out, t_out, flags_out))


def _ffn_kernel(x_ref, g_ref, wg_ref, wu_ref, wd_ref, o_ref, xn_ref, acc_ref):
    j = pl.program_id(1)

    @pl.when(j == 0)
    def _():
        x = x_ref[...]
        xn_ref[...] = _rms(x, g_ref[...]).astype(BF16)
        acc_ref[...] = x

    xn = xn_ref[...]
    gt = jnp.dot(xn, wg_ref[...], preferred_element_type=F32)
    up = jnp.dot(xn, wu_ref[...], preferred_element_type=F32)
    act = gt / (1.0 + jnp.exp(-gt)) * up
    acc_ref[...] += jnp.dot(act.astype(BF16), wd_ref[...], preferred_element_type=F32)

    @pl.when(j == pl.num_programs(1) - 1)
    def _():
        o_ref[...] = acc_ref[...]


def ffn_residual(x, gain, w_gate_up, w_down, *, tm, tf):
    M, K = x.shape
    F = w_down.shape[0]
    nf = F // tf
    return pl.pallas_call(
        _ffn_kernel,
        grid=(M // tm, nf),
        in_specs=[
            pl.BlockSpec((tm, K), lambda i, j: (i, 0)),
            pl.BlockSpec((1, K), lambda i, j: (0, 0)),
            pl.BlockSpec((K, tf), lambda i, j: (0, j)),
            pl.BlockSpec((K, tf), lambda i, j: (0, nf + j)),
            pl.BlockSpec((tf, K), lambda i, j: (j, 0)),
        ],
        out_specs=pl.BlockSpec((tm, K), lambda i, j: (i, 0)),
        out_shape=jax.ShapeDtypeStruct((M, K), F32),
        scratch_shapes=[pltpu.VMEM((tm, K), BF16), pltpu.VMEM((tm, K), F32)],
        compiler_params=_params("parallel", "arbitrary"),
        name="ffn_residual",
    )(x, gain.reshape(1, K), w_gate_up, w_gate_up, w_down)


def _even_mix(h, B, S, norm1, w_in, f_bias, q_norm, k_norm, conv_w, a_log, dt_bias, o_norm, w_out):
    M, D = h.shape
    fw, gw = FOX_HEADS * HEAD_DIM, GDN_HEADS * HEAD_DIM
    o_ff = 3 * fw
    o_gq = o_ff + FOX_HEADS
    o_ga = o_gq + 3 * gw
    o_gb = o_ga + GDN_HEADS
    o_gg = o_gb + GDN_HEADS
    w_big = jnp.concatenate([w_in[:, :o_ff], w_in[:, o_gq:o_ga], w_in[:, o_gg:]], axis=1).astype(BF16)
    w_small = jnp.concatenate([w_in[:, o_ff:o_gq], w_in[:, o_ga:o_gg],
                               jnp.zeros((D, LANES - FOX_HEADS - 2 * GDN_HEADS), F32)], axis=1).astype(BF16)
    z, zs = norm_matmul(h, norm1, w_big, w_small, tm=ROW_TILE, tn=896)
    par = jnp.zeros((8, LANES), F32)
    par = par.at[0, LANE_F:LANE_F + FOX_HEADS].set(f_bias).at[0, LANE_A:LANE_A + GDN_HEADS].set(dt_bias)
    par = par.at[1, LANE_A:LANE_A + GDN_HEADS].set(a_log)
    col, row = even_gates(zs.reshape(B, S, LANES), par)
    row = row.reshape(B, 16, 1, S)
    z = z.reshape(B, S, -1)
    fox = fox_attention(z, col, q_norm, k_norm)
    gdn = gdn_mixer(z, conv_w, col, row, o_norm)
    w_out = w_out.astype(BF16)
    return matmul_residual(h, [(fox.reshape(M, fw), w_out[:fw]), (gdn.reshape(M, gw), w_out[fw:])], tm=ROW_TILE, tn=1024)


def _odd_mix(h, B, S, norm1, w_qkv, q_norm, k_norm, w_out):
    M, D = h.shape
    z = norm_matmul(h, norm1, w_qkv.astype(BF16), tm=ROW_TILE, tn=1024).reshape(B, S, -1)
    half = HEAD_DIM // 2
    inv = jnp.power(ROPE_THETA, -jnp.arange(half, dtype=F32) / half)
    ang = jnp.arange(S, dtype=F32)[:, None] * inv[None, :]
    cos, sin = jnp.cos(ang), jnp.sin(ang)
    cos_full = jnp.concatenate([cos, cos], axis=-1)
    sin_signed = jnp.concatenate([-sin, sin], axis=-1)
    att = moba_attention(z, cos_full, sin_signed, q_norm, k_norm)
    return matmul_residual(h, [(att.reshape(M, -1), w_out.astype(BF16))], tm=ROW_TILE, tn=1024)


def _odd_moe(h, norm2, w_router, w_gate_up, w_down):
    return moe_residual(h, norm2, w_router, w_gate_up.astype(BF16), w_down.astype(BF16))


def kernel(x, e_norm1, e_w_in, e_fox_f_bias, e_fox_q_norm, e_fox_k_norm, e_gdn_conv, e_gdn_a_log,
           e_gdn_dt_bias, e_gdn_o_norm, e_w_out, e_norm2, e_ffn_w_gate_up, e_ffn_w_down,
           o_norm1, o_w_qkv, o_q_norm, o_k_norm, o_w_out, o_norm2, o_router, o_exp_w_gate_up, o_exp_w_down):
    B, S, D = x.shape
    h = x.reshape(B * S, D)
    depth = e_norm1.shape[0] + o_norm1.shape[0]
    for layer in range(depth):
        i = layer // 2
        if layer % 2 == 0:
            h = _even_mix(h, B, S, e_norm1[i], e_w_in[i], e_fox_f_bias[i], e_fox_q_norm[i], e_fox_k_norm[i],
                          e_gdn_conv[i], e_gdn_a_log[i], e_gdn_dt_bias[i], e_gdn_o_norm[i], e_w_out[i])
            h = ffn_residual(h, e_norm2[i], e_ffn_w_gate_up[i].astype(BF16), e_ffn_w_down[i].astype(BF16),
                             tm=ROW_TILE // 2, tf=1408)
        else:
            h = _odd_mix(h, B, S, o_norm1[i], o_w_qkv[i], o_q_norm[i], o_k_norm[i], o_w_out[i])
            h = _odd_moe(h, o_norm2[i], o_router[i], o_exp_w_gate_up[i], o_exp_w_down[i])
    return h.reshape(B, S, D)
```

```python
import functools

import jax
import jax.numpy as jnp
from jax import lax
from jax.experimental import pallas as pl
from jax.experimental.pallas import tpu as pltpu
from jax.experimental.pallas import tpu_sc as plsc

F32 = jnp.float32
BF16 = jnp.bfloat16

HEAD_DIM = 128
FOX_HEADS = 4
GDN_HEADS = 4
CONV_WIDTH = 4
MOBA_BLOCK = 256
MOBA_TOPK = 3
N_EXPERTS = 8
ROPE_THETA = 10000.0
EPS = 1e-6

LANES = 128
GDN_CHUNK = 128
INV_BLOCK = 16
GDN_HEADS_PER_STEP = 4
NEG = -(2.0 ** 100)
LOG2E = 1.4426950408889634
ATT_BLOCK = 256
ATT_HEADS = 2
V_ROWS = HEAD_DIM + 16
PERM_TILE = 512
SC_ROWS = 64
ROW_TILE = 1024
EXPERT_TILE = 1024
VMEM_LIMIT_BYTES = 56 * 1024 * 1024

FOX_Q0, FOX_K0, FOX_V0 = 0, 4, 8
GDN_Q0, GDN_K0, GDN_V0, GDN_G0 = 12, 16, 20, 24
LANE_F, LANE_A, LANE_B = 0, 4, 8


def _params(*sem):
    return pltpu.CompilerParams(dimension_semantics=sem, vmem_limit_bytes=VMEM_LIMIT_BYTES)


def _rms(x, gain):
    return x * lax.rsqrt(jnp.mean(x * x, axis=-1, keepdims=True) + EPS) * gain


def _dot_nt(a, b, **kw):
    return lax.dot_general(a, b, (((1,), (1,)), ((), ())), preferred_element_type=F32, **kw)


def _pick_lane(x, lane_idx):
    lane = lax.broadcasted_iota(jnp.int32, x.shape, 1)
    return jnp.sum(jnp.where(lane == lane_idx, x, 0.0), axis=-1, keepdims=True)


def _pack_bf16_pairs(x):
    n = x.shape[1] // 2
    hi = pltpu.bitcast(x[:, :n].astype(BF16).astype(F32), jnp.uint32)
    lo = pltpu.bitcast(x[:, n:].astype(BF16).astype(F32), jnp.uint32)
    return pltpu.bitcast(hi | (lo >> 16), jnp.int32)


def _unpack_bf16_pairs(w):
    u = pltpu.bitcast(w, jnp.uint32)
    hi = pltpu.bitcast(u & jnp.uint32(0xFFFF0000), F32).astype(BF16)
    lo = pltpu.bitcast(u << 16, F32).astype(BF16)
    return jnp.concatenate([hi, lo], axis=1)


def _norm_mm_kernel(x_ref, g_ref, w_ref, *rest, has_aux):
    if has_aux:
        waux_ref, o_ref, oaux_ref, xn_ref = rest
    else:
        o_ref, xn_ref = rest

    @pl.when(pl.program_id(1) == 0)
    def _():
        xn = _rms(x_ref[...], g_ref[...]).astype(BF16)
        xn_ref[...] = xn
        if has_aux:
            oaux_ref[...] = jnp.dot(xn, waux_ref[...], preferred_element_type=F32)

    o_ref[...] = jnp.dot(xn_ref[...], w_ref[...], preferred_element_type=F32).astype(o_ref.dtype)


def norm_matmul(x, gain, w, w_aux=None, *, tm, tn, out_dtype=BF16):
    M, K = x.shape
    N = w.shape[1]
    has_aux = w_aux is not None
    in_specs = [
        pl.BlockSpec((tm, K), lambda i, j: (i, 0)),
        pl.BlockSpec((1, K), lambda i, j: (0, 0)),
        pl.BlockSpec((K, tn), lambda i, j: (0, j)),
    ]
    out_shape = [jax.ShapeDtypeStruct((M, N), out_dtype)]
    out_specs = [pl.BlockSpec((tm, tn), lambda i, j: (i, j))]
    args = [x, gain.reshape(1, K), w]
    if has_aux:
        in_specs.append(pl.BlockSpec((K, LANES), lambda i, j: (0, 0)))
        out_shape.append(jax.ShapeDtypeStruct((M, LANES), F32))
        out_specs.append(pl.BlockSpec((tm, LANES), lambda i, j: (i, 0)))
        args.append(w_aux)
    res = pl.pallas_call(
        functools.partial(_norm_mm_kernel, has_aux=has_aux),
        grid=(M // tm, N // tn),
        in_specs=in_specs,
        out_specs=out_specs,
        out_shape=out_shape,
        scratch_shapes=[pltpu.VMEM((tm, K), BF16)],
        compiler_params=_params("parallel", "arbitrary"),
        name="norm_matmul",
    )(*args)
    return res if has_aux else res[0]


def _mm_res_kernel(*refs, n_in):
    res_ref = refs[0]
    o_ref = refs[1 + 2 * n_in]
    acc = res_ref[...]
    for t in range(n_in):
        acc = acc + jnp.dot(refs[1 + 2 * t][...], refs[2 + 2 * t][...], preferred_element_type=F32)
    o_ref[...] = acc


def matmul_residual(res, pairs, *, tm, tn):
    M, N = res.shape
    in_specs = [pl.BlockSpec((tm, tn), lambda i, j: (i, j))]
    args = [res]
    for a, w in pairs:
        K = a.shape[1]
        in_specs.append(pl.BlockSpec((tm, K), lambda i, j: (i, 0)))
        in_specs.append(pl.BlockSpec((K, tn), lambda i, j: (0, j)))
        args += [a, w]
    return pl.pallas_call(
        functools.partial(_mm_res_kernel, n_in=len(pairs)),
        grid=(M // tm, N // tn),
        in_specs=in_specs,
        out_specs=pl.BlockSpec((tm, tn), lambda i, j: (i, j)),
        out_shape=jax.ShapeDtypeStruct((M, N), F32),
        compiler_params=_params("parallel", "arbitrary"),
        name="matmul_residual",
    )(*args)


def _gate_kernel(zs_ref, par_ref, col_ref, row_ref, *, S):
    C = GDN_CHUNK
    bias = par_ref[0:1, :]
    neg_a = -jnp.exp(par_ref[1:2, :])
    r = lax.broadcasted_iota(jnp.int32, (C, C), 0)
    c = lax.broadcasted_iota(jnp.int32, (C, C), 1)
    tril = (r >= c).astype(F32)
    lane = lax.broadcasted_iota(jnp.int32, (C, LANES), 1)

    def body(n, carry):
        sl = pl.ds(pl.multiple_of(n * C, C), C)
        z = zs_ref[sl, :]
        t = z + bias
        soft = jnp.log(1.0 + jnp.exp(-jnp.abs(t)))
        log_f = jnp.minimum(t, 0.0) - soft
        g = neg_a * (jnp.maximum(t, 0.0) + soft)
        beta = 1.0 / (1.0 + jnp.exp(-z))
        u = jnp.where(lane < LANE_A, log_f, jnp.where(lane < LANE_B, g, 0.0))
        cs = jnp.dot(tril, u, preferred_element_type=F32, precision=lax.Precision.HIGHEST)
        cs = cs + jnp.where(lane < LANE_A, carry, 0.0)
        out = jnp.where(lane < LANE_B, cs, beta)
        col_ref[sl, :] = out
        row_ref[:, sl] = out.T[0:16, :]
        return cs[C - 1:C, :]

    lax.fori_loop(0, S // C, body, jnp.zeros((1, LANES), F32))


def even_gates(zs, par):
    B, S, _ = zs.shape
    return pl.pallas_call(
        functools.partial(_gate_kernel, S=S),
        grid=(B,),
        in_specs=[
            pl.BlockSpec((None, S, LANES), lambda b: (b, 0, 0)),
            pl.BlockSpec((8, LANES), lambda b: (0, 0)),
        ],
        out_specs=[
            pl.BlockSpec((None, S, LANES), lambda b: (b, 0, 0)),
            pl.BlockSpec((None, 16, S), lambda b: (b, 0, 0)),
        ],
        out_shape=[jax.ShapeDtypeStruct((B, S, LANES), F32), jax.ShapeDtypeStruct((B, 16, S), F32)],
        compiler_params=_params("parallel"),
        name="even_gates",
    )(zs, par)


def _split3(x):
    hi = x.astype(BF16).astype(F32)
    mid = (x - hi).astype(BF16).astype(F32)
    lo = (x - hi - mid).astype(BF16).astype(F32)
    return hi, mid, lo


def _interleave(gens):
    out = [None] * len(gens)
    live = list(range(len(gens)))
    while live:
        for n in list(live):
            try:
                next(gens[n])
            except StopIteration as stop:
                out[n] = stop.value
                live.remove(n)
    return out


def _attend(state, c, qa, ka_ref, vt_ref, key0, nkeys, keep=None):
    ks = pl.ds(pl.multiple_of(key0, ATT_BLOCK), nkeys)
    st = _dot_nt(ka_ref[ks, :], qa)
    yield
    if keep is not None:
        st = jnp.where(keep, st, NEG)
    m_new = jnp.max(st, axis=0, keepdims=True)
    if state[c] is not None:
        m_old, acc_old = state[c]
        m_new = jnp.maximum(m_old, m_new)
    p = jnp.exp2(st - m_new).astype(BF16)
    pv = jnp.dot(vt_ref[:, ks], p, preferred_element_type=F32)
    state[c] = (m_new, pv if state[c] is None else acc_old * jnp.exp2(m_old - m_new) + pv)


def _attend_tile_pair(i, prep, ka_ref, vt_ref, m_ref, acc_ref, o_ref):
    t = ATT_BLOCK
    chains = [(hd, c) for c in range(2) for hd in range(ATT_HEADS)]
    num = lambda hd, c: 2 * hd + c
    qa = _interleave(prep)

    def step(state, hd, c, which, **kw):
        return _attend(state, num(hd, c), qa[num(hd, c)][which], ka_ref.at[hd], vt_ref.at[hd], **kw)

    def save(state):
        m_ref[...] = jnp.stack([state[n][0] for n in range(len(chains))])
        acc_ref[...] = jnp.stack([state[n][1] for n in range(len(chains))])

    def past_blocks(key0, n_steps):
        state = {n: (m_ref[n], acc_ref[n]) for n in range(len(chains))}
        _interleave([step(state, hd, c, 1, key0=key0 + s * 2 * t, nkeys=2 * t)
                     for s in range(n_steps) for hd, c in chains])
        save(state)

    state = {n: None for n in range(len(chains))}
    keep = _causal_keep()
    _interleave([step(state, hd, c, 0, key0=(2 * i + c) * t, nkeys=t, keep=keep) for hd, c in chains]
                + [step(state, hd, 1, 1, key0=(2 * i) * t, nkeys=t) for hd in range(ATT_HEADS)])
    save(state)

    def four_blocks(g, _):
        past_blocks(g * (4 * t), 2)
        return 0

    lax.fori_loop(0, i // 2, four_blocks, 0)

    @pl.when((i & 1) != 0)
    def _():
        past_blocks((i // 2) * (4 * t), 1)

    for hd, c in chains:
        acc = acc_ref[num(hd, c)]
        out_t = acc[:HEAD_DIM, :] * (1.0 / acc[HEAD_DIM:HEAD_DIM + 1, :])
        o_ref[c * t:(c + 1) * t, hd * HEAD_DIM:(hd + 1) * HEAD_DIM] = out_t.T.astype(o_ref.dtype)


def _store_vt(vt_ref, v, sl):
    vt_ref[0:HEAD_DIM, sl] = v.astype(F32).T.astype(BF16)
    vt_ref[HEAD_DIM:V_ROWS, sl] = jnp.ones((V_ROWS - HEAD_DIM, ATT_BLOCK), BF16)


def _causal_keep():
    key = lax.broadcasted_iota(jnp.int32, (ATT_BLOCK, ATT_BLOCK), 0)
    qry = lax.broadcasted_iota(jnp.int32, (ATT_BLOCK, ATT_BLOCK), 1)
    return key <= qry


_ATT_SCRATCH = lambda S: [pltpu.VMEM((ATT_HEADS, S, 2 * HEAD_DIM), BF16), pltpu.VMEM((ATT_HEADS, V_ROWS, S), BF16),
                          pltpu.VMEM((2 * ATT_HEADS, 1, ATT_BLOCK), F32),
                          pltpu.VMEM((2 * ATT_HEADS, V_ROWS, ATT_BLOCK), F32)]


def _fox_kernel(q_ref, k_ref, v_ref, col_ref, qg_ref, kg_ref, o_ref, ka_ref, vt_ref, m_ref, acc_ref, *, S):
    h0 = pl.program_id(1) * ATT_HEADS
    i = pl.program_id(2)
    t = ATT_BLOCK
    D = HEAD_DIM
    lane = lax.broadcasted_iota(jnp.int32, (t, LANES), 1)

    @pl.when(i == 0)
    def _():
        def prep_keys(n, _):
            sl = pl.ds(pl.multiple_of(n * t, t), t)
            for hd in range(ATT_HEADS):
                cols = slice(hd * D, (hd + 1) * D)
                ka_ref[hd, sl, 0:D] = _rms(k_ref[sl, cols].astype(F32), kg_ref[...]).astype(BF16)
                hi, mid, lo = _split3(-LOG2E * _pick_lane(col_ref[sl, :], LANE_F + h0 + hd))
                ka_ref[hd, sl, D:] = jnp.where(
                    lane < 3, 1.0, jnp.where(lane == 3, hi, jnp.where(lane == 4, mid, jnp.where(lane == 5, lo, 0.0)))
                ).astype(BF16)
                _store_vt(vt_ref.at[hd], v_ref[sl, cols], sl)
            return 0

        lax.fori_loop(0, S // t, prep_keys, 0)

    def prep(hd, c):
        q = q_ref[c * t:(c + 1) * t, hd * D:(hd + 1) * D].astype(F32)
        qn = (_rms(q, qg_ref[...]) * (LOG2E * D ** -0.5)).astype(BF16)
        yield
        qsl = pl.ds(pl.multiple_of((2 * i + c) * t, t), t)
        hi, mid, lo = _split3(LOG2E * _pick_lane(col_ref[qsl, :], LANE_F + h0 + hd))
        aug = jnp.where(lane == 0, hi,
                        jnp.where(lane == 1, mid, jnp.where(lane == 2, lo, jnp.where(lane < 6, 1.0, 0.0))))
        qa = jnp.concatenate([qn, aug.astype(BF16)], axis=-1)
        return qa, qa

    _attend_tile_pair(i, [prep(hd, c) for hd in range(ATT_HEADS) for c in range(2)],
                      ka_ref, vt_ref, m_ref, acc_ref, o_ref)


def fox_attention(z, col, q_gain, k_gain):
    B, S, _ = z.shape
    t = 2 * ATT_BLOCK
    w = ATT_HEADS * HEAD_DIM
    return pl.pallas_call(
        functools.partial(_fox_kernel, S=S),
        grid=(B, FOX_HEADS // ATT_HEADS, S // t),
        in_specs=[
            pl.BlockSpec((None, t, w), lambda b, h, i: (b, i, FOX_Q0 // ATT_HEADS + h)),
            pl.BlockSpec((None, S, w), lambda b, h, i: (b, 0, FOX_K0 // ATT_HEADS + h)),
            pl.BlockSpec((None, S, w), lambda b, h, i: (b, 0, FOX_V0 // ATT_HEADS + h)),
            pl.BlockSpec((None, S, LANES), lambda b, h, i: (b, 0, 0)),
            pl.BlockSpec((1, HEAD_DIM), lambda b, h, i: (0, 0)),
            pl.BlockSpec((1, HEAD_DIM), lambda b, h, i: (0, 0)),
        ],
        out_specs=pl.BlockSpec((None, t, w), lambda b, h, i: (b, i, h)),
        out_shape=jax.ShapeDtypeStruct((B, S, FOX_HEADS * HEAD_DIM), BF16),
        scratch_shapes=_ATT_SCRATCH(S),
        compiler_params=_params("parallel", "parallel", "arbitrary"),
        name="fox_attention",
    )(z, z, z, col, q_gain.reshape(1, -1), k_gain.reshape(1, -1))


def _unit_lower_inverse(m):
    n = m.shape[0]
    r = lax.broadcasted_iota(jnp.int32, (n, n), 0)
    c = lax.broadcasted_iota(jnp.int32, (n, n), 1)
    eye = (r == c).astype(F32)

    def same_block(b):
        return (r // b) == (c // b)

    p = jnp.where(same_block(INV_BLOCK), m, 0.0)
    inv = eye - p
    k = 2
    while k < INV_BLOCK:
        pb = p.astype(BF16)
        p = jnp.dot(pb, pb, preferred_element_type=F32)
        yield
        inv = jnp.dot(inv.astype(BF16), (eye + p).astype(BF16), preferred_element_type=F32)
        yield
        k *= 2
    b = INV_BLOCK
    while b < n:
        off = jnp.where(same_block(2 * b), jnp.where(same_block(b), 0.0, m), 0.0).astype(BF16)
        ib = inv.astype(BF16)
        left = jnp.dot(ib, off, preferred_element_type=F32).astype(BF16)
        yield
        inv = inv - jnp.dot(left, ib, preferred_element_type=F32)
        yield
        b *= 2
    return inv


def _gdn_kernel(q_ref, k_ref, v_ref, gg_ref, wq_ref, wk_ref, wv_ref, col_ref, row_ref, on_ref,
                o_ref, pad_ref, qs_ref, ks_ref, vs_ref, *, S, rows, hb):
    h0 = pl.program_id(1) * hb
    C = GDN_CHUNK
    D = HEAD_DIM

    def conv_silu(x_ref, w_ref, dst_ref, hh, mode):
        cols = slice(hh * D, (hh + 1) * D)
        pad_ref[0:8, :] = jnp.zeros((8, D), F32)

        def fill(n, _):
            src = pl.ds(pl.multiple_of(n * rows, rows), rows)
            pad_ref[pl.ds(pl.multiple_of(8 + n * rows, 8), rows), :] = x_ref[src, cols].astype(F32)
            return 0

        lax.fori_loop(0, S // rows, fill, 0)
        w = w_ref[:, cols]

        def conv(n, _):
            base = pl.multiple_of(n * rows, rows)
            win = pad_ref[pl.ds(base, rows + 8), :]
            y = jnp.zeros((rows, D), F32)
            for tap in range(CONV_WIDTH):
                lead = 8 - (CONV_WIDTH - 1) + tap
                y = y + w[tap:tap + 1, :] * pltpu.roll(win, rows + 8 - lead, 0)[0:rows, :]
            y = y / (1.0 + jnp.exp(-y))
            if mode != "v":
                y = y * lax.rsqrt(jnp.sum(y * y, axis=-1, keepdims=True) + EPS)
            if mode == "q":
                y = y * D ** -0.5
            dst_ref[hh, pl.ds(base, rows), :] = y.astype(dst_ref.dtype)
            return 0

        lax.fori_loop(0, S // rows, conv, 0)

    for hh in range(hb):
        conv_silu(q_ref, wq_ref, qs_ref, hh, "q")
        conv_silu(k_ref, wk_ref, ks_ref, hh, "k")
        conv_silu(v_ref, wv_ref, vs_ref, hh, "v")

    r = lax.broadcasted_iota(jnp.int32, (C, C), 0)
    c = lax.broadcasted_iota(jnp.int32, (C, C), 1)
    incl = r >= c
    strict = r > c

    def head_chunk(hh, sl, tab, state):
        q = qs_ref[hh, sl, :].astype(F32)
        k = ks_ref[hh, sl, :].astype(F32)
        v = vs_ref[hh, sl, :].astype(F32)
        gcol = _pick_lane(tab, LANE_A + h0 + hh)
        beta = _pick_lane(tab, LANE_B + h0 + hh)
        grow = row_ref[hh, :, sl]
        glast = gcol[C - 1:C, :]
        decay = jnp.where(incl, jnp.exp(jnp.where(incl, gcol - grow, 0.0)), 0.0)
        eg = jnp.exp(gcol)
        kb = k * beta
        kbf = k.astype(BF16)
        m = jnp.where(strict, _dot_nt(kb.astype(BF16), kbf) * decay, 0.0)
        attn = _dot_nt(q.astype(BF16), kbf) * decay
        yield
        tinv = (yield from _unit_lower_inverse(m)).astype(BF16)
        rhs = jnp.concatenate([v * beta, kb * eg], axis=-1).astype(BF16)
        sol = jnp.dot(tinv, rhs, preferred_element_type=F32)
        yield
        u = sol[:, :D]
        w = sol[:, D:]
        qg = q * eg
        kg = k * jnp.exp(glast - gcol)
        sb = state.astype(BF16)
        v_new = u - jnp.dot(w.astype(BF16), sb, preferred_element_type=F32)
        o_state = jnp.dot(qg.astype(BF16), sb, preferred_element_type=F32)
        yield
        vb = v_new.astype(BF16)
        o = o_state + jnp.dot(attn.astype(BF16), vb, preferred_element_type=F32)
        state = state * jnp.exp(glast) + jnp.dot(kg.T.astype(BF16), vb, preferred_element_type=F32)
        yield
        gate = gg_ref[sl, hh * D:(hh + 1) * D].astype(F32)
        return state, (_rms(o, on_ref[...]) * (gate / (1.0 + jnp.exp(-gate)))).astype(o_ref.dtype)

    def chunk(n, states):
        sl = pl.ds(pl.multiple_of(n * C, C), C)
        tab = col_ref[sl, :]
        res = _interleave([head_chunk(hh, sl, tab, states[hh]) for hh in range(hb)])
        o_ref[sl, :] = jnp.concatenate([o for _, o in res], axis=-1)
        return tuple(s for s, _ in res)

    lax.fori_loop(0, S // C, chunk, tuple(jnp.zeros((D, D), F32) for _ in range(hb)))


def gdn_mixer(z, conv_w, col, row, o_gain):
    B, S, _ = z.shape
    D = HEAD_DIM
    hb = GDN_HEADS_PER_STEP
    rows = 256
    seq = lambda off: pl.BlockSpec((None, S, hb * D), lambda b, h: (b, 0, off // hb + h),
                                   pipeline_mode=pl.Buffered(1))
    cw = lambda off: pl.BlockSpec((CONV_WIDTH, hb * D), lambda b, h: (0, off // hb + h))
    return pl.pallas_call(
        functools.partial(_gdn_kernel, S=S, rows=rows, hb=hb),
        grid=(B, GDN_HEADS // hb),
        in_specs=[
            seq(GDN_Q0), seq(GDN_K0), seq(GDN_V0), seq(GDN_G0),
            cw(0), cw(GDN_HEADS), cw(2 * GDN_HEADS),
            pl.BlockSpec((None, S, LANES), lambda b, h: (b, 0, 0)),
            pl.BlockSpec((None, hb, 1, S), lambda b, h: (b, LANE_A // hb + h, 0, 0)),
            pl.BlockSpec((1, D), lambda b, h: (0, 0)),
        ],
        out_specs=pl.BlockSpec((None, S, hb * D), lambda b, h: (b, 0, h)),
        out_shape=jax.ShapeDtypeStruct((B, S, GDN_HEADS * D), BF16),
        scratch_shapes=[pltpu.VMEM((S + 8, D), F32)] + [pltpu.VMEM((hb, S, D), BF16)] * 3,
        compiler_params=_params("parallel", "parallel"),
        name="gdn_mixer",
    )(z, z, z, z, conv_w, conv_w, conv_w, col, row, o_gain.reshape(1, D))


def _rope(x, cos, sin_signed):
    return x * cos + pltpu.roll(x, HEAD_DIM // 2, 1) * sin_signed


def _moba_kernel(q_ref, k_ref, v_ref, cos_ref, sin_ref, qg_ref, kg_ref, o_ref,
                 ka_ref, vt_ref, m_ref, acc_ref, kmean_ref, *, S):
    i = pl.program_id(2)
    t = ATT_BLOCK
    D = HEAD_DIM
    lane = lax.broadcasted_iota(jnp.int32, (t, LANES), 1)

    @pl.when(i == 0)
    def _():
        kmean_ref[...] = jnp.zeros(kmean_ref.shape, F32)

        def prep_keys(n, _):
            sl = pl.ds(pl.multiple_of(n * t, t), t)
            for hd in range(ATT_HEADS):
                cols = slice(hd * D, (hd + 1) * D)
                k = _rope(_rms(k_ref[sl, cols].astype(F32), kg_ref[...]), cos_ref[sl, :], sin_ref[sl, :])
                ka_ref[hd, sl, 0:D] = k.astype(BF16)
                ka_ref[hd, sl, D:] = jnp.where(lane == n, 1.0, 0.0).astype(BF16)
                kmean_ref[hd, pl.ds(n, 1), :] = jnp.mean(k, axis=0, keepdims=True)
                _store_vt(vt_ref.at[hd], v_ref[sl, cols], sl)
            return 0

        lax.fori_loop(0, S // t, prep_keys, 0)

    lane_f = lane.astype(F32)

    def prep(hd, c):
        cur = 2 * i + c
        qsl = pl.ds(pl.multiple_of(cur * t, t), t)
        q = _rope(_rms(q_ref[c * t:(c + 1) * t, hd * D:(hd + 1) * D].astype(F32), qg_ref[...]),
                  cos_ref[qsl, :], sin_ref[qsl, :])
        yield
        kmean = kmean_ref[hd]
        km_hi = kmean.astype(BF16)
        km_split = jnp.concatenate([km_hi, (kmean - km_hi.astype(F32)).astype(BF16)], axis=0)
        q_hi = q.astype(BF16)
        q_lo = (q - q_hi.astype(F32)).astype(BF16)
        part = _dot_nt(q_hi, km_split)
        gate = part[:, :LANES] + part[:, LANES:] + _dot_nt(q_lo, km_split[:LANES, :])
        gate = jnp.where(lane < cur, gate, -jnp.inf)
        yield
        sel_bias = jnp.full((t, LANES), NEG, F32)
        for _ in range(MOBA_TOPK):
            top = jnp.max(gate, axis=-1, keepdims=True)
            yield
            first = jnp.min(jnp.where(gate == top, lane_f, float(LANES)), axis=-1, keepdims=True)
            yield
            pick = lane_f == first
            sel_bias = jnp.where(pick & (first < cur.astype(F32)), 0.0, sel_bias)
            gate = jnp.where(pick, -jnp.inf, gate)
        qs = (q * (LOG2E * D ** -0.5)).astype(BF16)
        return (jnp.concatenate([qs, jnp.zeros((t, LANES), BF16)], axis=-1),
                jnp.concatenate([qs, sel_bias.astype(BF16)], axis=-1))

    _attend_tile_pair(i, [prep(hd, c) for hd in range(ATT_HEADS) for c in range(2)],
                      ka_ref, vt_ref, m_ref, acc_ref, o_ref)


def moba_attention(z, cos, sin_signed, q_gain, k_gain):
    B, S, W = z.shape
    H = W // (3 * HEAD_DIM)
    assert ATT_BLOCK == MOBA_BLOCK
    t = 2 * ATT_BLOCK
    w = ATT_HEADS * HEAD_DIM
    hp = H // ATT_HEADS
    return pl.pallas_call(
        functools.partial(_moba_kernel, S=S),
        grid=(B, hp, S // t),
        in_specs=[
            pl.BlockSpec((None, t, w), lambda b, h, i: (b, i, h)),
            pl.BlockSpec((None, S, w), lambda b, h, i: (b, 0, hp + h)),
            pl.BlockSpec((None, S, w), lambda b, h, i: (b, 0, 2 * hp + h)),
            pl.BlockSpec((S, HEAD_DIM), lambda b, h, i: (0, 0)),
            pl.BlockSpec((S, HEAD_DIM), lambda b, h, i: (0, 0)),
            pl.BlockSpec((1, HEAD_DIM), lambda b, h, i: (0, 0)),
            pl.BlockSpec((1, HEAD_DIM), lambda b, h, i: (0, 0)),
        ],
        out_specs=pl.BlockSpec((None, t, w), lambda b, h, i: (b, i, h)),
        out_shape=jax.ShapeDtypeStruct((B, S, H * HEAD_DIM), BF16),
        scratch_shapes=_ATT_SCRATCH(S) + [pltpu.VMEM((ATT_HEADS, LANES, HEAD_DIM), F32)],
        compiler_params=_params("parallel", "parallel", "arbitrary"),
        name="moba_attention",
    )(z, z, z, cos, sin_signed, q_gain.reshape(1, -1), k_gain.reshape(1, -1))


META_IDX, META_RANK, META_GATE = 0, 2, 4


def _router_kernel(x_ref, g_ref, w_ref, xn_ref, meta_ref, cnt_ref, carry_ref):
    i = pl.program_id(0)
    tm = x_ref.shape[0]

    @pl.when(i == 0)
    def _():
        carry_ref[...] = jnp.zeros(carry_ref.shape, F32)

    xn = _rms(x_ref[...], g_ref[...])
    xn_ref[...] = _pack_bf16_pairs(xn)
    logits = jnp.dot(xn, w_ref[...], preferred_element_type=F32, precision=lax.Precision.HIGHEST)
    lane = lax.broadcasted_iota(jnp.int32, logits.shape, 1)
    logits = jnp.where(lane < N_EXPERTS, logits, -jnp.inf)
    top1 = jnp.max(logits, axis=-1, keepdims=True)
    lane_f = lane.astype(F32)
    idx1 = jnp.min(jnp.where(logits == top1, lane_f, float(LANES)), axis=-1, keepdims=True)
    rest = jnp.where(lane_f == idx1, -jnp.inf, logits)
    top2 = jnp.max(rest, axis=-1, keepdims=True)
    idx2 = jnp.min(jnp.where(rest == top2, lane_f, float(LANES)), axis=-1, keepdims=True)
    e2 = jnp.exp(top2 - top1)
    denom = 1.0 + e2
    chosen = jnp.where((lane_f == idx1) | (lane_f == idx2), 1.0, 0.0)
    r = lax.broadcasted_iota(jnp.int32, (tm, tm), 0)
    c = lax.broadcasted_iota(jnp.int32, (tm, tm), 1)
    ahead = jnp.dot(jnp.where(r > c, 1.0, 0.0).astype(BF16), chosen.astype(BF16), preferred_element_type=F32)
    carry = carry_ref[...]
    rank = ahead + carry
    rank1 = jnp.sum(jnp.where(lane_f == idx1, rank, 0.0), axis=-1, keepdims=True)
    rank2 = jnp.sum(jnp.where(lane_f == idx2, rank, 0.0), axis=-1, keepdims=True)
    vals = (idx1, idx2, rank1, rank2, 1.0 / denom, e2 / denom)
    meta = jnp.zeros(logits.shape, F32)
    for n, v in enumerate(vals):
        meta = jnp.where(lane == n, v, meta)
    meta_ref[...] = meta
    carry = carry + jnp.sum(chosen, axis=0, keepdims=True)
    carry_ref[...] = carry
    cnt_ref[...] = jnp.broadcast_to(carry, cnt_ref.shape)


def moe_router(x, gain, w_router, *, tm):
    M, K = x.shape
    w = jnp.zeros((K, LANES), F32).at[:, :N_EXPERTS].set(w_router)
    return pl.pallas_call(
        _router_kernel,
        grid=(M // tm,),
        in_specs=[
            pl.BlockSpec((tm, K), lambda i: (i, 0)),
            pl.BlockSpec((1, K), lambda i: (0, 0)),
            pl.BlockSpec((K, LANES), lambda i: (0, 0)),
        ],
        out_specs=[
            pl.BlockSpec((tm, K // 2), lambda i: (i, 0)),
            pl.BlockSpec((tm, LANES), lambda i: (i, 0)),
            pl.BlockSpec((8, LANES), lambda i: (0, 0)),
        ],
        out_shape=[
            jax.ShapeDtypeStruct((M, K // 2), jnp.int32),
            jax.ShapeDtypeStruct((M, LANES), F32),
            jax.ShapeDtypeStruct((8, LANES), F32),
        ],
        scratch_shapes=[pltpu.VMEM((1, LANES), F32)],
        compiler_params=_params("arbitrary"),
        name="moe_router",
    )(x, gain.reshape(1, K), w)


def _sc_workers():
    info = plsc.get_sparse_core_info()
    return info.num_cores, info.num_cores * info.num_subcores


def scatter_rows(rows, dest, n_out):
    M, W = rows.shape
    nc, nw = _sc_workers()
    per_w = M // nw
    mesh = plsc.VectorSubcoreMesh(core_axis_name="c", subcore_axis_name="s")

    @functools.partial(
        pl.kernel, mesh=mesh, out_type=jax.ShapeDtypeStruct((n_out, W), rows.dtype),
        scratch_types=[pltpu.VMEM((SC_ROWS,), jnp.int32), pltpu.VMEM((SC_ROWS, W), rows.dtype),
                       pltpu.SemaphoreType.DMA])
    def kern(rows_hbm, dest_hbm, out_hbm, idx_v, rows_v, sem):
        wid = lax.axis_index("s") * nc + lax.axis_index("c")

        @pl.loop(0, per_w // SC_ROWS)
        def _(g):
            base = wid * per_w + g * SC_ROWS
            pltpu.sync_copy(rows_hbm.at[pl.ds(base, SC_ROWS)], rows_v)
            for k in range(2):
                pltpu.sync_copy(dest_hbm.at[k, pl.ds(base, SC_ROWS)], idx_v)
                pltpu.async_copy(rows_v, out_hbm.at[idx_v], sem).wait()

    return kern(rows, dest)


def gather_rows(table, idx):
    N = idx.shape[0]
    W = table.shape[1]
    nc, nw = _sc_workers()
    per_w = N // nw
    mesh = plsc.VectorSubcoreMesh(core_axis_name="c", subcore_axis_name="s")

    @functools.partial(
        pl.kernel, mesh=mesh, out_type=jax.ShapeDtypeStruct((N, W), table.dtype),
        scratch_types=[pltpu.VMEM((SC_ROWS,), jnp.int32), pltpu.VMEM((SC_ROWS, W), table.dtype),
                       pltpu.SemaphoreType.DMA])
    def kern(table_hbm, idx_hbm, out_hbm, idx_v, rows_v, sem):
        wid = lax.axis_index("s") * nc + lax.axis_index("c")

        @pl.loop(0, per_w // SC_ROWS)
        def _(g):
            base = wid * per_w + g * SC_ROWS
            pltpu.sync_copy(idx_hbm.at[pl.ds(base, SC_ROWS)], idx_v)
            pltpu.async_copy(table_hbm.at[idx_v], rows_v, sem).wait()
            pltpu.sync_copy(rows_v, out_hbm.at[pl.ds(base, SC_ROWS)])

    return kern(table, idx)


def _combine_kernel(h_ref, y1_ref, y2_ref, meta_ref, o_ref):
    g1 = meta_ref[:, META_GATE:META_GATE + 1]
    g2 = meta_ref[:, META_GATE + 1:META_GATE + 2]
    o_ref[...] = (h_ref[...] + g1 * _unpack_bf16_pairs(y1_ref[...]).astype(F32)
                  + g2 * _unpack_bf16_pairs(y2_ref[...]).astype(F32))


def moe_combine(h, y_pairs, meta):
    M, K = h.shape
    p = PERM_TILE
    nt = M // p
    return pl.pallas_call(
        _combine_kernel,
        grid=(nt,),
        in_specs=[
            pl.BlockSpec((p, K), lambda i: (i, 0)),
            pl.BlockSpec((p, K // 2), lambda i: (i, 0)),
            pl.BlockSpec((p, K // 2), lambda i: (nt + i, 0)),
            pl.BlockSpec((p, LANES), lambda i: (i, 0)),
        ],
        out_specs=pl.BlockSpec((p, K), lambda i: (i, 0)),
        out_shape=jax.ShapeDtypeStruct((M, K), F32),
        compiler_params=_params("parallel"),
        name="moe_combine",
    )(h, y_pairs, y_pairs, meta)


def _expert_ffn_kernel(te_ref, tv_ref, x_ref, wg_ref, wu_ref, wd_ref, o_ref, xb_ref, acc_ref):
    del te_ref
    i = pl.program_id(0)
    j = pl.program_id(1)
    tm = x_ref.shape[0]
    valid = tv_ref[i]

    def swiglu_rows(rows):
        @pl.when(j == 0)
        def _():
            xb_ref[0:rows, :] = _unpack_bf16_pairs(x_ref[0:rows, :])

        x = xb_ref[0:rows, :]
        gt = jnp.dot(x, wg_ref[...], preferred_element_type=F32)
        up = jnp.dot(x, wu_ref[...], preferred_element_type=F32)
        act = (gt / (1.0 + jnp.exp(-gt)) * up).astype(BF16)
        part = jnp.dot(act, wd_ref[...], preferred_element_type=F32)

        @pl.when(j == 0)
        def _():
            acc_ref[0:rows, :] = part

        @pl.when(j > 0)
        def _():
            acc_ref[0:rows, :] += part

        @pl.when(j == pl.num_programs(1) - 1)
        def _():
            o_ref[0:rows, :] = _pack_bf16_pairs(acc_ref[0:rows, :])
            if rows < tm:
                o_ref[rows:tm, :] = jnp.zeros((tm - rows, o_ref.shape[1]), o_ref.dtype)

    @pl.when(valid > tm // 2)
    def _():
        swiglu_rows(tm)

    @pl.when((valid > 0) & (valid <= tm // 2))
    def _():
        swiglu_rows(tm // 2)

    @pl.when((valid == 0) & (j == 0))
    def _():
        o_ref[...] = jnp.zeros(o_ref.shape, o_ref.dtype)


def expert_ffn(xs, tile_expert, tile_valid, w_gate_up, w_down, *, tm, tf):
    R = xs.shape[0]
    E, F, K = w_down.shape
    nf = F // tf
    live = lambda i, tv: tv[i] > 0
    col = lambda i, j, tv: jnp.where(live(i, tv), j, nf - 1)
    grid_spec = pltpu.PrefetchScalarGridSpec(
        num_scalar_prefetch=2,
        grid=(R // tm, nf),
        in_specs=[
            pl.BlockSpec((tm, K // 2), lambda i, j, te, tv: (te[R // tm + i], 0)),
            pl.BlockSpec((None, K, tf), lambda i, j, te, tv: (te[i], 0, col(i, j, tv))),
            pl.BlockSpec((None, K, tf), lambda i, j, te, tv: (te[i], 0, nf + col(i, j, tv))),
            pl.BlockSpec((None, tf, K), lambda i, j, te, tv: (te[i], col(i, j, tv), 0)),
        ],
        out_specs=pl.BlockSpec((tm, K // 2), lambda i, j, te, tv: (i, 0)),
        scratch_shapes=[pltpu.VMEM((tm, K), BF16), pltpu.VMEM((tm, K), F32)],
    )
    return pl.pallas_call(
        _expert_ffn_kernel,
        grid_spec=grid_spec,
        out_shape=jax.ShapeDtypeStruct((R, K // 2), jnp.int32),
        compiler_params=_params("arbitrary", "arbitrary"),
        name="moe_expert_ffn",
    )(tile_expert, tile_valid, xs, w_gate_up, w_gate_up, w_down)


def moe_residual(h, gain, w_router, w_gate_up, w_down):
    M, K = h.shape
    p = PERM_TILE
    tm = EXPERT_TILE
    n_rows = 2 * M + N_EXPERTS * tm
    xn, meta, cnt = moe_router(h, gain, w_router, tm=p)

    counts = cnt[0, :N_EXPERTS].astype(jnp.int32)
    padded = (counts + tm - 1) // tm * tm
    ends = jnp.cumsum(padded)
    offsets = ends - padded
    n_tiles = n_rows // tm
    tile_row = jnp.arange(n_tiles) * tm
    n_used = ends[-1] // tm
    last_used = jnp.minimum(tile_row // tm, n_used - 1)
    expert_of = lambda row: jnp.minimum(jnp.sum(ends[None, :] <= row[:, None], axis=1), N_EXPERTS - 1)
    tile_expert = expert_of(last_used * tm)
    row_in_expert = tile_row - jnp.sum(jnp.where(tile_expert[:, None] == jnp.arange(N_EXPERTS), offsets, 0), axis=1)
    own_count = jnp.sum(jnp.where(tile_expert[:, None] == jnp.arange(N_EXPERTS), counts, 0), axis=1)
    tile_valid = jnp.where(tile_row // tm < n_used, jnp.clip(own_count - row_in_expert, 0, tm), 0)
    tile_tables = jnp.concatenate([tile_expert, last_used]).astype(jnp.int32)

    idx = meta[:, META_IDX:META_IDX + 2].astype(jnp.int32)
    rank = meta[:, META_RANK:META_RANK + 2].astype(jnp.int32)
    dest = (jnp.sum(jnp.where(idx[:, :, None] == jnp.arange(N_EXPERTS), offsets, 0), axis=-1) + rank).T

    xs = scatter_rows(xn, dest, n_rows)
    ys = expert_ffn(xs, tile_tables, tile_valid.astype(jnp.int32), w_gate_up, w_down, tm=tm, tf=896)
    return moe_combine(h, gather_rows(ys, dest.reshape(-1)), meta)


def _ffn_kernel(x_ref, g_ref, wg_ref, wu_ref, wd_ref, o_ref, xn_ref, acc_ref):
    j = pl.program_id(1)

    @pl.when(j == 0)
    def _():
        x = x_ref[...]
        xn_ref[...] = _rms(x, g_ref[...]).astype(BF16)
        acc_ref[...] = x

    xn = xn_ref[...]
    gt = jnp.dot(xn, wg_ref[...], preferred_element_type=F32)
    up = jnp.dot(xn, wu_ref[...], preferred_element_type=F32)
    act = gt / (1.0 + jnp.exp(-gt)) * up
    acc_ref[...] += jnp.dot(act.astype(BF16), wd_ref[...], preferred_element_type=F32)

    @pl.when(j == pl.num_programs(1) - 1)
    def _():
        o_ref[...] = acc_ref[...]


def ffn_residual(x, gain, w_gate_up, w_down, *, tm, tf):
    M, K = x.shape
    F = w_down.shape[0]
    nf = F // tf
    return pl.pallas_call(
        _ffn_kernel,
        grid=(M // tm, nf),
        in_specs=[
            pl.BlockSpec((tm, K), lambda i, j: (i, 0)),
            pl.BlockSpec((1, K), lambda i, j: (0, 0)),
            pl.BlockSpec((K, tf), lambda i, j: (0, j)),
            pl.BlockSpec((K, tf), lambda i, j: (0, nf + j)),
            pl.BlockSpec((tf, K), lambda i, j: (j, 0)),
        ],
        out_specs=pl.BlockSpec((tm, K), lambda i, j: (i, 0)),
        out_shape=jax.ShapeDtypeStruct((M, K), F32),
        scratch_shapes=[pltpu.VMEM((tm, K), BF16), pltpu.VMEM((tm, K), F32)],
        compiler_params=_params("parallel", "arbitrary"),
        name="ffn_residual",
    )(x, gain.reshape(1, K), w_gate_up, w_gate_up, w_down)


def _even_mix(h, B, S, norm1, w_in, f_bias, q_norm, k_norm, conv_w, a_log, dt_bias, o_norm, w_out):
    M, D = h.shape
    fw, gw = FOX_HEADS * HEAD_DIM, GDN_HEADS * HEAD_DIM
    o_ff = 3 * fw
    o_gq = o_ff + FOX_HEADS
    o_ga = o_gq + 3 * gw
    o_gb = o_ga + GDN_HEADS
    o_gg = o_gb + GDN_HEADS
    w_big = jnp.concatenate([w_in[:, :o_ff], w_in[:, o_gq:o_ga], w_in[:, o_gg:]], axis=1).astype(BF16)
    w_small = jnp.concatenate([w_in[:, o_ff:o_gq], w_in[:, o_ga:o_gg],
                               jnp.zeros((D, LANES - FOX_HEADS - 2 * GDN_HEADS), F32)], axis=1).astype(BF16)
    z, zs = norm_matmul(h, norm1, w_big, w_small, tm=ROW_TILE, tn=896)
    par = jnp.zeros((8, LANES), F32)
    par = par.at[0, LANE_F:LANE_F + FOX_HEADS].set(f_bias).at[0, LANE_A:LANE_A + GDN_HEADS].set(dt_bias)
    par = par.at[1, LANE_A:LANE_A + GDN_HEADS].set(a_log)
    col, row = even_gates(zs.reshape(B, S, LANES), par)
    row = row.reshape(B, 16, 1, S)
    z = z.reshape(B, S, -1)
    fox = fox_attention(z, col, q_norm, k_norm)
    gdn = gdn_mixer(z, conv_w, col, row, o_norm)
    w_out = w_out.astype(BF16)
    return matmul_residual(h, [(fox.reshape(M, fw), w_out[:fw]), (gdn.reshape(M, gw), w_out[fw:])], tm=ROW_TILE, tn=1024)


def _odd_mix(h, B, S, norm1, w_qkv, q_norm, k_norm, w_out):
    M, D = h.shape
    z = norm_matmul(h, norm1, w_qkv.astype(BF16), tm=ROW_TILE, tn=1024).reshape(B, S, -1)
    half = HEAD_DIM // 2
    inv = jnp.power(ROPE_THETA, -jnp.arange(half, dtype=F32) / half)
    ang = jnp.arange(S, dtype=F32)[:, None] * inv[None, :]
    cos, sin = jnp.cos(ang), jnp.sin(ang)
    cos_full = jnp.concatenate([cos, cos], axis=-1)
    sin_signed = jnp.concatenate([-sin, sin], axis=-1)
    att = moba_attention(z, cos_full, sin_signed, q_norm, k_norm)
    return matmul_residual(h, [(att.reshape(M, -1), w_out.astype(BF16))], tm=ROW_TILE, tn=1024)


def _odd_moe(h, norm2, w_router, w_gate_up, w_down):
    return moe_residual(h, norm2, w_router, w_gate_up.astype(BF16), w_down.astype(BF16))


def kernel(x, e_norm1, e_w_in, e_fox_f_bias, e_fox_q_norm, e_fox_k_norm, e_gdn_conv, e_gdn_a_log,
           e_gdn_dt_bias, e_gdn_o_norm, e_w_out, e_norm2, e_ffn_w_gate_up, e_ffn_w_down,
           o_norm1, o_w_qkv, o_q_norm, o_k_norm, o_w_out, o_norm2, o_router, o_exp_w_gate_up, o_exp_w_down):
    B, S, D = x.shape
    h = x.reshape(B * S, D)
    depth = e_norm1.shape[0] + o_norm1.shape[0]
    for layer in range(depth):
        i = layer // 2
        if layer % 2 == 0:
            h = _even_mix(h, B, S, e_norm1[i], e_w_in[i], e_fox_f_bias[i], e_fox_q_norm[i], e_fox_k_norm[i],
                          e_gdn_conv[i], e_gdn_a_log[i], e_gdn_dt_bias[i], e_gdn_o_norm[i], e_w_out[i])
            h = ffn_residual(h, e_norm2[i], e_ffn_w_gate_up[i].astype(BF16), e_ffn_w_down[i].astype(BF16),
                             tm=ROW_TILE // 2, tf=1408)
        else:
            h = _odd_mix(h, B, S, o_norm1[i], o_w_qkv[i], o_q_norm[i], o_k_norm[i], o_w_out[i])
            h = _odd_moe(h, o_norm2[i], o_router[i], o_exp_w_gate_up[i], o_exp_w_down[i])
    return h.reshape(B, S, D)
```

```python
import functools

import jax
import jax.numpy as jnp
from jax import lax
from jax.experimental import pallas as pl
from jax.experimental.pallas import tpu as pltpu
from jax.experimental.pallas import tpu_sc as plsc

F32 = jnp.float32
BF16 = jnp.bfloat16

HEAD_DIM = 128
FOX_HEADS = 4
GDN_HEADS = 4
CONV_WIDTH = 4
MOBA_BLOCK = 256
MOBA_TOPK = 3
N_EXPERTS = 8
ROPE_THETA = 10000.0
EPS = 1e-6

LANES = 128
GDN_CHUNK = 128
INV_BLOCK = 16
GDN_HEADS_PER_STEP = 4
NEG = -(2.0 ** 100)
LOG2E = 1.4426950408889634
ATT_BLOCK = 256
ATT_HEADS = 2
V_ROWS = HEAD_DIM + 16
PERM_TILE = 512
SC_ROWS = 64
ROW_TILE = 1024
EXPERT_TILE = 1024
VMEM_LIMIT_BYTES = 56 * 1024 * 1024

FOX_Q0, FOX_K0, FOX_V0 = 0, 4, 8
GDN_Q0, GDN_K0, GDN_V0, GDN_G0 = 12, 16, 20, 24
LANE_F, LANE_A, LANE_B = 0, 4, 8


def _params(*sem):
    return pltpu.CompilerParams(dimension_semantics=sem, vmem_limit_bytes=VMEM_LIMIT_BYTES)


def _rms(x, gain):
    return x * lax.rsqrt(jnp.mean(x * x, axis=-1, keepdims=True) + EPS) * gain


def _dot_nt(a, b, **kw):
    return lax.dot_general(a, b, (((1,), (1,)), ((), ())), preferred_element_type=F32, **kw)


def _pick_lane(x, lane_idx):
    lane = lax.broadcasted_iota(jnp.int32, x.shape, 1)
    return jnp.sum(jnp.where(lane == lane_idx, x, 0.0), axis=-1, keepdims=True)


def _pack_bf16_pairs(x):
    n = x.shape[1] // 2
    hi = pltpu.bitcast(x[:, :n].astype(BF16).astype(F32), jnp.uint32)
    lo = pltpu.bitcast(x[:, n:].astype(BF16).astype(F32), jnp.uint32)
    return pltpu.bitcast(hi | (lo >> 16), jnp.int32)


def _unpack_bf16_pairs(w):
    u = pltpu.bitcast(w, jnp.uint32)
    hi = pltpu.bitcast(u & jnp.uint32(0xFFFF0000), F32).astype(BF16)
    lo = pltpu.bitcast(u << 16, F32).astype(BF16)
    return jnp.concatenate([hi, lo], axis=1)


def _norm_mm_kernel(x_ref, g_ref, w_ref, *rest, has_aux):
    if has_aux:
        waux_ref, o_ref, oaux_ref, xn_ref = rest
    else:
        o_ref, xn_ref = rest

    @pl.when(pl.program_id(1) == 0)
    def _():
        xn = _rms(x_ref[...], g_ref[...]).astype(BF16)
        xn_ref[...] = xn
        if has_aux:
            oaux_ref[...] = jnp.dot(xn, waux_ref[...], preferred_element_type=F32)

    o_ref[...] = jnp.dot(xn_ref[...], w_ref[...], preferred_element_type=F32).astype(o_ref.dtype)


def norm_matmul(x, gain, w, w_aux=None, *, tm, tn, out_dtype=BF16):
    M, K = x.shape
    N = w.shape[1]
    has_aux = w_aux is not None
    in_specs = [
        pl.BlockSpec((tm, K), lambda i, j: (i, 0)),
        pl.BlockSpec((1, K), lambda i, j: (0, 0)),
        pl.BlockSpec((K, tn), lambda i, j: (0, j)),
    ]
    out_shape = [jax.ShapeDtypeStruct((M, N), out_dtype)]
    out_specs = [pl.BlockSpec((tm, tn), lambda i, j: (i, j))]
    args = [x, gain.reshape(1, K), w]
    if has_aux:
        in_specs.append(pl.BlockSpec((K, LANES), lambda i, j: (0, 0)))
        out_shape.append(jax.ShapeDtypeStruct((M, LANES), F32))
        out_specs.append(pl.BlockSpec((tm, LANES), lambda i, j: (i, 0)))
        args.append(w_aux)
    res = pl.pallas_call(
        functools.partial(_norm_mm_kernel, has_aux=has_aux),
        grid=(M // tm, N // tn),
        in_specs=in_specs,
        out_specs=out_specs,
        out_shape=out_shape,
        scratch_shapes=[pltpu.VMEM((tm, K), BF16)],
        compiler_params=_params("parallel", "arbitrary"),
        name="norm_matmul",
    )(*args)
    return res if has_aux else res[0]


def _mm_res_kernel(*refs, n_in):
    res_ref = refs[0]
    o_ref = refs[1 + 2 * n_in]
    acc = res_ref[...]
    for t in range(n_in):
        acc = acc + jnp.dot(refs[1 + 2 * t][...], refs[2 + 2 * t][...], preferred_element_type=F32)
    o_ref[...] = acc


def matmul_residual(res, pairs, *, tm, tn):
    M, N = res.shape
    in_specs = [pl.BlockSpec((tm, tn), lambda i, j: (i, j))]
    args = [res]
    for a, w in pairs:
        K = a.shape[1]
        in_specs.append(pl.BlockSpec((tm, K), lambda i, j: (i, 0)))
        in_specs.append(pl.BlockSpec((K, tn), lambda i, j: (0, j)))
        args += [a, w]
    return pl.pallas_call(
        functools.partial(_mm_res_kernel, n_in=len(pairs)),
        grid=(M // tm, N // tn),
        in_specs=in_specs,
        out_specs=pl.BlockSpec((tm, tn), lambda i, j: (i, j)),
        out_shape=jax.ShapeDtypeStruct((M, N), F32),
        compiler_params=_params("parallel", "arbitrary"),
        name="matmul_residual",
    )(*args)


def _gate_kernel(zs_ref, par_ref, col_ref, row_ref, *, S):
    C = GDN_CHUNK
    bias = par_ref[0:1, :]
    neg_a = -jnp.exp(par_ref[1:2, :])
    r = lax.broadcasted_iota(jnp.int32, (C, C), 0)
    c = lax.broadcasted_iota(jnp.int32, (C, C), 1)
    tril = (r >= c).astype(F32)
    lane = lax.broadcasted_iota(jnp.int32, (C, LANES), 1)

    def body(n, carry):
        sl = pl.ds(pl.multiple_of(n * C, C), C)
        z = zs_ref[sl, :]
        t = z + bias
        soft = jnp.log(1.0 + jnp.exp(-jnp.abs(t)))
        log_f = jnp.minimum(t, 0.0) - soft
        g = neg_a * (jnp.maximum(t, 0.0) + soft)
        beta = 1.0 / (1.0 + jnp.exp(-z))
        u = jnp.where(lane < LANE_A, log_f, jnp.where(lane < LANE_B, g, 0.0))
        cs = jnp.dot(tril, u, preferred_element_type=F32, precision=lax.Precision.HIGHEST)
        cs = cs + jnp.where(lane < LANE_A, carry, 0.0)
        out = jnp.where(lane < LANE_B, cs, beta)
        col_ref[sl, :] = out
        out_t = out.T
        for hd in range(GDN_HEADS):
            row_ref[hd, :, sl] = out_t[LANE_A + hd:LANE_A + hd + 1, :]
        return cs[C - 1:C, :]

    lax.fori_loop(0, S // C, body, jnp.zeros((1, LANES), F32))


def even_gates(zs, par):
    B, S, _ = zs.shape
    return pl.pallas_call(
        functools.partial(_gate_kernel, S=S),
        grid=(B,),
        in_specs=[
            pl.BlockSpec((None, S, LANES), lambda b: (b, 0, 0)),
            pl.BlockSpec((8, LANES), lambda b: (0, 0)),
        ],
        out_specs=[
            pl.BlockSpec((None, S, LANES), lambda b: (b, 0, 0)),
            pl.BlockSpec((None, GDN_HEADS, 1, S), lambda b: (b, 0, 0, 0)),
        ],
        out_shape=[jax.ShapeDtypeStruct((B, S, LANES), F32), jax.ShapeDtypeStruct((B, GDN_HEADS, 1, S), F32)],
        compiler_params=_params("parallel"),
        name="even_gates",
    )(zs, par)


def _split3(x):
    hi = x.astype(BF16).astype(F32)
    mid = (x - hi).astype(BF16).astype(F32)
    lo = (x - hi - mid).astype(BF16).astype(F32)
    return hi, mid, lo


def _interleave(gens):
    out = [None] * len(gens)
    live = list(range(len(gens)))
    while live:
        for n in list(live):
            try:
                next(gens[n])
            except StopIteration as stop:
                out[n] = stop.value
                live.remove(n)
    return out


def _attend(state, c, qa, ka_ref, vt_ref, key0, nkeys, keep=None):
    ks = pl.ds(pl.multiple_of(key0, ATT_BLOCK), nkeys)
    st = _dot_nt(ka_ref[ks, :], qa)
    yield
    if keep is not None:
        st = jnp.where(keep, st, NEG)
    m_new = jnp.max(st, axis=0, keepdims=True)
    if state[c] is not None:
        m_old, acc_old = state[c]
        m_new = jnp.maximum(m_old, m_new)
    p = jnp.exp2(st - m_new).astype(BF16)
    pv = jnp.dot(vt_ref[:, ks], p, preferred_element_type=F32)
    state[c] = (m_new, pv if state[c] is None else acc_old * jnp.exp2(m_old - m_new) + pv)


def _attend_tile_pair(i, prep, ka_ref, vt_ref, m_ref, acc_ref, o_ref):
    t = ATT_BLOCK
    chains = [(hd, c) for c in range(2) for hd in range(ATT_HEADS)]
    num = lambda hd, c: 2 * hd + c
    qa = _interleave(prep)

    def step(state, hd, c, which, **kw):
        return _attend(state, num(hd, c), qa[num(hd, c)][which], ka_ref.at[hd], vt_ref.at[hd], **kw)

    def save(state):
        m_ref[...] = jnp.stack([state[n][0] for n in range(len(chains))])
        acc_ref[...] = jnp.stack([state[n][1] for n in range(len(chains))])

    def past_blocks(key0, n_steps):
        state = {n: (m_ref[n], acc_ref[n]) for n in range(len(chains))}
        _interleave([step(state, hd, c, 1, key0=key0 + s * 2 * t, nkeys=2 * t)
                     for s in range(n_steps) for hd, c in chains])
        save(state)

    state = {n: None for n in range(len(chains))}
    keep = _causal_keep()
    _interleave([step(state, hd, c, 0, key0=(2 * i + c) * t, nkeys=t, keep=keep) for hd, c in chains]
                + [step(state, hd, 1, 1, key0=(2 * i) * t, nkeys=t) for hd in range(ATT_HEADS)])
    save(state)

    def four_blocks(g, _):
        past_blocks(g * (4 * t), 2)
        return 0

    lax.fori_loop(0, i // 2, four_blocks, 0)

    @pl.when((i & 1) != 0)
    def _():
        past_blocks((i // 2) * (4 * t), 1)

    for hd, c in chains:
        acc = acc_ref[num(hd, c)]
        out_t = acc[:HEAD_DIM, :] * (1.0 / acc[HEAD_DIM:HEAD_DIM + 1, :])
        o_ref[c * t:(c + 1) * t, hd * HEAD_DIM:(hd + 1) * HEAD_DIM] = out_t.T.astype(o_ref.dtype)


def _store_vt(vt_ref, v, sl):
    vt_ref[0:HEAD_DIM, sl] = v.astype(F32).T.astype(BF16)
    vt_ref[HEAD_DIM:V_ROWS, sl] = jnp.ones((V_ROWS - HEAD_DIM, ATT_BLOCK), BF16)


def _causal_keep():
    key = lax.broadcasted_iota(jnp.int32, (ATT_BLOCK, ATT_BLOCK), 0)
    qry = lax.broadcasted_iota(jnp.int32, (ATT_BLOCK, ATT_BLOCK), 1)
    return key <= qry


_ATT_SCRATCH = lambda S: [pltpu.VMEM((ATT_HEADS, S, 2 * HEAD_DIM), BF16), pltpu.VMEM((ATT_HEADS, V_ROWS, S), BF16),
                          pltpu.VMEM((2 * ATT_HEADS, 1, ATT_BLOCK), F32),
                          pltpu.VMEM((2 * ATT_HEADS, V_ROWS, ATT_BLOCK), F32)]


def _fox_kernel(q_ref, k_ref, v_ref, col_ref, qg_ref, kg_ref, o_ref, ka_ref, vt_ref, m_ref, acc_ref, *, S):
    h0 = pl.program_id(1) * ATT_HEADS
    i = pl.program_id(2)
    t = ATT_BLOCK
    D = HEAD_DIM
    lane = lax.broadcasted_iota(jnp.int32, (t, LANES), 1)

    @pl.when(i == 0)
    def _():
        def prep_keys(n, _):
            sl = pl.ds(pl.multiple_of(n * t, t), t)
            for hd in range(ATT_HEADS):
                cols = slice(hd * D, (hd + 1) * D)
                ka_ref[hd, sl, 0:D] = _rms(k_ref[sl, cols].astype(F32), kg_ref[...]).astype(BF16)
                hi, mid, lo = _split3(-LOG2E * _pick_lane(col_ref[sl, :], LANE_F + h0 + hd))
                ka_ref[hd, sl, D:] = jnp.where(
                    lane < 3, 1.0, jnp.where(lane == 3, hi, jnp.where(lane == 4, mid, jnp.where(lane == 5, lo, 0.0)))
                ).astype(BF16)
                _store_vt(vt_ref.at[hd], v_ref[sl, cols], sl)
            return 0

        lax.fori_loop(0, S // t, prep_keys, 0)

    def prep(hd, c):
        q = q_ref[c * t:(c + 1) * t, hd * D:(hd + 1) * D].astype(F32)
        qn = (_rms(q, qg_ref[...]) * (LOG2E * D ** -0.5)).astype(BF16)
        yield
        qsl = pl.ds(pl.multiple_of((2 * i + c) * t, t), t)
        hi, mid, lo = _split3(LOG2E * _pick_lane(col_ref[qsl, :], LANE_F + h0 + hd))
        aug = jnp.where(lane == 0, hi,
                        jnp.where(lane == 1, mid, jnp.where(lane == 2, lo, jnp.where(lane < 6, 1.0, 0.0))))
        qa = jnp.concatenate([qn, aug.astype(BF16)], axis=-1)
        return qa, qa

    _attend_tile_pair(i, [prep(hd, c) for hd in range(ATT_HEADS) for c in range(2)],
                      ka_ref, vt_ref, m_ref, acc_ref, o_ref)


def fox_attention(z, col, q_gain, k_gain):
    B, S, _ = z.shape
    t = 2 * ATT_BLOCK
    w = ATT_HEADS * HEAD_DIM
    return pl.pallas_call(
        functools.partial(_fox_kernel, S=S),
        grid=(B, FOX_HEADS // ATT_HEADS, S // t),
        in_specs=[
            pl.BlockSpec((None, t, w), lambda b, h, i: (b, i, FOX_Q0 // ATT_HEADS + h)),
            pl.BlockSpec((None, S, w), lambda b, h, i: (b, 0, FOX_K0 // ATT_HEADS + h)),
            pl.BlockSpec((None, S, w), lambda b, h, i: (b, 0, FOX_V0 // ATT_HEADS + h)),
            pl.BlockSpec((None, S, LANES), lambda b, h, i: (b, 0, 0)),
            pl.BlockSpec((1, HEAD_DIM), lambda b, h, i: (0, 0)),
            pl.BlockSpec((1, HEAD_DIM), lambda b, h, i: (0, 0)),
        ],
        out_specs=pl.BlockSpec((None, t, w), lambda b, h, i: (b, i, h)),
        out_shape=jax.ShapeDtypeStruct((B, S, FOX_HEADS * HEAD_DIM), BF16),
        scratch_shapes=_ATT_SCRATCH(S),
        compiler_params=_params("parallel", "parallel", "arbitrary"),
        name="fox_attention",
    )(z, z, z, col, q_gain.reshape(1, -1), k_gain.reshape(1, -1))


def _unit_lower_inverse(m):
    n = m.shape[0]
    r = lax.broadcasted_iota(jnp.int32, (n, n), 0)
    c = lax.broadcasted_iota(jnp.int32, (n, n), 1)
    eye = (r == c).astype(F32)

    def same_block(b):
        return (r // b) == (c // b)

    p = jnp.where(same_block(INV_BLOCK), m, 0.0)
    inv = eye - p
    k = 2
    while k < INV_BLOCK:
        pb = p.astype(BF16)
        p = jnp.dot(pb, pb, preferred_element_type=F32)
        yield
        inv = jnp.dot(inv.astype(BF16), (eye + p).astype(BF16), preferred_element_type=F32)
        yield
        k *= 2
    b = INV_BLOCK
    while b < n:
        off = jnp.where(same_block(2 * b), jnp.where(same_block(b), 0.0, m), 0.0).astype(BF16)
        ib = inv.astype(BF16)
        left = jnp.dot(ib, off, preferred_element_type=F32).astype(BF16)
        yield
        inv = inv - jnp.dot(left, ib, preferred_element_type=F32)
        yield
        b *= 2
    return inv


def _gdn_kernel(q_ref, k_ref, v_ref, gg_ref, wq_ref, wk_ref, wv_ref, col_ref, row_ref, on_ref,
                o_ref, pad_ref, qs_ref, ks_ref, vs_ref, *, S, rows, hb):
    h0 = pl.program_id(1) * hb
    C = GDN_CHUNK
    D = HEAD_DIM

    def conv_silu(x_ref, w_ref, dst_ref, hh, mode):
        cols = slice(hh * D, (hh + 1) * D)
        pad_ref[0:8, :] = jnp.zeros((8, D), F32)

        def fill(n, _):
            src = pl.ds(pl.multiple_of(n * rows, rows), rows)
            pad_ref[pl.ds(pl.multiple_of(8 + n * rows, 8), rows), :] = x_ref[src, cols].astype(F32)
            return 0

        lax.fori_loop(0, S // rows, fill, 0)
        w = w_ref[:, cols]

        def conv(n, _):
            base = pl.multiple_of(n * rows, rows)
            win = pad_ref[pl.ds(base, rows + 8), :]
            y = jnp.zeros((rows, D), F32)
            for tap in range(CONV_WIDTH):
                lead = 8 - (CONV_WIDTH - 1) + tap
                y = y + w[tap:tap + 1, :] * pltpu.roll(win, rows + 8 - lead, 0)[0:rows, :]
            y = y / (1.0 + jnp.exp(-y))
            if mode != "v":
                y = y * lax.rsqrt(jnp.sum(y * y, axis=-1, keepdims=True) + EPS)
            if mode == "q":
                y = y * D ** -0.5
            dst_ref[hh, pl.ds(base, rows), :] = y.astype(dst_ref.dtype)
            return 0

        lax.fori_loop(0, S // rows, conv, 0)

    for hh in range(hb):
        conv_silu(q_ref, wq_ref, qs_ref, hh, "q")
        conv_silu(k_ref, wk_ref, ks_ref, hh, "k")
        conv_silu(v_ref, wv_ref, vs_ref, hh, "v")

    r = lax.broadcasted_iota(jnp.int32, (C, C), 0)
    c = lax.broadcasted_iota(jnp.int32, (C, C), 1)
    incl = r >= c
    strict = r > c

    def head_chunk(hh, sl, tab, state):
        q = qs_ref[hh, sl, :].astype(F32)
        k = ks_ref[hh, sl, :].astype(F32)
        v = vs_ref[hh, sl, :].astype(F32)
        gcol = _pick_lane(tab, LANE_A + h0 + hh)
        beta = _pick_lane(tab, LANE_B + h0 + hh)
        grow = row_ref[hh, :, sl]
        glast = gcol[C - 1:C, :]
        decay = jnp.where(incl, jnp.exp(jnp.where(incl, gcol - grow, 0.0)), 0.0)
        eg = jnp.exp(gcol)
        kb = k * beta
        kbf = k.astype(BF16)
        m = jnp.where(strict, _dot_nt(kb.astype(BF16), kbf) * decay, 0.0)
        attn = _dot_nt(q.astype(BF16), kbf) * decay
        yield
        tinv = (yield from _unit_lower_inverse(m)).astype(BF16)
        rhs = jnp.concatenate([v * beta, kb * eg], axis=-1).astype(BF16)
        sol = jnp.dot(tinv, rhs, preferred_element_type=F32)
        yield
        u = sol[:, :D]
        w = sol[:, D:]
        qg = q * eg
        kg = k * jnp.exp(glast - gcol)
        sb = state.astype(BF16)
        v_new = u - jnp.dot(w.astype(BF16), sb, preferred_element_type=F32)
        o_state = jnp.dot(qg.astype(BF16), sb, preferred_element_type=F32)
        yield
        vb = v_new.astype(BF16)
        o = o_state + jnp.dot(attn.astype(BF16), vb, preferred_element_type=F32)
        state = state * jnp.exp(glast) + jnp.dot(kg.T.astype(BF16), vb, preferred_element_type=F32)
        yield
        gate = gg_ref[sl, hh * D:(hh + 1) * D].astype(F32)
        return state, (_rms(o, on_ref[...]) * (gate / (1.0 + jnp.exp(-gate)))).astype(o_ref.dtype)

    def chunk(n, states):
        sl = pl.ds(pl.multiple_of(n * C, C), C)
        tab = col_ref[sl, :]
        res = _interleave([head_chunk(hh, sl, tab, states[hh]) for hh in range(hb)])
        o_ref[sl, :] = jnp.concatenate([o for _, o in res], axis=-1)
        return tuple(s for s, _ in res)

    lax.fori_loop(0, S // C, chunk, tuple(jnp.zeros((D, D), F32) for _ in range(hb)))


def gdn_mixer(z, conv_w, col, row, o_gain):
    B, S, _ = z.shape
    D = HEAD_DIM
    hb = GDN_HEADS_PER_STEP
    rows = 256
    seq = lambda off: pl.BlockSpec((None, S, hb * D), lambda b, h: (b, 0, off // hb + h),
                                   pipeline_mode=pl.Buffered(1))
    cw = lambda off: pl.BlockSpec((CONV_WIDTH, hb * D), lambda b, h: (0, off // hb + h))
    return pl.pallas_call(
        functools.partial(_gdn_kernel, S=S, rows=rows, hb=hb),
        grid=(B, GDN_HEADS // hb),
        in_specs=[
            seq(GDN_Q0), seq(GDN_K0), seq(GDN_V0), seq(GDN_G0),
            cw(0), cw(GDN_HEADS), cw(2 * GDN_HEADS),
            pl.BlockSpec((None, S, LANES), lambda b, h: (b, 0, 0)),
            pl.BlockSpec((None, hb, 1, S), lambda b, h: (b, h, 0, 0)),
            pl.BlockSpec((1, D), lambda b, h: (0, 0)),
        ],
        out_specs=pl.BlockSpec((None, S, hb * D), lambda b, h: (b, 0, h)),
        out_shape=jax.ShapeDtypeStruct((B, S, GDN_HEADS * D), BF16),
        scratch_shapes=[pltpu.VMEM((S + 8, D), F32)] + [pltpu.VMEM((hb, S, D), BF16)] * 3,
        compiler_params=_params("parallel", "parallel"),
        name="gdn_mixer",
    )(z, z, z, z, conv_w, conv_w, conv_w, col, row, o_gain.reshape(1, D))


def _rope(x, cos, sin_signed):
    return x * cos + pltpu.roll(x, HEAD_DIM // 2, 1) * sin_signed


def _moba_kernel(q_ref, k_ref, v_ref, cos_ref, sin_ref, qg_ref, kg_ref, o_ref,
                 ka_ref, vt_ref, m_ref, acc_ref, kmean_ref, *, S):
    i = pl.program_id(2)
    t = ATT_BLOCK
    D = HEAD_DIM
    lane = lax.broadcasted_iota(jnp.int32, (t, LANES), 1)

    @pl.when(i == 0)
    def _():
        kmean_ref[...] = jnp.zeros(kmean_ref.shape, F32)

        def prep_keys(n, _):
            sl = pl.ds(pl.multiple_of(n * t, t), t)
            for hd in range(ATT_HEADS):
                cols = slice(hd * D, (hd + 1) * D)
                k = _rope(_rms(k_ref[sl, cols].astype(F32), kg_ref[...]), cos_ref[sl, :], sin_ref[sl, :])
                ka_ref[hd, sl, 0:D] = k.astype(BF16)
                ka_ref[hd, sl, D:] = jnp.where(lane == n, 1.0, 0.0).astype(BF16)
                kmean_ref[hd, pl.ds(n, 1), :] = jnp.mean(k, axis=0, keepdims=True)
                _store_vt(vt_ref.at[hd], v_ref[sl, cols], sl)
            return 0

        lax.fori_loop(0, S // t, prep_keys, 0)

    lane_f = lane.astype(F32)

    def prep(hd, c):
        cur = 2 * i + c
        qsl = pl.ds(pl.multiple_of(cur * t, t), t)
        q = _rope(_rms(q_ref[c * t:(c + 1) * t, hd * D:(hd + 1) * D].astype(F32), qg_ref[...]),
                  cos_ref[qsl, :], sin_ref[qsl, :])
        yield
        kmean = kmean_ref[hd]
        km_hi = kmean.astype(BF16)
        km_split = jnp.concatenate([km_hi, (kmean - km_hi.astype(F32)).astype(BF16)], axis=0)
        q_hi = q.astype(BF16)
        q_lo = (q - q_hi.astype(F32)).astype(BF16)
        part = _dot_nt(q_hi, km_split)
        gate = part[:, :LANES] + part[:, LANES:] + _dot_nt(q_lo, km_split[:LANES, :])
        gate = jnp.where(lane < cur, gate, -jnp.inf)
        yield
        sel_bias = jnp.full((t, LANES), NEG, F32)
        for _ in range(MOBA_TOPK):
            top = jnp.max(gate, axis=-1, keepdims=True)
            yield
            first = jnp.min(jnp.where(gate == top, lane_f, float(LANES)), axis=-1, keepdims=True)
            yield
            pick = lane_f == first
            sel_bias = jnp.where(pick & (first < cur.astype(F32)), 0.0, sel_bias)
            gate = jnp.where(pick, -jnp.inf, gate)
        qs = (q * (LOG2E * D ** -0.5)).astype(BF16)
        return (jnp.concatenate([qs, jnp.zeros((t, LANES), BF16)], axis=-1),
                jnp.concatenate([qs, sel_bias.astype(BF16)], axis=-1))

    _attend_tile_pair(i, [prep(hd, c) for hd in range(ATT_HEADS) for c in range(2)],
                      ka_ref, vt_ref, m_ref, acc_ref, o_ref)


def moba_attention(z, cos, sin_signed, q_gain, k_gain):
    B, S, W = z.shape
    H = W // (3 * HEAD_DIM)
    assert ATT_BLOCK == MOBA_BLOCK
    t = 2 * ATT_BLOCK
    w = ATT_HEADS * HEAD_DIM
    hp = H // ATT_HEADS
    return pl.pallas_call(
        functools.partial(_moba_kernel, S=S),
        grid=(B, hp, S // t),
        in_specs=[
            pl.BlockSpec((None, t, w), lambda b, h, i: (b, i, h)),
            pl.BlockSpec((None, S, w), lambda b, h, i: (b, 0, hp + h)),
            pl.BlockSpec((None, S, w), lambda b, h, i: (b, 0, 2 * hp + h)),
            pl.BlockSpec((S, HEAD_DIM), lambda b, h, i: (0, 0)),
            pl.BlockSpec((S, HEAD_DIM), lambda b, h, i: (0, 0)),
            pl.BlockSpec((1, HEAD_DIM), lambda b, h, i: (0, 0)),
            pl.BlockSpec((1, HEAD_DIM), lambda b, h, i: (0, 0)),
        ],
        out_specs=pl.BlockSpec((None, t, w), lambda b, h, i: (b, i, h)),
        out_shape=jax.ShapeDtypeStruct((B, S, H * HEAD_DIM), BF16),
        scratch_shapes=_ATT_SCRATCH(S) + [pltpu.VMEM((ATT_HEADS, LANES, HEAD_DIM), F32)],
        compiler_params=_params("parallel", "parallel", "arbitrary"),
        name="moba_attention",
    )(z, z, z, cos, sin_signed, q_gain.reshape(1, -1), k_gain.reshape(1, -1))


META_IDX, META_RANK, META_GATE = 0, 2, 4


def _router_kernel(x_ref, g_ref, w_ref, xn_ref, meta_ref, cnt_ref, carry_ref):
    i = pl.program_id(0)
    tm = x_ref.shape[0]

    @pl.when(i == 0)
    def _():
        carry_ref[...] = jnp.zeros(carry_ref.shape, F32)

    xn = _rms(x_ref[...], g_ref[...])
    xn_ref[...] = _pack_bf16_pairs(xn)
    w = w_ref[...]
    w_hi = w.astype(BF16)
    w_lo = (w - w_hi.astype(F32)).astype(BF16)
    x_hi = xn.astype(BF16)
    x_lo = (xn - x_hi.astype(F32)).astype(BF16)
    logits = (jnp.dot(x_hi, w_hi, preferred_element_type=F32) + jnp.dot(x_hi, w_lo, preferred_element_type=F32)
              + jnp.dot(x_lo, w_hi, preferred_element_type=F32))
    lane = lax.broadcasted_iota(jnp.int32, logits.shape, 1)
    logits = jnp.where(lane < N_EXPERTS, logits, -jnp.inf)
    top1 = jnp.max(logits, axis=-1, keepdims=True)
    lane_f = lane.astype(F32)
    idx1 = jnp.min(jnp.where(logits == top1, lane_f, float(LANES)), axis=-1, keepdims=True)
    rest = jnp.where(lane_f == idx1, -jnp.inf, logits)
    top2 = jnp.max(rest, axis=-1, keepdims=True)
    idx2 = jnp.min(jnp.where(rest == top2, lane_f, float(LANES)), axis=-1, keepdims=True)
    e2 = jnp.exp(top2 - top1)
    denom = 1.0 + e2
    chosen = jnp.where((lane_f == idx1) | (lane_f == idx2), 1.0, 0.0)
    r = lax.broadcasted_iota(jnp.int32, (tm, tm), 0)
    c = lax.broadcasted_iota(jnp.int32, (tm, tm), 1)
    ahead = jnp.dot(jnp.where(r > c, 1.0, 0.0).astype(BF16), chosen.astype(BF16), preferred_element_type=F32)
    carry = carry_ref[...]
    rank = ahead + carry
    rank1 = jnp.sum(jnp.where(lane_f == idx1, rank, 0.0), axis=-1, keepdims=True)
    rank2 = jnp.sum(jnp.where(lane_f == idx2, rank, 0.0), axis=-1, keepdims=True)
    vals = (idx1, idx2, rank1, rank2, 1.0 / denom, e2 / denom)
    meta = jnp.zeros(logits.shape, F32)
    for n, v in enumerate(vals):
        meta = jnp.where(lane == n, v, meta)
    meta_ref[...] = meta
    carry = carry + jnp.sum(chosen, axis=0, keepdims=True)
    carry_ref[...] = carry
    cnt_ref[...] = jnp.broadcast_to(carry, cnt_ref.shape)


def moe_router(x, gain, w_router, *, tm):
    M, K = x.shape
    w = jnp.zeros((K, LANES), F32).at[:, :N_EXPERTS].set(w_router)
    return pl.pallas_call(
        _router_kernel,
        grid=(M // tm,),
        in_specs=[
            pl.BlockSpec((tm, K), lambda i: (i, 0)),
            pl.BlockSpec((1, K), lambda i: (0, 0)),
            pl.BlockSpec((K, LANES), lambda i: (0, 0)),
        ],
        out_specs=[
            pl.BlockSpec((tm, K // 2), lambda i: (i, 0)),
            pl.BlockSpec((tm, LANES), lambda i: (i, 0)),
            pl.BlockSpec((8, LANES), lambda i: (0, 0)),
        ],
        out_shape=[
            jax.ShapeDtypeStruct((M, K // 2), jnp.int32),
            jax.ShapeDtypeStruct((M, LANES), F32),
            jax.ShapeDtypeStruct((8, LANES), F32),
        ],
        scratch_shapes=[pltpu.VMEM((1, LANES), F32)],
        compiler_params=_params("arbitrary"),
        name="moe_router",
    )(x, gain.reshape(1, K), w)


def _sc_workers():
    info = plsc.get_sparse_core_info()
    return info.num_cores, info.num_cores * info.num_subcores


def scatter_rows(rows, dest, n_out):
    M, W = rows.shape
    nc, nw = _sc_workers()
    per_w = M // nw
    mesh = plsc.VectorSubcoreMesh(core_axis_name="c", subcore_axis_name="s")

    @functools.partial(
        pl.kernel, mesh=mesh, out_type=jax.ShapeDtypeStruct((n_out, W), rows.dtype),
        scratch_types=[pltpu.VMEM((SC_ROWS,), jnp.int32), pltpu.VMEM((SC_ROWS, W), rows.dtype),
                       pltpu.SemaphoreType.DMA])
    def kern(rows_hbm, dest_hbm, out_hbm, idx_v, rows_v, sem):
        wid = lax.axis_index("s") * nc + lax.axis_index("c")

        @pl.loop(0, per_w // SC_ROWS)
        def _(g):
            base = wid * per_w + g * SC_ROWS
            pltpu.sync_copy(rows_hbm.at[pl.ds(base, SC_ROWS)], rows_v)
            for k in range(2):
                pltpu.sync_copy(dest_hbm.at[k, pl.ds(base, SC_ROWS)], idx_v)
                pltpu.async_copy(rows_v, out_hbm.at[idx_v], sem).wait()

    return kern(rows, dest)


def gather_rows(table, idx):
    N = idx.shape[0]
    W = table.shape[1]
    nc, nw = _sc_workers()
    per_w = N // nw
    mesh = plsc.VectorSubcoreMesh(core_axis_name="c", subcore_axis_name="s")

    @functools.partial(
        pl.kernel, mesh=mesh, out_type=jax.ShapeDtypeStruct((N, W), table.dtype),
        scratch_types=[pltpu.VMEM((SC_ROWS,), jnp.int32), pltpu.VMEM((SC_ROWS, W), table.dtype),
                       pltpu.SemaphoreType.DMA])
    def kern(table_hbm, idx_hbm, out_hbm, idx_v, rows_v, sem):
        wid = lax.axis_index("s") * nc + lax.axis_index("c")

        @pl.loop(0, per_w // SC_ROWS)
        def _(g):
            base = wid * per_w + g * SC_ROWS
            pltpu.sync_copy(idx_hbm.at[pl.ds(base, SC_ROWS)], idx_v)
            pltpu.async_copy(table_hbm.at[idx_v], rows_v, sem).wait()
            pltpu.sync_copy(rows_v, out_hbm.at[pl.ds(base, SC_ROWS)])

    return kern(table, idx)


def _combine_kernel(h_ref, y1_ref, y2_ref, meta_ref, o_ref):
    g1 = meta_ref[:, META_GATE:META_GATE + 1]
    g2 = meta_ref[:, META_GATE + 1:META_GATE + 2]
    o_ref[...] = (h_ref[...] + g1 * _unpack_bf16_pairs(y1_ref[...]).astype(F32)
                  + g2 * _unpack_bf16_pairs(y2_ref[...]).astype(F32))


def moe_combine(h, y_pairs, meta):
    M, K = h.shape
    p = PERM_TILE
    nt = M // p
    return pl.pallas_call(
        _combine_kernel,
        grid=(nt,),
        in_specs=[
            pl.BlockSpec((p, K), lambda i: (i, 0)),
            pl.BlockSpec((p, K // 2), lambda i: (i, 0)),
            pl.BlockSpec((p, K // 2), lambda i: (nt + i, 0)),
            pl.BlockSpec((p, LANES), lambda i: (i, 0)),
        ],
        out_specs=pl.BlockSpec((p, K), lambda i: (i, 0)),
        out_shape=jax.ShapeDtypeStruct((M, K), F32),
        compiler_params=_params("parallel"),
        name="moe_combine",
    )(h, y_pairs, y_pairs, meta)


def _expert_ffn_kernel(te_ref, tv_ref, x_ref, wg_ref, wu_ref, wd_ref, o_ref, xb_ref, acc_ref):
    del te_ref
    i = pl.program_id(0)
    j = pl.program_id(1)
    tm = x_ref.shape[0]
    valid = tv_ref[i]

    def swiglu_rows(rows):
        @pl.when(j == 0)
        def _():
            xb_ref[0:rows, :] = _unpack_bf16_pairs(x_ref[0:rows, :])

        x = xb_ref[0:rows, :]
        gt = jnp.dot(x, wg_ref[...], preferred_element_type=F32)
        up = jnp.dot(x, wu_ref[...], preferred_element_type=F32)
        act = (gt / (1.0 + jnp.exp(-gt)) * up).astype(BF16)
        part = jnp.dot(act, wd_ref[...], preferred_element_type=F32)

        @pl.when(j == 0)
        def _():
            acc_ref[0:rows, :] = part

        @pl.when(j > 0)
        def _():
            acc_ref[0:rows, :] += part

        @pl.when(j == pl.num_programs(1) - 1)
        def _():
            o_ref[0:rows, :] = _pack_bf16_pairs(acc_ref[0:rows, :])
            if rows < tm:
                o_ref[rows:tm, :] = jnp.zeros((tm - rows, o_ref.shape[1]), o_ref.dtype)

    @pl.when(valid > tm // 2)
    def _():
        swiglu_rows(tm)

    @pl.when((valid > 0) & (valid <= tm // 2))
    def _():
        swiglu_rows(tm // 2)

    @pl.when((valid == 0) & (j == 0))
    def _():
        o_ref[...] = jnp.zeros(o_ref.shape, o_ref.dtype)


def expert_ffn(xs, tile_expert, tile_valid, w_gate_up, w_down, *, tm, tf):
    R = xs.shape[0]
    E, F, K = w_down.shape
    nf = F // tf
    live = lambda i, tv: tv[i] > 0
    col = lambda i, j, tv: jnp.where(live(i, tv), j, nf - 1)
    grid_spec = pltpu.PrefetchScalarGridSpec(
        num_scalar_prefetch=2,
        grid=(R // tm, nf),
        in_specs=[
            pl.BlockSpec((tm, K // 2), lambda i, j, te, tv: (te[R // tm + i], 0)),
            pl.BlockSpec((None, K, tf), lambda i, j, te, tv: (te[i], 0, col(i, j, tv))),
            pl.BlockSpec((None, K, tf), lambda i, j, te, tv: (te[i], 0, nf + col(i, j, tv))),
            pl.BlockSpec((None, tf, K), lambda i, j, te, tv: (te[i], col(i, j, tv), 0)),
        ],
        out_specs=pl.BlockSpec((tm, K // 2), lambda i, j, te, tv: (i, 0)),
        scratch_shapes=[pltpu.VMEM((tm, K), BF16), pltpu.VMEM((tm, K), F32)],
    )
    return pl.pallas_call(
        _expert_ffn_kernel,
        grid_spec=grid_spec,
        out_shape=jax.ShapeDtypeStruct((R, K // 2), jnp.int32),
        compiler_params=_params("arbitrary", "arbitrary"),
        name="moe_expert_ffn",
    )(tile_expert, tile_valid, xs, w_gate_up, w_gate_up, w_down)


def moe_residual(h, gain, w_router, w_gate_up, w_down):
    M, K = h.shape
    p = PERM_TILE
    tm = EXPERT_TILE
    n_rows = 2 * M + N_EXPERTS * tm
    xn, meta, cnt = moe_router(h, gain, w_router, tm=p)

    counts = cnt[0, :N_EXPERTS].astype(jnp.int32)
    padded = (counts + tm - 1) // tm * tm
    ends = jnp.cumsum(padded)
    offsets = ends - padded
    n_tiles = n_rows // tm
    tile_row = jnp.arange(n_tiles) * tm
    n_used = ends[-1] // tm
    last_used = jnp.minimum(tile_row // tm, n_used - 1)
    expert_of = lambda row: jnp.minimum(jnp.sum(ends[None, :] <= row[:, None], axis=1), N_EXPERTS - 1)
    tile_expert = expert_of(last_used * tm)
    row_in_expert = tile_row - jnp.sum(jnp.where(tile_expert[:, None] == jnp.arange(N_EXPERTS), offsets, 0), axis=1)
    own_count = jnp.sum(jnp.where(tile_expert[:, None] == jnp.arange(N_EXPERTS), counts, 0), axis=1)
    tile_valid = jnp.where(tile_row // tm < n_used, jnp.clip(own_count - row_in_expert, 0, tm), 0)
    tile_tables = jnp.concatenate([tile_expert, last_used]).astype(jnp.int32)

    idx = meta[:, META_IDX:META_IDX + 2].astype(jnp.int32)
    rank = meta[:, META_RANK:META_RANK + 2].astype(jnp.int32)
    dest = (jnp.sum(jnp.where(idx[:, :, None] == jnp.arange(N_EXPERTS), offsets, 0), axis=-1) + rank).T

    xs = scatter_rows(xn, dest, n_rows)
    ys = expert_ffn(xs, tile_tables, tile_valid.astype(jnp.int32), w_gate_up, w_down, tm=tm, tf=896)
    return moe_combine(h, gather_rows(ys, dest.reshape(-1)), meta)


def _ffn_kernel(x_ref, g_ref, wg_ref, wu_ref, wd_ref, o_ref, xn_ref, acc_ref):
    j = pl.program_id(1)

    @pl.when(j == 0)
    def _():
        x = x_ref[...]
        xn_ref[...] = _rms(x, g_ref[...]).astype(BF16)
        acc_ref[...] = x

    xn = xn_ref[...]
    gt = jnp.dot(xn, wg_ref[...], preferred_element_type=F32)
    up = jnp.dot(xn, wu_ref[...], preferred_element_type=F32)
    act = gt / (1.0 + jnp.exp(-gt)) * up
    acc_ref[...] += jnp.dot(act.astype(BF16), wd_ref[...], preferred_element_type=F32)

    @pl.when(j == pl.num_programs(1) - 1)
    def _():
        o_ref[...] = acc_ref[...]


def ffn_residual(x, gain, w_gate_up, w_down, *, tm, tf):
    M, K = x.shape
    F = w_down.shape[0]
    nf = F // tf
    return pl.pallas_call(
        _ffn_kernel,
        grid=(M // tm, nf),
        in_specs=[
            pl.BlockSpec((tm, K), lambda i, j: (i, 0)),
            pl.BlockSpec((1, K), lambda i, j: (0, 0)),
            pl.BlockSpec((K, tf), lambda i, j: (0, j)),
            pl.BlockSpec((K, tf), lambda i, j: (0, nf + j)),
            pl.BlockSpec((tf, K), lambda i, j: (j, 0)),
        ],
        out_specs=pl.BlockSpec((tm, K), lambda i, j: (i, 0)),
        out_shape=jax.ShapeDtypeStruct((M, K), F32),
        scratch_shapes=[pltpu.VMEM((tm, K), BF16), pltpu.VMEM((tm, K), F32)],
        compiler_params=_params("parallel", "arbitrary"),
        name="ffn_residual",
    )(x, gain.reshape(1, K), w_gate_up, w_gate_up, w_down)


def _even_mix(h, B, S, norm1, w_in, f_bias, q_norm, k_norm, conv_w, a_log, dt_bias, o_norm, w_out):
    M, D = h.shape
    fw, gw = FOX_HEADS * HEAD_DIM, GDN_HEADS * HEAD_DIM
    o_ff = 3 * fw
    o_gq = o_ff + FOX_HEADS
    o_ga = o_gq + 3 * gw
    o_gb = o_ga + GDN_HEADS
    o_gg = o_gb + GDN_HEADS
    w_big = jnp.concatenate([w_in[:, :o_ff], w_in[:, o_gq:o_ga], w_in[:, o_gg:]], axis=1).astype(BF16)
    w_small = jnp.concatenate([w_in[:, o_ff:o_gq], w_in[:, o_ga:o_gg],
                               jnp.zeros((D, LANES - FOX_HEADS - 2 * GDN_HEADS), F32)], axis=1).astype(BF16)
    z, zs = norm_matmul(h, norm1, w_big, w_small, tm=ROW_TILE, tn=1792)
    par = jnp.zeros((8, LANES), F32)
    par = par.at[0, LANE_F:LANE_F + FOX_HEADS].set(f_bias).at[0, LANE_A:LANE_A + GDN_HEADS].set(dt_bias)
    par = par.at[1, LANE_A:LANE_A + GDN_HEADS].set(a_log)
    col, row = even_gates(zs.reshape(B, S, LANES), par)
    z = z.reshape(B, S, -1)
    fox = fox_attention(z, col, q_norm, k_norm)
    gdn = gdn_mixer(z, conv_w, col, row, o_norm)
    w_out = w_out.astype(BF16)
    return matmul_residual(h, [(fox.reshape(M, fw), w_out[:fw]), (gdn.reshape(M, gw), w_out[fw:])], tm=ROW_TILE, tn=1024)


def _odd_mix(h, B, S, norm1, w_qkv, q_norm, k_norm, w_out):
    M, D = h.shape
    z = norm_matmul(h, norm1, w_qkv.astype(BF16), tm=ROW_TILE, tn=1536).reshape(B, S, -1)
    half = HEAD_DIM // 2
    inv = jnp.power(ROPE_THETA, -jnp.arange(half, dtype=F32) / half)
    ang = jnp.arange(S, dtype=F32)[:, None] * inv[None, :]
    cos, sin = jnp.cos(ang), jnp.sin(ang)
    cos_full = jnp.concatenate([cos, cos], axis=-1)
    sin_signed = jnp.concatenate([-sin, sin], axis=-1)
    att = moba_attention(z, cos_full, sin_signed, q_norm, k_norm)
    return matmul_residual(h, [(att.reshape(M, -1), w_out.astype(BF16))], tm=ROW_TILE, tn=1024)


def _odd_moe(h, norm2, w_router, w_gate_up, w_down):
    return moe_residual(h, norm2, w_router, w_gate_up.astype(BF16), w_down.astype(BF16))


def kernel(x, e_norm1, e_w_in, e_fox_f_bias, e_fox_q_norm, e_fox_k_norm, e_gdn_conv, e_gdn_a_log,
           e_gdn_dt_bias, e_gdn_o_norm, e_w_out, e_norm2, e_ffn_w_gate_up, e_ffn_w_down,
           o_norm1, o_w_qkv, o_q_norm, o_k_norm, o_w_out, o_norm2, o_router, o_exp_w_gate_up, o_exp_w_down):
    B, S, D = x.shape
    h = x.reshape(B * S, D)
    depth = e_norm1.shape[0] + o_norm1.shape[0]
    for layer in range(depth):
        i = layer // 2
        if layer % 2 == 0:
            h = _even_mix(h, B, S, e_norm1[i], e_w_in[i], e_fox_f_bias[i], e_fox_q_norm[i], e_fox_k_norm[i],
                          e_gdn_conv[i], e_gdn_a_log[i], e_gdn_dt_bias[i], e_gdn_o_norm[i], e_w_out[i])
            h = ffn_residual(h, e_norm2[i], e_ffn_w_gate_up[i].astype(BF16), e_ffn_w_down[i].astype(BF16),
                             tm=ROW_TILE // 2, tf=1408)
        else:
            h = _odd_mix(h, B, S, o_norm1[i], o_w_qkv[i], o_q_norm[i], o_k_norm[i], o_w_out[i])
            h = _odd_moe(h, o_norm2[i], o_router[i], o_exp_w_gate_up[i], o_exp_w_down[i])
    return h.reshape(B, S, D)
```

```python
import functools

import jax
import jax.numpy as jnp
from jax import lax
from jax.experimental import pallas as pl
from jax.experimental.pallas import tpu as pltpu
from jax.experimental.pallas import tpu_sc as plsc

F32 = jnp.float32
BF16 = jnp.bfloat16

HEAD_DIM = 128
FOX_HEADS = 4
GDN_HEADS = 4
CONV_WIDTH = 4
MOBA_BLOCK = 256
MOBA_TOPK = 3
N_EXPERTS = 8
ROPE_THETA = 10000.0
EPS = 1e-6

LANES = 128
GDN_CHUNK = 128
INV_BLOCK = 16
GDN_HEADS_PER_STEP = 4
GDN_CHUNKS_PER_STEP = 2
NEG = -(2.0 ** 100)
LOG2E = 1.4426950408889634
ATT_BLOCK = 256
ATT_HEADS = 2
V_ROWS = HEAD_DIM + 16
PERM_TILE = 512
SC_ROWS = 64
ROW_TILE = 1024
EXPERT_TILE = 1024
VMEM_LIMIT_BYTES = 56 * 1024 * 1024

FOX_Q0, FOX_K0, FOX_V0 = 0, 4, 8
GDN_Q0, GDN_K0, GDN_V0, GDN_G0 = 12, 16, 20, 24
LANE_F, LANE_A, LANE_B = 0, 4, 8


def _params(*sem):
    return pltpu.CompilerParams(dimension_semantics=sem, vmem_limit_bytes=VMEM_LIMIT_BYTES)


def _rms(x, gain):
    return x * lax.rsqrt(jnp.mean(x * x, axis=-1, keepdims=True) + EPS) * gain


def _dot_nt(a, b, **kw):
    return lax.dot_general(a, b, (((1,), (1,)), ((), ())), preferred_element_type=F32, **kw)


def _pick_lane(x, lane_idx):
    lane = lax.broadcasted_iota(jnp.int32, x.shape, 1)
    return jnp.sum(jnp.where(lane == lane_idx, x, 0.0), axis=-1, keepdims=True)


def _pack_bf16_pairs(x):
    n = x.shape[1] // 2
    hi = pltpu.bitcast(x[:, :n].astype(BF16).astype(F32), jnp.uint32)
    lo = pltpu.bitcast(x[:, n:].astype(BF16).astype(F32), jnp.uint32)
    return pltpu.bitcast(hi | (lo >> 16), jnp.int32)


def _unpack_bf16_pairs(w):
    u = pltpu.bitcast(w, jnp.uint32)
    hi = pltpu.bitcast(u & jnp.uint32(0xFFFF0000), F32).astype(BF16)
    lo = pltpu.bitcast(u << 16, F32).astype(BF16)
    return jnp.concatenate([hi, lo], axis=1)


def _norm_mm_kernel(x_ref, g_ref, w_ref, *rest, has_aux):
    if has_aux:
        waux_ref, o_ref, oaux_ref, xn_ref = rest
    else:
        o_ref, xn_ref = rest

    @pl.when(pl.program_id(1) == 0)
    def _():
        xn = _rms(x_ref[...], g_ref[...]).astype(BF16)
        xn_ref[...] = xn
        if has_aux:
            oaux_ref[...] = jnp.dot(xn, waux_ref[...], preferred_element_type=F32)

    o_ref[...] = jnp.dot(xn_ref[...], w_ref[...], preferred_element_type=F32).astype(o_ref.dtype)


def norm_matmul(x, gain, w, w_aux=None, *, tm, tn, out_dtype=BF16):
    M, K = x.shape
    N = w.shape[1]
    has_aux = w_aux is not None
    in_specs = [
        pl.BlockSpec((tm, K), lambda i, j: (i, 0)),
        pl.BlockSpec((1, K), lambda i, j: (0, 0)),
        pl.BlockSpec((K, tn), lambda i, j: (0, j)),
    ]
    out_shape = [jax.ShapeDtypeStruct((M, N), out_dtype)]
    out_specs = [pl.BlockSpec((tm, tn), lambda i, j: (i, j))]
    args = [x, gain.reshape(1, K), w]
    if has_aux:
        in_specs.append(pl.BlockSpec((K, LANES), lambda i, j: (0, 0)))
        out_shape.append(jax.ShapeDtypeStruct((M, LANES), F32))
        out_specs.append(pl.BlockSpec((tm, LANES), lambda i, j: (i, 0)))
        args.append(w_aux)
    res = pl.pallas_call(
        functools.partial(_norm_mm_kernel, has_aux=has_aux),
        grid=(M // tm, N // tn),
        in_specs=in_specs,
        out_specs=out_specs,
        out_shape=out_shape,
        scratch_shapes=[pltpu.VMEM((tm, K), BF16)],
        compiler_params=_params("parallel", "arbitrary"),
        name="norm_matmul",
    )(*args)
    return res if has_aux else res[0]


def _mm_res_kernel(*refs, n_in):
    res_ref = refs[0]
    o_ref = refs[1 + 2 * n_in]
    acc = res_ref[...]
    for t in range(n_in):
        acc = acc + jnp.dot(refs[1 + 2 * t][...], refs[2 + 2 * t][...], preferred_element_type=F32)
    o_ref[...] = acc


def matmul_residual(res, pairs, *, tm, tn):
    M, N = res.shape
    in_specs = [pl.BlockSpec((tm, tn), lambda i, j: (i, j))]
    args = [res]
    for a, w in pairs:
        K = a.shape[1]
        in_specs.append(pl.BlockSpec((tm, K), lambda i, j: (i, 0)))
        in_specs.append(pl.BlockSpec((K, tn), lambda i, j: (0, j)))
        args += [a, w]
    return pl.pallas_call(
        functools.partial(_mm_res_kernel, n_in=len(pairs)),
        grid=(M // tm, N // tn),
        in_specs=in_specs,
        out_specs=pl.BlockSpec((tm, tn), lambda i, j: (i, j)),
        out_shape=jax.ShapeDtypeStruct((M, N), F32),
        compiler_params=_params("parallel", "arbitrary"),
        name="matmul_residual",
    )(*args)


def _gate_kernel(zs_ref, par_ref, col_ref, row_ref, *, S):
    C = GDN_CHUNK
    bias = par_ref[0:1, :]
    neg_a = -jnp.exp(par_ref[1:2, :])
    r = lax.broadcasted_iota(jnp.int32, (C, C), 0)
    c = lax.broadcasted_iota(jnp.int32, (C, C), 1)
    tril = (r >= c).astype(F32)
    lane = lax.broadcasted_iota(jnp.int32, (C, LANES), 1)

    def body(n, carry):
        sl = pl.ds(pl.multiple_of(n * C, C), C)
        z = zs_ref[sl, :]
        t = z + bias
        soft = jnp.log(1.0 + jnp.exp(-jnp.abs(t)))
        log_f = jnp.minimum(t, 0.0) - soft
        g = neg_a * (jnp.maximum(t, 0.0) + soft)
        beta = 1.0 / (1.0 + jnp.exp(-z))
        u = jnp.where(lane < LANE_A, log_f, jnp.where(lane < LANE_B, g, 0.0))
        cs = jnp.dot(tril, u, preferred_element_type=F32, precision=lax.Precision.HIGHEST)
        cs = cs + jnp.where(lane < LANE_A, carry, 0.0)
        out = jnp.where(lane < LANE_B, cs, beta)
        col_ref[sl, :] = out
        out_t = out.T
        for hd in range(GDN_HEADS):
            row_ref[hd, :, sl] = out_t[LANE_A + hd:LANE_A + hd + 1, :]
        return cs[C - 1:C, :]

    lax.fori_loop(0, S // C, body, jnp.zeros((1, LANES), F32))


def even_gates(zs, par):
    B, S, _ = zs.shape
    return pl.pallas_call(
        functools.partial(_gate_kernel, S=S),
        grid=(B,),
        in_specs=[
            pl.BlockSpec((None, S, LANES), lambda b: (b, 0, 0)),
            pl.BlockSpec((8, LANES), lambda b: (0, 0)),
        ],
        out_specs=[
            pl.BlockSpec((None, S, LANES), lambda b: (b, 0, 0)),
            pl.BlockSpec((None, GDN_HEADS, 1, S), lambda b: (b, 0, 0, 0)),
        ],
        out_shape=[jax.ShapeDtypeStruct((B, S, LANES), F32), jax.ShapeDtypeStruct((B, GDN_HEADS, 1, S), F32)],
        compiler_params=_params("parallel"),
        name="even_gates",
    )(zs, par)


def _split3(x):
    hi = x.astype(BF16).astype(F32)
    mid = (x - hi).astype(BF16).astype(F32)
    lo = (x - hi - mid).astype(BF16).astype(F32)
    return hi, mid, lo


def _interleave(gens):
    out = [None] * len(gens)
    live = list(range(len(gens)))
    while live:
        for n in list(live):
            try:
                next(gens[n])
            except StopIteration as stop:
                out[n] = stop.value
                live.remove(n)
    return out


def _attend(state, c, qa, ka_ref, vt_ref, key0, nkeys, keep=None):
    ks = pl.ds(pl.multiple_of(key0, ATT_BLOCK), nkeys)
    st = _dot_nt(ka_ref[ks, :], qa)
    yield
    if keep is not None:
        st = jnp.where(keep, st, NEG)
    m_new = jnp.max(st, axis=0, keepdims=True)
    if state[c] is not None:
        m_old, acc_old = state[c]
        m_new = jnp.maximum(m_old, m_new)
    p = jnp.exp2(st - m_new).astype(BF16)
    pv = jnp.dot(vt_ref[:, ks], p, preferred_element_type=F32)
    state[c] = (m_new, pv if state[c] is None else acc_old * jnp.exp2(m_old - m_new) + pv)


def _attend_tile_pair(i, prep, ka_ref, vt_ref, m_ref, acc_ref, o_ref):
    t = ATT_BLOCK
    chains = [(hd, c) for c in range(2) for hd in range(ATT_HEADS)]
    num = lambda hd, c: 2 * hd + c
    qa = _interleave(prep)

    def step(state, hd, c, which, **kw):
        return _attend(state, num(hd, c), qa[num(hd, c)][which], ka_ref.at[hd], vt_ref.at[hd], **kw)

    def save(state):
        m_ref[...] = jnp.stack([state[n][0] for n in range(len(chains))])
        acc_ref[...] = jnp.stack([state[n][1] for n in range(len(chains))])

    def past_blocks(key0, n_steps):
        state = {n: (m_ref[n], acc_ref[n]) for n in range(len(chains))}
        _interleave([step(state, hd, c, 1, key0=key0 + s * 2 * t, nkeys=2 * t)
                     for s in range(n_steps) for hd, c in chains])
        save(state)

    state = {n: None for n in range(len(chains))}
    keep = _causal_keep()
    _interleave([step(state, hd, c, 0, key0=(2 * i + c) * t, nkeys=t, keep=keep) for hd, c in chains]
                + [step(state, hd, 1, 1, key0=(2 * i) * t, nkeys=t) for hd in range(ATT_HEADS)])
    save(state)

    def four_blocks(g, _):
        past_blocks(g * (4 * t), 2)
        return 0

    lax.fori_loop(0, i // 2, four_blocks, 0)

    @pl.when((i & 1) != 0)
    def _():
        past_blocks((i // 2) * (4 * t), 1)

    for hd, c in chains:
        acc = acc_ref[num(hd, c)]
        out_t = acc[:HEAD_DIM, :] * (1.0 / acc[HEAD_DIM:HEAD_DIM + 1, :])
        o_ref[c * t:(c + 1) * t, hd * HEAD_DIM:(hd + 1) * HEAD_DIM] = out_t.T.astype(o_ref.dtype)


def _store_vt(vt_ref, v, sl):
    vt_ref[0:HEAD_DIM, sl] = v.astype(F32).T.astype(BF16)
    vt_ref[HEAD_DIM:V_ROWS, sl] = jnp.ones((V_ROWS - HEAD_DIM, ATT_BLOCK), BF16)


def _causal_keep():
    key = lax.broadcasted_iota(jnp.int32, (ATT_BLOCK, ATT_BLOCK), 0)
    qry = lax.broadcasted_iota(jnp.int32, (ATT_BLOCK, ATT_BLOCK), 1)
    return key <= qry


_ATT_SCRATCH = lambda S: [pltpu.VMEM((ATT_HEADS, S, 2 * HEAD_DIM), BF16), pltpu.VMEM((ATT_HEADS, V_ROWS, S), BF16),
                          pltpu.VMEM((2 * ATT_HEADS, 1, ATT_BLOCK), F32),
                          pltpu.VMEM((2 * ATT_HEADS, V_ROWS, ATT_BLOCK), F32)]


def _fox_kernel(q_ref, k_ref, v_ref, col_ref, qg_ref, kg_ref, o_ref, ka_ref, vt_ref, m_ref, acc_ref, *, S):
    h0 = pl.program_id(1) * ATT_HEADS
    i = pl.program_id(2)
    t = ATT_BLOCK
    D = HEAD_DIM
    lane = lax.broadcasted_iota(jnp.int32, (t, LANES), 1)

    @pl.when(i == 0)
    def _():
        def prep_keys(n, _):
            sl = pl.ds(pl.multiple_of(n * t, t), t)
            for hd in range(ATT_HEADS):
                cols = slice(hd * D, (hd + 1) * D)
                ka_ref[hd, sl, 0:D] = _rms(k_ref[sl, cols].astype(F32), kg_ref[...]).astype(BF16)
                hi, mid, lo = _split3(-LOG2E * _pick_lane(col_ref[sl, :], LANE_F + h0 + hd))
                ka_ref[hd, sl, D:] = jnp.where(
                    lane < 3, 1.0, jnp.where(lane == 3, hi, jnp.where(lane == 4, mid, jnp.where(lane == 5, lo, 0.0)))
                ).astype(BF16)
                _store_vt(vt_ref.at[hd], v_ref[sl, cols], sl)
            return 0

        lax.fori_loop(0, S // t, prep_keys, 0)

    def prep(hd, c):
        q = q_ref[c * t:(c + 1) * t, hd * D:(hd + 1) * D].astype(F32)
        qn = (_rms(q, qg_ref[...]) * (LOG2E * D ** -0.5)).astype(BF16)
        yield
        qsl = pl.ds(pl.multiple_of((2 * i + c) * t, t), t)
        hi, mid, lo = _split3(LOG2E * _pick_lane(col_ref[qsl, :], LANE_F + h0 + hd))
        aug = jnp.where(lane == 0, hi,
                        jnp.where(lane == 1, mid, jnp.where(lane == 2, lo, jnp.where(lane < 6, 1.0, 0.0))))
        qa = jnp.concatenate([qn, aug.astype(BF16)], axis=-1)
        return qa, qa

    _attend_tile_pair(i, [prep(hd, c) for hd in range(ATT_HEADS) for c in range(2)],
                      ka_ref, vt_ref, m_ref, acc_ref, o_ref)


def fox_attention(z, col, q_gain, k_gain):
    B, S, _ = z.shape
    t = 2 * ATT_BLOCK
    w = ATT_HEADS * HEAD_DIM
    return pl.pallas_call(
        functools.partial(_fox_kernel, S=S),
        grid=(B, FOX_HEADS // ATT_HEADS, S // t),
        in_specs=[
            pl.BlockSpec((None, t, w), lambda b, h, i: (b, i, FOX_Q0 // ATT_HEADS + h)),
            pl.BlockSpec((None, S, w), lambda b, h, i: (b, 0, FOX_K0 // ATT_HEADS + h)),
            pl.BlockSpec((None, S, w), lambda b, h, i: (b, 0, FOX_V0 // ATT_HEADS + h)),
            pl.BlockSpec((None, S, LANES), lambda b, h, i: (b, 0, 0)),
            pl.BlockSpec((1, HEAD_DIM), lambda b, h, i: (0, 0)),
            pl.BlockSpec((1, HEAD_DIM), lambda b, h, i: (0, 0)),
        ],
        out_specs=pl.BlockSpec((None, t, w), lambda b, h, i: (b, i, h)),
        out_shape=jax.ShapeDtypeStruct((B, S, FOX_HEADS * HEAD_DIM), BF16),
        scratch_shapes=_ATT_SCRATCH(S),
        compiler_params=_params("parallel", "parallel", "arbitrary"),
        name="fox_attention",
    )(z, z, z, col, q_gain.reshape(1, -1), k_gain.reshape(1, -1))


def _unit_lower_inverse(m):
    n = m.shape[0]
    r = lax.broadcasted_iota(jnp.int32, (n, n), 0)
    c = lax.broadcasted_iota(jnp.int32, (n, n), 1)
    eye = (r == c).astype(F32)

    def same_block(b):
        return (r // b) == (c // b)

    p = jnp.where(same_block(INV_BLOCK), m, 0.0)
    inv = eye - p
    k = 2
    while k < INV_BLOCK:
        pb = p.astype(BF16)
        p = jnp.dot(pb, pb, preferred_element_type=F32)
        yield
        inv = jnp.dot(inv.astype(BF16), (eye + p).astype(BF16), preferred_element_type=F32)
        yield
        k *= 2
    b = INV_BLOCK
    while b < n:
        off = jnp.where(same_block(2 * b), jnp.where(same_block(b), 0.0, m), 0.0).astype(BF16)
        ib = inv.astype(BF16)
        left = jnp.dot(ib, off, preferred_element_type=F32).astype(BF16)
        yield
        inv = inv - jnp.dot(left, ib, preferred_element_type=F32)
        yield
        b *= 2
    return inv


def _gdn_kernel(q_ref, k_ref, v_ref, gg_ref, wq_ref, wk_ref, wv_ref, col_ref, row_ref, on_ref,
                o_ref, pad_ref, qs_ref, ks_ref, vs_ref, *, S, rows, hb):
    h0 = pl.program_id(1) * hb
    C = GDN_CHUNK
    D = HEAD_DIM

    def conv_silu(x_ref, w_ref, dst_ref, hh, mode):
        cols = slice(hh * D, (hh + 1) * D)
        pad_ref[0:8, :] = jnp.zeros((8, D), F32)

        def fill(n, _):
            src = pl.ds(pl.multiple_of(n * rows, rows), rows)
            pad_ref[pl.ds(pl.multiple_of(8 + n * rows, 8), rows), :] = x_ref[src, cols].astype(F32)
            return 0

        lax.fori_loop(0, S // rows, fill, 0)
        w = w_ref[:, cols]

        def conv(n, _):
            base = pl.multiple_of(n * rows, rows)
            win = pad_ref[pl.ds(base, rows + 8), :]
            y = jnp.zeros((rows, D), F32)
            for tap in range(CONV_WIDTH):
                lead = 8 - (CONV_WIDTH - 1) + tap
                y = y + w[tap:tap + 1, :] * pltpu.roll(win, rows + 8 - lead, 0)[0:rows, :]
            y = y / (1.0 + jnp.exp(-y))
            if mode != "v":
                y = y * lax.rsqrt(jnp.sum(y * y, axis=-1, keepdims=True) + EPS)
            if mode == "q":
                y = y * D ** -0.5
            dst_ref[hh, pl.ds(base, rows), :] = y.astype(dst_ref.dtype)
            return 0

        lax.fori_loop(0, S // rows, conv, 0)

    for hh in range(hb):
        conv_silu(q_ref, wq_ref, qs_ref, hh, "q")
        conv_silu(k_ref, wk_ref, ks_ref, hh, "k")
        conv_silu(v_ref, wv_ref, vs_ref, hh, "v")

    r = lax.broadcasted_iota(jnp.int32, (C, C), 0)
    c = lax.broadcasted_iota(jnp.int32, (C, C), 1)
    incl = r >= c
    strict = r > c

    def chunk_local(hh, sl, tab):
        q = qs_ref[hh, sl, :].astype(F32)
        k = ks_ref[hh, sl, :].astype(F32)
        v = vs_ref[hh, sl, :].astype(F32)
        gcol = _pick_lane(tab, LANE_A + h0 + hh)
        beta = _pick_lane(tab, LANE_B + h0 + hh)
        grow = row_ref[hh, :, sl]
        glast = gcol[C - 1:C, :]
        decay = jnp.where(incl, jnp.exp(jnp.where(incl, gcol - grow, 0.0)), 0.0)
        eg = jnp.exp(gcol)
        kb = k * beta
        kbf = k.astype(BF16)
        m = jnp.where(strict, _dot_nt(kb.astype(BF16), kbf) * decay, 0.0)
        attn = (_dot_nt(q.astype(BF16), kbf) * decay).astype(BF16)
        yield
        tinv = (yield from _unit_lower_inverse(m)).astype(BF16)
        rhs = jnp.concatenate([v * beta, kb * eg], axis=-1).astype(BF16)
        sol = jnp.dot(tinv, rhs, preferred_element_type=F32)
        yield
        gate = gg_ref[sl, hh * D:(hh + 1) * D].astype(F32)
        return dict(u=sol[:, :D], w=sol[:, D:].astype(BF16), attn=attn, qg=(q * eg).astype(BF16),
                    kg_t=(k * jnp.exp(glast - gcol)).T.astype(BF16), keep=jnp.exp(glast),
                    gate=gate / (1.0 + jnp.exp(-gate)))

    def chunk_state(parts, state):
        outs = []
        for c in parts:
            sb = state.astype(BF16)
            v_new = c["u"] - jnp.dot(c["w"], sb, preferred_element_type=F32)
            o_state = jnp.dot(c["qg"], sb, preferred_element_type=F32)
            yield
            vb = v_new.astype(BF16)
            o = o_state + jnp.dot(c["attn"], vb, preferred_element_type=F32)
            state = state * c["keep"] + jnp.dot(c["kg_t"], vb, preferred_element_type=F32)
            yield
            outs.append((_rms(o, on_ref[...]) * c["gate"]).astype(o_ref.dtype))
        return state, outs

    def chunk_group(n, states):
        sls = [pl.ds(pl.multiple_of((n * GDN_CHUNKS_PER_STEP + g) * C, C), C) for g in range(GDN_CHUNKS_PER_STEP)]
        tabs = [col_ref[sl, :] for sl in sls]
        parts = _interleave([chunk_local(hh, sls[g], tabs[g]) for g in range(GDN_CHUNKS_PER_STEP) for hh in range(hb)])
        res = _interleave([chunk_state([parts[g * hb + hh] for g in range(GDN_CHUNKS_PER_STEP)], states[hh])
                           for hh in range(hb)])
        for g in range(GDN_CHUNKS_PER_STEP):
            o_ref[sls[g], :] = jnp.concatenate([outs[g] for _, outs in res], axis=-1)
        return tuple(s for s, _ in res)

    lax.fori_loop(0, S // (C * GDN_CHUNKS_PER_STEP), chunk_group, tuple(jnp.zeros((D, D), F32) for _ in range(hb)))


def gdn_mixer(z, conv_w, col, row, o_gain):
    B, S, _ = z.shape
    D = HEAD_DIM
    hb = GDN_HEADS_PER_STEP
    rows = 256
    seq = lambda off: pl.BlockSpec((None, S, hb * D), lambda b, h: (b, 0, off // hb + h),
                                   pipeline_mode=pl.Buffered(1))
    cw = lambda off: pl.BlockSpec((CONV_WIDTH, hb * D), lambda b, h: (0, off // hb + h))
    return pl.pallas_call(
        functools.partial(_gdn_kernel, S=S, rows=rows, hb=hb),
        grid=(B, GDN_HEADS // hb),
        in_specs=[
            seq(GDN_Q0), seq(GDN_K0), seq(GDN_V0), seq(GDN_G0),
            cw(0), cw(GDN_HEADS), cw(2 * GDN_HEADS),
            pl.BlockSpec((None, S, LANES), lambda b, h: (b, 0, 0)),
            pl.BlockSpec((None, hb, 1, S), lambda b, h: (b, h, 0, 0)),
            pl.BlockSpec((1, D), lambda b, h: (0, 0)),
        ],
        out_specs=pl.BlockSpec((None, S, hb * D), lambda b, h: (b, 0, h)),
        out_shape=jax.ShapeDtypeStruct((B, S, GDN_HEADS * D), BF16),
        scratch_shapes=[pltpu.VMEM((S + 8, D), F32)] + [pltpu.VMEM((hb, S, D), BF16)] * 3,
        compiler_params=_params("parallel", "parallel"),
        name="gdn_mixer",
    )(z, z, z, z, conv_w, conv_w, conv_w, col, row, o_gain.reshape(1, D))


def _rope(x, cos, sin_signed):
    return x * cos + pltpu.roll(x, HEAD_DIM // 2, 1) * sin_signed


def _moba_kernel(q_ref, k_ref, v_ref, cos_ref, sin_ref, qg_ref, kg_ref, o_ref,
                 ka_ref, vt_ref, m_ref, acc_ref, kmean_ref, *, S):
    i = pl.program_id(2)
    t = ATT_BLOCK
    D = HEAD_DIM
    lane = lax.broadcasted_iota(jnp.int32, (t, LANES), 1)

    @pl.when(i == 0)
    def _():
        kmean_ref[...] = jnp.zeros(kmean_ref.shape, F32)

        def prep_keys(n, _):
            sl = pl.ds(pl.multiple_of(n * t, t), t)
            for hd in range(ATT_HEADS):
                cols = slice(hd * D, (hd + 1) * D)
                k = _rope(_rms(k_ref[sl, cols].astype(F32), kg_ref[...]), cos_ref[sl, :], sin_ref[sl, :])
                ka_ref[hd, sl, 0:D] = k.astype(BF16)
                ka_ref[hd, sl, D:] = jnp.where(lane == n, 1.0, 0.0).astype(BF16)
                kmean_ref[hd, pl.ds(n, 1), :] = jnp.mean(k, axis=0, keepdims=True)
                _store_vt(vt_ref.at[hd], v_ref[sl, cols], sl)
            return 0

        lax.fori_loop(0, S // t, prep_keys, 0)

    lane_f = lane.astype(F32)

    def prep(hd, c):
        cur = 2 * i + c
        qsl = pl.ds(pl.multiple_of(cur * t, t), t)
        q = _rope(_rms(q_ref[c * t:(c + 1) * t, hd * D:(hd + 1) * D].astype(F32), qg_ref[...]),
                  cos_ref[qsl, :], sin_ref[qsl, :])
        yield
        kmean = kmean_ref[hd]
        km_hi = kmean.astype(BF16)
        km_split = jnp.concatenate([km_hi, (kmean - km_hi.astype(F32)).astype(BF16)], axis=0)
        q_hi = q.astype(BF16)
        q_lo = (q - q_hi.astype(F32)).astype(BF16)
        part = _dot_nt(q_hi, km_split)
        gate = part[:, :LANES] + part[:, LANES:] + _dot_nt(q_lo, km_split[:LANES, :])
        gate = jnp.where(lane < cur, gate, -jnp.inf)
        yield
        sel_bias = jnp.full((t, LANES), NEG, F32)
        for _ in range(MOBA_TOPK):
            top = jnp.max(gate, axis=-1, keepdims=True)
            yield
            first = jnp.min(jnp.where(gate == top, lane_f, float(LANES)), axis=-1, keepdims=True)
            yield
            pick = lane_f == first
            sel_bias = jnp.where(pick & (first < cur.astype(F32)), 0.0, sel_bias)
            gate = jnp.where(pick, -jnp.inf, gate)
        qs = (q * (LOG2E * D ** -0.5)).astype(BF16)
        return (jnp.concatenate([qs, jnp.zeros((t, LANES), BF16)], axis=-1),
                jnp.concatenate([qs, sel_bias.astype(BF16)], axis=-1))

    _attend_tile_pair(i, [prep(hd, c) for hd in range(ATT_HEADS) for c in range(2)],
                      ka_ref, vt_ref, m_ref, acc_ref, o_ref)


def moba_attention(z, cos, sin_signed, q_gain, k_gain):
    B, S, W = z.shape
    H = W // (3 * HEAD_DIM)
    assert ATT_BLOCK == MOBA_BLOCK
    t = 2 * ATT_BLOCK
    w = ATT_HEADS * HEAD_DIM
    hp = H // ATT_HEADS
    return pl.pallas_call(
        functools.partial(_moba_kernel, S=S),
        grid=(B, hp, S // t),
        in_specs=[
            pl.BlockSpec((None, t, w), lambda b, h, i: (b, i, h)),
            pl.BlockSpec((None, S, w), lambda b, h, i: (b, 0, hp + h)),
            pl.BlockSpec((None, S, w), lambda b, h, i: (b, 0, 2 * hp + h)),
            pl.BlockSpec((S, HEAD_DIM), lambda b, h, i: (0, 0)),
            pl.BlockSpec((S, HEAD_DIM), lambda b, h, i: (0, 0)),
            pl.BlockSpec((1, HEAD_DIM), lambda b, h, i: (0, 0)),
            pl.BlockSpec((1, HEAD_DIM), lambda b, h, i: (0, 0)),
        ],
        out_specs=pl.BlockSpec((None, t, w), lambda b, h, i: (b, i, h)),
        out_shape=jax.ShapeDtypeStruct((B, S, H * HEAD_DIM), BF16),
        scratch_shapes=_ATT_SCRATCH(S) + [pltpu.VMEM((ATT_HEADS, LANES, HEAD_DIM), F32)],
        compiler_params=_params("parallel", "parallel", "arbitrary"),
        name="moba_attention",
    )(z, z, z, cos, sin_signed, q_gain.reshape(1, -1), k_gain.reshape(1, -1))


META_IDX, META_RANK, META_GATE = 0, 2, 4


def _router_kernel(x_ref, g_ref, w_ref, xn_ref, meta_ref, cnt_ref, carry_ref):
    i = pl.program_id(0)
    tm = x_ref.shape[0]

    @pl.when(i == 0)
    def _():
        carry_ref[...] = jnp.zeros(carry_ref.shape, F32)

    xn = _rms(x_ref[...], g_ref[...])
    xn_ref[...] = _pack_bf16_pairs(xn)
    w = w_ref[...]
    w_hi = w.astype(BF16)
    w_lo = (w - w_hi.astype(F32)).astype(BF16)
    x_hi = xn.astype(BF16)
    x_lo = (xn - x_hi.astype(F32)).astype(BF16)
    logits = (jnp.dot(x_hi, w_hi, preferred_element_type=F32) + jnp.dot(x_hi, w_lo, preferred_element_type=F32)
              + jnp.dot(x_lo, w_hi, preferred_element_type=F32))
    lane = lax.broadcasted_iota(jnp.int32, logits.shape, 1)
    logits = jnp.where(lane < N_EXPERTS, logits, -jnp.inf)
    top1 = jnp.max(logits, axis=-1, keepdims=True)
    lane_f = lane.astype(F32)
    idx1 = jnp.min(jnp.where(logits == top1, lane_f, float(LANES)), axis=-1, keepdims=True)
    rest = jnp.where(lane_f == idx1, -jnp.inf, logits)
    top2 = jnp.max(rest, axis=-1, keepdims=True)
    idx2 = jnp.min(jnp.where(rest == top2, lane_f, float(LANES)), axis=-1, keepdims=True)
    e2 = jnp.exp(top2 - top1)
    denom = 1.0 + e2
    chosen = jnp.where((lane_f == idx1) | (lane_f == idx2), 1.0, 0.0)
    r = lax.broadcasted_iota(jnp.int32, (tm, tm), 0)
    c = lax.broadcasted_iota(jnp.int32, (tm, tm), 1)
    ahead = jnp.dot(jnp.where(r > c, 1.0, 0.0).astype(BF16), chosen.astype(BF16), preferred_element_type=F32)
    carry = carry_ref[...]
    rank = ahead + carry
    rank1 = jnp.sum(jnp.where(lane_f == idx1, rank, 0.0), axis=-1, keepdims=True)
    rank2 = jnp.sum(jnp.where(lane_f == idx2, rank, 0.0), axis=-1, keepdims=True)
    vals = (idx1, idx2, rank1, rank2, 1.0 / denom, e2 / denom)
    meta = jnp.zeros(logits.shape, F32)
    for n, v in enumerate(vals):
        meta = jnp.where(lane == n, v, meta)
    meta_ref[...] = meta
    carry = carry + jnp.sum(chosen, axis=0, keepdims=True)
    carry_ref[...] = carry
    cnt_ref[...] = jnp.broadcast_to(carry, cnt_ref.shape)


def moe_router(x, gain, w_router, *, tm):
    M, K = x.shape
    w = jnp.zeros((K, LANES), F32).at[:, :N_EXPERTS].set(w_router)
    return pl.pallas_call(
        _router_kernel,
        grid=(M // tm,),
        in_specs=[
            pl.BlockSpec((tm, K), lambda i: (i, 0)),
            pl.BlockSpec((1, K), lambda i: (0, 0)),
            pl.BlockSpec((K, LANES), lambda i: (0, 0)),
        ],
        out_specs=[
            pl.BlockSpec((tm, K // 2), lambda i: (i, 0)),
            pl.BlockSpec((tm, LANES), lambda i: (i, 0)),
            pl.BlockSpec((8, LANES), lambda i: (0, 0)),
        ],
        out_shape=[
            jax.ShapeDtypeStruct((M, K // 2), jnp.int32),
            jax.ShapeDtypeStruct((M, LANES), F32),
            jax.ShapeDtypeStruct((8, LANES), F32),
        ],
        scratch_shapes=[pltpu.VMEM((1, LANES), F32)],
        compiler_params=_params("arbitrary"),
        name="moe_router",
    )(x, gain.reshape(1, K), w)


def _sc_workers():
    info = plsc.get_sparse_core_info()
    return info.num_cores, info.num_cores * info.num_subcores


def scatter_rows(rows, dest, n_out):
    M, W = rows.shape
    nc, nw = _sc_workers()
    per_w = M // nw
    mesh = plsc.VectorSubcoreMesh(core_axis_name="c", subcore_axis_name="s")

    @functools.partial(
        pl.kernel, mesh=mesh, out_type=jax.ShapeDtypeStruct((n_out, W), rows.dtype),
        scratch_types=[pltpu.VMEM((SC_ROWS,), jnp.int32), pltpu.VMEM((SC_ROWS, W), rows.dtype),
                       pltpu.SemaphoreType.DMA])
    def kern(rows_hbm, dest_hbm, out_hbm, idx_v, rows_v, sem):
        wid = lax.axis_index("s") * nc + lax.axis_index("c")

        @pl.loop(0, per_w // SC_ROWS)
        def _(g):
            base = wid * per_w + g * SC_ROWS
            pltpu.sync_copy(rows_hbm.at[pl.ds(base, SC_ROWS)], rows_v)
            for k in range(2):
                pltpu.sync_copy(dest_hbm.at[k, pl.ds(base, SC_ROWS)], idx_v)
                pltpu.async_copy(rows_v, out_hbm.at[idx_v], sem).wait()

    return kern(rows, dest)


def gather_rows(table, idx):
    N = idx.shape[0]
    W = table.shape[1]
    nc, nw = _sc_workers()
    per_w = N // nw
    mesh = plsc.VectorSubcoreMesh(core_axis_name="c", subcore_axis_name="s")

    @functools.partial(
        pl.kernel, mesh=mesh, out_type=jax.ShapeDtypeStruct((N, W), table.dtype),
        scratch_types=[pltpu.VMEM((SC_ROWS,), jnp.int32), pltpu.VMEM((SC_ROWS, W), table.dtype),
                       pltpu.SemaphoreType.DMA])
    def kern(table_hbm, idx_hbm, out_hbm, idx_v, rows_v, sem):
        wid = lax.axis_index("s") * nc + lax.axis_index("c")

        @pl.loop(0, per_w // SC_ROWS)
        def _(g):
            base = wid * per_w + g * SC_ROWS
            pltpu.sync_copy(idx_hbm.at[pl.ds(base, SC_ROWS)], idx_v)
            pltpu.async_copy(table_hbm.at[idx_v], rows_v, sem).wait()
            pltpu.sync_copy(rows_v, out_hbm.at[pl.ds(base, SC_ROWS)])

    return kern(table, idx)


def _combine_kernel(h_ref, y1_ref, y2_ref, meta_ref, o_ref):
    g1 = meta_ref[:, META_GATE:META_GATE + 1]
    g2 = meta_ref[:, META_GATE + 1:META_GATE + 2]
    o_ref[...] = (h_ref[...] + g1 * _unpack_bf16_pairs(y1_ref[...]).astype(F32)
                  + g2 * _unpack_bf16_pairs(y2_ref[...]).astype(F32))


def moe_combine(h, y_pairs, meta):
    M, K = h.shape
    p = PERM_TILE
    nt = M // p
    return pl.pallas_call(
        _combine_kernel,
        grid=(nt,),
        in_specs=[
            pl.BlockSpec((p, K), lambda i: (i, 0)),
            pl.BlockSpec((p, K // 2), lambda i: (i, 0)),
            pl.BlockSpec((p, K // 2), lambda i: (nt + i, 0)),
            pl.BlockSpec((p, LANES), lambda i: (i, 0)),
        ],
        out_specs=pl.BlockSpec((p, K), lambda i: (i, 0)),
        out_shape=jax.ShapeDtypeStruct((M, K), F32),
        compiler_params=_params("parallel"),
        name="moe_combine",
    )(h, y_pairs, y_pairs, meta)


def _expert_ffn_kernel(te_ref, tv_ref, x_ref, wg_ref, wu_ref, wd_ref, o_ref, xb_ref, acc_ref):
    del te_ref
    i = pl.program_id(0)
    j = pl.program_id(1)
    tm = x_ref.shape[0]
    valid = tv_ref[i]

    def swiglu_rows(rows):
        @pl.when(j == 0)
        def _():
            xb_ref[0:rows, :] = _unpack_bf16_pairs(x_ref[0:rows, :])

        x = xb_ref[0:rows, :]
        gt = jnp.dot(x, wg_ref[...], preferred_element_type=F32)
        up = jnp.dot(x, wu_ref[...], preferred_element_type=F32)
        act = (gt / (1.0 + jnp.exp(-gt)) * up).astype(BF16)
        part = jnp.dot(act, wd_ref[...], preferred_element_type=F32)

        @pl.when(j == 0)
        def _():
            acc_ref[0:rows, :] = part

        @pl.when(j > 0)
        def _():
            acc_ref[0:rows, :] += part

        @pl.when(j == pl.num_programs(1) - 1)
        def _():
            o_ref[0:rows, :] = _pack_bf16_pairs(acc_ref[0:rows, :])
            if rows < tm:
                o_ref[rows:tm, :] = jnp.zeros((tm - rows, o_ref.shape[1]), o_ref.dtype)

    @pl.when(valid > tm // 2)
    def _():
        swiglu_rows(tm)

    @pl.when((valid > 0) & (valid <= tm // 2))
    def _():
        swiglu_rows(tm // 2)

    @pl.when((valid == 0) & (j == 0))
    def _():
        o_ref[...] = jnp.zeros(o_ref.shape, o_ref.dtype)


def expert_ffn(xs, tile_expert, tile_valid, w_gate_up, w_down, *, tm, tf):
    R = xs.shape[0]
    E, F, K = w_down.shape
    nf = F // tf
    live = lambda i, tv: tv[i] > 0
    col = lambda i, j, tv: jnp.where(live(i, tv), j, nf - 1)
    grid_spec = pltpu.PrefetchScalarGridSpec(
        num_scalar_prefetch=2,
        grid=(R // tm, nf),
        in_specs=[
            pl.BlockSpec((tm, K // 2), lambda i, j, te, tv: (te[R // tm + i], 0)),
            pl.BlockSpec((None, K, tf), lambda i, j, te, tv: (te[i], 0, col(i, j, tv))),
            pl.BlockSpec((None, K, tf), lambda i, j, te, tv: (te[i], 0, nf + col(i, j, tv))),
            pl.BlockSpec((None, tf, K), lambda i, j, te, tv: (te[i], col(i, j, tv), 0)),
        ],
        out_specs=pl.BlockSpec((tm, K // 2), lambda i, j, te, tv: (i, 0)),
        scratch_shapes=[pltpu.VMEM((tm, K), BF16), pltpu.VMEM((tm, K), F32)],
    )
    return pl.pallas_call(
        _expert_ffn_kernel,
        grid_spec=grid_spec,
        out_shape=jax.ShapeDtypeStruct((R, K // 2), jnp.int32),
        compiler_params=_params("arbitrary", "arbitrary"),
        name="moe_expert_ffn",
    )(tile_expert, tile_valid, xs, w_gate_up, w_gate_up, w_down)


def moe_residual(h, gain, w_router, w_gate_up, w_down):
    M, K = h.shape
    p = PERM_TILE
    tm = EXPERT_TILE
    n_rows = 2 * M + N_EXPERTS * tm
    xn, meta, cnt = moe_router(h, gain, w_router, tm=p)

    counts = cnt[0, :N_EXPERTS].astype(jnp.int32)
    padded = (counts + tm - 1) // tm * tm
    ends = jnp.cumsum(padded)
    offsets = ends - padded
    n_tiles = n_rows // tm
    tile_row = jnp.arange(n_tiles) * tm
    n_used = ends[-1] // tm
    last_used = jnp.minimum(tile_row // tm, n_used - 1)
    expert_of = lambda row: jnp.minimum(jnp.sum(ends[None, :] <= row[:, None], axis=1), N_EXPERTS - 1)
    tile_expert = expert_of(last_used * tm)
    row_in_expert = tile_row - jnp.sum(jnp.where(tile_expert[:, None] == jnp.arange(N_EXPERTS), offsets, 0), axis=1)
    own_count = jnp.sum(jnp.where(tile_expert[:, None] == jnp.arange(N_EXPERTS), counts, 0), axis=1)
    tile_valid = jnp.where(tile_row // tm < n_used, jnp.clip(own_count - row_in_expert, 0, tm), 0)
    tile_tables = jnp.concatenate([tile_expert, last_used]).astype(jnp.int32)

    idx = meta[:, META_IDX:META_IDX + 2].astype(jnp.int32)
    rank = meta[:, META_RANK:META_RANK + 2].astype(jnp.int32)
    dest = (jnp.sum(jnp.where(idx[:, :, None] == jnp.arange(N_EXPERTS), offsets, 0), axis=-1) + rank).T

    xs = scatter_rows(xn, dest, n_rows)
    ys = expert_ffn(xs, tile_tables, tile_valid.astype(jnp.int32), w_gate_up, w_down, tm=tm, tf=896)
    return moe_combine(h, gather_rows(ys, dest.reshape(-1)), meta)


def _ffn_kernel(x_ref, g_ref, wg_ref, wu_ref, wd_ref, o_ref, xn_ref, acc_ref):
    j = pl.program_id(1)

    @pl.when(j == 0)
    def _():
        x = x_ref[...]
        xn_ref[...] = _rms(x, g_ref[...]).astype(BF16)
        acc_ref[...] = x

    xn = xn_ref[...]
    gt = jnp.dot(xn, wg_ref[...], preferred_element_type=F32)
    up = jnp.dot(xn, wu_ref[...], preferred_element_type=F32)
    act = gt / (1.0 + jnp.exp(-gt)) * up
    acc_ref[...] += jnp.dot(act.astype(BF16), wd_ref[...], preferred_element_type=F32)

    @pl.when(j == pl.num_programs(1) - 1)
    def _():
        o_ref[...] = acc_ref[...]


def ffn_residual(x, gain, w_gate_up, w_down, *, tm, tf):
    M, K = x.shape
    F = w_down.shape[0]
    nf = F // tf
    return pl.pallas_call(
        _ffn_kernel,
        grid=(M // tm, nf),
        in_specs=[
            pl.BlockSpec((tm, K), lambda i, j: (i, 0)),
            pl.BlockSpec((1, K), lambda i, j: (0, 0)),
            pl.BlockSpec((K, tf), lambda i, j: (0, j)),
            pl.BlockSpec((K, tf), lambda i, j: (0, nf + j)),
            pl.BlockSpec((tf, K), lambda i, j: (j, 0)),
        ],
        out_specs=pl.BlockSpec((tm, K), lambda i, j: (i, 0)),
        out_shape=jax.ShapeDtypeStruct((M, K), F32),
        scratch_shapes=[pltpu.VMEM((tm, K), BF16), pltpu.VMEM((tm, K), F32)],
        compiler_params=_params("parallel", "arbitrary"),
        name="ffn_residual",
    )(x, gain.reshape(1, K), w_gate_up, w_gate_up, w_down)


def _even_mix(h, B, S, norm1, w_in, f_bias, q_norm, k_norm, conv_w, a_log, dt_bias, o_norm, w_out):
    M, D = h.shape
    fw, gw = FOX_HEADS * HEAD_DIM, GDN_HEADS * HEAD_DIM
    o_ff = 3 * fw
    o_gq = o_ff + FOX_HEADS
    o_ga = o_gq + 3 * gw
    o_gb = o_ga + GDN_HEADS
    o_gg = o_gb + GDN_HEADS
    w_big = jnp.concatenate([w_in[:, :o_ff], w_in[:, o_gq:o_ga], w_in[:, o_gg:]], axis=1).astype(BF16)
    w_small = jnp.concatenate([w_in[:, o_ff:o_gq], w_in[:, o_ga:o_gg],
                               jnp.zeros((D, LANES - FOX_HEADS - 2 * GDN_HEADS), F32)], axis=1).astype(BF16)
    z, zs = norm_matmul(h, norm1, w_big, w_small, tm=ROW_TILE, tn=1792)
    par = jnp.zeros((8, LANES), F32)
    par = par.at[0, LANE_F:LANE_F + FOX_HEADS].set(f_bias).at[0, LANE_A:LANE_A + GDN_HEADS].set(dt_bias)
    par = par.at[1, LANE_A:LANE_A + GDN_HEADS].set(a_log)
    col, row = even_gates(zs.reshape(B, S, LANES), par)
    z = z.reshape(B, S, -1)
    fox = fox_attention(z, col, q_norm, k_norm)
    gdn = gdn_mixer(z, conv_w, col, row, o_norm)
    w_out = w_out.astype(BF16)
    return matmul_residual(h, [(fox.reshape(M, fw), w_out[:fw]), (gdn.reshape(M, gw), w_out[fw:])], tm=ROW_TILE, tn=1024)


def _odd_mix(h, B, S, norm1, w_qkv, q_norm, k_norm, w_out):
    M, D = h.shape
    z = norm_matmul(h, norm1, w_qkv.astype(BF16), tm=ROW_TILE, tn=1536).reshape(B, S, -1)
    half = HEAD_DIM // 2
    inv = jnp.power(ROPE_THETA, -jnp.arange(half, dtype=F32) / half)
    ang = jnp.arange(S, dtype=F32)[:, None] * inv[None, :]
    cos, sin = jnp.cos(ang), jnp.sin(ang)
    cos_full = jnp.concatenate([cos, cos], axis=-1)
    sin_signed = jnp.concatenate([-sin, sin], axis=-1)
    att = moba_attention(z, cos_full, sin_signed, q_norm, k_norm)
    return matmul_residual(h, [(att.reshape(M, -1), w_out.astype(BF16))], tm=ROW_TILE, tn=1024)


def _odd_moe(h, norm2, w_router, w_gate_up, w_down):
    return moe_residual(h, norm2, w_router, w_gate_up.astype(BF16), w_down.astype(BF16))


def kernel(x, e_norm1, e_w_in, e_fox_f_bias, e_fox_q_norm, e_fox_k_norm, e_gdn_conv, e_gdn_a_log,
           e_gdn_dt_bias, e_gdn_o_norm, e_w_out, e_norm2, e_ffn_w_gate_up, e_ffn_w_down,
           o_norm1, o_w_qkv, o_q_norm, o_k_norm, o_w_out, o_norm2, o_router, o_exp_w_gate_up, o_exp_w_down):
    B, S, D = x.shape
    h = x.reshape(B * S, D)
    depth = e_norm1.shape[0] + o_norm1.shape[0]
    for layer in range(depth):
        i = layer // 2
        if layer % 2 == 0:
            h = _even_mix(h, B, S, e_norm1[i], e_w_in[i], e_fox_f_bias[i], e_fox_q_norm[i], e_fox_k_norm[i],
                          e_gdn_conv[i], e_gdn_a_log[i], e_gdn_dt_bias[i], e_gdn_o_norm[i], e_w_out[i])
            h = ffn_residual(h, e_norm2[i], e_ffn_w_gate_up[i].astype(BF16), e_ffn_w_down[i].astype(BF16),
                             tm=ROW_TILE // 2, tf=1408)
        else:
            h = _odd_mix(h, B, S, o_norm1[i], o_w_qkv[i], o_q_norm[i], o_k_norm[i], o_w_out[i])
            h = _odd_moe(h, o_norm2[i], o_router[i], o_exp_w_gate_up[i], o_exp_w_down[i])
    return h.reshape(B, S, D)
```

```python
import functools

import jax
import jax.numpy as jnp
from jax import lax
from jax.experimental import pallas as pl
from jax.experimental.pallas import tpu as pltpu
from jax.experimental.pallas import tpu_sc as plsc

F32 = jnp.float32
BF16 = jnp.bfloat16

HEAD_DIM = 128
FOX_HEADS = 4
GDN_HEADS = 4
CONV_WIDTH = 4
MOBA_BLOCK = 256
MOBA_TOPK = 3
N_EXPERTS = 8
ROPE_THETA = 10000.0
EPS = 1e-6

LANES = 128
GDN_CHUNK = 128
INV_BLOCK = 16
GDN_HEADS_PER_STEP = 4
GDN_CHUNKS_PER_STEP = 2
NEG = -(2.0 ** 100)
LOG2E = 1.4426950408889634
ATT_BLOCK = 256
ATT_HEADS = 2
V_ROWS = HEAD_DIM + 16
PERM_TILE = 512
SC_ROWS = 64
ROW_TILE = 1024
EXPERT_TILE = 1024
VMEM_LIMIT_BYTES = 56 * 1024 * 1024

FOX_Q0, FOX_K0, FOX_V0 = 0, 4, 8
GDN_Q0, GDN_K0, GDN_V0, GDN_G0 = 12, 16, 20, 24
LANE_F, LANE_A, LANE_B = 0, 4, 8


def _params(*sem):
    return pltpu.CompilerParams(dimension_semantics=sem, vmem_limit_bytes=VMEM_LIMIT_BYTES)


def _rms(x, gain):
    return x * lax.rsqrt(jnp.mean(x * x, axis=-1, keepdims=True) + EPS) * gain


def _dot_nt(a, b, **kw):
    return lax.dot_general(a, b, (((1,), (1,)), ((), ())), preferred_element_type=F32, **kw)


def _pick_lane(x, lane_idx):
    lane = lax.broadcasted_iota(jnp.int32, x.shape, 1)
    return jnp.sum(jnp.where(lane == lane_idx, x, 0.0), axis=-1, keepdims=True)


def _pack_bf16_pairs(x):
    n = x.shape[1] // 2
    hi = pltpu.bitcast(x[:, :n].astype(BF16).astype(F32), jnp.uint32)
    lo = pltpu.bitcast(x[:, n:].astype(BF16).astype(F32), jnp.uint32)
    return pltpu.bitcast(hi | (lo >> 16), jnp.int32)


def _unpack_bf16_pairs(w):
    u = pltpu.bitcast(w, jnp.uint32)
    hi = pltpu.bitcast(u & jnp.uint32(0xFFFF0000), F32).astype(BF16)
    lo = pltpu.bitcast(u << 16, F32).astype(BF16)
    return jnp.concatenate([hi, lo], axis=1)


def _norm_mm_kernel(x_ref, g_ref, w_ref, *rest, has_aux):
    if has_aux:
        waux_ref, o_ref, oaux_ref, xn_ref = rest
    else:
        o_ref, xn_ref = rest

    @pl.when(pl.program_id(1) == 0)
    def _():
        xn = _rms(x_ref[...], g_ref[...]).astype(BF16)
        xn_ref[...] = xn
        if has_aux:
            oaux_ref[...] = jnp.dot(xn, waux_ref[...], preferred_element_type=F32)

    o_ref[...] = jnp.dot(xn_ref[...], w_ref[...], preferred_element_type=F32).astype(o_ref.dtype)


def norm_matmul(x, gain, w, w_aux=None, *, tm, tn, out_dtype=BF16):
    M, K = x.shape
    N = w.shape[1]
    has_aux = w_aux is not None
    in_specs = [
        pl.BlockSpec((tm, K), lambda i, j: (i, 0)),
        pl.BlockSpec((1, K), lambda i, j: (0, 0)),
        pl.BlockSpec((K, tn), lambda i, j: (0, j)),
    ]
    out_shape = [jax.ShapeDtypeStruct((M, N), out_dtype)]
    out_specs = [pl.BlockSpec((tm, tn), lambda i, j: (i, j))]
    args = [x, gain.reshape(1, K), w]
    if has_aux:
        in_specs.append(pl.BlockSpec((K, LANES), lambda i, j: (0, 0)))
        out_shape.append(jax.ShapeDtypeStruct((M, LANES), F32))
        out_specs.append(pl.BlockSpec((tm, LANES), lambda i, j: (i, 0)))
        args.append(w_aux)
    res = pl.pallas_call(
        functools.partial(_norm_mm_kernel, has_aux=has_aux),
        grid=(M // tm, N // tn),
        in_specs=in_specs,
        out_specs=out_specs,
        out_shape=out_shape,
        scratch_shapes=[pltpu.VMEM((tm, K), BF16)],
        compiler_params=_params("parallel", "arbitrary"),
        name="norm_matmul",
    )(*args)
    return res if has_aux else res[0]


def _plus_projections(x, proj_refs):
    for a_ref, w_ref in zip(proj_refs[0::2], proj_refs[1::2]):
        x = x + jnp.dot(a_ref[...], w_ref[...], preferred_element_type=F32)
    return x


def _projection_specs(pairs, tm):
    specs, args = [], []
    for a, w in pairs:
        specs.append(pl.BlockSpec((tm, a.shape[1]), lambda i, *_: (i, 0)))
        specs.append(pl.BlockSpec(w.shape, lambda i, *_: (0, 0)))
        args += [a, w]
    return specs, args


def _gate_kernel(zs_ref, par_ref, col_ref, row_ref, *, S, B):
    C = GDN_CHUNK
    bias = par_ref[0:1, :]
    neg_a = -jnp.exp(par_ref[1:2, :])
    r = lax.broadcasted_iota(jnp.int32, (C, C), 0)
    c = lax.broadcasted_iota(jnp.int32, (C, C), 1)
    tril = (r >= c).astype(F32)
    lane = lax.broadcasted_iota(jnp.int32, (C, LANES), 1)

    def body(n, carry):
        sl = pl.ds(pl.multiple_of(n * C, C), C)
        us, betas = [], []
        for bi in range(B):
            z = zs_ref[bi, sl, :]
            t = z + bias
            soft = jnp.log(1.0 + jnp.exp(-jnp.abs(t)))
            log_f = jnp.minimum(t, 0.0) - soft
            g = neg_a * (jnp.maximum(t, 0.0) + soft)
            betas.append(1.0 / (1.0 + jnp.exp(-z)))
            us.append(jnp.where(lane < LANE_A, log_f, jnp.where(lane < LANE_B, g, 0.0)))
        sums = jnp.dot(tril, jnp.concatenate(us, axis=1), preferred_element_type=F32, precision=lax.Precision.HIGHEST)
        last = []
        for bi in range(B):
            cs = sums[:, bi * LANES:(bi + 1) * LANES] + jnp.where(lane < LANE_A, carry[bi], 0.0)
            out = jnp.where(lane < LANE_B, cs, betas[bi])
            col_ref[bi, sl, :] = out
            out_t = out.T
            for hd in range(GDN_HEADS):
                row_ref[bi, hd, :, sl] = out_t[LANE_A + hd:LANE_A + hd + 1, :]
            last.append(cs[C - 1:C, :])
        return tuple(last)

    lax.fori_loop(0, S // C, body, tuple(jnp.zeros((1, LANES), F32) for _ in range(B)))


def even_gates(zs, par):
    B, S, _ = zs.shape
    return pl.pallas_call(
        functools.partial(_gate_kernel, S=S, B=B),
        grid=(1,),
        in_specs=[
            pl.BlockSpec((B, S, LANES), lambda i: (0, 0, 0)),
            pl.BlockSpec((8, LANES), lambda i: (0, 0)),
        ],
        out_specs=[
            pl.BlockSpec((B, S, LANES), lambda i: (0, 0, 0)),
            pl.BlockSpec((B, GDN_HEADS, 1, S), lambda i: (0, 0, 0, 0)),
        ],
        out_shape=[jax.ShapeDtypeStruct((B, S, LANES), F32), jax.ShapeDtypeStruct((B, GDN_HEADS, 1, S), F32)],
        compiler_params=_params("arbitrary"),
        name="even_gates",
    )(zs, par)


def _split3(x):
    hi = x.astype(BF16).astype(F32)
    mid = (x - hi).astype(BF16).astype(F32)
    lo = (x - hi - mid).astype(BF16).astype(F32)
    return hi, mid, lo


def _interleave(gens):
    out = [None] * len(gens)
    live = list(range(len(gens)))
    while live:
        for n in list(live):
            try:
                next(gens[n])
            except StopIteration as stop:
                out[n] = stop.value
                live.remove(n)
    return out


def _attend(state, c, qa, ka_ref, vt_ref, key0, nkeys, keep=None):
    ks = pl.ds(pl.multiple_of(key0, ATT_BLOCK), nkeys)
    st = _dot_nt(ka_ref[ks, :], qa)
    yield
    if keep is not None:
        st = jnp.where(keep, st, NEG)
    m_new = jnp.max(st, axis=0, keepdims=True)
    if state[c] is not None:
        m_old, acc_old = state[c]
        m_new = jnp.maximum(m_old, m_new)
    p = jnp.exp2(st - m_new).astype(BF16)
    pv = jnp.dot(vt_ref[:, ks], p, preferred_element_type=F32)
    state[c] = (m_new, pv if state[c] is None else acc_old * jnp.exp2(m_old - m_new) + pv)


def _attend_tile_pair(i, prep, ka_ref, vt_ref, m_ref, acc_ref, o_ref):
    t = ATT_BLOCK
    chains = [(hd, c) for c in range(2) for hd in range(ATT_HEADS)]
    num = lambda hd, c: 2 * hd + c
    qa = _interleave(prep)

    def step(state, hd, c, which, **kw):
        return _attend(state, num(hd, c), qa[num(hd, c)][which], ka_ref.at[hd], vt_ref.at[hd], **kw)

    def save(state):
        m_ref[...] = jnp.stack([state[n][0] for n in range(len(chains))])
        acc_ref[...] = jnp.stack([state[n][1] for n in range(len(chains))])

    def past_blocks(key0, n_steps):
        state = {n: (m_ref[n], acc_ref[n]) for n in range(len(chains))}
        _interleave([step(state, hd, c, 1, key0=key0 + s * 2 * t, nkeys=2 * t)
                     for s in range(n_steps) for hd, c in chains])
        save(state)

    state = {n: None for n in range(len(chains))}
    keep = _causal_keep()
    _interleave([step(state, hd, c, 0, key0=(2 * i + c) * t, nkeys=t, keep=keep) for hd, c in chains]
                + [step(state, hd, 1, 1, key0=(2 * i) * t, nkeys=t) for hd in range(ATT_HEADS)])
    save(state)

    def four_blocks(g, _):
        past_blocks(g * (4 * t), 2)
        return 0

    lax.fori_loop(0, i // 2, four_blocks, 0)

    @pl.when((i & 1) != 0)
    def _():
        past_blocks((i // 2) * (4 * t), 1)

    for hd, c in chains:
        acc = acc_ref[num(hd, c)]
        out_t = acc[:HEAD_DIM, :] * (1.0 / acc[HEAD_DIM:HEAD_DIM + 1, :])
        o_ref[c * t:(c + 1) * t, hd * HEAD_DIM:(hd + 1) * HEAD_DIM] = out_t.T.astype(o_ref.dtype)


def _store_vt(vt_ref, v, sl):
    vt_ref[0:HEAD_DIM, sl] = v.astype(F32).T.astype(BF16)
    vt_ref[HEAD_DIM:V_ROWS, sl] = jnp.ones((V_ROWS - HEAD_DIM, ATT_BLOCK), BF16)


def _causal_keep():
    key = lax.broadcasted_iota(jnp.int32, (ATT_BLOCK, ATT_BLOCK), 0)
    qry = lax.broadcasted_iota(jnp.int32, (ATT_BLOCK, ATT_BLOCK), 1)
    return key <= qry


_ATT_SCRATCH = lambda S: [pltpu.VMEM((ATT_HEADS, S, 2 * HEAD_DIM), BF16), pltpu.VMEM((ATT_HEADS, V_ROWS, S), BF16),
                          pltpu.VMEM((2 * ATT_HEADS, 1, ATT_BLOCK), F32),
                          pltpu.VMEM((2 * ATT_HEADS, V_ROWS, ATT_BLOCK), F32)]


def _fox_kernel(q_ref, k_ref, v_ref, col_ref, qg_ref, kg_ref, o_ref, ka_ref, vt_ref, m_ref, acc_ref, *, S):
    h0 = pl.program_id(1) * ATT_HEADS
    i = pl.program_id(2)
    t = ATT_BLOCK
    D = HEAD_DIM
    lane = lax.broadcasted_iota(jnp.int32, (t, LANES), 1)

    @pl.when(i == 0)
    def _():
        def prep_keys(n, _):
            sl = pl.ds(pl.multiple_of(n * t, t), t)
            for hd in range(ATT_HEADS):
                cols = slice(hd * D, (hd + 1) * D)
                ka_ref[hd, sl, 0:D] = _rms(k_ref[sl, cols].astype(F32), kg_ref[...]).astype(BF16)
                hi, mid, lo = _split3(-LOG2E * _pick_lane(col_ref[sl, :], LANE_F + h0 + hd))
                ka_ref[hd, sl, D:] = jnp.where(
                    lane < 3, 1.0, jnp.where(lane == 3, hi, jnp.where(lane == 4, mid, jnp.where(lane == 5, lo, 0.0)))
                ).astype(BF16)
                _store_vt(vt_ref.at[hd], v_ref[sl, cols], sl)
            return 0

        lax.fori_loop(0, S // t, prep_keys, 0)

    def prep(hd, c):
        q = q_ref[c * t:(c + 1) * t, hd * D:(hd + 1) * D].astype(F32)
        qn = (_rms(q, qg_ref[...]) * (LOG2E * D ** -0.5)).astype(BF16)
        yield
        qsl = pl.ds(pl.multiple_of((2 * i + c) * t, t), t)
        hi, mid, lo = _split3(LOG2E * _pick_lane(col_ref[qsl, :], LANE_F + h0 + hd))
        aug = jnp.where(lane == 0, hi,
                        jnp.where(lane == 1, mid, jnp.where(lane == 2, lo, jnp.where(lane < 6, 1.0, 0.0))))
        qa = jnp.concatenate([qn, aug.astype(BF16)], axis=-1)
        return qa, qa

    _attend_tile_pair(i, [prep(hd, c) for hd in range(ATT_HEADS) for c in range(2)],
                      ka_ref, vt_ref, m_ref, acc_ref, o_ref)


def fox_attention(z, col, q_gain, k_gain):
    B, S, _ = z.shape
    t = 2 * ATT_BLOCK
    w = ATT_HEADS * HEAD_DIM
    return pl.pallas_call(
        functools.partial(_fox_kernel, S=S),
        grid=(B, FOX_HEADS // ATT_HEADS, S // t),
        in_specs=[
            pl.BlockSpec((None, t, w), lambda b, h, i: (b, i, FOX_Q0 // ATT_HEADS + h)),
            pl.BlockSpec((None, S, w), lambda b, h, i: (b, 0, FOX_K0 // ATT_HEADS + h)),
            pl.BlockSpec((None, S, w), lambda b, h, i: (b, 0, FOX_V0 // ATT_HEADS + h)),
            pl.BlockSpec((None, S, LANES), lambda b, h, i: (b, 0, 0)),
            pl.BlockSpec((1, HEAD_DIM), lambda b, h, i: (0, 0)),
            pl.BlockSpec((1, HEAD_DIM), lambda b, h, i: (0, 0)),
        ],
        out_specs=pl.BlockSpec((None, t, w), lambda b, h, i: (b, i, h)),
        out_shape=jax.ShapeDtypeStruct((B, S, FOX_HEADS * HEAD_DIM), BF16),
        scratch_shapes=_ATT_SCRATCH(S),
        compiler_params=_params("parallel", "parallel", "arbitrary"),
        name="fox_attention",
    )(z, z, z, col, q_gain.reshape(1, -1), k_gain.reshape(1, -1))


def _unit_lower_inverse(m):
    n = m.shape[0]
    r = lax.broadcasted_iota(jnp.int32, (n, n), 0)
    c = lax.broadcasted_iota(jnp.int32, (n, n), 1)
    eye = (r == c).astype(F32)

    def same_block(b):
        return (r // b) == (c // b)

    p = jnp.where(same_block(INV_BLOCK), m, 0.0)
    inv = eye - p
    k = 2
    while k < INV_BLOCK:
        pb = p.astype(BF16)
        p = jnp.dot(pb, pb, preferred_element_type=F32)
        yield
        inv = jnp.dot(inv.astype(BF16), (eye + p).astype(BF16), preferred_element_type=F32)
        yield
        k *= 2
    b = INV_BLOCK
    while b < n:
        off = jnp.where(same_block(2 * b), jnp.where(same_block(b), 0.0, m), 0.0).astype(BF16)
        ib = inv.astype(BF16)
        left = jnp.dot(ib, off, preferred_element_type=F32).astype(BF16)
        yield
        inv = inv - jnp.dot(left, ib, preferred_element_type=F32)
        yield
        b *= 2
    return inv


def _gdn_kernel(q_ref, k_ref, v_ref, gg_ref, wq_ref, wk_ref, wv_ref, col_ref, row_ref, on_ref,
                o_ref, pad_ref, qs_ref, ks_ref, vs_ref, *, S, rows, hb):
    h0 = pl.program_id(1) * hb
    C = GDN_CHUNK
    D = HEAD_DIM

    def conv_silu(x_ref, w_ref, dst_ref, hh, mode):
        cols = slice(hh * D, (hh + 1) * D)
        pad_ref[0:8, :] = jnp.zeros((8, D), F32)

        def fill(n, _):
            src = pl.ds(pl.multiple_of(n * rows, rows), rows)
            pad_ref[pl.ds(pl.multiple_of(8 + n * rows, 8), rows), :] = x_ref[src, cols].astype(F32)
            return 0

        lax.fori_loop(0, S // rows, fill, 0)
        w = w_ref[:, cols]

        def conv(n, _):
            base = pl.multiple_of(n * rows, rows)
            win = pad_ref[pl.ds(base, rows + 8), :]
            y = jnp.zeros((rows, D), F32)
            for tap in range(CONV_WIDTH):
                lead = 8 - (CONV_WIDTH - 1) + tap
                y = y + w[tap:tap + 1, :] * pltpu.roll(win, rows + 8 - lead, 0)[0:rows, :]
            y = y / (1.0 + jnp.exp(-y))
            if mode != "v":
                y = y * lax.rsqrt(jnp.sum(y * y, axis=-1, keepdims=True) + EPS)
            if mode == "q":
                y = y * D ** -0.5
            dst_ref[hh, pl.ds(base, rows), :] = y.astype(dst_ref.dtype)
            return 0

        lax.fori_loop(0, S // rows, conv, 0)

    for hh in range(hb):
        conv_silu(q_ref, wq_ref, qs_ref, hh, "q")
        conv_silu(k_ref, wk_ref, ks_ref, hh, "k")
        conv_silu(v_ref, wv_ref, vs_ref, hh, "v")

    r = lax.broadcasted_iota(jnp.int32, (C, C), 0)
    c = lax.broadcasted_iota(jnp.int32, (C, C), 1)
    incl = r >= c
    strict = r > c

    def chunk_local(hh, sl, tab):
        q = qs_ref[hh, sl, :].astype(F32)
        k = ks_ref[hh, sl, :].astype(F32)
        v = vs_ref[hh, sl, :].astype(F32)
        gcol = _pick_lane(tab, LANE_A + h0 + hh)
        beta = _pick_lane(tab, LANE_B + h0 + hh)
        grow = row_ref[hh, :, sl]
        glast = gcol[C - 1:C, :]
        decay = jnp.where(incl, jnp.exp(jnp.where(incl, gcol - grow, 0.0)), 0.0)
        eg = jnp.exp(gcol)
        kb = k * beta
        kbf = k.astype(BF16)
        m = jnp.where(strict, _dot_nt(kb.astype(BF16), kbf) * decay, 0.0)
        attn = (_dot_nt(q.astype(BF16), kbf) * decay).astype(BF16)
        yield
        tinv = (yield from _unit_lower_inverse(m)).astype(BF16)
        rhs = jnp.concatenate([v * beta, kb * eg], axis=-1).astype(BF16)
        sol = jnp.dot(tinv, rhs, preferred_element_type=F32)
        yield
        gate = gg_ref[sl, hh * D:(hh + 1) * D].astype(F32)
        return dict(u=sol[:, :D], w=sol[:, D:].astype(BF16), attn=attn, qg=(q * eg).astype(BF16),
                    kg_t=(k * jnp.exp(glast - gcol)).T.astype(BF16), keep=jnp.exp(glast),
                    gate=gate / (1.0 + jnp.exp(-gate)))

    def chunk_state(parts, state):
        outs = []
        for c in parts:
            sb = state.astype(BF16)
            v_new = c["u"] - jnp.dot(c["w"], sb, preferred_element_type=F32)
            o_state = jnp.dot(c["qg"], sb, preferred_element_type=F32)
            yield
            vb = v_new.astype(BF16)
            o = o_state + jnp.dot(c["attn"], vb, preferred_element_type=F32)
            state = state * c["keep"] + jnp.dot(c["kg_t"], vb, preferred_element_type=F32)
            yield
            outs.append((_rms(o, on_ref[...]) * c["gate"]).astype(o_ref.dtype))
        return state, outs

    def chunk_group(n, states):
        sls = [pl.ds(pl.multiple_of((n * GDN_CHUNKS_PER_STEP + g) * C, C), C) for g in range(GDN_CHUNKS_PER_STEP)]
        tabs = [col_ref[sl, :] for sl in sls]
        parts = _interleave([chunk_local(hh, sls[g], tabs[g]) for g in range(GDN_CHUNKS_PER_STEP) for hh in range(hb)])
        res = _interleave([chunk_state([parts[g * hb + hh] for g in range(GDN_CHUNKS_PER_STEP)], states[hh])
                           for hh in range(hb)])
        for g in range(GDN_CHUNKS_PER_STEP):
            o_ref[sls[g], :] = jnp.concatenate([outs[g] for _, outs in res], axis=-1)
        return tuple(s for s, _ in res)

    lax.fori_loop(0, S // (C * GDN_CHUNKS_PER_STEP), chunk_group, tuple(jnp.zeros((D, D), F32) for _ in range(hb)))


def gdn_mixer(z, conv_w, col, row, o_gain):
    B, S, _ = z.shape
    D = HEAD_DIM
    hb = GDN_HEADS_PER_STEP
    rows = 256
    seq = lambda off: pl.BlockSpec((None, S, hb * D), lambda b, h: (b, 0, off // hb + h),
                                   pipeline_mode=pl.Buffered(1))
    cw = lambda off: pl.BlockSpec((CONV_WIDTH, hb * D), lambda b, h: (0, off // hb + h))
    return pl.pallas_call(
        functools.partial(_gdn_kernel, S=S, rows=rows, hb=hb),
        grid=(B, GDN_HEADS // hb),
        in_specs=[
            seq(GDN_Q0), seq(GDN_K0), seq(GDN_V0), seq(GDN_G0),
            cw(0), cw(GDN_HEADS), cw(2 * GDN_HEADS),
            pl.BlockSpec((None, S, LANES), lambda b, h: (b, 0, 0)),
            pl.BlockSpec((None, hb, 1, S), lambda b, h: (b, h, 0, 0)),
            pl.BlockSpec((1, D), lambda b, h: (0, 0)),
        ],
        out_specs=pl.BlockSpec((None, S, hb * D), lambda b, h: (b, 0, h)),
        out_shape=jax.ShapeDtypeStruct((B, S, GDN_HEADS * D), BF16),
        scratch_shapes=[pltpu.VMEM((S + 8, D), F32)] + [pltpu.VMEM((hb, S, D), BF16)] * 3,
        compiler_params=_params("parallel", "parallel"),
        name="gdn_mixer",
    )(z, z, z, z, conv_w, conv_w, conv_w, col, row, o_gain.reshape(1, D))


def _rope(x, cos, sin_signed):
    return x * cos + pltpu.roll(x, HEAD_DIM // 2, 1) * sin_signed


def _moba_kernel(q_ref, k_ref, v_ref, cos_ref, sin_ref, qg_ref, kg_ref, o_ref,
                 ka_ref, vt_ref, m_ref, acc_ref, kmean_ref, *, S):
    i = pl.program_id(2)
    t = ATT_BLOCK
    D = HEAD_DIM
    lane = lax.broadcasted_iota(jnp.int32, (t, LANES), 1)

    @pl.when(i == 0)
    def _():
        kmean_ref[...] = jnp.zeros(kmean_ref.shape, F32)

        def prep_keys(n, _):
            sl = pl.ds(pl.multiple_of(n * t, t), t)
            for hd in range(ATT_HEADS):
                cols = slice(hd * D, (hd + 1) * D)
                k = _rope(_rms(k_ref[sl, cols].astype(F32), kg_ref[...]), cos_ref[sl, :], sin_ref[sl, :])
                ka_ref[hd, sl, 0:D] = k.astype(BF16)
                ka_ref[hd, sl, D:] = jnp.where(lane == n, 1.0, 0.0).astype(BF16)
                kmean_ref[hd, pl.ds(n, 1), :] = jnp.mean(k, axis=0, keepdims=True)
                _store_vt(vt_ref.at[hd], v_ref[sl, cols], sl)
            return 0

        lax.fori_loop(0, S // t, prep_keys, 0)

    lane_f = lane.astype(F32)

    def prep(hd, c):
        cur = 2 * i + c
        qsl = pl.ds(pl.multiple_of(cur * t, t), t)
        q = _rope(_rms(q_ref[c * t:(c + 1) * t, hd * D:(hd + 1) * D].astype(F32), qg_ref[...]),
                  cos_ref[qsl, :], sin_ref[qsl, :])
        yield
        kmean = kmean_ref[hd]
        km_hi = kmean.astype(BF16)
        km_split = jnp.concatenate([km_hi, (kmean - km_hi.astype(F32)).astype(BF16)], axis=0)
        q_hi = q.astype(BF16)
        q_lo = (q - q_hi.astype(F32)).astype(BF16)
        part = _dot_nt(q_hi, km_split)
        gate = part[:, :LANES] + part[:, LANES:] + _dot_nt(q_lo, km_split[:LANES, :])
        gate = jnp.where(lane < cur, gate, -jnp.inf)
        yield
        sel_bias = jnp.full((t, LANES), NEG, F32)
        for _ in range(MOBA_TOPK):
            top = jnp.max(gate, axis=-1, keepdims=True)
            yield
            first = jnp.min(jnp.where(gate == top, lane_f, float(LANES)), axis=-1, keepdims=True)
            yield
            pick = lane_f == first
            sel_bias = jnp.where(pick & (first < cur.astype(F32)), 0.0, sel_bias)
            gate = jnp.where(pick, -jnp.inf, gate)
        qs = (q * (LOG2E * D ** -0.5)).astype(BF16)
        return (jnp.concatenate([qs, jnp.zeros((t, LANES), BF16)], axis=-1),
                jnp.concatenate([qs, sel_bias.astype(BF16)], axis=-1))

    _attend_tile_pair(i, [prep(hd, c) for hd in range(ATT_HEADS) for c in range(2)],
                      ka_ref, vt_ref, m_ref, acc_ref, o_ref)


def moba_attention(z, cos, sin_signed, q_gain, k_gain):
    B, S, W = z.shape
    H = W // (3 * HEAD_DIM)
    assert ATT_BLOCK == MOBA_BLOCK
    t = 2 * ATT_BLOCK
    w = ATT_HEADS * HEAD_DIM
    hp = H // ATT_HEADS
    return pl.pallas_call(
        functools.partial(_moba_kernel, S=S),
        grid=(B, hp, S // t),
        in_specs=[
            pl.BlockSpec((None, t, w), lambda b, h, i: (b, i, h)),
            pl.BlockSpec((None, S, w), lambda b, h, i: (b, 0, hp + h)),
            pl.BlockSpec((None, S, w), lambda b, h, i: (b, 0, 2 * hp + h)),
            pl.BlockSpec((S, HEAD_DIM), lambda b, h, i: (0, 0)),
            pl.BlockSpec((S, HEAD_DIM), lambda b, h, i: (0, 0)),
            pl.BlockSpec((1, HEAD_DIM), lambda b, h, i: (0, 0)),
            pl.BlockSpec((1, HEAD_DIM), lambda b, h, i: (0, 0)),
        ],
        out_specs=pl.BlockSpec((None, t, w), lambda b, h, i: (b, i, h)),
        out_shape=jax.ShapeDtypeStruct((B, S, H * HEAD_DIM), BF16),
        scratch_shapes=_ATT_SCRATCH(S) + [pltpu.VMEM((ATT_HEADS, LANES, HEAD_DIM), F32)],
        compiler_params=_params("parallel", "parallel", "arbitrary"),
        name="moba_attention",
    )(z, z, z, cos, sin_signed, q_gain.reshape(1, -1), k_gain.reshape(1, -1))


META_IDX, META_RANK, META_GATE = 0, 2, 4


def _router_kernel(x_ref, *refs, n_proj):
    proj_refs = refs[:2 * n_proj]
    g_ref, w_ref, y_ref, xn_ref, meta_ref, cnt_ref, carry_ref = refs[2 * n_proj:]
    i = pl.program_id(0)
    tm = x_ref.shape[0]

    @pl.when(i == 0)
    def _():
        carry_ref[...] = jnp.zeros(carry_ref.shape, F32)

    y = _plus_projections(x_ref[...], proj_refs)
    y_ref[...] = y
    xn = _rms(y, g_ref[...])
    xn_ref[...] = _pack_bf16_pairs(xn)
    w = w_ref[...]
    w_hi = w.astype(BF16)
    w_lo = (w - w_hi.astype(F32)).astype(BF16)
    x_hi = xn.astype(BF16)
    x_lo = (xn - x_hi.astype(F32)).astype(BF16)
    logits = (jnp.dot(x_hi, w_hi, preferred_element_type=F32) + jnp.dot(x_hi, w_lo, preferred_element_type=F32)
              + jnp.dot(x_lo, w_hi, preferred_element_type=F32))
    lane = lax.broadcasted_iota(jnp.int32, logits.shape, 1)
    logits = jnp.where(lane < N_EXPERTS, logits, -jnp.inf)
    top1 = jnp.max(logits, axis=-1, keepdims=True)
    lane_f = lane.astype(F32)
    idx1 = jnp.min(jnp.where(logits == top1, lane_f, float(LANES)), axis=-1, keepdims=True)
    rest = jnp.where(lane_f == idx1, -jnp.inf, logits)
    top2 = jnp.max(rest, axis=-1, keepdims=True)
    idx2 = jnp.min(jnp.where(rest == top2, lane_f, float(LANES)), axis=-1, keepdims=True)
    e2 = jnp.exp(top2 - top1)
    denom = 1.0 + e2
    chosen = jnp.where((lane_f == idx1) | (lane_f == idx2), 1.0, 0.0)
    r = lax.broadcasted_iota(jnp.int32, (tm, tm), 0)
    c = lax.broadcasted_iota(jnp.int32, (tm, tm), 1)
    ahead = jnp.dot(jnp.where(r > c, 1.0, 0.0).astype(BF16), chosen.astype(BF16), preferred_element_type=F32)
    carry = carry_ref[...]
    rank = ahead + carry
    rank1 = jnp.sum(jnp.where(lane_f == idx1, rank, 0.0), axis=-1, keepdims=True)
    rank2 = jnp.sum(jnp.where(lane_f == idx2, rank, 0.0), axis=-1, keepdims=True)
    vals = (idx1, idx2, rank1, rank2, 1.0 / denom, e2 / denom)
    meta = jnp.zeros(logits.shape, F32)
    for n, v in enumerate(vals):
        meta = jnp.where(lane == n, v, meta)
    meta_ref[...] = meta
    carry = carry + jnp.sum(chosen, axis=0, keepdims=True)
    carry_ref[...] = carry
    cnt_ref[...] = jnp.broadcast_to(carry, cnt_ref.shape)


def moe_router(x, proj, gain, w_router, *, tm):
    M, K = x.shape
    w = jnp.zeros((K, LANES), F32).at[:, :N_EXPERTS].set(w_router)
    proj_specs, proj_args = _projection_specs(proj, tm)
    return pl.pallas_call(
        functools.partial(_router_kernel, n_proj=len(proj)),
        grid=(M // tm,),
        in_specs=[
            pl.BlockSpec((tm, K), lambda i: (i, 0)),
            *proj_specs,
            pl.BlockSpec((1, K), lambda i: (0, 0)),
            pl.BlockSpec((K, LANES), lambda i: (0, 0)),
        ],
        out_specs=[
            pl.BlockSpec((tm, K), lambda i: (i, 0)),
            pl.BlockSpec((tm, K // 2), lambda i: (i, 0)),
            pl.BlockSpec((tm, LANES), lambda i: (i, 0)),
            pl.BlockSpec((8, LANES), lambda i: (0, 0)),
        ],
        out_shape=[
            jax.ShapeDtypeStruct((M, K), F32),
            jax.ShapeDtypeStruct((M, K // 2), jnp.int32),
            jax.ShapeDtypeStruct((M, LANES), F32),
            jax.ShapeDtypeStruct((8, LANES), F32),
        ],
        scratch_shapes=[pltpu.VMEM((1, LANES), F32)],
        compiler_params=_params("arbitrary"),
        name="moe_router",
    )(x, *proj_args, gain.reshape(1, K), w)


def _sc_workers():
    info = plsc.get_sparse_core_info()
    return info.num_cores, info.num_cores * info.num_subcores


def scatter_rows(rows, dest, n_out):
    M, W = rows.shape
    nc, nw = _sc_workers()
    per_w = M // nw
    mesh = plsc.VectorSubcoreMesh(core_axis_name="c", subcore_axis_name="s")

    @functools.partial(
        pl.kernel, mesh=mesh, out_type=jax.ShapeDtypeStruct((n_out, W), rows.dtype),
        scratch_types=[pltpu.VMEM((SC_ROWS,), jnp.int32), pltpu.VMEM((SC_ROWS, W), rows.dtype),
                       pltpu.SemaphoreType.DMA])
    def kern(rows_hbm, dest_hbm, out_hbm, idx_v, rows_v, sem):
        wid = lax.axis_index("s") * nc + lax.axis_index("c")

        @pl.loop(0, per_w // SC_ROWS)
        def _(g):
            base = wid * per_w + g * SC_ROWS
            pltpu.sync_copy(rows_hbm.at[pl.ds(base, SC_ROWS)], rows_v)
            for k in range(2):
                pltpu.sync_copy(dest_hbm.at[k, pl.ds(base, SC_ROWS)], idx_v)
                pltpu.async_copy(rows_v, out_hbm.at[idx_v], sem).wait()

    return kern(rows, dest)


def gather_rows(table, idx):
    N = idx.shape[0]
    W = table.shape[1]
    nc, nw = _sc_workers()
    per_w = N // nw
    mesh = plsc.VectorSubcoreMesh(core_axis_name="c", subcore_axis_name="s")

    @functools.partial(
        pl.kernel, mesh=mesh, out_type=jax.ShapeDtypeStruct((N, W), table.dtype),
        scratch_types=[pltpu.VMEM((SC_ROWS,), jnp.int32), pltpu.VMEM((SC_ROWS, W), table.dtype),
                       pltpu.SemaphoreType.DMA])
    def kern(table_hbm, idx_hbm, out_hbm, idx_v, rows_v, sem):
        wid = lax.axis_index("s") * nc + lax.axis_index("c")

        @pl.loop(0, per_w // SC_ROWS)
        def _(g):
            base = wid * per_w + g * SC_ROWS
            pltpu.sync_copy(idx_hbm.at[pl.ds(base, SC_ROWS)], idx_v)
            pltpu.async_copy(table_hbm.at[idx_v], rows_v, sem).wait()
            pltpu.sync_copy(rows_v, out_hbm.at[pl.ds(base, SC_ROWS)])

    return kern(table, idx)


def _combine_kernel(h_ref, y1_ref, y2_ref, meta_ref, o_ref):
    g1 = meta_ref[:, META_GATE:META_GATE + 1]
    g2 = meta_ref[:, META_GATE + 1:META_GATE + 2]
    o_ref[...] = (h_ref[...] + g1 * _unpack_bf16_pairs(y1_ref[...]).astype(F32)
                  + g2 * _unpack_bf16_pairs(y2_ref[...]).astype(F32))


def moe_combine(h, y_pairs, meta):
    M, K = h.shape
    p = PERM_TILE
    nt = M // p
    return pl.pallas_call(
        _combine_kernel,
        grid=(nt,),
        in_specs=[
            pl.BlockSpec((p, K), lambda i: (i, 0)),
            pl.BlockSpec((p, K // 2), lambda i: (i, 0)),
            pl.BlockSpec((p, K // 2), lambda i: (nt + i, 0)),
            pl.BlockSpec((p, LANES), lambda i: (i, 0)),
        ],
        out_specs=pl.BlockSpec((p, K), lambda i: (i, 0)),
        out_shape=jax.ShapeDtypeStruct((M, K), F32),
        compiler_params=_params("parallel"),
        name="moe_combine",
    )(h, y_pairs, y_pairs, meta)


def _expert_ffn_kernel(te_ref, tv_ref, x_ref, wg_ref, wu_ref, wd_ref, o_ref, xb_ref, acc_ref):
    del te_ref
    i = pl.program_id(0)
    j = pl.program_id(1)
    tm = x_ref.shape[0]
    valid = tv_ref[i]

    def swiglu_rows(rows):
        @pl.when(j == 0)
        def _():
            xb_ref[0:rows, :] = _unpack_bf16_pairs(x_ref[0:rows, :])

        x = xb_ref[0:rows, :]
        gt = jnp.dot(x, wg_ref[...], preferred_element_type=F32)
        up = jnp.dot(x, wu_ref[...], preferred_element_type=F32)
        act = (gt / (1.0 + jnp.exp(-gt)) * up).astype(BF16)
        part = jnp.dot(act, wd_ref[...], preferred_element_type=F32)

        @pl.when(j == 0)
        def _():
            acc_ref[0:rows, :] = part

        @pl.when(j > 0)
        def _():
            acc_ref[0:rows, :] += part

        @pl.when(j == pl.num_programs(1) - 1)
        def _():
            o_ref[0:rows, :] = _pack_bf16_pairs(acc_ref[0:rows, :])
            if rows < tm:
                o_ref[rows:tm, :] = jnp.zeros((tm - rows, o_ref.shape[1]), o_ref.dtype)

    @pl.when(valid > tm // 2)
    def _():
        swiglu_rows(tm)

    @pl.when((valid > 0) & (valid <= tm // 2))
    def _():
        swiglu_rows(tm // 2)

    @pl.when((valid == 0) & (j == 0))
    def _():
        o_ref[...] = jnp.zeros(o_ref.shape, o_ref.dtype)


def expert_ffn(xs, tile_expert, tile_valid, w_gate_up, w_down, *, tm, tf):
    R = xs.shape[0]
    E, F, K = w_down.shape
    nf = F // tf
    live = lambda i, tv: tv[i] > 0
    col = lambda i, j, tv: jnp.where(live(i, tv), j, nf - 1)
    grid_spec = pltpu.PrefetchScalarGridSpec(
        num_scalar_prefetch=2,
        grid=(R // tm, nf),
        in_specs=[
            pl.BlockSpec((tm, K // 2), lambda i, j, te, tv: (te[R // tm + i], 0)),
            pl.BlockSpec((None, K, tf), lambda i, j, te, tv: (te[i], 0, col(i, j, tv))),
            pl.BlockSpec((None, K, tf), lambda i, j, te, tv: (te[i], 0, nf + col(i, j, tv))),
            pl.BlockSpec((None, tf, K), lambda i, j, te, tv: (te[i], col(i, j, tv), 0)),
        ],
        out_specs=pl.BlockSpec((tm, K // 2), lambda i, j, te, tv: (i, 0)),
        scratch_shapes=[pltpu.VMEM((tm, K), BF16), pltpu.VMEM((tm, K), F32)],
    )
    return pl.pallas_call(
        _expert_ffn_kernel,
        grid_spec=grid_spec,
        out_shape=jax.ShapeDtypeStruct((R, K // 2), jnp.int32),
        compiler_params=_params("arbitrary", "arbitrary"),
        name="moe_expert_ffn",
    )(tile_expert, tile_valid, xs, w_gate_up, w_gate_up, w_down)


def moe_residual(h, proj, gain, w_router, w_gate_up, w_down):
    M, K = h.shape
    p = PERM_TILE
    tm = EXPERT_TILE
    n_rows = 2 * M + N_EXPERTS * tm
    h, xn, meta, cnt = moe_router(h, proj, gain, w_router, tm=p)

    counts = cnt[0, :N_EXPERTS].astype(jnp.int32)
    padded = (counts + tm - 1) // tm * tm
    ends = jnp.cumsum(padded)
    offsets = ends - padded
    n_tiles = n_rows // tm
    tile_row = jnp.arange(n_tiles) * tm
    n_used = ends[-1] // tm
    last_used = jnp.minimum(tile_row // tm, n_used - 1)
    expert_of = lambda row: jnp.minimum(jnp.sum(ends[None, :] <= row[:, None], axis=1), N_EXPERTS - 1)
    tile_expert = expert_of(last_used * tm)
    row_in_expert = tile_row - jnp.sum(jnp.where(tile_expert[:, None] == jnp.arange(N_EXPERTS), offsets, 0), axis=1)
    own_count = jnp.sum(jnp.where(tile_expert[:, None] == jnp.arange(N_EXPERTS), counts, 0), axis=1)
    tile_valid = jnp.where(tile_row // tm < n_used, jnp.clip(own_count - row_in_expert, 0, tm), 0)
    tile_tables = jnp.concatenate([tile_expert, last_used]).astype(jnp.int32)

    idx = meta[:, META_IDX:META_IDX + 2].astype(jnp.int32)
    rank = meta[:, META_RANK:META_RANK + 2].astype(jnp.int32)
    dest = (jnp.sum(jnp.where(idx[:, :, None] == jnp.arange(N_EXPERTS), offsets, 0), axis=-1) + rank).T

    xs = scatter_rows(xn, dest, n_rows)
    ys = expert_ffn(xs, tile_tables, tile_valid.astype(jnp.int32), w_gate_up, w_down, tm=tm, tf=896)
    return moe_combine(h, gather_rows(ys, dest.reshape(-1)), meta)


def _ffn_kernel(x_ref, *refs, n_proj):
    proj_refs = refs[:2 * n_proj]
    g_ref, wg_ref, wu_ref, wd_ref, o_ref, xn_ref, acc_ref = refs[2 * n_proj:]
    j = pl.program_id(1)

    @pl.when(j == 0)
    def _():
        x = _plus_projections(x_ref[...], proj_refs)
        xn_ref[...] = _rms(x, g_ref[...]).astype(BF16)
        acc_ref[...] = x

    xn = xn_ref[...]
    gt = jnp.dot(xn, wg_ref[...], preferred_element_type=F32)
    up = jnp.dot(xn, wu_ref[...], preferred_element_type=F32)
    act = gt / (1.0 + jnp.exp(-gt)) * up
    acc_ref[...] += jnp.dot(act.astype(BF16), wd_ref[...], preferred_element_type=F32)

    @pl.when(j == pl.num_programs(1) - 1)
    def _():
        o_ref[...] = acc_ref[...]


def ffn_residual(x, proj, gain, w_gate_up, w_down, *, tm, tf):
    M, K = x.shape
    F = w_down.shape[0]
    nf = F // tf
    proj_specs, proj_args = _projection_specs(proj, tm)
    return pl.pallas_call(
        functools.partial(_ffn_kernel, n_proj=len(proj)),
        grid=(M // tm, nf),
        in_specs=[
            pl.BlockSpec((tm, K), lambda i, j: (i, 0)),
            *proj_specs,
            pl.BlockSpec((1, K), lambda i, j: (0, 0)),
            pl.BlockSpec((K, tf), lambda i, j: (0, j)),
            pl.BlockSpec((K, tf), lambda i, j: (0, nf + j)),
            pl.BlockSpec((tf, K), lambda i, j: (j, 0)),
        ],
        out_specs=pl.BlockSpec((tm, K), lambda i, j: (i, 0)),
        out_shape=jax.ShapeDtypeStruct((M, K), F32),
        scratch_shapes=[pltpu.VMEM((tm, K), BF16), pltpu.VMEM((tm, K), F32)],
        compiler_params=_params("parallel", "arbitrary"),
        name="ffn_residual",
    )(x, *proj_args, gain.reshape(1, K), w_gate_up, w_gate_up, w_down)


def _even_mix(h, B, S, norm1, w_in, f_bias, q_norm, k_norm, conv_w, a_log, dt_bias, o_norm, w_out):
    M, D = h.shape
    fw, gw = FOX_HEADS * HEAD_DIM, GDN_HEADS * HEAD_DIM
    o_ff = 3 * fw
    o_gq = o_ff + FOX_HEADS
    o_ga = o_gq + 3 * gw
    o_gb = o_ga + GDN_HEADS
    o_gg = o_gb + GDN_HEADS
    w_big = jnp.concatenate([w_in[:, :o_ff], w_in[:, o_gq:o_ga], w_in[:, o_gg:]], axis=1)
    w_small = jnp.concatenate([w_in[:, o_ff:o_gq], w_in[:, o_ga:o_gg],
                               jnp.zeros((D, LANES - FOX_HEADS - 2 * GDN_HEADS), F32)], axis=1)
    w_big, w_small = (t.astype(BF16) for t in lax.optimization_barrier((w_big, w_small)))
    z, zs = norm_matmul(h, norm1, w_big, w_small, tm=ROW_TILE, tn=1792)
    par = jnp.zeros((8, LANES), F32)
    par = par.at[0, LANE_F:LANE_F + FOX_HEADS].set(f_bias).at[0, LANE_A:LANE_A + GDN_HEADS].set(dt_bias)
    par = par.at[1, LANE_A:LANE_A + GDN_HEADS].set(a_log)
    col, row = even_gates(zs.reshape(B, S, LANES), par)
    z = z.reshape(B, S, -1)
    fox = fox_attention(z, col, q_norm, k_norm)
    gdn = gdn_mixer(z, conv_w, col, row, o_norm)
    w_out = w_out.astype(BF16)
    return [(fox.reshape(M, fw), w_out[:fw]), (gdn.reshape(M, gw), w_out[fw:])]


def _odd_mix(h, B, S, norm1, w_qkv, q_norm, k_norm, w_out):
    M, D = h.shape
    z = norm_matmul(h, norm1, w_qkv.astype(BF16), tm=ROW_TILE, tn=1536).reshape(B, S, -1)
    half = HEAD_DIM // 2
    inv = jnp.power(ROPE_THETA, -jnp.arange(half, dtype=F32) / half)
    ang = jnp.arange(S, dtype=F32)[:, None] * inv[None, :]
    cos, sin = jnp.cos(ang), jnp.sin(ang)
    cos_full = jnp.concatenate([cos, cos], axis=-1)
    sin_signed = jnp.concatenate([-sin, sin], axis=-1)
    att = moba_attention(z, cos_full, sin_signed, q_norm, k_norm)
    return [(att.reshape(M, -1), w_out.astype(BF16))]


def kernel(x, e_norm1, e_w_in, e_fox_f_bias, e_fox_q_norm, e_fox_k_norm, e_gdn_conv, e_gdn_a_log,
           e_gdn_dt_bias, e_gdn_o_norm, e_w_out, e_norm2, e_ffn_w_gate_up, e_ffn_w_down,
           o_norm1, o_w_qkv, o_q_norm, o_k_norm, o_w_out, o_norm2, o_router, o_exp_w_gate_up, o_exp_w_down):
    B, S, D = x.shape
    h = x.reshape(B * S, D)
    depth = e_norm1.shape[0] + o_norm1.shape[0]
    for layer in range(depth):
        i = layer // 2
        if layer % 2 == 0:
            mix = _even_mix(h, B, S, e_norm1[i], e_w_in[i], e_fox_f_bias[i], e_fox_q_norm[i], e_fox_k_norm[i],
                            e_gdn_conv[i], e_gdn_a_log[i], e_gdn_dt_bias[i], e_gdn_o_norm[i], e_w_out[i])
            h = ffn_residual(h, mix, e_norm2[i], e_ffn_w_gate_up[i].astype(BF16), e_ffn_w_down[i].astype(BF16),
                             tm=ROW_TILE // 2, tf=1408)
        else:
            mix = _odd_mix(h, B, S, o_norm1[i], o_w_qkv[i], o_q_norm[i], o_k_norm[i], o_w_out[i])
            h = moe_residual(h, mix, o_norm2[i], o_router[i], o_exp_w_gate_up[i].astype(BF16),
                             o_exp_w_down[i].astype(BF16))
    return h.reshape(B, S, D)
```

```python
import functools

import jax
import jax.numpy as jnp
from jax import lax
from jax.experimental import pallas as pl
from jax.experimental.pallas import tpu as pltpu
from jax.experimental.pallas import tpu_sc as plsc

F32 = jnp.float32
BF16 = jnp.bfloat16

HEAD_DIM = 128
FOX_HEADS = 4
GDN_HEADS = 4
CONV_WIDTH = 4
MOBA_BLOCK = 256
MOBA_TOPK = 3
N_EXPERTS = 8
ROPE_THETA = 10000.0
EPS = 1e-6

LANES = 128
GDN_CHUNK = 128
INV_BLOCK = 16
GDN_HEADS_PER_STEP = 4
GDN_CHUNKS_PER_STEP = 4
NEG = -(2.0 ** 100)
LOG2E = 1.4426950408889634
ATT_BLOCK = 256
ATT_HEADS = 4
V_ROWS = HEAD_DIM + 16
PERM_TILE = 1024
SC_ROWS = 64
ROW_TILE = 1024
EXPERT_TILE = 1024
VMEM_LIMIT_BYTES = 56 * 1024 * 1024

FOX_Q0, FOX_K0, FOX_V0 = 0, 4, 8
GDN_Q0, GDN_K0, GDN_V0, GDN_G0 = 12, 16, 20, 24
LANE_F, LANE_A, LANE_B = 0, 4, 8


def _params(*sem):
    return pltpu.CompilerParams(dimension_semantics=sem, vmem_limit_bytes=VMEM_LIMIT_BYTES)


def _rms(x, gain):
    return x * lax.rsqrt(jnp.mean(x * x, axis=-1, keepdims=True) + EPS) * gain


def _dot_nt(a, b, **kw):
    return lax.dot_general(a, b, (((1,), (1,)), ((), ())), preferred_element_type=F32, **kw)


def _pick_lane(x, lane_idx):
    lane = lax.broadcasted_iota(jnp.int32, x.shape, 1)
    return jnp.sum(jnp.where(lane == lane_idx, x, 0.0), axis=-1, keepdims=True)


def _pack_bf16_pairs(x):
    n = x.shape[1] // 2
    hi = pltpu.bitcast(x[:, :n].astype(BF16).astype(F32), jnp.uint32)
    lo = pltpu.bitcast(x[:, n:].astype(BF16).astype(F32), jnp.uint32)
    return pltpu.bitcast(hi | (lo >> 16), jnp.int32)


def _unpack_bf16_pairs(w):
    u = pltpu.bitcast(w, jnp.uint32)
    hi = pltpu.bitcast(u & jnp.uint32(0xFFFF0000), F32).astype(BF16)
    lo = pltpu.bitcast(u << 16, F32).astype(BF16)
    return jnp.concatenate([hi, lo], axis=1)


def _norm_mm_kernel(x_ref, g_ref, w_ref, *rest, has_aux):
    if has_aux:
        waux_ref, o_ref, oaux_ref, xn_ref = rest
    else:
        o_ref, xn_ref = rest

    @pl.when(pl.program_id(1) == 0)
    def _():
        xn = _rms(x_ref[...], g_ref[...]).astype(BF16)
        xn_ref[...] = xn
        if has_aux:
            oaux_ref[...] = jnp.dot(xn, waux_ref[...], preferred_element_type=F32)

    o_ref[...] = jnp.dot(xn_ref[...], w_ref[...], preferred_element_type=F32).astype(o_ref.dtype)


def norm_matmul(x, gain, w, w_aux=None, *, tm, tn, out_dtype=BF16):
    M, K = x.shape
    N = w.shape[1]
    has_aux = w_aux is not None
    in_specs = [
        pl.BlockSpec((tm, K), lambda i, j: (i, 0)),
        pl.BlockSpec((1, K), lambda i, j: (0, 0)),
        pl.BlockSpec((K, tn), lambda i, j: (0, j)),
    ]
    out_shape = [jax.ShapeDtypeStruct((M, N), out_dtype)]
    out_specs = [pl.BlockSpec((tm, tn), lambda i, j: (i, j))]
    args = [x, gain.reshape(1, K), w]
    if has_aux:
        in_specs.append(pl.BlockSpec((K, LANES), lambda i, j: (0, 0)))
        out_shape.append(jax.ShapeDtypeStruct((M, LANES), F32))
        out_specs.append(pl.BlockSpec((tm, LANES), lambda i, j: (i, 0)))
        args.append(w_aux)
    res = pl.pallas_call(
        functools.partial(_norm_mm_kernel, has_aux=has_aux),
        grid=(M // tm, N // tn),
        in_specs=in_specs,
        out_specs=out_specs,
        out_shape=out_shape,
        scratch_shapes=[pltpu.VMEM((tm, K), BF16)],
        compiler_params=_params("parallel", "arbitrary"),
        name="norm_matmul",
    )(*args)
    return res if has_aux else res[0]


def _plus_projections(x, proj_refs):
    for a_ref, w_ref in zip(proj_refs[0::2], proj_refs[1::2]):
        x = x + jnp.dot(a_ref[...], w_ref[...], preferred_element_type=F32)
    return x


def _projection_specs(pairs, tm):
    specs, args = [], []
    for a, w in pairs:
        specs.append(pl.BlockSpec((tm, a.shape[1]), lambda i, *_: (i, 0)))
        specs.append(pl.BlockSpec(w.shape, lambda i, *_: (0, 0)))
        args += [a, w]
    return specs, args


def _gate_kernel(zs_ref, par_ref, col_ref, row_ref, *, S, B):
    C = GDN_CHUNK
    bias = par_ref[0:1, :]
    neg_a = -jnp.exp(par_ref[1:2, :])
    r = lax.broadcasted_iota(jnp.int32, (C, C), 0)
    c = lax.broadcasted_iota(jnp.int32, (C, C), 1)
    tril = (r >= c).astype(F32)
    lane = lax.broadcasted_iota(jnp.int32, (C, LANES), 1)

    def body(n, carry):
        sl = pl.ds(pl.multiple_of(n * C, C), C)
        us, betas = [], []
        for bi in range(B):
            z = zs_ref[bi, sl, :]
            t = z + bias
            soft = jnp.log(1.0 + jnp.exp(-jnp.abs(t)))
            log_f = jnp.minimum(t, 0.0) - soft
            g = neg_a * (jnp.maximum(t, 0.0) + soft)
            betas.append(1.0 / (1.0 + jnp.exp(-z)))
            us.append(jnp.where(lane < LANE_A, log_f, jnp.where(lane < LANE_B, g, 0.0)))
        sums = jnp.dot(tril, jnp.concatenate(us, axis=1), preferred_element_type=F32, precision=lax.Precision.HIGHEST)
        last = []
        for bi in range(B):
            cs = sums[:, bi * LANES:(bi + 1) * LANES] + jnp.where(lane < LANE_A, carry[bi], 0.0)
            out = jnp.where(lane < LANE_B, cs, betas[bi])
            col_ref[bi, sl, :] = out
            out_t = out.T
            for hd in range(GDN_HEADS):
                row_ref[bi, hd, :, sl] = out_t[LANE_A + hd:LANE_A + hd + 1, :]
            last.append(cs[C - 1:C, :])
        return tuple(last)

    lax.fori_loop(0, S // C, body, tuple(jnp.zeros((1, LANES), F32) for _ in range(B)))


def even_gates(zs, par):
    B, S, _ = zs.shape
    return pl.pallas_call(
        functools.partial(_gate_kernel, S=S, B=B),
        grid=(1,),
        in_specs=[
            pl.BlockSpec((B, S, LANES), lambda i: (0, 0, 0)),
            pl.BlockSpec((8, LANES), lambda i: (0, 0)),
        ],
        out_specs=[
            pl.BlockSpec((B, S, LANES), lambda i: (0, 0, 0)),
            pl.BlockSpec((B, GDN_HEADS, 1, S), lambda i: (0, 0, 0, 0)),
        ],
        out_shape=[jax.ShapeDtypeStruct((B, S, LANES), F32), jax.ShapeDtypeStruct((B, GDN_HEADS, 1, S), F32)],
        compiler_params=_params("arbitrary"),
        name="even_gates",
    )(zs, par)


def _split3(x):
    hi = x.astype(BF16).astype(F32)
    mid = (x - hi).astype(BF16).astype(F32)
    lo = (x - hi - mid).astype(BF16).astype(F32)
    return hi, mid, lo


def _interleave(gens):
    out = [None] * len(gens)
    live = list(range(len(gens)))
    while live:
        for n in list(live):
            try:
                next(gens[n])
            except StopIteration as stop:
                out[n] = stop.value
                live.remove(n)
    return out


def _attend(state, c, qa, ka_ref, vt_ref, key0, nkeys, keep=None):
    ks = pl.ds(pl.multiple_of(key0, ATT_BLOCK), nkeys)
    st = _dot_nt(ka_ref[ks, :], qa)
    yield
    if keep is not None:
        st = jnp.where(keep, st, NEG)
    m_new = jnp.max(st, axis=0, keepdims=True)
    if state[c] is not None:
        m_old, acc_old = state[c]
        m_new = jnp.maximum(m_old, m_new)
    p = jnp.exp2(st - m_new).astype(BF16)
    pv = jnp.dot(vt_ref[:, ks], p, preferred_element_type=F32)
    state[c] = (m_new, pv if state[c] is None else acc_old * jnp.exp2(m_old - m_new) + pv)


def _attend_tile_pair(i, prep, ka_ref, vt_ref, m_ref, acc_ref, o_ref):
    t = ATT_BLOCK
    chains = [(hd, c) for c in range(2) for hd in range(ATT_HEADS)]
    num = lambda hd, c: 2 * hd + c
    qa = _interleave(prep)

    def step(state, hd, c, which, **kw):
        return _attend(state, num(hd, c), qa[num(hd, c)][which], ka_ref.at[hd], vt_ref.at[hd], **kw)

    def save(state):
        m_ref[...] = jnp.stack([state[n][0] for n in range(len(chains))])
        acc_ref[...] = jnp.stack([state[n][1] for n in range(len(chains))])

    def past_blocks(key0, n_steps):
        state = {n: (m_ref[n], acc_ref[n]) for n in range(len(chains))}
        _interleave([step(state, hd, c, 1, key0=key0 + s * 2 * t, nkeys=2 * t)
                     for s in range(n_steps) for hd, c in chains])
        save(state)

    state = {n: None for n in range(len(chains))}
    keep = _causal_keep()
    _interleave([step(state, hd, c, 0, key0=(2 * i + c) * t, nkeys=t, keep=keep) for hd, c in chains]
                + [step(state, hd, 1, 1, key0=(2 * i) * t, nkeys=t) for hd in range(ATT_HEADS)])
    save(state)

    def four_blocks(g, _):
        past_blocks(g * (4 * t), 2)
        return 0

    lax.fori_loop(0, i // 2, four_blocks, 0)

    @pl.when((i & 1) != 0)
    def _():
        past_blocks((i // 2) * (4 * t), 1)

    for hd, c in chains:
        acc = acc_ref[num(hd, c)]
        out_t = acc[:HEAD_DIM, :] * (1.0 / acc[HEAD_DIM:HEAD_DIM + 1, :])
        o_ref[c * t:(c + 1) * t, hd * HEAD_DIM:(hd + 1) * HEAD_DIM] = out_t.T.astype(o_ref.dtype)


def _store_vt(vt_ref, v, sl):
    vt_ref[0:HEAD_DIM, sl] = v.astype(F32).T.astype(BF16)
    vt_ref[HEAD_DIM:V_ROWS, sl] = jnp.ones((V_ROWS - HEAD_DIM, ATT_BLOCK), BF16)


def _causal_keep():
    key = lax.broadcasted_iota(jnp.int32, (ATT_BLOCK, ATT_BLOCK), 0)
    qry = lax.broadcasted_iota(jnp.int32, (ATT_BLOCK, ATT_BLOCK), 1)
    return key <= qry


_ATT_SCRATCH = lambda S: [pltpu.VMEM((ATT_HEADS, S, 2 * HEAD_DIM), BF16), pltpu.VMEM((ATT_HEADS, V_ROWS, S), BF16),
                          pltpu.VMEM((2 * ATT_HEADS, 1, ATT_BLOCK), F32),
                          pltpu.VMEM((2 * ATT_HEADS, V_ROWS, ATT_BLOCK), F32)]


def _fox_kernel(q_ref, k_ref, v_ref, col_ref, qg_ref, kg_ref, o_ref, ka_ref, vt_ref, m_ref, acc_ref, *, S):
    h0 = pl.program_id(1) * ATT_HEADS
    i = pl.program_id(2)
    t = ATT_BLOCK
    D = HEAD_DIM
    lane = lax.broadcasted_iota(jnp.int32, (t, LANES), 1)

    @pl.when(i == 0)
    def _():
        def prep_keys(n, _):
            sl = pl.ds(pl.multiple_of(n * t, t), t)
            for hd in range(ATT_HEADS):
                cols = slice(hd * D, (hd + 1) * D)
                ka_ref[hd, sl, 0:D] = _rms(k_ref[sl, cols].astype(F32), kg_ref[...]).astype(BF16)
                hi, mid, lo = _split3(-LOG2E * _pick_lane(col_ref[sl, :], LANE_F + h0 + hd))
                ka_ref[hd, sl, D:] = jnp.where(
                    lane < 3, 1.0, jnp.where(lane == 3, hi, jnp.where(lane == 4, mid, jnp.where(lane == 5, lo, 0.0)))
                ).astype(BF16)
                _store_vt(vt_ref.at[hd], v_ref[sl, cols], sl)
            return 0

        lax.fori_loop(0, S // t, prep_keys, 0)

    def prep(hd, c):
        q = q_ref[c * t:(c + 1) * t, hd * D:(hd + 1) * D].astype(F32)
        qn = (_rms(q, qg_ref[...]) * (LOG2E * D ** -0.5)).astype(BF16)
        yield
        qsl = pl.ds(pl.multiple_of((2 * i + c) * t, t), t)
        hi, mid, lo = _split3(LOG2E * _pick_lane(col_ref[qsl, :], LANE_F + h0 + hd))
        aug = jnp.where(lane == 0, hi,
                        jnp.where(lane == 1, mid, jnp.where(lane == 2, lo, jnp.where(lane < 6, 1.0, 0.0))))
        qa = jnp.concatenate([qn, aug.astype(BF16)], axis=-1)
        return qa, qa

    _attend_tile_pair(i, [prep(hd, c) for hd in range(ATT_HEADS) for c in range(2)],
                      ka_ref, vt_ref, m_ref, acc_ref, o_ref)


def fox_attention(z, col, q_gain, k_gain):
    B, S, _ = z.shape
    t = 2 * ATT_BLOCK
    w = ATT_HEADS * HEAD_DIM
    return pl.pallas_call(
        functools.partial(_fox_kernel, S=S),
        grid=(B, FOX_HEADS // ATT_HEADS, S // t),
        in_specs=[
            pl.BlockSpec((None, t, w), lambda b, h, i: (b, i, FOX_Q0 // ATT_HEADS + h)),
            pl.BlockSpec((None, S, w), lambda b, h, i: (b, 0, FOX_K0 // ATT_HEADS + h)),
            pl.BlockSpec((None, S, w), lambda b, h, i: (b, 0, FOX_V0 // ATT_HEADS + h)),
            pl.BlockSpec((None, S, LANES), lambda b, h, i: (b, 0, 0)),
            pl.BlockSpec((1, HEAD_DIM), lambda b, h, i: (0, 0)),
            pl.BlockSpec((1, HEAD_DIM), lambda b, h, i: (0, 0)),
        ],
        out_specs=pl.BlockSpec((None, t, w), lambda b, h, i: (b, i, h)),
        out_shape=jax.ShapeDtypeStruct((B, S, FOX_HEADS * HEAD_DIM), BF16),
        scratch_shapes=_ATT_SCRATCH(S),
        compiler_params=_params("parallel", "parallel", "arbitrary"),
        name="fox_attention",
    )(z, z, z, col, q_gain.reshape(1, -1), k_gain.reshape(1, -1))


def _unit_lower_inverse(m):
    n = m.shape[0]
    r = lax.broadcasted_iota(jnp.int32, (n, n), 0)
    c = lax.broadcasted_iota(jnp.int32, (n, n), 1)
    eye = (r == c).astype(F32)

    def same_block(b):
        return (r // b) == (c // b)

    p = jnp.where(same_block(INV_BLOCK), m, 0.0)
    inv = eye - p
    k = 2
    while k < INV_BLOCK:
        pb = p.astype(BF16)
        p = jnp.dot(pb, pb, preferred_element_type=F32)
        yield
        inv = jnp.dot(inv.astype(BF16), (eye + p).astype(BF16), preferred_element_type=F32)
        yield
        k *= 2
    b = INV_BLOCK
    while b < n:
        off = jnp.where(same_block(2 * b), jnp.where(same_block(b), 0.0, m), 0.0).astype(BF16)
        ib = inv.astype(BF16)
        left = jnp.dot(ib, off, preferred_element_type=F32).astype(BF16)
        yield
        inv = inv - jnp.dot(left, ib, preferred_element_type=F32)
        yield
        b *= 2
    return inv


def _gdn_kernel(q_ref, k_ref, v_ref, gg_ref, wq_ref, wk_ref, wv_ref, col_ref, row_ref, on_ref,
                o_ref, pad_ref, qs_ref, ks_ref, vs_ref, *, S, rows, hb):
    h0 = pl.program_id(1) * hb
    C = GDN_CHUNK
    D = HEAD_DIM

    def conv_silu(x_ref, w_ref, dst_ref, hh, mode):
        cols = slice(hh * D, (hh + 1) * D)
        pad_ref[0:8, :] = jnp.zeros((8, D), F32)

        def fill(n, _):
            src = pl.ds(pl.multiple_of(n * rows, rows), rows)
            pad_ref[pl.ds(pl.multiple_of(8 + n * rows, 8), rows), :] = x_ref[src, cols].astype(F32)
            return 0

        lax.fori_loop(0, S // rows, fill, 0)
        w = w_ref[:, cols]

        def conv(n, _):
            base = pl.multiple_of(n * rows, rows)
            win = pad_ref[pl.ds(base, rows + 8), :]
            y = jnp.zeros((rows, D), F32)
            for tap in range(CONV_WIDTH):
                lead = 8 - (CONV_WIDTH - 1) + tap
                y = y + w[tap:tap + 1, :] * pltpu.roll(win, rows + 8 - lead, 0)[0:rows, :]
            y = y / (1.0 + jnp.exp(-y))
            if mode != "v":
                y = y * lax.rsqrt(jnp.sum(y * y, axis=-1, keepdims=True) + EPS)
            if mode == "q":
                y = y * D ** -0.5
            dst_ref[hh, pl.ds(base, rows), :] = y.astype(dst_ref.dtype)
            return 0

        lax.fori_loop(0, S // rows, conv, 0)

    for hh in range(hb):
        conv_silu(q_ref, wq_ref, qs_ref, hh, "q")
        conv_silu(k_ref, wk_ref, ks_ref, hh, "k")
        conv_silu(v_ref, wv_ref, vs_ref, hh, "v")

    r = lax.broadcasted_iota(jnp.int32, (C, C), 0)
    c = lax.broadcasted_iota(jnp.int32, (C, C), 1)
    incl = r >= c
    strict = r > c

    def chunk_local(hh, sl, tab):
        q = qs_ref[hh, sl, :].astype(F32)
        k = ks_ref[hh, sl, :].astype(F32)
        v = vs_ref[hh, sl, :].astype(F32)
        gcol = _pick_lane(tab, LANE_A + h0 + hh)
        beta = _pick_lane(tab, LANE_B + h0 + hh)
        grow = row_ref[hh, :, sl]
        glast = gcol[C - 1:C, :]
        decay = jnp.where(incl, jnp.exp(jnp.where(incl, gcol - grow, 0.0)), 0.0)
        eg = jnp.exp(gcol)
        kb = k * beta
        kbf = k.astype(BF16)
        m = jnp.where(strict, _dot_nt(kb.astype(BF16), kbf) * decay, 0.0)
        attn = (_dot_nt(q.astype(BF16), kbf) * decay).astype(BF16)
        yield
        tinv = (yield from _unit_lower_inverse(m)).astype(BF16)
        rhs = jnp.concatenate([v * beta, kb * eg], axis=-1).astype(BF16)
        sol = jnp.dot(tinv, rhs, preferred_element_type=F32)
        yield
        gate = gg_ref[sl, hh * D:(hh + 1) * D].astype(F32)
        return dict(u=sol[:, :D], w=sol[:, D:].astype(BF16), attn=attn, qg=(q * eg).astype(BF16),
                    kg_t=(k * jnp.exp(glast - gcol)).T.astype(BF16), keep=jnp.exp(glast),
                    gate=gate / (1.0 + jnp.exp(-gate)))

    def chunk_state(parts, state):
        outs = []
        for c in parts:
            sb = state.astype(BF16)
            v_new = c["u"] - jnp.dot(c["w"], sb, preferred_element_type=F32)
            o_state = jnp.dot(c["qg"], sb, preferred_element_type=F32)
            yield
            vb = v_new.astype(BF16)
            o = o_state + jnp.dot(c["attn"], vb, preferred_element_type=F32)
            state = state * c["keep"] + jnp.dot(c["kg_t"], vb, preferred_element_type=F32)
            yield
            outs.append((_rms(o, on_ref[...]) * c["gate"]).astype(o_ref.dtype))
        return state, outs

    def chunk_group(n, states):
        sls = [pl.ds(pl.multiple_of((n * GDN_CHUNKS_PER_STEP + g) * C, C), C) for g in range(GDN_CHUNKS_PER_STEP)]
        tabs = [col_ref[sl, :] for sl in sls]
        parts = _interleave([chunk_local(hh, sls[g], tabs[g]) for g in range(GDN_CHUNKS_PER_STEP) for hh in range(hb)])
        res = _interleave([chunk_state([parts[g * hb + hh] for g in range(GDN_CHUNKS_PER_STEP)], states[hh])
                           for hh in range(hb)])
        for g in range(GDN_CHUNKS_PER_STEP):
            o_ref[sls[g], :] = jnp.concatenate([outs[g] for _, outs in res], axis=-1)
        return tuple(s for s, _ in res)

    lax.fori_loop(0, S // (C * GDN_CHUNKS_PER_STEP), chunk_group, tuple(jnp.zeros((D, D), F32) for _ in range(hb)))


def gdn_mixer(z, conv_w, col, row, o_gain):
    B, S, _ = z.shape
    D = HEAD_DIM
    hb = GDN_HEADS_PER_STEP
    rows = 256
    seq = lambda off: pl.BlockSpec((None, S, hb * D), lambda b, h: (b, 0, off // hb + h),
                                   pipeline_mode=pl.Buffered(1))
    cw = lambda off: pl.BlockSpec((CONV_WIDTH, hb * D), lambda b, h: (0, off // hb + h))
    return pl.pallas_call(
        functools.partial(_gdn_kernel, S=S, rows=rows, hb=hb),
        grid=(B, GDN_HEADS // hb),
        in_specs=[
            seq(GDN_Q0), seq(GDN_K0), seq(GDN_V0), seq(GDN_G0),
            cw(0), cw(GDN_HEADS), cw(2 * GDN_HEADS),
            pl.BlockSpec((None, S, LANES), lambda b, h: (b, 0, 0)),
            pl.BlockSpec((None, hb, 1, S), lambda b, h: (b, h, 0, 0)),
            pl.BlockSpec((1, D), lambda b, h: (0, 0)),
        ],
        out_specs=pl.BlockSpec((None, S, hb * D), lambda b, h: (b, 0, h)),
        out_shape=jax.ShapeDtypeStruct((B, S, GDN_HEADS * D), BF16),
        scratch_shapes=[pltpu.VMEM((S + 8, D), F32)] + [pltpu.VMEM((hb, S, D), BF16)] * 3,
        compiler_params=_params("parallel", "parallel"),
        name="gdn_mixer",
    )(z, z, z, z, conv_w, conv_w, conv_w, col, row, o_gain.reshape(1, D))


def _rope(x, cos, sin_signed):
    return x * cos + pltpu.roll(x, HEAD_DIM // 2, 1) * sin_signed


def _moba_kernel(q_ref, k_ref, v_ref, cos_ref, sin_ref, qg_ref, kg_ref, o_ref,
                 ka_ref, vt_ref, m_ref, acc_ref, kmean_ref, *, S):
    i = pl.program_id(2)
    t = ATT_BLOCK
    D = HEAD_DIM
    lane = lax.broadcasted_iota(jnp.int32, (t, LANES), 1)

    @pl.when(i == 0)
    def _():
        kmean_ref[...] = jnp.zeros(kmean_ref.shape, F32)

        def prep_keys(n, _):
            sl = pl.ds(pl.multiple_of(n * t, t), t)
            for hd in range(ATT_HEADS):
                cols = slice(hd * D, (hd + 1) * D)
                k = _rope(_rms(k_ref[sl, cols].astype(F32), kg_ref[...]), cos_ref[sl, :], sin_ref[sl, :])
                ka_ref[hd, sl, 0:D] = k.astype(BF16)
                ka_ref[hd, sl, D:] = jnp.where(lane == n, 1.0, 0.0).astype(BF16)
                kmean_ref[hd, pl.ds(n, 1), :] = jnp.mean(k, axis=0, keepdims=True)
                _store_vt(vt_ref.at[hd], v_ref[sl, cols], sl)
            return 0

        lax.fori_loop(0, S // t, prep_keys, 0)

    lane_f = lane.astype(F32)

    def prep(hd, c):
        cur = 2 * i + c
        qsl = pl.ds(pl.multiple_of(cur * t, t), t)
        q = _rope(_rms(q_ref[c * t:(c + 1) * t, hd * D:(hd + 1) * D].astype(F32), qg_ref[...]),
                  cos_ref[qsl, :], sin_ref[qsl, :])
        yield
        kmean = kmean_ref[hd]
        km_hi = kmean.astype(BF16)
        km_split = jnp.concatenate([km_hi, (kmean - km_hi.astype(F32)).astype(BF16)], axis=0)
        q_hi = q.astype(BF16)
        q_lo = (q - q_hi.astype(F32)).astype(BF16)
        part = _dot_nt(q_hi, km_split)
        gate = part[:, :LANES] + part[:, LANES:] + _dot_nt(q_lo, km_split[:LANES, :])
        gate = jnp.where(lane < cur, gate, -jnp.inf)
        yield
        sel_bias = jnp.full((t, LANES), NEG, F32)
        for _ in range(MOBA_TOPK):
            top = jnp.max(gate, axis=-1, keepdims=True)
            yield
            first = jnp.min(jnp.where(gate == top, lane_f, float(LANES)), axis=-1, keepdims=True)
            yield
            pick = lane_f == first
            sel_bias = jnp.where(pick & (first < cur.astype(F32)), 0.0, sel_bias)
            gate = jnp.where(pick, -jnp.inf, gate)
        qs = (q * (LOG2E * D ** -0.5)).astype(BF16)
        return (jnp.concatenate([qs, jnp.zeros((t, LANES), BF16)], axis=-1),
                jnp.concatenate([qs, sel_bias.astype(BF16)], axis=-1))

    _attend_tile_pair(i, [prep(hd, c) for hd in range(ATT_HEADS) for c in range(2)],
                      ka_ref, vt_ref, m_ref, acc_ref, o_ref)


def moba_attention(z, cos, sin_signed, q_gain, k_gain):
    B, S, W = z.shape
    H = W // (3 * HEAD_DIM)
    assert ATT_BLOCK == MOBA_BLOCK
    t = 2 * ATT_BLOCK
    w = ATT_HEADS * HEAD_DIM
    hp = H // ATT_HEADS
    return pl.pallas_call(
        functools.partial(_moba_kernel, S=S),
        grid=(B, hp, S // t),
        in_specs=[
            pl.BlockSpec((None, t, w), lambda b, h, i: (b, i, h)),
            pl.BlockSpec((None, S, w), lambda b, h, i: (b, 0, hp + h)),
            pl.BlockSpec((None, S, w), lambda b, h, i: (b, 0, 2 * hp + h)),
            pl.BlockSpec((S, HEAD_DIM), lambda b, h, i: (0, 0)),
            pl.BlockSpec((S, HEAD_DIM), lambda b, h, i: (0, 0)),
            pl.BlockSpec((1, HEAD_DIM), lambda b, h, i: (0, 0)),
            pl.BlockSpec((1, HEAD_DIM), lambda b, h, i: (0, 0)),
        ],
        out_specs=pl.BlockSpec((None, t, w), lambda b, h, i: (b, i, h)),
        out_shape=jax.ShapeDtypeStruct((B, S, H * HEAD_DIM), BF16),
        scratch_shapes=_ATT_SCRATCH(S) + [pltpu.VMEM((ATT_HEADS, LANES, HEAD_DIM), F32)],
        compiler_params=_params("parallel", "parallel", "arbitrary"),
        name="moba_attention",
    )(z, z, z, cos, sin_signed, q_gain.reshape(1, -1), k_gain.reshape(1, -1))


META_IDX, META_RANK, META_GATE = 0, 2, 4


def _router_kernel(x_ref, *refs, n_proj):
    proj_refs = refs[:2 * n_proj]
    g_ref, w_ref, y_ref, xn_ref, meta_ref, cnt_ref, carry_ref = refs[2 * n_proj:]
    i = pl.program_id(0)
    tm = x_ref.shape[0]

    @pl.when(i == 0)
    def _():
        carry_ref[...] = jnp.zeros(carry_ref.shape, F32)

    y = _plus_projections(x_ref[...], proj_refs)
    y_ref[...] = y
    xn = _rms(y, g_ref[...])
    xn_ref[...] = _pack_bf16_pairs(xn)
    w = w_ref[...]
    w_hi = w.astype(BF16)
    w_lo = (w - w_hi.astype(F32)).astype(BF16)
    x_hi = xn.astype(BF16)
    x_lo = (xn - x_hi.astype(F32)).astype(BF16)
    logits = (jnp.dot(x_hi, w_hi, preferred_element_type=F32) + jnp.dot(x_hi, w_lo, preferred_element_type=F32)
              + jnp.dot(x_lo, w_hi, preferred_element_type=F32))
    lane = lax.broadcasted_iota(jnp.int32, logits.shape, 1)
    logits = jnp.where(lane < N_EXPERTS, logits, -jnp.inf)
    top1 = jnp.max(logits, axis=-1, keepdims=True)
    lane_f = lane.astype(F32)
    idx1 = jnp.min(jnp.where(logits == top1, lane_f, float(LANES)), axis=-1, keepdims=True)
    rest = jnp.where(lane_f == idx1, -jnp.inf, logits)
    top2 = jnp.max(rest, axis=-1, keepdims=True)
    idx2 = jnp.min(jnp.where(rest == top2, lane_f, float(LANES)), axis=-1, keepdims=True)
    e2 = jnp.exp(top2 - top1)
    denom = 1.0 + e2
    chosen = jnp.where((lane_f == idx1) | (lane_f == idx2), 1.0, 0.0)
    r = lax.broadcasted_iota(jnp.int32, (tm, tm), 0)
    c = lax.broadcasted_iota(jnp.int32, (tm, tm), 1)
    ahead = jnp.dot(jnp.where(r > c, 1.0, 0.0).astype(BF16), chosen.astype(BF16), preferred_element_type=F32)
    carry = carry_ref[...]
    rank = ahead + carry
    rank1 = jnp.sum(jnp.where(lane_f == idx1, rank, 0.0), axis=-1, keepdims=True)
    rank2 = jnp.sum(jnp.where(lane_f == idx2, rank, 0.0), axis=-1, keepdims=True)
    vals = (idx1, idx2, rank1, rank2, 1.0 / denom, e2 / denom)
    meta = jnp.zeros(logits.shape, F32)
    for n, v in enumerate(vals):
        meta = jnp.where(lane == n, v, meta)
    meta_ref[...] = meta
    carry = carry + jnp.sum(chosen, axis=0, keepdims=True)
    carry_ref[...] = carry
    cnt_ref[...] = jnp.broadcast_to(carry, cnt_ref.shape)


def moe_router(x, proj, gain, w_router, *, tm):
    M, K = x.shape
    w = jnp.zeros((K, LANES), F32).at[:, :N_EXPERTS].set(w_router)
    proj_specs, proj_args = _projection_specs(proj, tm)
    return pl.pallas_call(
        functools.partial(_router_kernel, n_proj=len(proj)),
        grid=(M // tm,),
        in_specs=[
            pl.BlockSpec((tm, K), lambda i: (i, 0)),
            *proj_specs,
            pl.BlockSpec((1, K), lambda i: (0, 0)),
            pl.BlockSpec((K, LANES), lambda i: (0, 0)),
        ],
        out_specs=[
            pl.BlockSpec((tm, K), lambda i: (i, 0)),
            pl.BlockSpec((tm, K // 2), lambda i: (i, 0)),
            pl.BlockSpec((tm, LANES), lambda i: (i, 0)),
            pl.BlockSpec((8, LANES), lambda i: (0, 0)),
        ],
        out_shape=[
            jax.ShapeDtypeStruct((M, K), F32),
            jax.ShapeDtypeStruct((M, K // 2), jnp.int32),
            jax.ShapeDtypeStruct((M, LANES), F32),
            jax.ShapeDtypeStruct((8, LANES), F32),
        ],
        scratch_shapes=[pltpu.VMEM((1, LANES), F32)],
        compiler_params=_params("arbitrary"),
        name="moe_router",
    )(x, *proj_args, gain.reshape(1, K), w)


def _sc_workers():
    info = plsc.get_sparse_core_info()
    return info.num_cores, info.num_cores * info.num_subcores


def scatter_rows(rows, dest, n_out):
    M, W = rows.shape
    nc, nw = _sc_workers()
    per_w = M // nw
    mesh = plsc.VectorSubcoreMesh(core_axis_name="c", subcore_axis_name="s")

    @functools.partial(
        pl.kernel, mesh=mesh, out_type=jax.ShapeDtypeStruct((n_out, W), rows.dtype),
        scratch_types=[pltpu.VMEM((SC_ROWS,), jnp.int32), pltpu.VMEM((SC_ROWS, W), rows.dtype),
                       pltpu.SemaphoreType.DMA])
    def kern(rows_hbm, dest_hbm, out_hbm, idx_v, rows_v, sem):
        wid = lax.axis_index("s") * nc + lax.axis_index("c")

        @pl.loop(0, per_w // SC_ROWS)
        def _(g):
            base = wid * per_w + g * SC_ROWS
            pltpu.sync_copy(rows_hbm.at[pl.ds(base, SC_ROWS)], rows_v)
            for k in range(2):
                pltpu.sync_copy(dest_hbm.at[k, pl.ds(base, SC_ROWS)], idx_v)
                pltpu.async_copy(rows_v, out_hbm.at[idx_v], sem).wait()

    return kern(rows, dest)


def gather_rows(table, idx):
    N = idx.shape[0]
    W = table.shape[1]
    nc, nw = _sc_workers()
    per_w = N // nw
    mesh = plsc.VectorSubcoreMesh(core_axis_name="c", subcore_axis_name="s")

    @functools.partial(
        pl.kernel, mesh=mesh, out_type=jax.ShapeDtypeStruct((N, W), table.dtype),
        scratch_types=[pltpu.VMEM((SC_ROWS,), jnp.int32), pltpu.VMEM((SC_ROWS, W), table.dtype),
                       pltpu.SemaphoreType.DMA])
    def kern(table_hbm, idx_hbm, out_hbm, idx_v, rows_v, sem):
        wid = lax.axis_index("s") * nc + lax.axis_index("c")

        @pl.loop(0, per_w // SC_ROWS)
        def _(g):
            base = wid * per_w + g * SC_ROWS
            pltpu.sync_copy(idx_hbm.at[pl.ds(base, SC_ROWS)], idx_v)
            pltpu.async_copy(table_hbm.at[idx_v], rows_v, sem).wait()
            pltpu.sync_copy(rows_v, out_hbm.at[pl.ds(base, SC_ROWS)])

    return kern(table, idx)


def _combine_kernel(h_ref, y1_ref, y2_ref, meta_ref, o_ref):
    g1 = meta_ref[:, META_GATE:META_GATE + 1]
    g2 = meta_ref[:, META_GATE + 1:META_GATE + 2]
    o_ref[...] = (h_ref[...] + g1 * _unpack_bf16_pairs(y1_ref[...]).astype(F32)
                  + g2 * _unpack_bf16_pairs(y2_ref[...]).astype(F32))


def moe_combine(h, y_pairs, meta):
    M, K = h.shape
    p = PERM_TILE
    nt = M // p
    return pl.pallas_call(
        _combine_kernel,
        grid=(nt,),
        in_specs=[
            pl.BlockSpec((p, K), lambda i: (i, 0)),
            pl.BlockSpec((p, K // 2), lambda i: (i, 0)),
            pl.BlockSpec((p, K // 2), lambda i: (nt + i, 0)),
            pl.BlockSpec((p, LANES), lambda i: (i, 0)),
        ],
        out_specs=pl.BlockSpec((p, K), lambda i: (i, 0)),
        out_shape=jax.ShapeDtypeStruct((M, K), F32),
        compiler_params=_params("parallel"),
        name="moe_combine",
    )(h, y_pairs, y_pairs, meta)


def _expert_ffn_kernel(te_ref, tv_ref, x_ref, wg_ref, wu_ref, wd_ref, o_ref, xb_ref, acc_ref):
    del te_ref
    i = pl.program_id(0)
    j = pl.program_id(1)
    tm = x_ref.shape[0]
    valid = tv_ref[i]

    def swiglu_rows(rows):
        @pl.when(j == 0)
        def _():
            xb_ref[0:rows, :] = _unpack_bf16_pairs(x_ref[0:rows, :])

        x = xb_ref[0:rows, :]
        gt = jnp.dot(x, wg_ref[...], preferred_element_type=F32)
        up = jnp.dot(x, wu_ref[...], preferred_element_type=F32)
        act = (gt / (1.0 + jnp.exp(-gt)) * up).astype(BF16)
        part = jnp.dot(act, wd_ref[...], preferred_element_type=F32)

        @pl.when(j == 0)
        def _():
            acc_ref[0:rows, :] = part

        @pl.when(j > 0)
        def _():
            acc_ref[0:rows, :] += part

        @pl.when(j == pl.num_programs(1) - 1)
        def _():
            o_ref[0:rows, :] = _pack_bf16_pairs(acc_ref[0:rows, :])
            if rows < tm:
                o_ref[rows:tm, :] = jnp.zeros((tm - rows, o_ref.shape[1]), o_ref.dtype)

    @pl.when(valid > tm // 2)
    def _():
        swiglu_rows(tm)

    @pl.when((valid > 0) & (valid <= tm // 2))
    def _():
        swiglu_rows(tm // 2)

    @pl.when((valid == 0) & (j == 0))
    def _():
        o_ref[...] = jnp.zeros(o_ref.shape, o_ref.dtype)


def expert_ffn(xs, tile_expert, tile_valid, w_gate_up, w_down, *, tm, tf):
    R = xs.shape[0]
    E, F, K = w_down.shape
    nf = F // tf
    live = lambda i, tv: tv[i] > 0
    col = lambda i, j, tv: jnp.where(live(i, tv), j, nf - 1)
    grid_spec = pltpu.PrefetchScalarGridSpec(
        num_scalar_prefetch=2,
        grid=(R // tm, nf),
        in_specs=[
            pl.BlockSpec((tm, K // 2), lambda i, j, te, tv: (te[R // tm + i], 0)),
            pl.BlockSpec((None, K, tf), lambda i, j, te, tv: (te[i], 0, col(i, j, tv))),
            pl.BlockSpec((None, K, tf), lambda i, j, te, tv: (te[i], 0, nf + col(i, j, tv))),
            pl.BlockSpec((None, tf, K), lambda i, j, te, tv: (te[i], col(i, j, tv), 0)),
        ],
        out_specs=pl.BlockSpec((tm, K // 2), lambda i, j, te, tv: (i, 0)),
        scratch_shapes=[pltpu.VMEM((tm, K), BF16), pltpu.VMEM((tm, K), F32)],
    )
    return pl.pallas_call(
        _expert_ffn_kernel,
        grid_spec=grid_spec,
        out_shape=jax.ShapeDtypeStruct((R, K // 2), jnp.int32),
        compiler_params=_params("arbitrary", "arbitrary"),
        name="moe_expert_ffn",
    )(tile_expert, tile_valid, xs, w_gate_up, w_gate_up, w_down)


def moe_residual(h, proj, gain, w_router, w_gate_up, w_down):
    M, K = h.shape
    p = PERM_TILE
    tm = EXPERT_TILE
    n_rows = 2 * M + N_EXPERTS * tm
    h, xn, meta, cnt = moe_router(h, proj, gain, w_router, tm=p)

    counts = cnt[0, :N_EXPERTS].astype(jnp.int32)
    padded = (counts + tm - 1) // tm * tm
    ends = jnp.cumsum(padded)
    offsets = ends - padded
    n_tiles = n_rows // tm
    tile_row = jnp.arange(n_tiles) * tm
    n_used = ends[-1] // tm
    last_used = jnp.minimum(tile_row // tm, n_used - 1)
    expert_of = lambda row: jnp.minimum(jnp.sum(ends[None, :] <= row[:, None], axis=1), N_EXPERTS - 1)
    tile_expert = expert_of(last_used * tm)
    row_in_expert = tile_row - jnp.sum(jnp.where(tile_expert[:, None] == jnp.arange(N_EXPERTS), offsets, 0), axis=1)
    own_count = jnp.sum(jnp.where(tile_expert[:, None] == jnp.arange(N_EXPERTS), counts, 0), axis=1)
    tile_valid = jnp.where(tile_row // tm < n_used, jnp.clip(own_count - row_in_expert, 0, tm), 0)
    tile_tables = jnp.concatenate([tile_expert, last_used]).astype(jnp.int32)

    idx = meta[:, META_IDX:META_IDX + 2].astype(jnp.int32)
    rank = meta[:, META_RANK:META_RANK + 2].astype(jnp.int32)
    dest = (jnp.sum(jnp.where(idx[:, :, None] == jnp.arange(N_EXPERTS), offsets, 0), axis=-1) + rank).T

    xs = scatter_rows(xn, dest, n_rows)
    ys = expert_ffn(xs, tile_tables, tile_valid.astype(jnp.int32), w_gate_up, w_down, tm=tm, tf=896)
    return moe_combine(h, gather_rows(ys, dest.reshape(-1)), meta)


def _ffn_kernel(x_ref, *refs, n_proj):
    proj_refs = refs[:2 * n_proj]
    g_ref, wg_ref, wu_ref, wd_ref, o_ref, xn_ref, acc_ref = refs[2 * n_proj:]
    j = pl.program_id(1)

    @pl.when(j == 0)
    def _():
        x = _plus_projections(x_ref[...], proj_refs)
        xn_ref[...] = _rms(x, g_ref[...]).astype(BF16)
        acc_ref[...] = x

    xn = xn_ref[...]
    gt = jnp.dot(xn, wg_ref[...], preferred_element_type=F32)
    up = jnp.dot(xn, wu_ref[...], preferred_element_type=F32)
    act = gt / (1.0 + jnp.exp(-gt)) * up
    acc_ref[...] += jnp.dot(act.astype(BF16), wd_ref[...], preferred_element_type=F32)

    @pl.when(j == pl.num_programs(1) - 1)
    def _():
        o_ref[...] = acc_ref[...]


def ffn_residual(x, proj, gain, w_gate_up, w_down, *, tm, tf):
    M, K = x.shape
    F = w_down.shape[0]
    nf = F // tf
    proj_specs, proj_args = _projection_specs(proj, tm)
    return pl.pallas_call(
        functools.partial(_ffn_kernel, n_proj=len(proj)),
        grid=(M // tm, nf),
        in_specs=[
            pl.BlockSpec((tm, K), lambda i, j: (i, 0)),
            *proj_specs,
            pl.BlockSpec((1, K), lambda i, j: (0, 0)),
            pl.BlockSpec((K, tf), lambda i, j: (0, j)),
            pl.BlockSpec((K, tf), lambda i, j: (0, nf + j)),
            pl.BlockSpec((tf, K), lambda i, j: (j, 0)),
        ],
        out_specs=pl.BlockSpec((tm, K), lambda i, j: (i, 0)),
        out_shape=jax.ShapeDtypeStruct((M, K), F32),
        scratch_shapes=[pltpu.VMEM((tm, K), BF16), pltpu.VMEM((tm, K), F32)],
        compiler_params=_params("parallel", "arbitrary"),
        name="ffn_residual",
    )(x, *proj_args, gain.reshape(1, K), w_gate_up, w_gate_up, w_down)


def _even_mix(h, B, S, norm1, w_in, f_bias, q_norm, k_norm, conv_w, a_log, dt_bias, o_norm, w_out):
    M, D = h.shape
    fw, gw = FOX_HEADS * HEAD_DIM, GDN_HEADS * HEAD_DIM
    o_ff = 3 * fw
    o_gq = o_ff + FOX_HEADS
    o_ga = o_gq + 3 * gw
    o_gb = o_ga + GDN_HEADS
    o_gg = o_gb + GDN_HEADS
    w_big = jnp.concatenate([w_in[:, :o_ff], w_in[:, o_gq:o_ga], w_in[:, o_gg:]], axis=1)
    w_small = jnp.concatenate([w_in[:, o_ff:o_gq], w_in[:, o_ga:o_gg],
                               jnp.zeros((D, LANES - FOX_HEADS - 2 * GDN_HEADS), F32)], axis=1)
    w_big, w_small = (t.astype(BF16) for t in lax.optimization_barrier((w_big, w_small)))
    z, zs = norm_matmul(h, norm1, w_big, w_small, tm=ROW_TILE, tn=1792)
    par = jnp.zeros((8, LANES), F32)
    par = par.at[0, LANE_F:LANE_F + FOX_HEADS].set(f_bias).at[0, LANE_A:LANE_A + GDN_HEADS].set(dt_bias)
    par = par.at[1, LANE_A:LANE_A + GDN_HEADS].set(a_log)
    col, row = even_gates(zs.reshape(B, S, LANES), par)
    z = z.reshape(B, S, -1)
    fox = fox_attention(z, col, q_norm, k_norm)
    gdn = gdn_mixer(z, conv_w, col, row, o_norm)
    w_out = w_out.astype(BF16)
    return [(fox.reshape(M, fw), w_out[:fw]), (gdn.reshape(M, gw), w_out[fw:])]


def _odd_mix(h, B, S, norm1, w_qkv, q_norm, k_norm, w_out):
    M, D = h.shape
    z = norm_matmul(h, norm1, w_qkv.astype(BF16), tm=ROW_TILE, tn=1536).reshape(B, S, -1)
    half = HEAD_DIM // 2
    inv = jnp.power(ROPE_THETA, -jnp.arange(half, dtype=F32) / half)
    ang = jnp.arange(S, dtype=F32)[:, None] * inv[None, :]
    cos, sin = jnp.cos(ang), jnp.sin(ang)
    cos_full = jnp.concatenate([cos, cos], axis=-1)
    sin_signed = jnp.concatenate([-sin, sin], axis=-1)
    att = moba_attention(z, cos_full, sin_signed, q_norm, k_norm)
    return [(att.reshape(M, -1), w_out.astype(BF16))]


def kernel(x, e_norm1, e_w_in, e_fox_f_bias, e_fox_q_norm, e_fox_k_norm, e_gdn_conv, e_gdn_a_log,
           e_gdn_dt_bias, e_gdn_o_norm, e_w_out, e_norm2, e_ffn_w_gate_up, e_ffn_w_down,
           o_norm1, o_w_qkv, o_q_norm, o_k_norm, o_w_out, o_norm2, o_router, o_exp_w_gate_up, o_exp_w_down):
    B, S, D = x.shape
    h = x.reshape(B * S, D)
    depth = e_norm1.shape[0] + o_norm1.shape[0]
    for layer in range(depth):
        i = layer // 2
        if layer % 2 == 0:
            mix = _even_mix(h, B, S, e_norm1[i], e_w_in[i], e_fox_f_bias[i], e_fox_q_norm[i], e_fox_k_norm[i],
                            e_gdn_conv[i], e_gdn_a_log[i], e_gdn_dt_bias[i], e_gdn_o_norm[i], e_w_out[i])
            h = ffn_residual(h, mix, e_norm2[i], e_ffn_w_gate_up[i].astype(BF16), e_ffn_w_down[i].astype(BF16),
                             tm=ROW_TILE // 2, tf=1408)
        else:
            mix = _odd_mix(h, B, S, o_norm1[i], o_w_qkv[i], o_q_norm[i], o_k_norm[i], o_w_out[i])
            h = moe_residual(h, mix, o_norm2[i], o_router[i], o_exp_w_gate_up[i].astype(BF16),
                             o_exp_w_down[i].astype(BF16))
    return h.reshape(B, S, D)
```

```python
import functools

import jax
import jax.numpy as jnp
from jax import lax
from jax.experimental import pallas as pl
from jax.experimental.pallas import tpu as pltpu
from jax.experimental.pallas import tpu_sc as plsc

F32 = jnp.float32
BF16 = jnp.bfloat16

HEAD_DIM = 128
FOX_HEADS = 4
GDN_HEADS = 4
CONV_WIDTH = 4
MOBA_BLOCK = 256
MOBA_TOPK = 3
N_EXPERTS = 8
ROPE_THETA = 10000.0
EPS = 1e-6

LANES = 128
GDN_CHUNK = 128
INV_BLOCK = 16
GDN_HEADS_PER_STEP = 4
GDN_CHUNKS_PER_STEP = 4
NEG = -(2.0 ** 100)
LOG2E = 1.4426950408889634
ATT_BLOCK = 256
ATT_HEADS = 4
KEY_BLOCKS_PER_STEP = 2
V_ROWS = HEAD_DIM + 16
PERM_TILE = 1024
SC_ROWS = 64
ROW_TILE = 1024
MXU_COLS = 256
IN_PROJ_COLS = 7 * MXU_COLS
QKV_COLS = 6 * MXU_COLS
FFN_ROWS, FFN_COLS = 512, 1408
EXPERT_TILE = 1024
EXPERT_COLS = 896
VMEM_LIMIT_BYTES = 56 * 1024 * 1024

FOX_Q0, FOX_K0, FOX_V0 = 0, 4, 8
GDN_Q0, GDN_K0, GDN_V0, GDN_G0 = 12, 16, 20, 24
LANE_F, LANE_A, LANE_B = 0, 4, 8


def _params(*sem):
    return pltpu.CompilerParams(dimension_semantics=sem, vmem_limit_bytes=VMEM_LIMIT_BYTES)


def _rms(x, gain):
    return x * lax.rsqrt(jnp.mean(x * x, axis=-1, keepdims=True) + EPS) * gain


def _dot_nt(a, b, **kw):
    return lax.dot_general(a, b, (((1,), (1,)), ((), ())), preferred_element_type=F32, **kw)


def _pick_lane(x, lane_idx):
    lane = lax.broadcasted_iota(jnp.int32, x.shape, 1)
    return jnp.sum(jnp.where(lane == lane_idx, x, 0.0), axis=-1, keepdims=True)


def _pack_bf16_pairs(x):
    n = x.shape[1] // 2
    hi = pltpu.bitcast(x[:, :n].astype(BF16).astype(F32), jnp.uint32)
    lo = pltpu.bitcast(x[:, n:].astype(BF16).astype(F32), jnp.uint32)
    return pltpu.bitcast(hi | (lo >> 16), jnp.int32)


def _unpack_bf16_pairs(w):
    u = pltpu.bitcast(w, jnp.uint32)
    hi = pltpu.bitcast(u & jnp.uint32(0xFFFF0000), F32).astype(BF16)
    lo = pltpu.bitcast(u << 16, F32).astype(BF16)
    return jnp.concatenate([hi, lo], axis=1)


def _norm_mm_kernel(x_ref, g_ref, w_ref, *rest, has_aux):
    if has_aux:
        waux_ref, o_ref, oaux_ref, xn_ref = rest
    else:
        o_ref, xn_ref = rest

    @pl.when(pl.program_id(1) == 0)
    def _():
        xn = _rms(x_ref[...], g_ref[...]).astype(BF16)
        xn_ref[...] = xn
        if has_aux:
            oaux_ref[...] = jnp.dot(xn, waux_ref[...], preferred_element_type=F32)

    o_ref[...] = jnp.dot(xn_ref[...], w_ref[...], preferred_element_type=F32).astype(o_ref.dtype)


def norm_matmul(x, gain, w, w_aux=None, *, tm, tn, out_dtype=BF16):
    M, K = x.shape
    N = w.shape[1]
    has_aux = w_aux is not None
    in_specs = [
        pl.BlockSpec((tm, K), lambda i, j: (i, 0)),
        pl.BlockSpec((1, K), lambda i, j: (0, 0)),
        pl.BlockSpec((K, tn), lambda i, j: (0, j)),
    ]
    out_shape = [jax.ShapeDtypeStruct((M, N), out_dtype)]
    out_specs = [pl.BlockSpec((tm, tn), lambda i, j: (i, j))]
    args = [x, gain.reshape(1, K), w]
    if has_aux:
        in_specs.append(pl.BlockSpec((K, LANES), lambda i, j: (0, 0)))
        out_shape.append(jax.ShapeDtypeStruct((M, LANES), F32))
        out_specs.append(pl.BlockSpec((tm, LANES), lambda i, j: (i, 0)))
        args.append(w_aux)
    res = pl.pallas_call(
        functools.partial(_norm_mm_kernel, has_aux=has_aux),
        grid=(M // tm, N // tn),
        in_specs=in_specs,
        out_specs=out_specs,
        out_shape=out_shape,
        scratch_shapes=[pltpu.VMEM((tm, K), BF16)],
        compiler_params=_params("parallel", "arbitrary"),
        name="norm_matmul",
    )(*args)
    return res if has_aux else res[0]


def _plus_projections(x, proj_refs):
    for a_ref, w_ref in zip(proj_refs[0::2], proj_refs[1::2]):
        x = x + jnp.dot(a_ref[...], w_ref[...], preferred_element_type=F32)
    return x


def _projection_specs(pairs, tm):
    specs, args = [], []
    for a, w in pairs:
        specs.append(pl.BlockSpec((tm, a.shape[1]), lambda i, *_: (i, 0)))
        specs.append(pl.BlockSpec(w.shape, lambda i, *_: (0, 0)))
        args += [a, w]
    return specs, args


def _gate_kernel(zs_ref, par_ref, col_ref, row_ref, *, S, B):
    C = GDN_CHUNK
    bias = par_ref[0:1, :]
    neg_a = -jnp.exp(par_ref[1:2, :])
    r = lax.broadcasted_iota(jnp.int32, (C, C), 0)
    c = lax.broadcasted_iota(jnp.int32, (C, C), 1)
    tril = (r >= c).astype(F32)
    lane = lax.broadcasted_iota(jnp.int32, (C, LANES), 1)

    def body(n, carry):
        sl = pl.ds(pl.multiple_of(n * C, C), C)
        us, betas = [], []
        for bi in range(B):
            z = zs_ref[bi, sl, :]
            t = z + bias
            soft = jnp.log(1.0 + jnp.exp(-jnp.abs(t)))
            log_f = jnp.minimum(t, 0.0) - soft
            g = neg_a * (jnp.maximum(t, 0.0) + soft)
            betas.append(1.0 / (1.0 + jnp.exp(-z)))
            us.append(jnp.where(lane < LANE_A, log_f, jnp.where(lane < LANE_B, g, 0.0)))
        sums = jnp.dot(tril, jnp.concatenate(us, axis=1), preferred_element_type=F32, precision=lax.Precision.HIGHEST)
        last = []
        for bi in range(B):
            cs = sums[:, bi * LANES:(bi + 1) * LANES] + jnp.where(lane < LANE_A, carry[bi], 0.0)
            out = jnp.where(lane < LANE_B, cs, betas[bi])
            col_ref[bi, sl, :] = out
            out_t = out.T
            for hd in range(GDN_HEADS):
                row_ref[bi, hd, :, sl] = out_t[LANE_A + hd:LANE_A + hd + 1, :]
            last.append(cs[C - 1:C, :])
        return tuple(last)

    lax.fori_loop(0, S // C, body, tuple(jnp.zeros((1, LANES), F32) for _ in range(B)))


def even_gates(zs, par):
    B, S, _ = zs.shape
    return pl.pallas_call(
        functools.partial(_gate_kernel, S=S, B=B),
        grid=(1,),
        in_specs=[
            pl.BlockSpec((B, S, LANES), lambda i: (0, 0, 0)),
            pl.BlockSpec((8, LANES), lambda i: (0, 0)),
        ],
        out_specs=[
            pl.BlockSpec((B, S, LANES), lambda i: (0, 0, 0)),
            pl.BlockSpec((B, GDN_HEADS, 1, S), lambda i: (0, 0, 0, 0)),
        ],
        out_shape=[jax.ShapeDtypeStruct((B, S, LANES), F32), jax.ShapeDtypeStruct((B, GDN_HEADS, 1, S), F32)],
        compiler_params=_params("arbitrary"),
        name="even_gates",
    )(zs, par)


def _split3(x):
    hi = x.astype(BF16).astype(F32)
    mid = (x - hi).astype(BF16).astype(F32)
    lo = (x - hi - mid).astype(BF16).astype(F32)
    return hi, mid, lo


def _interleave(gens):
    out = [None] * len(gens)
    live = list(range(len(gens)))
    while live:
        for n in list(live):
            try:
                next(gens[n])
            except StopIteration as stop:
                out[n] = stop.value
                live.remove(n)
    return out


def _attend(state, c, qa, ka_ref, vt_ref, key0, nkeys, keep=None):
    ks = pl.ds(pl.multiple_of(key0, ATT_BLOCK), nkeys)
    st = _dot_nt(ka_ref[ks, :], qa)
    yield
    if keep is not None:
        st = jnp.where(keep, st, NEG)
    m_new = jnp.max(st, axis=0, keepdims=True)
    if state[c] is not None:
        m_old, acc_old = state[c]
        m_new = jnp.maximum(m_old, m_new)
    p = jnp.exp2(st - m_new).astype(BF16)
    pv = jnp.dot(vt_ref[:, ks], p, preferred_element_type=F32)
    state[c] = (m_new, pv if state[c] is None else acc_old * jnp.exp2(m_old - m_new) + pv)


def _attend_tile_pair(i, prep, ka_ref, vt_ref, m_ref, acc_ref, o_ref):
    t = ATT_BLOCK
    chains = [(hd, c) for c in range(2) for hd in range(ATT_HEADS)]
    num = lambda hd, c: 2 * hd + c
    qa = _interleave(prep)

    def step(state, hd, c, which, **kw):
        return _attend(state, num(hd, c), qa[num(hd, c)][which], ka_ref.at[hd], vt_ref.at[hd], **kw)

    def save(state):
        m_ref[...] = jnp.stack([state[n][0] for n in range(len(chains))])
        acc_ref[...] = jnp.stack([state[n][1] for n in range(len(chains))])

    def past_blocks(key0, n_steps):
        state = {n: (m_ref[n], acc_ref[n]) for n in range(len(chains))}
        _interleave([step(state, hd, c, 1, key0=key0 + s * 2 * t, nkeys=2 * t)
                     for s in range(n_steps) for hd, c in chains])
        save(state)

    state = {n: None for n in range(len(chains))}
    keep = _causal_keep()
    _interleave([step(state, hd, c, 0, key0=(2 * i + c) * t, nkeys=t, keep=keep) for hd, c in chains]
                + [step(state, hd, 1, 1, key0=(2 * i) * t, nkeys=t) for hd in range(ATT_HEADS)])
    save(state)

    def four_blocks(g, _):
        past_blocks(g * (4 * t), 2)
        return 0

    lax.fori_loop(0, i // 2, four_blocks, 0)

    @pl.when((i & 1) != 0)
    def _():
        past_blocks((i // 2) * (4 * t), 1)

    for hd, c in chains:
        acc = acc_ref[num(hd, c)]
        out_t = acc[:HEAD_DIM, :] * (1.0 / acc[HEAD_DIM:HEAD_DIM + 1, :])
        o_ref[c * t:(c + 1) * t, hd * HEAD_DIM:(hd + 1) * HEAD_DIM] = out_t.T.astype(o_ref.dtype)


def _transposed(v):
    r = lax.broadcasted_iota(jnp.int32, (HEAD_DIM, HEAD_DIM), 0)
    c = lax.broadcasted_iota(jnp.int32, (HEAD_DIM, HEAD_DIM), 1)
    v_t = _dot_nt(jnp.where(r == c, 1.0, 0.0).astype(BF16), v).astype(BF16)
    return jnp.concatenate([v_t, jnp.ones((V_ROWS - HEAD_DIM, ATT_BLOCK), BF16)], axis=0)


def _causal_keep():
    key = lax.broadcasted_iota(jnp.int32, (ATT_BLOCK, ATT_BLOCK), 0)
    qry = lax.broadcasted_iota(jnp.int32, (ATT_BLOCK, ATT_BLOCK), 1)
    return key <= qry


_ATT_SCRATCH = lambda S: [pltpu.VMEM((ATT_HEADS, S, 2 * HEAD_DIM), BF16), pltpu.VMEM((ATT_HEADS, V_ROWS, S), BF16),
                          pltpu.VMEM((2 * ATT_HEADS, 1, ATT_BLOCK), F32),
                          pltpu.VMEM((2 * ATT_HEADS, V_ROWS, ATT_BLOCK), F32)]


def _fox_kernel(q_ref, k_ref, v_ref, col_ref, qg_ref, kg_ref, o_ref, ka_ref, vt_ref, m_ref, acc_ref, *, S):
    h0 = pl.program_id(1) * ATT_HEADS
    i = pl.program_id(2)
    t = ATT_BLOCK
    D = HEAD_DIM
    lane = lax.broadcasted_iota(jnp.int32, (t, LANES), 1)

    @pl.when(i == 0)
    def _():
        def prep_keys(g, _):
            todo = []
            for u in range(KEY_BLOCKS_PER_STEP):
                sl = pl.ds(pl.multiple_of((g * KEY_BLOCKS_PER_STEP + u) * t, t), t)
                for hd in range(ATT_HEADS):
                    cols = slice(hd * D, (hd + 1) * D)
                    kn = _rms(k_ref[sl, cols].astype(F32), kg_ref[...]).astype(BF16)
                    hi, mid, lo = _split3(-LOG2E * _pick_lane(col_ref[sl, :], LANE_F + h0 + hd))
                    aug = jnp.where(lane < 3, 1.0, jnp.where(lane == 3, hi, jnp.where(lane == 4, mid,
                                                                                    jnp.where(lane == 5, lo, 0.0))))
                    todo.append((hd, sl, jnp.concatenate([kn, aug.astype(BF16)], axis=1), _transposed(v_ref[sl, cols])))
            for hd, sl, ka, vt in todo:
                ka_ref[hd, sl, :] = ka
                vt_ref[hd, :, sl] = vt
            return 0

        lax.fori_loop(0, S // (t * KEY_BLOCKS_PER_STEP), prep_keys, 0)

    def prep(hd, c):
        q = q_ref[c * t:(c + 1) * t, hd * D:(hd + 1) * D].astype(F32)
        qn = (_rms(q, qg_ref[...]) * (LOG2E * D ** -0.5)).astype(BF16)
        yield
        qsl = pl.ds(pl.multiple_of((2 * i + c) * t, t), t)
        hi, mid, lo = _split3(LOG2E * _pick_lane(col_ref[qsl, :], LANE_F + h0 + hd))
        aug = jnp.where(lane == 0, hi,
                        jnp.where(lane == 1, mid, jnp.where(lane == 2, lo, jnp.where(lane < 6, 1.0, 0.0))))
        qa = jnp.concatenate([qn, aug.astype(BF16)], axis=-1)
        return qa, qa

    _attend_tile_pair(i, [prep(hd, c) for hd in range(ATT_HEADS) for c in range(2)],
                      ka_ref, vt_ref, m_ref, acc_ref, o_ref)


def fox_attention(z, col, q_gain, k_gain):
    B, S, _ = z.shape
    t = 2 * ATT_BLOCK
    w = ATT_HEADS * HEAD_DIM
    return pl.pallas_call(
        functools.partial(_fox_kernel, S=S),
        grid=(B, FOX_HEADS // ATT_HEADS, S // t),
        in_specs=[
            pl.BlockSpec((None, t, w), lambda b, h, i: (b, i, FOX_Q0 // ATT_HEADS + h)),
            pl.BlockSpec((None, S, w), lambda b, h, i: (b, 0, FOX_K0 // ATT_HEADS + h)),
            pl.BlockSpec((None, S, w), lambda b, h, i: (b, 0, FOX_V0 // ATT_HEADS + h)),
            pl.BlockSpec((None, S, LANES), lambda b, h, i: (b, 0, 0)),
            pl.BlockSpec((1, HEAD_DIM), lambda b, h, i: (0, 0)),
            pl.BlockSpec((1, HEAD_DIM), lambda b, h, i: (0, 0)),
        ],
        out_specs=pl.BlockSpec((None, t, w), lambda b, h, i: (b, i, h)),
        out_shape=jax.ShapeDtypeStruct((B, S, FOX_HEADS * HEAD_DIM), BF16),
        scratch_shapes=_ATT_SCRATCH(S),
        compiler_params=_params("parallel", "parallel", "arbitrary"),
        name="fox_attention",
    )(z, z, z, col, q_gain.reshape(1, -1), k_gain.reshape(1, -1))


def _unit_lower_inverse(m):
    n = m.shape[0]
    r = lax.broadcasted_iota(jnp.int32, (n, n), 0)
    c = lax.broadcasted_iota(jnp.int32, (n, n), 1)
    eye = (r == c).astype(F32)

    def same_block(b):
        return (r // b) == (c // b)

    p = jnp.where(same_block(INV_BLOCK), m, 0.0)
    inv = eye - p
    k = 2
    while k < INV_BLOCK:
        pb = p.astype(BF16)
        p = jnp.dot(pb, pb, preferred_element_type=F32)
        yield
        inv = jnp.dot(inv.astype(BF16), (eye + p).astype(BF16), preferred_element_type=F32)
        yield
        k *= 2
    b = INV_BLOCK
    while b < n:
        off = jnp.where(same_block(2 * b), jnp.where(same_block(b), 0.0, m), 0.0).astype(BF16)
        ib = inv.astype(BF16)
        left = jnp.dot(ib, off, preferred_element_type=F32).astype(BF16)
        yield
        inv = inv - jnp.dot(left, ib, preferred_element_type=F32)
        yield
        b *= 2
    return inv


def _gdn_kernel(q_ref, k_ref, v_ref, gg_ref, wq_ref, wk_ref, wv_ref, col_ref, row_ref, on_ref,
                o_ref, pad_ref, qs_ref, ks_ref, vs_ref, *, S, rows, hb):
    h0 = pl.program_id(1) * hb
    C = GDN_CHUNK
    D = HEAD_DIM

    def conv_silu(x_ref, w_ref, dst_ref, hh, mode):
        cols = slice(hh * D, (hh + 1) * D)
        pad_ref[0:8, :] = jnp.zeros((8, D), F32)

        def fill(n, _):
            src = pl.ds(pl.multiple_of(n * rows, rows), rows)
            pad_ref[pl.ds(pl.multiple_of(8 + n * rows, 8), rows), :] = x_ref[src, cols].astype(F32)
            return 0

        lax.fori_loop(0, S // rows, fill, 0)
        w = w_ref[:, cols]

        def conv(n, _):
            base = pl.multiple_of(n * rows, rows)
            win = pad_ref[pl.ds(base, rows + 8), :]
            y = jnp.zeros((rows, D), F32)
            for tap in range(CONV_WIDTH):
                lead = 8 - (CONV_WIDTH - 1) + tap
                y = y + w[tap:tap + 1, :] * pltpu.roll(win, rows + 8 - lead, 0)[0:rows, :]
            y = y / (1.0 + jnp.exp(-y))
            if mode != "v":
                y = y * lax.rsqrt(jnp.sum(y * y, axis=-1, keepdims=True) + EPS)
            if mode == "q":
                y = y * D ** -0.5
            dst_ref[hh, pl.ds(base, rows), :] = y.astype(dst_ref.dtype)
            return 0

        lax.fori_loop(0, S // rows, conv, 0)

    for hh in range(hb):
        conv_silu(q_ref, wq_ref, qs_ref, hh, "q")
        conv_silu(k_ref, wk_ref, ks_ref, hh, "k")
        conv_silu(v_ref, wv_ref, vs_ref, hh, "v")

    r = lax.broadcasted_iota(jnp.int32, (C, C), 0)
    c = lax.broadcasted_iota(jnp.int32, (C, C), 1)
    incl = r >= c
    strict = r > c

    def chunk_local(hh, sl, tab):
        q = qs_ref[hh, sl, :].astype(F32)
        k = ks_ref[hh, sl, :].astype(F32)
        v = vs_ref[hh, sl, :].astype(F32)
        gcol = _pick_lane(tab, LANE_A + h0 + hh)
        beta = _pick_lane(tab, LANE_B + h0 + hh)
        grow = row_ref[hh, :, sl]
        glast = gcol[C - 1:C, :]
        decay = jnp.where(incl, jnp.exp(jnp.where(incl, gcol - grow, 0.0)), 0.0)
        eg = jnp.exp(gcol)
        kb = k * beta
        kbf = k.astype(BF16)
        m = jnp.where(strict, _dot_nt(kb.astype(BF16), kbf) * decay, 0.0)
        attn = (_dot_nt(q.astype(BF16), kbf) * decay).astype(BF16)
        yield
        tinv = (yield from _unit_lower_inverse(m)).astype(BF16)
        rhs = jnp.concatenate([v * beta, kb * eg], axis=-1).astype(BF16)
        sol = jnp.dot(tinv, rhs, preferred_element_type=F32)
        yield
        gate = gg_ref[sl, hh * D:(hh + 1) * D].astype(F32)
        return dict(u=sol[:, :D], w=sol[:, D:].astype(BF16), attn=attn, qg=(q * eg).astype(BF16),
                    kg_t=(k * jnp.exp(glast - gcol)).T.astype(BF16), keep=jnp.exp(glast),
                    gate=gate / (1.0 + jnp.exp(-gate)))

    def chunk_state(parts, state):
        outs = []
        for c in parts:
            sb = state.astype(BF16)
            v_new = c["u"] - jnp.dot(c["w"], sb, preferred_element_type=F32)
            o_state = jnp.dot(c["qg"], sb, preferred_element_type=F32)
            yield
            vb = v_new.astype(BF16)
            o = o_state + jnp.dot(c["attn"], vb, preferred_element_type=F32)
            state = state * c["keep"] + jnp.dot(c["kg_t"], vb, preferred_element_type=F32)
            yield
            outs.append((_rms(o, on_ref[...]) * c["gate"]).astype(o_ref.dtype))
        return state, outs

    def chunk_group(n, states):
        sls = [pl.ds(pl.multiple_of((n * GDN_CHUNKS_PER_STEP + g) * C, C), C) for g in range(GDN_CHUNKS_PER_STEP)]
        tabs = [col_ref[sl, :] for sl in sls]
        parts = _interleave([chunk_local(hh, sls[g], tabs[g]) for g in range(GDN_CHUNKS_PER_STEP) for hh in range(hb)])
        res = _interleave([chunk_state([parts[g * hb + hh] for g in range(GDN_CHUNKS_PER_STEP)], states[hh])
                           for hh in range(hb)])
        for g in range(GDN_CHUNKS_PER_STEP):
            o_ref[sls[g], :] = jnp.concatenate([outs[g] for _, outs in res], axis=-1)
        return tuple(s for s, _ in res)

    lax.fori_loop(0, S // (C * GDN_CHUNKS_PER_STEP), chunk_group, tuple(jnp.zeros((D, D), F32) for _ in range(hb)))


def gdn_mixer(z, conv_w, col, row, o_gain):
    B, S, _ = z.shape
    D = HEAD_DIM
    hb = GDN_HEADS_PER_STEP
    rows = 256
    seq = lambda off: pl.BlockSpec((None, S, hb * D), lambda b, h: (b, 0, off // hb + h),
                                   pipeline_mode=pl.Buffered(1))
    cw = lambda off: pl.BlockSpec((CONV_WIDTH, hb * D), lambda b, h: (0, off // hb + h))
    return pl.pallas_call(
        functools.partial(_gdn_kernel, S=S, rows=rows, hb=hb),
        grid=(B, GDN_HEADS // hb),
        in_specs=[
            seq(GDN_Q0), seq(GDN_K0), seq(GDN_V0), seq(GDN_G0),
            cw(0), cw(GDN_HEADS), cw(2 * GDN_HEADS),
            pl.BlockSpec((None, S, LANES), lambda b, h: (b, 0, 0)),
            pl.BlockSpec((None, hb, 1, S), lambda b, h: (b, h, 0, 0)),
            pl.BlockSpec((1, D), lambda b, h: (0, 0)),
        ],
        out_specs=pl.BlockSpec((None, S, hb * D), lambda b, h: (b, 0, h)),
        out_shape=jax.ShapeDtypeStruct((B, S, GDN_HEADS * D), BF16),
        scratch_shapes=[pltpu.VMEM((S + 8, D), F32)] + [pltpu.VMEM((hb, S, D), BF16)] * 3,
        compiler_params=_params("parallel", "parallel"),
        name="gdn_mixer",
    )(z, z, z, z, conv_w, conv_w, conv_w, col, row, o_gain.reshape(1, D))


def _rope(x, cos, sin_signed):
    return x * cos + pltpu.roll(x, HEAD_DIM // 2, 1) * sin_signed


def _moba_kernel(q_ref, k_ref, v_ref, cos_ref, sin_ref, qg_ref, kg_ref, o_ref,
                 ka_ref, vt_ref, m_ref, acc_ref, kmean_ref, *, S):
    i = pl.program_id(2)
    t = ATT_BLOCK
    D = HEAD_DIM
    lane = lax.broadcasted_iota(jnp.int32, (t, LANES), 1)

    @pl.when(i == 0)
    def _():
        kmean_ref[...] = jnp.zeros(kmean_ref.shape, F32)

        def prep_keys(g, _):
            todo = []
            for u in range(KEY_BLOCKS_PER_STEP):
                n = g * KEY_BLOCKS_PER_STEP + u
                sl = pl.ds(pl.multiple_of(n * t, t), t)
                onehot = jnp.where(lane == n, 1.0, 0.0).astype(BF16)
                for hd in range(ATT_HEADS):
                    cols = slice(hd * D, (hd + 1) * D)
                    k = _rope(_rms(k_ref[sl, cols].astype(F32), kg_ref[...]), cos_ref[sl, :], sin_ref[sl, :])
                    todo.append((hd, n, sl, jnp.concatenate([k.astype(BF16), onehot], axis=1),
                                 jnp.mean(k, axis=0, keepdims=True), _transposed(v_ref[sl, cols])))
            for hd, n, sl, ka, kmean, vt in todo:
                ka_ref[hd, sl, :] = ka
                kmean_ref[hd, pl.ds(n, 1), :] = kmean
                vt_ref[hd, :, sl] = vt
            return 0

        lax.fori_loop(0, S // (t * KEY_BLOCKS_PER_STEP), prep_keys, 0)

    lane_f = lane.astype(F32)

    def prep(hd, c):
        cur = 2 * i + c
        qsl = pl.ds(pl.multiple_of(cur * t, t), t)
        q = _rope(_rms(q_ref[c * t:(c + 1) * t, hd * D:(hd + 1) * D].astype(F32), qg_ref[...]),
                  cos_ref[qsl, :], sin_ref[qsl, :])
        yield
        kmean = kmean_ref[hd]
        km_hi = kmean.astype(BF16)
        km_split = jnp.concatenate([km_hi, (kmean - km_hi.astype(F32)).astype(BF16)], axis=0)
        q_hi = q.astype(BF16)
        q_lo = (q - q_hi.astype(F32)).astype(BF16)
        part = _dot_nt(q_hi, km_split)
        gate = part[:, :LANES] + part[:, LANES:] + _dot_nt(q_lo, km_split[:LANES, :])
        gate = jnp.where(lane < cur, gate, -jnp.inf)
        yield
        sel_bias = jnp.full((t, LANES), NEG, F32)
        for _ in range(MOBA_TOPK):
            top = jnp.max(gate, axis=-1, keepdims=True)
            yield
            first = jnp.min(jnp.where(gate == top, lane_f, float(LANES)), axis=-1, keepdims=True)
            yield
            pick = lane_f == first
            sel_bias = jnp.where(pick & (first < cur.astype(F32)), 0.0, sel_bias)
            gate = jnp.where(pick, -jnp.inf, gate)
        qs = (q * (LOG2E * D ** -0.5)).astype(BF16)
        return (jnp.concatenate([qs, jnp.zeros((t, LANES), BF16)], axis=-1),
                jnp.concatenate([qs, sel_bias.astype(BF16)], axis=-1))

    _attend_tile_pair(i, [prep(hd, c) for hd in range(ATT_HEADS) for c in range(2)],
                      ka_ref, vt_ref, m_ref, acc_ref, o_ref)


def moba_attention(z, cos, sin_signed, q_gain, k_gain):
    B, S, W = z.shape
    H = W // (3 * HEAD_DIM)
    assert ATT_BLOCK == MOBA_BLOCK
    t = 2 * ATT_BLOCK
    w = ATT_HEADS * HEAD_DIM
    hp = H // ATT_HEADS
    return pl.pallas_call(
        functools.partial(_moba_kernel, S=S),
        grid=(B, hp, S // t),
        in_specs=[
            pl.BlockSpec((None, t, w), lambda b, h, i: (b, i, h)),
            pl.BlockSpec((None, S, w), lambda b, h, i: (b, 0, hp + h)),
            pl.BlockSpec((None, S, w), lambda b, h, i: (b, 0, 2 * hp + h)),
            pl.BlockSpec((S, HEAD_DIM), lambda b, h, i: (0, 0)),
            pl.BlockSpec((S, HEAD_DIM), lambda b, h, i: (0, 0)),
            pl.BlockSpec((1, HEAD_DIM), lambda b, h, i: (0, 0)),
            pl.BlockSpec((1, HEAD_DIM), lambda b, h, i: (0, 0)),
        ],
        out_specs=pl.BlockSpec((None, t, w), lambda b, h, i: (b, i, h)),
        out_shape=jax.ShapeDtypeStruct((B, S, H * HEAD_DIM), BF16),
        scratch_shapes=_ATT_SCRATCH(S) + [pltpu.VMEM((ATT_HEADS, LANES, HEAD_DIM), F32)],
        compiler_params=_params("parallel", "parallel", "arbitrary"),
        name="moba_attention",
    )(z, z, z, cos, sin_signed, q_gain.reshape(1, -1), k_gain.reshape(1, -1))


META_IDX, META_RANK, META_GATE = 0, 2, 4


def _router_kernel(x_ref, *refs, n_proj):
    proj_refs = refs[:2 * n_proj]
    g_ref, w_ref, y_ref, xn_ref, meta_ref, cnt_ref, carry_ref = refs[2 * n_proj:]
    i = pl.program_id(0)
    tm = x_ref.shape[0]

    @pl.when(i == 0)
    def _():
        carry_ref[...] = jnp.zeros(carry_ref.shape, F32)

    y = _plus_projections(x_ref[...], proj_refs)
    y_ref[...] = y
    xn = _rms(y, g_ref[...])
    xn_ref[...] = _pack_bf16_pairs(xn)
    w = w_ref[...]
    w_hi = w.astype(BF16)
    w_lo = (w - w_hi.astype(F32)).astype(BF16)
    x_hi = xn.astype(BF16)
    x_lo = (xn - x_hi.astype(F32)).astype(BF16)
    logits = (jnp.dot(x_hi, w_hi, preferred_element_type=F32) + jnp.dot(x_hi, w_lo, preferred_element_type=F32)
              + jnp.dot(x_lo, w_hi, preferred_element_type=F32))
    lane = lax.broadcasted_iota(jnp.int32, logits.shape, 1)
    logits = jnp.where(lane < N_EXPERTS, logits, -jnp.inf)
    top1 = jnp.max(logits, axis=-1, keepdims=True)
    lane_f = lane.astype(F32)
    idx1 = jnp.min(jnp.where(logits == top1, lane_f, float(LANES)), axis=-1, keepdims=True)
    rest = jnp.where(lane_f == idx1, -jnp.inf, logits)
    top2 = jnp.max(rest, axis=-1, keepdims=True)
    idx2 = jnp.min(jnp.where(rest == top2, lane_f, float(LANES)), axis=-1, keepdims=True)
    e2 = jnp.exp(top2 - top1)
    denom = 1.0 + e2
    chosen = jnp.where((lane_f == idx1) | (lane_f == idx2), 1.0, 0.0)
    r = lax.broadcasted_iota(jnp.int32, (tm, tm), 0)
    c = lax.broadcasted_iota(jnp.int32, (tm, tm), 1)
    ahead = jnp.dot(jnp.where(r > c, 1.0, 0.0).astype(BF16), chosen.astype(BF16), preferred_element_type=F32)
    carry = carry_ref[...]
    rank = ahead + carry
    rank1 = jnp.sum(jnp.where(lane_f == idx1, rank, 0.0), axis=-1, keepdims=True)
    rank2 = jnp.sum(jnp.where(lane_f == idx2, rank, 0.0), axis=-1, keepdims=True)
    vals = (idx1, idx2, rank1, rank2, 1.0 / denom, e2 / denom)
    meta = jnp.zeros(logits.shape, F32)
    for n, v in enumerate(vals):
        meta = jnp.where(lane == n, v, meta)
    meta_ref[...] = meta
    carry = carry + jnp.sum(chosen, axis=0, keepdims=True)
    carry_ref[...] = carry
    cnt_ref[...] = jnp.broadcast_to(carry, cnt_ref.shape)


def moe_router(x, proj, gain, w_router, *, tm):
    M, K = x.shape
    w = jnp.zeros((K, LANES), F32).at[:, :N_EXPERTS].set(w_router)
    proj_specs, proj_args = _projection_specs(proj, tm)
    return pl.pallas_call(
        functools.partial(_router_kernel, n_proj=len(proj)),
        grid=(M // tm,),
        in_specs=[
            pl.BlockSpec((tm, K), lambda i: (i, 0)),
            *proj_specs,
            pl.BlockSpec((1, K), lambda i: (0, 0)),
            pl.BlockSpec((K, LANES), lambda i: (0, 0)),
        ],
        out_specs=[
            pl.BlockSpec((tm, K), lambda i: (i, 0)),
            pl.BlockSpec((tm, K // 2), lambda i: (i, 0)),
            pl.BlockSpec((tm, LANES), lambda i: (i, 0)),
            pl.BlockSpec((8, LANES), lambda i: (0, 0)),
        ],
        out_shape=[
            jax.ShapeDtypeStruct((M, K), F32),
            jax.ShapeDtypeStruct((M, K // 2), jnp.int32),
            jax.ShapeDtypeStruct((M, LANES), F32),
            jax.ShapeDtypeStruct((8, LANES), F32),
        ],
        scratch_shapes=[pltpu.VMEM((1, LANES), F32)],
        compiler_params=_params("arbitrary"),
        name="moe_router",
    )(x, *proj_args, gain.reshape(1, K), w)


def _sc_workers():
    info = plsc.get_sparse_core_info()
    return info.num_cores, info.num_cores * info.num_subcores


def scatter_rows(rows, dest, n_out):
    M, W = rows.shape
    nc, nw = _sc_workers()
    assert M % (nw * SC_ROWS) == 0
    per_w = M // nw
    mesh = plsc.VectorSubcoreMesh(core_axis_name="c", subcore_axis_name="s")

    @functools.partial(
        pl.kernel, mesh=mesh, out_type=jax.ShapeDtypeStruct((n_out, W), rows.dtype),
        scratch_types=[pltpu.VMEM((SC_ROWS,), jnp.int32), pltpu.VMEM((SC_ROWS, W), rows.dtype),
                       pltpu.SemaphoreType.DMA])
    def kern(rows_hbm, dest_hbm, out_hbm, idx_v, rows_v, sem):
        wid = lax.axis_index("s") * nc + lax.axis_index("c")

        @pl.loop(0, per_w // SC_ROWS)
        def _(g):
            base = wid * per_w + g * SC_ROWS
            pltpu.sync_copy(rows_hbm.at[pl.ds(base, SC_ROWS)], rows_v)
            for k in range(2):
                pltpu.sync_copy(dest_hbm.at[k, pl.ds(base, SC_ROWS)], idx_v)
                pltpu.async_copy(rows_v, out_hbm.at[idx_v], sem).wait()

    return kern(rows, dest)


def gather_rows(table, idx):
    N = idx.shape[0]
    W = table.shape[1]
    nc, nw = _sc_workers()
    assert N % (nw * SC_ROWS) == 0
    per_w = N // nw
    mesh = plsc.VectorSubcoreMesh(core_axis_name="c", subcore_axis_name="s")

    @functools.partial(
        pl.kernel, mesh=mesh, out_type=jax.ShapeDtypeStruct((N, W), table.dtype),
        scratch_types=[pltpu.VMEM((SC_ROWS,), jnp.int32), pltpu.VMEM((SC_ROWS, W), table.dtype),
                       pltpu.SemaphoreType.DMA])
    def kern(table_hbm, idx_hbm, out_hbm, idx_v, rows_v, sem):
        wid = lax.axis_index("s") * nc + lax.axis_index("c")

        @pl.loop(0, per_w // SC_ROWS)
        def _(g):
            base = wid * per_w + g * SC_ROWS
            pltpu.sync_copy(idx_hbm.at[pl.ds(base, SC_ROWS)], idx_v)
            pltpu.async_copy(table_hbm.at[idx_v], rows_v, sem).wait()
            pltpu.sync_copy(rows_v, out_hbm.at[pl.ds(base, SC_ROWS)])

    return kern(table, idx)


def _combine_kernel(h_ref, y1_ref, y2_ref, meta_ref, o_ref):
    g1 = meta_ref[:, META_GATE:META_GATE + 1]
    g2 = meta_ref[:, META_GATE + 1:META_GATE + 2]
    o_ref[...] = (h_ref[...] + g1 * _unpack_bf16_pairs(y1_ref[...]).astype(F32)
                  + g2 * _unpack_bf16_pairs(y2_ref[...]).astype(F32))


def moe_combine(h, y_pairs, meta):
    M, K = h.shape
    p = PERM_TILE
    nt = M // p
    return pl.pallas_call(
        _combine_kernel,
        grid=(nt,),
        in_specs=[
            pl.BlockSpec((p, K), lambda i: (i, 0)),
            pl.BlockSpec((p, K // 2), lambda i: (i, 0)),
            pl.BlockSpec((p, K // 2), lambda i: (nt + i, 0)),
            pl.BlockSpec((p, LANES), lambda i: (i, 0)),
        ],
        out_specs=pl.BlockSpec((p, K), lambda i: (i, 0)),
        out_shape=jax.ShapeDtypeStruct((M, K), F32),
        compiler_params=_params("parallel"),
        name="moe_combine",
    )(h, y_pairs, y_pairs, meta)


def _expert_ffn_kernel(te_ref, tv_ref, x_ref, wg_ref, wu_ref, wd_ref, o_ref, xb_ref, acc_ref):
    del te_ref
    i = pl.program_id(0)
    j = pl.program_id(1)
    tm = x_ref.shape[0]
    valid = tv_ref[i]

    def swiglu_rows(rows):
        @pl.when(j == 0)
        def _():
            xb_ref[0:rows, :] = _unpack_bf16_pairs(x_ref[0:rows, :])

        x = xb_ref[0:rows, :]
        gt = jnp.dot(x, wg_ref[...], preferred_element_type=F32)
        up = jnp.dot(x, wu_ref[...], preferred_element_type=F32)
        act = (gt / (1.0 + jnp.exp(-gt)) * up).astype(BF16)
        part = jnp.dot(act, wd_ref[...], preferred_element_type=F32)

        @pl.when(j == 0)
        def _():
            acc_ref[0:rows, :] = part

        @pl.when(j > 0)
        def _():
            acc_ref[0:rows, :] += part

        @pl.when(j == pl.num_programs(1) - 1)
        def _():
            o_ref[0:rows, :] = _pack_bf16_pairs(acc_ref[0:rows, :])
            if rows < tm:
                o_ref[rows:tm, :] = jnp.zeros((tm - rows, o_ref.shape[1]), o_ref.dtype)

    @pl.when(valid > tm // 2)
    def _():
        swiglu_rows(tm)

    @pl.when((valid > 0) & (valid <= tm // 2))
    def _():
        swiglu_rows(tm // 2)

    @pl.when((valid == 0) & (j == 0))
    def _():
        o_ref[...] = jnp.zeros(o_ref.shape, o_ref.dtype)


def expert_ffn(xs, tile_expert, tile_valid, w_gate_up, w_down, *, tm, tf):
    R = xs.shape[0]
    E, F, K = w_down.shape
    nf = F // tf
    live = lambda i, tv: tv[i] > 0
    col = lambda i, j, tv: jnp.where(live(i, tv), j, nf - 1)
    grid_spec = pltpu.PrefetchScalarGridSpec(
        num_scalar_prefetch=2,
        grid=(R // tm, nf),
        in_specs=[
            pl.BlockSpec((tm, K // 2), lambda i, j, te, tv: (te[R // tm + i], 0)),
            pl.BlockSpec((None, K, tf), lambda i, j, te, tv: (te[i], 0, col(i, j, tv))),
            pl.BlockSpec((None, K, tf), lambda i, j, te, tv: (te[i], 0, nf + col(i, j, tv))),
            pl.BlockSpec((None, tf, K), lambda i, j, te, tv: (te[i], col(i, j, tv), 0)),
        ],
        out_specs=pl.BlockSpec((tm, K // 2), lambda i, j, te, tv: (i, 0)),
        scratch_shapes=[pltpu.VMEM((tm, K), BF16), pltpu.VMEM((tm, K), F32)],
    )
    return pl.pallas_call(
        _expert_ffn_kernel,
        grid_spec=grid_spec,
        out_shape=jax.ShapeDtypeStruct((R, K // 2), jnp.int32),
        compiler_params=_params("arbitrary", "arbitrary"),
        name="moe_expert_ffn",
    )(tile_expert, tile_valid, xs, w_gate_up, w_gate_up, w_down)


def moe_residual(h, proj, gain, w_router, w_gate_up, w_down):
    M, K = h.shape
    p = PERM_TILE
    tm = EXPERT_TILE
    n_rows = 2 * M + N_EXPERTS * tm
    h, xn, meta, cnt = moe_router(h, proj, gain, w_router, tm=p)

    counts = cnt[0, :N_EXPERTS].astype(jnp.int32)
    padded = (counts + tm - 1) // tm * tm
    ends = jnp.cumsum(padded)
    offsets = ends - padded
    n_tiles = n_rows // tm
    tile_row = jnp.arange(n_tiles) * tm
    n_used = ends[-1] // tm
    last_used = jnp.minimum(tile_row // tm, n_used - 1)
    expert_of = lambda row: jnp.minimum(jnp.sum(ends[None, :] <= row[:, None], axis=1), N_EXPERTS - 1)
    tile_expert = expert_of(last_used * tm)
    row_in_expert = tile_row - jnp.sum(jnp.where(tile_expert[:, None] == jnp.arange(N_EXPERTS), offsets, 0), axis=1)
    own_count = jnp.sum(jnp.where(tile_expert[:, None] == jnp.arange(N_EXPERTS), counts, 0), axis=1)
    tile_valid = jnp.where(tile_row // tm < n_used, jnp.clip(own_count - row_in_expert, 0, tm), 0)
    tile_tables = jnp.concatenate([tile_expert, last_used]).astype(jnp.int32)

    idx = meta[:, META_IDX:META_IDX + 2].astype(jnp.int32)
    rank = meta[:, META_RANK:META_RANK + 2].astype(jnp.int32)
    dest = (jnp.sum(jnp.where(idx[:, :, None] == jnp.arange(N_EXPERTS), offsets, 0), axis=-1) + rank).T

    xs = scatter_rows(xn, dest, n_rows)
    ys = expert_ffn(xs, tile_tables, tile_valid.astype(jnp.int32), w_gate_up, w_down, tm=tm, tf=EXPERT_COLS)
    return moe_combine(h, gather_rows(ys, dest.reshape(-1)), meta)


def _ffn_kernel(x_ref, *refs, n_proj):
    proj_refs = refs[:2 * n_proj]
    g_ref, wg_ref, wu_ref, wd_ref, o_ref, xn_ref, acc_ref = refs[2 * n_proj:]
    j = pl.program_id(1)

    @pl.when(j == 0)
    def _():
        x = _plus_projections(x_ref[...], proj_refs)
        xn_ref[...] = _rms(x, g_ref[...]).astype(BF16)
        acc_ref[...] = x

    xn = xn_ref[...]
    gt = jnp.dot(xn, wg_ref[...], preferred_element_type=F32)
    up = jnp.dot(xn, wu_ref[...], preferred_element_type=F32)
    act = gt / (1.0 + jnp.exp(-gt)) * up
    acc_ref[...] += jnp.dot(act.astype(BF16), wd_ref[...], preferred_element_type=F32)

    @pl.when(j == pl.num_programs(1) - 1)
    def _():
        o_ref[...] = acc_ref[...]


def ffn_residual(x, proj, gain, w_gate_up, w_down, *, tm, tf):
    M, K = x.shape
    F = w_down.shape[0]
    nf = F // tf
    proj_specs, proj_args = _projection_specs(proj, tm)
    return pl.pallas_call(
        functools.partial(_ffn_kernel, n_proj=len(proj)),
        grid=(M // tm, nf),
        in_specs=[
            pl.BlockSpec((tm, K), lambda i, j: (i, 0)),
            *proj_specs,
            pl.BlockSpec((1, K), lambda i, j: (0, 0)),
            pl.BlockSpec((K, tf), lambda i, j: (0, j)),
            pl.BlockSpec((K, tf), lambda i, j: (0, nf + j)),
            pl.BlockSpec((tf, K), lambda i, j: (j, 0)),
        ],
        out_specs=pl.BlockSpec((tm, K), lambda i, j: (i, 0)),
        out_shape=jax.ShapeDtypeStruct((M, K), F32),
        scratch_shapes=[pltpu.VMEM((tm, K), BF16), pltpu.VMEM((tm, K), F32)],
        compiler_params=_params("parallel", "arbitrary"),
        name="ffn_residual",
    )(x, *proj_args, gain.reshape(1, K), w_gate_up, w_gate_up, w_down)


def _even_mix(h, B, S, norm1, w_in, f_bias, q_norm, k_norm, conv_w, a_log, dt_bias, o_norm, w_out):
    M, D = h.shape
    fw, gw = FOX_HEADS * HEAD_DIM, GDN_HEADS * HEAD_DIM
    o_ff = 3 * fw
    o_gq = o_ff + FOX_HEADS
    o_ga = o_gq + 3 * gw
    o_gb = o_ga + GDN_HEADS
    o_gg = o_gb + GDN_HEADS
    w_big = jnp.concatenate([w_in[:, :o_ff], w_in[:, o_gq:o_ga], w_in[:, o_gg:]], axis=1)
    w_small = jnp.concatenate([w_in[:, o_ff:o_gq], w_in[:, o_ga:o_gg],
                               jnp.zeros((D, LANES - FOX_HEADS - 2 * GDN_HEADS), F32)], axis=1)
    w_big, w_small = (t.astype(BF16) for t in lax.optimization_barrier((w_big, w_small)))
    z, zs = norm_matmul(h, norm1, w_big, w_small, tm=ROW_TILE, tn=IN_PROJ_COLS)
    par = jnp.zeros((8, LANES), F32)
    par = par.at[0, LANE_F:LANE_F + FOX_HEADS].set(f_bias).at[0, LANE_A:LANE_A + GDN_HEADS].set(dt_bias)
    par = par.at[1, LANE_A:LANE_A + GDN_HEADS].set(a_log)
    col, row = even_gates(zs.reshape(B, S, LANES), par)
    z = z.reshape(B, S, -1)
    fox = fox_attention(z, col, q_norm, k_norm)
    gdn = gdn_mixer(z, conv_w, col, row, o_norm)
    w_out = w_out.astype(BF16)
    return [(fox.reshape(M, fw), w_out[:fw]), (gdn.reshape(M, gw), w_out[fw:])]


def _odd_mix(h, B, S, norm1, w_qkv, q_norm, k_norm, w_out):
    M, D = h.shape
    z = norm_matmul(h, norm1, w_qkv.astype(BF16), tm=ROW_TILE, tn=QKV_COLS).reshape(B, S, -1)
    half = HEAD_DIM // 2
    inv = jnp.power(ROPE_THETA, -jnp.arange(half, dtype=F32) / half)
    ang = jnp.arange(S, dtype=F32)[:, None] * inv[None, :]
    cos, sin = jnp.cos(ang), jnp.sin(ang)
    cos_full = jnp.concatenate([cos, cos], axis=-1)
    sin_signed = jnp.concatenate([-sin, sin], axis=-1)
    att = moba_attention(z, cos_full, sin_signed, q_norm, k_norm)
    return [(att.reshape(M, -1), w_out.astype(BF16))]


def kernel(x, e_norm1, e_w_in, e_fox_f_bias, e_fox_q_norm, e_fox_k_norm, e_gdn_conv, e_gdn_a_log,
           e_gdn_dt_bias, e_gdn_o_norm, e_w_out, e_norm2, e_ffn_w_gate_up, e_ffn_w_down,
           o_norm1, o_w_qkv, o_q_norm, o_k_norm, o_w_out, o_norm2, o_router, o_exp_w_gate_up, o_exp_w_down):
    B, S, D = x.shape
    h = x.reshape(B * S, D)
    depth = e_norm1.shape[0] + o_norm1.shape[0]
    for layer in range(depth):
        i = layer // 2
        if layer % 2 == 0:
            mix = _even_mix(h, B, S, e_norm1[i], e_w_in[i], e_fox_f_bias[i], e_fox_q_norm[i], e_fox_k_norm[i],
                            e_gdn_conv[i], e_gdn_a_log[i], e_gdn_dt_bias[i], e_gdn_o_norm[i], e_w_out[i])
            h = ffn_residual(h, mix, e_norm2[i], e_ffn_w_gate_up[i].astype(BF16), e_ffn_w_down[i].astype(BF16),
                             tm=FFN_ROWS, tf=FFN_COLS)
        else:
            mix = _odd_mix(h, B, S, o_norm1[i], o_w_qkv[i], o_q_norm[i], o_k_norm[i], o_w_out[i])
            h = moe_residual(h, mix, o_norm2[i], o_router[i], o_exp_w_gate_up[i].astype(BF16),
                             o_exp_w_down[i].astype(BF16))
    return h.reshape(B, S, D)
```

```python
import functools

import jax
import jax.numpy as jnp
from jax import lax
from jax.experimental import pallas as pl
from jax.experimental.pallas import tpu as pltpu
from jax.experimental.pallas import tpu_sc as plsc

F32 = jnp.float32
BF16 = jnp.bfloat16

HEAD_DIM = 128
FOX_HEADS = 4
GDN_HEADS = 4
CONV_WIDTH = 4
MOBA_BLOCK = 256
MOBA_TOPK = 3
N_EXPERTS = 8
ROPE_THETA = 10000.0
EPS = 1e-6

LANES = 128
GDN_CHUNK = 128
INV_BLOCK = 16
GDN_HEADS_PER_STEP = 4
GDN_CHUNKS_PER_STEP = 4
NEG = -(2.0 ** 100)
LOG2E = 1.4426950408889634
ATT_BLOCK = 256
ATT_HEADS = 4
KEY_BLOCKS_PER_STEP = 2
V_ROWS = HEAD_DIM + 16
PERM_TILE = 1024
ROUTER_CHAINS = 4
SC_ROWS = 64
ROW_TILE = 1024
MXU_COLS = 256
IN_PROJ_COLS = 7 * MXU_COLS
QKV_COLS = 6 * MXU_COLS
FFN_ROWS, FFN_COLS = 512, 1408
EXPERT_TILE = 1024
EXPERT_COLS = 896
VMEM_LIMIT_BYTES = 56 * 1024 * 1024

FOX_Q0, FOX_K0, FOX_V0 = 0, 4, 8
GDN_Q0, GDN_K0, GDN_V0, GDN_G0 = 12, 16, 20, 24
LANE_F, LANE_A, LANE_B = 0, 4, 8


def _params(*sem):
    return pltpu.CompilerParams(dimension_semantics=sem, vmem_limit_bytes=VMEM_LIMIT_BYTES)


def _rms(x, gain):
    return x * lax.rsqrt(jnp.mean(x * x, axis=-1, keepdims=True) + EPS) * gain


def _dot_nt(a, b, **kw):
    return lax.dot_general(a, b, (((1,), (1,)), ((), ())), preferred_element_type=F32, **kw)


def _pick_lane(x, lane_idx):
    lane = lax.broadcasted_iota(jnp.int32, x.shape, 1)
    return jnp.sum(jnp.where(lane == lane_idx, x, 0.0), axis=-1, keepdims=True)


def _pack_bf16_pairs(x):
    n = x.shape[1] // 2
    hi = pltpu.bitcast(x[:, :n].astype(BF16).astype(F32), jnp.uint32)
    lo = pltpu.bitcast(x[:, n:].astype(BF16).astype(F32), jnp.uint32)
    return pltpu.bitcast(hi | (lo >> 16), jnp.int32)


def _unpack_bf16_pairs(w):
    u = pltpu.bitcast(w, jnp.uint32)
    hi = pltpu.bitcast(u & jnp.uint32(0xFFFF0000), F32).astype(BF16)
    lo = pltpu.bitcast(u << 16, F32).astype(BF16)
    return jnp.concatenate([hi, lo], axis=1)


def _norm_mm_kernel(x_ref, g_ref, w_ref, *rest, has_aux):
    if has_aux:
        waux_ref, o_ref, oaux_ref, xn_ref = rest
    else:
        o_ref, xn_ref = rest

    @pl.when(pl.program_id(1) == 0)
    def _():
        xn = _rms(x_ref[...], g_ref[...]).astype(BF16)
        xn_ref[...] = xn
        if has_aux:
            oaux_ref[...] = jnp.dot(xn, waux_ref[...], preferred_element_type=F32)

    o_ref[...] = jnp.dot(xn_ref[...], w_ref[...], preferred_element_type=F32).astype(o_ref.dtype)


def norm_matmul(x, gain, w, w_aux=None, *, tm, tn, out_dtype=BF16):
    M, K = x.shape
    N = w.shape[1]
    has_aux = w_aux is not None
    in_specs = [
        pl.BlockSpec((tm, K), lambda i, j: (i, 0)),
        pl.BlockSpec((1, K), lambda i, j: (0, 0)),
        pl.BlockSpec((K, tn), lambda i, j: (0, j)),
    ]
    out_shape = [jax.ShapeDtypeStruct((M, N), out_dtype)]
    out_specs = [pl.BlockSpec((tm, tn), lambda i, j: (i, j))]
    args = [x, gain.reshape(1, K), w]
    if has_aux:
        in_specs.append(pl.BlockSpec((K, LANES), lambda i, j: (0, 0)))
        out_shape.append(jax.ShapeDtypeStruct((M, LANES), F32))
        out_specs.append(pl.BlockSpec((tm, LANES), lambda i, j: (i, 0)))
        args.append(w_aux)
    res = pl.pallas_call(
        functools.partial(_norm_mm_kernel, has_aux=has_aux),
        grid=(M // tm, N // tn),
        in_specs=in_specs,
        out_specs=out_specs,
        out_shape=out_shape,
        scratch_shapes=[pltpu.VMEM((tm, K), BF16)],
        compiler_params=_params("parallel", "arbitrary"),
        name="norm_matmul",
    )(*args)
    return res if has_aux else res[0]


def _plus_projections(x, proj_refs, rows=slice(None)):
    for a_ref, w_ref in zip(proj_refs[0::2], proj_refs[1::2]):
        x = x + jnp.dot(a_ref[rows, :], w_ref[...], preferred_element_type=F32)
    return x


def _projection_specs(pairs, tm):
    specs, args = [], []
    for a, w in pairs:
        specs.append(pl.BlockSpec((tm, a.shape[1]), lambda i, *_: (i, 0)))
        specs.append(pl.BlockSpec(w.shape, lambda i, *_: (0, 0)))
        args += [a, w]
    return specs, args


def _gate_kernel(zs_ref, par_ref, col_ref, row_ref, *, S, B):
    C = GDN_CHUNK
    bias = par_ref[0:1, :]
    neg_a = -jnp.exp(par_ref[1:2, :])
    r = lax.broadcasted_iota(jnp.int32, (C, C), 0)
    c = lax.broadcasted_iota(jnp.int32, (C, C), 1)
    tril = (r >= c).astype(F32)
    lane = lax.broadcasted_iota(jnp.int32, (C, LANES), 1)

    def body(n, carry):
        sl = pl.ds(pl.multiple_of(n * C, C), C)
        us, betas = [], []
        for bi in range(B):
            z = zs_ref[bi, sl, :]
            t = z + bias
            soft = jnp.log(1.0 + jnp.exp(-jnp.abs(t)))
            log_f = jnp.minimum(t, 0.0) - soft
            g = neg_a * (jnp.maximum(t, 0.0) + soft)
            betas.append(1.0 / (1.0 + jnp.exp(-z)))
            us.append(jnp.where(lane < LANE_A, log_f, jnp.where(lane < LANE_B, g, 0.0)))
        sums = jnp.dot(tril, jnp.concatenate(us, axis=1), preferred_element_type=F32, precision=lax.Precision.HIGHEST)
        last = []
        for bi in range(B):
            cs = sums[:, bi * LANES:(bi + 1) * LANES] + jnp.where(lane < LANE_A, carry[bi], 0.0)
            out = jnp.where(lane < LANE_B, cs, betas[bi])
            col_ref[bi, sl, :] = out
            out_t = out.T
            for hd in range(GDN_HEADS):
                row_ref[bi, hd, :, sl] = out_t[LANE_A + hd:LANE_A + hd + 1, :]
            last.append(cs[C - 1:C, :])
        return tuple(last)

    lax.fori_loop(0, S // C, body, tuple(jnp.zeros((1, LANES), F32) for _ in range(B)))


def even_gates(zs, par):
    B, S, _ = zs.shape
    return pl.pallas_call(
        functools.partial(_gate_kernel, S=S, B=B),
        grid=(1,),
        in_specs=[
            pl.BlockSpec((B, S, LANES), lambda i: (0, 0, 0)),
            pl.BlockSpec((8, LANES), lambda i: (0, 0)),
        ],
        out_specs=[
            pl.BlockSpec((B, S, LANES), lambda i: (0, 0, 0)),
            pl.BlockSpec((B, GDN_HEADS, 1, S), lambda i: (0, 0, 0, 0)),
        ],
        out_shape=[jax.ShapeDtypeStruct((B, S, LANES), F32), jax.ShapeDtypeStruct((B, GDN_HEADS, 1, S), F32)],
        compiler_params=_params("arbitrary"),
        name="even_gates",
    )(zs, par)


def _split3(x):
    hi = x.astype(BF16).astype(F32)
    mid = (x - hi).astype(BF16).astype(F32)
    lo = (x - hi - mid).astype(BF16).astype(F32)
    return hi, mid, lo


def _interleave(gens):
    out = [None] * len(gens)
    live = list(range(len(gens)))
    while live:
        for n in list(live):
            try:
                next(gens[n])
            except StopIteration as stop:
                out[n] = stop.value
                live.remove(n)
    return out


def _attend(state, c, qa, ka_ref, vt_ref, key0, nkeys, keep=None):
    ks = pl.ds(pl.multiple_of(key0, ATT_BLOCK), nkeys)
    st = _dot_nt(ka_ref[ks, :], qa)
    yield
    if keep is not None:
        st = jnp.where(keep, st, NEG)
    m_new = jnp.max(st, axis=0, keepdims=True)
    if state[c] is not None:
        m_old, acc_old = state[c]
        m_new = jnp.maximum(m_old, m_new)
    p = jnp.exp2(st - m_new).astype(BF16)
    pv = jnp.dot(vt_ref[:, ks], p, preferred_element_type=F32)
    state[c] = (m_new, pv if state[c] is None else acc_old * jnp.exp2(m_old - m_new) + pv)


def _attend_tile_pair(i, prep, ka_ref, vt_ref, m_ref, acc_ref, o_ref):
    t = ATT_BLOCK
    chains = [(hd, c) for c in range(2) for hd in range(ATT_HEADS)]
    num = lambda hd, c: 2 * hd + c
    qa = _interleave(prep)

    def step(state, hd, c, which, **kw):
        return _attend(state, num(hd, c), qa[num(hd, c)][which], ka_ref.at[hd], vt_ref.at[hd], **kw)

    def save(state):
        m_ref[...] = jnp.stack([state[n][0] for n in range(len(chains))])
        acc_ref[...] = jnp.stack([state[n][1] for n in range(len(chains))])

    def past_blocks(key0, n_steps):
        state = {n: (m_ref[n], acc_ref[n]) for n in range(len(chains))}
        _interleave([step(state, hd, c, 1, key0=key0 + s * 2 * t, nkeys=2 * t)
                     for s in range(n_steps) for hd, c in chains])
        save(state)

    state = {n: None for n in range(len(chains))}
    keep = _causal_keep()
    _interleave([step(state, hd, c, 0, key0=(2 * i + c) * t, nkeys=t, keep=keep) for hd, c in chains]
                + [step(state, hd, 1, 1, key0=(2 * i) * t, nkeys=t) for hd in range(ATT_HEADS)])
    save(state)

    def four_blocks(g, _):
        past_blocks(g * (4 * t), 2)
        return 0

    lax.fori_loop(0, i // 2, four_blocks, 0)

    @pl.when((i & 1) != 0)
    def _():
        past_blocks((i // 2) * (4 * t), 1)

    for hd, c in chains:
        acc = acc_ref[num(hd, c)]
        out_t = acc[:HEAD_DIM, :] * (1.0 / acc[HEAD_DIM:HEAD_DIM + 1, :])
        o_ref[c * t:(c + 1) * t, hd * HEAD_DIM:(hd + 1) * HEAD_DIM] = out_t.T.astype(o_ref.dtype)


def _transposed(v):
    r = lax.broadcasted_iota(jnp.int32, (HEAD_DIM, HEAD_DIM), 0)
    c = lax.broadcasted_iota(jnp.int32, (HEAD_DIM, HEAD_DIM), 1)
    v_t = _dot_nt(jnp.where(r == c, 1.0, 0.0).astype(BF16), v).astype(BF16)
    return jnp.concatenate([v_t, jnp.ones((V_ROWS - HEAD_DIM, ATT_BLOCK), BF16)], axis=0)


def _causal_keep():
    key = lax.broadcasted_iota(jnp.int32, (ATT_BLOCK, ATT_BLOCK), 0)
    qry = lax.broadcasted_iota(jnp.int32, (ATT_BLOCK, ATT_BLOCK), 1)
    return key <= qry


_ATT_SCRATCH = lambda S: [pltpu.VMEM((ATT_HEADS, S, 2 * HEAD_DIM), BF16), pltpu.VMEM((ATT_HEADS, V_ROWS, S), BF16),
                          pltpu.VMEM((2 * ATT_HEADS, 1, ATT_BLOCK), F32),
                          pltpu.VMEM((2 * ATT_HEADS, V_ROWS, ATT_BLOCK), F32)]


def _fox_kernel(q_ref, k_ref, v_ref, col_ref, qg_ref, kg_ref, o_ref, ka_ref, vt_ref, m_ref, acc_ref, *, S):
    h0 = pl.program_id(1) * ATT_HEADS
    i = pl.program_id(2)
    t = ATT_BLOCK
    D = HEAD_DIM
    lane = lax.broadcasted_iota(jnp.int32, (t, LANES), 1)

    @pl.when(i == 0)
    def _():
        def prep_keys(g, _):
            todo = []
            for u in range(KEY_BLOCKS_PER_STEP):
                sl = pl.ds(pl.multiple_of((g * KEY_BLOCKS_PER_STEP + u) * t, t), t)
                for hd in range(ATT_HEADS):
                    cols = slice(hd * D, (hd + 1) * D)
                    kn = _rms(k_ref[sl, cols].astype(F32), kg_ref[...]).astype(BF16)
                    hi, mid, lo = _split3(-LOG2E * _pick_lane(col_ref[sl, :], LANE_F + h0 + hd))
                    aug = jnp.where(lane < 3, 1.0, jnp.where(lane == 3, hi, jnp.where(lane == 4, mid,
                                                                                    jnp.where(lane == 5, lo, 0.0))))
                    todo.append((hd, sl, jnp.concatenate([kn, aug.astype(BF16)], axis=1), _transposed(v_ref[sl, cols])))
            for hd, sl, ka, vt in todo:
                ka_ref[hd, sl, :] = ka
                vt_ref[hd, :, sl] = vt
            return 0

        lax.fori_loop(0, S // (t * KEY_BLOCKS_PER_STEP), prep_keys, 0)

    def prep(hd, c):
        q = q_ref[c * t:(c + 1) * t, hd * D:(hd + 1) * D].astype(F32)
        qn = (_rms(q, qg_ref[...]) * (LOG2E * D ** -0.5)).astype(BF16)
        yield
        qsl = pl.ds(pl.multiple_of((2 * i + c) * t, t), t)
        hi, mid, lo = _split3(LOG2E * _pick_lane(col_ref[qsl, :], LANE_F + h0 + hd))
        aug = jnp.where(lane == 0, hi,
                        jnp.where(lane == 1, mid, jnp.where(lane == 2, lo, jnp.where(lane < 6, 1.0, 0.0))))
        qa = jnp.concatenate([qn, aug.astype(BF16)], axis=-1)
        return qa, qa

    _attend_tile_pair(i, [prep(hd, c) for hd in range(ATT_HEADS) for c in range(2)],
                      ka_ref, vt_ref, m_ref, acc_ref, o_ref)


def fox_attention(z, col, q_gain, k_gain):
    B, S, _ = z.shape
    t = 2 * ATT_BLOCK
    w = ATT_HEADS * HEAD_DIM
    return pl.pallas_call(
        functools.partial(_fox_kernel, S=S),
        grid=(B, FOX_HEADS // ATT_HEADS, S // t),
        in_specs=[
            pl.BlockSpec((None, t, w), lambda b, h, i: (b, i, FOX_Q0 // ATT_HEADS + h)),
            pl.BlockSpec((None, S, w), lambda b, h, i: (b, 0, FOX_K0 // ATT_HEADS + h)),
            pl.BlockSpec((None, S, w), lambda b, h, i: (b, 0, FOX_V0 // ATT_HEADS + h)),
            pl.BlockSpec((None, S, LANES), lambda b, h, i: (b, 0, 0)),
            pl.BlockSpec((1, HEAD_DIM), lambda b, h, i: (0, 0)),
            pl.BlockSpec((1, HEAD_DIM), lambda b, h, i: (0, 0)),
        ],
        out_specs=pl.BlockSpec((None, t, w), lambda b, h, i: (b, i, h)),
        out_shape=jax.ShapeDtypeStruct((B, S, FOX_HEADS * HEAD_DIM), BF16),
        scratch_shapes=_ATT_SCRATCH(S),
        compiler_params=_params("parallel", "parallel", "arbitrary"),
        name="fox_attention",
    )(z, z, z, col, q_gain.reshape(1, -1), k_gain.reshape(1, -1))


def _unit_lower_inverse(m):
    n = m.shape[0]
    r = lax.broadcasted_iota(jnp.int32, (n, n), 0)
    c = lax.broadcasted_iota(jnp.int32, (n, n), 1)
    eye = (r == c).astype(F32)

    def same_block(b):
        return (r // b) == (c // b)

    p = jnp.where(same_block(INV_BLOCK), m, 0.0)
    inv = eye - p
    k = 2
    while k < INV_BLOCK:
        pb = p.astype(BF16)
        p = jnp.dot(pb, pb, preferred_element_type=F32)
        yield
        inv = jnp.dot(inv.astype(BF16), (eye + p).astype(BF16), preferred_element_type=F32)
        yield
        k *= 2
    b = INV_BLOCK
    while b < n:
        off = jnp.where(same_block(2 * b), jnp.where(same_block(b), 0.0, m), 0.0).astype(BF16)
        ib = inv.astype(BF16)
        left = jnp.dot(ib, off, preferred_element_type=F32).astype(BF16)
        yield
        inv = inv - jnp.dot(left, ib, preferred_element_type=F32)
        yield
        b *= 2
    return inv


def _gdn_kernel(q_ref, k_ref, v_ref, gg_ref, wq_ref, wk_ref, wv_ref, col_ref, row_ref, on_ref,
                o_ref, qs_ref, ks_ref, vs_ref, *, S, rows, hb):
    h0 = pl.program_id(1) * hb
    C = GDN_CHUNK
    D = HEAD_DIM

    streams = [(x_ref, w_ref, dst_ref, hh, mode) for hh in range(hb)
               for x_ref, w_ref, dst_ref, mode in ((q_ref, wq_ref, qs_ref, "q"), (k_ref, wk_ref, ks_ref, "k"),
                                                   (v_ref, wv_ref, vs_ref, "v"))]
    halo = 16

    def conv_step(base, windows):
        outs = []
        for (x_ref, w_ref, dst_ref, hh, mode), win in zip(streams, windows):
            w = w_ref[:, hh * D:(hh + 1) * D]
            y = jnp.zeros((rows, D), F32)
            for tap in range(CONV_WIDTH):
                lead = halo - (CONV_WIDTH - 1) + tap
                y = y + w[tap:tap + 1, :] * pltpu.roll(win, rows + halo - lead, 0)[0:rows, :]
            y = y / (1.0 + jnp.exp(-y))
            if mode != "v":
                y = y * lax.rsqrt(jnp.sum(y * y, axis=-1, keepdims=True) + EPS)
            if mode == "q":
                y = y * D ** -0.5
            outs.append(y)
        for (x_ref, w_ref, dst_ref, hh, mode), y in zip(streams, outs):
            dst_ref[hh, pl.ds(base, rows), :] = y.astype(dst_ref.dtype)

    conv_step(0, [jnp.concatenate([jnp.zeros((halo, D), F32), x_ref[0:rows, hh * D:(hh + 1) * D].astype(F32)], axis=0)
                  for x_ref, _, _, hh, _ in streams])

    def conv_rest(n, _):
        base = pl.multiple_of(n * rows, rows)
        conv_step(base, [x_ref[pl.ds(pl.multiple_of(base - halo, halo), rows + halo), hh * D:(hh + 1) * D].astype(F32)
                         for x_ref, _, _, hh, _ in streams])
        return 0

    lax.fori_loop(1, S // rows, conv_rest, 0)

    r = lax.broadcasted_iota(jnp.int32, (C, C), 0)
    c = lax.broadcasted_iota(jnp.int32, (C, C), 1)
    incl = r >= c
    strict = r > c

    def chunk_local(hh, sl, tab):
        q = qs_ref[hh, sl, :].astype(F32)
        k = ks_ref[hh, sl, :].astype(F32)
        v = vs_ref[hh, sl, :].astype(F32)
        gcol = _pick_lane(tab, LANE_A + h0 + hh)
        beta = _pick_lane(tab, LANE_B + h0 + hh)
        grow = row_ref[hh, :, sl]
        glast = gcol[C - 1:C, :]
        decay = jnp.where(incl, jnp.exp(jnp.where(incl, gcol - grow, 0.0)), 0.0)
        eg = jnp.exp(gcol)
        kb = k * beta
        kbf = k.astype(BF16)
        m = jnp.where(strict, _dot_nt(kb.astype(BF16), kbf) * decay, 0.0)
        attn = (_dot_nt(q.astype(BF16), kbf) * decay).astype(BF16)
        yield
        tinv = (yield from _unit_lower_inverse(m)).astype(BF16)
        rhs = jnp.concatenate([v * beta, kb * eg], axis=-1).astype(BF16)
        sol = jnp.dot(tinv, rhs, preferred_element_type=F32)
        yield
        gate = gg_ref[sl, hh * D:(hh + 1) * D].astype(F32)
        return dict(u=sol[:, :D], w=sol[:, D:].astype(BF16), attn=attn, qg=(q * eg).astype(BF16),
                    kg_t=(k * jnp.exp(glast - gcol)).T.astype(BF16), keep=jnp.exp(glast),
                    gate=gate / (1.0 + jnp.exp(-gate)))

    def chunk_state(parts, state):
        outs = []
        for c in parts:
            sb = state.astype(BF16)
            v_new = c["u"] - jnp.dot(c["w"], sb, preferred_element_type=F32)
            o_state = jnp.dot(c["qg"], sb, preferred_element_type=F32)
            yield
            vb = v_new.astype(BF16)
            o = o_state + jnp.dot(c["attn"], vb, preferred_element_type=F32)
            state = state * c["keep"] + jnp.dot(c["kg_t"], vb, preferred_element_type=F32)
            yield
            outs.append((_rms(o, on_ref[...]) * c["gate"]).astype(o_ref.dtype))
        return state, outs

    def chunk_group(n, states):
        sls = [pl.ds(pl.multiple_of((n * GDN_CHUNKS_PER_STEP + g) * C, C), C) for g in range(GDN_CHUNKS_PER_STEP)]
        tabs = [col_ref[sl, :] for sl in sls]
        parts = _interleave([chunk_local(hh, sls[g], tabs[g]) for g in range(GDN_CHUNKS_PER_STEP) for hh in range(hb)])
        res = _interleave([chunk_state([parts[g * hb + hh] for g in range(GDN_CHUNKS_PER_STEP)], states[hh])
                           for hh in range(hb)])
        for g in range(GDN_CHUNKS_PER_STEP):
            o_ref[sls[g], :] = jnp.concatenate([outs[g] for _, outs in res], axis=-1)
        return tuple(s for s, _ in res)

    lax.fori_loop(0, S // (C * GDN_CHUNKS_PER_STEP), chunk_group, tuple(jnp.zeros((D, D), F32) for _ in range(hb)))


def gdn_mixer(z, conv_w, col, row, o_gain):
    B, S, _ = z.shape
    D = HEAD_DIM
    hb = GDN_HEADS_PER_STEP
    rows = 256
    seq = lambda off: pl.BlockSpec((None, S, hb * D), lambda b, h: (b, 0, off // hb + h),
                                   pipeline_mode=pl.Buffered(1))
    cw = lambda off: pl.BlockSpec((CONV_WIDTH, hb * D), lambda b, h: (0, off // hb + h))
    return pl.pallas_call(
        functools.partial(_gdn_kernel, S=S, rows=rows, hb=hb),
        grid=(B, GDN_HEADS // hb),
        in_specs=[
            seq(GDN_Q0), seq(GDN_K0), seq(GDN_V0), seq(GDN_G0),
            cw(0), cw(GDN_HEADS), cw(2 * GDN_HEADS),
            pl.BlockSpec((None, S, LANES), lambda b, h: (b, 0, 0)),
            pl.BlockSpec((None, hb, 1, S), lambda b, h: (b, h, 0, 0)),
            pl.BlockSpec((1, D), lambda b, h: (0, 0)),
        ],
        out_specs=pl.BlockSpec((None, S, hb * D), lambda b, h: (b, 0, h)),
        out_shape=jax.ShapeDtypeStruct((B, S, GDN_HEADS * D), BF16),
        scratch_shapes=[pltpu.VMEM((hb, S, D), BF16)] * 3,
        compiler_params=_params("parallel", "parallel"),
        name="gdn_mixer",
    )(z, z, z, z, conv_w, conv_w, conv_w, col, row, o_gain.reshape(1, D))


def _rope(x, cos, sin_signed):
    return x * cos + pltpu.roll(x, HEAD_DIM // 2, 1) * sin_signed


def _moba_kernel(q_ref, k_ref, v_ref, cos_ref, sin_ref, qg_ref, kg_ref, o_ref,
                 ka_ref, vt_ref, m_ref, acc_ref, kmean_ref, *, S):
    i = pl.program_id(2)
    t = ATT_BLOCK
    D = HEAD_DIM
    lane = lax.broadcasted_iota(jnp.int32, (t, LANES), 1)

    @pl.when(i == 0)
    def _():
        kmean_ref[...] = jnp.zeros(kmean_ref.shape, F32)

        def prep_keys(g, _):
            todo = []
            for u in range(KEY_BLOCKS_PER_STEP):
                n = g * KEY_BLOCKS_PER_STEP + u
                sl = pl.ds(pl.multiple_of(n * t, t), t)
                onehot = jnp.where(lane == n, 1.0, 0.0).astype(BF16)
                for hd in range(ATT_HEADS):
                    cols = slice(hd * D, (hd + 1) * D)
                    k = _rope(_rms(k_ref[sl, cols].astype(F32), kg_ref[...]), cos_ref[sl, :], sin_ref[sl, :])
                    todo.append((hd, n, sl, jnp.concatenate([k.astype(BF16), onehot], axis=1),
                                 jnp.mean(k, axis=0, keepdims=True), _transposed(v_ref[sl, cols])))
            for hd, n, sl, ka, kmean, vt in todo:
                ka_ref[hd, sl, :] = ka
                kmean_ref[hd, pl.ds(n, 1), :] = kmean
                vt_ref[hd, :, sl] = vt
            return 0

        lax.fori_loop(0, S // (t * KEY_BLOCKS_PER_STEP), prep_keys, 0)

    lane_f = lane.astype(F32)

    def prep(hd, c):
        cur = 2 * i + c
        qsl = pl.ds(pl.multiple_of(cur * t, t), t)
        q = _rope(_rms(q_ref[c * t:(c + 1) * t, hd * D:(hd + 1) * D].astype(F32), qg_ref[...]),
                  cos_ref[qsl, :], sin_ref[qsl, :])
        yield
        kmean = kmean_ref[hd]
        km_hi = kmean.astype(BF16)
        km_split = jnp.concatenate([km_hi, (kmean - km_hi.astype(F32)).astype(BF16)], axis=0)
        q_hi = q.astype(BF16)
        q_lo = (q - q_hi.astype(F32)).astype(BF16)
        part = _dot_nt(q_hi, km_split)
        gate = part[:, :LANES] + part[:, LANES:] + _dot_nt(q_lo, km_split[:LANES, :])
        gate = jnp.where(lane < cur, gate, -jnp.inf)
        yield
        sel_bias = jnp.full((t, LANES), NEG, F32)
        for _ in range(MOBA_TOPK):
            top = jnp.max(gate, axis=-1, keepdims=True)
            yield
            first = jnp.min(jnp.where(gate == top, lane_f, float(LANES)), axis=-1, keepdims=True)
            yield
            pick = lane_f == first
            sel_bias = jnp.where(pick & (first < cur.astype(F32)), 0.0, sel_bias)
            gate = jnp.where(pick, -jnp.inf, gate)
        qs = (q * (LOG2E * D ** -0.5)).astype(BF16)
        return (jnp.concatenate([qs, jnp.zeros((t, LANES), BF16)], axis=-1),
                jnp.concatenate([qs, sel_bias.astype(BF16)], axis=-1))

    _attend_tile_pair(i, [prep(hd, c) for hd in range(ATT_HEADS) for c in range(2)],
                      ka_ref, vt_ref, m_ref, acc_ref, o_ref)


def moba_attention(z, cos, sin_signed, q_gain, k_gain):
    B, S, W = z.shape
    H = W // (3 * HEAD_DIM)
    assert ATT_BLOCK == MOBA_BLOCK
    t = 2 * ATT_BLOCK
    w = ATT_HEADS * HEAD_DIM
    hp = H // ATT_HEADS
    return pl.pallas_call(
        functools.partial(_moba_kernel, S=S),
        grid=(B, hp, S // t),
        in_specs=[
            pl.BlockSpec((None, t, w), lambda b, h, i: (b, i, h)),
            pl.BlockSpec((None, S, w), lambda b, h, i: (b, 0, hp + h)),
            pl.BlockSpec((None, S, w), lambda b, h, i: (b, 0, 2 * hp + h)),
            pl.BlockSpec((S, HEAD_DIM), lambda b, h, i: (0, 0)),
            pl.BlockSpec((S, HEAD_DIM), lambda b, h, i: (0, 0)),
            pl.BlockSpec((1, HEAD_DIM), lambda b, h, i: (0, 0)),
            pl.BlockSpec((1, HEAD_DIM), lambda b, h, i: (0, 0)),
        ],
        out_specs=pl.BlockSpec((None, t, w), lambda b, h, i: (b, i, h)),
        out_shape=jax.ShapeDtypeStruct((B, S, H * HEAD_DIM), BF16),
        scratch_shapes=_ATT_SCRATCH(S) + [pltpu.VMEM((ATT_HEADS, LANES, HEAD_DIM), F32)],
        compiler_params=_params("parallel", "parallel", "arbitrary"),
        name="moba_attention",
    )(z, z, z, cos, sin_signed, q_gain.reshape(1, -1), k_gain.reshape(1, -1))


META_IDX, META_RANK, META_GATE = 0, 2, 4


def _router_kernel(x_ref, *refs, n_proj):
    proj_refs = refs[:2 * n_proj]
    g_ref, w_ref, y_ref, xn_ref, meta_ref, cnt_ref, carry_ref = refs[2 * n_proj:]
    i = pl.program_id(0)
    tm = x_ref.shape[0]

    @pl.when(i == 0)
    def _():
        carry_ref[...] = jnp.zeros(carry_ref.shape, F32)

    w = w_ref[...]
    w_hi = w.astype(BF16)
    w_lo = (w - w_hi.astype(F32)).astype(BF16)
    part = tm // ROUTER_CHAINS
    lane = lax.broadcasted_iota(jnp.int32, (part, LANES), 1)
    lane_f = lane.astype(F32)

    def route(n):
        rows = slice(n * part, (n + 1) * part)
        y = _plus_projections(x_ref[rows, :], proj_refs, rows)
        y_ref[rows, :] = y
        yield
        xn = _rms(y, g_ref[...])
        xn_ref[rows, :] = _pack_bf16_pairs(xn)
        x_hi = xn.astype(BF16)
        x_lo = (xn - x_hi.astype(F32)).astype(BF16)
        logits = (jnp.dot(x_hi, w_hi, preferred_element_type=F32) + jnp.dot(x_hi, w_lo, preferred_element_type=F32)
                  + jnp.dot(x_lo, w_hi, preferred_element_type=F32))
        yield
        logits = jnp.where(lane < N_EXPERTS, logits, -jnp.inf)
        top1 = jnp.max(logits, axis=-1, keepdims=True)
        yield
        idx1 = jnp.min(jnp.where(logits == top1, lane_f, float(LANES)), axis=-1, keepdims=True)
        yield
        rest = jnp.where(lane_f == idx1, -jnp.inf, logits)
        top2 = jnp.max(rest, axis=-1, keepdims=True)
        yield
        idx2 = jnp.min(jnp.where(rest == top2, lane_f, float(LANES)), axis=-1, keepdims=True)
        e2 = jnp.exp(top2 - top1)
        denom = 1.0 + e2
        return idx1, idx2, 1.0 / denom, e2 / denom, jnp.where((lane_f == idx1) | (lane_f == idx2), 1.0, 0.0)

    picks = _interleave([route(n) for n in range(ROUTER_CHAINS)])
    chosen = jnp.concatenate([p[4] for p in picks], axis=0)
    r = lax.broadcasted_iota(jnp.int32, (tm, tm), 0)
    c = lax.broadcasted_iota(jnp.int32, (tm, tm), 1)
    ahead = jnp.dot(jnp.where(r > c, 1.0, 0.0).astype(BF16), chosen.astype(BF16), preferred_element_type=F32)
    carry = carry_ref[...]
    for n, (idx1, idx2, g1, g2, _) in enumerate(picks):
        rows = slice(n * part, (n + 1) * part)
        rank = ahead[rows, :] + carry
        rank1 = jnp.sum(jnp.where(lane_f == idx1, rank, 0.0), axis=-1, keepdims=True)
        rank2 = jnp.sum(jnp.where(lane_f == idx2, rank, 0.0), axis=-1, keepdims=True)
        meta = jnp.zeros((part, LANES), F32)
        for k, v in enumerate((idx1, idx2, rank1, rank2, g1, g2)):
            meta = jnp.where(lane == k, v, meta)
        meta_ref[rows, :] = meta
    carry = carry + jnp.sum(chosen, axis=0, keepdims=True)
    carry_ref[...] = carry
    cnt_ref[...] = jnp.broadcast_to(carry, cnt_ref.shape)


def moe_router(x, proj, gain, w_router, *, tm):
    M, K = x.shape
    w = jnp.zeros((K, LANES), F32).at[:, :N_EXPERTS].set(w_router)
    proj_specs, proj_args = _projection_specs(proj, tm)
    return pl.pallas_call(
        functools.partial(_router_kernel, n_proj=len(proj)),
        grid=(M // tm,),
        in_specs=[
            pl.BlockSpec((tm, K), lambda i: (i, 0)),
            *proj_specs,
            pl.BlockSpec((1, K), lambda i: (0, 0)),
            pl.BlockSpec((K, LANES), lambda i: (0, 0)),
        ],
        out_specs=[
            pl.BlockSpec((tm, K), lambda i: (i, 0)),
            pl.BlockSpec((tm, K // 2), lambda i: (i, 0)),
            pl.BlockSpec((tm, LANES), lambda i: (i, 0)),
            pl.BlockSpec((8, LANES), lambda i: (0, 0)),
        ],
        out_shape=[
            jax.ShapeDtypeStruct((M, K), F32),
            jax.ShapeDtypeStruct((M, K // 2), jnp.int32),
            jax.ShapeDtypeStruct((M, LANES), F32),
            jax.ShapeDtypeStruct((8, LANES), F32),
        ],
        scratch_shapes=[pltpu.VMEM((1, LANES), F32)],
        compiler_params=_params("arbitrary"),
        name="moe_router",
    )(x, *proj_args, gain.reshape(1, K), w)


def _sc_workers():
    info = plsc.get_sparse_core_info()
    return info.num_cores, info.num_cores * info.num_subcores


def scatter_rows(rows, dest, n_out):
    M, W = rows.shape
    nc, nw = _sc_workers()
    assert M % (nw * SC_ROWS) == 0
    per_w = M // nw
    mesh = plsc.VectorSubcoreMesh(core_axis_name="c", subcore_axis_name="s")

    @functools.partial(
        pl.kernel, mesh=mesh, out_type=jax.ShapeDtypeStruct((n_out, W), rows.dtype),
        scratch_types=[pltpu.VMEM((SC_ROWS,), jnp.int32), pltpu.VMEM((SC_ROWS, W), rows.dtype),
                       pltpu.SemaphoreType.DMA])
    def kern(rows_hbm, dest_hbm, out_hbm, idx_v, rows_v, sem):
        wid = lax.axis_index("s") * nc + lax.axis_index("c")

        @pl.loop(0, per_w // SC_ROWS)
        def _(g):
            base = wid * per_w + g * SC_ROWS
            pltpu.sync_copy(rows_hbm.at[pl.ds(base, SC_ROWS)], rows_v)
            for k in range(2):
                pltpu.sync_copy(dest_hbm.at[k, pl.ds(base, SC_ROWS)], idx_v)
                pltpu.async_copy(rows_v, out_hbm.at[idx_v], sem).wait()

    return kern(rows, dest)


def gather_rows(table, idx):
    N = idx.shape[0]
    W = table.shape[1]
    nc, nw = _sc_workers()
    assert N % (nw * SC_ROWS) == 0
    per_w = N // nw
    mesh = plsc.VectorSubcoreMesh(core_axis_name="c", subcore_axis_name="s")

    @functools.partial(
        pl.kernel, mesh=mesh, out_type=jax.ShapeDtypeStruct((N, W), table.dtype),
        scratch_types=[pltpu.VMEM((SC_ROWS,), jnp.int32), pltpu.VMEM((SC_ROWS, W), table.dtype),
                       pltpu.SemaphoreType.DMA])
    def kern(table_hbm, idx_hbm, out_hbm, idx_v, rows_v, sem):
        wid = lax.axis_index("s") * nc + lax.axis_index("c")

        @pl.loop(0, per_w // SC_ROWS)
        def _(g):
            base = wid * per_w + g * SC_ROWS
            pltpu.sync_copy(idx_hbm.at[pl.ds(base, SC_ROWS)], idx_v)
            pltpu.async_copy(table_hbm.at[idx_v], rows_v, sem).wait()
            pltpu.sync_copy(rows_v, out_hbm.at[pl.ds(base, SC_ROWS)])

    return kern(table, idx)


def _combine_kernel(h_ref, y1_ref, y2_ref, meta_ref, o_ref):
    g1 = meta_ref[:, META_GATE:META_GATE + 1]
    g2 = meta_ref[:, META_GATE + 1:META_GATE + 2]
    o_ref[...] = (h_ref[...] + g1 * _unpack_bf16_pairs(y1_ref[...]).astype(F32)
                  + g2 * _unpack_bf16_pairs(y2_ref[...]).astype(F32))


def moe_combine(h, y_pairs, meta):
    M, K = h.shape
    p = PERM_TILE
    nt = M // p
    return pl.pallas_call(
        _combine_kernel,
        grid=(nt,),
        in_specs=[
            pl.BlockSpec((p, K), lambda i: (i, 0)),
            pl.BlockSpec((p, K // 2), lambda i: (i, 0)),
            pl.BlockSpec((p, K // 2), lambda i: (nt + i, 0)),
            pl.BlockSpec((p, LANES), lambda i: (i, 0)),
        ],
        out_specs=pl.BlockSpec((p, K), lambda i: (i, 0)),
        out_shape=jax.ShapeDtypeStruct((M, K), F32),
        compiler_params=_params("parallel"),
        name="moe_combine",
    )(h, y_pairs, y_pairs, meta)


def _expert_ffn_kernel(te_ref, tv_ref, x_ref, wg_ref, wu_ref, wd_ref, o_ref, xb_ref, acc_ref):
    del te_ref
    i = pl.program_id(0)
    j = pl.program_id(1)
    tm = x_ref.shape[0]
    valid = tv_ref[i]

    def swiglu_rows(rows):
        @pl.when(j == 0)
        def _():
            xb_ref[0:rows, :] = _unpack_bf16_pairs(x_ref[0:rows, :])

        x = xb_ref[0:rows, :]
        gt = jnp.dot(x, wg_ref[...], preferred_element_type=F32)
        up = jnp.dot(x, wu_ref[...], preferred_element_type=F32)
        act = (gt / (1.0 + jnp.exp(-gt)) * up).astype(BF16)
        part = jnp.dot(act, wd_ref[...], preferred_element_type=F32)

        @pl.when(j == 0)
        def _():
            acc_ref[0:rows, :] = part

        @pl.when(j > 0)
        def _():
            acc_ref[0:rows, :] += part

        @pl.when(j == pl.num_programs(1) - 1)
        def _():
            o_ref[0:rows, :] = _pack_bf16_pairs(acc_ref[0:rows, :])
            if rows < tm:
                o_ref[rows:tm, :] = jnp.zeros((tm - rows, o_ref.shape[1]), o_ref.dtype)

    @pl.when(valid > tm // 2)
    def _():
        swiglu_rows(tm)

    @pl.when((valid > 0) & (valid <= tm // 2))
    def _():
        swiglu_rows(tm // 2)

    @pl.when((valid == 0) & (j == 0))
    def _():
        o_ref[...] = jnp.zeros(o_ref.shape, o_ref.dtype)


def expert_ffn(xs, tile_expert, tile_valid, w_gate_up, w_down, *, tm, tf):
    R = xs.shape[0]
    E, F, K = w_down.shape
    nf = F // tf
    live = lambda i, tv: tv[i] > 0
    col = lambda i, j, tv: jnp.where(live(i, tv), j, nf - 1)
    grid_spec = pltpu.PrefetchScalarGridSpec(
        num_scalar_prefetch=2,
        grid=(R // tm, nf),
        in_specs=[
            pl.BlockSpec((tm, K // 2), lambda i, j, te, tv: (te[R // tm + i], 0)),
            pl.BlockSpec((None, K, tf), lambda i, j, te, tv: (te[i], 0, col(i, j, tv))),
            pl.BlockSpec((None, K, tf), lambda i, j, te, tv: (te[i], 0, nf + col(i, j, tv))),
            pl.BlockSpec((None, tf, K), lambda i, j, te, tv: (te[i], col(i, j, tv), 0)),
        ],
        out_specs=pl.BlockSpec((tm, K // 2), lambda i, j, te, tv: (i, 0)),
        scratch_shapes=[pltpu.VMEM((tm, K), BF16), pltpu.VMEM((tm, K), F32)],
    )
    return pl.pallas_call(
        _expert_ffn_kernel,
        grid_spec=grid_spec,
        out_shape=jax.ShapeDtypeStruct((R, K // 2), jnp.int32),
        compiler_params=_params("arbitrary", "arbitrary"),
        name="moe_expert_ffn",
    )(tile_expert, tile_valid, xs, w_gate_up, w_gate_up, w_down)


def moe_residual(h, proj, gain, w_router, w_gate_up, w_down):
    M, K = h.shape
    p = PERM_TILE
    tm = EXPERT_TILE
    n_rows = 2 * M + N_EXPERTS * tm
    h, xn, meta, cnt = moe_router(h, proj, gain, w_router, tm=p)

    counts = cnt[0, :N_EXPERTS].astype(jnp.int32)
    padded = (counts + tm - 1) // tm * tm
    ends = jnp.cumsum(padded)
    offsets = ends - padded
    n_tiles = n_rows // tm
    tile_row = jnp.arange(n_tiles) * tm
    n_used = ends[-1] // tm
    last_used = jnp.minimum(tile_row // tm, n_used - 1)
    expert_of = lambda row: jnp.minimum(jnp.sum(ends[None, :] <= row[:, None], axis=1), N_EXPERTS - 1)
    tile_expert = expert_of(last_used * tm)
    row_in_expert = tile_row - jnp.sum(jnp.where(tile_expert[:, None] == jnp.arange(N_EXPERTS), offsets, 0), axis=1)
    own_count = jnp.sum(jnp.where(tile_expert[:, None] == jnp.arange(N_EXPERTS), counts, 0), axis=1)
    tile_valid = jnp.where(tile_row // tm < n_used, jnp.clip(own_count - row_in_expert, 0, tm), 0)
    tile_tables = jnp.concatenate([tile_expert, last_used]).astype(jnp.int32)

    idx = meta[:, META_IDX:META_IDX + 2].astype(jnp.int32)
    rank = meta[:, META_RANK:META_RANK + 2].astype(jnp.int32)
    dest = (jnp.sum(jnp.where(idx[:, :, None] == jnp.arange(N_EXPERTS), offsets, 0), axis=-1) + rank).T

    xs = scatter_rows(xn, dest, n_rows)
    ys = expert_ffn(xs, tile_tables, tile_valid.astype(jnp.int32), w_gate_up, w_down, tm=tm, tf=EXPERT_COLS)
    return moe_combine(h, gather_rows(ys, dest.reshape(-1)), meta)


def _ffn_kernel(x_ref, *refs, n_proj):
    proj_refs = refs[:2 * n_proj]
    g_ref, wg_ref, wu_ref, wd_ref, o_ref, xn_ref, acc_ref = refs[2 * n_proj:]
    j = pl.program_id(1)

    @pl.when(j == 0)
    def _():
        x = _plus_projections(x_ref[...], proj_refs)
        xn_ref[...] = _rms(x, g_ref[...]).astype(BF16)
        acc_ref[...] = x

    xn = xn_ref[...]
    gt = jnp.dot(xn, wg_ref[...], preferred_element_type=F32)
    up = jnp.dot(xn, wu_ref[...], preferred_element_type=F32)
    act = gt / (1.0 + jnp.exp(-gt)) * up
    acc_ref[...] += jnp.dot(act.astype(BF16), wd_ref[...], preferred_element_type=F32)

    @pl.when(j == pl.num_programs(1) - 1)
    def _():
        o_ref[...] = acc_ref[...]


def ffn_residual(x, proj, gain, w_gate_up, w_down, *, tm, tf):
    M, K = x.shape
    F = w_down.shape[0]
    nf = F // tf
    proj_specs, proj_args = _projection_specs(proj, tm)
    return pl.pallas_call(
        functools.partial(_ffn_kernel, n_proj=len(proj)),
        grid=(M // tm, nf),
        in_specs=[
            pl.BlockSpec((tm, K), lambda i, j: (i, 0)),
            *proj_specs,
            pl.BlockSpec((1, K), lambda i, j: (0, 0)),
            pl.BlockSpec((K, tf), lambda i, j: (0, j)),
            pl.BlockSpec((K, tf), lambda i, j: (0, nf + j)),
            pl.BlockSpec((tf, K), lambda i, j: (j, 0)),
        ],
        out_specs=pl.BlockSpec((tm, K), lambda i, j: (i, 0)),
        out_shape=jax.ShapeDtypeStruct((M, K), F32),
        scratch_shapes=[pltpu.VMEM((tm, K), BF16), pltpu.VMEM((tm, K), F32)],
        compiler_params=_params("parallel", "arbitrary"),
        name="ffn_residual",
    )(x, *proj_args, gain.reshape(1, K), w_gate_up, w_gate_up, w_down)


def _even_mix(h, B, S, norm1, w_in, f_bias, q_norm, k_norm, conv_w, a_log, dt_bias, o_norm, w_out):
    M, D = h.shape
    fw, gw = FOX_HEADS * HEAD_DIM, GDN_HEADS * HEAD_DIM
    o_ff = 3 * fw
    o_gq = o_ff + FOX_HEADS
    o_ga = o_gq + 3 * gw
    o_gb = o_ga + GDN_HEADS
    o_gg = o_gb + GDN_HEADS
    w_big = jnp.concatenate([w_in[:, :o_ff], w_in[:, o_gq:o_ga], w_in[:, o_gg:]], axis=1)
    w_small = jnp.concatenate([w_in[:, o_ff:o_gq], w_in[:, o_ga:o_gg],
                               jnp.zeros((D, LANES - FOX_HEADS - 2 * GDN_HEADS), F32)], axis=1)
    w_big, w_small = (t.astype(BF16) for t in lax.optimization_barrier((w_big, w_small)))
    z, zs = norm_matmul(h, norm1, w_big, w_small, tm=ROW_TILE, tn=IN_PROJ_COLS)
    par = jnp.zeros((8, LANES), F32)
    par = par.at[0, LANE_F:LANE_F + FOX_HEADS].set(f_bias).at[0, LANE_A:LANE_A + GDN_HEADS].set(dt_bias)
    par = par.at[1, LANE_A:LANE_A + GDN_HEADS].set(a_log)
    col, row = even_gates(zs.reshape(B, S, LANES), par)
    z = z.reshape(B, S, -1)
    fox = fox_attention(z, col, q_norm, k_norm)
    gdn = gdn_mixer(z, conv_w, col, row, o_norm)
    w_out = w_out.astype(BF16)
    return [(fox.reshape(M, fw), w_out[:fw]), (gdn.reshape(M, gw), w_out[fw:])]


def _odd_mix(h, B, S, norm1, w_qkv, q_norm, k_norm, w_out):
    M, D = h.shape
    z = norm_matmul(h, norm1, w_qkv.astype(BF16), tm=ROW_TILE, tn=QKV_COLS).reshape(B, S, -1)
    half = HEAD_DIM // 2
    inv = jnp.power(ROPE_THETA, -jnp.arange(half, dtype=F32) / half)
    ang = jnp.arange(S, dtype=F32)[:, None] * inv[None, :]
    cos, sin = jnp.cos(ang), jnp.sin(ang)
    cos_full = jnp.concatenate([cos, cos], axis=-1)
    sin_signed = jnp.concatenate([-sin, sin], axis=-1)
    att = moba_attention(z, cos_full, sin_signed, q_norm, k_norm)
    return [(att.reshape(M, -1), w_out.astype(BF16))]


def kernel(x, e_norm1, e_w_in, e_fox_f_bias, e_fox_q_norm, e_fox_k_norm, e_gdn_conv, e_gdn_a_log,
           e_gdn_dt_bias, e_gdn_o_norm, e_w_out, e_norm2, e_ffn_w_gate_up, e_ffn_w_down,
           o_norm1, o_w_qkv, o_q_norm, o_k_norm, o_w_out, o_norm2, o_router, o_exp_w_gate_up, o_exp_w_down):
    B, S, D = x.shape
    h = x.reshape(B * S, D)
    depth = e_norm1.shape[0] + o_norm1.shape[0]
    for layer in range(depth):
        i = layer // 2
        if layer % 2 == 0:
            mix = _even_mix(h, B, S, e_norm1[i], e_w_in[i], e_fox_f_bias[i], e_fox_q_norm[i], e_fox_k_norm[i],
                            e_gdn_conv[i], e_gdn_a_log[i], e_gdn_dt_bias[i], e_gdn_o_norm[i], e_w_out[i])
            h = ffn_residual(h, mix, e_norm2[i], e_ffn_w_gate_up[i].astype(BF16), e_ffn_w_down[i].astype(BF16),
                             tm=FFN_ROWS, tf=FFN_COLS)
        else:
            mix = _odd_mix(h, B, S, o_norm1[i], o_w_qkv[i], o_q_norm[i], o_k_norm[i], o_w_out[i])
            h = moe_residual(h, mix, o_norm2[i], o_router[i], o_exp_w_gate_up[i].astype(BF16),
                             o_exp_w_down[i].astype(BF16))
    return h.reshape(B, S, D)
```

```python
import functools

import jax
import jax.numpy as jnp
from jax import lax
from jax.experimental import pallas as pl
from jax.experimental.pallas import tpu as pltpu
from jax.experimental.pallas import tpu_sc as plsc

F32 = jnp.float32
BF16 = jnp.bfloat16

HEAD_DIM = 128
FOX_HEADS = 4
GDN_HEADS = 4
CONV_WIDTH = 4
MOBA_BLOCK = 256
MOBA_TOPK = 3
N_EXPERTS = 8
ROPE_THETA = 10000.0
EPS = 1e-6

LANES = 128
GDN_CHUNK = 128
INV_BLOCK = 16
GDN_HEADS_PER_STEP = 4
GDN_CHUNKS_PER_STEP = 4
NEG = -(2.0 ** 100)
LOG2E = 1.4426950408889634
ATT_BLOCK = 256
ATT_HEADS = 4
KEY_BLOCKS_PER_STEP = 2
V_ROWS = HEAD_DIM + 16
PERM_TILE = 1024
ROUTER_CHAINS = 4
SC_ROWS = 64
ROW_TILE = 1024
MXU_COLS = 256
IN_PROJ_COLS = 7 * MXU_COLS
QKV_COLS = 6 * MXU_COLS
FFN_ROWS, FFN_COLS = 512, 1408
EXPERT_TILE = 1024
EXPERT_COLS = 896
EXPERT_ROW_PARTS = 4
VMEM_LIMIT_BYTES = 56 * 1024 * 1024

FOX_Q0, FOX_K0, FOX_V0 = 0, 4, 8
GDN_Q0, GDN_K0, GDN_V0, GDN_G0 = 12, 16, 20, 24
LANE_F, LANE_A, LANE_B = 0, 4, 8


def _params(*sem):
    return pltpu.CompilerParams(dimension_semantics=sem, vmem_limit_bytes=VMEM_LIMIT_BYTES)


def _rms(x, gain):
    return x * lax.rsqrt(jnp.mean(x * x, axis=-1, keepdims=True) + EPS) * gain


def _dot_nt(a, b, **kw):
    return lax.dot_general(a, b, (((1,), (1,)), ((), ())), preferred_element_type=F32, **kw)


def _pick_lane(x, lane_idx):
    lane = lax.broadcasted_iota(jnp.int32, x.shape, 1)
    return jnp.sum(jnp.where(lane == lane_idx, x, 0.0), axis=-1, keepdims=True)


def _pack_bf16_pairs(x):
    n = x.shape[1] // 2
    hi = pltpu.bitcast(x[:, :n].astype(BF16).astype(F32), jnp.uint32)
    lo = pltpu.bitcast(x[:, n:].astype(BF16).astype(F32), jnp.uint32)
    return pltpu.bitcast(hi | (lo >> 16), jnp.int32)


def _unpack_bf16_pairs(w):
    u = pltpu.bitcast(w, jnp.uint32)
    hi = pltpu.bitcast(u & jnp.uint32(0xFFFF0000), F32).astype(BF16)
    lo = pltpu.bitcast(u << 16, F32).astype(BF16)
    return jnp.concatenate([hi, lo], axis=1)


def _norm_mm_kernel(x_ref, g_ref, w_ref, *rest, has_aux):
    if has_aux:
        waux_ref, o_ref, oaux_ref, xn_ref = rest
    else:
        o_ref, xn_ref = rest

    @pl.when(pl.program_id(1) == 0)
    def _():
        xn = _rms(x_ref[...], g_ref[...]).astype(BF16)
        xn_ref[...] = xn
        if has_aux:
            oaux_ref[...] = jnp.dot(xn, waux_ref[...], preferred_element_type=F32)

    o_ref[...] = jnp.dot(xn_ref[...], w_ref[...], preferred_element_type=F32).astype(o_ref.dtype)


def norm_matmul(x, gain, w, w_aux=None, *, tm, tn, out_dtype=BF16):
    M, K = x.shape
    N = w.shape[1]
    has_aux = w_aux is not None
    in_specs = [
        pl.BlockSpec((tm, K), lambda i, j: (i, 0)),
        pl.BlockSpec((1, K), lambda i, j: (0, 0)),
        pl.BlockSpec((K, tn), lambda i, j: (0, j)),
    ]
    out_shape = [jax.ShapeDtypeStruct((M, N), out_dtype)]
    out_specs = [pl.BlockSpec((tm, tn), lambda i, j: (i, j))]
    args = [x, gain.reshape(1, K), w]
    if has_aux:
        in_specs.append(pl.BlockSpec((K, LANES), lambda i, j: (0, 0)))
        out_shape.append(jax.ShapeDtypeStruct((M, LANES), F32))
        out_specs.append(pl.BlockSpec((tm, LANES), lambda i, j: (i, 0)))
        args.append(w_aux)
    res = pl.pallas_call(
        functools.partial(_norm_mm_kernel, has_aux=has_aux),
        grid=(M // tm, N // tn),
        in_specs=in_specs,
        out_specs=out_specs,
        out_shape=out_shape,
        scratch_shapes=[pltpu.VMEM((tm, K), BF16)],
        compiler_params=_params("parallel", "arbitrary"),
        name="norm_matmul",
    )(*args)
    return res if has_aux else res[0]


def _plus_projections(x, proj_refs, rows=slice(None)):
    for a_ref, w_ref in zip(proj_refs[0::2], proj_refs[1::2]):
        x = x + jnp.dot(a_ref[rows, :], w_ref[...], preferred_element_type=F32)
    return x


def _projection_specs(pairs, tm):
    specs, args = [], []
    for a, w in pairs:
        specs.append(pl.BlockSpec((tm, a.shape[1]), lambda i, *_: (i, 0)))
        specs.append(pl.BlockSpec(w.shape, lambda i, *_: (0, 0)))
        args += [a, w]
    return specs, args


def _gate_kernel(zs_ref, par_ref, col_ref, row_ref, *, S, B):
    C = GDN_CHUNK
    bias = par_ref[0:1, :]
    neg_a = -jnp.exp(par_ref[1:2, :])
    r = lax.broadcasted_iota(jnp.int32, (C, C), 0)
    c = lax.broadcasted_iota(jnp.int32, (C, C), 1)
    tril = (r >= c).astype(F32)
    lane = lax.broadcasted_iota(jnp.int32, (C, LANES), 1)

    def body(n, carry):
        sl = pl.ds(pl.multiple_of(n * C, C), C)
        us, betas = [], []
        for bi in range(B):
            z = zs_ref[bi, sl, :]
            t = z + bias
            soft = jnp.log(1.0 + jnp.exp(-jnp.abs(t)))
            log_f = jnp.minimum(t, 0.0) - soft
            g = neg_a * (jnp.maximum(t, 0.0) + soft)
            betas.append(1.0 / (1.0 + jnp.exp(-z)))
            us.append(jnp.where(lane < LANE_A, log_f, jnp.where(lane < LANE_B, g, 0.0)))
        sums = jnp.dot(tril, jnp.concatenate(us, axis=1), preferred_element_type=F32, precision=lax.Precision.HIGHEST)
        last = []
        for bi in range(B):
            cs = sums[:, bi * LANES:(bi + 1) * LANES] + jnp.where(lane < LANE_A, carry[bi], 0.0)
            out = jnp.where(lane < LANE_B, cs, betas[bi])
            col_ref[bi, sl, :] = out
            out_t = out.T
            for hd in range(GDN_HEADS):
                row_ref[bi, hd, :, sl] = out_t[LANE_A + hd:LANE_A + hd + 1, :]
            last.append(cs[C - 1:C, :])
        return tuple(last)

    lax.fori_loop(0, S // C, body, tuple(jnp.zeros((1, LANES), F32) for _ in range(B)))


def even_gates(zs, par):
    B, S, _ = zs.shape
    return pl.pallas_call(
        functools.partial(_gate_kernel, S=S, B=B),
        grid=(1,),
        in_specs=[
            pl.BlockSpec((B, S, LANES), lambda i: (0, 0, 0)),
            pl.BlockSpec((8, LANES), lambda i: (0, 0)),
        ],
        out_specs=[
            pl.BlockSpec((B, S, LANES), lambda i: (0, 0, 0)),
            pl.BlockSpec((B, GDN_HEADS, 1, S), lambda i: (0, 0, 0, 0)),
        ],
        out_shape=[jax.ShapeDtypeStruct((B, S, LANES), F32), jax.ShapeDtypeStruct((B, GDN_HEADS, 1, S), F32)],
        compiler_params=_params("arbitrary"),
        name="even_gates",
    )(zs, par)


def _split3(x):
    hi = x.astype(BF16).astype(F32)
    mid = (x - hi).astype(BF16).astype(F32)
    lo = (x - hi - mid).astype(BF16).astype(F32)
    return hi, mid, lo


def _interleave(gens):
    out = [None] * len(gens)
    live = list(range(len(gens)))
    while live:
        for n in list(live):
            try:
                next(gens[n])
            except StopIteration as stop:
                out[n] = stop.value
                live.remove(n)
    return out


def _attend(state, c, qa, ka_ref, vt_ref, key0, nkeys, keep=None):
    ks = pl.ds(pl.multiple_of(key0, ATT_BLOCK), nkeys)
    st = _dot_nt(ka_ref[ks, :], qa)
    yield
    if keep is not None:
        st = jnp.where(keep, st, NEG)
    m_new = jnp.max(st, axis=0, keepdims=True)
    if state[c] is not None:
        m_old, acc_old = state[c]
        m_new = jnp.maximum(m_old, m_new)
    p = jnp.exp2(st - m_new).astype(BF16)
    pv = jnp.dot(vt_ref[:, ks], p, preferred_element_type=F32)
    state[c] = (m_new, pv if state[c] is None else acc_old * jnp.exp2(m_old - m_new) + pv)


def _attend_tile_pair(i, prep, ka_ref, vt_ref, m_ref, acc_ref, o_ref):
    t = ATT_BLOCK
    chains = [(hd, c) for c in range(2) for hd in range(ATT_HEADS)]
    num = lambda hd, c: 2 * hd + c
    qa = _interleave(prep)

    def step(state, hd, c, which, **kw):
        return _attend(state, num(hd, c), qa[num(hd, c)][which], ka_ref.at[hd], vt_ref.at[hd], **kw)

    def save(state):
        m_ref[...] = jnp.stack([state[n][0] for n in range(len(chains))])
        acc_ref[...] = jnp.stack([state[n][1] for n in range(len(chains))])

    def past_blocks(key0, n_steps):
        state = {n: (m_ref[n], acc_ref[n]) for n in range(len(chains))}
        _interleave([step(state, hd, c, 1, key0=key0 + s * 2 * t, nkeys=2 * t)
                     for s in range(n_steps) for hd, c in chains])
        save(state)

    state = {n: None for n in range(len(chains))}
    keep = _causal_keep()
    _interleave([step(state, hd, c, 0, key0=(2 * i + c) * t, nkeys=t, keep=keep) for hd, c in chains]
                + [step(state, hd, 1, 1, key0=(2 * i) * t, nkeys=t) for hd in range(ATT_HEADS)])
    save(state)

    def four_blocks(g, _):
        past_blocks(g * (4 * t), 2)
        return 0

    lax.fori_loop(0, i // 2, four_blocks, 0)

    @pl.when((i & 1) != 0)
    def _():
        past_blocks((i // 2) * (4 * t), 1)

    for hd, c in chains:
        acc = acc_ref[num(hd, c)]
        out_t = acc[:HEAD_DIM, :] * (1.0 / acc[HEAD_DIM:HEAD_DIM + 1, :])
        o_ref[c * t:(c + 1) * t, hd * HEAD_DIM:(hd + 1) * HEAD_DIM] = out_t.T.astype(o_ref.dtype)


def _transposed(v):
    r = lax.broadcasted_iota(jnp.int32, (HEAD_DIM, HEAD_DIM), 0)
    c = lax.broadcasted_iota(jnp.int32, (HEAD_DIM, HEAD_DIM), 1)
    v_t = _dot_nt(jnp.where(r == c, 1.0, 0.0).astype(BF16), v).astype(BF16)
    return jnp.concatenate([v_t, jnp.ones((V_ROWS - HEAD_DIM, ATT_BLOCK), BF16)], axis=0)


def _causal_keep():
    key = lax.broadcasted_iota(jnp.int32, (ATT_BLOCK, ATT_BLOCK), 0)
    qry = lax.broadcasted_iota(jnp.int32, (ATT_BLOCK, ATT_BLOCK), 1)
    return key <= qry


_ATT_SCRATCH = lambda S: [pltpu.VMEM((ATT_HEADS, S, 2 * HEAD_DIM), BF16), pltpu.VMEM((ATT_HEADS, V_ROWS, S), BF16),
                          pltpu.VMEM((2 * ATT_HEADS, 1, ATT_BLOCK), F32),
                          pltpu.VMEM((2 * ATT_HEADS, V_ROWS, ATT_BLOCK), F32)]


def _fox_kernel(q_ref, k_ref, v_ref, col_ref, qg_ref, kg_ref, o_ref, ka_ref, vt_ref, m_ref, acc_ref, *, S):
    h0 = pl.program_id(1) * ATT_HEADS
    i = pl.program_id(2)
    t = ATT_BLOCK
    D = HEAD_DIM
    lane = lax.broadcasted_iota(jnp.int32, (t, LANES), 1)

    @pl.when(i == 0)
    def _():
        def prep_keys(g, _):
            todo = []
            for u in range(KEY_BLOCKS_PER_STEP):
                sl = pl.ds(pl.multiple_of((g * KEY_BLOCKS_PER_STEP + u) * t, t), t)
                for hd in range(ATT_HEADS):
                    cols = slice(hd * D, (hd + 1) * D)
                    kn = _rms(k_ref[sl, cols].astype(F32), kg_ref[...]).astype(BF16)
                    hi, mid, lo = _split3(-LOG2E * _pick_lane(col_ref[sl, :], LANE_F + h0 + hd))
                    aug = jnp.where(lane < 3, 1.0, jnp.where(lane == 3, hi, jnp.where(lane == 4, mid,
                                                                                    jnp.where(lane == 5, lo, 0.0))))
                    todo.append((hd, sl, jnp.concatenate([kn, aug.astype(BF16)], axis=1), _transposed(v_ref[sl, cols])))
            for hd, sl, ka, vt in todo:
                ka_ref[hd, sl, :] = ka
                vt_ref[hd, :, sl] = vt
            return 0

        lax.fori_loop(0, S // (t * KEY_BLOCKS_PER_STEP), prep_keys, 0)

    def prep(hd, c):
        q = q_ref[c * t:(c + 1) * t, hd * D:(hd + 1) * D].astype(F32)
        qn = (_rms(q, qg_ref[...]) * (LOG2E * D ** -0.5)).astype(BF16)
        yield
        qsl = pl.ds(pl.multiple_of((2 * i + c) * t, t), t)
        hi, mid, lo = _split3(LOG2E * _pick_lane(col_ref[qsl, :], LANE_F + h0 + hd))
        aug = jnp.where(lane == 0, hi,
                        jnp.where(lane == 1, mid, jnp.where(lane == 2, lo, jnp.where(lane < 6, 1.0, 0.0))))
        qa = jnp.concatenate([qn, aug.astype(BF16)], axis=-1)
        return qa, qa

    _attend_tile_pair(i, [prep(hd, c) for hd in range(ATT_HEADS) for c in range(2)],
                      ka_ref, vt_ref, m_ref, acc_ref, o_ref)


def fox_attention(z, col, q_gain, k_gain):
    B, S, _ = z.shape
    t = 2 * ATT_BLOCK
    w = ATT_HEADS * HEAD_DIM
    return pl.pallas_call(
        functools.partial(_fox_kernel, S=S),
        grid=(B, FOX_HEADS // ATT_HEADS, S // t),
        in_specs=[
            pl.BlockSpec((None, t, w), lambda b, h, i: (b, i, FOX_Q0 // ATT_HEADS + h)),
            pl.BlockSpec((None, S, w), lambda b, h, i: (b, 0, FOX_K0 // ATT_HEADS + h)),
            pl.BlockSpec((None, S, w), lambda b, h, i: (b, 0, FOX_V0 // ATT_HEADS + h)),
            pl.BlockSpec((None, S, LANES), lambda b, h, i: (b, 0, 0)),
            pl.BlockSpec((1, HEAD_DIM), lambda b, h, i: (0, 0)),
            pl.BlockSpec((1, HEAD_DIM), lambda b, h, i: (0, 0)),
        ],
        out_specs=pl.BlockSpec((None, t, w), lambda b, h, i: (b, i, h)),
        out_shape=jax.ShapeDtypeStruct((B, S, FOX_HEADS * HEAD_DIM), BF16),
        scratch_shapes=_ATT_SCRATCH(S),
        compiler_params=_params("parallel", "parallel", "arbitrary"),
        name="fox_attention",
    )(z, z, z, col, q_gain.reshape(1, -1), k_gain.reshape(1, -1))


def _unit_lower_inverse(m):
    n = m.shape[0]
    r = lax.broadcasted_iota(jnp.int32, (n, n), 0)
    c = lax.broadcasted_iota(jnp.int32, (n, n), 1)
    eye = (r == c).astype(F32)

    def same_block(b):
        return (r // b) == (c // b)

    p = jnp.where(same_block(INV_BLOCK), m, 0.0)
    inv = eye - p
    k = 2
    while k < INV_BLOCK:
        pb = p.astype(BF16)
        p = jnp.dot(pb, pb, preferred_element_type=F32)
        yield
        inv = jnp.dot(inv.astype(BF16), (eye + p).astype(BF16), preferred_element_type=F32)
        yield
        k *= 2
    b = INV_BLOCK
    while b < n:
        off = jnp.where(same_block(2 * b), jnp.where(same_block(b), 0.0, m), 0.0).astype(BF16)
        ib = inv.astype(BF16)
        left = jnp.dot(ib, off, preferred_element_type=F32).astype(BF16)
        yield
        inv = inv - jnp.dot(left, ib, preferred_element_type=F32)
        yield
        b *= 2
    return inv


def _gdn_kernel(q_ref, k_ref, v_ref, gg_ref, wq_ref, wk_ref, wv_ref, col_ref, row_ref, on_ref,
                o_ref, qs_ref, ks_ref, vs_ref, *, S, rows, hb):
    h0 = pl.program_id(1) * hb
    C = GDN_CHUNK
    D = HEAD_DIM

    streams = [(x_ref, w_ref, dst_ref, hh, mode) for hh in range(hb)
               for x_ref, w_ref, dst_ref, mode in ((q_ref, wq_ref, qs_ref, "q"), (k_ref, wk_ref, ks_ref, "k"),
                                                   (v_ref, wv_ref, vs_ref, "v"))]
    halo = 16

    def conv_step(base, windows):
        outs = []
        for (x_ref, w_ref, dst_ref, hh, mode), win in zip(streams, windows):
            w = w_ref[:, hh * D:(hh + 1) * D]
            y = jnp.zeros((rows, D), F32)
            for tap in range(CONV_WIDTH):
                lead = halo - (CONV_WIDTH - 1) + tap
                y = y + w[tap:tap + 1, :] * pltpu.roll(win, rows + halo - lead, 0)[0:rows, :]
            y = y / (1.0 + jnp.exp(-y))
            if mode != "v":
                y = y * lax.rsqrt(jnp.sum(y * y, axis=-1, keepdims=True) + EPS)
            if mode == "q":
                y = y * D ** -0.5
            outs.append(y)
        for (x_ref, w_ref, dst_ref, hh, mode), y in zip(streams, outs):
            dst_ref[hh, pl.ds(base, rows), :] = y.astype(dst_ref.dtype)

    conv_step(0, [jnp.concatenate([jnp.zeros((halo, D), F32), x_ref[0:rows, hh * D:(hh + 1) * D].astype(F32)], axis=0)
                  for x_ref, _, _, hh, _ in streams])

    def conv_rest(n, _):
        base = pl.multiple_of(n * rows, rows)
        conv_step(base, [x_ref[pl.ds(pl.multiple_of(base - halo, halo), rows + halo), hh * D:(hh + 1) * D].astype(F32)
                         for x_ref, _, _, hh, _ in streams])
        return 0

    lax.fori_loop(1, S // rows, conv_rest, 0)

    r = lax.broadcasted_iota(jnp.int32, (C, C), 0)
    c = lax.broadcasted_iota(jnp.int32, (C, C), 1)
    incl = r >= c
    strict = r > c

    def chunk_local(hh, sl, tab):
        q = qs_ref[hh, sl, :].astype(F32)
        k = ks_ref[hh, sl, :].astype(F32)
        v = vs_ref[hh, sl, :].astype(F32)
        gcol = _pick_lane(tab, LANE_A + h0 + hh)
        beta = _pick_lane(tab, LANE_B + h0 + hh)
        grow = row_ref[hh, :, sl]
        glast = gcol[C - 1:C, :]
        decay = jnp.where(incl, jnp.exp(jnp.where(incl, gcol - grow, 0.0)), 0.0)
        eg = jnp.exp(gcol)
        kb = k * beta
        kbf = k.astype(BF16)
        m = jnp.where(strict, _dot_nt(kb.astype(BF16), kbf) * decay, 0.0)
        attn = (_dot_nt(q.astype(BF16), kbf) * decay).astype(BF16)
        yield
        tinv = (yield from _unit_lower_inverse(m)).astype(BF16)
        rhs = jnp.concatenate([v * beta, kb * eg], axis=-1).astype(BF16)
        sol = jnp.dot(tinv, rhs, preferred_element_type=F32)
        yield
        gate = gg_ref[sl, hh * D:(hh + 1) * D].astype(F32)
        return dict(u=sol[:, :D], w=sol[:, D:].astype(BF16), attn=attn, qg=(q * eg).astype(BF16),
                    kg_t=(k * jnp.exp(glast - gcol)).T.astype(BF16), keep=jnp.exp(glast),
                    gate=gate / (1.0 + jnp.exp(-gate)))

    def chunk_state(parts, state):
        outs = []
        for c in parts:
            sb = state.astype(BF16)
            v_new = c["u"] - jnp.dot(c["w"], sb, preferred_element_type=F32)
            o_state = jnp.dot(c["qg"], sb, preferred_element_type=F32)
            yield
            vb = v_new.astype(BF16)
            o = o_state + jnp.dot(c["attn"], vb, preferred_element_type=F32)
            state = state * c["keep"] + jnp.dot(c["kg_t"], vb, preferred_element_type=F32)
            yield
            outs.append((_rms(o, on_ref[...]) * c["gate"]).astype(o_ref.dtype))
        return state, outs

    def chunk_group(n, states):
        sls = [pl.ds(pl.multiple_of((n * GDN_CHUNKS_PER_STEP + g) * C, C), C) for g in range(GDN_CHUNKS_PER_STEP)]
        tabs = [col_ref[sl, :] for sl in sls]
        parts = _interleave([chunk_local(hh, sls[g], tabs[g]) for g in range(GDN_CHUNKS_PER_STEP) for hh in range(hb)])
        res = _interleave([chunk_state([parts[g * hb + hh] for g in range(GDN_CHUNKS_PER_STEP)], states[hh])
                           for hh in range(hb)])
        for g in range(GDN_CHUNKS_PER_STEP):
            o_ref[sls[g], :] = jnp.concatenate([outs[g] for _, outs in res], axis=-1)
        return tuple(s for s, _ in res)

    lax.fori_loop(0, S // (C * GDN_CHUNKS_PER_STEP), chunk_group, tuple(jnp.zeros((D, D), F32) for _ in range(hb)))


def gdn_mixer(z, conv_w, col, row, o_gain):
    B, S, _ = z.shape
    D = HEAD_DIM
    hb = GDN_HEADS_PER_STEP
    rows = 256
    seq = lambda off: pl.BlockSpec((None, S, hb * D), lambda b, h: (b, 0, off // hb + h),
                                   pipeline_mode=pl.Buffered(1))
    cw = lambda off: pl.BlockSpec((CONV_WIDTH, hb * D), lambda b, h: (0, off // hb + h))
    return pl.pallas_call(
        functools.partial(_gdn_kernel, S=S, rows=rows, hb=hb),
        grid=(B, GDN_HEADS // hb),
        in_specs=[
            seq(GDN_Q0), seq(GDN_K0), seq(GDN_V0), seq(GDN_G0),
            cw(0), cw(GDN_HEADS), cw(2 * GDN_HEADS),
            pl.BlockSpec((None, S, LANES), lambda b, h: (b, 0, 0)),
            pl.BlockSpec((None, hb, 1, S), lambda b, h: (b, h, 0, 0)),
            pl.BlockSpec((1, D), lambda b, h: (0, 0)),
        ],
        out_specs=pl.BlockSpec((None, S, hb * D), lambda b, h: (b, 0, h)),
        out_shape=jax.ShapeDtypeStruct((B, S, GDN_HEADS * D), BF16),
        scratch_shapes=[pltpu.VMEM((hb, S, D), BF16)] * 3,
        compiler_params=_params("parallel", "parallel"),
        name="gdn_mixer",
    )(z, z, z, z, conv_w, conv_w, conv_w, col, row, o_gain.reshape(1, D))


def _rope(x, cos, sin_signed):
    return x * cos + pltpu.roll(x, HEAD_DIM // 2, 1) * sin_signed


def _moba_kernel(q_ref, k_ref, v_ref, cos_ref, sin_ref, qg_ref, kg_ref, o_ref,
                 ka_ref, vt_ref, m_ref, acc_ref, kmean_ref, *, S):
    i = pl.program_id(2)
    t = ATT_BLOCK
    D = HEAD_DIM
    lane = lax.broadcasted_iota(jnp.int32, (t, LANES), 1)

    @pl.when(i == 0)
    def _():
        kmean_ref[...] = jnp.zeros(kmean_ref.shape, F32)

        def prep_keys(g, _):
            todo = []
            for u in range(KEY_BLOCKS_PER_STEP):
                n = g * KEY_BLOCKS_PER_STEP + u
                sl = pl.ds(pl.multiple_of(n * t, t), t)
                onehot = jnp.where(lane == n, 1.0, 0.0).astype(BF16)
                for hd in range(ATT_HEADS):
                    cols = slice(hd * D, (hd + 1) * D)
                    k = _rope(_rms(k_ref[sl, cols].astype(F32), kg_ref[...]), cos_ref[sl, :], sin_ref[sl, :])
                    todo.append((hd, n, sl, jnp.concatenate([k.astype(BF16), onehot], axis=1),
                                 jnp.mean(k, axis=0, keepdims=True), _transposed(v_ref[sl, cols])))
            for hd, n, sl, ka, kmean, vt in todo:
                ka_ref[hd, sl, :] = ka
                kmean_ref[hd, pl.ds(n, 1), :] = kmean
                vt_ref[hd, :, sl] = vt
            return 0

        lax.fori_loop(0, S // (t * KEY_BLOCKS_PER_STEP), prep_keys, 0)

    lane_f = lane.astype(F32)

    def prep(hd, c):
        cur = 2 * i + c
        qsl = pl.ds(pl.multiple_of(cur * t, t), t)
        q = _rope(_rms(q_ref[c * t:(c + 1) * t, hd * D:(hd + 1) * D].astype(F32), qg_ref[...]),
                  cos_ref[qsl, :], sin_ref[qsl, :])
        yield
        kmean = kmean_ref[hd]
        km_hi = kmean.astype(BF16)
        km_split = jnp.concatenate([km_hi, (kmean - km_hi.astype(F32)).astype(BF16)], axis=0)
        q_hi = q.astype(BF16)
        q_lo = (q - q_hi.astype(F32)).astype(BF16)
        part = _dot_nt(q_hi, km_split)
        gate = part[:, :LANES] + part[:, LANES:] + _dot_nt(q_lo, km_split[:LANES, :])
        gate = jnp.where(lane < cur, gate, -jnp.inf)
        yield
        sel_bias = jnp.full((t, LANES), NEG, F32)
        for _ in range(MOBA_TOPK):
            top = jnp.max(gate, axis=-1, keepdims=True)
            yield
            first = jnp.min(jnp.where(gate == top, lane_f, float(LANES)), axis=-1, keepdims=True)
            yield
            pick = lane_f == first
            sel_bias = jnp.where(pick & (first < cur.astype(F32)), 0.0, sel_bias)
            gate = jnp.where(pick, -jnp.inf, gate)
        qs = (q * (LOG2E * D ** -0.5)).astype(BF16)
        return (jnp.concatenate([qs, jnp.zeros((t, LANES), BF16)], axis=-1),
                jnp.concatenate([qs, sel_bias.astype(BF16)], axis=-1))

    _attend_tile_pair(i, [prep(hd, c) for hd in range(ATT_HEADS) for c in range(2)],
                      ka_ref, vt_ref, m_ref, acc_ref, o_ref)


def moba_attention(z, cos, sin_signed, q_gain, k_gain):
    B, S, W = z.shape
    H = W // (3 * HEAD_DIM)
    assert ATT_BLOCK == MOBA_BLOCK
    t = 2 * ATT_BLOCK
    w = ATT_HEADS * HEAD_DIM
    hp = H // ATT_HEADS
    return pl.pallas_call(
        functools.partial(_moba_kernel, S=S),
        grid=(B, hp, S // t),
        in_specs=[
            pl.BlockSpec((None, t, w), lambda b, h, i: (b, i, h)),
            pl.BlockSpec((None, S, w), lambda b, h, i: (b, 0, hp + h)),
            pl.BlockSpec((None, S, w), lambda b, h, i: (b, 0, 2 * hp + h)),
            pl.BlockSpec((S, HEAD_DIM), lambda b, h, i: (0, 0)),
            pl.BlockSpec((S, HEAD_DIM), lambda b, h, i: (0, 0)),
            pl.BlockSpec((1, HEAD_DIM), lambda b, h, i: (0, 0)),
            pl.BlockSpec((1, HEAD_DIM), lambda b, h, i: (0, 0)),
        ],
        out_specs=pl.BlockSpec((None, t, w), lambda b, h, i: (b, i, h)),
        out_shape=jax.ShapeDtypeStruct((B, S, H * HEAD_DIM), BF16),
        scratch_shapes=_ATT_SCRATCH(S) + [pltpu.VMEM((ATT_HEADS, LANES, HEAD_DIM), F32)],
        compiler_params=_params("parallel", "parallel", "arbitrary"),
        name="moba_attention",
    )(z, z, z, cos, sin_signed, q_gain.reshape(1, -1), k_gain.reshape(1, -1))


META_IDX, META_RANK, META_GATE = 0, 2, 4


def _router_kernel(x_ref, *refs, n_proj):
    proj_refs = refs[:2 * n_proj]
    g_ref, w_ref, y_ref, xn_ref, meta_ref, cnt_ref, carry_ref = refs[2 * n_proj:]
    i = pl.program_id(0)
    tm = x_ref.shape[0]

    @pl.when(i == 0)
    def _():
        carry_ref[...] = jnp.zeros(carry_ref.shape, F32)

    w = w_ref[...]
    w_hi = w.astype(BF16)
    w_lo = (w - w_hi.astype(F32)).astype(BF16)
    part = tm // ROUTER_CHAINS
    lane = lax.broadcasted_iota(jnp.int32, (part, LANES), 1)
    lane_f = lane.astype(F32)

    def route(n):
        rows = slice(n * part, (n + 1) * part)
        y = _plus_projections(x_ref[rows, :], proj_refs, rows)
        y_ref[rows, :] = y
        yield
        xn = _rms(y, g_ref[...])
        xn_ref[rows, :] = _pack_bf16_pairs(xn)
        x_hi = xn.astype(BF16)
        x_lo = (xn - x_hi.astype(F32)).astype(BF16)
        logits = (jnp.dot(x_hi, w_hi, preferred_element_type=F32) + jnp.dot(x_hi, w_lo, preferred_element_type=F32)
                  + jnp.dot(x_lo, w_hi, preferred_element_type=F32))
        yield
        logits = jnp.where(lane < N_EXPERTS, logits, -jnp.inf)
        top1 = jnp.max(logits, axis=-1, keepdims=True)
        yield
        idx1 = jnp.min(jnp.where(logits == top1, lane_f, float(LANES)), axis=-1, keepdims=True)
        yield
        rest = jnp.where(lane_f == idx1, -jnp.inf, logits)
        top2 = jnp.max(rest, axis=-1, keepdims=True)
        yield
        idx2 = jnp.min(jnp.where(rest == top2, lane_f, float(LANES)), axis=-1, keepdims=True)
        e2 = jnp.exp(top2 - top1)
        denom = 1.0 + e2
        return idx1, idx2, 1.0 / denom, e2 / denom, jnp.where((lane_f == idx1) | (lane_f == idx2), 1.0, 0.0)

    picks = _interleave([route(n) for n in range(ROUTER_CHAINS)])
    chosen = jnp.concatenate([p[4] for p in picks], axis=0)
    r = lax.broadcasted_iota(jnp.int32, (tm, tm), 0)
    c = lax.broadcasted_iota(jnp.int32, (tm, tm), 1)
    ahead = jnp.dot(jnp.where(r > c, 1.0, 0.0).astype(BF16), chosen.astype(BF16), preferred_element_type=F32)
    carry = carry_ref[...]
    for n, (idx1, idx2, g1, g2, _) in enumerate(picks):
        rows = slice(n * part, (n + 1) * part)
        rank = ahead[rows, :] + carry
        rank1 = jnp.sum(jnp.where(lane_f == idx1, rank, 0.0), axis=-1, keepdims=True)
        rank2 = jnp.sum(jnp.where(lane_f == idx2, rank, 0.0), axis=-1, keepdims=True)
        meta = jnp.zeros((part, LANES), F32)
        for k, v in enumerate((idx1, idx2, rank1, rank2, g1, g2)):
            meta = jnp.where(lane == k, v, meta)
        meta_ref[rows, :] = meta
    carry = carry + jnp.sum(chosen, axis=0, keepdims=True)
    carry_ref[...] = carry
    cnt_ref[...] = jnp.broadcast_to(carry, cnt_ref.shape)


def moe_router(x, proj, gain, w_router, *, tm):
    M, K = x.shape
    w = jnp.zeros((K, LANES), F32).at[:, :N_EXPERTS].set(w_router)
    proj_specs, proj_args = _projection_specs(proj, tm)
    return pl.pallas_call(
        functools.partial(_router_kernel, n_proj=len(proj)),
        grid=(M // tm,),
        in_specs=[
            pl.BlockSpec((tm, K), lambda i: (i, 0)),
            *proj_specs,
            pl.BlockSpec((1, K), lambda i: (0, 0)),
            pl.BlockSpec((K, LANES), lambda i: (0, 0)),
        ],
        out_specs=[
            pl.BlockSpec((tm, K), lambda i: (i, 0)),
            pl.BlockSpec((tm, K // 2), lambda i: (i, 0)),
            pl.BlockSpec((tm, LANES), lambda i: (i, 0)),
            pl.BlockSpec((8, LANES), lambda i: (0, 0)),
        ],
        out_shape=[
            jax.ShapeDtypeStruct((M, K), F32),
            jax.ShapeDtypeStruct((M, K // 2), jnp.int32),
            jax.ShapeDtypeStruct((M, LANES), F32),
            jax.ShapeDtypeStruct((8, LANES), F32),
        ],
        scratch_shapes=[pltpu.VMEM((1, LANES), F32)],
        compiler_params=_params("arbitrary"),
        name="moe_router",
    )(x, *proj_args, gain.reshape(1, K), w)


def _sc_workers():
    info = plsc.get_sparse_core_info()
    return info.num_cores, info.num_cores * info.num_subcores


def scatter_rows(rows, dest, n_out):
    M, W = rows.shape
    nc, nw = _sc_workers()
    assert M % (nw * SC_ROWS) == 0
    per_w = M // nw
    mesh = plsc.VectorSubcoreMesh(core_axis_name="c", subcore_axis_name="s")

    @functools.partial(
        pl.kernel, mesh=mesh, out_type=jax.ShapeDtypeStruct((n_out, W), rows.dtype),
        scratch_types=[pltpu.VMEM((SC_ROWS,), jnp.int32), pltpu.VMEM((SC_ROWS, W), rows.dtype),
                       pltpu.SemaphoreType.DMA])
    def kern(rows_hbm, dest_hbm, out_hbm, idx_v, rows_v, sem):
        wid = lax.axis_index("s") * nc + lax.axis_index("c")

        @pl.loop(0, per_w // SC_ROWS)
        def _(g):
            base = wid * per_w + g * SC_ROWS
            pltpu.sync_copy(rows_hbm.at[pl.ds(base, SC_ROWS)], rows_v)
            for k in range(2):
                pltpu.sync_copy(dest_hbm.at[k, pl.ds(base, SC_ROWS)], idx_v)
                pltpu.async_copy(rows_v, out_hbm.at[idx_v], sem).wait()

    return kern(rows, dest)


def gather_rows(table, idx):
    N = idx.shape[0]
    W = table.shape[1]
    nc, nw = _sc_workers()
    assert N % (nw * SC_ROWS) == 0
    per_w = N // nw
    mesh = plsc.VectorSubcoreMesh(core_axis_name="c", subcore_axis_name="s")

    @functools.partial(
        pl.kernel, mesh=mesh, out_type=jax.ShapeDtypeStruct((N, W), table.dtype),
        scratch_types=[pltpu.VMEM((SC_ROWS,), jnp.int32), pltpu.VMEM((SC_ROWS, W), table.dtype),
                       pltpu.SemaphoreType.DMA])
    def kern(table_hbm, idx_hbm, out_hbm, idx_v, rows_v, sem):
        wid = lax.axis_index("s") * nc + lax.axis_index("c")

        @pl.loop(0, per_w // SC_ROWS)
        def _(g):
            base = wid * per_w + g * SC_ROWS
            pltpu.sync_copy(idx_hbm.at[pl.ds(base, SC_ROWS)], idx_v)
            pltpu.async_copy(table_hbm.at[idx_v], rows_v, sem).wait()
            pltpu.sync_copy(rows_v, out_hbm.at[pl.ds(base, SC_ROWS)])

    return kern(table, idx)


def _combine_kernel(h_ref, y1_ref, y2_ref, meta_ref, o_ref):
    g1 = meta_ref[:, META_GATE:META_GATE + 1]
    g2 = meta_ref[:, META_GATE + 1:META_GATE + 2]
    o_ref[...] = (h_ref[...] + g1 * _unpack_bf16_pairs(y1_ref[...]).astype(F32)
                  + g2 * _unpack_bf16_pairs(y2_ref[...]).astype(F32))


def moe_combine(h, y_pairs, meta):
    M, K = h.shape
    p = PERM_TILE
    nt = M // p
    return pl.pallas_call(
        _combine_kernel,
        grid=(nt,),
        in_specs=[
            pl.BlockSpec((p, K), lambda i: (i, 0)),
            pl.BlockSpec((p, K // 2), lambda i: (i, 0)),
            pl.BlockSpec((p, K // 2), lambda i: (nt + i, 0)),
            pl.BlockSpec((p, LANES), lambda i: (i, 0)),
        ],
        out_specs=pl.BlockSpec((p, K), lambda i: (i, 0)),
        out_shape=jax.ShapeDtypeStruct((M, K), F32),
        compiler_params=_params("parallel"),
        name="moe_combine",
    )(h, y_pairs, y_pairs, meta)


def _expert_ffn_kernel(te_ref, tv_ref, x_ref, wg_ref, wu_ref, wd_ref, o_ref, xb_ref, acc_ref):
    del te_ref
    i = pl.program_id(0)
    j = pl.program_id(1)
    tm = x_ref.shape[0]
    valid = tv_ref[i]

    def swiglu_rows(rows):
        @pl.when(j == 0)
        def _():
            xb_ref[0:rows, :] = _unpack_bf16_pairs(x_ref[0:rows, :])

        x = xb_ref[0:rows, :]
        gt = jnp.dot(x, wg_ref[...], preferred_element_type=F32)
        up = jnp.dot(x, wu_ref[...], preferred_element_type=F32)
        act = (gt / (1.0 + jnp.exp(-gt)) * up).astype(BF16)
        part = jnp.dot(act, wd_ref[...], preferred_element_type=F32)

        @pl.when(j == 0)
        def _():
            acc_ref[0:rows, :] = part

        @pl.when(j > 0)
        def _():
            acc_ref[0:rows, :] += part

        @pl.when(j == pl.num_programs(1) - 1)
        def _():
            o_ref[0:rows, :] = _pack_bf16_pairs(acc_ref[0:rows, :])
            if rows < tm:
                o_ref[rows:tm, :] = jnp.zeros((tm - rows, o_ref.shape[1]), o_ref.dtype)

    part = tm // EXPERT_ROW_PARTS
    for n in range(1, EXPERT_ROW_PARTS + 1):
        pl.when((valid > (n - 1) * part) & (valid <= n * part))(functools.partial(swiglu_rows, n * part))

    @pl.when((valid == 0) & (j == 0))
    def _():
        o_ref[...] = jnp.zeros(o_ref.shape, o_ref.dtype)


def expert_ffn(xs, tile_expert, tile_valid, w_gate_up, w_down, *, tm, tf):
    R = xs.shape[0]
    E, F, K = w_down.shape
    nf = F // tf
    live = lambda i, tv: tv[i] > 0
    col = lambda i, j, tv: jnp.where(live(i, tv), j, nf - 1)
    grid_spec = pltpu.PrefetchScalarGridSpec(
        num_scalar_prefetch=2,
        grid=(R // tm, nf),
        in_specs=[
            pl.BlockSpec((tm, K // 2), lambda i, j, te, tv: (te[R // tm + i], 0)),
            pl.BlockSpec((None, K, tf), lambda i, j, te, tv: (te[i], 0, col(i, j, tv))),
            pl.BlockSpec((None, K, tf), lambda i, j, te, tv: (te[i], 0, nf + col(i, j, tv))),
            pl.BlockSpec((None, tf, K), lambda i, j, te, tv: (te[i], col(i, j, tv), 0)),
        ],
        out_specs=pl.BlockSpec((tm, K // 2), lambda i, j, te, tv: (i, 0)),
        scratch_shapes=[pltpu.VMEM((tm, K), BF16), pltpu.VMEM((tm, K), F32)],
    )
    return pl.pallas_call(
        _expert_ffn_kernel,
        grid_spec=grid_spec,
        out_shape=jax.ShapeDtypeStruct((R, K // 2), jnp.int32),
        compiler_params=_params("arbitrary", "arbitrary"),
        name="moe_expert_ffn",
    )(tile_expert, tile_valid, xs, w_gate_up, w_gate_up, w_down)


def moe_residual(h, proj, gain, w_router, w_gate_up, w_down):
    M, K = h.shape
    p = PERM_TILE
    tm = EXPERT_TILE
    n_rows = 2 * M + N_EXPERTS * tm
    h, xn, meta, cnt = moe_router(h, proj, gain, w_router, tm=p)

    counts = cnt[0, :N_EXPERTS].astype(jnp.int32)
    padded = (counts + tm - 1) // tm * tm
    ends = jnp.cumsum(padded)
    offsets = ends - padded
    n_tiles = n_rows // tm
    tile_row = jnp.arange(n_tiles) * tm
    n_used = ends[-1] // tm
    last_used = jnp.minimum(tile_row // tm, n_used - 1)
    expert_of = lambda row: jnp.minimum(jnp.sum(ends[None, :] <= row[:, None], axis=1), N_EXPERTS - 1)
    tile_expert = expert_of(last_used * tm)
    row_in_expert = tile_row - jnp.sum(jnp.where(tile_expert[:, None] == jnp.arange(N_EXPERTS), offsets, 0), axis=1)
    own_count = jnp.sum(jnp.where(tile_expert[:, None] == jnp.arange(N_EXPERTS), counts, 0), axis=1)
    tile_valid = jnp.where(tile_row // tm < n_used, jnp.clip(own_count - row_in_expert, 0, tm), 0)
    tile_tables = jnp.concatenate([tile_expert, last_used]).astype(jnp.int32)

    idx = meta[:, META_IDX:META_IDX + 2].astype(jnp.int32)
    rank = meta[:, META_RANK:META_RANK + 2].astype(jnp.int32)
    dest = (jnp.sum(jnp.where(idx[:, :, None] == jnp.arange(N_EXPERTS), offsets, 0), axis=-1) + rank).T

    xs = scatter_rows(xn, dest, n_rows)
    ys = expert_ffn(xs, tile_tables, tile_valid.astype(jnp.int32), w_gate_up, w_down, tm=tm, tf=EXPERT_COLS)
    return moe_combine(h, gather_rows(ys, dest.reshape(-1)), meta)


def _ffn_kernel(x_ref, *refs, n_proj):
    proj_refs = refs[:2 * n_proj]
    g_ref, wg_ref, wu_ref, wd_ref, o_ref, xn_ref, acc_ref = refs[2 * n_proj:]
    j = pl.program_id(1)

    @pl.when(j == 0)
    def _():
        x = _plus_projections(x_ref[...], proj_refs)
        xn_ref[...] = _rms(x, g_ref[...]).astype(BF16)
        acc_ref[...] = x

    xn = xn_ref[...]
    gt = jnp.dot(xn, wg_ref[...], preferred_element_type=F32)
    up = jnp.dot(xn, wu_ref[...], preferred_element_type=F32)
    act = gt / (1.0 + jnp.exp(-gt)) * up
    acc_ref[...] += jnp.dot(act.astype(BF16), wd_ref[...], preferred_element_type=F32)

    @pl.when(j == pl.num_programs(1) - 1)
    def _():
        o_ref[...] = acc_ref[...]


def ffn_residual(x, proj, gain, w_gate_up, w_down, *, tm, tf):
    M, K = x.shape
    F = w_down.shape[0]
    nf = F // tf
    proj_specs, proj_args = _projection_specs(proj, tm)
    return pl.pallas_call(
        functools.partial(_ffn_kernel, n_proj=len(proj)),
        grid=(M // tm, nf),
        in_specs=[
            pl.BlockSpec((tm, K), lambda i, j: (i, 0)),
            *proj_specs,
            pl.BlockSpec((1, K), lambda i, j: (0, 0)),
            pl.BlockSpec((K, tf), lambda i, j: (0, j)),
            pl.BlockSpec((K, tf), lambda i, j: (0, nf + j)),
            pl.BlockSpec((tf, K), lambda i, j: (j, 0)),
        ],
        out_specs=pl.BlockSpec((tm, K), lambda i, j: (i, 0)),
        out_shape=jax.ShapeDtypeStruct((M, K), F32),
        scratch_shapes=[pltpu.VMEM((tm, K), BF16), pltpu.VMEM((tm, K), F32)],
        compiler_params=_params("parallel", "arbitrary"),
        name="ffn_residual",
    )(x, *proj_args, gain.reshape(1, K), w_gate_up, w_gate_up, w_down)


def _even_mix(h, B, S, norm1, w_in, f_bias, q_norm, k_norm, conv_w, a_log, dt_bias, o_norm, w_out):
    M, D = h.shape
    fw, gw = FOX_HEADS * HEAD_DIM, GDN_HEADS * HEAD_DIM
    o_ff = 3 * fw
    o_gq = o_ff + FOX_HEADS
    o_ga = o_gq + 3 * gw
    o_gb = o_ga + GDN_HEADS
    o_gg = o_gb + GDN_HEADS
    w_big = jnp.concatenate([w_in[:, :o_ff], w_in[:, o_gq:o_ga], w_in[:, o_gg:]], axis=1)
    w_small = jnp.concatenate([w_in[:, o_ff:o_gq], w_in[:, o_ga:o_gg],
                               jnp.zeros((D, LANES - FOX_HEADS - 2 * GDN_HEADS), F32)], axis=1)
    w_big, w_small = (t.astype(BF16) for t in lax.optimization_barrier((w_big, w_small)))
    z, zs = norm_matmul(h, norm1, w_big, w_small, tm=ROW_TILE, tn=IN_PROJ_COLS)
    par = jnp.zeros((8, LANES), F32)
    par = par.at[0, LANE_F:LANE_F + FOX_HEADS].set(f_bias).at[0, LANE_A:LANE_A + GDN_HEADS].set(dt_bias)
    par = par.at[1, LANE_A:LANE_A + GDN_HEADS].set(a_log)
    col, row = even_gates(zs.reshape(B, S, LANES), par)
    z = z.reshape(B, S, -1)
    fox = fox_attention(z, col, q_norm, k_norm)
    gdn = gdn_mixer(z, conv_w, col, row, o_norm)
    w_out = w_out.astype(BF16)
    return [(fox.reshape(M, fw), w_out[:fw]), (gdn.reshape(M, gw), w_out[fw:])]


def _odd_mix(h, B, S, norm1, w_qkv, q_norm, k_norm, w_out):
    M, D = h.shape
    z = norm_matmul(h, norm1, w_qkv.astype(BF16), tm=ROW_TILE, tn=QKV_COLS).reshape(B, S, -1)
    half = HEAD_DIM // 2
    inv = jnp.power(ROPE_THETA, -jnp.arange(half, dtype=F32) / half)
    ang = jnp.arange(S, dtype=F32)[:, None] * inv[None, :]
    cos, sin = jnp.cos(ang), jnp.sin(ang)
    cos_full = jnp.concatenate([cos, cos], axis=-1)
    sin_signed = jnp.concatenate([-sin, sin], axis=-1)
    att = moba_attention(z, cos_full, sin_signed, q_norm, k_norm)
    return [(att.reshape(M, -1), w_out.astype(BF16))]


def kernel(x, e_norm1, e_w_in, e_fox_f_bias, e_fox_q_norm, e_fox_k_norm, e_gdn_conv, e_gdn_a_log,
           e_gdn_dt_bias, e_gdn_o_norm, e_w_out, e_norm2, e_ffn_w_gate_up, e_ffn_w_down,
           o_norm1, o_w_qkv, o_q_norm, o_k_norm, o_w_out, o_norm2, o_router, o_exp_w_gate_up, o_exp_w_down):
    B, S, D = x.shape
    h = x.reshape(B * S, D)
    depth = e_norm1.shape[0] + o_norm1.shape[0]
    for layer in range(depth):
        i = layer // 2
        if layer % 2 == 0:
            mix = _even_mix(h, B, S, e_norm1[i], e_w_in[i], e_fox_f_bias[i], e_fox_q_norm[i], e_fox_k_norm[i],
                            e_gdn_conv[i], e_gdn_a_log[i], e_gdn_dt_bias[i], e_gdn_o_norm[i], e_w_out[i])
            h = ffn_residual(h, mix, e_norm2[i], e_ffn_w_gate_up[i].astype(BF16), e_ffn_w_down[i].astype(BF16),
                             tm=FFN_ROWS, tf=FFN_COLS)
        else:
            mix = _odd_mix(h, B, S, o_norm1[i], o_w_qkv[i], o_q_norm[i], o_k_norm[i], o_w_out[i])
            h = moe_residual(h, mix, o_norm2[i], o_router[i], o_exp_w_gate_up[i].astype(BF16),
                             o_exp_w_down[i].astype(BF16))
    return h.reshape(B, S, D)
```

```python
import functools

import jax
import jax.numpy as jnp
from jax import lax
from jax.experimental import pallas as pl
from jax.experimental.pallas import tpu as pltpu
from jax.experimental.pallas import tpu_sc as plsc

F32 = jnp.float32
BF16 = jnp.bfloat16

HEAD_DIM = 128
FOX_HEADS = 4
GDN_HEADS = 4
CONV_WIDTH = 4
MOBA_BLOCK = 256
MOBA_TOPK = 3
N_EXPERTS = 8
ROPE_THETA = 10000.0
EPS = 1e-6

LANES = 128
GDN_CHUNK = 128
INV_BLOCK = 16
GDN_HEADS_PER_STEP = 4
GDN_CHUNKS_PER_STEP = 4
NEG = -(2.0 ** 100)
LOG2E = 1.4426950408889634
ATT_BLOCK = 256
ATT_HEADS = 4
KEY_BLOCKS_PER_STEP = 2
V_ROWS = HEAD_DIM + 16
PERM_TILE = 1024
ROUTER_CHAINS = 4
SC_ROWS = 64
ROW_TILE = 1024
MXU_COLS = 256
IN_PROJ_COLS = 7 * MXU_COLS
QKV_COLS = 6 * MXU_COLS
FFN_ROWS, FFN_COLS = 512, 1408
EXPERT_TILE = 1024
EXPERT_COLS = 896
EXPERT_ROW_PARTS = 4
VMEM_LIMIT_BYTES = 56 * 1024 * 1024

FOX_Q0, FOX_K0, FOX_V0 = 0, 4, 8
GDN_Q0, GDN_K0, GDN_V0, GDN_G0 = 12, 16, 20, 24
LANE_F, LANE_A, LANE_B = 0, 4, 8


def _params(*sem):
    return pltpu.CompilerParams(dimension_semantics=sem, vmem_limit_bytes=VMEM_LIMIT_BYTES)


def _rms(x, gain):
    return x * lax.rsqrt(jnp.mean(x * x, axis=-1, keepdims=True) + EPS) * gain


def _dot_nt(a, b, **kw):
    return lax.dot_general(a, b, (((1,), (1,)), ((), ())), preferred_element_type=F32, **kw)


def _pick_lane(x, lane_idx):
    lane = lax.broadcasted_iota(jnp.int32, x.shape, 1)
    return jnp.sum(jnp.where(lane == lane_idx, x, 0.0), axis=-1, keepdims=True)


def _pack_bf16_pairs(x):
    n = x.shape[1] // 2
    hi = pltpu.bitcast(x[:, :n].astype(BF16).astype(F32), jnp.uint32)
    lo = pltpu.bitcast(x[:, n:].astype(BF16).astype(F32), jnp.uint32)
    return pltpu.bitcast(hi | (lo >> 16), jnp.int32)


def _unpack_bf16_pairs(w):
    u = pltpu.bitcast(w, jnp.uint32)
    hi = pltpu.bitcast(u & jnp.uint32(0xFFFF0000), F32).astype(BF16)
    lo = pltpu.bitcast(u << 16, F32).astype(BF16)
    return jnp.concatenate([hi, lo], axis=1)


def _norm_mm_kernel(x_ref, g_ref, w_ref, *rest, has_aux):
    if has_aux:
        waux_ref, o_ref, oaux_ref, xn_ref = rest
    else:
        o_ref, xn_ref = rest

    @pl.when(pl.program_id(1) == 0)
    def _():
        xn = _rms(x_ref[...], g_ref[...]).astype(BF16)
        xn_ref[...] = xn
        if has_aux:
            oaux_ref[...] = jnp.dot(xn, waux_ref[...], preferred_element_type=F32)

    o_ref[...] = jnp.dot(xn_ref[...], w_ref[...], preferred_element_type=F32).astype(o_ref.dtype)


def norm_matmul(x, gain, w, w_aux=None, *, tm, tn, out_dtype=BF16):
    M, K = x.shape
    N = w.shape[1]
    has_aux = w_aux is not None
    in_specs = [
        pl.BlockSpec((tm, K), lambda i, j: (i, 0)),
        pl.BlockSpec((1, K), lambda i, j: (0, 0)),
        pl.BlockSpec((K, tn), lambda i, j: (0, j)),
    ]
    out_shape = [jax.ShapeDtypeStruct((M, N), out_dtype)]
    out_specs = [pl.BlockSpec((tm, tn), lambda i, j: (i, j))]
    args = [x, gain.reshape(1, K), w]
    if has_aux:
        in_specs.append(pl.BlockSpec((K, LANES), lambda i, j: (0, 0)))
        out_shape.append(jax.ShapeDtypeStruct((M, LANES), F32))
        out_specs.append(pl.BlockSpec((tm, LANES), lambda i, j: (i, 0)))
        args.append(w_aux)
    res = pl.pallas_call(
        functools.partial(_norm_mm_kernel, has_aux=has_aux),
        grid=(M // tm, N // tn),
        in_specs=in_specs,
        out_specs=out_specs,
        out_shape=out_shape,
        scratch_shapes=[pltpu.VMEM((tm, K), BF16)],
        compiler_params=_params("parallel", "arbitrary"),
        name="norm_matmul",
    )(*args)
    return res if has_aux else res[0]


def _plus_projections(x, proj_refs, rows=slice(None)):
    for a_ref, w_ref in zip(proj_refs[0::2], proj_refs[1::2]):
        x = x + jnp.dot(a_ref[rows, :], w_ref[...], preferred_element_type=F32)
    return x


def _projection_specs(pairs, tm):
    specs, args = [], []
    for a, w in pairs:
        specs.append(pl.BlockSpec((tm, a.shape[1]), lambda i, *_: (i, 0)))
        specs.append(pl.BlockSpec(w.shape, lambda i, *_: (0, 0)))
        args += [a, w]
    return specs, args


def _gate_kernel(zs_ref, par_ref, col_ref, row_ref, *, S, B):
    C = GDN_CHUNK
    bias = par_ref[0:1, :]
    neg_a = -jnp.exp(par_ref[1:2, :])
    r = lax.broadcasted_iota(jnp.int32, (C, C), 0)
    c = lax.broadcasted_iota(jnp.int32, (C, C), 1)
    tril = (r >= c).astype(F32)
    lane = lax.broadcasted_iota(jnp.int32, (C, LANES), 1)

    def body(n, carry):
        sl = pl.ds(pl.multiple_of(n * C, C), C)
        us, betas = [], []
        for bi in range(B):
            z = zs_ref[bi, sl, :]
            t = z + bias
            soft = jnp.log(1.0 + jnp.exp(-jnp.abs(t)))
            log_f = jnp.minimum(t, 0.0) - soft
            g = neg_a * (jnp.maximum(t, 0.0) + soft)
            betas.append(1.0 / (1.0 + jnp.exp(-z)))
            us.append(jnp.where(lane < LANE_A, log_f, jnp.where(lane < LANE_B, g, 0.0)))
        sums = jnp.dot(tril, jnp.concatenate(us, axis=1), preferred_element_type=F32, precision=lax.Precision.HIGHEST)
        last = []
        for bi in range(B):
            cs = sums[:, bi * LANES:(bi + 1) * LANES] + jnp.where(lane < LANE_A, carry[bi], 0.0)
            out = jnp.where(lane < LANE_B, cs, betas[bi])
            col_ref[bi, sl, :] = out
            out_t = out.T
            for hd in range(GDN_HEADS):
                row_ref[bi, hd, :, sl] = out_t[LANE_A + hd:LANE_A + hd + 1, :]
            last.append(cs[C - 1:C, :])
        return tuple(last)

    lax.fori_loop(0, S // C, body, tuple(jnp.zeros((1, LANES), F32) for _ in range(B)))


def even_gates(zs, par):
    B, S, _ = zs.shape
    return pl.pallas_call(
        functools.partial(_gate_kernel, S=S, B=B),
        grid=(1,),
        in_specs=[
            pl.BlockSpec((B, S, LANES), lambda i: (0, 0, 0)),
            pl.BlockSpec((8, LANES), lambda i: (0, 0)),
        ],
        out_specs=[
            pl.BlockSpec((B, S, LANES), lambda i: (0, 0, 0)),
            pl.BlockSpec((B, GDN_HEADS, 1, S), lambda i: (0, 0, 0, 0)),
        ],
        out_shape=[jax.ShapeDtypeStruct((B, S, LANES), F32), jax.ShapeDtypeStruct((B, GDN_HEADS, 1, S), F32)],
        compiler_params=_params("arbitrary"),
        name="even_gates",
    )(zs, par)


def _split3(x):
    hi = x.astype(BF16).astype(F32)
    mid = (x - hi).astype(BF16).astype(F32)
    lo = (x - hi - mid).astype(BF16).astype(F32)
    return hi, mid, lo


def _interleave(gens):
    out = [None] * len(gens)
    live = list(range(len(gens)))
    while live:
        for n in list(live):
            try:
                next(gens[n])
            except StopIteration as stop:
                out[n] = stop.value
                live.remove(n)
    return out


def _attend(state, c, qa, ka_ref, vt_ref, key0, nkeys, keep=None):
    ks = pl.ds(pl.multiple_of(key0, ATT_BLOCK), nkeys)
    st = _dot_nt(ka_ref[ks, :], qa)
    yield
    if keep is not None:
        st = jnp.where(keep, st, NEG)
    m_new = jnp.max(st, axis=0, keepdims=True)
    if state[c] is not None:
        m_old, acc_old = state[c]
        m_new = jnp.maximum(m_old, m_new)
    p = jnp.exp2(st - m_new).astype(BF16)
    pv = jnp.dot(vt_ref[:, ks], p, preferred_element_type=F32)
    state[c] = (m_new, pv if state[c] is None else acc_old * jnp.exp2(m_old - m_new) + pv)


def _attend_tile_pair(i, prep, ka_ref, vt_ref, m_ref, acc_ref, o_ref):
    t = ATT_BLOCK
    chains = [(hd, c) for c in range(2) for hd in range(ATT_HEADS)]
    num = lambda hd, c: 2 * hd + c
    qa = _interleave(prep)

    def step(state, hd, c, which, **kw):
        return _attend(state, num(hd, c), qa[num(hd, c)][which], ka_ref.at[hd], vt_ref.at[hd], **kw)

    def save(state):
        m_ref[...] = jnp.stack([state[n][0] for n in range(len(chains))])
        acc_ref[...] = jnp.stack([state[n][1] for n in range(len(chains))])

    def past_blocks(key0, n_steps):
        state = {n: (m_ref[n], acc_ref[n]) for n in range(len(chains))}
        _interleave([step(state, hd, c, 1, key0=key0 + s * 2 * t, nkeys=2 * t)
                     for s in range(n_steps) for hd, c in chains])
        save(state)

    state = {n: None for n in range(len(chains))}
    keep = _causal_keep()
    _interleave([step(state, hd, c, 0, key0=(2 * i + c) * t, nkeys=t, keep=keep) for hd, c in chains]
                + [step(state, hd, 1, 1, key0=(2 * i) * t, nkeys=t) for hd in range(ATT_HEADS)])
    save(state)

    def four_blocks(g, _):
        past_blocks(g * (4 * t), 2)
        return 0

    lax.fori_loop(0, i // 2, four_blocks, 0)

    @pl.when((i & 1) != 0)
    def _():
        past_blocks((i // 2) * (4 * t), 1)

    for hd, c in chains:
        acc = acc_ref[num(hd, c)]
        out_t = acc[:HEAD_DIM, :] * (1.0 / acc[HEAD_DIM:HEAD_DIM + 1, :])
        o_ref[c * t:(c + 1) * t, hd * HEAD_DIM:(hd + 1) * HEAD_DIM] = out_t.T.astype(o_ref.dtype)


def _transposed(v):
    r = lax.broadcasted_iota(jnp.int32, (HEAD_DIM, HEAD_DIM), 0)
    c = lax.broadcasted_iota(jnp.int32, (HEAD_DIM, HEAD_DIM), 1)
    v_t = _dot_nt(jnp.where(r == c, 1.0, 0.0).astype(BF16), v).astype(BF16)
    return jnp.concatenate([v_t, jnp.ones((V_ROWS - HEAD_DIM, ATT_BLOCK), BF16)], axis=0)


def _causal_keep():
    key = lax.broadcasted_iota(jnp.int32, (ATT_BLOCK, ATT_BLOCK), 0)
    qry = lax.broadcasted_iota(jnp.int32, (ATT_BLOCK, ATT_BLOCK), 1)
    return key <= qry


_ATT_SCRATCH = lambda S: [pltpu.VMEM((ATT_HEADS, S, 2 * HEAD_DIM), BF16), pltpu.VMEM((ATT_HEADS, V_ROWS, S), BF16),
                          pltpu.VMEM((2 * ATT_HEADS, 1, ATT_BLOCK), F32),
                          pltpu.VMEM((2 * ATT_HEADS, V_ROWS, ATT_BLOCK), F32)]


def _fox_kernel(q_ref, k_ref, v_ref, col_ref, qg_ref, kg_ref, o_ref, ka_ref, vt_ref, m_ref, acc_ref, *, S):
    h0 = pl.program_id(1) * ATT_HEADS
    i = pl.program_id(2)
    t = ATT_BLOCK
    D = HEAD_DIM
    lane = lax.broadcasted_iota(jnp.int32, (t, LANES), 1)

    @pl.when(i == 0)
    def _():
        def prep_keys(g, _):
            todo = []
            for u in range(KEY_BLOCKS_PER_STEP):
                sl = pl.ds(pl.multiple_of((g * KEY_BLOCKS_PER_STEP + u) * t, t), t)
                for hd in range(ATT_HEADS):
                    cols = slice(hd * D, (hd + 1) * D)
                    kn = _rms(k_ref[sl, cols].astype(F32), kg_ref[...]).astype(BF16)
                    hi, mid, lo = _split3(-LOG2E * _pick_lane(col_ref[sl, :], LANE_F + h0 + hd))
                    aug = jnp.where(lane < 3, 1.0, jnp.where(lane == 3, hi, jnp.where(lane == 4, mid,
                                                                                    jnp.where(lane == 5, lo, 0.0))))
                    todo.append((hd, sl, jnp.concatenate([kn, aug.astype(BF16)], axis=1), _transposed(v_ref[sl, cols])))
            for hd, sl, ka, vt in todo:
                ka_ref[hd, sl, :] = ka
                vt_ref[hd, :, sl] = vt
            return 0

        lax.fori_loop(0, S // (t * KEY_BLOCKS_PER_STEP), prep_keys, 0)

    def prep(hd, c):
        q = q_ref[c * t:(c + 1) * t, hd * D:(hd + 1) * D].astype(F32)
        qn = (_rms(q, qg_ref[...]) * (LOG2E * D ** -0.5)).astype(BF16)
        yield
        qsl = pl.ds(pl.multiple_of((2 * i + c) * t, t), t)
        hi, mid, lo = _split3(LOG2E * _pick_lane(col_ref[qsl, :], LANE_F + h0 + hd))
        aug = jnp.where(lane == 0, hi,
                        jnp.where(lane == 1, mid, jnp.where(lane == 2, lo, jnp.where(lane < 6, 1.0, 0.0))))
        qa = jnp.concatenate([qn, aug.astype(BF16)], axis=-1)
        return qa, qa

    _attend_tile_pair(i, [prep(hd, c) for hd in range(ATT_HEADS) for c in range(2)],
                      ka_ref, vt_ref, m_ref, acc_ref, o_ref)


def fox_attention(z, col, q_gain, k_gain):
    B, S, _ = z.shape
    t = 2 * ATT_BLOCK
    w = ATT_HEADS * HEAD_DIM
    return pl.pallas_call(
        functools.partial(_fox_kernel, S=S),
        grid=(B, FOX_HEADS // ATT_HEADS, S // t),
        in_specs=[
            pl.BlockSpec((None, t, w), lambda b, h, i: (b, i, FOX_Q0 // ATT_HEADS + h)),
            pl.BlockSpec((None, S, w), lambda b, h, i: (b, 0, FOX_K0 // ATT_HEADS + h)),
            pl.BlockSpec((None, S, w), lambda b, h, i: (b, 0, FOX_V0 // ATT_HEADS + h)),
            pl.BlockSpec((None, S, LANES), lambda b, h, i: (b, 0, 0)),
            pl.BlockSpec((1, HEAD_DIM), lambda b, h, i: (0, 0)),
            pl.BlockSpec((1, HEAD_DIM), lambda b, h, i: (0, 0)),
        ],
        out_specs=pl.BlockSpec((None, t, w), lambda b, h, i: (b, i, h)),
        out_shape=jax.ShapeDtypeStruct((B, S, FOX_HEADS * HEAD_DIM), BF16),
        scratch_shapes=_ATT_SCRATCH(S),
        compiler_params=_params("parallel", "parallel", "arbitrary"),
        name="fox_attention",
    )(z, z, z, col, q_gain.reshape(1, -1), k_gain.reshape(1, -1))


def _unit_lower_inverse(m):
    n = m.shape[0]
    r = lax.broadcasted_iota(jnp.int32, (n, n), 0)
    c = lax.broadcasted_iota(jnp.int32, (n, n), 1)
    eye = (r == c).astype(F32)

    def same_block(b):
        return (r // b) == (c // b)

    p = jnp.where(same_block(INV_BLOCK), m, 0.0)
    inv = eye - p
    k = 2
    while k < INV_BLOCK:
        pb = p.astype(BF16)
        p = jnp.dot(pb, pb, preferred_element_type=F32)
        yield
        inv = jnp.dot(inv.astype(BF16), (eye + p).astype(BF16), preferred_element_type=F32)
        yield
        k *= 2
    b = INV_BLOCK
    while b < n:
        off = jnp.where(same_block(2 * b), jnp.where(same_block(b), 0.0, m), 0.0).astype(BF16)
        ib = inv.astype(BF16)
        left = jnp.dot(ib, off, preferred_element_type=F32).astype(BF16)
        yield
        inv = inv - jnp.dot(left, ib, preferred_element_type=F32)
        yield
        b *= 2
    return inv


def _gdn_kernel(q_ref, k_ref, v_ref, gg_ref, wq_ref, wk_ref, wv_ref, col_ref, row_ref, on_ref,
                o_ref, qs_ref, ks_ref, vs_ref, *, S, rows, hb):
    h0 = pl.program_id(1) * hb
    C = GDN_CHUNK
    D = HEAD_DIM

    streams = [(x_ref, w_ref, dst_ref, hh, mode) for hh in range(hb)
               for x_ref, w_ref, dst_ref, mode in ((q_ref, wq_ref, qs_ref, "q"), (k_ref, wk_ref, ks_ref, "k"),
                                                   (v_ref, wv_ref, vs_ref, "v"))]
    halo = 16

    def conv_step(base, windows):
        outs = []
        for (x_ref, w_ref, dst_ref, hh, mode), win in zip(streams, windows):
            w = w_ref[:, hh * D:(hh + 1) * D]
            y = jnp.zeros((rows, D), F32)
            for tap in range(CONV_WIDTH):
                lead = halo - (CONV_WIDTH - 1) + tap
                y = y + w[tap:tap + 1, :] * pltpu.roll(win, rows + halo - lead, 0)[0:rows, :]
            y = y / (1.0 + jnp.exp(-y))
            if mode != "v":
                y = y * lax.rsqrt(jnp.sum(y * y, axis=-1, keepdims=True) + EPS)
            if mode == "q":
                y = y * D ** -0.5
            outs.append(y)
        for (x_ref, w_ref, dst_ref, hh, mode), y in zip(streams, outs):
            dst_ref[hh, pl.ds(base, rows), :] = y.astype(dst_ref.dtype)

    conv_step(0, [jnp.concatenate([jnp.zeros((halo, D), F32), x_ref[0:rows, hh * D:(hh + 1) * D].astype(F32)], axis=0)
                  for x_ref, _, _, hh, _ in streams])

    def conv_rest(n, _):
        base = pl.multiple_of(n * rows, rows)
        conv_step(base, [x_ref[pl.ds(pl.multiple_of(base - halo, halo), rows + halo), hh * D:(hh + 1) * D].astype(F32)
                         for x_ref, _, _, hh, _ in streams])
        return 0

    lax.fori_loop(1, S // rows, conv_rest, 0)

    r = lax.broadcasted_iota(jnp.int32, (C, C), 0)
    c = lax.broadcasted_iota(jnp.int32, (C, C), 1)
    incl = r >= c
    strict = r > c

    def chunk_local(hh, sl, tab):
        q = qs_ref[hh, sl, :].astype(F32)
        k = ks_ref[hh, sl, :].astype(F32)
        v = vs_ref[hh, sl, :].astype(F32)
        gcol = _pick_lane(tab, LANE_A + h0 + hh)
        beta = _pick_lane(tab, LANE_B + h0 + hh)
        grow = row_ref[hh, :, sl]
        glast = gcol[C - 1:C, :]
        decay = jnp.where(incl, jnp.exp(jnp.where(incl, gcol - grow, 0.0)), 0.0)
        eg = jnp.exp(gcol)
        kb = k * beta
        kbf = k.astype(BF16)
        m = jnp.where(strict, _dot_nt(kb.astype(BF16), kbf) * decay, 0.0)
        attn = (_dot_nt(q.astype(BF16), kbf) * decay).astype(BF16)
        yield
        tinv = (yield from _unit_lower_inverse(m)).astype(BF16)
        rhs = jnp.concatenate([v * beta, kb * eg], axis=-1).astype(BF16)
        sol = jnp.dot(tinv, rhs, preferred_element_type=F32)
        yield
        gate = gg_ref[sl, hh * D:(hh + 1) * D].astype(F32)
        return dict(u=sol[:, :D], w=sol[:, D:].astype(BF16), attn=attn, qg=(q * eg).astype(BF16),
                    kg_t=(k * jnp.exp(glast - gcol)).T.astype(BF16), keep=jnp.exp(glast),
                    gate=gate / (1.0 + jnp.exp(-gate)))

    def chunk_state(parts, state):
        outs = []
        for c in parts:
            sb = state.astype(BF16)
            v_new = c["u"] - jnp.dot(c["w"], sb, preferred_element_type=F32)
            o_state = jnp.dot(c["qg"], sb, preferred_element_type=F32)
            yield
            vb = v_new.astype(BF16)
            o = o_state + jnp.dot(c["attn"], vb, preferred_element_type=F32)
            state = state * c["keep"] + jnp.dot(c["kg_t"], vb, preferred_element_type=F32)
            yield
            outs.append((_rms(o, on_ref[...]) * c["gate"]).astype(o_ref.dtype))
        return state, outs

    def chunk_group(n, states):
        sls = [pl.ds(pl.multiple_of((n * GDN_CHUNKS_PER_STEP + g) * C, C), C) for g in range(GDN_CHUNKS_PER_STEP)]
        tabs = [col_ref[sl, :] for sl in sls]
        parts = _interleave([chunk_local(hh, sls[g], tabs[g]) for g in range(GDN_CHUNKS_PER_STEP) for hh in range(hb)])
        res = _interleave([chunk_state([parts[g * hb + hh] for g in range(GDN_CHUNKS_PER_STEP)], states[hh])
                           for hh in range(hb)])
        for g in range(GDN_CHUNKS_PER_STEP):
            o_ref[sls[g], :] = jnp.concatenate([outs[g] for _, outs in res], axis=-1)
        return tuple(s for s, _ in res)

    lax.fori_loop(0, S // (C * GDN_CHUNKS_PER_STEP), chunk_group, tuple(jnp.zeros((D, D), F32) for _ in range(hb)))


def gdn_mixer(z, conv_w, col, row, o_gain):
    B, S, _ = z.shape
    D = HEAD_DIM
    hb = GDN_HEADS_PER_STEP
    rows = 256
    seq = lambda off: pl.BlockSpec((None, S, hb * D), lambda b, h: (b, 0, off // hb + h),
                                   pipeline_mode=pl.Buffered(1))
    cw = lambda off: pl.BlockSpec((CONV_WIDTH, hb * D), lambda b, h: (0, off // hb + h))
    return pl.pallas_call(
        functools.partial(_gdn_kernel, S=S, rows=rows, hb=hb),
        grid=(B, GDN_HEADS // hb),
        in_specs=[
            seq(GDN_Q0), seq(GDN_K0), seq(GDN_V0), seq(GDN_G0),
            cw(0), cw(GDN_HEADS), cw(2 * GDN_HEADS),
            pl.BlockSpec((None, S, LANES), lambda b, h: (b, 0, 0)),
            pl.BlockSpec((None, hb, 1, S), lambda b, h: (b, h, 0, 0)),
            pl.BlockSpec((1, D), lambda b, h: (0, 0)),
        ],
        out_specs=pl.BlockSpec((None, S, hb * D), lambda b, h: (b, 0, h)),
        out_shape=jax.ShapeDtypeStruct((B, S, GDN_HEADS * D), BF16),
        scratch_shapes=[pltpu.VMEM((hb, S, D), BF16)] * 3,
        compiler_params=_params("parallel", "parallel"),
        name="gdn_mixer",
    )(z, z, z, z, conv_w, conv_w, conv_w, col, row, o_gain.reshape(1, D))


def _rope(x, cos, sin_signed):
    return x * cos + pltpu.roll(x, HEAD_DIM // 2, 1) * sin_signed


def _moba_kernel(q_ref, k_ref, v_ref, cos_ref, sin_ref, qg_ref, kg_ref, o_ref,
                 ka_ref, vt_ref, m_ref, acc_ref, kmean_ref, *, S):
    i = pl.program_id(2)
    t = ATT_BLOCK
    D = HEAD_DIM
    lane = lax.broadcasted_iota(jnp.int32, (t, LANES), 1)

    @pl.when(i == 0)
    def _():
        kmean_ref[...] = jnp.zeros(kmean_ref.shape, F32)

        def prep_keys(g, _):
            todo = []
            for u in range(KEY_BLOCKS_PER_STEP):
                n = g * KEY_BLOCKS_PER_STEP + u
                sl = pl.ds(pl.multiple_of(n * t, t), t)
                onehot = jnp.where(lane == n, 1.0, 0.0).astype(BF16)
                for hd in range(ATT_HEADS):
                    cols = slice(hd * D, (hd + 1) * D)
                    k = _rope(_rms(k_ref[sl, cols].astype(F32), kg_ref[...]), cos_ref[sl, :], sin_ref[sl, :])
                    todo.append((hd, n, sl, jnp.concatenate([k.astype(BF16), onehot], axis=1),
                                 jnp.mean(k, axis=0, keepdims=True), _transposed(v_ref[sl, cols])))
            for hd, n, sl, ka, kmean, vt in todo:
                ka_ref[hd, sl, :] = ka
                kmean_ref[hd, pl.ds(n, 1), :] = kmean
                vt_ref[hd, :, sl] = vt
            return 0

        lax.fori_loop(0, S // (t * KEY_BLOCKS_PER_STEP), prep_keys, 0)

    nb = -(-(S // t) // 8) * 8

    def prep(hd, c):
        cur = 2 * i + c
        qsl = pl.ds(pl.multiple_of(cur * t, t), t)
        q = _rope(_rms(q_ref[c * t:(c + 1) * t, hd * D:(hd + 1) * D].astype(F32), qg_ref[...]),
                  cos_ref[qsl, :], sin_ref[qsl, :])
        yield
        kmean = kmean_ref[hd]
        km_hi = kmean.astype(BF16)
        km_split = jnp.concatenate([km_hi, (kmean - km_hi.astype(F32)).astype(BF16)], axis=0)
        q_hi = q.astype(BF16)
        q_lo = (q - q_hi.astype(F32)).astype(BF16)
        part = _dot_nt(km_split, q_hi)
        gate = (part[:LANES] + part[LANES:] + _dot_nt(km_split[:LANES, :], q_lo))[0:nb]
        yield
        blk = lax.broadcasted_iota(jnp.int32, (nb, t), 0)
        blk_f = blk.astype(F32)
        gate = jnp.where(blk < cur, gate, -jnp.inf)
        sel = jnp.full((nb, t), NEG, F32)
        for _ in range(MOBA_TOPK):
            top = jnp.max(gate, axis=0, keepdims=True)
            first = jnp.min(jnp.where(gate == top, blk_f, float(LANES)), axis=0, keepdims=True)
            pick = blk_f == first
            sel = jnp.where(pick & (first < cur.astype(F32)), 0.0, sel)
            gate = jnp.where(pick, -jnp.inf, gate)
        sel_bias = jnp.concatenate([sel, jnp.full((LANES - nb, t), NEG, F32)], axis=0).T
        qs = (q * (LOG2E * D ** -0.5)).astype(BF16)
        return (jnp.concatenate([qs, jnp.zeros((t, LANES), BF16)], axis=-1),
                jnp.concatenate([qs, sel_bias.astype(BF16)], axis=-1))

    _attend_tile_pair(i, [prep(hd, c) for hd in range(ATT_HEADS) for c in range(2)],
                      ka_ref, vt_ref, m_ref, acc_ref, o_ref)


def moba_attention(z, cos, sin_signed, q_gain, k_gain):
    B, S, W = z.shape
    H = W // (3 * HEAD_DIM)
    assert ATT_BLOCK == MOBA_BLOCK
    t = 2 * ATT_BLOCK
    w = ATT_HEADS * HEAD_DIM
    hp = H // ATT_HEADS
    return pl.pallas_call(
        functools.partial(_moba_kernel, S=S),
        grid=(B, hp, S // t),
        in_specs=[
            pl.BlockSpec((None, t, w), lambda b, h, i: (b, i, h)),
            pl.BlockSpec((None, S, w), lambda b, h, i: (b, 0, hp + h)),
            pl.BlockSpec((None, S, w), lambda b, h, i: (b, 0, 2 * hp + h)),
            pl.BlockSpec((S, HEAD_DIM), lambda b, h, i: (0, 0)),
            pl.BlockSpec((S, HEAD_DIM), lambda b, h, i: (0, 0)),
            pl.BlockSpec((1, HEAD_DIM), lambda b, h, i: (0, 0)),
            pl.BlockSpec((1, HEAD_DIM), lambda b, h, i: (0, 0)),
        ],
        out_specs=pl.BlockSpec((None, t, w), lambda b, h, i: (b, i, h)),
        out_shape=jax.ShapeDtypeStruct((B, S, H * HEAD_DIM), BF16),
        scratch_shapes=_ATT_SCRATCH(S) + [pltpu.VMEM((ATT_HEADS, LANES, HEAD_DIM), F32)],
        compiler_params=_params("parallel", "parallel", "arbitrary"),
        name="moba_attention",
    )(z, z, z, cos, sin_signed, q_gain.reshape(1, -1), k_gain.reshape(1, -1))


META_IDX, META_RANK, META_GATE = 0, 2, 4


def _router_kernel(x_ref, *refs, n_proj):
    proj_refs = refs[:2 * n_proj]
    g_ref, w_ref, y_ref, xn_ref, meta_ref, cnt_ref, carry_ref = refs[2 * n_proj:]
    i = pl.program_id(0)
    tm = x_ref.shape[0]

    @pl.when(i == 0)
    def _():
        carry_ref[...] = jnp.zeros(carry_ref.shape, F32)

    w = w_ref[...]
    w_hi = w.astype(BF16)
    w_lo = (w - w_hi.astype(F32)).astype(BF16)
    part = tm // ROUTER_CHAINS
    lane = lax.broadcasted_iota(jnp.int32, (part, LANES), 1)
    lane_f = lane.astype(F32)

    def route(n):
        rows = slice(n * part, (n + 1) * part)
        y = _plus_projections(x_ref[rows, :], proj_refs, rows)
        y_ref[rows, :] = y
        yield
        xn = _rms(y, g_ref[...])
        xn_ref[rows, :] = _pack_bf16_pairs(xn)
        x_hi = xn.astype(BF16)
        x_lo = (xn - x_hi.astype(F32)).astype(BF16)
        logits = (jnp.dot(x_hi, w_hi, preferred_element_type=F32) + jnp.dot(x_hi, w_lo, preferred_element_type=F32)
                  + jnp.dot(x_lo, w_hi, preferred_element_type=F32))
        yield
        logits = jnp.where(lane < N_EXPERTS, logits, -jnp.inf)
        top1 = jnp.max(logits, axis=-1, keepdims=True)
        yield
        idx1 = jnp.min(jnp.where(logits == top1, lane_f, float(LANES)), axis=-1, keepdims=True)
        yield
        rest = jnp.where(lane_f == idx1, -jnp.inf, logits)
        top2 = jnp.max(rest, axis=-1, keepdims=True)
        yield
        idx2 = jnp.min(jnp.where(rest == top2, lane_f, float(LANES)), axis=-1, keepdims=True)
        e2 = jnp.exp(top2 - top1)
        denom = 1.0 + e2
        return idx1, idx2, 1.0 / denom, e2 / denom, jnp.where((lane_f == idx1) | (lane_f == idx2), 1.0, 0.0)

    picks = _interleave([route(n) for n in range(ROUTER_CHAINS)])
    chosen = jnp.concatenate([p[4] for p in picks], axis=0)
    r = lax.broadcasted_iota(jnp.int32, (tm, tm), 0)
    c = lax.broadcasted_iota(jnp.int32, (tm, tm), 1)
    ahead = jnp.dot(jnp.where(r > c, 1.0, 0.0).astype(BF16), chosen.astype(BF16), preferred_element_type=F32)
    carry = carry_ref[...]
    for n, (idx1, idx2, g1, g2, _) in enumerate(picks):
        rows = slice(n * part, (n + 1) * part)
        rank = ahead[rows, :] + carry
        rank1 = jnp.sum(jnp.where(lane_f == idx1, rank, 0.0), axis=-1, keepdims=True)
        rank2 = jnp.sum(jnp.where(lane_f == idx2, rank, 0.0), axis=-1, keepdims=True)
        meta = jnp.zeros((part, LANES), F32)
        for k, v in enumerate((idx1, idx2, rank1, rank2, g1, g2)):
            meta = jnp.where(lane == k, v, meta)
        meta_ref[rows, :] = meta
    carry = carry + jnp.sum(chosen, axis=0, keepdims=True)
    carry_ref[...] = carry
    cnt_ref[...] = jnp.broadcast_to(carry, cnt_ref.shape)


def moe_router(x, proj, gain, w_router, *, tm):
    M, K = x.shape
    w = jnp.zeros((K, LANES), F32).at[:, :N_EXPERTS].set(w_router)
    proj_specs, proj_args = _projection_specs(proj, tm)
    return pl.pallas_call(
        functools.partial(_router_kernel, n_proj=len(proj)),
        grid=(M // tm,),
        in_specs=[
            pl.BlockSpec((tm, K), lambda i: (i, 0)),
            *proj_specs,
            pl.BlockSpec((1, K), lambda i: (0, 0)),
            pl.BlockSpec((K, LANES), lambda i: (0, 0)),
        ],
        out_specs=[
            pl.BlockSpec((tm, K), lambda i: (i, 0)),
            pl.BlockSpec((tm, K // 2), lambda i: (i, 0)),
            pl.BlockSpec((tm, LANES), lambda i: (i, 0)),
            pl.BlockSpec((8, LANES), lambda i: (0, 0)),
        ],
        out_shape=[
            jax.ShapeDtypeStruct((M, K), F32),
            jax.ShapeDtypeStruct((M, K // 2), jnp.int32),
            jax.ShapeDtypeStruct((M, LANES), F32),
            jax.ShapeDtypeStruct((8, LANES), F32),
        ],
        scratch_shapes=[pltpu.VMEM((1, LANES), F32)],
        compiler_params=_params("arbitrary"),
        name="moe_router",
    )(x, *proj_args, gain.reshape(1, K), w)


def _sc_workers():
    info = plsc.get_sparse_core_info()
    return info.num_cores, info.num_cores * info.num_subcores


def scatter_rows(rows, dest, n_out):
    M, W = rows.shape
    nc, nw = _sc_workers()
    assert M % (nw * SC_ROWS) == 0
    per_w = M // nw
    mesh = plsc.VectorSubcoreMesh(core_axis_name="c", subcore_axis_name="s")

    @functools.partial(
        pl.kernel, mesh=mesh, out_type=jax.ShapeDtypeStruct((n_out, W), rows.dtype),
        scratch_types=[pltpu.VMEM((SC_ROWS,), jnp.int32), pltpu.VMEM((SC_ROWS, W), rows.dtype),
                       pltpu.SemaphoreType.DMA])
    def kern(rows_hbm, dest_hbm, out_hbm, idx_v, rows_v, sem):
        wid = lax.axis_index("s") * nc + lax.axis_index("c")

        @pl.loop(0, per_w // SC_ROWS)
        def _(g):
            base = wid * per_w + g * SC_ROWS
            pltpu.sync_copy(rows_hbm.at[pl.ds(base, SC_ROWS)], rows_v)
            for k in range(2):
                pltpu.sync_copy(dest_hbm.at[k, pl.ds(base, SC_ROWS)], idx_v)
                pltpu.async_copy(rows_v, out_hbm.at[idx_v], sem).wait()

    return kern(rows, dest)


def gather_rows(table, idx):
    N = idx.shape[0]
    W = table.shape[1]
    nc, nw = _sc_workers()
    assert N % (nw * SC_ROWS) == 0
    per_w = N // nw
    mesh = plsc.VectorSubcoreMesh(core_axis_name="c", subcore_axis_name="s")

    @functools.partial(
        pl.kernel, mesh=mesh, out_type=jax.ShapeDtypeStruct((N, W), table.dtype),
        scratch_types=[pltpu.VMEM((SC_ROWS,), jnp.int32), pltpu.VMEM((SC_ROWS, W), table.dtype),
                       pltpu.SemaphoreType.DMA])
    def kern(table_hbm, idx_hbm, out_hbm, idx_v, rows_v, sem):
        wid = lax.axis_index("s") * nc + lax.axis_index("c")

        @pl.loop(0, per_w // SC_ROWS)
        def _(g):
            base = wid * per_w + g * SC_ROWS
            pltpu.sync_copy(idx_hbm.at[pl.ds(base, SC_ROWS)], idx_v)
            pltpu.async_copy(table_hbm.at[idx_v], rows_v, sem).wait()
            pltpu.sync_copy(rows_v, out_hbm.at[pl.ds(base, SC_ROWS)])

    return kern(table, idx)


def _combine_kernel(h_ref, y1_ref, y2_ref, meta_ref, o_ref):
    g1 = meta_ref[:, META_GATE:META_GATE + 1]
    g2 = meta_ref[:, META_GATE + 1:META_GATE + 2]
    o_ref[...] = (h_ref[...] + g1 * _unpack_bf16_pairs(y1_ref[...]).astype(F32)
                  + g2 * _unpack_bf16_pairs(y2_ref[...]).astype(F32))


def moe_combine(h, y_pairs, meta):
    M, K = h.shape
    p = PERM_TILE
    nt = M // p
    return pl.pallas_call(
        _combine_kernel,
        grid=(nt,),
        in_specs=[
            pl.BlockSpec((p, K), lambda i: (i, 0)),
            pl.BlockSpec((p, K // 2), lambda i: (i, 0)),
            pl.BlockSpec((p, K // 2), lambda i: (nt + i, 0)),
            pl.BlockSpec((p, LANES), lambda i: (i, 0)),
        ],
        out_specs=pl.BlockSpec((p, K), lambda i: (i, 0)),
        out_shape=jax.ShapeDtypeStruct((M, K), F32),
        compiler_params=_params("parallel"),
        name="moe_combine",
    )(h, y_pairs, y_pairs, meta)


def _expert_ffn_kernel(te_ref, tv_ref, x_ref, wg_ref, wu_ref, wd_ref, o_ref, acc_ref, *, nf):
    del te_ref
    i = pl.program_id(0)
    j = pl.program_id(1)
    tm = x_ref.shape[0]
    valid = tv_ref[i]

    def swiglu_rows(rows):
        x = _unpack_bf16_pairs(x_ref[0:rows, :])
        gt = jnp.dot(x, wg_ref[...], preferred_element_type=F32)
        up = jnp.dot(x, wu_ref[...], preferred_element_type=F32)
        act = (gt / (1.0 + jnp.exp(-gt)) * up).astype(BF16)
        part = jnp.dot(act, wd_ref[...], preferred_element_type=F32)

        def finish(total):
            o_ref[0:rows, :] = _pack_bf16_pairs(total)
            if rows < tm:
                o_ref[rows:tm, :] = jnp.zeros((tm - rows, o_ref.shape[1]), o_ref.dtype)

        if nf == 1:
            finish(part)
            return

        @pl.when(j == 0)
        def _():
            acc_ref[0:rows, :] = part

        @pl.when((j > 0) & (j < nf - 1))
        def _():
            acc_ref[0:rows, :] += part

        @pl.when(j == nf - 1)
        def _():
            finish(acc_ref[0:rows, :] + part)

    part = tm // EXPERT_ROW_PARTS
    for n in range(1, EXPERT_ROW_PARTS + 1):
        pl.when((valid > (n - 1) * part) & (valid <= n * part))(functools.partial(swiglu_rows, n * part))

    @pl.when((valid == 0) & (j == 0))
    def _():
        o_ref[...] = jnp.zeros(o_ref.shape, o_ref.dtype)


def expert_ffn(xs, tile_expert, tile_valid, w_gate_up, w_down, *, tm, tf):
    R = xs.shape[0]
    E, F, K = w_down.shape
    nf = F // tf
    live = lambda i, tv: tv[i] > 0
    col = lambda i, j, tv: jnp.where(live(i, tv), j, nf - 1)
    grid_spec = pltpu.PrefetchScalarGridSpec(
        num_scalar_prefetch=2,
        grid=(R // tm, nf),
        in_specs=[
            pl.BlockSpec((tm, K // 2), lambda i, j, te, tv: (te[R // tm + i], 0)),
            pl.BlockSpec((None, K, tf), lambda i, j, te, tv: (te[i], 0, col(i, j, tv))),
            pl.BlockSpec((None, K, tf), lambda i, j, te, tv: (te[i], 0, nf + col(i, j, tv))),
            pl.BlockSpec((None, tf, K), lambda i, j, te, tv: (te[i], col(i, j, tv), 0)),
        ],
        out_specs=pl.BlockSpec((tm, K // 2), lambda i, j, te, tv: (i, 0)),
        scratch_shapes=[pltpu.VMEM((tm, K), F32)],
    )
    return pl.pallas_call(
        functools.partial(_expert_ffn_kernel, nf=nf),
        grid_spec=grid_spec,
        out_shape=jax.ShapeDtypeStruct((R, K // 2), jnp.int32),
        compiler_params=_params("arbitrary", "arbitrary"),
        name="moe_expert_ffn",
    )(tile_expert, tile_valid, xs, w_gate_up, w_gate_up, w_down)


def moe_residual(h, proj, gain, w_router, w_gate_up, w_down):
    M, K = h.shape
    p = PERM_TILE
    tm = EXPERT_TILE
    n_rows = 2 * M + N_EXPERTS * tm
    h, xn, meta, cnt = moe_router(h, proj, gain, w_router, tm=p)

    counts = cnt[0, :N_EXPERTS].astype(jnp.int32)
    padded = (counts + tm - 1) // tm * tm
    ends = jnp.cumsum(padded)
    offsets = ends - padded
    n_tiles = n_rows // tm
    tile_row = jnp.arange(n_tiles) * tm
    n_used = ends[-1] // tm
    last_used = jnp.minimum(tile_row // tm, n_used - 1)
    expert_of = lambda row: jnp.minimum(jnp.sum(ends[None, :] <= row[:, None], axis=1), N_EXPERTS - 1)
    tile_expert = expert_of(last_used * tm)
    row_in_expert = tile_row - jnp.sum(jnp.where(tile_expert[:, None] == jnp.arange(N_EXPERTS), offsets, 0), axis=1)
    own_count = jnp.sum(jnp.where(tile_expert[:, None] == jnp.arange(N_EXPERTS), counts, 0), axis=1)
    tile_valid = jnp.where(tile_row // tm < n_used, jnp.clip(own_count - row_in_expert, 0, tm), 0)
    tile_tables = jnp.concatenate([tile_expert, last_used]).astype(jnp.int32)

    idx = meta[:, META_IDX:META_IDX + 2].astype(jnp.int32)
    rank = meta[:, META_RANK:META_RANK + 2].astype(jnp.int32)
    dest = (jnp.sum(jnp.where(idx[:, :, None] == jnp.arange(N_EXPERTS), offsets, 0), axis=-1) + rank).T

    xs = scatter_rows(xn, dest, n_rows)
    ys = expert_ffn(xs, tile_tables, tile_valid.astype(jnp.int32), w_gate_up, w_down, tm=tm, tf=EXPERT_COLS)
    return moe_combine(h, gather_rows(ys, dest.reshape(-1)), meta)


def _ffn_kernel(x_ref, *refs, n_proj):
    proj_refs = refs[:2 * n_proj]
    g_ref, wg_ref, wu_ref, wd_ref, o_ref, xn_ref, acc_ref = refs[2 * n_proj:]
    j = pl.program_id(1)

    @pl.when(j == 0)
    def _():
        x = _plus_projections(x_ref[...], proj_refs)
        xn_ref[...] = _rms(x, g_ref[...]).astype(BF16)
        acc_ref[...] = x

    xn = xn_ref[...]
    gt = jnp.dot(xn, wg_ref[...], preferred_element_type=F32)
    up = jnp.dot(xn, wu_ref[...], preferred_element_type=F32)
    act = gt / (1.0 + jnp.exp(-gt)) * up
    acc_ref[...] += jnp.dot(act.astype(BF16), wd_ref[...], preferred_element_type=F32)

    @pl.when(j == pl.num_programs(1) - 1)
    def _():
        o_ref[...] = acc_ref[...]


def ffn_residual(x, proj, gain, w_gate_up, w_down, *, tm, tf):
    M, K = x.shape
    F = w_down.shape[0]
    nf = F // tf
    proj_specs, proj_args = _projection_specs(proj, tm)
    return pl.pallas_call(
        functools.partial(_ffn_kernel, n_proj=len(proj)),
        grid=(M // tm, nf),
        in_specs=[
            pl.BlockSpec((tm, K), lambda i, j: (i, 0)),
            *proj_specs,
            pl.BlockSpec((1, K), lambda i, j: (0, 0)),
            pl.BlockSpec((K, tf), lambda i, j: (0, j)),
            pl.BlockSpec((K, tf), lambda i, j: (0, nf + j)),
            pl.BlockSpec((tf, K), lambda i, j: (j, 0)),
        ],
        out_specs=pl.BlockSpec((tm, K), lambda i, j: (i, 0)),
        out_shape=jax.ShapeDtypeStruct((M, K), F32),
        scratch_shapes=[pltpu.VMEM((tm, K), BF16), pltpu.VMEM((tm, K), F32)],
        compiler_params=_params("parallel", "arbitrary"),
        name="ffn_residual",
    )(x, *proj_args, gain.reshape(1, K), w_gate_up, w_gate_up, w_down)


def _even_mix(h, B, S, norm1, w_in, f_bias, q_norm, k_norm, conv_w, a_log, dt_bias, o_norm, w_out):
    M, D = h.shape
    fw, gw = FOX_HEADS * HEAD_DIM, GDN_HEADS * HEAD_DIM
    o_ff = 3 * fw
    o_gq = o_ff + FOX_HEADS
    o_ga = o_gq + 3 * gw
    o_gb = o_ga + GDN_HEADS
    o_gg = o_gb + GDN_HEADS
    w_big = jnp.concatenate([w_in[:, :o_ff], w_in[:, o_gq:o_ga], w_in[:, o_gg:]], axis=1)
    w_small = jnp.concatenate([w_in[:, o_ff:o_gq], w_in[:, o_ga:o_gg],
                               jnp.zeros((D, LANES - FOX_HEADS - 2 * GDN_HEADS), F32)], axis=1)
    w_big, w_small = (t.astype(BF16) for t in lax.optimization_barrier((w_big, w_small)))
    z, zs = norm_matmul(h, norm1, w_big, w_small, tm=ROW_TILE, tn=IN_PROJ_COLS)
    par = jnp.zeros((8, LANES), F32)
    par = par.at[0, LANE_F:LANE_F + FOX_HEADS].set(f_bias).at[0, LANE_A:LANE_A + GDN_HEADS].set(dt_bias)
    par = par.at[1, LANE_A:LANE_A + GDN_HEADS].set(a_log)
    col, row = even_gates(zs.reshape(B, S, LANES), par)
    z = z.reshape(B, S, -1)
    fox = fox_attention(z, col, q_norm, k_norm)
    gdn = gdn_mixer(z, conv_w, col, row, o_norm)
    w_out = w_out.astype(BF16)
    return [(fox.reshape(M, fw), w_out[:fw]), (gdn.reshape(M, gw), w_out[fw:])]


def _odd_mix(h, B, S, norm1, w_qkv, q_norm, k_norm, w_out):
    M, D = h.shape
    z = norm_matmul(h, norm1, w_qkv.astype(BF16), tm=ROW_TILE, tn=QKV_COLS).reshape(B, S, -1)
    half = HEAD_DIM // 2
    inv = jnp.power(ROPE_THETA, -jnp.arange(half, dtype=F32) / half)
    ang = jnp.arange(S, dtype=F32)[:, None] * inv[None, :]
    cos, sin = jnp.cos(ang), jnp.sin(ang)
    cos_full = jnp.concatenate([cos, cos], axis=-1)
    sin_signed = jnp.concatenate([-sin, sin], axis=-1)
    att = moba_attention(z, cos_full, sin_signed, q_norm, k_norm)
    return [(att.reshape(M, -1), w_out.astype(BF16))]


def kernel(x, e_norm1, e_w_in, e_fox_f_bias, e_fox_q_norm, e_fox_k_norm, e_gdn_conv, e_gdn_a_log,
           e_gdn_dt_bias, e_gdn_o_norm, e_w_out, e_norm2, e_ffn_w_gate_up, e_ffn_w_down,
           o_norm1, o_w_qkv, o_q_norm, o_k_norm, o_w_out, o_norm2, o_router, o_exp_w_gate_up, o_exp_w_down):
    B, S, D = x.shape
    h = x.reshape(B * S, D)
    depth = e_norm1.shape[0] + o_norm1.shape[0]
    for layer in range(depth):
        i = layer // 2
        if layer % 2 == 0:
            mix = _even_mix(h, B, S, e_norm1[i], e_w_in[i], e_fox_f_bias[i], e_fox_q_norm[i], e_fox_k_norm[i],
                            e_gdn_conv[i], e_gdn_a_log[i], e_gdn_dt_bias[i], e_gdn_o_norm[i], e_w_out[i])
            h = ffn_residual(h, mix, e_norm2[i], e_ffn_w_gate_up[i].astype(BF16), e_ffn_w_down[i].astype(BF16),
                             tm=FFN_ROWS, tf=FFN_COLS)
        else:
            mix = _odd_mix(h, B, S, o_norm1[i], o_w_qkv[i], o_q_norm[i], o_k_norm[i], o_w_out[i])
            h = moe_residual(h, mix, o_norm2[i], o_router[i], o_exp_w_gate_up[i].astype(BF16),
                             o_exp_w_down[i].astype(BF16))
    return h.reshape(B, S, D)
```

```python
import functools

import jax
import jax.numpy as jnp
from jax import lax
from jax.experimental import pallas as pl
from jax.experimental.pallas import tpu as pltpu
from jax.experimental.pallas import tpu_sc as plsc

F32 = jnp.float32
BF16 = jnp.bfloat16

HEAD_DIM = 128
FOX_HEADS = 4
GDN_HEADS = 4
CONV_WIDTH = 4
MOBA_BLOCK = 256
MOBA_TOPK = 3
N_EXPERTS = 8
ROPE_THETA = 10000.0
EPS = 1e-6

LANES = 128
GDN_CHUNK = 128
INV_BLOCK = 16
GDN_HEADS_PER_STEP = 4
GDN_CHUNKS_PER_STEP = 4
NEG = -(2.0 ** 100)
LOG2E = 1.4426950408889634
ATT_BLOCK = 256
ATT_HEADS = 4
KEY_BLOCKS_PER_STEP = 2
V_ROWS = HEAD_DIM + 16
PERM_TILE = 1024
ROUTER_CHAINS = 4
SC_ROWS = 64
ROW_TILE = 1024
MXU_COLS = 256
IN_PROJ_COLS = 7 * MXU_COLS
QKV_COLS = 6 * MXU_COLS
FFN_ROWS, FFN_COLS = 512, 1408
EXPERT_TILE = 1024
EXPERT_COLS = 896
EXPERT_ROW_PARTS = 4
VMEM_LIMIT_BYTES = 56 * 1024 * 1024

FOX_Q0, FOX_K0, FOX_V0 = 0, 4, 8
GDN_Q0, GDN_K0, GDN_V0, GDN_G0 = 12, 16, 20, 24
LANE_F, LANE_A, LANE_B = 0, 4, 8


def _params(*sem):
    return pltpu.CompilerParams(dimension_semantics=sem, vmem_limit_bytes=VMEM_LIMIT_BYTES)


def _rms(x, gain):
    return x * lax.rsqrt(jnp.mean(x * x, axis=-1, keepdims=True) + EPS) * gain


def _dot_nt(a, b, **kw):
    return lax.dot_general(a, b, (((1,), (1,)), ((), ())), preferred_element_type=F32, **kw)


def _pick_lane(x, lane_idx):
    lane = lax.broadcasted_iota(jnp.int32, x.shape, 1)
    return jnp.sum(jnp.where(lane == lane_idx, x, 0.0), axis=-1, keepdims=True)


def _pack_bf16_pairs(x):
    n = x.shape[1] // 2
    hi = pltpu.bitcast(x[:, :n].astype(BF16).astype(F32), jnp.uint32)
    lo = pltpu.bitcast(x[:, n:].astype(BF16).astype(F32), jnp.uint32)
    return pltpu.bitcast(hi | (lo >> 16), jnp.int32)


def _unpack_bf16_pairs(w):
    u = pltpu.bitcast(w, jnp.uint32)
    hi = pltpu.bitcast(u & jnp.uint32(0xFFFF0000), F32).astype(BF16)
    lo = pltpu.bitcast(u << 16, F32).astype(BF16)
    return jnp.concatenate([hi, lo], axis=1)


def _norm_mm_kernel(x_ref, g_ref, w_ref, *rest, has_aux):
    if has_aux:
        waux_ref, o_ref, oaux_ref, xn_ref = rest
    else:
        o_ref, xn_ref = rest

    @pl.when(pl.program_id(1) == 0)
    def _():
        xn = _rms(x_ref[...], g_ref[...]).astype(BF16)
        xn_ref[...] = xn
        if has_aux:
            oaux_ref[...] = jnp.dot(xn, waux_ref[...], preferred_element_type=F32)

    o_ref[...] = jnp.dot(xn_ref[...], w_ref[...], preferred_element_type=F32).astype(o_ref.dtype)


def norm_matmul(x, gain, w, w_aux=None, *, tm, tn, out_dtype=BF16):
    M, K = x.shape
    N = w.shape[1]
    has_aux = w_aux is not None
    in_specs = [
        pl.BlockSpec((tm, K), lambda i, j: (i, 0)),
        pl.BlockSpec((1, K), lambda i, j: (0, 0)),
        pl.BlockSpec((K, tn), lambda i, j: (0, j)),
    ]
    out_shape = [jax.ShapeDtypeStruct((M, N), out_dtype)]
    out_specs = [pl.BlockSpec((tm, tn), lambda i, j: (i, j))]
    args = [x, gain.reshape(1, K), w]
    if has_aux:
        in_specs.append(pl.BlockSpec((K, LANES), lambda i, j: (0, 0)))
        out_shape.append(jax.ShapeDtypeStruct((M, LANES), F32))
        out_specs.append(pl.BlockSpec((tm, LANES), lambda i, j: (i, 0)))
        args.append(w_aux)
    res = pl.pallas_call(
        functools.partial(_norm_mm_kernel, has_aux=has_aux),
        grid=(M // tm, N // tn),
        in_specs=in_specs,
        out_specs=out_specs,
        out_shape=out_shape,
        scratch_shapes=[pltpu.VMEM((tm, K), BF16)],
        compiler_params=_params("parallel", "arbitrary"),
        name="norm_matmul",
    )(*args)
    return res if has_aux else res[0]


def _plus_projections(x, proj_refs, rows=slice(None)):
    for a_ref, w_ref in zip(proj_refs[0::2], proj_refs[1::2]):
        x = x + jnp.dot(a_ref[rows, :], w_ref[...], preferred_element_type=F32)
    return x


def _projection_specs(pairs, tm):
    specs, args = [], []
    for a, w in pairs:
        specs.append(pl.BlockSpec((tm, a.shape[1]), lambda i, *_: (i, 0)))
        specs.append(pl.BlockSpec(w.shape, lambda i, *_: (0, 0)))
        args += [a, w]
    return specs, args


def _gate_kernel(zs_ref, par_ref, col_ref, row_ref, *, S, B):
    C = GDN_CHUNK
    bias = par_ref[0:1, :]
    neg_a = -jnp.exp(par_ref[1:2, :])
    r = lax.broadcasted_iota(jnp.int32, (C, C), 0)
    c = lax.broadcasted_iota(jnp.int32, (C, C), 1)
    tril = (r >= c).astype(F32)
    lane = lax.broadcasted_iota(jnp.int32, (C, LANES), 1)

    def body(n, carry):
        sl = pl.ds(pl.multiple_of(n * C, C), C)
        us, betas = [], []
        for bi in range(B):
            z = zs_ref[bi, sl, :]
            t = z + bias
            soft = jnp.log(1.0 + jnp.exp(-jnp.abs(t)))
            log_f = jnp.minimum(t, 0.0) - soft
            g = neg_a * (jnp.maximum(t, 0.0) + soft)
            betas.append(1.0 / (1.0 + jnp.exp(-z)))
            us.append(jnp.where(lane < LANE_A, log_f, jnp.where(lane < LANE_B, g, 0.0)))
        sums = jnp.dot(tril, jnp.concatenate(us, axis=1), preferred_element_type=F32, precision=lax.Precision.HIGHEST)
        last = []
        for bi in range(B):
            cs = sums[:, bi * LANES:(bi + 1) * LANES] + jnp.where(lane < LANE_A, carry[bi], 0.0)
            out = jnp.where(lane < LANE_B, cs, betas[bi])
            col_ref[bi, sl, :] = out
            out_t = out.T
            for hd in range(GDN_HEADS):
                row_ref[bi, hd, :, sl] = out_t[LANE_A + hd:LANE_A + hd + 1, :]
            last.append(cs[C - 1:C, :])
        return tuple(last)

    lax.fori_loop(0, S // C, body, tuple(jnp.zeros((1, LANES), F32) for _ in range(B)))


def even_gates(zs, par):
    B, S, _ = zs.shape
    return pl.pallas_call(
        functools.partial(_gate_kernel, S=S, B=B),
        grid=(1,),
        in_specs=[
            pl.BlockSpec((B, S, LANES), lambda i: (0, 0, 0)),
            pl.BlockSpec((8, LANES), lambda i: (0, 0)),
        ],
        out_specs=[
            pl.BlockSpec((B, S, LANES), lambda i: (0, 0, 0)),
            pl.BlockSpec((B, GDN_HEADS, 1, S), lambda i: (0, 0, 0, 0)),
        ],
        out_shape=[jax.ShapeDtypeStruct((B, S, LANES), F32), jax.ShapeDtypeStruct((B, GDN_HEADS, 1, S), F32)],
        compiler_params=_params("arbitrary"),
        name="even_gates",
    )(zs, par)


def _split3(x):
    hi = x.astype(BF16).astype(F32)
    mid = (x - hi).astype(BF16).astype(F32)
    lo = (x - hi - mid).astype(BF16).astype(F32)
    return hi, mid, lo


def _interleave(gens):
    out = [None] * len(gens)
    live = list(range(len(gens)))
    while live:
        for n in list(live):
            try:
                next(gens[n])
            except StopIteration as stop:
                out[n] = stop.value
                live.remove(n)
    return out


def _attend(state, c, qa, ka_ref, vt_ref, key0, nkeys, keep=None):
    ks = pl.ds(pl.multiple_of(key0, ATT_BLOCK), nkeys)
    st = _dot_nt(ka_ref[ks, :], qa)
    yield
    if keep is not None:
        st = jnp.where(keep, st, NEG)
    m_new = jnp.max(st, axis=0, keepdims=True)
    if state[c] is not None:
        m_old, acc_old = state[c]
        m_new = jnp.maximum(m_old, m_new)
    p = jnp.exp2(st - m_new).astype(BF16)
    pv = jnp.dot(vt_ref[:, ks], p, preferred_element_type=F32)
    state[c] = (m_new, pv if state[c] is None else acc_old * jnp.exp2(m_old - m_new) + pv)


def _attend_tile_pair(i, prep, ka_ref, vt_ref, m_ref, acc_ref, o_ref):
    t = ATT_BLOCK
    chains = [(hd, c) for c in range(2) for hd in range(ATT_HEADS)]
    num = lambda hd, c: 2 * hd + c
    qa = _interleave(prep)

    def step(state, hd, c, which, **kw):
        return _attend(state, num(hd, c), qa[num(hd, c)][which], ka_ref.at[hd], vt_ref.at[hd], **kw)

    def save(state):
        m_ref[...] = jnp.stack([state[n][0] for n in range(len(chains))])
        acc_ref[...] = jnp.stack([state[n][1] for n in range(len(chains))])

    def past_blocks(key0, n_steps):
        state = {n: (m_ref[n], acc_ref[n]) for n in range(len(chains))}
        _interleave([step(state, hd, c, 1, key0=key0 + s * 2 * t, nkeys=2 * t)
                     for s in range(n_steps) for hd, c in chains])
        save(state)

    state = {n: None for n in range(len(chains))}
    keep = _causal_keep()
    _interleave([step(state, hd, c, 0, key0=(2 * i + c) * t, nkeys=t, keep=keep) for hd, c in chains]
                + [step(state, hd, 1, 1, key0=(2 * i) * t, nkeys=t) for hd in range(ATT_HEADS)])
    save(state)

    def four_blocks(g, _):
        past_blocks(g * (4 * t), 2)
        return 0

    lax.fori_loop(0, i // 2, four_blocks, 0)

    @pl.when((i & 1) != 0)
    def _():
        past_blocks((i // 2) * (4 * t), 1)

    for hd, c in chains:
        acc = acc_ref[num(hd, c)]
        out_t = acc[:HEAD_DIM, :] * (1.0 / acc[HEAD_DIM:HEAD_DIM + 1, :])
        o_ref[c * t:(c + 1) * t, hd * HEAD_DIM:(hd + 1) * HEAD_DIM] = out_t.T.astype(o_ref.dtype)


def _transposed(v):
    r = lax.broadcasted_iota(jnp.int32, (HEAD_DIM, HEAD_DIM), 0)
    c = lax.broadcasted_iota(jnp.int32, (HEAD_DIM, HEAD_DIM), 1)
    v_t = _dot_nt(jnp.where(r == c, 1.0, 0.0).astype(BF16), v).astype(BF16)
    return jnp.concatenate([v_t, jnp.ones((V_ROWS - HEAD_DIM, ATT_BLOCK), BF16)], axis=0)


def _causal_keep():
    key = lax.broadcasted_iota(jnp.int32, (ATT_BLOCK, ATT_BLOCK), 0)
    qry = lax.broadcasted_iota(jnp.int32, (ATT_BLOCK, ATT_BLOCK), 1)
    return key <= qry


_ATT_SCRATCH = lambda S: [pltpu.VMEM((ATT_HEADS, S, 2 * HEAD_DIM), BF16), pltpu.VMEM((ATT_HEADS, V_ROWS, S), BF16),
                          pltpu.VMEM((2 * ATT_HEADS, 1, ATT_BLOCK), F32),
                          pltpu.VMEM((2 * ATT_HEADS, V_ROWS, ATT_BLOCK), F32)]


def _fox_kernel(q_ref, k_ref, v_ref, col_ref, qg_ref, kg_ref, o_ref, ka_ref, vt_ref, m_ref, acc_ref, *, S):
    h0 = pl.program_id(1) * ATT_HEADS
    i = pl.program_id(2)
    t = ATT_BLOCK
    D = HEAD_DIM
    lane = lax.broadcasted_iota(jnp.int32, (t, LANES), 1)

    @pl.when(i == 0)
    def _():
        def prep_keys(g, _):
            todo = []
            for u in range(KEY_BLOCKS_PER_STEP):
                sl = pl.ds(pl.multiple_of((g * KEY_BLOCKS_PER_STEP + u) * t, t), t)
                for hd in range(ATT_HEADS):
                    cols = slice(hd * D, (hd + 1) * D)
                    kn = _rms(k_ref[sl, cols].astype(F32), kg_ref[...]).astype(BF16)
                    hi, mid, lo = _split3(-LOG2E * _pick_lane(col_ref[sl, :], LANE_F + h0 + hd))
                    aug = jnp.where(lane < 3, 1.0, jnp.where(lane == 3, hi, jnp.where(lane == 4, mid,
                                                                                    jnp.where(lane == 5, lo, 0.0))))
                    todo.append((hd, sl, jnp.concatenate([kn, aug.astype(BF16)], axis=1), _transposed(v_ref[sl, cols])))
            for hd, sl, ka, vt in todo:
                ka_ref[hd, sl, :] = ka
                vt_ref[hd, :, sl] = vt
            return 0

        lax.fori_loop(0, S // (t * KEY_BLOCKS_PER_STEP), prep_keys, 0)

    def prep(hd, c):
        q = q_ref[c * t:(c + 1) * t, hd * D:(hd + 1) * D].astype(F32)
        qn = (_rms(q, qg_ref[...]) * (LOG2E * D ** -0.5)).astype(BF16)
        yield
        qsl = pl.ds(pl.multiple_of((2 * i + c) * t, t), t)
        hi, mid, lo = _split3(LOG2E * _pick_lane(col_ref[qsl, :], LANE_F + h0 + hd))
        aug = jnp.where(lane == 0, hi,
                        jnp.where(lane == 1, mid, jnp.where(lane == 2, lo, jnp.where(lane < 6, 1.0, 0.0))))
        qa = jnp.concatenate([qn, aug.astype(BF16)], axis=-1)
        return qa, qa

    _attend_tile_pair(i, [prep(hd, c) for hd in range(ATT_HEADS) for c in range(2)],
                      ka_ref, vt_ref, m_ref, acc_ref, o_ref)


def fox_attention(z, col, q_gain, k_gain):
    B, S, _ = z.shape
    t = 2 * ATT_BLOCK
    w = ATT_HEADS * HEAD_DIM
    return pl.pallas_call(
        functools.partial(_fox_kernel, S=S),
        grid=(B, FOX_HEADS // ATT_HEADS, S // t),
        in_specs=[
            pl.BlockSpec((None, t, w), lambda b, h, i: (b, i, FOX_Q0 // ATT_HEADS + h)),
            pl.BlockSpec((None, S, w), lambda b, h, i: (b, 0, FOX_K0 // ATT_HEADS + h)),
            pl.BlockSpec((None, S, w), lambda b, h, i: (b, 0, FOX_V0 // ATT_HEADS + h)),
            pl.BlockSpec((None, S, LANES), lambda b, h, i: (b, 0, 0)),
            pl.BlockSpec((1, HEAD_DIM), lambda b, h, i: (0, 0)),
            pl.BlockSpec((1, HEAD_DIM), lambda b, h, i: (0, 0)),
        ],
        out_specs=pl.BlockSpec((None, t, w), lambda b, h, i: (b, i, h)),
        out_shape=jax.ShapeDtypeStruct((B, S, FOX_HEADS * HEAD_DIM), BF16),
        scratch_shapes=_ATT_SCRATCH(S),
        compiler_params=_params("parallel", "parallel", "arbitrary"),
        name="fox_attention",
    )(z, z, z, col, q_gain.reshape(1, -1), k_gain.reshape(1, -1))


def _unit_lower_inverse(m):
    n = m.shape[0]
    r = lax.broadcasted_iota(jnp.int32, (n, n), 0)
    c = lax.broadcasted_iota(jnp.int32, (n, n), 1)
    eye = (r == c).astype(F32)

    def same_block(b):
        return (r // b) == (c // b)

    p = jnp.where(same_block(INV_BLOCK), m, 0.0)
    inv = eye - p
    k = 2
    while k < INV_BLOCK:
        pb = p.astype(BF16)
        p = jnp.dot(pb, pb, preferred_element_type=F32)
        yield
        inv = jnp.dot(inv.astype(BF16), (eye + p).astype(BF16), preferred_element_type=F32)
        yield
        k *= 2
    b = INV_BLOCK
    while b < n:
        off = jnp.where(same_block(2 * b), jnp.where(same_block(b), 0.0, m), 0.0).astype(BF16)
        ib = inv.astype(BF16)
        left = jnp.dot(ib, off, preferred_element_type=F32).astype(BF16)
        yield
        inv = inv - jnp.dot(left, ib, preferred_element_type=F32)
        yield
        b *= 2
    return inv


def _gdn_kernel(q_ref, k_ref, v_ref, gg_ref, wq_ref, wk_ref, wv_ref, col_ref, row_ref, on_ref,
                o_ref, qs_ref, ks_ref, vs_ref, *, S, rows, hb):
    h0 = pl.program_id(1) * hb
    C = GDN_CHUNK
    D = HEAD_DIM

    streams = [(x_ref, w_ref, dst_ref, hh, mode) for hh in range(hb)
               for x_ref, w_ref, dst_ref, mode in ((q_ref, wq_ref, qs_ref, "q"), (k_ref, wk_ref, ks_ref, "k"),
                                                   (v_ref, wv_ref, vs_ref, "v"))]
    halo = 16

    def conv_step(base, windows):
        outs = []
        for (x_ref, w_ref, dst_ref, hh, mode), win in zip(streams, windows):
            w = w_ref[:, hh * D:(hh + 1) * D]
            y = jnp.zeros((rows, D), F32)
            for tap in range(CONV_WIDTH):
                lead = halo - (CONV_WIDTH - 1) + tap
                y = y + w[tap:tap + 1, :] * pltpu.roll(win, rows + halo - lead, 0)[0:rows, :]
            y = y / (1.0 + jnp.exp(-y))
            if mode != "v":
                y = y * lax.rsqrt(jnp.sum(y * y, axis=-1, keepdims=True) + EPS)
            if mode == "q":
                y = y * D ** -0.5
            outs.append(y)
        for (x_ref, w_ref, dst_ref, hh, mode), y in zip(streams, outs):
            dst_ref[hh, pl.ds(base, rows), :] = y.astype(dst_ref.dtype)

    conv_step(0, [jnp.concatenate([jnp.zeros((halo, D), F32), x_ref[0:rows, hh * D:(hh + 1) * D].astype(F32)], axis=0)
                  for x_ref, _, _, hh, _ in streams])

    def conv_rest(n, _):
        base = pl.multiple_of(n * rows, rows)
        conv_step(base, [x_ref[pl.ds(pl.multiple_of(base - halo, halo), rows + halo), hh * D:(hh + 1) * D].astype(F32)
                         for x_ref, _, _, hh, _ in streams])
        return 0

    lax.fori_loop(1, S // rows, conv_rest, 0)

    r = lax.broadcasted_iota(jnp.int32, (C, C), 0)
    c = lax.broadcasted_iota(jnp.int32, (C, C), 1)
    incl = r >= c
    strict = r > c

    def chunk_local(hh, sl, tab):
        q = qs_ref[hh, sl, :].astype(F32)
        k = ks_ref[hh, sl, :].astype(F32)
        v = vs_ref[hh, sl, :].astype(F32)
        gcol = _pick_lane(tab, LANE_A + h0 + hh)
        beta = _pick_lane(tab, LANE_B + h0 + hh)
        grow = row_ref[hh, :, sl]
        glast = gcol[C - 1:C, :]
        decay = jnp.where(incl, jnp.exp(jnp.where(incl, gcol - grow, 0.0)), 0.0)
        eg = jnp.exp(gcol)
        kb = k * beta
        kbf = k.astype(BF16)
        m = jnp.where(strict, _dot_nt(kb.astype(BF16), kbf) * decay, 0.0)
        attn = (_dot_nt(q.astype(BF16), kbf) * decay).astype(BF16)
        yield
        tinv = (yield from _unit_lower_inverse(m)).astype(BF16)
        rhs = jnp.concatenate([v * beta, kb * eg], axis=-1).astype(BF16)
        sol = jnp.dot(tinv, rhs, preferred_element_type=F32)
        yield
        gate = gg_ref[sl, hh * D:(hh + 1) * D].astype(F32)
        return dict(u=sol[:, :D], w=sol[:, D:].astype(BF16), attn=attn, qg=(q * eg).astype(BF16),
                    kg_t=(k * jnp.exp(glast - gcol)).T.astype(BF16), keep=jnp.exp(glast),
                    gate=gate / (1.0 + jnp.exp(-gate)))

    def chunk_state(parts, state):
        outs = []
        for c in parts:
            sb = state.astype(BF16)
            v_new = c["u"] - jnp.dot(c["w"], sb, preferred_element_type=F32)
            o_state = jnp.dot(c["qg"], sb, preferred_element_type=F32)
            yield
            vb = v_new.astype(BF16)
            o = o_state + jnp.dot(c["attn"], vb, preferred_element_type=F32)
            state = state * c["keep"] + jnp.dot(c["kg_t"], vb, preferred_element_type=F32)
            yield
            outs.append((_rms(o, on_ref[...]) * c["gate"]).astype(o_ref.dtype))
        return state, outs

    def chunk_group(n, states):
        sls = [pl.ds(pl.multiple_of((n * GDN_CHUNKS_PER_STEP + g) * C, C), C) for g in range(GDN_CHUNKS_PER_STEP)]
        tabs = [col_ref[sl, :] for sl in sls]
        parts = _interleave([chunk_local(hh, sls[g], tabs[g]) for g in range(GDN_CHUNKS_PER_STEP) for hh in range(hb)])
        res = _interleave([chunk_state([parts[g * hb + hh] for g in range(GDN_CHUNKS_PER_STEP)], states[hh])
                           for hh in range(hb)])
        for g in range(GDN_CHUNKS_PER_STEP):
            o_ref[sls[g], :] = jnp.concatenate([outs[g] for _, outs in res], axis=-1)
        return tuple(s for s, _ in res)

    lax.fori_loop(0, S // (C * GDN_CHUNKS_PER_STEP), chunk_group, tuple(jnp.zeros((D, D), F32) for _ in range(hb)))


def gdn_mixer(z, conv_w, col, row, o_gain):
    B, S, _ = z.shape
    D = HEAD_DIM
    hb = GDN_HEADS_PER_STEP
    rows = 256
    seq = lambda off: pl.BlockSpec((None, S, hb * D), lambda b, h: (b, 0, off // hb + h),
                                   pipeline_mode=pl.Buffered(1))
    cw = lambda off: pl.BlockSpec((CONV_WIDTH, hb * D), lambda b, h: (0, off // hb + h))
    return pl.pallas_call(
        functools.partial(_gdn_kernel, S=S, rows=rows, hb=hb),
        grid=(B, GDN_HEADS // hb),
        in_specs=[
            seq(GDN_Q0), seq(GDN_K0), seq(GDN_V0), seq(GDN_G0),
            cw(0), cw(GDN_HEADS), cw(2 * GDN_HEADS),
            pl.BlockSpec((None, S, LANES), lambda b, h: (b, 0, 0)),
            pl.BlockSpec((None, hb, 1, S), lambda b, h: (b, h, 0, 0)),
            pl.BlockSpec((1, D), lambda b, h: (0, 0)),
        ],
        out_specs=pl.BlockSpec((None, S, hb * D), lambda b, h: (b, 0, h)),
        out_shape=jax.ShapeDtypeStruct((B, S, GDN_HEADS * D), BF16),
        scratch_shapes=[pltpu.VMEM((hb, S, D), BF16)] * 3,
        compiler_params=_params("parallel", "parallel"),
        name="gdn_mixer",
    )(z, z, z, z, conv_w, conv_w, conv_w, col, row, o_gain.reshape(1, D))


def _rope(x, cos, sin_signed):
    return x * cos + pltpu.roll(x, HEAD_DIM // 2, 1) * sin_signed


def _moba_kernel(q_ref, k_ref, v_ref, cos_ref, sin_ref, qg_ref, kg_ref, o_ref,
                 ka_ref, vt_ref, m_ref, acc_ref, kmean_ref, *, S):
    i = pl.program_id(2)
    t = ATT_BLOCK
    D = HEAD_DIM
    lane = lax.broadcasted_iota(jnp.int32, (t, LANES), 1)

    @pl.when(i == 0)
    def _():
        kmean_ref[...] = jnp.zeros(kmean_ref.shape, F32)

        def prep_keys(g, _):
            todo = []
            for u in range(KEY_BLOCKS_PER_STEP):
                n = g * KEY_BLOCKS_PER_STEP + u
                sl = pl.ds(pl.multiple_of(n * t, t), t)
                onehot = jnp.where(lane == n, 1.0, 0.0).astype(BF16)
                for hd in range(ATT_HEADS):
                    cols = slice(hd * D, (hd + 1) * D)
                    k = _rope(_rms(k_ref[sl, cols].astype(F32), kg_ref[...]), cos_ref[sl, :], sin_ref[sl, :])
                    todo.append((hd, n, sl, jnp.concatenate([k.astype(BF16), onehot], axis=1),
                                 jnp.mean(k, axis=0, keepdims=True), _transposed(v_ref[sl, cols])))
            for hd, n, sl, ka, kmean, vt in todo:
                ka_ref[hd, sl, :] = ka
                kmean_ref[hd, pl.ds(n, 1), :] = kmean
                vt_ref[hd, :, sl] = vt
            return 0

        lax.fori_loop(0, S // (t * KEY_BLOCKS_PER_STEP), prep_keys, 0)

    nb = -(-(S // t) // 8) * 8

    def prep(hd, c):
        cur = 2 * i + c
        qsl = pl.ds(pl.multiple_of(cur * t, t), t)
        q = _rope(_rms(q_ref[c * t:(c + 1) * t, hd * D:(hd + 1) * D].astype(F32), qg_ref[...]),
                  cos_ref[qsl, :], sin_ref[qsl, :])
        yield
        kmean = kmean_ref[hd]
        km_hi = kmean.astype(BF16)
        km_split = jnp.concatenate([km_hi, (kmean - km_hi.astype(F32)).astype(BF16)], axis=0)
        q_hi = q.astype(BF16)
        q_lo = (q - q_hi.astype(F32)).astype(BF16)
        part = _dot_nt(km_split, q_hi)
        gate = (part[:LANES] + part[LANES:] + _dot_nt(km_split[:LANES, :], q_lo))[0:nb]
        yield
        blk = lax.broadcasted_iota(jnp.int32, (nb, t), 0)
        blk_f = blk.astype(F32)
        gate = jnp.where(blk < cur, gate, -jnp.inf)
        sel = jnp.full((nb, t), NEG, F32)
        for _ in range(MOBA_TOPK):
            top = jnp.max(gate, axis=0, keepdims=True)
            first = jnp.min(jnp.where(gate == top, blk_f, float(LANES)), axis=0, keepdims=True)
            pick = blk_f == first
            sel = jnp.where(pick & (first < cur.astype(F32)), 0.0, sel)
            gate = jnp.where(pick, -jnp.inf, gate)
        sel_bias = jnp.concatenate([sel, jnp.full((LANES - nb, t), NEG, F32)], axis=0).T
        qs = (q * (LOG2E * D ** -0.5)).astype(BF16)
        return (jnp.concatenate([qs, jnp.zeros((t, LANES), BF16)], axis=-1),
                jnp.concatenate([qs, sel_bias.astype(BF16)], axis=-1))

    _attend_tile_pair(i, [prep(hd, c) for hd in range(ATT_HEADS) for c in range(2)],
                      ka_ref, vt_ref, m_ref, acc_ref, o_ref)


def moba_attention(z, cos, sin_signed, q_gain, k_gain):
    B, S, W = z.shape
    H = W // (3 * HEAD_DIM)
    assert ATT_BLOCK == MOBA_BLOCK
    t = 2 * ATT_BLOCK
    w = ATT_HEADS * HEAD_DIM
    hp = H // ATT_HEADS
    return pl.pallas_call(
        functools.partial(_moba_kernel, S=S),
        grid=(B, hp, S // t),
        in_specs=[
            pl.BlockSpec((None, t, w), lambda b, h, i: (b, i, h)),
            pl.BlockSpec((None, S, w), lambda b, h, i: (b, 0, hp + h)),
            pl.BlockSpec((None, S, w), lambda b, h, i: (b, 0, 2 * hp + h)),
            pl.BlockSpec((S, HEAD_DIM), lambda b, h, i: (0, 0)),
            pl.BlockSpec((S, HEAD_DIM), lambda b, h, i: (0, 0)),
            pl.BlockSpec((1, HEAD_DIM), lambda b, h, i: (0, 0)),
            pl.BlockSpec((1, HEAD_DIM), lambda b, h, i: (0, 0)),
        ],
        out_specs=pl.BlockSpec((None, t, w), lambda b, h, i: (b, i, h)),
        out_shape=jax.ShapeDtypeStruct((B, S, H * HEAD_DIM), BF16),
        scratch_shapes=_ATT_SCRATCH(S) + [pltpu.VMEM((ATT_HEADS, LANES, HEAD_DIM), F32)],
        compiler_params=_params("parallel", "parallel", "arbitrary"),
        name="moba_attention",
    )(z, z, z, cos, sin_signed, q_gain.reshape(1, -1), k_gain.reshape(1, -1))


META_IDX, META_RANK, META_GATE = 0, 2, 4


def _router_kernel(x_ref, *refs, n_proj):
    proj_refs = refs[:2 * n_proj]
    g_ref, w_ref, y_ref, xn_ref, meta_ref, cnt_ref, carry_ref = refs[2 * n_proj:]
    i = pl.program_id(0)
    tm = x_ref.shape[0]

    @pl.when(i == 0)
    def _():
        carry_ref[...] = jnp.zeros(carry_ref.shape, F32)

    w = w_ref[...]
    w_hi = w.astype(BF16)
    w_lo = (w - w_hi.astype(F32)).astype(BF16)
    part = tm // ROUTER_CHAINS
    lane = lax.broadcasted_iota(jnp.int32, (part, LANES), 1)
    lane_f = lane.astype(F32)

    def route(n):
        rows = slice(n * part, (n + 1) * part)
        y = _plus_projections(x_ref[rows, :], proj_refs, rows)
        y_ref[rows, :] = y
        yield
        xn = _rms(y, g_ref[...])
        xn_ref[rows, :] = _pack_bf16_pairs(xn)
        x_hi = xn.astype(BF16)
        x_lo = (xn - x_hi.astype(F32)).astype(BF16)
        logits = (jnp.dot(x_hi, w_hi, preferred_element_type=F32) + jnp.dot(x_hi, w_lo, preferred_element_type=F32)
                  + jnp.dot(x_lo, w_hi, preferred_element_type=F32))
        yield
        logits = jnp.where(lane < N_EXPERTS, logits, -jnp.inf)
        top1 = jnp.max(logits, axis=-1, keepdims=True)
        yield
        idx1 = jnp.min(jnp.where(logits == top1, lane_f, float(LANES)), axis=-1, keepdims=True)
        yield
        rest = jnp.where(lane_f == idx1, -jnp.inf, logits)
        top2 = jnp.max(rest, axis=-1, keepdims=True)
        yield
        idx2 = jnp.min(jnp.where(rest == top2, lane_f, float(LANES)), axis=-1, keepdims=True)
        e2 = jnp.exp(top2 - top1)
        denom = 1.0 + e2
        return idx1, idx2, 1.0 / denom, e2 / denom, jnp.where((lane_f == idx1) | (lane_f == idx2), 1.0, 0.0)

    picks = _interleave([route(n) for n in range(ROUTER_CHAINS)])
    chosen = jnp.concatenate([p[4] for p in picks], axis=0)
    r = lax.broadcasted_iota(jnp.int32, (tm, tm), 0)
    c = lax.broadcasted_iota(jnp.int32, (tm, tm), 1)
    ahead = jnp.dot(jnp.where(r > c, 1.0, 0.0).astype(BF16), chosen.astype(BF16), preferred_element_type=F32)
    carry = carry_ref[...]
    for n, (idx1, idx2, g1, g2, _) in enumerate(picks):
        rows = slice(n * part, (n + 1) * part)
        rank = ahead[rows, :] + carry
        rank1 = jnp.sum(jnp.where(lane_f == idx1, rank, 0.0), axis=-1, keepdims=True)
        rank2 = jnp.sum(jnp.where(lane_f == idx2, rank, 0.0), axis=-1, keepdims=True)
        meta = jnp.zeros((part, LANES), F32)
        for k, v in enumerate((idx1, idx2, rank1, rank2, g1, g2)):
            meta = jnp.where(lane == k, v, meta)
        meta_ref[rows, :] = meta
    carry = carry + jnp.sum(chosen, axis=0, keepdims=True)
    carry_ref[...] = carry
    cnt_ref[...] = jnp.broadcast_to(carry, cnt_ref.shape)


def moe_router(x, proj, gain, w_router, *, tm):
    M, K = x.shape
    w = jnp.zeros((K, LANES), F32).at[:, :N_EXPERTS].set(w_router)
    proj_specs, proj_args = _projection_specs(proj, tm)
    return pl.pallas_call(
        functools.partial(_router_kernel, n_proj=len(proj)),
        grid=(M // tm,),
        in_specs=[
            pl.BlockSpec((tm, K), lambda i: (i, 0)),
            *proj_specs,
            pl.BlockSpec((1, K), lambda i: (0, 0)),
            pl.BlockSpec((K, LANES), lambda i: (0, 0)),
        ],
        out_specs=[
            pl.BlockSpec((tm, K), lambda i: (i, 0)),
            pl.BlockSpec((tm, K // 2), lambda i: (i, 0)),
            pl.BlockSpec((tm, LANES), lambda i: (i, 0)),
            pl.BlockSpec((8, LANES), lambda i: (0, 0)),
        ],
        out_shape=[
            jax.ShapeDtypeStruct((M, K), F32),
            jax.ShapeDtypeStruct((M, K // 2), jnp.int32),
            jax.ShapeDtypeStruct((M, LANES), F32),
            jax.ShapeDtypeStruct((8, LANES), F32),
        ],
        scratch_shapes=[pltpu.VMEM((1, LANES), F32)],
        compiler_params=_params("arbitrary"),
        name="moe_router",
    )(x, *proj_args, gain.reshape(1, K), w)


def _sc_workers():
    info = plsc.get_sparse_core_info()
    return info.num_cores, info.num_cores * info.num_subcores


def scatter_rows(rows, dest, n_out):
    M, W = rows.shape
    nc, nw = _sc_workers()
    assert M % (nw * SC_ROWS) == 0
    per_w = M // nw
    mesh = plsc.VectorSubcoreMesh(core_axis_name="c", subcore_axis_name="s")

    @functools.partial(
        pl.kernel, mesh=mesh, out_type=jax.ShapeDtypeStruct((n_out, W), rows.dtype),
        scratch_types=[pltpu.VMEM((SC_ROWS,), jnp.int32), pltpu.VMEM((SC_ROWS, W), rows.dtype),
                       pltpu.SemaphoreType.DMA])
    def kern(rows_hbm, dest_hbm, out_hbm, idx_v, rows_v, sem):
        wid = lax.axis_index("s") * nc + lax.axis_index("c")

        @pl.loop(0, per_w // SC_ROWS)
        def _(g):
            base = wid * per_w + g * SC_ROWS
            pltpu.sync_copy(rows_hbm.at[pl.ds(base, SC_ROWS)], rows_v)
            for k in range(2):
                pltpu.sync_copy(dest_hbm.at[k, pl.ds(base, SC_ROWS)], idx_v)
                pltpu.async_copy(rows_v, out_hbm.at[idx_v], sem).wait()

    return kern(rows, dest)


def gather_rows(table, idx):
    N = idx.shape[0]
    W = table.shape[1]
    nc, nw = _sc_workers()
    assert N % (nw * SC_ROWS) == 0
    per_w = N // nw
    mesh = plsc.VectorSubcoreMesh(core_axis_name="c", subcore_axis_name="s")

    @functools.partial(
        pl.kernel, mesh=mesh, out_type=jax.ShapeDtypeStruct((N, W), table.dtype),
        scratch_types=[pltpu.VMEM((SC_ROWS,), jnp.int32), pltpu.VMEM((SC_ROWS, W), table.dtype),
                       pltpu.SemaphoreType.DMA])
    def kern(table_hbm, idx_hbm, out_hbm, idx_v, rows_v, sem):
        wid = lax.axis_index("s") * nc + lax.axis_index("c")

        @pl.loop(0, per_w // SC_ROWS)
        def _(g):
            base = wid * per_w + g * SC_ROWS
            pltpu.sync_copy(idx_hbm.at[pl.ds(base, SC_ROWS)], idx_v)
            pltpu.async_copy(table_hbm.at[idx_v], rows_v, sem).wait()
            pltpu.sync_copy(rows_v, out_hbm.at[pl.ds(base, SC_ROWS)])

    return kern(table, idx)


def _combine_kernel(h_ref, y1_ref, y2_ref, meta_ref, o_ref):
    g1 = meta_ref[:, META_GATE:META_GATE + 1]
    g2 = meta_ref[:, META_GATE + 1:META_GATE + 2]
    o_ref[...] = (h_ref[...] + g1 * _unpack_bf16_pairs(y1_ref[...]).astype(F32)
                  + g2 * _unpack_bf16_pairs(y2_ref[...]).astype(F32))


def moe_combine(h, y_pairs, meta):
    M, K = h.shape
    p = PERM_TILE
    nt = M // p
    return pl.pallas_call(
        _combine_kernel,
        grid=(nt,),
        in_specs=[
            pl.BlockSpec((p, K), lambda i: (i, 0)),
            pl.BlockSpec((p, K // 2), lambda i: (i, 0)),
            pl.BlockSpec((p, K // 2), lambda i: (nt + i, 0)),
            pl.BlockSpec((p, LANES), lambda i: (i, 0)),
        ],
        out_specs=pl.BlockSpec((p, K), lambda i: (i, 0)),
        out_shape=jax.ShapeDtypeStruct((M, K), F32),
        compiler_params=_params("parallel"),
        name="moe_combine",
    )(h, y_pairs, y_pairs, meta)


def _expert_ffn_kernel(te_ref, tv_ref, x_ref, wg_ref, wu_ref, wd_ref, o_ref, acc_ref, *, nf):
    del te_ref
    i = pl.program_id(0)
    j = pl.program_id(1)
    tm = x_ref.shape[0]
    valid = tv_ref[i]

    def swiglu_rows(rows):
        x = _unpack_bf16_pairs(x_ref[0:rows, :])
        gt = jnp.dot(x, wg_ref[...], preferred_element_type=F32)
        up = jnp.dot(x, wu_ref[...], preferred_element_type=F32)
        act = (gt / (1.0 + jnp.exp(-gt)) * up).astype(BF16)
        part = jnp.dot(act, wd_ref[...].astype(BF16), preferred_element_type=F32)

        def finish(total):
            o_ref[0:rows, :] = _pack_bf16_pairs(total)
            if rows < tm:
                o_ref[rows:tm, :] = jnp.zeros((tm - rows, o_ref.shape[1]), o_ref.dtype)

        if nf == 1:
            finish(part)
            return

        @pl.when(j == 0)
        def _():
            acc_ref[0:rows, :] = part

        @pl.when((j > 0) & (j < nf - 1))
        def _():
            acc_ref[0:rows, :] += part

        @pl.when(j == nf - 1)
        def _():
            finish(acc_ref[0:rows, :] + part)

    part = tm // EXPERT_ROW_PARTS
    for n in range(1, EXPERT_ROW_PARTS + 1):
        pl.when((valid > (n - 1) * part) & (valid <= n * part))(functools.partial(swiglu_rows, n * part))

    @pl.when((valid == 0) & (j == 0))
    def _():
        o_ref[...] = jnp.zeros(o_ref.shape, o_ref.dtype)


def expert_ffn(xs, tile_expert, tile_valid, w_gate_up, w_down, *, tm, tf):
    R = xs.shape[0]
    E, F, K = w_down.shape
    nf = F // tf
    live = lambda i, tv: tv[i] > 0
    col = lambda i, j, tv: jnp.where(live(i, tv), j, nf - 1)
    grid_spec = pltpu.PrefetchScalarGridSpec(
        num_scalar_prefetch=2,
        grid=(R // tm, nf),
        in_specs=[
            pl.BlockSpec((tm, K // 2), lambda i, j, te, tv: (te[R // tm + i], 0)),
            pl.BlockSpec((None, K, tf), lambda i, j, te, tv: (te[i], 0, col(i, j, tv))),
            pl.BlockSpec((None, K, tf), lambda i, j, te, tv: (te[i], 0, nf + col(i, j, tv))),
            pl.BlockSpec((None, tf, K), lambda i, j, te, tv: (te[i], col(i, j, tv), 0)),
        ],
        out_specs=pl.BlockSpec((tm, K // 2), lambda i, j, te, tv: (i, 0)),
        scratch_shapes=[pltpu.VMEM((tm, K), F32)],
    )
    return pl.pallas_call(
        functools.partial(_expert_ffn_kernel, nf=nf),
        grid_spec=grid_spec,
        out_shape=jax.ShapeDtypeStruct((R, K // 2), jnp.int32),
        compiler_params=_params("arbitrary", "arbitrary"),
        name="moe_expert_ffn",
    )(tile_expert, tile_valid, xs, w_gate_up, w_gate_up, w_down)


def moe_residual(h, proj, gain, w_router, w_gate_up, w_down):
    M, K = h.shape
    p = PERM_TILE
    tm = EXPERT_TILE
    n_rows = 2 * M + N_EXPERTS * tm
    h, xn, meta, cnt = moe_router(h, proj, gain, w_router, tm=p)

    counts = cnt[0, :N_EXPERTS].astype(jnp.int32)
    padded = (counts + tm - 1) // tm * tm
    ends = jnp.cumsum(padded)
    offsets = ends - padded
    n_tiles = n_rows // tm
    tile_row = jnp.arange(n_tiles) * tm
    n_used = ends[-1] // tm
    last_used = jnp.minimum(tile_row // tm, n_used - 1)
    expert_of = lambda row: jnp.minimum(jnp.sum(ends[None, :] <= row[:, None], axis=1), N_EXPERTS - 1)
    tile_expert = expert_of(last_used * tm)
    row_in_expert = tile_row - jnp.sum(jnp.where(tile_expert[:, None] == jnp.arange(N_EXPERTS), offsets, 0), axis=1)
    own_count = jnp.sum(jnp.where(tile_expert[:, None] == jnp.arange(N_EXPERTS), counts, 0), axis=1)
    tile_valid = jnp.where(tile_row // tm < n_used, jnp.clip(own_count - row_in_expert, 0, tm), 0)
    tile_tables = jnp.concatenate([tile_expert, last_used]).astype(jnp.int32)

    idx = meta[:, META_IDX:META_IDX + 2].astype(jnp.int32)
    rank = meta[:, META_RANK:META_RANK + 2].astype(jnp.int32)
    dest = (jnp.sum(jnp.where(idx[:, :, None] == jnp.arange(N_EXPERTS), offsets, 0), axis=-1) + rank).T

    xs = scatter_rows(xn, dest, n_rows)
    ys = expert_ffn(xs, tile_tables, tile_valid.astype(jnp.int32), w_gate_up, w_down, tm=tm, tf=EXPERT_COLS)
    return moe_combine(h, gather_rows(ys, dest.reshape(-1)), meta)


def _ffn_kernel(x_ref, *refs, n_proj):
    proj_refs = refs[:2 * n_proj]
    g_ref, wg_ref, wu_ref, wd_ref, o_ref, xn_ref, acc_ref = refs[2 * n_proj:]
    j = pl.program_id(1)

    @pl.when(j == 0)
    def _():
        x = _plus_projections(x_ref[...], proj_refs)
        xn_ref[...] = _rms(x, g_ref[...]).astype(BF16)
        acc_ref[...] = x

    xn = xn_ref[...]
    gt = jnp.dot(xn, wg_ref[...], preferred_element_type=F32)
    up = jnp.dot(xn, wu_ref[...], preferred_element_type=F32)
    act = gt / (1.0 + jnp.exp(-gt)) * up
    acc_ref[...] += jnp.dot(act.astype(BF16), wd_ref[...], preferred_element_type=F32)

    @pl.when(j == pl.num_programs(1) - 1)
    def _():
        o_ref[...] = acc_ref[...]


def ffn_residual(x, proj, gain, w_gate_up, w_down, *, tm, tf):
    M, K = x.shape
    F = w_down.shape[0]
    nf = F // tf
    proj_specs, proj_args = _projection_specs(proj, tm)
    return pl.pallas_call(
        functools.partial(_ffn_kernel, n_proj=len(proj)),
        grid=(M // tm, nf),
        in_specs=[
            pl.BlockSpec((tm, K), lambda i, j: (i, 0)),
            *proj_specs,
            pl.BlockSpec((1, K), lambda i, j: (0, 0)),
            pl.BlockSpec((K, tf), lambda i, j: (0, j)),
            pl.BlockSpec((K, tf), lambda i, j: (0, nf + j)),
            pl.BlockSpec((tf, K), lambda i, j: (j, 0)),
        ],
        out_specs=pl.BlockSpec((tm, K), lambda i, j: (i, 0)),
        out_shape=jax.ShapeDtypeStruct((M, K), F32),
        scratch_shapes=[pltpu.VMEM((tm, K), BF16), pltpu.VMEM((tm, K), F32)],
        compiler_params=_params("parallel", "arbitrary"),
        name="ffn_residual",
    )(x, *proj_args, gain.reshape(1, K), w_gate_up, w_gate_up, w_down)


def _even_mix(h, B, S, norm1, w_in, f_bias, q_norm, k_norm, conv_w, a_log, dt_bias, o_norm, w_out):
    M, D = h.shape
    fw, gw = FOX_HEADS * HEAD_DIM, GDN_HEADS * HEAD_DIM
    o_ff = 3 * fw
    o_gq = o_ff + FOX_HEADS
    o_ga = o_gq + 3 * gw
    o_gb = o_ga + GDN_HEADS
    o_gg = o_gb + GDN_HEADS
    w_big = jnp.concatenate([w_in[:, :o_ff], w_in[:, o_gq:o_ga], w_in[:, o_gg:]], axis=1)
    w_small = jnp.concatenate([w_in[:, o_ff:o_gq], w_in[:, o_ga:o_gg],
                               jnp.zeros((D, LANES - FOX_HEADS - 2 * GDN_HEADS), F32)], axis=1)
    w_big, w_small = (t.astype(BF16) for t in lax.optimization_barrier((w_big, w_small)))
    z, zs = norm_matmul(h, norm1, w_big, w_small, tm=ROW_TILE, tn=IN_PROJ_COLS)
    par = jnp.zeros((8, LANES), F32)
    par = par.at[0, LANE_F:LANE_F + FOX_HEADS].set(f_bias).at[0, LANE_A:LANE_A + GDN_HEADS].set(dt_bias)
    par = par.at[1, LANE_A:LANE_A + GDN_HEADS].set(a_log)
    col, row = even_gates(zs.reshape(B, S, LANES), par)
    z = z.reshape(B, S, -1)
    fox = fox_attention(z, col, q_norm, k_norm)
    gdn = gdn_mixer(z, conv_w, col, row, o_norm)
    w_out = w_out.astype(BF16)
    return [(fox.reshape(M, fw), w_out[:fw]), (gdn.reshape(M, gw), w_out[fw:])]


def _odd_mix(h, B, S, norm1, w_qkv, q_norm, k_norm, w_out):
    M, D = h.shape
    z = norm_matmul(h, norm1, w_qkv.astype(BF16), tm=ROW_TILE, tn=QKV_COLS).reshape(B, S, -1)
    half = HEAD_DIM // 2
    inv = jnp.power(ROPE_THETA, -jnp.arange(half, dtype=F32) / half)
    ang = jnp.arange(S, dtype=F32)[:, None] * inv[None, :]
    cos, sin = jnp.cos(ang), jnp.sin(ang)
    cos_full = jnp.concatenate([cos, cos], axis=-1)
    sin_signed = jnp.concatenate([-sin, sin], axis=-1)
    att = moba_attention(z, cos_full, sin_signed, q_norm, k_norm)
    return [(att.reshape(M, -1), w_out.astype(BF16))]


def kernel(x, e_norm1, e_w_in, e_fox_f_bias, e_fox_q_norm, e_fox_k_norm, e_gdn_conv, e_gdn_a_log,
           e_gdn_dt_bias, e_gdn_o_norm, e_w_out, e_norm2, e_ffn_w_gate_up, e_ffn_w_down,
           o_norm1, o_w_qkv, o_q_norm, o_k_norm, o_w_out, o_norm2, o_router, o_exp_w_gate_up, o_exp_w_down):
    B, S, D = x.shape
    h = x.reshape(B * S, D)
    depth = e_norm1.shape[0] + o_norm1.shape[0]
    for layer in range(depth):
        i = layer // 2
        if layer % 2 == 0:
            mix = _even_mix(h, B, S, e_norm1[i], e_w_in[i], e_fox_f_bias[i], e_fox_q_norm[i], e_fox_k_norm[i],
                            e_gdn_conv[i], e_gdn_a_log[i], e_gdn_dt_bias[i], e_gdn_o_norm[i], e_w_out[i])
            h = ffn_residual(h, mix, e_norm2[i], e_ffn_w_gate_up[i].astype(BF16), e_ffn_w_down[i].astype(BF16),
                             tm=FFN_ROWS, tf=FFN_COLS)
        else:
            mix = _odd_mix(h, B, S, o_norm1[i], o_w_qkv[i], o_q_norm[i], o_k_norm[i], o_w_out[i])
            h = moe_residual(h, mix, o_norm2[i], o_router[i], o_exp_w_gate_up[i].astype(BF16),
                             o_exp_w_down[i])
    return h.reshape(B, S, D)
```

```python
import functools

import jax
import jax.numpy as jnp
from jax import lax
from jax.experimental import pallas as pl
from jax.experimental.pallas import tpu as pltpu
from jax.experimental.pallas import tpu_sc as plsc

F32 = jnp.float32
BF16 = jnp.bfloat16

HEAD_DIM = 128
FOX_HEADS = 4
GDN_HEADS = 4
CONV_WIDTH = 4
MOBA_BLOCK = 256
MOBA_TOPK = 3
N_EXPERTS = 8
ROPE_THETA = 10000.0
EPS = 1e-6

LANES = 128
GDN_CHUNK = 128
INV_BLOCK = 16
GDN_HEADS_PER_STEP = 4
GDN_CHUNKS_PER_STEP = 4
NEG = -(2.0 ** 100)
LOG2E = 1.4426950408889634
ATT_BLOCK = 256
ATT_HEADS = 4
KEY_BLOCKS_PER_STEP = 2
V_ROWS = HEAD_DIM + 16
PERM_TILE = 1024
ROUTER_CHAINS = 4
SC_ROWS = 64
ROW_TILE = 1024
MXU_COLS = 256
IN_PROJ_COLS = 7 * MXU_COLS
QKV_COLS = 6 * MXU_COLS
FFN_ROWS, FFN_COLS = 512, 1408
EXPERT_TILE = 1024
EXPERT_COLS = 896
EXPERT_ROW_PARTS = 4
VMEM_LIMIT_BYTES = 56 * 1024 * 1024

FOX_Q0, FOX_K0, FOX_V0 = 0, 4, 8
GDN_Q0, GDN_K0, GDN_V0, GDN_G0 = 12, 16, 20, 24
LANE_F, LANE_A, LANE_B = 0, 4, 8


def _params(*sem):
    return pltpu.CompilerParams(dimension_semantics=sem, vmem_limit_bytes=VMEM_LIMIT_BYTES)


def _rms(x, gain):
    return x * lax.rsqrt(jnp.mean(x * x, axis=-1, keepdims=True) + EPS) * gain


def _dot_nt(a, b, **kw):
    return lax.dot_general(a, b, (((1,), (1,)), ((), ())), preferred_element_type=F32, **kw)


def _pick_lane(x, lane_idx):
    lane = lax.broadcasted_iota(jnp.int32, x.shape, 1)
    return jnp.sum(jnp.where(lane == lane_idx, x, 0.0), axis=-1, keepdims=True)


def _pack_bf16_pairs(x):
    n = x.shape[1] // 2
    hi = pltpu.bitcast(x[:, :n].astype(BF16).astype(F32), jnp.uint32)
    lo = pltpu.bitcast(x[:, n:].astype(BF16).astype(F32), jnp.uint32)
    return pltpu.bitcast(hi | (lo >> 16), jnp.int32)


def _unpack_bf16_pairs(w):
    u = pltpu.bitcast(w, jnp.uint32)
    hi = pltpu.bitcast(u & jnp.uint32(0xFFFF0000), F32).astype(BF16)
    lo = pltpu.bitcast(u << 16, F32).astype(BF16)
    return jnp.concatenate([hi, lo], axis=1)


def _norm_mm_kernel(x_ref, g_ref, w_ref, *rest, has_aux):
    if has_aux:
        waux_ref, o_ref, oaux_ref, xn_ref = rest
    else:
        o_ref, xn_ref = rest

    @pl.when(pl.program_id(1) == 0)
    def _():
        xn = _rms(x_ref[...], g_ref[...]).astype(BF16)
        xn_ref[...] = xn
        if has_aux:
            oaux_ref[...] = jnp.dot(xn, waux_ref[...], preferred_element_type=F32)

    o_ref[...] = jnp.dot(xn_ref[...], w_ref[...], preferred_element_type=F32).astype(o_ref.dtype)


def norm_matmul(x, gain, w, w_aux=None, *, tm, tn, out_dtype=BF16):
    M, K = x.shape
    N = w.shape[1]
    has_aux = w_aux is not None
    in_specs = [
        pl.BlockSpec((tm, K), lambda i, j: (i, 0)),
        pl.BlockSpec((1, K), lambda i, j: (0, 0)),
        pl.BlockSpec((K, tn), lambda i, j: (0, j)),
    ]
    out_shape = [jax.ShapeDtypeStruct((M, N), out_dtype)]
    out_specs = [pl.BlockSpec((tm, tn), lambda i, j: (i, j))]
    args = [x, gain.reshape(1, K), w]
    if has_aux:
        in_specs.append(pl.BlockSpec((K, LANES), lambda i, j: (0, 0)))
        out_shape.append(jax.ShapeDtypeStruct((M, LANES), F32))
        out_specs.append(pl.BlockSpec((tm, LANES), lambda i, j: (i, 0)))
        args.append(w_aux)
    res = pl.pallas_call(
        functools.partial(_norm_mm_kernel, has_aux=has_aux),
        grid=(M // tm, N // tn),
        in_specs=in_specs,
        out_specs=out_specs,
        out_shape=out_shape,
        scratch_shapes=[pltpu.VMEM((tm, K), BF16)],
        compiler_params=_params("parallel", "arbitrary"),
        name="norm_matmul",
    )(*args)
    return res if has_aux else res[0]


def _plus_projections(x, proj_refs, rows=slice(None)):
    for a_ref, w_ref in zip(proj_refs[0::2], proj_refs[1::2]):
        x = x + jnp.dot(a_ref[rows, :], w_ref[...], preferred_element_type=F32)
    return x


def _projection_specs(pairs, tm):
    specs, args = [], []
    for a, w in pairs:
        specs.append(pl.BlockSpec((tm, a.shape[1]), lambda i, *_: (i, 0)))
        specs.append(pl.BlockSpec(w.shape, lambda i, *_: (0, 0)))
        args += [a, w]
    return specs, args


def _gate_kernel(zs_ref, par_ref, col_ref, row_ref, *, S, B):
    C = GDN_CHUNK
    bias = par_ref[0:1, :]
    neg_a = -jnp.exp(par_ref[1:2, :])
    r = lax.broadcasted_iota(jnp.int32, (C, C), 0)
    c = lax.broadcasted_iota(jnp.int32, (C, C), 1)
    tril = (r >= c).astype(F32)
    lane = lax.broadcasted_iota(jnp.int32, (C, LANES), 1)

    def body(n, carry):
        sl = pl.ds(pl.multiple_of(n * C, C), C)
        us, betas = [], []
        for bi in range(B):
            z = zs_ref[bi, sl, :]
            t = z + bias
            soft = jnp.log(1.0 + jnp.exp(-jnp.abs(t)))
            log_f = jnp.minimum(t, 0.0) - soft
            g = neg_a * (jnp.maximum(t, 0.0) + soft)
            betas.append(1.0 / (1.0 + jnp.exp(-z)))
            us.append(jnp.where(lane < LANE_A, log_f, jnp.where(lane < LANE_B, g, 0.0)))
        sums = jnp.dot(tril, jnp.concatenate(us, axis=1), preferred_element_type=F32, precision=lax.Precision.HIGHEST)
        last = []
        for bi in range(B):
            cs = sums[:, bi * LANES:(bi + 1) * LANES] + jnp.where(lane < LANE_A, carry[bi], 0.0)
            out = jnp.where(lane < LANE_B, cs, betas[bi])
            col_ref[bi, sl, :] = out
            out_t = out.T
            for hd in range(GDN_HEADS):
                row_ref[bi, hd, :, sl] = out_t[LANE_A + hd:LANE_A + hd + 1, :]
            last.append(cs[C - 1:C, :])
        return tuple(last)

    lax.fori_loop(0, S // C, body, tuple(jnp.zeros((1, LANES), F32) for _ in range(B)))


def even_gates(zs, par):
    B, S, _ = zs.shape
    return pl.pallas_call(
        functools.partial(_gate_kernel, S=S, B=B),
        grid=(1,),
        in_specs=[
            pl.BlockSpec((B, S, LANES), lambda i: (0, 0, 0)),
            pl.BlockSpec((8, LANES), lambda i: (0, 0)),
        ],
        out_specs=[
            pl.BlockSpec((B, S, LANES), lambda i: (0, 0, 0)),
            pl.BlockSpec((B, GDN_HEADS, 1, S), lambda i: (0, 0, 0, 0)),
        ],
        out_shape=[jax.ShapeDtypeStruct((B, S, LANES), F32), jax.ShapeDtypeStruct((B, GDN_HEADS, 1, S), F32)],
        compiler_params=_params("arbitrary"),
        name="even_gates",
    )(zs, par)


def _split3(x):
    hi = x.astype(BF16).astype(F32)
    mid = (x - hi).astype(BF16).astype(F32)
    lo = (x - hi - mid).astype(BF16).astype(F32)
    return hi, mid, lo


def _interleave(gens):
    out = [None] * len(gens)
    live = list(range(len(gens)))
    while live:
        for n in list(live):
            try:
                next(gens[n])
            except StopIteration as stop:
                out[n] = stop.value
                live.remove(n)
    return out


def _attend(state, c, qa, ka_ref, vt_ref, key0, nkeys, keep=None):
    ks = pl.ds(pl.multiple_of(key0, ATT_BLOCK), nkeys)
    st = _dot_nt(ka_ref[ks, :], qa)
    yield
    if keep is not None:
        st = jnp.where(keep, st, NEG)
    m_new = jnp.max(st, axis=0, keepdims=True)
    if state[c] is not None:
        m_old, acc_old = state[c]
        m_new = jnp.maximum(m_old, m_new)
    p = jnp.exp2(st - m_new).astype(BF16)
    pv = jnp.dot(vt_ref[:, ks], p, preferred_element_type=F32)
    state[c] = (m_new, pv if state[c] is None else acc_old * jnp.exp2(m_old - m_new) + pv)


def _attend_tile_pair(i, prep, ka_ref, vt_ref, m_ref, acc_ref, o_ref):
    t = ATT_BLOCK
    chains = [(hd, c) for c in range(2) for hd in range(ATT_HEADS)]
    num = lambda hd, c: 2 * hd + c
    qa = _interleave(prep)

    def step(state, hd, c, which, **kw):
        return _attend(state, num(hd, c), qa[num(hd, c)][which], ka_ref.at[hd], vt_ref.at[hd], **kw)

    def save(state):
        m_ref[...] = jnp.stack([state[n][0] for n in range(len(chains))])
        acc_ref[...] = jnp.stack([state[n][1] for n in range(len(chains))])

    def past_blocks(key0, n_steps):
        state = {n: (m_ref[n], acc_ref[n]) for n in range(len(chains))}
        _interleave([step(state, hd, c, 1, key0=key0 + s * 2 * t, nkeys=2 * t)
                     for s in range(n_steps) for hd, c in chains])
        save(state)

    state = {n: None for n in range(len(chains))}
    keep = _causal_keep()
    _interleave([step(state, hd, c, 0, key0=(2 * i + c) * t, nkeys=t, keep=keep) for hd, c in chains]
                + [step(state, hd, 1, 1, key0=(2 * i) * t, nkeys=t) for hd in range(ATT_HEADS)])
    save(state)

    def four_blocks(g, _):
        past_blocks(g * (4 * t), 2)
        return 0

    lax.fori_loop(0, i // 2, four_blocks, 0)

    @pl.when((i & 1) != 0)
    def _():
        past_blocks((i // 2) * (4 * t), 1)

    for hd, c in chains:
        acc = acc_ref[num(hd, c)]
        out_t = acc[:HEAD_DIM, :] * (1.0 / acc[HEAD_DIM:HEAD_DIM + 1, :])
        o_ref[c * t:(c + 1) * t, hd * HEAD_DIM:(hd + 1) * HEAD_DIM] = out_t.T.astype(o_ref.dtype)


def _transposed(v):
    r = lax.broadcasted_iota(jnp.int32, (HEAD_DIM, HEAD_DIM), 0)
    c = lax.broadcasted_iota(jnp.int32, (HEAD_DIM, HEAD_DIM), 1)
    v_t = _dot_nt(jnp.where(r == c, 1.0, 0.0).astype(BF16), v).astype(BF16)
    return jnp.concatenate([v_t, jnp.ones((V_ROWS - HEAD_DIM, ATT_BLOCK), BF16)], axis=0)


def _causal_keep():
    key = lax.broadcasted_iota(jnp.int32, (ATT_BLOCK, ATT_BLOCK), 0)
    qry = lax.broadcasted_iota(jnp.int32, (ATT_BLOCK, ATT_BLOCK), 1)
    return key <= qry


_ATT_SCRATCH = lambda S: [pltpu.VMEM((ATT_HEADS, S, 2 * HEAD_DIM), BF16), pltpu.VMEM((ATT_HEADS, V_ROWS, S), BF16),
                          pltpu.VMEM((2 * ATT_HEADS, 1, ATT_BLOCK), F32),
                          pltpu.VMEM((2 * ATT_HEADS, V_ROWS, ATT_BLOCK), F32)]


def _fox_kernel(q_ref, k_ref, v_ref, col_ref, qg_ref, kg_ref, o_ref, ka_ref, vt_ref, m_ref, acc_ref, *, S):
    h0 = pl.program_id(1) * ATT_HEADS
    i = pl.program_id(2)
    t = ATT_BLOCK
    D = HEAD_DIM
    lane = lax.broadcasted_iota(jnp.int32, (t, LANES), 1)

    @pl.when(i == 0)
    def _():
        def prep_keys(g, _):
            todo = []
            for u in range(KEY_BLOCKS_PER_STEP):
                sl = pl.ds(pl.multiple_of((g * KEY_BLOCKS_PER_STEP + u) * t, t), t)
                for hd in range(ATT_HEADS):
                    cols = slice(hd * D, (hd + 1) * D)
                    kn = _rms(k_ref[sl, cols].astype(F32), kg_ref[...]).astype(BF16)
                    hi, mid, lo = _split3(-LOG2E * _pick_lane(col_ref[sl, :], LANE_F + h0 + hd))
                    aug = jnp.where(lane < 3, 1.0, jnp.where(lane == 3, hi, jnp.where(lane == 4, mid,
                                                                                    jnp.where(lane == 5, lo, 0.0))))
                    todo.append((hd, sl, jnp.concatenate([kn, aug.astype(BF16)], axis=1), _transposed(v_ref[sl, cols])))
            for hd, sl, ka, vt in todo:
                ka_ref[hd, sl, :] = ka
                vt_ref[hd, :, sl] = vt
            return 0

        lax.fori_loop(0, S // (t * KEY_BLOCKS_PER_STEP), prep_keys, 0)

    def prep(hd, c):
        q = q_ref[c * t:(c + 1) * t, hd * D:(hd + 1) * D].astype(F32)
        qn = (_rms(q, qg_ref[...]) * (LOG2E * D ** -0.5)).astype(BF16)
        yield
        qsl = pl.ds(pl.multiple_of((2 * i + c) * t, t), t)
        hi, mid, lo = _split3(LOG2E * _pick_lane(col_ref[qsl, :], LANE_F + h0 + hd))
        aug = jnp.where(lane == 0, hi,
                        jnp.where(lane == 1, mid, jnp.where(lane == 2, lo, jnp.where(lane < 6, 1.0, 0.0))))
        qa = jnp.concatenate([qn, aug.astype(BF16)], axis=-1)
        return qa, qa

    _attend_tile_pair(i, [prep(hd, c) for hd in range(ATT_HEADS) for c in range(2)],
                      ka_ref, vt_ref, m_ref, acc_ref, o_ref)


def fox_attention(z, col, q_gain, k_gain):
    B, S, _ = z.shape
    t = 2 * ATT_BLOCK
    w = ATT_HEADS * HEAD_DIM
    return pl.pallas_call(
        functools.partial(_fox_kernel, S=S),
        grid=(B, FOX_HEADS // ATT_HEADS, S // t),
        in_specs=[
            pl.BlockSpec((None, t, w), lambda b, h, i: (b, i, FOX_Q0 // ATT_HEADS + h)),
            pl.BlockSpec((None, S, w), lambda b, h, i: (b, 0, FOX_K0 // ATT_HEADS + h)),
            pl.BlockSpec((None, S, w), lambda b, h, i: (b, 0, FOX_V0 // ATT_HEADS + h)),
            pl.BlockSpec((None, S, LANES), lambda b, h, i: (b, 0, 0)),
            pl.BlockSpec((1, HEAD_DIM), lambda b, h, i: (0, 0)),
            pl.BlockSpec((1, HEAD_DIM), lambda b, h, i: (0, 0)),
        ],
        out_specs=pl.BlockSpec((None, t, w), lambda b, h, i: (b, i, h)),
        out_shape=jax.ShapeDtypeStruct((B, S, FOX_HEADS * HEAD_DIM), BF16),
        scratch_shapes=_ATT_SCRATCH(S),
        compiler_params=_params("parallel", "parallel", "arbitrary"),
        name="fox_attention",
    )(z, z, z, col, q_gain.reshape(1, -1), k_gain.reshape(1, -1))


def _unit_lower_inverse(m):
    n = m.shape[0]
    r = lax.broadcasted_iota(jnp.int32, (n, n), 0)
    c = lax.broadcasted_iota(jnp.int32, (n, n), 1)
    eye = (r == c).astype(F32)

    def same_block(b):
        return (r // b) == (c // b)

    p = jnp.where(same_block(INV_BLOCK), m, 0.0)
    inv = eye - p
    k = 2
    while k < INV_BLOCK:
        pb = p.astype(BF16)
        p = jnp.dot(pb, pb, preferred_element_type=F32)
        yield
        inv = jnp.dot(inv.astype(BF16), (eye + p).astype(BF16), preferred_element_type=F32)
        yield
        k *= 2
    b = INV_BLOCK
    while b < n:
        off = jnp.where(same_block(2 * b), jnp.where(same_block(b), 0.0, m), 0.0).astype(BF16)
        ib = inv.astype(BF16)
        left = jnp.dot(ib, off, preferred_element_type=F32).astype(BF16)
        yield
        inv = inv - jnp.dot(left, ib, preferred_element_type=F32)
        yield
        b *= 2
    return inv


def _gdn_kernel(q_ref, k_ref, v_ref, gg_ref, wq_ref, wk_ref, wv_ref, col_ref, row_ref, on_ref,
                o_ref, qs_ref, ks_ref, vs_ref, *, S, rows, hb):
    h0 = pl.program_id(1) * hb
    C = GDN_CHUNK
    D = HEAD_DIM

    streams = [(x_ref, w_ref, dst_ref, hh, mode) for hh in range(hb)
               for x_ref, w_ref, dst_ref, mode in ((q_ref, wq_ref, qs_ref, "q"), (k_ref, wk_ref, ks_ref, "k"),
                                                   (v_ref, wv_ref, vs_ref, "v"))]
    halo = 16

    def conv_step(base, windows):
        outs = []
        for (x_ref, w_ref, dst_ref, hh, mode), win in zip(streams, windows):
            w = w_ref[:, hh * D:(hh + 1) * D]
            y = jnp.zeros((rows, D), F32)
            for tap in range(CONV_WIDTH):
                lead = halo - (CONV_WIDTH - 1) + tap
                y = y + w[tap:tap + 1, :] * pltpu.roll(win, rows + halo - lead, 0)[0:rows, :]
            y = y / (1.0 + jnp.exp(-y))
            if mode != "v":
                y = y * lax.rsqrt(jnp.sum(y * y, axis=-1, keepdims=True) + EPS)
            if mode == "q":
                y = y * D ** -0.5
            outs.append(y)
        for (x_ref, w_ref, dst_ref, hh, mode), y in zip(streams, outs):
            dst_ref[hh, pl.ds(base, rows), :] = y.astype(dst_ref.dtype)

    conv_step(0, [jnp.concatenate([jnp.zeros((halo, D), F32), x_ref[0:rows, hh * D:(hh + 1) * D].astype(F32)], axis=0)
                  for x_ref, _, _, hh, _ in streams])

    def conv_rest(n, _):
        base = pl.multiple_of(n * rows, rows)
        conv_step(base, [x_ref[pl.ds(pl.multiple_of(base - halo, halo), rows + halo), hh * D:(hh + 1) * D].astype(F32)
                         for x_ref, _, _, hh, _ in streams])
        return 0

    lax.fori_loop(1, S // rows, conv_rest, 0)

    r = lax.broadcasted_iota(jnp.int32, (C, C), 0)
    c = lax.broadcasted_iota(jnp.int32, (C, C), 1)
    incl = r >= c
    strict = r > c

    def chunk_local(hh, sl, tab):
        q = qs_ref[hh, sl, :].astype(F32)
        k = ks_ref[hh, sl, :].astype(F32)
        v = vs_ref[hh, sl, :].astype(F32)
        gcol = _pick_lane(tab, LANE_A + h0 + hh)
        beta = _pick_lane(tab, LANE_B + h0 + hh)
        grow = row_ref[hh, :, sl]
        glast = gcol[C - 1:C, :]
        decay = jnp.where(incl, jnp.exp(jnp.where(incl, gcol - grow, 0.0)), 0.0)
        eg = jnp.exp(gcol)
        kb = k * beta
        kbf = k.astype(BF16)
        m = jnp.where(strict, _dot_nt(kb.astype(BF16), kbf) * decay, 0.0)
        attn = (_dot_nt(q.astype(BF16), kbf) * decay).astype(BF16)
        yield
        tinv = (yield from _unit_lower_inverse(m)).astype(BF16)
        rhs = jnp.concatenate([v * beta, kb * eg], axis=-1).astype(BF16)
        sol = jnp.dot(tinv, rhs, preferred_element_type=F32)
        yield
        gate = gg_ref[sl, hh * D:(hh + 1) * D].astype(F32)
        return dict(u=sol[:, :D], w=sol[:, D:].astype(BF16), attn=attn, qg=(q * eg).astype(BF16),
                    kg_t=(k * jnp.exp(glast - gcol)).T.astype(BF16), keep=jnp.exp(glast),
                    gate=gate / (1.0 + jnp.exp(-gate)))

    def chunk_state(parts, state):
        outs = []
        for c in parts:
            sb = state.astype(BF16)
            v_new = c["u"] - jnp.dot(c["w"], sb, preferred_element_type=F32)
            o_state = jnp.dot(c["qg"], sb, preferred_element_type=F32)
            yield
            vb = v_new.astype(BF16)
            o = o_state + jnp.dot(c["attn"], vb, preferred_element_type=F32)
            state = state * c["keep"] + jnp.dot(c["kg_t"], vb, preferred_element_type=F32)
            yield
            outs.append((_rms(o, on_ref[...]) * c["gate"]).astype(o_ref.dtype))
        return state, outs

    def chunk_group(n, states):
        sls = [pl.ds(pl.multiple_of((n * GDN_CHUNKS_PER_STEP + g) * C, C), C) for g in range(GDN_CHUNKS_PER_STEP)]
        tabs = [col_ref[sl, :] for sl in sls]
        parts = _interleave([chunk_local(hh, sls[g], tabs[g]) for g in range(GDN_CHUNKS_PER_STEP) for hh in range(hb)])
        res = _interleave([chunk_state([parts[g * hb + hh] for g in range(GDN_CHUNKS_PER_STEP)], states[hh])
                           for hh in range(hb)])
        for g in range(GDN_CHUNKS_PER_STEP):
            o_ref[sls[g], :] = jnp.concatenate([outs[g] for _, outs in res], axis=-1)
        return tuple(s for s, _ in res)

    lax.fori_loop(0, S // (C * GDN_CHUNKS_PER_STEP), chunk_group, tuple(jnp.zeros((D, D), F32) for _ in range(hb)))


def gdn_mixer(z, conv_w, col, row, o_gain):
    B, S, _ = z.shape
    D = HEAD_DIM
    hb = GDN_HEADS_PER_STEP
    rows = 256
    seq = lambda off: pl.BlockSpec((None, S, hb * D), lambda b, h: (b, 0, off // hb + h),
                                   pipeline_mode=pl.Buffered(1))
    cw = lambda off: pl.BlockSpec((CONV_WIDTH, hb * D), lambda b, h: (0, off // hb + h))
    return pl.pallas_call(
        functools.partial(_gdn_kernel, S=S, rows=rows, hb=hb),
        grid=(B, GDN_HEADS // hb),
        in_specs=[
            seq(GDN_Q0), seq(GDN_K0), seq(GDN_V0), seq(GDN_G0),
            cw(0), cw(GDN_HEADS), cw(2 * GDN_HEADS),
            pl.BlockSpec((None, S, LANES), lambda b, h: (b, 0, 0)),
            pl.BlockSpec((None, hb, 1, S), lambda b, h: (b, h, 0, 0)),
            pl.BlockSpec((1, D), lambda b, h: (0, 0)),
        ],
        out_specs=pl.BlockSpec((None, S, hb * D), lambda b, h: (b, 0, h)),
        out_shape=jax.ShapeDtypeStruct((B, S, GDN_HEADS * D), BF16),
        scratch_shapes=[pltpu.VMEM((hb, S, D), BF16)] * 3,
        compiler_params=_params("parallel", "parallel"),
        name="gdn_mixer",
    )(z, z, z, z, conv_w, conv_w, conv_w, col, row, o_gain.reshape(1, D))


def _rope(x, cos, sin_signed):
    return x * cos + pltpu.roll(x, HEAD_DIM // 2, 1) * sin_signed


def _moba_kernel(q_ref, k_ref, v_ref, cos_ref, sin_ref, qg_ref, kg_ref, o_ref,
                 ka_ref, vt_ref, m_ref, acc_ref, kmean_ref, *, S):
    i = pl.program_id(2)
    t = ATT_BLOCK
    D = HEAD_DIM
    lane = lax.broadcasted_iota(jnp.int32, (t, LANES), 1)

    @pl.when(i == 0)
    def _():
        kmean_ref[...] = jnp.zeros(kmean_ref.shape, F32)

        def prep_keys(g, _):
            todo = []
            for u in range(KEY_BLOCKS_PER_STEP):
                n = g * KEY_BLOCKS_PER_STEP + u
                sl = pl.ds(pl.multiple_of(n * t, t), t)
                onehot = jnp.where(lane == n, 1.0, 0.0).astype(BF16)
                for hd in range(ATT_HEADS):
                    cols = slice(hd * D, (hd + 1) * D)
                    k = _rope(_rms(k_ref[sl, cols].astype(F32), kg_ref[...]), cos_ref[sl, :], sin_ref[sl, :])
                    todo.append((hd, n, sl, jnp.concatenate([k.astype(BF16), onehot], axis=1),
                                 jnp.mean(k, axis=0, keepdims=True), _transposed(v_ref[sl, cols])))
            for hd, n, sl, ka, kmean, vt in todo:
                ka_ref[hd, sl, :] = ka
                kmean_ref[hd, pl.ds(n, 1), :] = kmean
                vt_ref[hd, :, sl] = vt
            return 0

        lax.fori_loop(0, S // (t * KEY_BLOCKS_PER_STEP), prep_keys, 0)

    nb = -(-(S // t) // 8) * 8

    def prep(hd, c):
        cur = 2 * i + c
        qsl = pl.ds(pl.multiple_of(cur * t, t), t)
        q = _rope(_rms(q_ref[c * t:(c + 1) * t, hd * D:(hd + 1) * D].astype(F32), qg_ref[...]),
                  cos_ref[qsl, :], sin_ref[qsl, :])
        yield
        kmean = kmean_ref[hd]
        km_hi = kmean.astype(BF16)
        km_split = jnp.concatenate([km_hi, (kmean - km_hi.astype(F32)).astype(BF16)], axis=0)
        q_hi = q.astype(BF16)
        q_lo = (q - q_hi.astype(F32)).astype(BF16)
        part = _dot_nt(km_split, q_hi)
        gate = (part[:LANES] + part[LANES:] + _dot_nt(km_split[:LANES, :], q_lo))[0:nb]
        yield
        blk = lax.broadcasted_iota(jnp.int32, (nb, t), 0)
        blk_f = blk.astype(F32)
        gate = jnp.where(blk < cur, gate, -jnp.inf)
        sel = jnp.full((nb, t), NEG, F32)
        for _ in range(MOBA_TOPK):
            top = jnp.max(gate, axis=0, keepdims=True)
            first = jnp.min(jnp.where(gate == top, blk_f, float(LANES)), axis=0, keepdims=True)
            pick = blk_f == first
            sel = jnp.where(pick & (first < cur.astype(F32)), 0.0, sel)
            gate = jnp.where(pick, -jnp.inf, gate)
        sel_bias = jnp.concatenate([sel, jnp.full((LANES - nb, t), NEG, F32)], axis=0).T
        qs = (q * (LOG2E * D ** -0.5)).astype(BF16)
        return (jnp.concatenate([qs, jnp.zeros((t, LANES), BF16)], axis=-1),
                jnp.concatenate([qs, sel_bias.astype(BF16)], axis=-1))

    _attend_tile_pair(i, [prep(hd, c) for hd in range(ATT_HEADS) for c in range(2)],
                      ka_ref, vt_ref, m_ref, acc_ref, o_ref)


def moba_attention(z, cos, sin_signed, q_gain, k_gain):
    B, S, W = z.shape
    H = W // (3 * HEAD_DIM)
    assert ATT_BLOCK == MOBA_BLOCK
    t = 2 * ATT_BLOCK
    w = ATT_HEADS * HEAD_DIM
    hp = H // ATT_HEADS
    return pl.pallas_call(
        functools.partial(_moba_kernel, S=S),
        grid=(B, hp, S // t),
        in_specs=[
            pl.BlockSpec((None, t, w), lambda b, h, i: (b, i, h)),
            pl.BlockSpec((None, S, w), lambda b, h, i: (b, 0, hp + h)),
            pl.BlockSpec((None, S, w), lambda b, h, i: (b, 0, 2 * hp + h)),
            pl.BlockSpec((S, HEAD_DIM), lambda b, h, i: (0, 0)),
            pl.BlockSpec((S, HEAD_DIM), lambda b, h, i: (0, 0)),
            pl.BlockSpec((1, HEAD_DIM), lambda b, h, i: (0, 0)),
            pl.BlockSpec((1, HEAD_DIM), lambda b, h, i: (0, 0)),
        ],
        out_specs=pl.BlockSpec((None, t, w), lambda b, h, i: (b, i, h)),
        out_shape=jax.ShapeDtypeStruct((B, S, H * HEAD_DIM), BF16),
        scratch_shapes=_ATT_SCRATCH(S) + [pltpu.VMEM((ATT_HEADS, LANES, HEAD_DIM), F32)],
        compiler_params=_params("parallel", "parallel", "arbitrary"),
        name="moba_attention",
    )(z, z, z, cos, sin_signed, q_gain.reshape(1, -1), k_gain.reshape(1, -1))


META_IDX, META_RANK, META_GATE = 0, 2, 4


def _router_kernel(x_ref, *refs, n_proj):
    proj_refs = refs[:2 * n_proj]
    g_ref, w_ref, y_ref, xn_ref, meta_ref, cnt_ref, carry_ref = refs[2 * n_proj:]
    i = pl.program_id(0)
    tm = x_ref.shape[0]

    @pl.when(i == 0)
    def _():
        carry_ref[...] = jnp.zeros(carry_ref.shape, F32)

    w = w_ref[...]
    w_hi = w.astype(BF16)
    w_lo = (w - w_hi.astype(F32)).astype(BF16)
    part = tm // ROUTER_CHAINS
    lane = lax.broadcasted_iota(jnp.int32, (part, LANES), 1)
    lane_f = lane.astype(F32)

    def route(n):
        rows = slice(n * part, (n + 1) * part)
        y = _plus_projections(x_ref[rows, :], proj_refs, rows)
        y_ref[rows, :] = y
        yield
        xn = _rms(y, g_ref[...])
        xn_ref[rows, :] = _pack_bf16_pairs(xn)
        x_hi = xn.astype(BF16)
        x_lo = (xn - x_hi.astype(F32)).astype(BF16)
        logits = (jnp.dot(x_hi, w_hi, preferred_element_type=F32) + jnp.dot(x_hi, w_lo, preferred_element_type=F32)
                  + jnp.dot(x_lo, w_hi, preferred_element_type=F32))
        yield
        logits = jnp.where(lane < N_EXPERTS, logits, -jnp.inf)
        top1 = jnp.max(logits, axis=-1, keepdims=True)
        yield
        idx1 = jnp.min(jnp.where(logits == top1, lane_f, float(LANES)), axis=-1, keepdims=True)
        yield
        rest = jnp.where(lane_f == idx1, -jnp.inf, logits)
        top2 = jnp.max(rest, axis=-1, keepdims=True)
        yield
        idx2 = jnp.min(jnp.where(rest == top2, lane_f, float(LANES)), axis=-1, keepdims=True)
        e2 = jnp.exp(top2 - top1)
        denom = 1.0 + e2
        return idx1, idx2, 1.0 / denom, e2 / denom, jnp.where((lane_f == idx1) | (lane_f == idx2), 1.0, 0.0)

    picks = _interleave([route(n) for n in range(ROUTER_CHAINS)])
    chosen = jnp.concatenate([p[4] for p in picks], axis=0)
    r = lax.broadcasted_iota(jnp.int32, (tm, tm), 0)
    c = lax.broadcasted_iota(jnp.int32, (tm, tm), 1)
    ahead = jnp.dot(jnp.where(r > c, 1.0, 0.0).astype(BF16), chosen.astype(BF16), preferred_element_type=F32)
    carry = carry_ref[...]
    for n, (idx1, idx2, g1, g2, _) in enumerate(picks):
        rows = slice(n * part, (n + 1) * part)
        rank = ahead[rows, :] + carry
        rank1 = jnp.sum(jnp.where(lane_f == idx1, rank, 0.0), axis=-1, keepdims=True)
        rank2 = jnp.sum(jnp.where(lane_f == idx2, rank, 0.0), axis=-1, keepdims=True)
        meta = jnp.zeros((part, LANES), F32)
        for k, v in enumerate((idx1, idx2, rank1, rank2, g1, g2)):
            meta = jnp.where(lane == k, v, meta)
        meta_ref[rows, :] = meta
    carry = carry + jnp.sum(chosen, axis=0, keepdims=True)
    carry_ref[...] = carry
    cnt_ref[...] = jnp.broadcast_to(carry, cnt_ref.shape)


def moe_router(x, proj, gain, w_router, *, tm):
    M, K = x.shape
    w = jnp.zeros((K, LANES), F32).at[:, :N_EXPERTS].set(w_router)
    proj_specs, proj_args = _projection_specs(proj, tm)
    return pl.pallas_call(
        functools.partial(_router_kernel, n_proj=len(proj)),
        grid=(M // tm,),
        in_specs=[
            pl.BlockSpec((tm, K), lambda i: (i, 0)),
            *proj_specs,
            pl.BlockSpec((1, K), lambda i: (0, 0)),
            pl.BlockSpec((K, LANES), lambda i: (0, 0)),
        ],
        out_specs=[
            pl.BlockSpec((tm, K), lambda i: (i, 0)),
            pl.BlockSpec((tm, K // 2), lambda i: (i, 0)),
            pl.BlockSpec((tm, LANES), lambda i: (i, 0)),
            pl.BlockSpec((8, LANES), lambda i: (0, 0)),
        ],
        out_shape=[
            jax.ShapeDtypeStruct((M, K), F32),
            jax.ShapeDtypeStruct((M, K // 2), jnp.int32),
            jax.ShapeDtypeStruct((M, LANES), F32),
            jax.ShapeDtypeStruct((8, LANES), F32),
        ],
        scratch_shapes=[pltpu.VMEM((1, LANES), F32)],
        compiler_params=_params("arbitrary"),
        name="moe_router",
    )(x, *proj_args, gain.reshape(1, K), w)


def _sc_workers():
    info = plsc.get_sparse_core_info()
    return info.num_cores, info.num_cores * info.num_subcores


def scatter_rows(rows, dest, n_out):
    M, W = rows.shape
    nc, nw = _sc_workers()
    assert M % (nw * SC_ROWS) == 0
    per_w = M // nw
    mesh = plsc.VectorSubcoreMesh(core_axis_name="c", subcore_axis_name="s")

    @functools.partial(
        pl.kernel, mesh=mesh, out_type=jax.ShapeDtypeStruct((n_out, W), rows.dtype),
        scratch_types=[pltpu.VMEM((SC_ROWS,), jnp.int32), pltpu.VMEM((SC_ROWS, W), rows.dtype),
                       pltpu.SemaphoreType.DMA])
    def kern(rows_hbm, dest_hbm, out_hbm, idx_v, rows_v, sem):
        wid = lax.axis_index("s") * nc + lax.axis_index("c")

        @pl.loop(0, per_w // SC_ROWS)
        def _(g):
            base = wid * per_w + g * SC_ROWS
            pltpu.sync_copy(rows_hbm.at[pl.ds(base, SC_ROWS)], rows_v)
            for k in range(2):
                pltpu.sync_copy(dest_hbm.at[k, pl.ds(base, SC_ROWS)], idx_v)
                pltpu.async_copy(rows_v, out_hbm.at[idx_v], sem).wait()

    return kern(rows, dest)


def gather_rows(table, idx):
    N = idx.shape[0]
    W = table.shape[1]
    nc, nw = _sc_workers()
    assert N % (nw * SC_ROWS) == 0
    per_w = N // nw
    mesh = plsc.VectorSubcoreMesh(core_axis_name="c", subcore_axis_name="s")

    @functools.partial(
        pl.kernel, mesh=mesh, out_type=jax.ShapeDtypeStruct((N, W), table.dtype),
        scratch_types=[pltpu.VMEM((SC_ROWS,), jnp.int32), pltpu.VMEM((SC_ROWS, W), table.dtype),
                       pltpu.SemaphoreType.DMA])
    def kern(table_hbm, idx_hbm, out_hbm, idx_v, rows_v, sem):
        wid = lax.axis_index("s") * nc + lax.axis_index("c")

        @pl.loop(0, per_w // SC_ROWS)
        def _(g):
            base = wid * per_w + g * SC_ROWS
            pltpu.sync_copy(idx_hbm.at[pl.ds(base, SC_ROWS)], idx_v)
            pltpu.async_copy(table_hbm.at[idx_v], rows_v, sem).wait()
            pltpu.sync_copy(rows_v, out_hbm.at[pl.ds(base, SC_ROWS)])

    return kern(table, idx)


def _combine_kernel(h_ref, y1_ref, y2_ref, meta_ref, o_ref):
    g1 = meta_ref[:, META_GATE:META_GATE + 1]
    g2 = meta_ref[:, META_GATE + 1:META_GATE + 2]
    o_ref[...] = (h_ref[...] + g1 * _unpack_bf16_pairs(y1_ref[...]).astype(F32)
                  + g2 * _unpack_bf16_pairs(y2_ref[...]).astype(F32))


def moe_combine(h, y_pairs, meta):
    M, K = h.shape
    p = PERM_TILE
    nt = M // p
    return pl.pallas_call(
        _combine_kernel,
        grid=(nt,),
        in_specs=[
            pl.BlockSpec((p, K), lambda i: (i, 0)),
            pl.BlockSpec((p, K // 2), lambda i: (i, 0)),
            pl.BlockSpec((p, K // 2), lambda i: (nt + i, 0)),
            pl.BlockSpec((p, LANES), lambda i: (i, 0)),
        ],
        out_specs=pl.BlockSpec((p, K), lambda i: (i, 0)),
        out_shape=jax.ShapeDtypeStruct((M, K), F32),
        compiler_params=_params("parallel"),
        name="moe_combine",
    )(h, y_pairs, y_pairs, meta)


def _expert_ffn_kernel(te_ref, tv_ref, x_ref, wg_ref, wu_ref, wd_ref, o_ref, acc_ref, *, nf):
    del te_ref
    i = pl.program_id(0)
    j = pl.program_id(1)
    tm = x_ref.shape[0]
    valid = tv_ref[i]

    def swiglu_rows(rows):
        x = _unpack_bf16_pairs(x_ref[0:rows, :])
        gt = jnp.dot(x, wg_ref[...].astype(BF16), preferred_element_type=F32)
        up = jnp.dot(x, wu_ref[...].astype(BF16), preferred_element_type=F32)
        act = (gt / (1.0 + jnp.exp(-gt)) * up).astype(BF16)
        part = jnp.dot(act, wd_ref[...].astype(BF16), preferred_element_type=F32)

        def finish(total):
            o_ref[0:rows, :] = _pack_bf16_pairs(total)
            if rows < tm:
                o_ref[rows:tm, :] = jnp.zeros((tm - rows, o_ref.shape[1]), o_ref.dtype)

        if nf == 1:
            finish(part)
            return

        @pl.when(j == 0)
        def _():
            acc_ref[0:rows, :] = part

        @pl.when((j > 0) & (j < nf - 1))
        def _():
            acc_ref[0:rows, :] += part

        @pl.when(j == nf - 1)
        def _():
            finish(acc_ref[0:rows, :] + part)

    part = tm // EXPERT_ROW_PARTS
    for n in range(1, EXPERT_ROW_PARTS + 1):
        pl.when((valid > (n - 1) * part) & (valid <= n * part))(functools.partial(swiglu_rows, n * part))

    @pl.when((valid == 0) & (j == 0))
    def _():
        o_ref[...] = jnp.zeros(o_ref.shape, o_ref.dtype)


def expert_ffn(xs, tile_expert, tile_valid, w_gate_up, w_down, *, tm, tf):
    R = xs.shape[0]
    E, F, K = w_down.shape
    nf = F // tf
    live = lambda i, tv: tv[i] > 0
    col = lambda i, j, tv: jnp.where(live(i, tv), j, nf - 1)
    grid_spec = pltpu.PrefetchScalarGridSpec(
        num_scalar_prefetch=2,
        grid=(R // tm, nf),
        in_specs=[
            pl.BlockSpec((tm, K // 2), lambda i, j, te, tv: (te[R // tm + i], 0)),
            pl.BlockSpec((None, K, tf), lambda i, j, te, tv: (te[i], 0, col(i, j, tv))),
            pl.BlockSpec((None, K, tf), lambda i, j, te, tv: (te[i], 0, nf + col(i, j, tv))),
            pl.BlockSpec((None, tf, K), lambda i, j, te, tv: (te[i], col(i, j, tv), 0)),
        ],
        out_specs=pl.BlockSpec((tm, K // 2), lambda i, j, te, tv: (i, 0)),
        scratch_shapes=[pltpu.VMEM((tm, K), F32)],
    )
    return pl.pallas_call(
        functools.partial(_expert_ffn_kernel, nf=nf),
        grid_spec=grid_spec,
        out_shape=jax.ShapeDtypeStruct((R, K // 2), jnp.int32),
        compiler_params=_params("arbitrary", "arbitrary"),
        name="moe_expert_ffn",
    )(tile_expert, tile_valid, xs, w_gate_up, w_gate_up, w_down)


def moe_residual(h, proj, gain, w_router, w_gate_up, w_down):
    M, K = h.shape
    p = PERM_TILE
    tm = EXPERT_TILE
    n_rows = 2 * M + N_EXPERTS * tm
    h, xn, meta, cnt = moe_router(h, proj, gain, w_router, tm=p)

    counts = cnt[0, :N_EXPERTS].astype(jnp.int32)
    padded = (counts + tm - 1) // tm * tm
    ends = jnp.cumsum(padded)
    offsets = ends - padded
    n_tiles = n_rows // tm
    tile_row = jnp.arange(n_tiles) * tm
    n_used = ends[-1] // tm
    last_used = jnp.minimum(tile_row // tm, n_used - 1)
    expert_of = lambda row: jnp.minimum(jnp.sum(ends[None, :] <= row[:, None], axis=1), N_EXPERTS - 1)
    tile_expert = expert_of(last_used * tm)
    row_in_expert = tile_row - jnp.sum(jnp.where(tile_expert[:, None] == jnp.arange(N_EXPERTS), offsets, 0), axis=1)
    own_count = jnp.sum(jnp.where(tile_expert[:, None] == jnp.arange(N_EXPERTS), counts, 0), axis=1)
    tile_valid = jnp.where(tile_row // tm < n_used, jnp.clip(own_count - row_in_expert, 0, tm), 0)
    tile_tables = jnp.concatenate([tile_expert, last_used]).astype(jnp.int32)

    idx = meta[:, META_IDX:META_IDX + 2].astype(jnp.int32)
    rank = meta[:, META_RANK:META_RANK + 2].astype(jnp.int32)
    dest = (jnp.sum(jnp.where(idx[:, :, None] == jnp.arange(N_EXPERTS), offsets, 0), axis=-1) + rank).T

    xs = scatter_rows(xn, dest, n_rows)
    ys = expert_ffn(xs, tile_tables, tile_valid.astype(jnp.int32), w_gate_up, w_down, tm=tm, tf=EXPERT_COLS)
    return moe_combine(h, gather_rows(ys, dest.reshape(-1)), meta)


def _ffn_kernel(x_ref, *refs, n_proj):
    proj_refs = refs[:2 * n_proj]
    g_ref, wg_ref, wu_ref, wd_ref, o_ref, xn_ref, acc_ref = refs[2 * n_proj:]
    j = pl.program_id(1)

    @pl.when(j == 0)
    def _():
        x = _plus_projections(x_ref[...], proj_refs)
        xn_ref[...] = _rms(x, g_ref[...]).astype(BF16)
        acc_ref[...] = x

    xn = xn_ref[...]
    gt = jnp.dot(xn, wg_ref[...], preferred_element_type=F32)
    up = jnp.dot(xn, wu_ref[...], preferred_element_type=F32)
    act = gt / (1.0 + jnp.exp(-gt)) * up
    acc_ref[...] += jnp.dot(act.astype(BF16), wd_ref[...], preferred_element_type=F32)

    @pl.when(j == pl.num_programs(1) - 1)
    def _():
        o_ref[...] = acc_ref[...]


def ffn_residual(x, proj, gain, w_gate_up, w_down, *, tm, tf):
    M, K = x.shape
    F = w_down.shape[0]
    nf = F // tf
    proj_specs, proj_args = _projection_specs(proj, tm)
    return pl.pallas_call(
        functools.partial(_ffn_kernel, n_proj=len(proj)),
        grid=(M // tm, nf),
        in_specs=[
            pl.BlockSpec((tm, K), lambda i, j: (i, 0)),
            *proj_specs,
            pl.BlockSpec((1, K), lambda i, j: (0, 0)),
            pl.BlockSpec((K, tf), lambda i, j: (0, j)),
            pl.BlockSpec((K, tf), lambda i, j: (0, nf + j)),
            pl.BlockSpec((tf, K), lambda i, j: (j, 0)),
        ],
        out_specs=pl.BlockSpec((tm, K), lambda i, j: (i, 0)),
        out_shape=jax.ShapeDtypeStruct((M, K), F32),
        scratch_shapes=[pltpu.VMEM((tm, K), BF16), pltpu.VMEM((tm, K), F32)],
        compiler_params=_params("parallel", "arbitrary"),
        name="ffn_residual",
    )(x, *proj_args, gain.reshape(1, K), w_gate_up, w_gate_up, w_down)


def _even_mix(h, B, S, norm1, w_in, f_bias, q_norm, k_norm, conv_w, a_log, dt_bias, o_norm, w_out):
    M, D = h.shape
    fw, gw = FOX_HEADS * HEAD_DIM, GDN_HEADS * HEAD_DIM
    o_ff = 3 * fw
    o_gq = o_ff + FOX_HEADS
    o_ga = o_gq + 3 * gw
    o_gb = o_ga + GDN_HEADS
    o_gg = o_gb + GDN_HEADS
    w_big = jnp.concatenate([w_in[:, :o_ff], w_in[:, o_gq:o_ga], w_in[:, o_gg:]], axis=1)
    w_small = jnp.concatenate([w_in[:, o_ff:o_gq], w_in[:, o_ga:o_gg],
                               jnp.zeros((D, LANES - FOX_HEADS - 2 * GDN_HEADS), F32)], axis=1)
    w_big, w_small = (t.astype(BF16) for t in lax.optimization_barrier((w_big, w_small)))
    z, zs = norm_matmul(h, norm1, w_big, w_small, tm=ROW_TILE, tn=IN_PROJ_COLS)
    par = jnp.zeros((8, LANES), F32)
    par = par.at[0, LANE_F:LANE_F + FOX_HEADS].set(f_bias).at[0, LANE_A:LANE_A + GDN_HEADS].set(dt_bias)
    par = par.at[1, LANE_A:LANE_A + GDN_HEADS].set(a_log)
    col, row = even_gates(zs.reshape(B, S, LANES), par)
    z = z.reshape(B, S, -1)
    fox = fox_attention(z, col, q_norm, k_norm)
    gdn = gdn_mixer(z, conv_w, col, row, o_norm)
    w_out = w_out.astype(BF16)
    return [(fox.reshape(M, fw), w_out[:fw]), (gdn.reshape(M, gw), w_out[fw:])]


def _odd_mix(h, B, S, norm1, w_qkv, q_norm, k_norm, w_out):
    M, D = h.shape
    z = norm_matmul(h, norm1, w_qkv.astype(BF16), tm=ROW_TILE, tn=QKV_COLS).reshape(B, S, -1)
    half = HEAD_DIM // 2
    inv = jnp.power(ROPE_THETA, -jnp.arange(half, dtype=F32) / half)
    ang = jnp.arange(S, dtype=F32)[:, None] * inv[None, :]
    cos, sin = jnp.cos(ang), jnp.sin(ang)
    cos_full = jnp.concatenate([cos, cos], axis=-1)
    sin_signed = jnp.concatenate([-sin, sin], axis=-1)
    att = moba_attention(z, cos_full, sin_signed, q_norm, k_norm)
    return [(att.reshape(M, -1), w_out.astype(BF16))]


def kernel(x, e_norm1, e_w_in, e_fox_f_bias, e_fox_q_norm, e_fox_k_norm, e_gdn_conv, e_gdn_a_log,
           e_gdn_dt_bias, e_gdn_o_norm, e_w_out, e_norm2, e_ffn_w_gate_up, e_ffn_w_down,
           o_norm1, o_w_qkv, o_q_norm, o_k_norm, o_w_out, o_norm2, o_router, o_exp_w_gate_up, o_exp_w_down):
    B, S, D = x.shape
    h = x.reshape(B * S, D)
    depth = e_norm1.shape[0] + o_norm1.shape[0]
    for layer in range(depth):
        i = layer // 2
        if layer % 2 == 0:
            mix = _even_mix(h, B, S, e_norm1[i], e_w_in[i], e_fox_f_bias[i], e_fox_q_norm[i], e_fox_k_norm[i],
                            e_gdn_conv[i], e_gdn_a_log[i], e_gdn_dt_bias[i], e_gdn_o_norm[i], e_w_out[i])
            h = ffn_residual(h, mix, e_norm2[i], e_ffn_w_gate_up[i].astype(BF16), e_ffn_w_down[i].astype(BF16),
                             tm=FFN_ROWS, tf=FFN_COLS)
        else:
            mix = _odd_mix(h, B, S, o_norm1[i], o_w_qkv[i], o_q_norm[i], o_k_norm[i], o_w_out[i])
            h = moe_residual(h, mix, o_norm2[i], o_router[i], o_exp_w_gate_up[i], o_exp_w_down[i])
    return h.reshape(B, S, D)
```

```python
import functools

import jax
import jax.numpy as jnp
from jax import lax
from jax.experimental import pallas as pl
from jax.experimental.pallas import tpu as pltpu
from jax.experimental.pallas import tpu_sc as plsc

F32 = jnp.float32
BF16 = jnp.bfloat16

HEAD_DIM = 128
FOX_HEADS = 4
GDN_HEADS = 4
CONV_WIDTH = 4
MOBA_BLOCK = 256
MOBA_TOPK = 3
N_EXPERTS = 8
ROPE_THETA = 10000.0
EPS = 1e-6

LANES = 128
GDN_CHUNK = 128
INV_BLOCK = 16
GDN_HEADS_PER_STEP = 4
GDN_CHUNKS_PER_STEP = 4
NEG = -(2.0 ** 100)
LOG2E = 1.4426950408889634
ATT_BLOCK = 256
ATT_HEADS = 4
KEY_BLOCKS_PER_STEP = 2
V_ROWS = HEAD_DIM + 16
PERM_TILE = 1024
ROUTER_CHAINS = 4
SC_ROWS = 64
ROW_TILE = 1024
MXU_COLS = 256
IN_PROJ_COLS = 7 * MXU_COLS
QKV_COLS = 6 * MXU_COLS
FFN_ROWS, FFN_COLS = 512, 1408
EXPERT_TILE = 1024
EXPERT_COLS = 896
EXPERT_ROW_PARTS = 4
VMEM_LIMIT_BYTES = 56 * 1024 * 1024

FOX_Q0, FOX_K0, FOX_V0 = 0, 4, 8
GDN_Q0, GDN_K0, GDN_V0, GDN_G0 = 12, 16, 20, 24
LANE_F, LANE_A, LANE_B = 0, 4, 8


def _params(*sem):
    return pltpu.CompilerParams(dimension_semantics=sem, vmem_limit_bytes=VMEM_LIMIT_BYTES)


def _rms(x, gain):
    return x * lax.rsqrt(jnp.mean(x * x, axis=-1, keepdims=True) + EPS) * gain


def _dot_nt(a, b, **kw):
    return lax.dot_general(a, b, (((1,), (1,)), ((), ())), preferred_element_type=F32, **kw)


def _pick_lane(x, lane_idx):
    lane = lax.broadcasted_iota(jnp.int32, x.shape, 1)
    return jnp.sum(jnp.where(lane == lane_idx, x, 0.0), axis=-1, keepdims=True)


def _pack_bf16_pairs(x):
    n = x.shape[1] // 2
    hi = pltpu.bitcast(x[:, :n].astype(BF16).astype(F32), jnp.uint32)
    lo = pltpu.bitcast(x[:, n:].astype(BF16).astype(F32), jnp.uint32)
    return pltpu.bitcast(hi | (lo >> 16), jnp.int32)


def _unpack_bf16_pairs(w):
    u = pltpu.bitcast(w, jnp.uint32)
    hi = pltpu.bitcast(u & jnp.uint32(0xFFFF0000), F32).astype(BF16)
    lo = pltpu.bitcast(u << 16, F32).astype(BF16)
    return jnp.concatenate([hi, lo], axis=1)


def _norm_mm_kernel(x_ref, g_ref, w_ref, *rest, has_aux):
    if has_aux:
        waux_ref, o_ref, oaux_ref, xn_ref = rest
    else:
        o_ref, xn_ref = rest

    @pl.when(pl.program_id(1) == 0)
    def _():
        xn = _rms(x_ref[...], g_ref[...]).astype(BF16)
        xn_ref[...] = xn
        if has_aux:
            oaux_ref[...] = jnp.dot(xn, waux_ref[...], preferred_element_type=F32)

    o_ref[...] = jnp.dot(xn_ref[...], w_ref[...].astype(BF16), preferred_element_type=F32).astype(o_ref.dtype)


def norm_matmul(x, gain, w, w_aux=None, *, tm, tn, out_dtype=BF16):
    M, K = x.shape
    N = w.shape[1]
    has_aux = w_aux is not None
    in_specs = [
        pl.BlockSpec((tm, K), lambda i, j: (i, 0)),
        pl.BlockSpec((1, K), lambda i, j: (0, 0)),
        pl.BlockSpec((K, tn), lambda i, j: (0, j)),
    ]
    out_shape = [jax.ShapeDtypeStruct((M, N), out_dtype)]
    out_specs = [pl.BlockSpec((tm, tn), lambda i, j: (i, j))]
    args = [x, gain.reshape(1, K), w]
    if has_aux:
        in_specs.append(pl.BlockSpec((K, LANES), lambda i, j: (0, 0)))
        out_shape.append(jax.ShapeDtypeStruct((M, LANES), F32))
        out_specs.append(pl.BlockSpec((tm, LANES), lambda i, j: (i, 0)))
        args.append(w_aux)
    res = pl.pallas_call(
        functools.partial(_norm_mm_kernel, has_aux=has_aux),
        grid=(M // tm, N // tn),
        in_specs=in_specs,
        out_specs=out_specs,
        out_shape=out_shape,
        scratch_shapes=[pltpu.VMEM((tm, K), BF16)],
        compiler_params=_params("parallel", "arbitrary"),
        name="norm_matmul",
    )(*args)
    return res if has_aux else res[0]


def _plus_projections(x, proj_refs, rows=slice(None)):
    for a_ref, w_ref in zip(proj_refs[0::2], proj_refs[1::2]):
        x = x + jnp.dot(a_ref[rows, :], w_ref[...].astype(BF16), preferred_element_type=F32)
    return x


def _projection_specs(pairs, tm):
    specs, args = [], []
    for a, w in pairs:
        specs.append(pl.BlockSpec((tm, a.shape[1]), lambda i, *_: (i, 0)))
        specs.append(pl.BlockSpec(w.shape, lambda i, *_: (0, 0)))
        args += [a, w]
    return specs, args


def _gate_kernel(zs_ref, par_ref, col_ref, row_ref, *, S, B):
    C = GDN_CHUNK
    bias = par_ref[0:1, :]
    neg_a = -jnp.exp(par_ref[1:2, :])
    r = lax.broadcasted_iota(jnp.int32, (C, C), 0)
    c = lax.broadcasted_iota(jnp.int32, (C, C), 1)
    tril = (r >= c).astype(F32)
    lane = lax.broadcasted_iota(jnp.int32, (C, LANES), 1)

    def body(n, carry):
        sl = pl.ds(pl.multiple_of(n * C, C), C)
        us, betas = [], []
        for bi in range(B):
            z = zs_ref[bi, sl, :]
            t = z + bias
            soft = jnp.log(1.0 + jnp.exp(-jnp.abs(t)))
            log_f = jnp.minimum(t, 0.0) - soft
            g = neg_a * (jnp.maximum(t, 0.0) + soft)
            betas.append(1.0 / (1.0 + jnp.exp(-z)))
            us.append(jnp.where(lane < LANE_A, log_f, jnp.where(lane < LANE_B, g, 0.0)))
        sums = jnp.dot(tril, jnp.concatenate(us, axis=1), preferred_element_type=F32, precision=lax.Precision.HIGHEST)
        last = []
        for bi in range(B):
            cs = sums[:, bi * LANES:(bi + 1) * LANES] + jnp.where(lane < LANE_A, carry[bi], 0.0)
            out = jnp.where(lane < LANE_B, cs, betas[bi])
            col_ref[bi, sl, :] = out
            out_t = out.T
            for hd in range(GDN_HEADS):
                row_ref[bi, hd, :, sl] = out_t[LANE_A + hd:LANE_A + hd + 1, :]
            last.append(cs[C - 1:C, :])
        return tuple(last)

    lax.fori_loop(0, S // C, body, tuple(jnp.zeros((1, LANES), F32) for _ in range(B)))


def even_gates(zs, par):
    B, S, _ = zs.shape
    return pl.pallas_call(
        functools.partial(_gate_kernel, S=S, B=B),
        grid=(1,),
        in_specs=[
            pl.BlockSpec((B, S, LANES), lambda i: (0, 0, 0)),
            pl.BlockSpec((8, LANES), lambda i: (0, 0)),
        ],
        out_specs=[
            pl.BlockSpec((B, S, LANES), lambda i: (0, 0, 0)),
            pl.BlockSpec((B, GDN_HEADS, 1, S), lambda i: (0, 0, 0, 0)),
        ],
        out_shape=[jax.ShapeDtypeStruct((B, S, LANES), F32), jax.ShapeDtypeStruct((B, GDN_HEADS, 1, S), F32)],
        compiler_params=_params("arbitrary"),
        name="even_gates",
    )(zs, par)


def _split3(x):
    hi = x.astype(BF16).astype(F32)
    mid = (x - hi).astype(BF16).astype(F32)
    lo = (x - hi - mid).astype(BF16).astype(F32)
    return hi, mid, lo


def _interleave(gens):
    out = [None] * len(gens)
    live = list(range(len(gens)))
    while live:
        for n in list(live):
            try:
                next(gens[n])
            except StopIteration as stop:
                out[n] = stop.value
                live.remove(n)
    return out


def _attend(state, c, qa, ka_ref, vt_ref, key0, nkeys, keep=None):
    ks = pl.ds(pl.multiple_of(key0, ATT_BLOCK), nkeys)
    st = _dot_nt(ka_ref[ks, :], qa)
    yield
    if keep is not None:
        st = jnp.where(keep, st, NEG)
    m_new = jnp.max(st, axis=0, keepdims=True)
    if state[c] is not None:
        m_old, acc_old = state[c]
        m_new = jnp.maximum(m_old, m_new)
    p = jnp.exp2(st - m_new).astype(BF16)
    pv = jnp.dot(vt_ref[:, ks], p, preferred_element_type=F32)
    state[c] = (m_new, pv if state[c] is None else acc_old * jnp.exp2(m_old - m_new) + pv)


def _attend_tile_pair(i, prep, ka_ref, vt_ref, m_ref, acc_ref, o_ref):
    t = ATT_BLOCK
    chains = [(hd, c) for c in range(2) for hd in range(ATT_HEADS)]
    num = lambda hd, c: 2 * hd + c
    qa = _interleave(prep)

    def step(state, hd, c, which, **kw):
        return _attend(state, num(hd, c), qa[num(hd, c)][which], ka_ref.at[hd], vt_ref.at[hd], **kw)

    def save(state):
        m_ref[...] = jnp.stack([state[n][0] for n in range(len(chains))])
        acc_ref[...] = jnp.stack([state[n][1] for n in range(len(chains))])

    def past_blocks(key0, n_steps):
        state = {n: (m_ref[n], acc_ref[n]) for n in range(len(chains))}
        _interleave([step(state, hd, c, 1, key0=key0 + s * 2 * t, nkeys=2 * t)
                     for s in range(n_steps) for hd, c in chains])
        save(state)

    state = {n: None for n in range(len(chains))}
    keep = _causal_keep()
    _interleave([step(state, hd, c, 0, key0=(2 * i + c) * t, nkeys=t, keep=keep) for hd, c in chains]
                + [step(state, hd, 1, 1, key0=(2 * i) * t, nkeys=t) for hd in range(ATT_HEADS)])
    save(state)

    def four_blocks(g, _):
        past_blocks(g * (4 * t), 2)
        return 0

    lax.fori_loop(0, i // 2, four_blocks, 0)

    @pl.when((i & 1) != 0)
    def _():
        past_blocks((i // 2) * (4 * t), 1)

    for hd, c in chains:
        acc = acc_ref[num(hd, c)]
        out_t = acc[:HEAD_DIM, :] * (1.0 / acc[HEAD_DIM:HEAD_DIM + 1, :])
        o_ref[c * t:(c + 1) * t, hd * HEAD_DIM:(hd + 1) * HEAD_DIM] = out_t.T.astype(o_ref.dtype)


def _transposed(v):
    r = lax.broadcasted_iota(jnp.int32, (HEAD_DIM, HEAD_DIM), 0)
    c = lax.broadcasted_iota(jnp.int32, (HEAD_DIM, HEAD_DIM), 1)
    v_t = _dot_nt(jnp.where(r == c, 1.0, 0.0).astype(BF16), v).astype(BF16)
    return jnp.concatenate([v_t, jnp.ones((V_ROWS - HEAD_DIM, ATT_BLOCK), BF16)], axis=0)


def _causal_keep():
    key = lax.broadcasted_iota(jnp.int32, (ATT_BLOCK, ATT_BLOCK), 0)
    qry = lax.broadcasted_iota(jnp.int32, (ATT_BLOCK, ATT_BLOCK), 1)
    return key <= qry


_ATT_SCRATCH = lambda S: [pltpu.VMEM((ATT_HEADS, S, 2 * HEAD_DIM), BF16), pltpu.VMEM((ATT_HEADS, V_ROWS, S), BF16),
                          pltpu.VMEM((2 * ATT_HEADS, 1, ATT_BLOCK), F32),
                          pltpu.VMEM((2 * ATT_HEADS, V_ROWS, ATT_BLOCK), F32)]


def _fox_kernel(q_ref, k_ref, v_ref, col_ref, qg_ref, kg_ref, o_ref, ka_ref, vt_ref, m_ref, acc_ref, *, S):
    h0 = pl.program_id(1) * ATT_HEADS
    i = pl.program_id(2)
    t = ATT_BLOCK
    D = HEAD_DIM
    lane = lax.broadcasted_iota(jnp.int32, (t, LANES), 1)

    @pl.when(i == 0)
    def _():
        def prep_keys(g, _):
            todo = []
            for u in range(KEY_BLOCKS_PER_STEP):
                sl = pl.ds(pl.multiple_of((g * KEY_BLOCKS_PER_STEP + u) * t, t), t)
                for hd in range(ATT_HEADS):
                    cols = slice(hd * D, (hd + 1) * D)
                    kn = _rms(k_ref[sl, cols].astype(F32), kg_ref[...]).astype(BF16)
                    hi, mid, lo = _split3(-LOG2E * _pick_lane(col_ref[sl, :], LANE_F + h0 + hd))
                    aug = jnp.where(lane < 3, 1.0, jnp.where(lane == 3, hi, jnp.where(lane == 4, mid,
                                                                                    jnp.where(lane == 5, lo, 0.0))))
                    todo.append((hd, sl, jnp.concatenate([kn, aug.astype(BF16)], axis=1), _transposed(v_ref[sl, cols])))
            for hd, sl, ka, vt in todo:
                ka_ref[hd, sl, :] = ka
                vt_ref[hd, :, sl] = vt
            return 0

        lax.fori_loop(0, S // (t * KEY_BLOCKS_PER_STEP), prep_keys, 0)

    def prep(hd, c):
        q = q_ref[c * t:(c + 1) * t, hd * D:(hd + 1) * D].astype(F32)
        qn = (_rms(q, qg_ref[...]) * (LOG2E * D ** -0.5)).astype(BF16)
        yield
        qsl = pl.ds(pl.multiple_of((2 * i + c) * t, t), t)
        hi, mid, lo = _split3(LOG2E * _pick_lane(col_ref[qsl, :], LANE_F + h0 + hd))
        aug = jnp.where(lane == 0, hi,
                        jnp.where(lane == 1, mid, jnp.where(lane == 2, lo, jnp.where(lane < 6, 1.0, 0.0))))
        qa = jnp.concatenate([qn, aug.astype(BF16)], axis=-1)
        return qa, qa

    _attend_tile_pair(i, [prep(hd, c) for hd in range(ATT_HEADS) for c in range(2)],
                      ka_ref, vt_ref, m_ref, acc_ref, o_ref)


def fox_attention(z, col, q_gain, k_gain):
    B, S, _ = z.shape
    t = 2 * ATT_BLOCK
    w = ATT_HEADS * HEAD_DIM
    return pl.pallas_call(
        functools.partial(_fox_kernel, S=S),
        grid=(B, FOX_HEADS // ATT_HEADS, S // t),
        in_specs=[
            pl.BlockSpec((None, t, w), lambda b, h, i: (b, i, FOX_Q0 // ATT_HEADS + h)),
            pl.BlockSpec((None, S, w), lambda b, h, i: (b, 0, FOX_K0 // ATT_HEADS + h)),
            pl.BlockSpec((None, S, w), lambda b, h, i: (b, 0, FOX_V0 // ATT_HEADS + h)),
            pl.BlockSpec((None, S, LANES), lambda b, h, i: (b, 0, 0)),
            pl.BlockSpec((1, HEAD_DIM), lambda b, h, i: (0, 0)),
            pl.BlockSpec((1, HEAD_DIM), lambda b, h, i: (0, 0)),
        ],
        out_specs=pl.BlockSpec((None, t, w), lambda b, h, i: (b, i, h)),
        out_shape=jax.ShapeDtypeStruct((B, S, FOX_HEADS * HEAD_DIM), BF16),
        scratch_shapes=_ATT_SCRATCH(S),
        compiler_params=_params("parallel", "parallel", "arbitrary"),
        name="fox_attention",
    )(z, z, z, col, q_gain.reshape(1, -1), k_gain.reshape(1, -1))


def _unit_lower_inverse(m):
    n = m.shape[0]
    r = lax.broadcasted_iota(jnp.int32, (n, n), 0)
    c = lax.broadcasted_iota(jnp.int32, (n, n), 1)
    eye = (r == c).astype(F32)

    def same_block(b):
        return (r // b) == (c // b)

    p = jnp.where(same_block(INV_BLOCK), m, 0.0)
    inv = eye - p
    k = 2
    while k < INV_BLOCK:
        pb = p.astype(BF16)
        p = jnp.dot(pb, pb, preferred_element_type=F32)
        yield
        inv = jnp.dot(inv.astype(BF16), (eye + p).astype(BF16), preferred_element_type=F32)
        yield
        k *= 2
    b = INV_BLOCK
    while b < n:
        off = jnp.where(same_block(2 * b), jnp.where(same_block(b), 0.0, m), 0.0).astype(BF16)
        ib = inv.astype(BF16)
        left = jnp.dot(ib, off, preferred_element_type=F32).astype(BF16)
        yield
        inv = inv - jnp.dot(left, ib, preferred_element_type=F32)
        yield
        b *= 2
    return inv


def _gdn_kernel(q_ref, k_ref, v_ref, gg_ref, wq_ref, wk_ref, wv_ref, col_ref, row_ref, on_ref,
                o_ref, qs_ref, ks_ref, vs_ref, *, S, rows, hb):
    h0 = pl.program_id(1) * hb
    C = GDN_CHUNK
    D = HEAD_DIM

    streams = [(x_ref, w_ref, dst_ref, hh, mode) for hh in range(hb)
               for x_ref, w_ref, dst_ref, mode in ((q_ref, wq_ref, qs_ref, "q"), (k_ref, wk_ref, ks_ref, "k"),
                                                   (v_ref, wv_ref, vs_ref, "v"))]
    halo = 16

    def conv_step(base, windows):
        outs = []
        for (x_ref, w_ref, dst_ref, hh, mode), win in zip(streams, windows):
            w = w_ref[:, hh * D:(hh + 1) * D]
            y = jnp.zeros((rows, D), F32)
            for tap in range(CONV_WIDTH):
                lead = halo - (CONV_WIDTH - 1) + tap
                y = y + w[tap:tap + 1, :] * pltpu.roll(win, rows + halo - lead, 0)[0:rows, :]
            y = y / (1.0 + jnp.exp(-y))
            if mode != "v":
                y = y * lax.rsqrt(jnp.sum(y * y, axis=-1, keepdims=True) + EPS)
            if mode == "q":
                y = y * D ** -0.5
            outs.append(y)
        for (x_ref, w_ref, dst_ref, hh, mode), y in zip(streams, outs):
            dst_ref[hh, pl.ds(base, rows), :] = y.astype(dst_ref.dtype)

    conv_step(0, [jnp.concatenate([jnp.zeros((halo, D), F32), x_ref[0:rows, hh * D:(hh + 1) * D].astype(F32)], axis=0)
                  for x_ref, _, _, hh, _ in streams])

    def conv_rest(n, _):
        base = pl.multiple_of(n * rows, rows)
        conv_step(base, [x_ref[pl.ds(pl.multiple_of(base - halo, halo), rows + halo), hh * D:(hh + 1) * D].astype(F32)
                         for x_ref, _, _, hh, _ in streams])
        return 0

    lax.fori_loop(1, S // rows, conv_rest, 0)

    r = lax.broadcasted_iota(jnp.int32, (C, C), 0)
    c = lax.broadcasted_iota(jnp.int32, (C, C), 1)
    incl = r >= c
    strict = r > c

    def chunk_local(hh, sl, tab):
        q = qs_ref[hh, sl, :].astype(F32)
        k = ks_ref[hh, sl, :].astype(F32)
        v = vs_ref[hh, sl, :].astype(F32)
        gcol = _pick_lane(tab, LANE_A + h0 + hh)
        beta = _pick_lane(tab, LANE_B + h0 + hh)
        grow = row_ref[hh, :, sl]
        glast = gcol[C - 1:C, :]
        decay = jnp.where(incl, jnp.exp(jnp.where(incl, gcol - grow, 0.0)), 0.0)
        eg = jnp.exp(gcol)
        kb = k * beta
        kbf = k.astype(BF16)
        m = jnp.where(strict, _dot_nt(kb.astype(BF16), kbf) * decay, 0.0)
        attn = (_dot_nt(q.astype(BF16), kbf) * decay).astype(BF16)
        yield
        tinv = (yield from _unit_lower_inverse(m)).astype(BF16)
        rhs = jnp.concatenate([v * beta, kb * eg], axis=-1).astype(BF16)
        sol = jnp.dot(tinv, rhs, preferred_element_type=F32)
        yield
        gate = gg_ref[sl, hh * D:(hh + 1) * D].astype(F32)
        return dict(u=sol[:, :D], w=sol[:, D:].astype(BF16), attn=attn, qg=(q * eg).astype(BF16),
                    kg_t=(k * jnp.exp(glast - gcol)).T.astype(BF16), keep=jnp.exp(glast),
                    gate=gate / (1.0 + jnp.exp(-gate)))

    def chunk_state(parts, state):
        outs = []
        for c in parts:
            sb = state.astype(BF16)
            v_new = c["u"] - jnp.dot(c["w"], sb, preferred_element_type=F32)
            o_state = jnp.dot(c["qg"], sb, preferred_element_type=F32)
            yield
            vb = v_new.astype(BF16)
            o = o_state + jnp.dot(c["attn"], vb, preferred_element_type=F32)
            state = state * c["keep"] + jnp.dot(c["kg_t"], vb, preferred_element_type=F32)
            yield
            outs.append((_rms(o, on_ref[...]) * c["gate"]).astype(o_ref.dtype))
        return state, outs

    def chunk_group(n, states):
        sls = [pl.ds(pl.multiple_of((n * GDN_CHUNKS_PER_STEP + g) * C, C), C) for g in range(GDN_CHUNKS_PER_STEP)]
        tabs = [col_ref[sl, :] for sl in sls]
        parts = _interleave([chunk_local(hh, sls[g], tabs[g]) for g in range(GDN_CHUNKS_PER_STEP) for hh in range(hb)])
        res = _interleave([chunk_state([parts[g * hb + hh] for g in range(GDN_CHUNKS_PER_STEP)], states[hh])
                           for hh in range(hb)])
        for g in range(GDN_CHUNKS_PER_STEP):
            o_ref[sls[g], :] = jnp.concatenate([outs[g] for _, outs in res], axis=-1)
        return tuple(s for s, _ in res)

    lax.fori_loop(0, S // (C * GDN_CHUNKS_PER_STEP), chunk_group, tuple(jnp.zeros((D, D), F32) for _ in range(hb)))


def gdn_mixer(z, conv_w, col, row, o_gain):
    B, S, _ = z.shape
    D = HEAD_DIM
    hb = GDN_HEADS_PER_STEP
    rows = 256
    seq = lambda off: pl.BlockSpec((None, S, hb * D), lambda b, h: (b, 0, off // hb + h),
                                   pipeline_mode=pl.Buffered(1))
    cw = lambda off: pl.BlockSpec((CONV_WIDTH, hb * D), lambda b, h: (0, off // hb + h))
    return pl.pallas_call(
        functools.partial(_gdn_kernel, S=S, rows=rows, hb=hb),
        grid=(B, GDN_HEADS // hb),
        in_specs=[
            seq(GDN_Q0), seq(GDN_K0), seq(GDN_V0), seq(GDN_G0),
            cw(0), cw(GDN_HEADS), cw(2 * GDN_HEADS),
            pl.BlockSpec((None, S, LANES), lambda b, h: (b, 0, 0)),
            pl.BlockSpec((None, hb, 1, S), lambda b, h: (b, h, 0, 0)),
            pl.BlockSpec((1, D), lambda b, h: (0, 0)),
        ],
        out_specs=pl.BlockSpec((None, S, hb * D), lambda b, h: (b, 0, h)),
        out_shape=jax.ShapeDtypeStruct((B, S, GDN_HEADS * D), BF16),
        scratch_shapes=[pltpu.VMEM((hb, S, D), BF16)] * 3,
        compiler_params=_params("parallel", "parallel"),
        name="gdn_mixer",
    )(z, z, z, z, conv_w, conv_w, conv_w, col, row, o_gain.reshape(1, D))


def _rope(x, cos, sin_signed):
    return x * cos + pltpu.roll(x, HEAD_DIM // 2, 1) * sin_signed


def _moba_kernel(q_ref, k_ref, v_ref, cos_ref, sin_ref, qg_ref, kg_ref, o_ref,
                 ka_ref, vt_ref, m_ref, acc_ref, kmean_ref, *, S):
    i = pl.program_id(2)
    t = ATT_BLOCK
    D = HEAD_DIM
    lane = lax.broadcasted_iota(jnp.int32, (t, LANES), 1)

    @pl.when(i == 0)
    def _():
        kmean_ref[...] = jnp.zeros(kmean_ref.shape, F32)

        def prep_keys(g, _):
            todo = []
            for u in range(KEY_BLOCKS_PER_STEP):
                n = g * KEY_BLOCKS_PER_STEP + u
                sl = pl.ds(pl.multiple_of(n * t, t), t)
                onehot = jnp.where(lane == n, 1.0, 0.0).astype(BF16)
                for hd in range(ATT_HEADS):
                    cols = slice(hd * D, (hd + 1) * D)
                    k = _rope(_rms(k_ref[sl, cols].astype(F32), kg_ref[...]), cos_ref[sl, :], sin_ref[sl, :])
                    todo.append((hd, n, sl, jnp.concatenate([k.astype(BF16), onehot], axis=1),
                                 jnp.mean(k, axis=0, keepdims=True), _transposed(v_ref[sl, cols])))
            for hd, n, sl, ka, kmean, vt in todo:
                ka_ref[hd, sl, :] = ka
                kmean_ref[hd, pl.ds(n, 1), :] = kmean
                vt_ref[hd, :, sl] = vt
            return 0

        lax.fori_loop(0, S // (t * KEY_BLOCKS_PER_STEP), prep_keys, 0)

    nb = -(-(S // t) // 8) * 8

    def prep(hd, c):
        cur = 2 * i + c
        qsl = pl.ds(pl.multiple_of(cur * t, t), t)
        q = _rope(_rms(q_ref[c * t:(c + 1) * t, hd * D:(hd + 1) * D].astype(F32), qg_ref[...]),
                  cos_ref[qsl, :], sin_ref[qsl, :])
        yield
        kmean = kmean_ref[hd]
        km_hi = kmean.astype(BF16)
        km_split = jnp.concatenate([km_hi, (kmean - km_hi.astype(F32)).astype(BF16)], axis=0)
        q_hi = q.astype(BF16)
        q_lo = (q - q_hi.astype(F32)).astype(BF16)
        part = _dot_nt(km_split, q_hi)
        gate = (part[:LANES] + part[LANES:] + _dot_nt(km_split[:LANES, :], q_lo))[0:nb]
        yield
        blk = lax.broadcasted_iota(jnp.int32, (nb, t), 0)
        blk_f = blk.astype(F32)
        gate = jnp.where(blk < cur, gate, -jnp.inf)
        sel = jnp.full((nb, t), NEG, F32)
        for _ in range(MOBA_TOPK):
            top = jnp.max(gate, axis=0, keepdims=True)
            first = jnp.min(jnp.where(gate == top, blk_f, float(LANES)), axis=0, keepdims=True)
            pick = blk_f == first
            sel = jnp.where(pick & (first < cur.astype(F32)), 0.0, sel)
            gate = jnp.where(pick, -jnp.inf, gate)
        sel_bias = jnp.concatenate([sel, jnp.full((LANES - nb, t), NEG, F32)], axis=0).T
        qs = (q * (LOG2E * D ** -0.5)).astype(BF16)
        return (jnp.concatenate([qs, jnp.zeros((t, LANES), BF16)], axis=-1),
                jnp.concatenate([qs, sel_bias.astype(BF16)], axis=-1))

    _attend_tile_pair(i, [prep(hd, c) for hd in range(ATT_HEADS) for c in range(2)],
                      ka_ref, vt_ref, m_ref, acc_ref, o_ref)


def moba_attention(z, cos, sin_signed, q_gain, k_gain):
    B, S, W = z.shape
    H = W // (3 * HEAD_DIM)
    assert ATT_BLOCK == MOBA_BLOCK
    t = 2 * ATT_BLOCK
    w = ATT_HEADS * HEAD_DIM
    hp = H // ATT_HEADS
    return pl.pallas_call(
        functools.partial(_moba_kernel, S=S),
        grid=(B, hp, S // t),
        in_specs=[
            pl.BlockSpec((None, t, w), lambda b, h, i: (b, i, h)),
            pl.BlockSpec((None, S, w), lambda b, h, i: (b, 0, hp + h)),
            pl.BlockSpec((None, S, w), lambda b, h, i: (b, 0, 2 * hp + h)),
            pl.BlockSpec((S, HEAD_DIM), lambda b, h, i: (0, 0)),
            pl.BlockSpec((S, HEAD_DIM), lambda b, h, i: (0, 0)),
            pl.BlockSpec((1, HEAD_DIM), lambda b, h, i: (0, 0)),
            pl.BlockSpec((1, HEAD_DIM), lambda b, h, i: (0, 0)),
        ],
        out_specs=pl.BlockSpec((None, t, w), lambda b, h, i: (b, i, h)),
        out_shape=jax.ShapeDtypeStruct((B, S, H * HEAD_DIM), BF16),
        scratch_shapes=_ATT_SCRATCH(S) + [pltpu.VMEM((ATT_HEADS, LANES, HEAD_DIM), F32)],
        compiler_params=_params("parallel", "parallel", "arbitrary"),
        name="moba_attention",
    )(z, z, z, cos, sin_signed, q_gain.reshape(1, -1), k_gain.reshape(1, -1))


META_IDX, META_RANK, META_GATE = 0, 2, 4


def _router_kernel(x_ref, *refs, n_proj):
    proj_refs = refs[:2 * n_proj]
    g_ref, w_ref, y_ref, xn_ref, meta_ref, cnt_ref, carry_ref = refs[2 * n_proj:]
    i = pl.program_id(0)
    tm = x_ref.shape[0]

    @pl.when(i == 0)
    def _():
        carry_ref[...] = jnp.zeros(carry_ref.shape, F32)

    w = w_ref[...]
    w_hi = w.astype(BF16)
    w_lo = (w - w_hi.astype(F32)).astype(BF16)
    part = tm // ROUTER_CHAINS
    lane = lax.broadcasted_iota(jnp.int32, (part, LANES), 1)
    lane_f = lane.astype(F32)

    def route(n):
        rows = slice(n * part, (n + 1) * part)
        y = _plus_projections(x_ref[rows, :], proj_refs, rows)
        y_ref[rows, :] = y
        yield
        xn = _rms(y, g_ref[...])
        xn_ref[rows, :] = _pack_bf16_pairs(xn)
        x_hi = xn.astype(BF16)
        x_lo = (xn - x_hi.astype(F32)).astype(BF16)
        logits = (jnp.dot(x_hi, w_hi, preferred_element_type=F32) + jnp.dot(x_hi, w_lo, preferred_element_type=F32)
                  + jnp.dot(x_lo, w_hi, preferred_element_type=F32))
        yield
        logits = jnp.where(lane < N_EXPERTS, logits, -jnp.inf)
        top1 = jnp.max(logits, axis=-1, keepdims=True)
        yield
        idx1 = jnp.min(jnp.where(logits == top1, lane_f, float(LANES)), axis=-1, keepdims=True)
        yield
        rest = jnp.where(lane_f == idx1, -jnp.inf, logits)
        top2 = jnp.max(rest, axis=-1, keepdims=True)
        yield
        idx2 = jnp.min(jnp.where(rest == top2, lane_f, float(LANES)), axis=-1, keepdims=True)
        e2 = jnp.exp(top2 - top1)
        denom = 1.0 + e2
        return idx1, idx2, 1.0 / denom, e2 / denom, jnp.where((lane_f == idx1) | (lane_f == idx2), 1.0, 0.0)

    picks = _interleave([route(n) for n in range(ROUTER_CHAINS)])
    chosen = jnp.concatenate([p[4] for p in picks], axis=0)
    r = lax.broadcasted_iota(jnp.int32, (tm, tm), 0)
    c = lax.broadcasted_iota(jnp.int32, (tm, tm), 1)
    ahead = jnp.dot(jnp.where(r > c, 1.0, 0.0).astype(BF16), chosen.astype(BF16), preferred_element_type=F32)
    carry = carry_ref[...]
    for n, (idx1, idx2, g1, g2, _) in enumerate(picks):
        rows = slice(n * part, (n + 1) * part)
        rank = ahead[rows, :] + carry
        rank1 = jnp.sum(jnp.where(lane_f == idx1, rank, 0.0), axis=-1, keepdims=True)
        rank2 = jnp.sum(jnp.where(lane_f == idx2, rank, 0.0), axis=-1, keepdims=True)
        meta = jnp.zeros((part, LANES), F32)
        for k, v in enumerate((idx1, idx2, rank1, rank2, g1, g2)):
            meta = jnp.where(lane == k, v, meta)
        meta_ref[rows, :] = meta
    carry = carry + jnp.sum(chosen, axis=0, keepdims=True)
    carry_ref[...] = carry
    cnt_ref[...] = jnp.broadcast_to(carry, cnt_ref.shape)


def moe_router(x, proj, gain, w_router, *, tm):
    M, K = x.shape
    w = jnp.zeros((K, LANES), F32).at[:, :N_EXPERTS].set(w_router)
    proj_specs, proj_args = _projection_specs(proj, tm)
    return pl.pallas_call(
        functools.partial(_router_kernel, n_proj=len(proj)),
        grid=(M // tm,),
        in_specs=[
            pl.BlockSpec((tm, K), lambda i: (i, 0)),
            *proj_specs,
            pl.BlockSpec((1, K), lambda i: (0, 0)),
            pl.BlockSpec((K, LANES), lambda i: (0, 0)),
        ],
        out_specs=[
            pl.BlockSpec((tm, K), lambda i: (i, 0)),
            pl.BlockSpec((tm, K // 2), lambda i: (i, 0)),
            pl.BlockSpec((tm, LANES), lambda i: (i, 0)),
            pl.BlockSpec((8, LANES), lambda i: (0, 0)),
        ],
        out_shape=[
            jax.ShapeDtypeStruct((M, K), F32),
            jax.ShapeDtypeStruct((M, K // 2), jnp.int32),
            jax.ShapeDtypeStruct((M, LANES), F32),
            jax.ShapeDtypeStruct((8, LANES), F32),
        ],
        scratch_shapes=[pltpu.VMEM((1, LANES), F32)],
        compiler_params=_params("arbitrary"),
        name="moe_router",
    )(x, *proj_args, gain.reshape(1, K), w)


def _sc_workers():
    info = plsc.get_sparse_core_info()
    return info.num_cores, info.num_cores * info.num_subcores


def scatter_rows(rows, dest, n_out):
    M, W = rows.shape
    nc, nw = _sc_workers()
    assert M % (nw * SC_ROWS) == 0
    per_w = M // nw
    mesh = plsc.VectorSubcoreMesh(core_axis_name="c", subcore_axis_name="s")

    @functools.partial(
        pl.kernel, mesh=mesh, out_type=jax.ShapeDtypeStruct((n_out, W), rows.dtype),
        scratch_types=[pltpu.VMEM((SC_ROWS,), jnp.int32), pltpu.VMEM((SC_ROWS, W), rows.dtype),
                       pltpu.SemaphoreType.DMA])
    def kern(rows_hbm, dest_hbm, out_hbm, idx_v, rows_v, sem):
        wid = lax.axis_index("s") * nc + lax.axis_index("c")

        @pl.loop(0, per_w // SC_ROWS)
        def _(g):
            base = wid * per_w + g * SC_ROWS
            pltpu.sync_copy(rows_hbm.at[pl.ds(base, SC_ROWS)], rows_v)
            for k in range(2):
                pltpu.sync_copy(dest_hbm.at[k, pl.ds(base, SC_ROWS)], idx_v)
                pltpu.async_copy(rows_v, out_hbm.at[idx_v], sem).wait()

    return kern(rows, dest)


def gather_rows(table, idx):
    N = idx.shape[0]
    W = table.shape[1]
    nc, nw = _sc_workers()
    assert N % (nw * SC_ROWS) == 0
    per_w = N // nw
    mesh = plsc.VectorSubcoreMesh(core_axis_name="c", subcore_axis_name="s")

    @functools.partial(
        pl.kernel, mesh=mesh, out_type=jax.ShapeDtypeStruct((N, W), table.dtype),
        scratch_types=[pltpu.VMEM((SC_ROWS,), jnp.int32), pltpu.VMEM((SC_ROWS, W), table.dtype),
                       pltpu.SemaphoreType.DMA])
    def kern(table_hbm, idx_hbm, out_hbm, idx_v, rows_v, sem):
        wid = lax.axis_index("s") * nc + lax.axis_index("c")

        @pl.loop(0, per_w // SC_ROWS)
        def _(g):
            base = wid * per_w + g * SC_ROWS
            pltpu.sync_copy(idx_hbm.at[pl.ds(base, SC_ROWS)], idx_v)
            pltpu.async_copy(table_hbm.at[idx_v], rows_v, sem).wait()
            pltpu.sync_copy(rows_v, out_hbm.at[pl.ds(base, SC_ROWS)])

    return kern(table, idx)


def _combine_kernel(h_ref, y1_ref, y2_ref, meta_ref, o_ref):
    g1 = meta_ref[:, META_GATE:META_GATE + 1]
    g2 = meta_ref[:, META_GATE + 1:META_GATE + 2]
    o_ref[...] = (h_ref[...] + g1 * _unpack_bf16_pairs(y1_ref[...]).astype(F32)
                  + g2 * _unpack_bf16_pairs(y2_ref[...]).astype(F32))


def moe_combine(h, y_pairs, meta):
    M, K = h.shape
    p = PERM_TILE
    nt = M // p
    return pl.pallas_call(
        _combine_kernel,
        grid=(nt,),
        in_specs=[
            pl.BlockSpec((p, K), lambda i: (i, 0)),
            pl.BlockSpec((p, K // 2), lambda i: (i, 0)),
            pl.BlockSpec((p, K // 2), lambda i: (nt + i, 0)),
            pl.BlockSpec((p, LANES), lambda i: (i, 0)),
        ],
        out_specs=pl.BlockSpec((p, K), lambda i: (i, 0)),
        out_shape=jax.ShapeDtypeStruct((M, K), F32),
        compiler_params=_params("parallel"),
        name="moe_combine",
    )(h, y_pairs, y_pairs, meta)


def _expert_ffn_kernel(te_ref, tv_ref, x_ref, wg_ref, wu_ref, wd_ref, o_ref, acc_ref, *, nf):
    del te_ref
    i = pl.program_id(0)
    j = pl.program_id(1)
    tm = x_ref.shape[0]
    valid = tv_ref[i]

    def swiglu_rows(rows):
        x = _unpack_bf16_pairs(x_ref[0:rows, :])
        gt = jnp.dot(x, wg_ref[...].astype(BF16), preferred_element_type=F32)
        up = jnp.dot(x, wu_ref[...].astype(BF16), preferred_element_type=F32)
        act = (gt / (1.0 + jnp.exp(-gt)) * up).astype(BF16)
        part = jnp.dot(act, wd_ref[...].astype(BF16), preferred_element_type=F32)

        def finish(total):
            o_ref[0:rows, :] = _pack_bf16_pairs(total)
            if rows < tm:
                o_ref[rows:tm, :] = jnp.zeros((tm - rows, o_ref.shape[1]), o_ref.dtype)

        if nf == 1:
            finish(part)
            return

        @pl.when(j == 0)
        def _():
            acc_ref[0:rows, :] = part

        @pl.when((j > 0) & (j < nf - 1))
        def _():
            acc_ref[0:rows, :] += part

        @pl.when(j == nf - 1)
        def _():
            finish(acc_ref[0:rows, :] + part)

    part = tm // EXPERT_ROW_PARTS
    for n in range(1, EXPERT_ROW_PARTS + 1):
        pl.when((valid > (n - 1) * part) & (valid <= n * part))(functools.partial(swiglu_rows, n * part))

    @pl.when((valid == 0) & (j == 0))
    def _():
        o_ref[...] = jnp.zeros(o_ref.shape, o_ref.dtype)


def expert_ffn(xs, tile_expert, tile_valid, w_gate_up, w_down, *, tm, tf):
    R = xs.shape[0]
    E, F, K = w_down.shape
    nf = F // tf
    live = lambda i, tv: tv[i] > 0
    col = lambda i, j, tv: jnp.where(live(i, tv), j, nf - 1)
    grid_spec = pltpu.PrefetchScalarGridSpec(
        num_scalar_prefetch=2,
        grid=(R // tm, nf),
        in_specs=[
            pl.BlockSpec((tm, K // 2), lambda i, j, te, tv: (te[R // tm + i], 0)),
            pl.BlockSpec((None, K, tf), lambda i, j, te, tv: (te[i], 0, col(i, j, tv))),
            pl.BlockSpec((None, K, tf), lambda i, j, te, tv: (te[i], 0, nf + col(i, j, tv))),
            pl.BlockSpec((None, tf, K), lambda i, j, te, tv: (te[i], col(i, j, tv), 0)),
        ],
        out_specs=pl.BlockSpec((tm, K // 2), lambda i, j, te, tv: (i, 0)),
        scratch_shapes=[pltpu.VMEM((tm, K), F32)],
    )
    return pl.pallas_call(
        functools.partial(_expert_ffn_kernel, nf=nf),
        grid_spec=grid_spec,
        out_shape=jax.ShapeDtypeStruct((R, K // 2), jnp.int32),
        compiler_params=_params("arbitrary", "arbitrary"),
        name="moe_expert_ffn",
    )(tile_expert, tile_valid, xs, w_gate_up, w_gate_up, w_down)


def moe_residual(h, proj, gain, w_router, w_gate_up, w_down):
    M, K = h.shape
    p = PERM_TILE
    tm = EXPERT_TILE
    n_rows = 2 * M + N_EXPERTS * tm
    h, xn, meta, cnt = moe_router(h, proj, gain, w_router, tm=p)

    counts = cnt[0, :N_EXPERTS].astype(jnp.int32)
    padded = (counts + tm - 1) // tm * tm
    ends = jnp.cumsum(padded)
    offsets = ends - padded
    n_tiles = n_rows // tm
    tile_row = jnp.arange(n_tiles) * tm
    n_used = ends[-1] // tm
    last_used = jnp.minimum(tile_row // tm, n_used - 1)
    expert_of = lambda row: jnp.minimum(jnp.sum(ends[None, :] <= row[:, None], axis=1), N_EXPERTS - 1)
    tile_expert = expert_of(last_used * tm)
    row_in_expert = tile_row - jnp.sum(jnp.where(tile_expert[:, None] == jnp.arange(N_EXPERTS), offsets, 0), axis=1)
    own_count = jnp.sum(jnp.where(tile_expert[:, None] == jnp.arange(N_EXPERTS), counts, 0), axis=1)
    tile_valid = jnp.where(tile_row // tm < n_used, jnp.clip(own_count - row_in_expert, 0, tm), 0)
    tile_tables = jnp.concatenate([tile_expert, last_used]).astype(jnp.int32)

    idx = meta[:, META_IDX:META_IDX + 2].astype(jnp.int32)
    rank = meta[:, META_RANK:META_RANK + 2].astype(jnp.int32)
    dest = (jnp.sum(jnp.where(idx[:, :, None] == jnp.arange(N_EXPERTS), offsets, 0), axis=-1) + rank).T

    xs = scatter_rows(xn, dest, n_rows)
    ys = expert_ffn(xs, tile_tables, tile_valid.astype(jnp.int32), w_gate_up, w_down, tm=tm, tf=EXPERT_COLS)
    return moe_combine(h, gather_rows(ys, dest.reshape(-1)), meta)


def _ffn_kernel(x_ref, *refs, n_proj):
    proj_refs = refs[:2 * n_proj]
    g_ref, wg_ref, wu_ref, wd_ref, o_ref, xn_ref, acc_ref = refs[2 * n_proj:]
    j = pl.program_id(1)

    @pl.when(j == 0)
    def _():
        x = _plus_projections(x_ref[...], proj_refs)
        xn_ref[...] = _rms(x, g_ref[...]).astype(BF16)
        acc_ref[...] = x

    xn = xn_ref[...]
    gt = jnp.dot(xn, wg_ref[...], preferred_element_type=F32)
    up = jnp.dot(xn, wu_ref[...], preferred_element_type=F32)
    act = gt / (1.0 + jnp.exp(-gt)) * up
    acc_ref[...] += jnp.dot(act.astype(BF16), wd_ref[...], preferred_element_type=F32)

    @pl.when(j == pl.num_programs(1) - 1)
    def _():
        o_ref[...] = acc_ref[...]


def ffn_residual(x, proj, gain, w_gate_up, w_down, *, tm, tf):
    M, K = x.shape
    F = w_down.shape[0]
    nf = F // tf
    proj_specs, proj_args = _projection_specs(proj, tm)
    return pl.pallas_call(
        functools.partial(_ffn_kernel, n_proj=len(proj)),
        grid=(M // tm, nf),
        in_specs=[
            pl.BlockSpec((tm, K), lambda i, j: (i, 0)),
            *proj_specs,
            pl.BlockSpec((1, K), lambda i, j: (0, 0)),
            pl.BlockSpec((K, tf), lambda i, j: (0, j)),
            pl.BlockSpec((K, tf), lambda i, j: (0, nf + j)),
            pl.BlockSpec((tf, K), lambda i, j: (j, 0)),
        ],
        out_specs=pl.BlockSpec((tm, K), lambda i, j: (i, 0)),
        out_shape=jax.ShapeDtypeStruct((M, K), F32),
        scratch_shapes=[pltpu.VMEM((tm, K), BF16), pltpu.VMEM((tm, K), F32)],
        compiler_params=_params("parallel", "arbitrary"),
        name="ffn_residual",
    )(x, *proj_args, gain.reshape(1, K), w_gate_up, w_gate_up, w_down)


def _even_mix(h, B, S, norm1, w_in, f_bias, q_norm, k_norm, conv_w, a_log, dt_bias, o_norm, w_out):
    M, D = h.shape
    fw, gw = FOX_HEADS * HEAD_DIM, GDN_HEADS * HEAD_DIM
    o_ff = 3 * fw
    o_gq = o_ff + FOX_HEADS
    o_ga = o_gq + 3 * gw
    o_gb = o_ga + GDN_HEADS
    o_gg = o_gb + GDN_HEADS
    w_big = jnp.concatenate([w_in[:, :o_ff], w_in[:, o_gq:o_ga], w_in[:, o_gg:]], axis=1)
    w_small = jnp.concatenate([w_in[:, o_ff:o_gq], w_in[:, o_ga:o_gg],
                               jnp.zeros((D, LANES - FOX_HEADS - 2 * GDN_HEADS), F32)], axis=1)
    w_big, w_small = (t.astype(BF16) for t in lax.optimization_barrier((w_big, w_small)))
    z, zs = norm_matmul(h, norm1, w_big, w_small, tm=ROW_TILE, tn=IN_PROJ_COLS)
    par = jnp.zeros((8, LANES), F32)
    par = par.at[0, LANE_F:LANE_F + FOX_HEADS].set(f_bias).at[0, LANE_A:LANE_A + GDN_HEADS].set(dt_bias)
    par = par.at[1, LANE_A:LANE_A + GDN_HEADS].set(a_log)
    col, row = even_gates(zs.reshape(B, S, LANES), par)
    z = z.reshape(B, S, -1)
    fox = fox_attention(z, col, q_norm, k_norm)
    gdn = gdn_mixer(z, conv_w, col, row, o_norm)
    return [(fox.reshape(M, fw), w_out[:fw]), (gdn.reshape(M, gw), w_out[fw:])]


def _odd_mix(h, B, S, norm1, w_qkv, q_norm, k_norm, w_out):
    M, D = h.shape
    z = norm_matmul(h, norm1, w_qkv, tm=ROW_TILE, tn=QKV_COLS).reshape(B, S, -1)
    half = HEAD_DIM // 2
    inv = jnp.power(ROPE_THETA, -jnp.arange(half, dtype=F32) / half)
    ang = jnp.arange(S, dtype=F32)[:, None] * inv[None, :]
    cos, sin = jnp.cos(ang), jnp.sin(ang)
    cos_full = jnp.concatenate([cos, cos], axis=-1)
    sin_signed = jnp.concatenate([-sin, sin], axis=-1)
    att = moba_attention(z, cos_full, sin_signed, q_norm, k_norm)
    return [(att.reshape(M, -1), w_out)]


def kernel(x, e_norm1, e_w_in, e_fox_f_bias, e_fox_q_norm, e_fox_k_norm, e_gdn_conv, e_gdn_a_log,
           e_gdn_dt_bias, e_gdn_o_norm, e_w_out, e_norm2, e_ffn_w_gate_up, e_ffn_w_down,
           o_norm1, o_w_qkv, o_q_norm, o_k_norm, o_w_out, o_norm2, o_router, o_exp_w_gate_up, o_exp_w_down):
    B, S, D = x.shape
    h = x.reshape(B * S, D)
    depth = e_norm1.shape[0] + o_norm1.shape[0]
    for layer in range(depth):
        i = layer // 2
        if layer % 2 == 0:
            mix = _even_mix(h, B, S, e_norm1[i], e_w_in[i], e_fox_f_bias[i], e_fox_q_norm[i], e_fox_k_norm[i],
                            e_gdn_conv[i], e_gdn_a_log[i], e_gdn_dt_bias[i], e_gdn_o_norm[i], e_w_out[i])
            h = ffn_residual(h, mix, e_norm2[i], e_ffn_w_gate_up[i].astype(BF16), e_ffn_w_down[i].astype(BF16),
                             tm=FFN_ROWS, tf=FFN_COLS)
        else:
            mix = _odd_mix(h, B, S, o_norm1[i], o_w_qkv[i], o_q_norm[i], o_k_norm[i], o_w_out[i])
            h = moe_residual(h, mix, o_norm2[i], o_router[i], o_exp_w_gate_up[i], o_exp_w_down[i])
    return h.reshape(B, S, D)
```

```python
import functools

import jax
import jax.numpy as jnp
from jax import lax
from jax.experimental import pallas as pl
from jax.experimental.pallas import tpu as pltpu
from jax.experimental.pallas import tpu_sc as plsc

F32 = jnp.float32
BF16 = jnp.bfloat16

HEAD_DIM = 128
FOX_HEADS = 4
GDN_HEADS = 4
CONV_WIDTH = 4
MOBA_BLOCK = 256
MOBA_TOPK = 3
N_EXPERTS = 8
ROPE_THETA = 10000.0
EPS = 1e-6

LANES = 128
GDN_CHUNK = 128
INV_BLOCK = 16
GDN_HEADS_PER_STEP = 4
GDN_CHUNKS_PER_STEP = 4
NEG = -(2.0 ** 100)
LOG2E = 1.4426950408889634
ATT_BLOCK = 256
ATT_HEADS = 4
KEY_BLOCKS_PER_STEP = 2
V_ROWS = HEAD_DIM + 16
PERM_TILE = 1024
ROUTER_CHAINS = 4
SC_ROWS = 64
ROW_TILE = 1024
MXU_COLS = 256
IN_PROJ_COLS = 7 * MXU_COLS
QKV_COLS = 6 * MXU_COLS
FFN_ROWS, FFN_COLS = 512, 1408
EXPERT_TILE = 1024
EXPERT_COLS = 896
EXPERT_ROW_PARTS = 4
VMEM_LIMIT_BYTES = 56 * 1024 * 1024

FOX_Q0, FOX_K0, FOX_V0 = 0, 4, 8
GDN_Q0, GDN_K0, GDN_V0, GDN_G0 = 12, 16, 20, 24
LANE_F, LANE_A, LANE_B = 0, 4, 8


def _params(*sem):
    return pltpu.CompilerParams(dimension_semantics=sem, vmem_limit_bytes=VMEM_LIMIT_BYTES)


def _rms(x, gain):
    return x * lax.rsqrt(jnp.mean(x * x, axis=-1, keepdims=True) + EPS) * gain


def _dot_nt(a, b, **kw):
    return lax.dot_general(a, b, (((1,), (1,)), ((), ())), preferred_element_type=F32, **kw)


def _pick_lane(x, lane_idx):
    lane = lax.broadcasted_iota(jnp.int32, x.shape, 1)
    return jnp.sum(jnp.where(lane == lane_idx, x, 0.0), axis=-1, keepdims=True)


def _pack_bf16_pairs(x):
    n = x.shape[1] // 2
    hi = pltpu.bitcast(x[:, :n].astype(BF16).astype(F32), jnp.uint32)
    lo = pltpu.bitcast(x[:, n:].astype(BF16).astype(F32), jnp.uint32)
    return pltpu.bitcast(hi | (lo >> 16), jnp.int32)


def _unpack_bf16_pairs(w):
    u = pltpu.bitcast(w, jnp.uint32)
    hi = pltpu.bitcast(u & jnp.uint32(0xFFFF0000), F32).astype(BF16)
    lo = pltpu.bitcast(u << 16, F32).astype(BF16)
    return jnp.concatenate([hi, lo], axis=1)


def _norm_mm_kernel(x_ref, g_ref, w_ref, *rest, has_aux):
    if has_aux:
        waux_ref, o_ref, oaux_ref, xn_ref = rest
    else:
        o_ref, xn_ref = rest

    @pl.when(pl.program_id(1) == 0)
    def _():
        xn = _rms(x_ref[...], g_ref[...]).astype(BF16)
        xn_ref[...] = xn
        if has_aux:
            oaux_ref[...] = jnp.dot(xn, waux_ref[...], preferred_element_type=F32)

    o_ref[...] = jnp.dot(xn_ref[...], w_ref[...].astype(BF16), preferred_element_type=F32).astype(o_ref.dtype)


def norm_matmul(x, gain, w, w_aux=None, *, tm, tn, out_dtype=BF16):
    M, K = x.shape
    N = w.shape[1]
    has_aux = w_aux is not None
    in_specs = [
        pl.BlockSpec((tm, K), lambda i, j: (i, 0)),
        pl.BlockSpec((1, K), lambda i, j: (0, 0)),
        pl.BlockSpec((K, tn), lambda i, j: (0, j)),
    ]
    out_shape = [jax.ShapeDtypeStruct((M, N), out_dtype)]
    out_specs = [pl.BlockSpec((tm, tn), lambda i, j: (i, j))]
    args = [x, gain.reshape(1, K), w]
    if has_aux:
        in_specs.append(pl.BlockSpec((K, LANES), lambda i, j: (0, 0)))
        out_shape.append(jax.ShapeDtypeStruct((M, LANES), F32))
        out_specs.append(pl.BlockSpec((tm, LANES), lambda i, j: (i, 0)))
        args.append(w_aux)
    res = pl.pallas_call(
        functools.partial(_norm_mm_kernel, has_aux=has_aux),
        grid=(M // tm, N // tn),
        in_specs=in_specs,
        out_specs=out_specs,
        out_shape=out_shape,
        scratch_shapes=[pltpu.VMEM((tm, K), BF16)],
        compiler_params=_params("parallel", "arbitrary"),
        name="norm_matmul",
    )(*args)
    return res if has_aux else res[0]


def _plus_projections(x, proj_refs, rows=slice(None)):
    for a_ref, w_ref in zip(proj_refs[0::2], proj_refs[1::2]):
        x = x + jnp.dot(a_ref[rows, :], w_ref[...].astype(BF16), preferred_element_type=F32)
    return x


def _projection_specs(pairs, tm):
    specs, args = [], []
    for a, w in pairs:
        specs.append(pl.BlockSpec((tm, a.shape[1]), lambda i, *_: (i, 0)))
        specs.append(pl.BlockSpec(w.shape, lambda i, *_: (0, 0)))
        args += [a, w]
    return specs, args


def _gate_kernel(zs_ref, par_ref, col_ref, row_ref, *, S, B):
    C = GDN_CHUNK
    bias = par_ref[0:1, :]
    neg_a = -jnp.exp(par_ref[1:2, :])
    r = lax.broadcasted_iota(jnp.int32, (C, C), 0)
    c = lax.broadcasted_iota(jnp.int32, (C, C), 1)
    tril = (r >= c).astype(F32)
    lane = lax.broadcasted_iota(jnp.int32, (C, LANES), 1)

    def body(n, carry):
        sl = pl.ds(pl.multiple_of(n * C, C), C)
        us, betas = [], []
        for bi in range(B):
            z = zs_ref[bi, sl, :]
            t = z + bias
            soft = jnp.log(1.0 + jnp.exp(-jnp.abs(t)))
            log_f = jnp.minimum(t, 0.0) - soft
            g = neg_a * (jnp.maximum(t, 0.0) + soft)
            betas.append(1.0 / (1.0 + jnp.exp(-z)))
            us.append(jnp.where(lane < LANE_A, log_f, jnp.where(lane < LANE_B, g, 0.0)))
        sums = jnp.dot(tril, jnp.concatenate(us, axis=1), preferred_element_type=F32, precision=lax.Precision.HIGHEST)
        last = []
        for bi in range(B):
            cs = sums[:, bi * LANES:(bi + 1) * LANES] + jnp.where(lane < LANE_A, carry[bi], 0.0)
            out = jnp.where(lane < LANE_B, cs, betas[bi])
            col_ref[bi, sl, :] = out
            out_t = out.T
            for hd in range(GDN_HEADS):
                row_ref[bi, hd, :, sl] = out_t[LANE_A + hd:LANE_A + hd + 1, :]
            last.append(cs[C - 1:C, :])
        return tuple(last)

    lax.fori_loop(0, S // C, body, tuple(jnp.zeros((1, LANES), F32) for _ in range(B)))


def even_gates(zs, par):
    B, S, _ = zs.shape
    return pl.pallas_call(
        functools.partial(_gate_kernel, S=S, B=B),
        grid=(1,),
        in_specs=[
            pl.BlockSpec((B, S, LANES), lambda i: (0, 0, 0)),
            pl.BlockSpec((8, LANES), lambda i: (0, 0)),
        ],
        out_specs=[
            pl.BlockSpec((B, S, LANES), lambda i: (0, 0, 0)),
            pl.BlockSpec((B, GDN_HEADS, 1, S), lambda i: (0, 0, 0, 0)),
        ],
        out_shape=[jax.ShapeDtypeStruct((B, S, LANES), F32), jax.ShapeDtypeStruct((B, GDN_HEADS, 1, S), F32)],
        compiler_params=_params("arbitrary"),
        name="even_gates",
    )(zs, par)


def _split3(x):
    hi = x.astype(BF16).astype(F32)
    mid = (x - hi).astype(BF16).astype(F32)
    lo = (x - hi - mid).astype(BF16).astype(F32)
    return hi, mid, lo


def _interleave(gens):
    out = [None] * len(gens)
    live = list(range(len(gens)))
    while live:
        for n in list(live):
            try:
                next(gens[n])
            except StopIteration as stop:
                out[n] = stop.value
                live.remove(n)
    return out


def _attend(state, c, qa, ka_ref, vt_ref, key0, nkeys, keep=None):
    ks = pl.ds(pl.multiple_of(key0, ATT_BLOCK), nkeys)
    st = _dot_nt(ka_ref[ks, :], qa)
    yield
    if keep is not None:
        st = jnp.where(keep, st, NEG)
    m_new = jnp.max(st, axis=0, keepdims=True)
    if state[c] is not None:
        m_old, acc_old = state[c]
        m_new = jnp.maximum(m_old, m_new)
    p = jnp.exp2(st - m_new).astype(BF16)
    pv = jnp.dot(vt_ref[:, ks], p, preferred_element_type=F32)
    state[c] = (m_new, pv if state[c] is None else acc_old * jnp.exp2(m_old - m_new) + pv)


def _attend_tile_pair(i, prep, ka_ref, vt_ref, m_ref, acc_ref, o_ref):
    t = ATT_BLOCK
    chains = [(hd, c) for c in range(2) for hd in range(ATT_HEADS)]
    num = lambda hd, c: 2 * hd + c
    qa = _interleave(prep)

    def step(state, hd, c, which, **kw):
        return _attend(state, num(hd, c), qa[num(hd, c)][which], ka_ref.at[hd], vt_ref.at[hd], **kw)

    def save(state):
        m_ref[...] = jnp.stack([state[n][0] for n in range(len(chains))])
        acc_ref[...] = jnp.stack([state[n][1] for n in range(len(chains))])

    def past_blocks(key0, n_steps):
        state = {n: (m_ref[n], acc_ref[n]) for n in range(len(chains))}
        _interleave([step(state, hd, c, 1, key0=key0 + s * 2 * t, nkeys=2 * t)
                     for s in range(n_steps) for hd, c in chains])
        save(state)

    state = {n: None for n in range(len(chains))}
    keep = _causal_keep()
    _interleave([step(state, hd, c, 0, key0=(2 * i + c) * t, nkeys=t, keep=keep) for hd, c in chains]
                + [step(state, hd, 1, 1, key0=(2 * i) * t, nkeys=t) for hd in range(ATT_HEADS)])
    save(state)

    def four_blocks(g, _):
        past_blocks(g * (4 * t), 2)
        return 0

    lax.fori_loop(0, i // 2, four_blocks, 0)

    @pl.when((i & 1) != 0)
    def _():
        past_blocks((i // 2) * (4 * t), 1)

    for hd, c in chains:
        acc = acc_ref[num(hd, c)]
        out_t = acc[:HEAD_DIM, :] * (1.0 / acc[HEAD_DIM:HEAD_DIM + 1, :])
        o_ref[c * t:(c + 1) * t, hd * HEAD_DIM:(hd + 1) * HEAD_DIM] = out_t.T.astype(o_ref.dtype)


def _transposed(v):
    r = lax.broadcasted_iota(jnp.int32, (HEAD_DIM, HEAD_DIM), 0)
    c = lax.broadcasted_iota(jnp.int32, (HEAD_DIM, HEAD_DIM), 1)
    v_t = _dot_nt(jnp.where(r == c, 1.0, 0.0).astype(BF16), v).astype(BF16)
    return jnp.concatenate([v_t, jnp.ones((V_ROWS - HEAD_DIM, ATT_BLOCK), BF16)], axis=0)


def _causal_keep():
    key = lax.broadcasted_iota(jnp.int32, (ATT_BLOCK, ATT_BLOCK), 0)
    qry = lax.broadcasted_iota(jnp.int32, (ATT_BLOCK, ATT_BLOCK), 1)
    return key <= qry


_ATT_SCRATCH = lambda S: [pltpu.VMEM((ATT_HEADS, S, 2 * HEAD_DIM), BF16), pltpu.VMEM((ATT_HEADS, V_ROWS, S), BF16),
                          pltpu.VMEM((2 * ATT_HEADS, 1, ATT_BLOCK), F32),
                          pltpu.VMEM((2 * ATT_HEADS, V_ROWS, ATT_BLOCK), F32)]


def _fox_kernel(q_ref, k_ref, v_ref, col_ref, qg_ref, kg_ref, o_ref, ka_ref, vt_ref, m_ref, acc_ref, *, S):
    h0 = pl.program_id(1) * ATT_HEADS
    i = pl.program_id(2)
    t = ATT_BLOCK
    D = HEAD_DIM
    lane = lax.broadcasted_iota(jnp.int32, (t, LANES), 1)

    @pl.when(i == 0)
    def _():
        def prep_keys(g, _):
            todo = []
            for u in range(KEY_BLOCKS_PER_STEP):
                sl = pl.ds(pl.multiple_of((g * KEY_BLOCKS_PER_STEP + u) * t, t), t)
                for hd in range(ATT_HEADS):
                    cols = slice(hd * D, (hd + 1) * D)
                    kn = _rms(k_ref[sl, cols].astype(F32), kg_ref[...]).astype(BF16)
                    hi, mid, lo = _split3(-LOG2E * _pick_lane(col_ref[sl, :], LANE_F + h0 + hd))
                    aug = jnp.where(lane < 3, 1.0, jnp.where(lane == 3, hi, jnp.where(lane == 4, mid,
                                                                                    jnp.where(lane == 5, lo, 0.0))))
                    todo.append((hd, sl, jnp.concatenate([kn, aug.astype(BF16)], axis=1), _transposed(v_ref[sl, cols])))
            for hd, sl, ka, vt in todo:
                ka_ref[hd, sl, :] = ka
                vt_ref[hd, :, sl] = vt
            return 0

        lax.fori_loop(0, S // (t * KEY_BLOCKS_PER_STEP), prep_keys, 0)

    def prep(hd, c):
        q = q_ref[c * t:(c + 1) * t, hd * D:(hd + 1) * D].astype(F32)
        qn = (_rms(q, qg_ref[...]) * (LOG2E * D ** -0.5)).astype(BF16)
        yield
        qsl = pl.ds(pl.multiple_of((2 * i + c) * t, t), t)
        hi, mid, lo = _split3(LOG2E * _pick_lane(col_ref[qsl, :], LANE_F + h0 + hd))
        aug = jnp.where(lane == 0, hi,
                        jnp.where(lane == 1, mid, jnp.where(lane == 2, lo, jnp.where(lane < 6, 1.0, 0.0))))
        qa = jnp.concatenate([qn, aug.astype(BF16)], axis=-1)
        return qa, qa

    _attend_tile_pair(i, [prep(hd, c) for hd in range(ATT_HEADS) for c in range(2)],
                      ka_ref, vt_ref, m_ref, acc_ref, o_ref)


def fox_attention(z, col, q_gain, k_gain):
    B, S, _ = z.shape
    t = 2 * ATT_BLOCK
    w = ATT_HEADS * HEAD_DIM
    return pl.pallas_call(
        functools.partial(_fox_kernel, S=S),
        grid=(B, FOX_HEADS // ATT_HEADS, S // t),
        in_specs=[
            pl.BlockSpec((None, t, w), lambda b, h, i: (b, i, FOX_Q0 // ATT_HEADS + h)),
            pl.BlockSpec((None, S, w), lambda b, h, i: (b, 0, FOX_K0 // ATT_HEADS + h)),
            pl.BlockSpec((None, S, w), lambda b, h, i: (b, 0, FOX_V0 // ATT_HEADS + h)),
            pl.BlockSpec((None, S, LANES), lambda b, h, i: (b, 0, 0)),
            pl.BlockSpec((1, HEAD_DIM), lambda b, h, i: (0, 0)),
            pl.BlockSpec((1, HEAD_DIM), lambda b, h, i: (0, 0)),
        ],
        out_specs=pl.BlockSpec((None, t, w), lambda b, h, i: (b, i, h)),
        out_shape=jax.ShapeDtypeStruct((B, S, FOX_HEADS * HEAD_DIM), BF16),
        scratch_shapes=_ATT_SCRATCH(S),
        compiler_params=_params("parallel", "parallel", "arbitrary"),
        name="fox_attention",
    )(z, z, z, col, q_gain.reshape(1, -1), k_gain.reshape(1, -1))


def _unit_lower_inverse(m):
    n = m.shape[0]
    r = lax.broadcasted_iota(jnp.int32, (n, n), 0)
    c = lax.broadcasted_iota(jnp.int32, (n, n), 1)
    eye = (r == c).astype(F32)

    def same_block(b):
        return (r // b) == (c // b)

    p = jnp.where(same_block(INV_BLOCK), m, 0.0)
    inv = eye - p
    k = 2
    while k < INV_BLOCK:
        pb = p.astype(BF16)
        p = jnp.dot(pb, pb, preferred_element_type=F32)
        yield
        inv = jnp.dot(inv.astype(BF16), (eye + p).astype(BF16), preferred_element_type=F32)
        yield
        k *= 2
    b = INV_BLOCK
    while b < n:
        off = jnp.where(same_block(2 * b), jnp.where(same_block(b), 0.0, m), 0.0).astype(BF16)
        ib = inv.astype(BF16)
        left = jnp.dot(ib, off, preferred_element_type=F32).astype(BF16)
        yield
        inv = inv - jnp.dot(left, ib, preferred_element_type=F32)
        yield
        b *= 2
    return inv


def _gdn_kernel(q_ref, k_ref, v_ref, gg_ref, wq_ref, wk_ref, wv_ref, col_ref, row_ref, on_ref,
                o_ref, qs_ref, ks_ref, vs_ref, *, S, rows, hb):
    h0 = pl.program_id(1) * hb
    C = GDN_CHUNK
    D = HEAD_DIM

    streams = [(x_ref, w_ref, dst_ref, hh, mode) for hh in range(hb)
               for x_ref, w_ref, dst_ref, mode in ((q_ref, wq_ref, qs_ref, "q"), (k_ref, wk_ref, ks_ref, "k"),
                                                   (v_ref, wv_ref, vs_ref, "v"))]
    halo = 16

    def conv_step(base, windows):
        outs = []
        for (x_ref, w_ref, dst_ref, hh, mode), win in zip(streams, windows):
            w = w_ref[:, hh * D:(hh + 1) * D]
            y = jnp.zeros((rows, D), F32)
            for tap in range(CONV_WIDTH):
                lead = halo - (CONV_WIDTH - 1) + tap
                y = y + w[tap:tap + 1, :] * pltpu.roll(win, rows + halo - lead, 0)[0:rows, :]
            y = y / (1.0 + jnp.exp(-y))
            if mode != "v":
                y = y * lax.rsqrt(jnp.sum(y * y, axis=-1, keepdims=True) + EPS)
            if mode == "q":
                y = y * D ** -0.5
            outs.append(y)
        for (x_ref, w_ref, dst_ref, hh, mode), y in zip(streams, outs):
            dst_ref[hh, pl.ds(base, rows), :] = y.astype(dst_ref.dtype)

    conv_step(0, [jnp.concatenate([jnp.zeros((halo, D), F32), x_ref[0:rows, hh * D:(hh + 1) * D].astype(F32)], axis=0)
                  for x_ref, _, _, hh, _ in streams])

    def conv_rest(n, _):
        base = pl.multiple_of(n * rows, rows)
        conv_step(base, [x_ref[pl.ds(pl.multiple_of(base - halo, halo), rows + halo), hh * D:(hh + 1) * D].astype(F32)
                         for x_ref, _, _, hh, _ in streams])
        return 0

    lax.fori_loop(1, S // rows, conv_rest, 0)

    r = lax.broadcasted_iota(jnp.int32, (C, C), 0)
    c = lax.broadcasted_iota(jnp.int32, (C, C), 1)
    incl = r >= c
    strict = r > c

    def chunk_local(hh, sl, tab):
        q = qs_ref[hh, sl, :].astype(F32)
        k = ks_ref[hh, sl, :].astype(F32)
        v = vs_ref[hh, sl, :].astype(F32)
        gcol = _pick_lane(tab, LANE_A + h0 + hh)
        beta = _pick_lane(tab, LANE_B + h0 + hh)
        grow = row_ref[hh, :, sl]
        glast = gcol[C - 1:C, :]
        decay = jnp.where(incl, jnp.exp(jnp.where(incl, gcol - grow, 0.0)), 0.0)
        eg = jnp.exp(gcol)
        kb = k * beta
        kbf = k.astype(BF16)
        m = jnp.where(strict, _dot_nt(kb.astype(BF16), kbf) * decay, 0.0)
        attn = (_dot_nt(q.astype(BF16), kbf) * decay).astype(BF16)
        yield
        tinv = (yield from _unit_lower_inverse(m)).astype(BF16)
        rhs = jnp.concatenate([v * beta, kb * eg], axis=-1).astype(BF16)
        sol = jnp.dot(tinv, rhs, preferred_element_type=F32)
        yield
        gate = gg_ref[sl, hh * D:(hh + 1) * D].astype(F32)
        return dict(u=sol[:, :D], w=sol[:, D:].astype(BF16), attn=attn, qg=(q * eg).astype(BF16),
                    kg_t=(k * jnp.exp(glast - gcol)).T.astype(BF16), keep=jnp.exp(glast),
                    gate=gate / (1.0 + jnp.exp(-gate)))

    def chunk_state(parts, state):
        outs = []
        for c in parts:
            sb = state.astype(BF16)
            v_new = c["u"] - jnp.dot(c["w"], sb, preferred_element_type=F32)
            o_state = jnp.dot(c["qg"], sb, preferred_element_type=F32)
            yield
            vb = v_new.astype(BF16)
            o = o_state + jnp.dot(c["attn"], vb, preferred_element_type=F32)
            state = state * c["keep"] + jnp.dot(c["kg_t"], vb, preferred_element_type=F32)
            yield
            outs.append((_rms(o, on_ref[...]) * c["gate"]).astype(o_ref.dtype))
        return state, outs

    def chunk_group(n, states):
        sls = [pl.ds(pl.multiple_of((n * GDN_CHUNKS_PER_STEP + g) * C, C), C) for g in range(GDN_CHUNKS_PER_STEP)]
        tabs = [col_ref[sl, :] for sl in sls]
        parts = _interleave([chunk_local(hh, sls[g], tabs[g]) for g in range(GDN_CHUNKS_PER_STEP) for hh in range(hb)])
        res = _interleave([chunk_state([parts[g * hb + hh] for g in range(GDN_CHUNKS_PER_STEP)], states[hh])
                           for hh in range(hb)])
        for g in range(GDN_CHUNKS_PER_STEP):
            o_ref[sls[g], :] = jnp.concatenate([outs[g] for _, outs in res], axis=-1)
        return tuple(s for s, _ in res)

    lax.fori_loop(0, S // (C * GDN_CHUNKS_PER_STEP), chunk_group, tuple(jnp.zeros((D, D), F32) for _ in range(hb)))


def gdn_mixer(z, conv_w, col, row, o_gain):
    B, S, _ = z.shape
    D = HEAD_DIM
    hb = GDN_HEADS_PER_STEP
    rows = 256
    seq = lambda off: pl.BlockSpec((None, S, hb * D), lambda b, h: (b, 0, off // hb + h),
                                   pipeline_mode=pl.Buffered(1))
    cw = lambda off: pl.BlockSpec((CONV_WIDTH, hb * D), lambda b, h: (0, off // hb + h))
    return pl.pallas_call(
        functools.partial(_gdn_kernel, S=S, rows=rows, hb=hb),
        grid=(B, GDN_HEADS // hb),
        in_specs=[
            seq(GDN_Q0), seq(GDN_K0), seq(GDN_V0), seq(GDN_G0),
            cw(0), cw(GDN_HEADS), cw(2 * GDN_HEADS),
            pl.BlockSpec((None, S, LANES), lambda b, h: (b, 0, 0)),
            pl.BlockSpec((None, hb, 1, S), lambda b, h: (b, h, 0, 0)),
            pl.BlockSpec((1, D), lambda b, h: (0, 0)),
        ],
        out_specs=pl.BlockSpec((None, S, hb * D), lambda b, h: (b, 0, h)),
        out_shape=jax.ShapeDtypeStruct((B, S, GDN_HEADS * D), BF16),
        scratch_shapes=[pltpu.VMEM((hb, S, D), BF16)] * 3,
        compiler_params=_params("parallel", "parallel"),
        name="gdn_mixer",
    )(z, z, z, z, conv_w, conv_w, conv_w, col, row, o_gain.reshape(1, D))


def _rope(x, cos, sin_signed):
    return x * cos + pltpu.roll(x, HEAD_DIM // 2, 1) * sin_signed


def _moba_kernel(q_ref, k_ref, v_ref, cos_ref, sin_ref, qg_ref, kg_ref, o_ref,
                 ka_ref, vt_ref, m_ref, acc_ref, kmean_ref, *, S):
    i = pl.program_id(2)
    t = ATT_BLOCK
    D = HEAD_DIM
    lane = lax.broadcasted_iota(jnp.int32, (t, LANES), 1)

    @pl.when(i == 0)
    def _():
        kmean_ref[...] = jnp.zeros(kmean_ref.shape, F32)

        def prep_keys(g, _):
            todo = []
            for u in range(KEY_BLOCKS_PER_STEP):
                n = g * KEY_BLOCKS_PER_STEP + u
                sl = pl.ds(pl.multiple_of(n * t, t), t)
                onehot = jnp.where(lane == n, 1.0, 0.0).astype(BF16)
                for hd in range(ATT_HEADS):
                    cols = slice(hd * D, (hd + 1) * D)
                    k = _rope(_rms(k_ref[sl, cols].astype(F32), kg_ref[...]), cos_ref[sl, :], sin_ref[sl, :])
                    todo.append((hd, n, sl, jnp.concatenate([k.astype(BF16), onehot], axis=1),
                                 jnp.mean(k, axis=0, keepdims=True), _transposed(v_ref[sl, cols])))
            for hd, n, sl, ka, kmean, vt in todo:
                ka_ref[hd, sl, :] = ka
                kmean_ref[hd, pl.ds(n, 1), :] = kmean
                vt_ref[hd, :, sl] = vt
            return 0

        lax.fori_loop(0, S // (t * KEY_BLOCKS_PER_STEP), prep_keys, 0)

    nb = -(-(S // t) // 8) * 8

    def prep(hd, c):
        cur = 2 * i + c
        qsl = pl.ds(pl.multiple_of(cur * t, t), t)
        q = _rope(_rms(q_ref[c * t:(c + 1) * t, hd * D:(hd + 1) * D].astype(F32), qg_ref[...]),
                  cos_ref[qsl, :], sin_ref[qsl, :])
        yield
        kmean = kmean_ref[hd]
        km_hi = kmean.astype(BF16)
        km_split = jnp.concatenate([km_hi, (kmean - km_hi.astype(F32)).astype(BF16)], axis=0)
        q_hi = q.astype(BF16)
        q_lo = (q - q_hi.astype(F32)).astype(BF16)
        part = _dot_nt(km_split, q_hi)
        gate = (part[:LANES] + part[LANES:] + _dot_nt(km_split[:LANES, :], q_lo))[0:nb]
        yield
        blk = lax.broadcasted_iota(jnp.int32, (nb, t), 0)
        blk_f = blk.astype(F32)
        gate = jnp.where(blk < cur, gate, -jnp.inf)
        sel = jnp.full((nb, t), NEG, F32)
        for _ in range(MOBA_TOPK):
            top = jnp.max(gate, axis=0, keepdims=True)
            first = jnp.min(jnp.where(gate == top, blk_f, float(LANES)), axis=0, keepdims=True)
            pick = blk_f == first
            sel = jnp.where(pick & (first < cur.astype(F32)), 0.0, sel)
            gate = jnp.where(pick, -jnp.inf, gate)
        sel_bias = jnp.concatenate([sel, jnp.full((LANES - nb, t), NEG, F32)], axis=0).T
        qs = (q * (LOG2E * D ** -0.5)).astype(BF16)
        return (jnp.concatenate([qs, jnp.zeros((t, LANES), BF16)], axis=-1),
                jnp.concatenate([qs, sel_bias.astype(BF16)], axis=-1))

    _attend_tile_pair(i, [prep(hd, c) for hd in range(ATT_HEADS) for c in range(2)],
                      ka_ref, vt_ref, m_ref, acc_ref, o_ref)


def moba_attention(z, cos, sin_signed, q_gain, k_gain):
    B, S, W = z.shape
    H = W // (3 * HEAD_DIM)
    assert ATT_BLOCK == MOBA_BLOCK
    t = 2 * ATT_BLOCK
    w = ATT_HEADS * HEAD_DIM
    hp = H // ATT_HEADS
    return pl.pallas_call(
        functools.partial(_moba_kernel, S=S),
        grid=(B, hp, S // t),
        in_specs=[
            pl.BlockSpec((None, t, w), lambda b, h, i: (b, i, h)),
            pl.BlockSpec((None, S, w), lambda b, h, i: (b, 0, hp + h)),
            pl.BlockSpec((None, S, w), lambda b, h, i: (b, 0, 2 * hp + h)),
            pl.BlockSpec((S, HEAD_DIM), lambda b, h, i: (0, 0)),
            pl.BlockSpec((S, HEAD_DIM), lambda b, h, i: (0, 0)),
            pl.BlockSpec((1, HEAD_DIM), lambda b, h, i: (0, 0)),
            pl.BlockSpec((1, HEAD_DIM), lambda b, h, i: (0, 0)),
        ],
        out_specs=pl.BlockSpec((None, t, w), lambda b, h, i: (b, i, h)),
        out_shape=jax.ShapeDtypeStruct((B, S, H * HEAD_DIM), BF16),
        scratch_shapes=_ATT_SCRATCH(S) + [pltpu.VMEM((ATT_HEADS, LANES, HEAD_DIM), F32)],
        compiler_params=_params("parallel", "parallel", "arbitrary"),
        name="moba_attention",
    )(z, z, z, cos, sin_signed, q_gain.reshape(1, -1), k_gain.reshape(1, -1))


META_IDX, META_RANK, META_GATE = 0, 2, 4


def _router_kernel(x_ref, *refs, n_proj):
    proj_refs = refs[:2 * n_proj]
    g_ref, w_ref, y_ref, xn_ref, meta_ref, cnt_ref, carry_ref = refs[2 * n_proj:]
    i = pl.program_id(0)
    tm = x_ref.shape[0]

    @pl.when(i == 0)
    def _():
        carry_ref[...] = jnp.zeros(carry_ref.shape, F32)

    w = w_ref[...]
    w_hi = w.astype(BF16)
    w_lo = (w - w_hi.astype(F32)).astype(BF16)
    part = tm // ROUTER_CHAINS
    lane = lax.broadcasted_iota(jnp.int32, (part, LANES), 1)
    lane_f = lane.astype(F32)

    def route(n):
        rows = slice(n * part, (n + 1) * part)
        y = _plus_projections(x_ref[rows, :], proj_refs, rows)
        y_ref[rows, :] = y
        yield
        xn = _rms(y, g_ref[...])
        xn_ref[rows, :] = _pack_bf16_pairs(xn)
        x_hi = xn.astype(BF16)
        x_lo = (xn - x_hi.astype(F32)).astype(BF16)
        logits = (jnp.dot(x_hi, w_hi, preferred_element_type=F32) + jnp.dot(x_hi, w_lo, preferred_element_type=F32)
                  + jnp.dot(x_lo, w_hi, preferred_element_type=F32))
        yield
        logits = jnp.where(lane < N_EXPERTS, logits, -jnp.inf)
        top1 = jnp.max(logits, axis=-1, keepdims=True)
        yield
        idx1 = jnp.min(jnp.where(logits == top1, lane_f, float(LANES)), axis=-1, keepdims=True)
        yield
        rest = jnp.where(lane_f == idx1, -jnp.inf, logits)
        top2 = jnp.max(rest, axis=-1, keepdims=True)
        yield
        idx2 = jnp.min(jnp.where(rest == top2, lane_f, float(LANES)), axis=-1, keepdims=True)
        e2 = jnp.exp(top2 - top1)
        denom = 1.0 + e2
        return idx1, idx2, 1.0 / denom, e2 / denom, jnp.where((lane_f == idx1) | (lane_f == idx2), 1.0, 0.0)

    picks = _interleave([route(n) for n in range(ROUTER_CHAINS)])
    chosen = jnp.concatenate([p[4] for p in picks], axis=0)
    r = lax.broadcasted_iota(jnp.int32, (tm, tm), 0)
    c = lax.broadcasted_iota(jnp.int32, (tm, tm), 1)
    ahead = jnp.dot(jnp.where(r > c, 1.0, 0.0).astype(BF16), chosen.astype(BF16), preferred_element_type=F32)
    carry = carry_ref[...]
    for n, (idx1, idx2, g1, g2, _) in enumerate(picks):
        rows = slice(n * part, (n + 1) * part)
        rank = ahead[rows, :] + carry
        rank1 = jnp.sum(jnp.where(lane_f == idx1, rank, 0.0), axis=-1, keepdims=True)
        rank2 = jnp.sum(jnp.where(lane_f == idx2, rank, 0.0), axis=-1, keepdims=True)
        meta = jnp.zeros((part, LANES), F32)
        for k, v in enumerate((idx1, idx2, rank1, rank2, g1, g2)):
            meta = jnp.where(lane == k, v, meta)
        meta_ref[rows, :] = meta
    carry = carry + jnp.sum(chosen, axis=0, keepdims=True)
    carry_ref[...] = carry
    cnt_ref[...] = jnp.broadcast_to(carry, cnt_ref.shape)


def moe_router(x, proj, gain, w_router, *, tm):
    M, K = x.shape
    w = jnp.zeros((K, LANES), F32).at[:, :N_EXPERTS].set(w_router)
    proj_specs, proj_args = _projection_specs(proj, tm)
    return pl.pallas_call(
        functools.partial(_router_kernel, n_proj=len(proj)),
        grid=(M // tm,),
        in_specs=[
            pl.BlockSpec((tm, K), lambda i: (i, 0)),
            *proj_specs,
            pl.BlockSpec((1, K), lambda i: (0, 0)),
            pl.BlockSpec((K, LANES), lambda i: (0, 0)),
        ],
        out_specs=[
            pl.BlockSpec((tm, K), lambda i: (i, 0)),
            pl.BlockSpec((tm, K // 2), lambda i: (i, 0)),
            pl.BlockSpec((tm, LANES), lambda i: (i, 0)),
            pl.BlockSpec((8, LANES), lambda i: (0, 0)),
        ],
        out_shape=[
            jax.ShapeDtypeStruct((M, K), F32),
            jax.ShapeDtypeStruct((M, K // 2), jnp.int32),
            jax.ShapeDtypeStruct((M, LANES), F32),
            jax.ShapeDtypeStruct((8, LANES), F32),
        ],
        scratch_shapes=[pltpu.VMEM((1, LANES), F32)],
        compiler_params=_params("arbitrary"),
        name="moe_router",
    )(x, *proj_args, gain.reshape(1, K), w)


def _sc_workers():
    info = plsc.get_sparse_core_info()
    return info.num_cores, info.num_cores * info.num_subcores


def scatter_rows(rows, dest, n_out):
    M, W = rows.shape
    nc, nw = _sc_workers()
    assert M % (nw * SC_ROWS) == 0
    per_w = M // nw
    mesh = plsc.VectorSubcoreMesh(core_axis_name="c", subcore_axis_name="s")

    @functools.partial(
        pl.kernel, mesh=mesh, out_type=jax.ShapeDtypeStruct((n_out, W), rows.dtype),
        scratch_types=[pltpu.VMEM((SC_ROWS,), jnp.int32), pltpu.VMEM((SC_ROWS, W), rows.dtype),
                       pltpu.SemaphoreType.DMA])
    def kern(rows_hbm, dest_hbm, out_hbm, idx_v, rows_v, sem):
        wid = lax.axis_index("s") * nc + lax.axis_index("c")

        @pl.loop(0, per_w // SC_ROWS)
        def _(g):
            base = wid * per_w + g * SC_ROWS
            pltpu.sync_copy(rows_hbm.at[pl.ds(base, SC_ROWS)], rows_v)
            for k in range(2):
                pltpu.sync_copy(dest_hbm.at[k, pl.ds(base, SC_ROWS)], idx_v)
                pltpu.async_copy(rows_v, out_hbm.at[idx_v], sem).wait()

    return kern(rows, dest)


def gather_rows(table, idx):
    N = idx.shape[0]
    W = table.shape[1]
    nc, nw = _sc_workers()
    assert N % (nw * SC_ROWS) == 0
    per_w = N // nw
    mesh = plsc.VectorSubcoreMesh(core_axis_name="c", subcore_axis_name="s")

    n_chunks = per_w // SC_ROWS
    assert n_chunks % 2 == 0
    buf = [pltpu.VMEM((SC_ROWS,), jnp.int32), pltpu.VMEM((SC_ROWS, W), table.dtype), pltpu.SemaphoreType.DMA]

    @functools.partial(pl.kernel, mesh=mesh, out_type=jax.ShapeDtypeStruct((N, W), table.dtype),
                       scratch_types=buf + buf)
    def kern(table_hbm, idx_hbm, out_hbm, idx0, rows0, sem0, idx1, rows1, sem1):
        wid = lax.axis_index("s") * nc + lax.axis_index("c")
        slots = ((idx0, rows0, sem0), (idx1, rows1, sem1))

        def fetch(g, slot):
            idx_v, rows_v, sem = slot
            pltpu.sync_copy(idx_hbm.at[pl.ds(wid * per_w + g * SC_ROWS, SC_ROWS)], idx_v)
            pltpu.async_copy(table_hbm.at[idx_v], rows_v, sem)

        def drain(g, slot):
            idx_v, rows_v, sem = slot
            pltpu.make_async_copy(table_hbm.at[idx_v], rows_v, sem).wait()
            pltpu.sync_copy(rows_v, out_hbm.at[pl.ds(wid * per_w + g * SC_ROWS, SC_ROWS)])

        fetch(0, slots[0])

        @pl.loop(0, n_chunks, step=2)
        def _(g):
            fetch(g + 1, slots[1])
            drain(g, slots[0])

            @pl.when(g + 2 < n_chunks)
            def _():
                fetch(g + 2, slots[0])

            drain(g + 1, slots[1])

    return kern(table, idx)


def _combine_kernel(h_ref, y1_ref, y2_ref, meta_ref, o_ref):
    g1 = meta_ref[:, META_GATE:META_GATE + 1]
    g2 = meta_ref[:, META_GATE + 1:META_GATE + 2]
    o_ref[...] = (h_ref[...] + g1 * _unpack_bf16_pairs(y1_ref[...]).astype(F32)
                  + g2 * _unpack_bf16_pairs(y2_ref[...]).astype(F32))


def moe_combine(h, y_pairs, meta):
    M, K = h.shape
    p = PERM_TILE
    nt = M // p
    return pl.pallas_call(
        _combine_kernel,
        grid=(nt,),
        in_specs=[
            pl.BlockSpec((p, K), lambda i: (i, 0)),
            pl.BlockSpec((p, K // 2), lambda i: (i, 0)),
            pl.BlockSpec((p, K // 2), lambda i: (nt + i, 0)),
            pl.BlockSpec((p, LANES), lambda i: (i, 0)),
        ],
        out_specs=pl.BlockSpec((p, K), lambda i: (i, 0)),
        out_shape=jax.ShapeDtypeStruct((M, K), F32),
        compiler_params=_params("parallel"),
        name="moe_combine",
    )(h, y_pairs, y_pairs, meta)


def _expert_ffn_kernel(te_ref, tv_ref, x_ref, wg_ref, wu_ref, wd_ref, o_ref, acc_ref, *, nf):
    del te_ref
    i = pl.program_id(0)
    j = pl.program_id(1)
    tm = x_ref.shape[0]
    valid = tv_ref[i]

    def swiglu_rows(rows):
        x = _unpack_bf16_pairs(x_ref[0:rows, :])
        gt = jnp.dot(x, wg_ref[...].astype(BF16), preferred_element_type=F32)
        up = jnp.dot(x, wu_ref[...].astype(BF16), preferred_element_type=F32)
        act = (gt / (1.0 + jnp.exp(-gt)) * up).astype(BF16)
        part = jnp.dot(act, wd_ref[...].astype(BF16), preferred_element_type=F32)

        def finish(total):
            o_ref[0:rows, :] = _pack_bf16_pairs(total)
            if rows < tm:
                o_ref[rows:tm, :] = jnp.zeros((tm - rows, o_ref.shape[1]), o_ref.dtype)

        if nf == 1:
            finish(part)
            return

        @pl.when(j == 0)
        def _():
            acc_ref[0:rows, :] = part

        @pl.when((j > 0) & (j < nf - 1))
        def _():
            acc_ref[0:rows, :] += part

        @pl.when(j == nf - 1)
        def _():
            finish(acc_ref[0:rows, :] + part)

    part = tm // EXPERT_ROW_PARTS
    for n in range(1, EXPERT_ROW_PARTS + 1):
        pl.when((valid > (n - 1) * part) & (valid <= n * part))(functools.partial(swiglu_rows, n * part))

    @pl.when((valid == 0) & (j == 0))
    def _():
        o_ref[...] = jnp.zeros(o_ref.shape, o_ref.dtype)


def expert_ffn(xs, tile_expert, tile_valid, w_gate_up, w_down, *, tm, tf):
    R = xs.shape[0]
    E, F, K = w_down.shape
    nf = F // tf
    live = lambda i, tv: tv[i] > 0
    col = lambda i, j, tv: jnp.where(live(i, tv), j, nf - 1)
    grid_spec = pltpu.PrefetchScalarGridSpec(
        num_scalar_prefetch=2,
        grid=(R // tm, nf),
        in_specs=[
            pl.BlockSpec((tm, K // 2), lambda i, j, te, tv: (te[R // tm + i], 0)),
            pl.BlockSpec((None, K, tf), lambda i, j, te, tv: (te[i], 0, col(i, j, tv))),
            pl.BlockSpec((None, K, tf), lambda i, j, te, tv: (te[i], 0, nf + col(i, j, tv))),
            pl.BlockSpec((None, tf, K), lambda i, j, te, tv: (te[i], col(i, j, tv), 0)),
        ],
        out_specs=pl.BlockSpec((tm, K // 2), lambda i, j, te, tv: (i, 0)),
        scratch_shapes=[pltpu.VMEM((tm, K), F32)],
    )
    return pl.pallas_call(
        functools.partial(_expert_ffn_kernel, nf=nf),
        grid_spec=grid_spec,
        out_shape=jax.ShapeDtypeStruct((R, K // 2), jnp.int32),
        compiler_params=_params("arbitrary", "arbitrary"),
        name="moe_expert_ffn",
    )(tile_expert, tile_valid, xs, w_gate_up, w_gate_up, w_down)


def moe_residual(h, proj, gain, w_router, w_gate_up, w_down):
    M, K = h.shape
    p = PERM_TILE
    tm = EXPERT_TILE
    n_rows = 2 * M + N_EXPERTS * tm
    h, xn, meta, cnt = moe_router(h, proj, gain, w_router, tm=p)

    counts = cnt[0, :N_EXPERTS].astype(jnp.int32)
    padded = (counts + tm - 1) // tm * tm
    ends = jnp.cumsum(padded)
    offsets = ends - padded
    n_tiles = n_rows // tm
    tile_row = jnp.arange(n_tiles) * tm
    n_used = ends[-1] // tm
    last_used = jnp.minimum(tile_row // tm, n_used - 1)
    expert_of = lambda row: jnp.minimum(jnp.sum(ends[None, :] <= row[:, None], axis=1), N_EXPERTS - 1)
    tile_expert = expert_of(last_used * tm)
    row_in_expert = tile_row - jnp.sum(jnp.where(tile_expert[:, None] == jnp.arange(N_EXPERTS), offsets, 0), axis=1)
    own_count = jnp.sum(jnp.where(tile_expert[:, None] == jnp.arange(N_EXPERTS), counts, 0), axis=1)
    tile_valid = jnp.where(tile_row // tm < n_used, jnp.clip(own_count - row_in_expert, 0, tm), 0)
    tile_tables = jnp.concatenate([tile_expert, last_used]).astype(jnp.int32)

    idx = meta[:, META_IDX:META_IDX + 2].astype(jnp.int32)
    rank = meta[:, META_RANK:META_RANK + 2].astype(jnp.int32)
    dest = (jnp.sum(jnp.where(idx[:, :, None] == jnp.arange(N_EXPERTS), offsets, 0), axis=-1) + rank).T

    xs = scatter_rows(xn, dest, n_rows)
    ys = expert_ffn(xs, tile_tables, tile_valid.astype(jnp.int32), w_gate_up, w_down, tm=tm, tf=EXPERT_COLS)
    return moe_combine(h, gather_rows(ys, dest.reshape(-1)), meta)


def _ffn_kernel(x_ref, *refs, n_proj):
    proj_refs = refs[:2 * n_proj]
    g_ref, wg_ref, wu_ref, wd_ref, o_ref, xn_ref, acc_ref = refs[2 * n_proj:]
    j = pl.program_id(1)

    @pl.when(j == 0)
    def _():
        x = _plus_projections(x_ref[...], proj_refs)
        xn_ref[...] = _rms(x, g_ref[...]).astype(BF16)
        acc_ref[...] = x

    xn = xn_ref[...]
    gt = jnp.dot(xn, wg_ref[...], preferred_element_type=F32)
    up = jnp.dot(xn, wu_ref[...], preferred_element_type=F32)
    act = gt / (1.0 + jnp.exp(-gt)) * up
    acc_ref[...] += jnp.dot(act.astype(BF16), wd_ref[...], preferred_element_type=F32)

    @pl.when(j == pl.num_programs(1) - 1)
    def _():
        o_ref[...] = acc_ref[...]


def ffn_residual(x, proj, gain, w_gate_up, w_down, *, tm, tf):
    M, K = x.shape
    F = w_down.shape[0]
    nf = F // tf
    proj_specs, proj_args = _projection_specs(proj, tm)
    return pl.pallas_call(
        functools.partial(_ffn_kernel, n_proj=len(proj)),
        grid=(M // tm, nf),
        in_specs=[
            pl.BlockSpec((tm, K), lambda i, j: (i, 0)),
            *proj_specs,
            pl.BlockSpec((1, K), lambda i, j: (0, 0)),
            pl.BlockSpec((K, tf), lambda i, j: (0, j)),
            pl.BlockSpec((K, tf), lambda i, j: (0, nf + j)),
            pl.BlockSpec((tf, K), lambda i, j: (j, 0)),
        ],
        out_specs=pl.BlockSpec((tm, K), lambda i, j: (i, 0)),
        out_shape=jax.ShapeDtypeStruct((M, K), F32),
        scratch_shapes=[pltpu.VMEM((tm, K), BF16), pltpu.VMEM((tm, K), F32)],
        compiler_params=_params("parallel", "arbitrary"),
        name="ffn_residual",
    )(x, *proj_args, gain.reshape(1, K), w_gate_up, w_gate_up, w_down)


def _even_mix(h, B, S, norm1, w_in, f_bias, q_norm, k_norm, conv_w, a_log, dt_bias, o_norm, w_out):
    M, D = h.shape
    fw, gw = FOX_HEADS * HEAD_DIM, GDN_HEADS * HEAD_DIM
    o_ff = 3 * fw
    o_gq = o_ff + FOX_HEADS
    o_ga = o_gq + 3 * gw
    o_gb = o_ga + GDN_HEADS
    o_gg = o_gb + GDN_HEADS
    w_big = jnp.concatenate([w_in[:, :o_ff], w_in[:, o_gq:o_ga], w_in[:, o_gg:]], axis=1)
    w_small = jnp.concatenate([w_in[:, o_ff:o_gq], w_in[:, o_ga:o_gg],
                               jnp.zeros((D, LANES - FOX_HEADS - 2 * GDN_HEADS), F32)], axis=1)
    w_big, w_small = (t.astype(BF16) for t in lax.optimization_barrier((w_big, w_small)))
    z, zs = norm_matmul(h, norm1, w_big, w_small, tm=ROW_TILE, tn=IN_PROJ_COLS)
    par = jnp.zeros((8, LANES), F32)
    par = par.at[0, LANE_F:LANE_F + FOX_HEADS].set(f_bias).at[0, LANE_A:LANE_A + GDN_HEADS].set(dt_bias)
    par = par.at[1, LANE_A:LANE_A + GDN_HEADS].set(a_log)
    col, row = even_gates(zs.reshape(B, S, LANES), par)
    z = z.reshape(B, S, -1)
    fox = fox_attention(z, col, q_norm, k_norm)
    gdn = gdn_mixer(z, conv_w, col, row, o_norm)
    return [(fox.reshape(M, fw), w_out[:fw]), (gdn.reshape(M, gw), w_out[fw:])]


def _odd_mix(h, B, S, norm1, w_qkv, q_norm, k_norm, w_out):
    M, D = h.shape
    z = norm_matmul(h, norm1, w_qkv, tm=ROW_TILE, tn=QKV_COLS).reshape(B, S, -1)
    half = HEAD_DIM // 2
    inv = jnp.power(ROPE_THETA, -jnp.arange(half, dtype=F32) / half)
    ang = jnp.arange(S, dtype=F32)[:, None] * inv[None, :]
    cos, sin = jnp.cos(ang), jnp.sin(ang)
    cos_full = jnp.concatenate([cos, cos], axis=-1)
    sin_signed = jnp.concatenate([-sin, sin], axis=-1)
    att = moba_attention(z, cos_full, sin_signed, q_norm, k_norm)
    return [(att.reshape(M, -1), w_out)]


def kernel(x, e_norm1, e_w_in, e_fox_f_bias, e_fox_q_norm, e_fox_k_norm, e_gdn_conv, e_gdn_a_log,
           e_gdn_dt_bias, e_gdn_o_norm, e_w_out, e_norm2, e_ffn_w_gate_up, e_ffn_w_down,
           o_norm1, o_w_qkv, o_q_norm, o_k_norm, o_w_out, o_norm2, o_router, o_exp_w_gate_up, o_exp_w_down):
    B, S, D = x.shape
    h = x.reshape(B * S, D)
    depth = e_norm1.shape[0] + o_norm1.shape[0]
    for layer in range(depth):
        i = layer // 2
        if layer % 2 == 0:
            mix = _even_mix(h, B, S, e_norm1[i], e_w_in[i], e_fox_f_bias[i], e_fox_q_norm[i], e_fox_k_norm[i],
                            e_gdn_conv[i], e_gdn_a_log[i], e_gdn_dt_bias[i], e_gdn_o_norm[i], e_w_out[i])
            h = ffn_residual(h, mix, e_norm2[i], e_ffn_w_gate_up[i].astype(BF16), e_ffn_w_down[i].astype(BF16),
                             tm=FFN_ROWS, tf=FFN_COLS)
        else:
            mix = _odd_mix(h, B, S, o_norm1[i], o_w_qkv[i], o_q_norm[i], o_k_norm[i], o_w_out[i])
            h = moe_residual(h, mix, o_norm2[i], o_router[i], o_exp_w_gate_up[i], o_exp_w_down[i])
    return h.reshape(B, S, D)
```

```python
import functools

import jax
import jax.numpy as jnp
from jax import lax
from jax.experimental import pallas as pl
from jax.experimental.pallas import tpu as pltpu
from jax.experimental.pallas import tpu_sc as plsc

F32 = jnp.float32
BF16 = jnp.bfloat16

HEAD_DIM = 128
FOX_HEADS = 4
GDN_HEADS = 4
CONV_WIDTH = 4
MOBA_BLOCK = 256
MOBA_TOPK = 3
N_EXPERTS = 8
ROPE_THETA = 10000.0
EPS = 1e-6

LANES = 128
GDN_CHUNK = 128
INV_BLOCK = 16
GDN_HEADS_PER_STEP = 4
GDN_CHUNKS_PER_STEP = 4
NEG = -(2.0 ** 100)
LOG2E = 1.4426950408889634
ATT_BLOCK = 256
ATT_HEADS = 4
KEY_BLOCKS_PER_STEP = 2
V_ROWS = HEAD_DIM + 16
PERM_TILE = 1024
ROUTER_CHAINS = 4
SC_ROWS = 64
ROW_TILE = 1024
MXU_COLS = 256
IN_PROJ_COLS = 7 * MXU_COLS
QKV_COLS = 6 * MXU_COLS
FFN_ROWS, FFN_COLS = 512, 1408
EXPERT_TILE = 1024
EXPERT_COLS = 896
EXPERT_ROW_PARTS = 4
VMEM_LIMIT_BYTES = 56 * 1024 * 1024

FOX_Q0, FOX_K0, FOX_V0 = 0, 4, 8
GDN_Q0, GDN_K0, GDN_V0, GDN_G0 = 12, 16, 20, 24
LANE_F, LANE_A, LANE_B = 0, 4, 8


def _params(*sem):
    return pltpu.CompilerParams(dimension_semantics=sem, vmem_limit_bytes=VMEM_LIMIT_BYTES)


def _rms(x, gain):
    return x * lax.rsqrt(jnp.mean(x * x, axis=-1, keepdims=True) + EPS) * gain


def _dot_nt(a, b, **kw):
    return lax.dot_general(a, b, (((1,), (1,)), ((), ())), preferred_element_type=F32, **kw)


def _pick_lane(x, lane_idx):
    lane = lax.broadcasted_iota(jnp.int32, x.shape, 1)
    return jnp.sum(jnp.where(lane == lane_idx, x, 0.0), axis=-1, keepdims=True)


def _pack_bf16_pairs(x):
    n = x.shape[1] // 2
    hi = pltpu.bitcast(x[:, :n].astype(BF16).astype(F32), jnp.uint32)
    lo = pltpu.bitcast(x[:, n:].astype(BF16).astype(F32), jnp.uint32)
    return pltpu.bitcast(hi | (lo >> 16), jnp.int32)


def _unpack_bf16_pairs(w):
    u = pltpu.bitcast(w, jnp.uint32)
    hi = pltpu.bitcast(u & jnp.uint32(0xFFFF0000), F32).astype(BF16)
    lo = pltpu.bitcast(u << 16, F32).astype(BF16)
    return jnp.concatenate([hi, lo], axis=1)


def _norm_mm_kernel(x_ref, g_ref, w_ref, *rest, has_aux):
    if has_aux:
        waux_ref, o_ref, oaux_ref, xn_ref = rest
    else:
        o_ref, xn_ref = rest

    @pl.when(pl.program_id(1) == 0)
    def _():
        xn = _rms(x_ref[...], g_ref[...]).astype(BF16)
        xn_ref[...] = xn
        if has_aux:
            oaux_ref[...] = jnp.dot(xn, waux_ref[...], preferred_element_type=F32)

    o_ref[...] = jnp.dot(xn_ref[...], w_ref[...].astype(BF16), preferred_element_type=F32).astype(o_ref.dtype)


def norm_matmul(x, gain, w, w_aux=None, *, tm, tn, out_dtype=BF16):
    M, K = x.shape
    N = w.shape[1]
    has_aux = w_aux is not None
    in_specs = [
        pl.BlockSpec((tm, K), lambda i, j: (i, 0)),
        pl.BlockSpec((1, K), lambda i, j: (0, 0)),
        pl.BlockSpec((K, tn), lambda i, j: (0, j)),
    ]
    out_shape = [jax.ShapeDtypeStruct((M, N), out_dtype)]
    out_specs = [pl.BlockSpec((tm, tn), lambda i, j: (i, j))]
    args = [x, gain.reshape(1, K), w]
    if has_aux:
        in_specs.append(pl.BlockSpec((K, LANES), lambda i, j: (0, 0)))
        out_shape.append(jax.ShapeDtypeStruct((M, LANES), F32))
        out_specs.append(pl.BlockSpec((tm, LANES), lambda i, j: (i, 0)))
        args.append(w_aux)
    res = pl.pallas_call(
        functools.partial(_norm_mm_kernel, has_aux=has_aux),
        grid=(M // tm, N // tn),
        in_specs=in_specs,
        out_specs=out_specs,
        out_shape=out_shape,
        scratch_shapes=[pltpu.VMEM((tm, K), BF16)],
        compiler_params=_params("parallel", "arbitrary"),
        name="norm_matmul",
    )(*args)
    return res if has_aux else res[0]


def _plus_projections(x, proj_refs, rows=slice(None)):
    for a_ref, w_ref in zip(proj_refs[0::2], proj_refs[1::2]):
        x = x + jnp.dot(a_ref[rows, :], w_ref[...].astype(BF16), preferred_element_type=F32)
    return x


def _projection_specs(pairs, tm):
    specs, args = [], []
    for a, w in pairs:
        specs.append(pl.BlockSpec((tm, a.shape[1]), lambda i, *_: (i, 0)))
        specs.append(pl.BlockSpec(w.shape, lambda i, *_: (0, 0)))
        args += [a, w]
    return specs, args


def _gate_kernel(zs_ref, par_ref, col_ref, row_ref, *, S, B):
    C = GDN_CHUNK
    bias = par_ref[0:1, :]
    neg_a = -jnp.exp(par_ref[1:2, :])
    r = lax.broadcasted_iota(jnp.int32, (C, C), 0)
    c = lax.broadcasted_iota(jnp.int32, (C, C), 1)
    tril = (r >= c).astype(F32)
    lane = lax.broadcasted_iota(jnp.int32, (C, LANES), 1)

    def body(n, carry):
        sl = pl.ds(pl.multiple_of(n * C, C), C)
        us, betas = [], []
        for bi in range(B):
            z = zs_ref[bi, sl, :]
            t = z + bias
            soft = jnp.log(1.0 + jnp.exp(-jnp.abs(t)))
            log_f = jnp.minimum(t, 0.0) - soft
            g = neg_a * (jnp.maximum(t, 0.0) + soft)
            betas.append(1.0 / (1.0 + jnp.exp(-z)))
            us.append(jnp.where(lane < LANE_A, log_f, jnp.where(lane < LANE_B, g, 0.0)))
        sums = jnp.dot(tril, jnp.concatenate(us, axis=1), preferred_element_type=F32, precision=lax.Precision.HIGHEST)
        last = []
        for bi in range(B):
            cs = sums[:, bi * LANES:(bi + 1) * LANES] + jnp.where(lane < LANE_A, carry[bi], 0.0)
            out = jnp.where(lane < LANE_B, cs, betas[bi])
            col_ref[bi, sl, :] = out
            out_t = out.T
            for hd in range(GDN_HEADS):
                row_ref[bi, hd, :, sl] = out_t[LANE_A + hd:LANE_A + hd + 1, :]
            last.append(cs[C - 1:C, :])
        return tuple(last)

    lax.fori_loop(0, S // C, body, tuple(jnp.zeros((1, LANES), F32) for _ in range(B)))


def even_gates(zs, par):
    B, S, _ = zs.shape
    return pl.pallas_call(
        functools.partial(_gate_kernel, S=S, B=B),
        grid=(1,),
        in_specs=[
            pl.BlockSpec((B, S, LANES), lambda i: (0, 0, 0)),
            pl.BlockSpec((8, LANES), lambda i: (0, 0)),
        ],
        out_specs=[
            pl.BlockSpec((B, S, LANES), lambda i: (0, 0, 0)),
            pl.BlockSpec((B, GDN_HEADS, 1, S), lambda i: (0, 0, 0, 0)),
        ],
        out_shape=[jax.ShapeDtypeStruct((B, S, LANES), F32), jax.ShapeDtypeStruct((B, GDN_HEADS, 1, S), F32)],
        compiler_params=_params("arbitrary"),
        name="even_gates",
    )(zs, par)


def _split3(x):
    hi = x.astype(BF16).astype(F32)
    mid = (x - hi).astype(BF16).astype(F32)
    lo = (x - hi - mid).astype(BF16).astype(F32)
    return hi, mid, lo


def _interleave(gens):
    out = [None] * len(gens)
    live = list(range(len(gens)))
    while live:
        for n in list(live):
            try:
                next(gens[n])
            except StopIteration as stop:
                out[n] = stop.value
                live.remove(n)
    return out


def _attend(state, c, qa, ka_ref, vt_ref, key0, nkeys, keep=None):
    ks = pl.ds(pl.multiple_of(key0, ATT_BLOCK), nkeys)
    st = _dot_nt(ka_ref[ks, :], qa)
    yield
    if keep is not None:
        st = jnp.where(keep, st, NEG)
    m_new = jnp.max(st, axis=0, keepdims=True)
    if state[c] is not None:
        m_old, acc_old = state[c]
        m_new = jnp.maximum(m_old, m_new)
    p = jnp.exp2(st - m_new).astype(BF16)
    pv = jnp.dot(vt_ref[:, ks], p, preferred_element_type=F32)
    state[c] = (m_new, pv if state[c] is None else acc_old * jnp.exp2(m_old - m_new) + pv)


def _attend_tile_pair(i, prep, ka_ref, vt_ref, m_ref, acc_ref, o_ref):
    t = ATT_BLOCK
    chains = [(hd, c) for c in range(2) for hd in range(ATT_HEADS)]
    num = lambda hd, c: 2 * hd + c
    qa = _interleave(prep)

    def step(state, hd, c, which, **kw):
        return _attend(state, num(hd, c), qa[num(hd, c)][which], ka_ref.at[hd], vt_ref.at[hd], **kw)

    def save(state):
        m_ref[...] = jnp.stack([state[n][0] for n in range(len(chains))])
        acc_ref[...] = jnp.stack([state[n][1] for n in range(len(chains))])

    def past_blocks(key0, n_steps):
        state = {n: (m_ref[n], acc_ref[n]) for n in range(len(chains))}
        _interleave([step(state, hd, c, 1, key0=key0 + s * 2 * t, nkeys=2 * t)
                     for s in range(n_steps) for hd, c in chains])
        save(state)

    state = {n: None for n in range(len(chains))}
    keep = _causal_keep()
    _interleave([step(state, hd, c, 0, key0=(2 * i + c) * t, nkeys=t, keep=keep) for hd, c in chains]
                + [step(state, hd, 1, 1, key0=(2 * i) * t, nkeys=t) for hd in range(ATT_HEADS)])
    save(state)

    def four_blocks(g, _):
        past_blocks(g * (4 * t), 2)
        return 0

    lax.fori_loop(0, i // 2, four_blocks, 0)

    @pl.when((i & 1) != 0)
    def _():
        past_blocks((i // 2) * (4 * t), 1)

    for hd, c in chains:
        acc = acc_ref[num(hd, c)]
        out_t = acc[:HEAD_DIM, :] * (1.0 / acc[HEAD_DIM:HEAD_DIM + 1, :])
        o_ref[c * t:(c + 1) * t, hd * HEAD_DIM:(hd + 1) * HEAD_DIM] = out_t.T.astype(o_ref.dtype)


def _transposed(v):
    r = lax.broadcasted_iota(jnp.int32, (HEAD_DIM, HEAD_DIM), 0)
    c = lax.broadcasted_iota(jnp.int32, (HEAD_DIM, HEAD_DIM), 1)
    v_t = _dot_nt(jnp.where(r == c, 1.0, 0.0).astype(BF16), v).astype(BF16)
    return jnp.concatenate([v_t, jnp.ones((V_ROWS - HEAD_DIM, ATT_BLOCK), BF16)], axis=0)


def _causal_keep():
    key = lax.broadcasted_iota(jnp.int32, (ATT_BLOCK, ATT_BLOCK), 0)
    qry = lax.broadcasted_iota(jnp.int32, (ATT_BLOCK, ATT_BLOCK), 1)
    return key <= qry


_ATT_SCRATCH = lambda S: [pltpu.VMEM((ATT_HEADS, S, 2 * HEAD_DIM), BF16), pltpu.VMEM((ATT_HEADS, V_ROWS, S), BF16),
                          pltpu.VMEM((2 * ATT_HEADS, 1, ATT_BLOCK), F32),
                          pltpu.VMEM((2 * ATT_HEADS, V_ROWS, ATT_BLOCK), F32)]


def _fox_kernel(q_ref, k_ref, v_ref, col_ref, qg_ref, kg_ref, o_ref, ka_ref, vt_ref, m_ref, acc_ref, *, S):
    h0 = pl.program_id(1) * ATT_HEADS
    i = pl.program_id(2)
    t = ATT_BLOCK
    D = HEAD_DIM
    lane = lax.broadcasted_iota(jnp.int32, (t, LANES), 1)

    @pl.when(i == 0)
    def _():
        def prep_keys(g, _):
            todo = []
            for u in range(KEY_BLOCKS_PER_STEP):
                sl = pl.ds(pl.multiple_of((g * KEY_BLOCKS_PER_STEP + u) * t, t), t)
                for hd in range(ATT_HEADS):
                    cols = slice(hd * D, (hd + 1) * D)
                    kn = _rms(k_ref[sl, cols].astype(F32), kg_ref[...]).astype(BF16)
                    hi, mid, lo = _split3(-LOG2E * _pick_lane(col_ref[sl, :], LANE_F + h0 + hd))
                    aug = jnp.where(lane < 3, 1.0, jnp.where(lane == 3, hi, jnp.where(lane == 4, mid,
                                                                                    jnp.where(lane == 5, lo, 0.0))))
                    todo.append((hd, sl, jnp.concatenate([kn, aug.astype(BF16)], axis=1), _transposed(v_ref[sl, cols])))
            for hd, sl, ka, vt in todo:
                ka_ref[hd, sl, :] = ka
                vt_ref[hd, :, sl] = vt
            return 0

        lax.fori_loop(0, S // (t * KEY_BLOCKS_PER_STEP), prep_keys, 0)

    def prep(hd, c):
        q = q_ref[c * t:(c + 1) * t, hd * D:(hd + 1) * D].astype(F32)
        qn = (_rms(q, qg_ref[...]) * (LOG2E * D ** -0.5)).astype(BF16)
        yield
        qsl = pl.ds(pl.multiple_of((2 * i + c) * t, t), t)
        hi, mid, lo = _split3(LOG2E * _pick_lane(col_ref[qsl, :], LANE_F + h0 + hd))
        aug = jnp.where(lane == 0, hi,
                        jnp.where(lane == 1, mid, jnp.where(lane == 2, lo, jnp.where(lane < 6, 1.0, 0.0))))
        qa = jnp.concatenate([qn, aug.astype(BF16)], axis=-1)
        return qa, qa

    _attend_tile_pair(i, [prep(hd, c) for hd in range(ATT_HEADS) for c in range(2)],
                      ka_ref, vt_ref, m_ref, acc_ref, o_ref)


def fox_attention(z, col, q_gain, k_gain):
    B, S, _ = z.shape
    t = 2 * ATT_BLOCK
    w = ATT_HEADS * HEAD_DIM
    return pl.pallas_call(
        functools.partial(_fox_kernel, S=S),
        grid=(B, FOX_HEADS // ATT_HEADS, S // t),
        in_specs=[
            pl.BlockSpec((None, t, w), lambda b, h, i: (b, i, FOX_Q0 // ATT_HEADS + h)),
            pl.BlockSpec((None, S, w), lambda b, h, i: (b, 0, FOX_K0 // ATT_HEADS + h)),
            pl.BlockSpec((None, S, w), lambda b, h, i: (b, 0, FOX_V0 // ATT_HEADS + h)),
            pl.BlockSpec((None, S, LANES), lambda b, h, i: (b, 0, 0)),
            pl.BlockSpec((1, HEAD_DIM), lambda b, h, i: (0, 0)),
            pl.BlockSpec((1, HEAD_DIM), lambda b, h, i: (0, 0)),
        ],
        out_specs=pl.BlockSpec((None, t, w), lambda b, h, i: (b, i, h)),
        out_shape=jax.ShapeDtypeStruct((B, S, FOX_HEADS * HEAD_DIM), BF16),
        scratch_shapes=_ATT_SCRATCH(S),
        compiler_params=_params("parallel", "parallel", "arbitrary"),
        name="fox_attention",
    )(z, z, z, col, q_gain.reshape(1, -1), k_gain.reshape(1, -1))


def _unit_lower_inverse(m):
    n = m.shape[0]
    r = lax.broadcasted_iota(jnp.int32, (n, n), 0)
    c = lax.broadcasted_iota(jnp.int32, (n, n), 1)
    eye = (r == c).astype(F32)

    def same_block(b):
        return (r // b) == (c // b)

    p = jnp.where(same_block(INV_BLOCK), m, 0.0)
    inv = eye - p
    k = 2
    while k < INV_BLOCK:
        pb = p.astype(BF16)
        p = jnp.dot(pb, pb, preferred_element_type=F32)
        yield
        inv = jnp.dot(inv.astype(BF16), (eye + p).astype(BF16), preferred_element_type=F32)
        yield
        k *= 2
    b = INV_BLOCK
    while b < n:
        off = jnp.where(same_block(2 * b), jnp.where(same_block(b), 0.0, m), 0.0).astype(BF16)
        ib = inv.astype(BF16)
        left = jnp.dot(ib, off, preferred_element_type=F32).astype(BF16)
        yield
        inv = inv - jnp.dot(left, ib, preferred_element_type=F32)
        yield
        b *= 2
    return inv


def _gdn_kernel(q_ref, k_ref, v_ref, gg_ref, wq_ref, wk_ref, wv_ref, col_ref, row_ref, on_ref,
                o_ref, qs_ref, ks_ref, vs_ref, *, S, rows, hb):
    h0 = pl.program_id(1) * hb
    C = GDN_CHUNK
    D = HEAD_DIM

    streams = [(x_ref, w_ref, dst_ref, hh, mode) for hh in range(hb)
               for x_ref, w_ref, dst_ref, mode in ((q_ref, wq_ref, qs_ref, "q"), (k_ref, wk_ref, ks_ref, "k"),
                                                   (v_ref, wv_ref, vs_ref, "v"))]
    halo = 16

    def conv_step(base, windows):
        outs = []
        for (x_ref, w_ref, dst_ref, hh, mode), win in zip(streams, windows):
            w = w_ref[:, hh * D:(hh + 1) * D]
            y = jnp.zeros((rows, D), F32)
            for tap in range(CONV_WIDTH):
                lead = halo - (CONV_WIDTH - 1) + tap
                y = y + w[tap:tap + 1, :] * pltpu.roll(win, rows + halo - lead, 0)[0:rows, :]
            y = y / (1.0 + jnp.exp(-y))
            if mode != "v":
                y = y * lax.rsqrt(jnp.sum(y * y, axis=-1, keepdims=True) + EPS)
            if mode == "q":
                y = y * D ** -0.5
            outs.append(y)
        for (x_ref, w_ref, dst_ref, hh, mode), y in zip(streams, outs):
            dst_ref[hh, pl.ds(base, rows), :] = y.astype(dst_ref.dtype)

    conv_step(0, [jnp.concatenate([jnp.zeros((halo, D), F32), x_ref[0:rows, hh * D:(hh + 1) * D].astype(F32)], axis=0)
                  for x_ref, _, _, hh, _ in streams])

    def conv_rest(n, _):
        base = pl.multiple_of(n * rows, rows)
        conv_step(base, [x_ref[pl.ds(pl.multiple_of(base - halo, halo), rows + halo), hh * D:(hh + 1) * D].astype(F32)
                         for x_ref, _, _, hh, _ in streams])
        return 0

    lax.fori_loop(1, S // rows, conv_rest, 0)

    r = lax.broadcasted_iota(jnp.int32, (C, C), 0)
    c = lax.broadcasted_iota(jnp.int32, (C, C), 1)
    incl = r >= c
    strict = r > c

    def chunk_local(hh, sl, tab):
        q = qs_ref[hh, sl, :].astype(F32)
        k = ks_ref[hh, sl, :].astype(F32)
        v = vs_ref[hh, sl, :].astype(F32)
        gcol = _pick_lane(tab, LANE_A + h0 + hh)
        beta = _pick_lane(tab, LANE_B + h0 + hh)
        grow = row_ref[hh, :, sl]
        glast = gcol[C - 1:C, :]
        decay = jnp.where(incl, jnp.exp(jnp.where(incl, gcol - grow, 0.0)), 0.0)
        eg = jnp.exp(gcol)
        kb = k * beta
        kbf = k.astype(BF16)
        m = jnp.where(strict, _dot_nt(kb.astype(BF16), kbf) * decay, 0.0)
        attn = (_dot_nt(q.astype(BF16), kbf) * decay).astype(BF16)
        yield
        tinv = (yield from _unit_lower_inverse(m)).astype(BF16)
        rhs = jnp.concatenate([v * beta, kb * eg], axis=-1).astype(BF16)
        sol = jnp.dot(tinv, rhs, preferred_element_type=F32)
        yield
        gate = gg_ref[sl, hh * D:(hh + 1) * D].astype(F32)
        return dict(u=sol[:, :D], w=sol[:, D:].astype(BF16), attn=attn, qg=(q * eg).astype(BF16),
                    kg_t=(k * jnp.exp(glast - gcol)).T.astype(BF16), keep=jnp.exp(glast),
                    gate=gate / (1.0 + jnp.exp(-gate)))

    def chunk_state(parts, state):
        outs = []
        for c in parts:
            sb = state.astype(BF16)
            v_new = c["u"] - jnp.dot(c["w"], sb, preferred_element_type=F32)
            o_state = jnp.dot(c["qg"], sb, preferred_element_type=F32)
            yield
            vb = v_new.astype(BF16)
            o = o_state + jnp.dot(c["attn"], vb, preferred_element_type=F32)
            state = state * c["keep"] + jnp.dot(c["kg_t"], vb, preferred_element_type=F32)
            yield
            outs.append((_rms(o, on_ref[...]) * c["gate"]).astype(o_ref.dtype))
        return state, outs

    def chunk_group(n, states):
        sls = [pl.ds(pl.multiple_of((n * GDN_CHUNKS_PER_STEP + g) * C, C), C) for g in range(GDN_CHUNKS_PER_STEP)]
        tabs = [col_ref[sl, :] for sl in sls]
        parts = _interleave([chunk_local(hh, sls[g], tabs[g]) for g in range(GDN_CHUNKS_PER_STEP) for hh in range(hb)])
        res = _interleave([chunk_state([parts[g * hb + hh] for g in range(GDN_CHUNKS_PER_STEP)], states[hh])
                           for hh in range(hb)])
        for g in range(GDN_CHUNKS_PER_STEP):
            o_ref[sls[g], :] = jnp.concatenate([outs[g] for _, outs in res], axis=-1)
        return tuple(s for s, _ in res)

    lax.fori_loop(0, S // (C * GDN_CHUNKS_PER_STEP), chunk_group, tuple(jnp.zeros((D, D), F32) for _ in range(hb)))


def gdn_mixer(z, conv_w, col, row, o_gain):
    B, S, _ = z.shape
    D = HEAD_DIM
    hb = GDN_HEADS_PER_STEP
    rows = 256
    seq = lambda off: pl.BlockSpec((None, S, hb * D), lambda b, h: (b, 0, off // hb + h),
                                   pipeline_mode=pl.Buffered(1))
    cw = lambda off: pl.BlockSpec((CONV_WIDTH, hb * D), lambda b, h: (0, off // hb + h))
    return pl.pallas_call(
        functools.partial(_gdn_kernel, S=S, rows=rows, hb=hb),
        grid=(B, GDN_HEADS // hb),
        in_specs=[
            seq(GDN_Q0), seq(GDN_K0), seq(GDN_V0), seq(GDN_G0),
            cw(0), cw(GDN_HEADS), cw(2 * GDN_HEADS),
            pl.BlockSpec((None, S, LANES), lambda b, h: (b, 0, 0)),
            pl.BlockSpec((None, hb, 1, S), lambda b, h: (b, h, 0, 0)),
            pl.BlockSpec((1, D), lambda b, h: (0, 0)),
        ],
        out_specs=pl.BlockSpec((None, S, hb * D), lambda b, h: (b, 0, h)),
        out_shape=jax.ShapeDtypeStruct((B, S, GDN_HEADS * D), BF16),
        scratch_shapes=[pltpu.VMEM((hb, S, D), BF16)] * 3,
        compiler_params=_params("parallel", "parallel"),
        name="gdn_mixer",
    )(z, z, z, z, conv_w, conv_w, conv_w, col, row, o_gain.reshape(1, D))


def _rope(x, cos, sin_signed):
    return x * cos + pltpu.roll(x, HEAD_DIM // 2, 1) * sin_signed


def _moba_kernel(q_ref, k_ref, v_ref, cos_ref, sin_ref, qg_ref, kg_ref, o_ref,
                 ka_ref, vt_ref, m_ref, acc_ref, kmean_ref, *, S):
    i = pl.program_id(2)
    t = ATT_BLOCK
    D = HEAD_DIM
    lane = lax.broadcasted_iota(jnp.int32, (t, LANES), 1)

    @pl.when(i == 0)
    def _():
        kmean_ref[...] = jnp.zeros(kmean_ref.shape, F32)

        def prep_keys(g, _):
            todo = []
            for u in range(KEY_BLOCKS_PER_STEP):
                n = g * KEY_BLOCKS_PER_STEP + u
                sl = pl.ds(pl.multiple_of(n * t, t), t)
                onehot = jnp.where(lane == n, 1.0, 0.0).astype(BF16)
                for hd in range(ATT_HEADS):
                    cols = slice(hd * D, (hd + 1) * D)
                    k = _rope(_rms(k_ref[sl, cols].astype(F32), kg_ref[...]), cos_ref[sl, :], sin_ref[sl, :])
                    todo.append((hd, n, sl, jnp.concatenate([k.astype(BF16), onehot], axis=1),
                                 jnp.mean(k, axis=0, keepdims=True), _transposed(v_ref[sl, cols])))
            for hd, n, sl, ka, kmean, vt in todo:
                ka_ref[hd, sl, :] = ka
                kmean_ref[hd, pl.ds(n, 1), :] = kmean
                vt_ref[hd, :, sl] = vt
            return 0

        lax.fori_loop(0, S // (t * KEY_BLOCKS_PER_STEP), prep_keys, 0)

    nb = -(-(S // t) // 8) * 8

    def prep(hd, c):
        cur = 2 * i + c
        qsl = pl.ds(pl.multiple_of(cur * t, t), t)
        q = _rope(_rms(q_ref[c * t:(c + 1) * t, hd * D:(hd + 1) * D].astype(F32), qg_ref[...]),
                  cos_ref[qsl, :], sin_ref[qsl, :])
        yield
        kmean = kmean_ref[hd]
        km_hi = kmean.astype(BF16)
        km_split = jnp.concatenate([km_hi, (kmean - km_hi.astype(F32)).astype(BF16)], axis=0)
        q_hi = q.astype(BF16)
        q_lo = (q - q_hi.astype(F32)).astype(BF16)
        part = _dot_nt(km_split, q_hi)
        gate = (part[:LANES] + part[LANES:] + _dot_nt(km_split[:LANES, :], q_lo))[0:nb]
        yield
        blk = lax.broadcasted_iota(jnp.int32, (nb, t), 0)
        blk_f = blk.astype(F32)
        gate = jnp.where(blk < cur, gate, -jnp.inf)
        sel = jnp.full((nb, t), NEG, F32)
        for _ in range(MOBA_TOPK):
            top = jnp.max(gate, axis=0, keepdims=True)
            first = jnp.min(jnp.where(gate == top, blk_f, float(LANES)), axis=0, keepdims=True)
            pick = blk_f == first
            sel = jnp.where(pick & (first < cur.astype(F32)), 0.0, sel)
            gate = jnp.where(pick, -jnp.inf, gate)
        sel_bias = jnp.concatenate([sel, jnp.full((LANES - nb, t), NEG, F32)], axis=0).T
        qs = (q * (LOG2E * D ** -0.5)).astype(BF16)
        return (jnp.concatenate([qs, jnp.zeros((t, LANES), BF16)], axis=-1),
                jnp.concatenate([qs, sel_bias.astype(BF16)], axis=-1))

    _attend_tile_pair(i, [prep(hd, c) for hd in range(ATT_HEADS) for c in range(2)],
                      ka_ref, vt_ref, m_ref, acc_ref, o_ref)


def moba_attention(z, cos, sin_signed, q_gain, k_gain):
    B, S, W = z.shape
    H = W // (3 * HEAD_DIM)
    assert ATT_BLOCK == MOBA_BLOCK
    t = 2 * ATT_BLOCK
    w = ATT_HEADS * HEAD_DIM
    hp = H // ATT_HEADS
    return pl.pallas_call(
        functools.partial(_moba_kernel, S=S),
        grid=(B, hp, S // t),
        in_specs=[
            pl.BlockSpec((None, t, w), lambda b, h, i: (b, i, h)),
            pl.BlockSpec((None, S, w), lambda b, h, i: (b, 0, hp + h)),
            pl.BlockSpec((None, S, w), lambda b, h, i: (b, 0, 2 * hp + h)),
            pl.BlockSpec((S, HEAD_DIM), lambda b, h, i: (0, 0)),
            pl.BlockSpec((S, HEAD_DIM), lambda b, h, i: (0, 0)),
            pl.BlockSpec((1, HEAD_DIM), lambda b, h, i: (0, 0)),
            pl.BlockSpec((1, HEAD_DIM), lambda b, h, i: (0, 0)),
        ],
        out_specs=pl.BlockSpec((None, t, w), lambda b, h, i: (b, i, h)),
        out_shape=jax.ShapeDtypeStruct((B, S, H * HEAD_DIM), BF16),
        scratch_shapes=_ATT_SCRATCH(S) + [pltpu.VMEM((ATT_HEADS, LANES, HEAD_DIM), F32)],
        compiler_params=_params("parallel", "parallel", "arbitrary"),
        name="moba_attention",
    )(z, z, z, cos, sin_signed, q_gain.reshape(1, -1), k_gain.reshape(1, -1))


META_IDX, META_RANK, META_GATE = 0, 2, 4


def _router_kernel(x_ref, *refs, n_proj):
    proj_refs = refs[:2 * n_proj]
    g_ref, w_ref, y_ref, xn_ref, meta_ref, cnt_ref, carry_ref = refs[2 * n_proj:]
    i = pl.program_id(0)
    tm = x_ref.shape[0]

    @pl.when(i == 0)
    def _():
        carry_ref[...] = jnp.zeros(carry_ref.shape, F32)

    w = w_ref[...]
    w_hi = w.astype(BF16)
    w_lo = (w - w_hi.astype(F32)).astype(BF16)
    part = tm // ROUTER_CHAINS
    lane = lax.broadcasted_iota(jnp.int32, (part, LANES), 1)
    lane_f = lane.astype(F32)

    def route(n):
        rows = slice(n * part, (n + 1) * part)
        y = _plus_projections(x_ref[rows, :], proj_refs, rows)
        y_ref[rows, :] = y
        yield
        xn = _rms(y, g_ref[...])
        xn_ref[rows, :] = _pack_bf16_pairs(xn)
        x_hi = xn.astype(BF16)
        x_lo = (xn - x_hi.astype(F32)).astype(BF16)
        logits = (jnp.dot(x_hi, w_hi, preferred_element_type=F32) + jnp.dot(x_hi, w_lo, preferred_element_type=F32)
                  + jnp.dot(x_lo, w_hi, preferred_element_type=F32))
        yield
        logits = jnp.where(lane < N_EXPERTS, logits, -jnp.inf)
        top1 = jnp.max(logits, axis=-1, keepdims=True)
        yield
        idx1 = jnp.min(jnp.where(logits == top1, lane_f, float(LANES)), axis=-1, keepdims=True)
        yield
        rest = jnp.where(lane_f == idx1, -jnp.inf, logits)
        top2 = jnp.max(rest, axis=-1, keepdims=True)
        yield
        idx2 = jnp.min(jnp.where(rest == top2, lane_f, float(LANES)), axis=-1, keepdims=True)
        e2 = jnp.exp(top2 - top1)
        denom = 1.0 + e2
        return idx1, idx2, 1.0 / denom, e2 / denom, jnp.where((lane_f == idx1) | (lane_f == idx2), 1.0, 0.0)

    picks = _interleave([route(n) for n in range(ROUTER_CHAINS)])
    chosen = jnp.concatenate([p[4] for p in picks], axis=0)
    r = lax.broadcasted_iota(jnp.int32, (tm, tm), 0)
    c = lax.broadcasted_iota(jnp.int32, (tm, tm), 1)
    ahead = jnp.dot(jnp.where(r > c, 1.0, 0.0).astype(BF16), chosen.astype(BF16), preferred_element_type=F32)
    carry = carry_ref[...]
    for n, (idx1, idx2, g1, g2, _) in enumerate(picks):
        rows = slice(n * part, (n + 1) * part)
        rank = ahead[rows, :] + carry
        rank1 = jnp.sum(jnp.where(lane_f == idx1, rank, 0.0), axis=-1, keepdims=True)
        rank2 = jnp.sum(jnp.where(lane_f == idx2, rank, 0.0), axis=-1, keepdims=True)
        meta = jnp.zeros((part, LANES), F32)
        for k, v in enumerate((idx1, idx2, rank1, rank2, g1, g2)):
            meta = jnp.where(lane == k, v, meta)
        meta_ref[rows, :] = meta
    carry = carry + jnp.sum(chosen, axis=0, keepdims=True)
    carry_ref[...] = carry
    cnt_ref[...] = jnp.broadcast_to(carry, cnt_ref.shape)


def moe_router(x, proj, gain, w_router, *, tm):
    M, K = x.shape
    w = jnp.zeros((K, LANES), F32).at[:, :N_EXPERTS].set(w_router)
    proj_specs, proj_args = _projection_specs(proj, tm)
    return pl.pallas_call(
        functools.partial(_router_kernel, n_proj=len(proj)),
        grid=(M // tm,),
        in_specs=[
            pl.BlockSpec((tm, K), lambda i: (i, 0)),
            *proj_specs,
            pl.BlockSpec((1, K), lambda i: (0, 0)),
            pl.BlockSpec((K, LANES), lambda i: (0, 0)),
        ],
        out_specs=[
            pl.BlockSpec((tm, K), lambda i: (i, 0)),
            pl.BlockSpec((tm, K // 2), lambda i: (i, 0)),
            pl.BlockSpec((tm, LANES), lambda i: (i, 0)),
            pl.BlockSpec((8, LANES), lambda i: (0, 0)),
        ],
        out_shape=[
            jax.ShapeDtypeStruct((M, K), F32),
            jax.ShapeDtypeStruct((M, K // 2), jnp.int32),
            jax.ShapeDtypeStruct((M, LANES), F32),
            jax.ShapeDtypeStruct((8, LANES), F32),
        ],
        scratch_shapes=[pltpu.VMEM((1, LANES), F32)],
        compiler_params=_params("arbitrary"),
        name="moe_router",
    )(x, *proj_args, gain.reshape(1, K), w)


def _sc_workers():
    info = plsc.get_sparse_core_info()
    return info.num_cores, info.num_cores * info.num_subcores


def scatter_rows(rows, dest, n_out):
    M, W = rows.shape
    nc, nw = _sc_workers()
    assert M % (nw * SC_ROWS) == 0
    per_w = M // nw
    mesh = plsc.VectorSubcoreMesh(core_axis_name="c", subcore_axis_name="s")

    n_chunks = per_w // SC_ROWS
    assert n_chunks % 2 == 0
    buf = [pltpu.VMEM((SC_ROWS, W), rows.dtype), pltpu.VMEM((SC_ROWS,), jnp.int32), pltpu.VMEM((SC_ROWS,), jnp.int32),
           pltpu.SemaphoreType.DMA]

    @functools.partial(pl.kernel, mesh=mesh, out_type=jax.ShapeDtypeStruct((n_out, W), rows.dtype),
                       scratch_types=buf + buf)
    def kern(rows_hbm, dest_hbm, out_hbm, rows0, ia0, ib0, sem0, rows1, ia1, ib1, sem1):
        wid = lax.axis_index("s") * nc + lax.axis_index("c")
        slots = ((rows0, (ia0, ib0), sem0), (rows1, (ia1, ib1), sem1))

        def send(g, slot):
            rows_v, idx, sem = slot
            base = wid * per_w + g * SC_ROWS
            pltpu.sync_copy(rows_hbm.at[pl.ds(base, SC_ROWS)], rows_v)
            for k in range(2):
                pltpu.sync_copy(dest_hbm.at[k, pl.ds(base, SC_ROWS)], idx[k])
                pltpu.async_copy(rows_v, out_hbm.at[idx[k]], sem)

        def drain(slot):
            rows_v, idx, sem = slot
            for k in range(2):
                pltpu.make_async_copy(rows_v, out_hbm.at[idx[k]], sem).wait()

        send(0, slots[0])

        @pl.loop(0, n_chunks, step=2)
        def _(g):
            send(g + 1, slots[1])
            drain(slots[0])

            @pl.when(g + 2 < n_chunks)
            def _():
                send(g + 2, slots[0])

            drain(slots[1])

    return kern(rows, dest)


def gather_rows(table, idx):
    N = idx.shape[0]
    W = table.shape[1]
    nc, nw = _sc_workers()
    assert N % (nw * SC_ROWS) == 0
    per_w = N // nw
    mesh = plsc.VectorSubcoreMesh(core_axis_name="c", subcore_axis_name="s")

    n_chunks = per_w // SC_ROWS
    assert n_chunks % 2 == 0
    buf = [pltpu.VMEM((SC_ROWS,), jnp.int32), pltpu.VMEM((SC_ROWS, W), table.dtype), pltpu.SemaphoreType.DMA]

    @functools.partial(pl.kernel, mesh=mesh, out_type=jax.ShapeDtypeStruct((N, W), table.dtype),
                       scratch_types=buf + buf)
    def kern(table_hbm, idx_hbm, out_hbm, idx0, rows0, sem0, idx1, rows1, sem1):
        wid = lax.axis_index("s") * nc + lax.axis_index("c")
        slots = ((idx0, rows0, sem0), (idx1, rows1, sem1))

        def fetch(g, slot):
            idx_v, rows_v, sem = slot
            pltpu.sync_copy(idx_hbm.at[pl.ds(wid * per_w + g * SC_ROWS, SC_ROWS)], idx_v)
            pltpu.async_copy(table_hbm.at[idx_v], rows_v, sem)

        def drain(g, slot):
            idx_v, rows_v, sem = slot
            pltpu.make_async_copy(table_hbm.at[idx_v], rows_v, sem).wait()
            pltpu.sync_copy(rows_v, out_hbm.at[pl.ds(wid * per_w + g * SC_ROWS, SC_ROWS)])

        fetch(0, slots[0])

        @pl.loop(0, n_chunks, step=2)
        def _(g):
            fetch(g + 1, slots[1])
            drain(g, slots[0])

            @pl.when(g + 2 < n_chunks)
            def _():
                fetch(g + 2, slots[0])

            drain(g + 1, slots[1])

    return kern(table, idx)


def _combine_kernel(h_ref, y1_ref, y2_ref, meta_ref, o_ref):
    g1 = meta_ref[:, META_GATE:META_GATE + 1]
    g2 = meta_ref[:, META_GATE + 1:META_GATE + 2]
    o_ref[...] = (h_ref[...] + g1 * _unpack_bf16_pairs(y1_ref[...]).astype(F32)
                  + g2 * _unpack_bf16_pairs(y2_ref[...]).astype(F32))


def moe_combine(h, y_pairs, meta):
    M, K = h.shape
    p = PERM_TILE
    nt = M // p
    return pl.pallas_call(
        _combine_kernel,
        grid=(nt,),
        in_specs=[
            pl.BlockSpec((p, K), lambda i: (i, 0)),
            pl.BlockSpec((p, K // 2), lambda i: (i, 0)),
            pl.BlockSpec((p, K // 2), lambda i: (nt + i, 0)),
            pl.BlockSpec((p, LANES), lambda i: (i, 0)),
        ],
        out_specs=pl.BlockSpec((p, K), lambda i: (i, 0)),
        out_shape=jax.ShapeDtypeStruct((M, K), F32),
        compiler_params=_params("parallel"),
        name="moe_combine",
    )(h, y_pairs, y_pairs, meta)


def _expert_ffn_kernel(te_ref, tv_ref, x_ref, wg_ref, wu_ref, wd_ref, o_ref, acc_ref, *, nf):
    del te_ref
    i = pl.program_id(0)
    j = pl.program_id(1)
    tm = x_ref.shape[0]
    valid = tv_ref[i]

    def swiglu_rows(rows):
        x = _unpack_bf16_pairs(x_ref[0:rows, :])
        gt = jnp.dot(x, wg_ref[...].astype(BF16), preferred_element_type=F32)
        up = jnp.dot(x, wu_ref[...].astype(BF16), preferred_element_type=F32)
        act = (gt / (1.0 + jnp.exp(-gt)) * up).astype(BF16)
        part = jnp.dot(act, wd_ref[...].astype(BF16), preferred_element_type=F32)

        def finish(total):
            o_ref[0:rows, :] = _pack_bf16_pairs(total)
            if rows < tm:
                o_ref[rows:tm, :] = jnp.zeros((tm - rows, o_ref.shape[1]), o_ref.dtype)

        if nf == 1:
            finish(part)
            return

        @pl.when(j == 0)
        def _():
            acc_ref[0:rows, :] = part

        @pl.when((j > 0) & (j < nf - 1))
        def _():
            acc_ref[0:rows, :] += part

        @pl.when(j == nf - 1)
        def _():
            finish(acc_ref[0:rows, :] + part)

    part = tm // EXPERT_ROW_PARTS
    for n in range(1, EXPERT_ROW_PARTS + 1):
        pl.when((valid > (n - 1) * part) & (valid <= n * part))(functools.partial(swiglu_rows, n * part))

    @pl.when((valid == 0) & (j == 0))
    def _():
        o_ref[...] = jnp.zeros(o_ref.shape, o_ref.dtype)


def expert_ffn(xs, tile_expert, tile_valid, w_gate_up, w_down, *, tm, tf):
    R = xs.shape[0]
    E, F, K = w_down.shape
    nf = F // tf
    live = lambda i, tv: tv[i] > 0
    col = lambda i, j, tv: jnp.where(live(i, tv), j, nf - 1)
    grid_spec = pltpu.PrefetchScalarGridSpec(
        num_scalar_prefetch=2,
        grid=(R // tm, nf),
        in_specs=[
            pl.BlockSpec((tm, K // 2), lambda i, j, te, tv: (te[R // tm + i], 0)),
            pl.BlockSpec((None, K, tf), lambda i, j, te, tv: (te[i], 0, col(i, j, tv))),
            pl.BlockSpec((None, K, tf), lambda i, j, te, tv: (te[i], 0, nf + col(i, j, tv))),
            pl.BlockSpec((None, tf, K), lambda i, j, te, tv: (te[i], col(i, j, tv), 0)),
        ],
        out_specs=pl.BlockSpec((tm, K // 2), lambda i, j, te, tv: (i, 0)),
        scratch_shapes=[pltpu.VMEM((tm, K), F32)],
    )
    return pl.pallas_call(
        functools.partial(_expert_ffn_kernel, nf=nf),
        grid_spec=grid_spec,
        out_shape=jax.ShapeDtypeStruct((R, K // 2), jnp.int32),
        compiler_params=_params("arbitrary", "arbitrary"),
        name="moe_expert_ffn",
    )(tile_expert, tile_valid, xs, w_gate_up, w_gate_up, w_down)


def moe_residual(h, proj, gain, w_router, w_gate_up, w_down):
    M, K = h.shape
    p = PERM_TILE
    tm = EXPERT_TILE
    n_rows = 2 * M + N_EXPERTS * tm
    h, xn, meta, cnt = moe_router(h, proj, gain, w_router, tm=p)

    counts = cnt[0, :N_EXPERTS].astype(jnp.int32)
    padded = (counts + tm - 1) // tm * tm
    ends = jnp.cumsum(padded)
    offsets = ends - padded
    n_tiles = n_rows // tm
    tile_row = jnp.arange(n_tiles) * tm
    n_used = ends[-1] // tm
    last_used = jnp.minimum(tile_row // tm, n_used - 1)
    expert_of = lambda row: jnp.minimum(jnp.sum(ends[None, :] <= row[:, None], axis=1), N_EXPERTS - 1)
    tile_expert = expert_of(last_used * tm)
    row_in_expert = tile_row - jnp.sum(jnp.where(tile_expert[:, None] == jnp.arange(N_EXPERTS), offsets, 0), axis=1)
    own_count = jnp.sum(jnp.where(tile_expert[:, None] == jnp.arange(N_EXPERTS), counts, 0), axis=1)
    tile_valid = jnp.where(tile_row // tm < n_used, jnp.clip(own_count - row_in_expert, 0, tm), 0)
    tile_tables = jnp.concatenate([tile_expert, last_used]).astype(jnp.int32)

    idx = meta[:, META_IDX:META_IDX + 2].astype(jnp.int32)
    rank = meta[:, META_RANK:META_RANK + 2].astype(jnp.int32)
    dest = (jnp.sum(jnp.where(idx[:, :, None] == jnp.arange(N_EXPERTS), offsets, 0), axis=-1) + rank).T

    xs = scatter_rows(xn, dest, n_rows)
    ys = expert_ffn(xs, tile_tables, tile_valid.astype(jnp.int32), w_gate_up, w_down, tm=tm, tf=EXPERT_COLS)
    return moe_combine(h, gather_rows(ys, dest.reshape(-1)), meta)


def _ffn_kernel(x_ref, *refs, n_proj):
    proj_refs = refs[:2 * n_proj]
    g_ref, wg_ref, wu_ref, wd_ref, o_ref, xn_ref, acc_ref = refs[2 * n_proj:]
    j = pl.program_id(1)

    @pl.when(j == 0)
    def _():
        x = _plus_projections(x_ref[...], proj_refs)
        xn_ref[...] = _rms(x, g_ref[...]).astype(BF16)
        acc_ref[...] = x

    xn = xn_ref[...]
    gt = jnp.dot(xn, wg_ref[...], preferred_element_type=F32)
    up = jnp.dot(xn, wu_ref[...], preferred_element_type=F32)
    act = gt / (1.0 + jnp.exp(-gt)) * up
    acc_ref[...] += jnp.dot(act.astype(BF16), wd_ref[...], preferred_element_type=F32)

    @pl.when(j == pl.num_programs(1) - 1)
    def _():
        o_ref[...] = acc_ref[...]


def ffn_residual(x, proj, gain, w_gate_up, w_down, *, tm, tf):
    M, K = x.shape
    F = w_down.shape[0]
    nf = F // tf
    proj_specs, proj_args = _projection_specs(proj, tm)
    return pl.pallas_call(
        functools.partial(_ffn_kernel, n_proj=len(proj)),
        grid=(M // tm, nf),
        in_specs=[
            pl.BlockSpec((tm, K), lambda i, j: (i, 0)),
            *proj_specs,
            pl.BlockSpec((1, K), lambda i, j: (0, 0)),
            pl.BlockSpec((K, tf), lambda i, j: (0, j)),
            pl.BlockSpec((K, tf), lambda i, j: (0, nf + j)),
            pl.BlockSpec((tf, K), lambda i, j: (j, 0)),
        ],
        out_specs=pl.BlockSpec((tm, K), lambda i, j: (i, 0)),
        out_shape=jax.ShapeDtypeStruct((M, K), F32),
        scratch_shapes=[pltpu.VMEM((tm, K), BF16), pltpu.VMEM((tm, K), F32)],
        compiler_params=_params("parallel", "arbitrary"),
        name="ffn_residual",
    )(x, *proj_args, gain.reshape(1, K), w_gate_up, w_gate_up, w_down)


def _even_mix(h, B, S, norm1, w_in, f_bias, q_norm, k_norm, conv_w, a_log, dt_bias, o_norm, w_out):
    M, D = h.shape
    fw, gw = FOX_HEADS * HEAD_DIM, GDN_HEADS * HEAD_DIM
    o_ff = 3 * fw
    o_gq = o_ff + FOX_HEADS
    o_ga = o_gq + 3 * gw
    o_gb = o_ga + GDN_HEADS
    o_gg = o_gb + GDN_HEADS
    w_big = jnp.concatenate([w_in[:, :o_ff], w_in[:, o_gq:o_ga], w_in[:, o_gg:]], axis=1)
    w_small = jnp.concatenate([w_in[:, o_ff:o_gq], w_in[:, o_ga:o_gg],
                               jnp.zeros((D, LANES - FOX_HEADS - 2 * GDN_HEADS), F32)], axis=1)
    w_big, w_small = (t.astype(BF16) for t in lax.optimization_barrier((w_big, w_small)))
    z, zs = norm_matmul(h, norm1, w_big, w_small, tm=ROW_TILE, tn=IN_PROJ_COLS)
    par = jnp.zeros((8, LANES), F32)
    par = par.at[0, LANE_F:LANE_F + FOX_HEADS].set(f_bias).at[0, LANE_A:LANE_A + GDN_HEADS].set(dt_bias)
    par = par.at[1, LANE_A:LANE_A + GDN_HEADS].set(a_log)
    col, row = even_gates(zs.reshape(B, S, LANES), par)
    z = z.reshape(B, S, -1)
    fox = fox_attention(z, col, q_norm, k_norm)
    gdn = gdn_mixer(z, conv_w, col, row, o_norm)
    return [(fox.reshape(M, fw), w_out[:fw]), (gdn.reshape(M, gw), w_out[fw:])]


def _odd_mix(h, B, S, norm1, w_qkv, q_norm, k_norm, w_out):
    M, D = h.shape
    z = norm_matmul(h, norm1, w_qkv, tm=ROW_TILE, tn=QKV_COLS).reshape(B, S, -1)
    half = HEAD_DIM // 2
    inv = jnp.power(ROPE_THETA, -jnp.arange(half, dtype=F32) / half)
    ang = jnp.arange(S, dtype=F32)[:, None] * inv[None, :]
    cos, sin = jnp.cos(ang), jnp.sin(ang)
    cos_full = jnp.concatenate([cos, cos], axis=-1)
    sin_signed = jnp.concatenate([-sin, sin], axis=-1)
    att = moba_attention(z, cos_full, sin_signed, q_norm, k_norm)
    return [(att.reshape(M, -1), w_out)]


def kernel(x, e_norm1, e_w_in, e_fox_f_bias, e_fox_q_norm, e_fox_k_norm, e_gdn_conv, e_gdn_a_log,
           e_gdn_dt_bias, e_gdn_o_norm, e_w_out, e_norm2, e_ffn_w_gate_up, e_ffn_w_down,
           o_norm1, o_w_qkv, o_q_norm, o_k_norm, o_w_out, o_norm2, o_router, o_exp_w_gate_up, o_exp_w_down):
    B, S, D = x.shape
    h = x.reshape(B * S, D)
    depth = e_norm1.shape[0] + o_norm1.shape[0]
    for layer in range(depth):
        i = layer // 2
        if layer % 2 == 0:
            mix = _even_mix(h, B, S, e_norm1[i], e_w_in[i], e_fox_f_bias[i], e_fox_q_norm[i], e_fox_k_norm[i],
                            e_gdn_conv[i], e_gdn_a_log[i], e_gdn_dt_bias[i], e_gdn_o_norm[i], e_w_out[i])
            h = ffn_residual(h, mix, e_norm2[i], e_ffn_w_gate_up[i].astype(BF16), e_ffn_w_down[i].astype(BF16),
                             tm=FFN_ROWS, tf=FFN_COLS)
        else:
            mix = _odd_mix(h, B, S, o_norm1[i], o_w_qkv[i], o_q_norm[i], o_k_norm[i], o_w_out[i])
            h = moe_residual(h, mix, o_norm2[i], o_router[i], o_exp_w_gate_up[i], o_exp_w_down[i])
    return h.reshape(B, S, D)
```

```python
import functools

import jax
import jax.numpy as jnp
from jax import lax
from jax.experimental import pallas as pl
from jax.experimental.pallas import tpu as pltpu
from jax.experimental.pallas import tpu_sc as plsc

F32 = jnp.float32
BF16 = jnp.bfloat16

HEAD_DIM = 128
FOX_HEADS = 4
GDN_HEADS = 4
CONV_WIDTH = 4
MOBA_BLOCK = 256
MOBA_TOPK = 3
N_EXPERTS = 8
ROPE_THETA = 10000.0
EPS = 1e-6

LANES = 128
GDN_CHUNK = 128
INV_BLOCK = 16
GDN_HEADS_PER_STEP = 4
GDN_CHUNKS_PER_STEP = 4
NEG = -(2.0 ** 100)
LOG2E = 1.4426950408889634
ATT_BLOCK = 256
ATT_HEADS = 4
KEY_BLOCKS_PER_STEP = 2
V_ROWS = HEAD_DIM + 16
PERM_TILE = 1024
ROUTER_CHAINS = 4
SC_ROWS = 64
ROW_TILE = 1024
MXU_COLS = 256
IN_PROJ_COLS = 7 * MXU_COLS
QKV_COLS = 6 * MXU_COLS
FFN_ROWS, FFN_COLS = 512, 1408
EXPERT_TILE = 1024
EXPERT_COLS = 896
EXPERT_ROW_PARTS = 4
VMEM_LIMIT_BYTES = 56 * 1024 * 1024

FOX_Q0, FOX_K0, FOX_V0 = 0, 4, 8
GDN_Q0, GDN_K0, GDN_V0, GDN_G0 = 12, 16, 20, 24
LANE_F, LANE_A, LANE_B = 0, 4, 8


def _params(*sem):
    return pltpu.CompilerParams(dimension_semantics=sem, vmem_limit_bytes=VMEM_LIMIT_BYTES)


def _rms(x, gain):
    return x * lax.rsqrt(jnp.mean(x * x, axis=-1, keepdims=True) + EPS) * gain


def _dot_nt(a, b, **kw):
    return lax.dot_general(a, b, (((1,), (1,)), ((), ())), preferred_element_type=F32, **kw)


def _pick_lane(x, lane_idx):
    lane = lax.broadcasted_iota(jnp.int32, x.shape, 1)
    return jnp.sum(jnp.where(lane == lane_idx, x, 0.0), axis=-1, keepdims=True)


def _pack_bf16_pairs(x):
    n = x.shape[1] // 2
    hi = pltpu.bitcast(x[:, :n].astype(BF16).astype(F32), jnp.uint32)
    lo = pltpu.bitcast(x[:, n:].astype(BF16).astype(F32), jnp.uint32)
    return pltpu.bitcast(hi | (lo >> 16), jnp.int32)


def _unpack_bf16_pairs(w):
    u = pltpu.bitcast(w, jnp.uint32)
    hi = pltpu.bitcast(u & jnp.uint32(0xFFFF0000), F32).astype(BF16)
    lo = pltpu.bitcast(u << 16, F32).astype(BF16)
    return jnp.concatenate([hi, lo], axis=1)


def _norm_mm_kernel(x_ref, g_ref, w_ref, *rest, has_aux):
    if has_aux:
        waux_ref, o_ref, oaux_ref, xn_ref = rest
    else:
        o_ref, xn_ref = rest

    @pl.when(pl.program_id(1) == 0)
    def _():
        xn = _rms(x_ref[...], g_ref[...]).astype(BF16)
        xn_ref[...] = xn
        if has_aux:
            oaux_ref[...] = jnp.dot(xn, waux_ref[...], preferred_element_type=F32)

    o_ref[...] = jnp.dot(xn_ref[...], w_ref[...].astype(BF16), preferred_element_type=F32).astype(o_ref.dtype)


def norm_matmul(x, gain, w, w_aux=None, *, tm, tn, out_dtype=BF16):
    M, K = x.shape
    N = w.shape[1]
    has_aux = w_aux is not None
    in_specs = [
        pl.BlockSpec((tm, K), lambda i, j: (i, 0)),
        pl.BlockSpec((1, K), lambda i, j: (0, 0)),
        pl.BlockSpec((K, tn), lambda i, j: (0, j)),
    ]
    out_shape = [jax.ShapeDtypeStruct((M, N), out_dtype)]
    out_specs = [pl.BlockSpec((tm, tn), lambda i, j: (i, j))]
    args = [x, gain.reshape(1, K), w]
    if has_aux:
        in_specs.append(pl.BlockSpec((K, LANES), lambda i, j: (0, 0)))
        out_shape.append(jax.ShapeDtypeStruct((M, LANES), F32))
        out_specs.append(pl.BlockSpec((tm, LANES), lambda i, j: (i, 0)))
        args.append(w_aux)
    res = pl.pallas_call(
        functools.partial(_norm_mm_kernel, has_aux=has_aux),
        grid=(M // tm, N // tn),
        in_specs=in_specs,
        out_specs=out_specs,
        out_shape=out_shape,
        scratch_shapes=[pltpu.VMEM((tm, K), BF16)],
        compiler_params=_params("parallel", "arbitrary"),
        name="norm_matmul",
    )(*args)
    return res if has_aux else res[0]


def _plus_projections(x, proj_refs, rows=slice(None)):
    for a_ref, w_ref in zip(proj_refs[0::2], proj_refs[1::2]):
        x = x + jnp.dot(a_ref[rows, :], w_ref[...].astype(BF16), preferred_element_type=F32)
    return x


def _projection_specs(pairs, tm):
    specs, args = [], []
    for a, w in pairs:
        specs.append(pl.BlockSpec((tm, a.shape[1]), lambda i, *_: (i, 0)))
        specs.append(pl.BlockSpec(w.shape, lambda i, *_: (0, 0)))
        args += [a, w]
    return specs, args


def _gate_kernel(zs_ref, par_ref, col_ref, row_ref, *, S, B):
    C = GDN_CHUNK
    bias = par_ref[0:1, :]
    neg_a = -jnp.exp(par_ref[1:2, :])
    r = lax.broadcasted_iota(jnp.int32, (C, C), 0)
    c = lax.broadcasted_iota(jnp.int32, (C, C), 1)
    tril = (r >= c).astype(F32)
    lane = lax.broadcasted_iota(jnp.int32, (C, LANES), 1)

    def body(n, carry):
        sl = pl.ds(pl.multiple_of(n * C, C), C)
        us, betas = [], []
        for bi in range(B):
            z = zs_ref[bi, sl, :]
            t = z + bias
            soft = jnp.log(1.0 + jnp.exp(-jnp.abs(t)))
            log_f = jnp.minimum(t, 0.0) - soft
            g = neg_a * (jnp.maximum(t, 0.0) + soft)
            betas.append(1.0 / (1.0 + jnp.exp(-z)))
            us.append(jnp.where(lane < LANE_A, log_f, jnp.where(lane < LANE_B, g, 0.0)))
        sums = jnp.dot(tril, jnp.concatenate(us, axis=1), preferred_element_type=F32, precision=lax.Precision.HIGHEST)
        last = []
        for bi in range(B):
            cs = sums[:, bi * LANES:(bi + 1) * LANES] + jnp.where(lane < LANE_A, carry[bi], 0.0)
            out = jnp.where(lane < LANE_B, cs, betas[bi])
            col_ref[bi, sl, :] = out
            out_t = out.T
            for hd in range(GDN_HEADS):
                row_ref[bi, hd, :, sl] = out_t[LANE_A + hd:LANE_A + hd + 1, :]
            last.append(cs[C - 1:C, :])
        return tuple(last)

    lax.fori_loop(0, S // C, body, tuple(jnp.zeros((1, LANES), F32) for _ in range(B)))


def even_gates(zs, par):
    B, S, _ = zs.shape
    return pl.pallas_call(
        functools.partial(_gate_kernel, S=S, B=B),
        grid=(1,),
        in_specs=[
            pl.BlockSpec((B, S, LANES), lambda i: (0, 0, 0)),
            pl.BlockSpec((8, LANES), lambda i: (0, 0)),
        ],
        out_specs=[
            pl.BlockSpec((B, S, LANES), lambda i: (0, 0, 0)),
            pl.BlockSpec((B, GDN_HEADS, 1, S), lambda i: (0, 0, 0, 0)),
        ],
        out_shape=[jax.ShapeDtypeStruct((B, S, LANES), F32), jax.ShapeDtypeStruct((B, GDN_HEADS, 1, S), F32)],
        compiler_params=_params("arbitrary"),
        name="even_gates",
    )(zs, par)


def _split3(x):
    hi = x.astype(BF16).astype(F32)
    mid = (x - hi).astype(BF16).astype(F32)
    lo = (x - hi - mid).astype(BF16).astype(F32)
    return hi, mid, lo


def _interleave(gens):
    out = [None] * len(gens)
    live = list(range(len(gens)))
    while live:
        for n in list(live):
            try:
                next(gens[n])
            except StopIteration as stop:
                out[n] = stop.value
                live.remove(n)
    return out


def _attend(state, c, qa, ka_ref, vt_ref, key0, nkeys, keep=None):
    ks = pl.ds(pl.multiple_of(key0, ATT_BLOCK), nkeys)
    st = _dot_nt(ka_ref[ks, :], qa)
    yield
    if keep is not None:
        st = jnp.where(keep, st, NEG)
    m_new = jnp.max(st, axis=0, keepdims=True)
    if state[c] is not None:
        m_old, acc_old = state[c]
        m_new = jnp.maximum(m_old, m_new)
    p = jnp.exp2(st - m_new).astype(BF16)
    pv = jnp.dot(vt_ref[:, ks], p, preferred_element_type=F32)
    state[c] = (m_new, pv if state[c] is None else acc_old * jnp.exp2(m_old - m_new) + pv)


def _attend_tile_pair(i, prep, ka_ref, vt_ref, m_ref, acc_ref, o_ref):
    t = ATT_BLOCK
    chains = [(hd, c) for c in range(2) for hd in range(ATT_HEADS)]
    num = lambda hd, c: 2 * hd + c
    qa = _interleave(prep)

    def step(state, hd, c, which, **kw):
        return _attend(state, num(hd, c), qa[num(hd, c)][which], ka_ref.at[hd], vt_ref.at[hd], **kw)

    def save(state):
        m_ref[...] = jnp.stack([state[n][0] for n in range(len(chains))])
        acc_ref[...] = jnp.stack([state[n][1] for n in range(len(chains))])

    def past_blocks(key0, n_steps):
        state = {n: (m_ref[n], acc_ref[n]) for n in range(len(chains))}
        _interleave([step(state, hd, c, 1, key0=key0 + s * 2 * t, nkeys=2 * t)
                     for s in range(n_steps) for hd, c in chains])
        save(state)

    state = {n: None for n in range(len(chains))}
    keep = _causal_keep()
    _interleave([step(state, hd, c, 0, key0=(2 * i + c) * t, nkeys=t, keep=keep) for hd, c in chains]
                + [step(state, hd, 1, 1, key0=(2 * i) * t, nkeys=t) for hd in range(ATT_HEADS)])
    save(state)

    def four_blocks(g, _):
        past_blocks(g * (4 * t), 2)
        return 0

    lax.fori_loop(0, i // 2, four_blocks, 0)

    @pl.when((i & 1) != 0)
    def _():
        past_blocks((i // 2) * (4 * t), 1)

    for hd, c in chains:
        acc = acc_ref[num(hd, c)]
        out_t = acc[:HEAD_DIM, :] * (1.0 / acc[HEAD_DIM:HEAD_DIM + 1, :])
        o_ref[c * t:(c + 1) * t, hd * HEAD_DIM:(hd + 1) * HEAD_DIM] = out_t.T.astype(o_ref.dtype)


def _transposed(v):
    r = lax.broadcasted_iota(jnp.int32, (HEAD_DIM, HEAD_DIM), 0)
    c = lax.broadcasted_iota(jnp.int32, (HEAD_DIM, HEAD_DIM), 1)
    v_t = _dot_nt(jnp.where(r == c, 1.0, 0.0).astype(BF16), v).astype(BF16)
    return jnp.concatenate([v_t, jnp.ones((V_ROWS - HEAD_DIM, ATT_BLOCK), BF16)], axis=0)


def _causal_keep():
    key = lax.broadcasted_iota(jnp.int32, (ATT_BLOCK, ATT_BLOCK), 0)
    qry = lax.broadcasted_iota(jnp.int32, (ATT_BLOCK, ATT_BLOCK), 1)
    return key <= qry


_ATT_SCRATCH = lambda S: [pltpu.VMEM((ATT_HEADS, S, 2 * HEAD_DIM), BF16), pltpu.VMEM((ATT_HEADS, V_ROWS, S), BF16),
                          pltpu.VMEM((2 * ATT_HEADS, 1, ATT_BLOCK), F32),
                          pltpu.VMEM((2 * ATT_HEADS, V_ROWS, ATT_BLOCK), F32)]


def _fox_kernel(q_ref, k_ref, v_ref, col_ref, qg_ref, kg_ref, o_ref, ka_ref, vt_ref, m_ref, acc_ref, *, S):
    h0 = pl.program_id(1) * ATT_HEADS
    i = pl.program_id(2)
    t = ATT_BLOCK
    D = HEAD_DIM
    lane = lax.broadcasted_iota(jnp.int32, (t, LANES), 1)

    @pl.when(i == 0)
    def _():
        def prep_keys(g, _):
            todo = []
            for u in range(KEY_BLOCKS_PER_STEP):
                sl = pl.ds(pl.multiple_of((g * KEY_BLOCKS_PER_STEP + u) * t, t), t)
                for hd in range(ATT_HEADS):
                    cols = slice(hd * D, (hd + 1) * D)
                    kn = _rms(k_ref[sl, cols].astype(F32), kg_ref[...]).astype(BF16)
                    hi, mid, lo = _split3(-LOG2E * _pick_lane(col_ref[sl, :], LANE_F + h0 + hd))
                    aug = jnp.where(lane < 3, 1.0, jnp.where(lane == 3, hi, jnp.where(lane == 4, mid,
                                                                                    jnp.where(lane == 5, lo, 0.0))))
                    todo.append((hd, sl, jnp.concatenate([kn, aug.astype(BF16)], axis=1), _transposed(v_ref[sl, cols])))
            for hd, sl, ka, vt in todo:
                ka_ref[hd, sl, :] = ka
                vt_ref[hd, :, sl] = vt
            return 0

        lax.fori_loop(0, S // (t * KEY_BLOCKS_PER_STEP), prep_keys, 0)

    def prep(hd, c):
        q = q_ref[c * t:(c + 1) * t, hd * D:(hd + 1) * D].astype(F32)
        qn = (_rms(q, qg_ref[...]) * (LOG2E * D ** -0.5)).astype(BF16)
        yield
        qsl = pl.ds(pl.multiple_of((2 * i + c) * t, t), t)
        hi, mid, lo = _split3(LOG2E * _pick_lane(col_ref[qsl, :], LANE_F + h0 + hd))
        aug = jnp.where(lane == 0, hi,
                        jnp.where(lane == 1, mid, jnp.where(lane == 2, lo, jnp.where(lane < 6, 1.0, 0.0))))
        qa = jnp.concatenate([qn, aug.astype(BF16)], axis=-1)
        return qa, qa

    _attend_tile_pair(i, [prep(hd, c) for hd in range(ATT_HEADS) for c in range(2)],
                      ka_ref, vt_ref, m_ref, acc_ref, o_ref)


def fox_attention(z, col, q_gain, k_gain):
    B, S, _ = z.shape
    t = 2 * ATT_BLOCK
    w = ATT_HEADS * HEAD_DIM
    return pl.pallas_call(
        functools.partial(_fox_kernel, S=S),
        grid=(B, FOX_HEADS // ATT_HEADS, S // t),
        in_specs=[
            pl.BlockSpec((None, t, w), lambda b, h, i: (b, i, FOX_Q0 // ATT_HEADS + h)),
            pl.BlockSpec((None, S, w), lambda b, h, i: (b, 0, FOX_K0 // ATT_HEADS + h)),
            pl.BlockSpec((None, S, w), lambda b, h, i: (b, 0, FOX_V0 // ATT_HEADS + h)),
            pl.BlockSpec((None, S, LANES), lambda b, h, i: (b, 0, 0)),
            pl.BlockSpec((1, HEAD_DIM), lambda b, h, i: (0, 0)),
            pl.BlockSpec((1, HEAD_DIM), lambda b, h, i: (0, 0)),
        ],
        out_specs=pl.BlockSpec((None, t, w), lambda b, h, i: (b, i, h)),
        out_shape=jax.ShapeDtypeStruct((B, S, FOX_HEADS * HEAD_DIM), BF16),
        scratch_shapes=_ATT_SCRATCH(S),
        compiler_params=_params("parallel", "parallel", "arbitrary"),
        name="fox_attention",
    )(z, z, z, col, q_gain.reshape(1, -1), k_gain.reshape(1, -1))


def _unit_lower_inverse(m):
    n = m.shape[0]
    r = lax.broadcasted_iota(jnp.int32, (n, n), 0)
    c = lax.broadcasted_iota(jnp.int32, (n, n), 1)
    eye = (r == c).astype(F32)

    def same_block(b):
        return (r // b) == (c // b)

    p = jnp.where(same_block(INV_BLOCK), m, 0.0)
    inv = eye - p
    k = 2
    while k < INV_BLOCK:
        pb = p.astype(BF16)
        p = jnp.dot(pb, pb, preferred_element_type=F32)
        yield
        inv = jnp.dot(inv.astype(BF16), (eye + p).astype(BF16), preferred_element_type=F32)
        yield
        k *= 2
    b = INV_BLOCK
    while b < n:
        off = jnp.where(same_block(2 * b), jnp.where(same_block(b), 0.0, m), 0.0).astype(BF16)
        ib = inv.astype(BF16)
        left = jnp.dot(ib, off, preferred_element_type=F32).astype(BF16)
        yield
        inv = inv - jnp.dot(left, ib, preferred_element_type=F32)
        yield
        b *= 2
    return inv


def _gdn_kernel(q_ref, k_ref, v_ref, gg_ref, wq_ref, wk_ref, wv_ref, col_ref, row_ref, on_ref,
                o_ref, qs_ref, ks_ref, vs_ref, *, S, rows, hb):
    h0 = pl.program_id(1) * hb
    C = GDN_CHUNK
    D = HEAD_DIM

    streams = [(x_ref, w_ref, dst_ref, hh, mode) for hh in range(hb)
               for x_ref, w_ref, dst_ref, mode in ((q_ref, wq_ref, qs_ref, "q"), (k_ref, wk_ref, ks_ref, "k"),
                                                   (v_ref, wv_ref, vs_ref, "v"))]
    halo = 16

    def conv_step(base, windows):
        outs = []
        for (x_ref, w_ref, dst_ref, hh, mode), win in zip(streams, windows):
            w = w_ref[:, hh * D:(hh + 1) * D]
            y = jnp.zeros((rows, D), F32)
            for tap in range(CONV_WIDTH):
                lead = halo - (CONV_WIDTH - 1) + tap
                y = y + w[tap:tap + 1, :] * pltpu.roll(win, rows + halo - lead, 0)[0:rows, :]
            y = y / (1.0 + jnp.exp(-y))
            if mode != "v":
                y = y * lax.rsqrt(jnp.sum(y * y, axis=-1, keepdims=True) + EPS)
            if mode == "q":
                y = y * D ** -0.5
            outs.append(y)
        for (x_ref, w_ref, dst_ref, hh, mode), y in zip(streams, outs):
            dst_ref[hh, pl.ds(base, rows), :] = y.astype(dst_ref.dtype)

    conv_step(0, [jnp.concatenate([jnp.zeros((halo, D), F32), x_ref[0:rows, hh * D:(hh + 1) * D].astype(F32)], axis=0)
                  for x_ref, _, _, hh, _ in streams])

    def conv_rest(n, _):
        base = pl.multiple_of(n * rows, rows)
        conv_step(base, [x_ref[pl.ds(pl.multiple_of(base - halo, halo), rows + halo), hh * D:(hh + 1) * D].astype(F32)
                         for x_ref, _, _, hh, _ in streams])
        return 0

    lax.fori_loop(1, S // rows, conv_rest, 0)

    r = lax.broadcasted_iota(jnp.int32, (C, C), 0)
    c = lax.broadcasted_iota(jnp.int32, (C, C), 1)
    incl = r >= c
    strict = r > c

    def chunk_local(hh, sl, tab):
        q = qs_ref[hh, sl, :].astype(F32)
        k = ks_ref[hh, sl, :].astype(F32)
        v = vs_ref[hh, sl, :].astype(F32)
        gcol = _pick_lane(tab, LANE_A + h0 + hh)
        beta = _pick_lane(tab, LANE_B + h0 + hh)
        grow = row_ref[hh, :, sl]
        glast = gcol[C - 1:C, :]
        decay = jnp.where(incl, jnp.exp(jnp.where(incl, gcol - grow, 0.0)), 0.0)
        eg = jnp.exp(gcol)
        kb = k * beta
        kbf = k.astype(BF16)
        m = jnp.where(strict, _dot_nt(kb.astype(BF16), kbf) * decay, 0.0)
        attn = (_dot_nt(q.astype(BF16), kbf) * decay).astype(BF16)
        yield
        tinv = (yield from _unit_lower_inverse(m)).astype(BF16)
        rhs = jnp.concatenate([v * beta, kb * eg], axis=-1).astype(BF16)
        sol = jnp.dot(tinv, rhs, preferred_element_type=F32)
        yield
        gate = gg_ref[sl, hh * D:(hh + 1) * D].astype(F32)
        return dict(u=sol[:, :D], w=sol[:, D:].astype(BF16), attn=attn, qg=(q * eg).astype(BF16),
                    kg_t=(k * jnp.exp(glast - gcol)).T.astype(BF16), keep=jnp.exp(glast),
                    gate=gate / (1.0 + jnp.exp(-gate)))

    def chunk_state(parts, state):
        outs = []
        for c in parts:
            sb = state.astype(BF16)
            v_new = c["u"] - jnp.dot(c["w"], sb, preferred_element_type=F32)
            o_state = jnp.dot(c["qg"], sb, preferred_element_type=F32)
            yield
            vb = v_new.astype(BF16)
            o = o_state + jnp.dot(c["attn"], vb, preferred_element_type=F32)
            state = state * c["keep"] + jnp.dot(c["kg_t"], vb, preferred_element_type=F32)
            yield
            outs.append((_rms(o, on_ref[...]) * c["gate"]).astype(o_ref.dtype))
        return state, outs

    def chunk_group(n, states):
        sls = [pl.ds(pl.multiple_of((n * GDN_CHUNKS_PER_STEP + g) * C, C), C) for g in range(GDN_CHUNKS_PER_STEP)]
        tabs = [col_ref[sl, :] for sl in sls]
        parts = _interleave([chunk_local(hh, sls[g], tabs[g]) for g in range(GDN_CHUNKS_PER_STEP) for hh in range(hb)])
        res = _interleave([chunk_state([parts[g * hb + hh] for g in range(GDN_CHUNKS_PER_STEP)], states[hh])
                           for hh in range(hb)])
        for g in range(GDN_CHUNKS_PER_STEP):
            o_ref[sls[g], :] = jnp.concatenate([outs[g] for _, outs in res], axis=-1)
        return tuple(s for s, _ in res)

    lax.fori_loop(0, S // (C * GDN_CHUNKS_PER_STEP), chunk_group, tuple(jnp.zeros((D, D), F32) for _ in range(hb)))


def gdn_mixer(z, conv_w, col, row, o_gain):
    B, S, _ = z.shape
    D = HEAD_DIM
    hb = GDN_HEADS_PER_STEP
    rows = 256
    seq = lambda off: pl.BlockSpec((None, S, hb * D), lambda b, h: (b, 0, off // hb + h),
                                   pipeline_mode=pl.Buffered(1))
    cw = lambda off: pl.BlockSpec((CONV_WIDTH, hb * D), lambda b, h: (0, off // hb + h))
    return pl.pallas_call(
        functools.partial(_gdn_kernel, S=S, rows=rows, hb=hb),
        grid=(B, GDN_HEADS // hb),
        in_specs=[
            seq(GDN_Q0), seq(GDN_K0), seq(GDN_V0), seq(GDN_G0),
            cw(0), cw(GDN_HEADS), cw(2 * GDN_HEADS),
            pl.BlockSpec((None, S, LANES), lambda b, h: (b, 0, 0)),
            pl.BlockSpec((None, hb, 1, S), lambda b, h: (b, h, 0, 0)),
            pl.BlockSpec((1, D), lambda b, h: (0, 0)),
        ],
        out_specs=pl.BlockSpec((None, S, hb * D), lambda b, h: (b, 0, h)),
        out_shape=jax.ShapeDtypeStruct((B, S, GDN_HEADS * D), BF16),
        scratch_shapes=[pltpu.VMEM((hb, S, D), BF16)] * 3,
        compiler_params=_params("parallel", "parallel"),
        name="gdn_mixer",
    )(z, z, z, z, conv_w, conv_w, conv_w, col, row, o_gain.reshape(1, D))


def _rope(x, cos, sin_signed):
    return x * cos + pltpu.roll(x, HEAD_DIM // 2, 1) * sin_signed


def _moba_kernel(q_ref, k_ref, v_ref, cos_ref, sin_ref, qg_ref, kg_ref, o_ref,
                 ka_ref, vt_ref, m_ref, acc_ref, kmean_ref, *, S):
    i = pl.program_id(2)
    t = ATT_BLOCK
    D = HEAD_DIM
    lane = lax.broadcasted_iota(jnp.int32, (t, LANES), 1)

    @pl.when(i == 0)
    def _():
        kmean_ref[...] = jnp.zeros(kmean_ref.shape, F32)

        def prep_keys(g, _):
            todo = []
            for u in range(KEY_BLOCKS_PER_STEP):
                n = g * KEY_BLOCKS_PER_STEP + u
                sl = pl.ds(pl.multiple_of(n * t, t), t)
                onehot = jnp.where(lane == n, 1.0, 0.0).astype(BF16)
                for hd in range(ATT_HEADS):
                    cols = slice(hd * D, (hd + 1) * D)
                    k = _rope(_rms(k_ref[sl, cols].astype(F32), kg_ref[...]), cos_ref[sl, :], sin_ref[sl, :])
                    todo.append((hd, n, sl, jnp.concatenate([k.astype(BF16), onehot], axis=1),
                                 jnp.mean(k, axis=0, keepdims=True), _transposed(v_ref[sl, cols])))
            for hd, n, sl, ka, kmean, vt in todo:
                ka_ref[hd, sl, :] = ka
                kmean_ref[hd, pl.ds(n, 1), :] = kmean
                vt_ref[hd, :, sl] = vt
            return 0

        lax.fori_loop(0, S // (t * KEY_BLOCKS_PER_STEP), prep_keys, 0)

    nb = -(-(S // t) // 8) * 8

    def prep(hd, c):
        cur = 2 * i + c
        qsl = pl.ds(pl.multiple_of(cur * t, t), t)
        q = _rope(_rms(q_ref[c * t:(c + 1) * t, hd * D:(hd + 1) * D].astype(F32), qg_ref[...]),
                  cos_ref[qsl, :], sin_ref[qsl, :])
        yield
        kmean = kmean_ref[hd]
        km_hi = kmean.astype(BF16)
        km_split = jnp.concatenate([km_hi, (kmean - km_hi.astype(F32)).astype(BF16)], axis=0)
        q_hi = q.astype(BF16)
        q_lo = (q - q_hi.astype(F32)).astype(BF16)
        part = _dot_nt(km_split, q_hi)
        gate = (part[:LANES] + part[LANES:] + _dot_nt(km_split[:LANES, :], q_lo))[0:nb]
        yield
        blk = lax.broadcasted_iota(jnp.int32, (nb, t), 0)
        blk_f = blk.astype(F32)
        gate = jnp.where(blk < cur, gate, -jnp.inf)
        sel = jnp.full((nb, t), NEG, F32)
        for _ in range(MOBA_TOPK):
            top = jnp.max(gate, axis=0, keepdims=True)
            first = jnp.min(jnp.where(gate == top, blk_f, float(LANES)), axis=0, keepdims=True)
            pick = blk_f == first
            sel = jnp.where(pick & (first < cur.astype(F32)), 0.0, sel)
            gate = jnp.where(pick, -jnp.inf, gate)
        sel_bias = jnp.concatenate([sel, jnp.full((LANES - nb, t), NEG, F32)], axis=0).T
        qs = (q * (LOG2E * D ** -0.5)).astype(BF16)
        return (jnp.concatenate([qs, jnp.zeros((t, LANES), BF16)], axis=-1),
                jnp.concatenate([qs, sel_bias.astype(BF16)], axis=-1))

    _attend_tile_pair(i, [prep(hd, c) for hd in range(ATT_HEADS) for c in range(2)],
                      ka_ref, vt_ref, m_ref, acc_ref, o_ref)


def moba_attention(z, cos, sin_signed, q_gain, k_gain):
    B, S, W = z.shape
    H = W // (3 * HEAD_DIM)
    assert ATT_BLOCK == MOBA_BLOCK
    t = 2 * ATT_BLOCK
    w = ATT_HEADS * HEAD_DIM
    hp = H // ATT_HEADS
    return pl.pallas_call(
        functools.partial(_moba_kernel, S=S),
        grid=(B, hp, S // t),
        in_specs=[
            pl.BlockSpec((None, t, w), lambda b, h, i: (b, i, h)),
            pl.BlockSpec((None, S, w), lambda b, h, i: (b, 0, hp + h)),
            pl.BlockSpec((None, S, w), lambda b, h, i: (b, 0, 2 * hp + h)),
            pl.BlockSpec((S, HEAD_DIM), lambda b, h, i: (0, 0)),
            pl.BlockSpec((S, HEAD_DIM), lambda b, h, i: (0, 0)),
            pl.BlockSpec((1, HEAD_DIM), lambda b, h, i: (0, 0)),
            pl.BlockSpec((1, HEAD_DIM), lambda b, h, i: (0, 0)),
        ],
        out_specs=pl.BlockSpec((None, t, w), lambda b, h, i: (b, i, h)),
        out_shape=jax.ShapeDtypeStruct((B, S, H * HEAD_DIM), BF16),
        scratch_shapes=_ATT_SCRATCH(S) + [pltpu.VMEM((ATT_HEADS, LANES, HEAD_DIM), F32)],
        compiler_params=_params("parallel", "parallel", "arbitrary"),
        name="moba_attention",
    )(z, z, z, cos, sin_signed, q_gain.reshape(1, -1), k_gain.reshape(1, -1))


META_IDX, META_RANK, META_GATE = 0, 2, 4


def _router_kernel(x_ref, *refs, n_proj):
    proj_refs = refs[:2 * n_proj]
    g_ref, w_ref, y_ref, xn_ref, meta_ref, cnt_ref, carry_ref = refs[2 * n_proj:]
    i = pl.program_id(0)
    tm = x_ref.shape[0]

    @pl.when(i == 0)
    def _():
        carry_ref[...] = jnp.zeros(carry_ref.shape, F32)

    w = w_ref[...]
    w_hi = w.astype(BF16)
    w_lo = (w - w_hi.astype(F32)).astype(BF16)
    part = tm // ROUTER_CHAINS
    lane = lax.broadcasted_iota(jnp.int32, (part, LANES), 1)
    lane_f = lane.astype(F32)

    def route(n):
        rows = slice(n * part, (n + 1) * part)
        y = _plus_projections(x_ref[rows, :], proj_refs, rows)
        y_ref[rows, :] = y
        yield
        xn = _rms(y, g_ref[...])
        xn_ref[rows, :] = _pack_bf16_pairs(xn)
        x_hi = xn.astype(BF16)
        x_lo = (xn - x_hi.astype(F32)).astype(BF16)
        logits = (jnp.dot(x_hi, w_hi, preferred_element_type=F32) + jnp.dot(x_hi, w_lo, preferred_element_type=F32)
                  + jnp.dot(x_lo, w_hi, preferred_element_type=F32))
        yield
        logits = jnp.where(lane < N_EXPERTS, logits, -jnp.inf)
        top1 = jnp.max(logits, axis=-1, keepdims=True)
        yield
        idx1 = jnp.min(jnp.where(logits == top1, lane_f, float(LANES)), axis=-1, keepdims=True)
        yield
        rest = jnp.where(lane_f == idx1, -jnp.inf, logits)
        top2 = jnp.max(rest, axis=-1, keepdims=True)
        yield
        idx2 = jnp.min(jnp.where(rest == top2, lane_f, float(LANES)), axis=-1, keepdims=True)
        e2 = jnp.exp(top2 - top1)
        denom = 1.0 + e2
        return idx1, idx2, 1.0 / denom, e2 / denom, jnp.where((lane_f == idx1) | (lane_f == idx2), 1.0, 0.0)

    picks = _interleave([route(n) for n in range(ROUTER_CHAINS)])
    chosen = jnp.concatenate([p[4] for p in picks], axis=0)
    r = lax.broadcasted_iota(jnp.int32, (tm, tm), 0)
    c = lax.broadcasted_iota(jnp.int32, (tm, tm), 1)
    ahead = jnp.dot(jnp.where(r > c, 1.0, 0.0).astype(BF16), chosen.astype(BF16), preferred_element_type=F32)
    carry = carry_ref[...]
    for n, (idx1, idx2, g1, g2, _) in enumerate(picks):
        rows = slice(n * part, (n + 1) * part)
        rank = ahead[rows, :] + carry
        rank1 = jnp.sum(jnp.where(lane_f == idx1, rank, 0.0), axis=-1, keepdims=True)
        rank2 = jnp.sum(jnp.where(lane_f == idx2, rank, 0.0), axis=-1, keepdims=True)
        meta = jnp.zeros((part, LANES), F32)
        for k, v in enumerate((idx1, idx2, rank1, rank2, g1, g2)):
            meta = jnp.where(lane == k, v, meta)
        meta_ref[rows, :] = meta
    carry = carry + jnp.sum(chosen, axis=0, keepdims=True)
    carry_ref[...] = carry
    cnt_ref[...] = jnp.broadcast_to(carry, cnt_ref.shape)


def moe_router(x, proj, gain, w_router, *, tm):
    M, K = x.shape
    w = jnp.zeros((K, LANES), F32).at[:, :N_EXPERTS].set(w_router)
    proj_specs, proj_args = _projection_specs(proj, tm)
    return pl.pallas_call(
        functools.partial(_router_kernel, n_proj=len(proj)),
        grid=(M // tm,),
        in_specs=[
            pl.BlockSpec((tm, K), lambda i: (i, 0)),
            *proj_specs,
            pl.BlockSpec((1, K), lambda i: (0, 0)),
            pl.BlockSpec((K, LANES), lambda i: (0, 0)),
        ],
        out_specs=[
            pl.BlockSpec((tm, K), lambda i: (i, 0)),
            pl.BlockSpec((tm, K // 2), lambda i: (i, 0)),
            pl.BlockSpec((tm, LANES), lambda i: (i, 0)),
            pl.BlockSpec((8, LANES), lambda i: (0, 0)),
        ],
        out_shape=[
            jax.ShapeDtypeStruct((M, K), F32),
            jax.ShapeDtypeStruct((M, K // 2), jnp.int32),
            jax.ShapeDtypeStruct((M, LANES), F32),
            jax.ShapeDtypeStruct((8, LANES), F32),
        ],
        scratch_shapes=[pltpu.VMEM((1, LANES), F32)],
        compiler_params=_params("arbitrary"),
        name="moe_router",
    )(x, *proj_args, gain.reshape(1, K), w)


def _sc_workers():
    info = plsc.get_sparse_core_info()
    return info.num_cores, info.num_cores * info.num_subcores


def scatter_rows(rows, dest, n_out):
    M, W = rows.shape
    nc, nw = _sc_workers()
    assert M % (nw * SC_ROWS) == 0
    per_w = M // nw
    mesh = plsc.VectorSubcoreMesh(core_axis_name="c", subcore_axis_name="s")

    n_chunks = per_w // SC_ROWS
    assert n_chunks % 2 == 0
    buf = [pltpu.VMEM((SC_ROWS, W), rows.dtype), pltpu.VMEM((SC_ROWS,), jnp.int32), pltpu.VMEM((SC_ROWS,), jnp.int32),
           pltpu.SemaphoreType.DMA]

    @functools.partial(pl.kernel, mesh=mesh, out_type=jax.ShapeDtypeStruct((n_out, W), rows.dtype),
                       scratch_types=buf + buf)
    def kern(rows_hbm, dest_hbm, out_hbm, rows0, ia0, ib0, sem0, rows1, ia1, ib1, sem1):
        wid = lax.axis_index("s") * nc + lax.axis_index("c")
        slots = ((rows0, (ia0, ib0), sem0), (rows1, (ia1, ib1), sem1))

        def send(g, slot):
            rows_v, idx, sem = slot
            base = wid * per_w + g * SC_ROWS
            pltpu.sync_copy(rows_hbm.at[pl.ds(base, SC_ROWS)], rows_v)
            for k in range(2):
                pltpu.sync_copy(dest_hbm.at[k, pl.ds(base, SC_ROWS)], idx[k])
                pltpu.async_copy(rows_v, out_hbm.at[idx[k]], sem)

        def drain(slot):
            rows_v, idx, sem = slot
            for k in range(2):
                pltpu.make_async_copy(rows_v, out_hbm.at[idx[k]], sem).wait()

        send(0, slots[0])

        @pl.loop(0, n_chunks, step=2)
        def _(g):
            send(g + 1, slots[1])
            drain(slots[0])

            @pl.when(g + 2 < n_chunks)
            def _():
                send(g + 2, slots[0])

            drain(slots[1])

    return kern(rows, dest)


def gather_rows(table, idx):
    N = idx.shape[0]
    W = table.shape[1]
    nc, nw = _sc_workers()
    assert N % (nw * SC_ROWS) == 0
    per_w = N // nw
    mesh = plsc.VectorSubcoreMesh(core_axis_name="c", subcore_axis_name="s")

    n_chunks = per_w // SC_ROWS
    assert n_chunks % 2 == 0
    buf = [pltpu.VMEM((SC_ROWS,), jnp.int32), pltpu.VMEM((SC_ROWS, W), table.dtype), pltpu.SemaphoreType.DMA]

    @functools.partial(pl.kernel, mesh=mesh, out_type=jax.ShapeDtypeStruct((N, W), table.dtype),
                       scratch_types=buf + buf)
    def kern(table_hbm, idx_hbm, out_hbm, idx0, rows0, sem0, idx1, rows1, sem1):
        wid = lax.axis_index("s") * nc + lax.axis_index("c")
        slots = ((idx0, rows0, sem0), (idx1, rows1, sem1))

        def fetch(g, slot):
            idx_v, rows_v, sem = slot
            pltpu.sync_copy(idx_hbm.at[pl.ds(wid * per_w + g * SC_ROWS, SC_ROWS)], idx_v)
            pltpu.async_copy(table_hbm.at[idx_v], rows_v, sem)

        def drain(g, slot):
            idx_v, rows_v, sem = slot
            pltpu.make_async_copy(table_hbm.at[idx_v], rows_v, sem).wait()
            pltpu.sync_copy(rows_v, out_hbm.at[pl.ds(wid * per_w + g * SC_ROWS, SC_ROWS)])

        fetch(0, slots[0])

        @pl.loop(0, n_chunks, step=2)
        def _(g):
            fetch(g + 1, slots[1])
            drain(g, slots[0])

            @pl.when(g + 2 < n_chunks)
            def _():
                fetch(g + 2, slots[0])

            drain(g + 1, slots[1])

    return kern(table, idx)


def _combine_kernel(h_ref, y1_ref, y2_ref, meta_ref, o_ref):
    g1 = meta_ref[:, META_GATE:META_GATE + 1]
    g2 = meta_ref[:, META_GATE + 1:META_GATE + 2]
    o_ref[...] = (h_ref[...] + g1 * _unpack_bf16_pairs(y1_ref[...]).astype(F32)
                  + g2 * _unpack_bf16_pairs(y2_ref[...]).astype(F32))


def moe_combine(h, y_pairs, meta):
    M, K = h.shape
    p = PERM_TILE
    nt = M // p
    return pl.pallas_call(
        _combine_kernel,
        grid=(nt,),
        in_specs=[
            pl.BlockSpec((p, K), lambda i: (i, 0)),
            pl.BlockSpec((p, K // 2), lambda i: (i, 0)),
            pl.BlockSpec((p, K // 2), lambda i: (nt + i, 0)),
            pl.BlockSpec((p, LANES), lambda i: (i, 0)),
        ],
        out_specs=pl.BlockSpec((p, K), lambda i: (i, 0)),
        out_shape=jax.ShapeDtypeStruct((M, K), F32),
        compiler_params=_params("parallel"),
        name="moe_combine",
    )(h, y_pairs, y_pairs, meta)


def _expert_ffn_kernel(te_ref, tv_ref, x_ref, wg_ref, wu_ref, wd_ref, o_ref, acc_ref, *, nf):
    del te_ref
    i = pl.program_id(0)
    j = pl.program_id(1)
    tm = x_ref.shape[0]
    valid = tv_ref[i]

    def swiglu_rows(rows):
        x = _unpack_bf16_pairs(x_ref[0:rows, :])
        gt = jnp.dot(x, wg_ref[...].astype(BF16), preferred_element_type=F32)
        up = jnp.dot(x, wu_ref[...].astype(BF16), preferred_element_type=F32)
        act = (gt / (1.0 + jnp.exp(-gt)) * up).astype(BF16)
        part = jnp.dot(act, wd_ref[...].astype(BF16), preferred_element_type=F32)

        def finish(total):
            o_ref[0:rows, :] = _pack_bf16_pairs(total)
            if rows < tm:
                o_ref[rows:tm, :] = jnp.zeros((tm - rows, o_ref.shape[1]), o_ref.dtype)

        if nf == 1:
            finish(part)
            return

        @pl.when(j == 0)
        def _():
            acc_ref[0:rows, :] = part

        @pl.when((j > 0) & (j < nf - 1))
        def _():
            acc_ref[0:rows, :] += part

        @pl.when(j == nf - 1)
        def _():
            finish(acc_ref[0:rows, :] + part)

    part = tm // EXPERT_ROW_PARTS
    for n in range(1, EXPERT_ROW_PARTS + 1):
        pl.when((valid > (n - 1) * part) & (valid <= n * part))(functools.partial(swiglu_rows, n * part))

    @pl.when((valid == 0) & (j == 0))
    def _():
        o_ref[...] = jnp.zeros(o_ref.shape, o_ref.dtype)


def expert_ffn(xs, tile_expert, tile_valid, w_gate_up, w_down, *, tm, tf):
    R = xs.shape[0]
    E, F, K = w_down.shape
    nf = F // tf
    live = lambda i, tv: tv[i] > 0
    col = lambda i, j, tv: jnp.where(live(i, tv), j, nf - 1)
    grid_spec = pltpu.PrefetchScalarGridSpec(
        num_scalar_prefetch=2,
        grid=(R // tm, nf),
        in_specs=[
            pl.BlockSpec((tm, K // 2), lambda i, j, te, tv: (te[R // tm + i], 0)),
            pl.BlockSpec((None, K, tf), lambda i, j, te, tv: (te[i], 0, col(i, j, tv))),
            pl.BlockSpec((None, K, tf), lambda i, j, te, tv: (te[i], 0, nf + col(i, j, tv))),
            pl.BlockSpec((None, tf, K), lambda i, j, te, tv: (te[i], col(i, j, tv), 0)),
        ],
        out_specs=pl.BlockSpec((tm, K // 2), lambda i, j, te, tv: (i, 0)),
        scratch_shapes=[pltpu.VMEM((tm, K), F32)],
    )
    return pl.pallas_call(
        functools.partial(_expert_ffn_kernel, nf=nf),
        grid_spec=grid_spec,
        out_shape=jax.ShapeDtypeStruct((R, K // 2), jnp.int32),
        compiler_params=_params("arbitrary", "arbitrary"),
        name="moe_expert_ffn",
    )(tile_expert, tile_valid, xs, w_gate_up, w_gate_up, w_down)


def moe_residual(h, proj, gain, w_router, w_gate_up, w_down):
    M, K = h.shape
    p = PERM_TILE
    tm = EXPERT_TILE
    n_rows = 2 * M + N_EXPERTS * tm
    h, xn, meta, cnt = moe_router(h, proj, gain, w_router, tm=p)

    counts = cnt[0, :N_EXPERTS].astype(jnp.int32)
    padded = (counts + tm - 1) // tm * tm
    ends = jnp.cumsum(padded)
    offsets = ends - padded
    n_tiles = n_rows // tm
    tile_row = jnp.arange(n_tiles) * tm
    n_used = ends[-1] // tm
    last_used = jnp.minimum(tile_row // tm, n_used - 1)
    expert_of = lambda row: jnp.minimum(jnp.sum(ends[None, :] <= row[:, None], axis=1), N_EXPERTS - 1)
    tile_expert = expert_of(last_used * tm)
    row_in_expert = tile_row - jnp.sum(jnp.where(tile_expert[:, None] == jnp.arange(N_EXPERTS), offsets, 0), axis=1)
    own_count = jnp.sum(jnp.where(tile_expert[:, None] == jnp.arange(N_EXPERTS), counts, 0), axis=1)
    tile_valid = jnp.where(tile_row // tm < n_used, jnp.clip(own_count - row_in_expert, 0, tm), 0)
    tile_tables = jnp.concatenate([tile_expert, last_used]).astype(jnp.int32)

    idx = meta[:, META_IDX:META_IDX + 2].astype(jnp.int32)
    rank = meta[:, META_RANK:META_RANK + 2].astype(jnp.int32)
    dest = (jnp.sum(jnp.where(idx[:, :, None] == jnp.arange(N_EXPERTS), offsets, 0), axis=-1) + rank).T

    xs = scatter_rows(xn, dest, n_rows)
    ys = expert_ffn(xs, tile_tables, tile_valid.astype(jnp.int32), w_gate_up, w_down, tm=tm, tf=EXPERT_COLS)
    return moe_combine(h, gather_rows(ys, dest.reshape(-1)), meta)


def _ffn_kernel(x_ref, *refs, n_proj):
    proj_refs = refs[:2 * n_proj]
    g_ref, wg_ref, wu_ref, wd_ref, o_ref, xn_ref, acc_ref = refs[2 * n_proj:]
    j = pl.program_id(1)

    @pl.when(j == 0)
    def _():
        x = _plus_projections(x_ref[...], proj_refs)
        xn_ref[...] = _rms(x, g_ref[...]).astype(BF16)
        acc_ref[...] = x

    xn = xn_ref[...]
    gt = jnp.dot(xn, wg_ref[...], preferred_element_type=F32)
    up = jnp.dot(xn, wu_ref[...], preferred_element_type=F32)
    act = gt / (1.0 + jnp.exp(-gt)) * up
    acc_ref[...] += jnp.dot(act.astype(BF16), wd_ref[...], preferred_element_type=F32)

    @pl.when(j == pl.num_programs(1) - 1)
    def _():
        o_ref[...] = acc_ref[...]


def ffn_residual(x, proj, gain, w_gate_up, w_down, *, tm, tf):
    M, K = x.shape
    F = w_down.shape[0]
    nf = F // tf
    proj_specs, proj_args = _projection_specs(proj, tm)
    return pl.pallas_call(
        functools.partial(_ffn_kernel, n_proj=len(proj)),
        grid=(M // tm, nf),
        in_specs=[
            pl.BlockSpec((tm, K), lambda i, j: (i, 0)),
            *proj_specs,
            pl.BlockSpec((1, K), lambda i, j: (0, 0)),
            pl.BlockSpec((K, tf), lambda i, j: (0, j)),
            pl.BlockSpec((K, tf), lambda i, j: (0, nf + j)),
            pl.BlockSpec((tf, K), lambda i, j: (j, 0)),
        ],
        out_specs=pl.BlockSpec((tm, K), lambda i, j: (i, 0)),
        out_shape=jax.ShapeDtypeStruct((M, K), F32),
        scratch_shapes=[pltpu.VMEM((tm, K), BF16), pltpu.VMEM((tm, K), F32)],
        compiler_params=_params("parallel", "arbitrary"),
        name="ffn_residual",
    )(x, *proj_args, gain.reshape(1, K), w_gate_up, w_gate_up, w_down)


def _even_mix(h, B, S, norm1, w_in, f_bias, q_norm, k_norm, conv_w, a_log, dt_bias, o_norm, w_out):
    M, D = h.shape
    fw, gw = FOX_HEADS * HEAD_DIM, GDN_HEADS * HEAD_DIM
    o_ff = 3 * fw
    o_gq = o_ff + FOX_HEADS
    o_ga = o_gq + 3 * gw
    o_gb = o_ga + GDN_HEADS
    o_gg = o_gb + GDN_HEADS
    w_big = jnp.concatenate([w_in[:, :o_ff], w_in[:, o_gq:o_ga], w_in[:, o_gg:]], axis=1)
    w_small = jnp.concatenate([w_in[:, o_ff:o_gq], w_in[:, o_ga:o_gg],
                               jnp.zeros((D, LANES - FOX_HEADS - 2 * GDN_HEADS), F32)], axis=1)
    w_big, w_small = (t.astype(BF16) for t in lax.optimization_barrier((w_big, w_small)))
    z, zs = norm_matmul(h, norm1, w_big, w_small, tm=ROW_TILE, tn=IN_PROJ_COLS)
    par = jnp.zeros((8, LANES), F32)
    par = par.at[0, LANE_F:LANE_F + FOX_HEADS].set(f_bias).at[0, LANE_A:LANE_A + GDN_HEADS].set(dt_bias)
    par = par.at[1, LANE_A:LANE_A + GDN_HEADS].set(a_log)
    col, row = even_gates(zs.reshape(B, S, LANES), par)
    z = z.reshape(B, S, -1)
    fox = fox_attention(z, col, q_norm, k_norm)
    gdn = gdn_mixer(z, conv_w, col, row, o_norm)
    return [(fox.reshape(M, fw), w_out[:fw]), (gdn.reshape(M, gw), w_out[fw:])]


def _odd_mix(h, B, S, norm1, w_qkv, q_norm, k_norm, w_out):
    M, D = h.shape
    z = norm_matmul(h, norm1, w_qkv.astype(BF16), tm=ROW_TILE, tn=QKV_COLS).reshape(B, S, -1)
    half = HEAD_DIM // 2
    inv = jnp.power(ROPE_THETA, -jnp.arange(half, dtype=F32) / half)
    ang = jnp.arange(S, dtype=F32)[:, None] * inv[None, :]
    cos, sin = jnp.cos(ang), jnp.sin(ang)
    cos_full = jnp.concatenate([cos, cos], axis=-1)
    sin_signed = jnp.concatenate([-sin, sin], axis=-1)
    att = moba_attention(z, cos_full, sin_signed, q_norm, k_norm)
    return [(att.reshape(M, -1), w_out)]


def kernel(x, e_norm1, e_w_in, e_fox_f_bias, e_fox_q_norm, e_fox_k_norm, e_gdn_conv, e_gdn_a_log,
           e_gdn_dt_bias, e_gdn_o_norm, e_w_out, e_norm2, e_ffn_w_gate_up, e_ffn_w_down,
           o_norm1, o_w_qkv, o_q_norm, o_k_norm, o_w_out, o_norm2, o_router, o_exp_w_gate_up, o_exp_w_down):
    B, S, D = x.shape
    h = x.reshape(B * S, D)
    depth = e_norm1.shape[0] + o_norm1.shape[0]
    for layer in range(depth):
        i = layer // 2
        if layer % 2 == 0:
            mix = _even_mix(h, B, S, e_norm1[i], e_w_in[i], e_fox_f_bias[i], e_fox_q_norm[i], e_fox_k_norm[i],
                            e_gdn_conv[i], e_gdn_a_log[i], e_gdn_dt_bias[i], e_gdn_o_norm[i], e_w_out[i])
            h = ffn_residual(h, mix, e_norm2[i], e_ffn_w_gate_up[i].astype(BF16), e_ffn_w_down[i].astype(BF16),
                             tm=FFN_ROWS, tf=FFN_COLS)
        else:
            mix = _odd_mix(h, B, S, o_norm1[i], o_w_qkv[i], o_q_norm[i], o_k_norm[i], o_w_out[i])
            h = moe_residual(h, mix, o_norm2[i], o_router[i], o_exp_w_gate_up[i], o_exp_w_down[i])
    return h.reshape(B, S, D)
```

```python
import functools

import jax
import jax.numpy as jnp
from jax import lax
from jax.experimental import pallas as pl
from jax.experimental.pallas import tpu as pltpu
from jax.experimental.pallas import tpu_sc as plsc

F32 = jnp.float32
BF16 = jnp.bfloat16

HEAD_DIM = 128
FOX_HEADS = 4
GDN_HEADS = 4
CONV_WIDTH = 4
MOBA_BLOCK = 256
MOBA_TOPK = 3
N_EXPERTS = 8
ROPE_THETA = 10000.0
EPS = 1e-6

LANES = 128
GDN_CHUNK = 128
INV_BLOCK = 16
GDN_HEADS_PER_STEP = 4
GDN_CHUNKS_PER_STEP = 4
NEG = -(2.0 ** 100)
LOG2E = 1.4426950408889634
ATT_BLOCK = 256
ATT_HEADS = 4
KEY_BLOCKS_PER_STEP = 2
V_ROWS = HEAD_DIM + 16
PERM_TILE = 1024
ROUTER_CHAINS = 4
SC_ROWS = 64
ROW_TILE = 1024
MXU_COLS = 256
IN_PROJ_COLS = 7 * MXU_COLS
QKV_COLS = 6 * MXU_COLS
FFN_ROWS, FFN_COLS = 512, 1408
EXPERT_TILE = 1024
EXPERT_COLS = 896
EXPERT_WEIGHT_BUFFERS = 3
EXPERT_ROW_PARTS = 4
VMEM_LIMIT_BYTES = 56 * 1024 * 1024

FOX_Q0, FOX_K0, FOX_V0 = 0, 4, 8
GDN_Q0, GDN_K0, GDN_V0, GDN_G0 = 12, 16, 20, 24
LANE_F, LANE_A, LANE_B = 0, 4, 8


def _params(*sem):
    return pltpu.CompilerParams(dimension_semantics=sem, vmem_limit_bytes=VMEM_LIMIT_BYTES)


def _rms(x, gain):
    return x * lax.rsqrt(jnp.mean(x * x, axis=-1, keepdims=True) + EPS) * gain


def _dot_nt(a, b, **kw):
    return lax.dot_general(a, b, (((1,), (1,)), ((), ())), preferred_element_type=F32, **kw)


def _pick_lane(x, lane_idx):
    lane = lax.broadcasted_iota(jnp.int32, x.shape, 1)
    return jnp.sum(jnp.where(lane == lane_idx, x, 0.0), axis=-1, keepdims=True)


def _pack_bf16_pairs(x):
    n = x.shape[1] // 2
    hi = pltpu.bitcast(x[:, :n].astype(BF16).astype(F32), jnp.uint32)
    lo = pltpu.bitcast(x[:, n:].astype(BF16).astype(F32), jnp.uint32)
    return pltpu.bitcast(hi | (lo >> 16), jnp.int32)


def _unpack_bf16_pairs(w):
    u = pltpu.bitcast(w, jnp.uint32)
    hi = pltpu.bitcast(u & jnp.uint32(0xFFFF0000), F32).astype(BF16)
    lo = pltpu.bitcast(u << 16, F32).astype(BF16)
    return jnp.concatenate([hi, lo], axis=1)


def _norm_mm_kernel(x_ref, g_ref, w_ref, *rest, has_aux):
    if has_aux:
        waux_ref, o_ref, oaux_ref, xn_ref = rest
    else:
        o_ref, xn_ref = rest

    @pl.when(pl.program_id(1) == 0)
    def _():
        xn = _rms(x_ref[...], g_ref[...]).astype(BF16)
        xn_ref[...] = xn
        if has_aux:
            oaux_ref[...] = jnp.dot(xn, waux_ref[...], preferred_element_type=F32)

    o_ref[...] = jnp.dot(xn_ref[...], w_ref[...].astype(BF16), preferred_element_type=F32).astype(o_ref.dtype)


def norm_matmul(x, gain, w, w_aux=None, *, tm, tn, out_dtype=BF16):
    M, K = x.shape
    N = w.shape[1]
    has_aux = w_aux is not None
    in_specs = [
        pl.BlockSpec((tm, K), lambda i, j: (i, 0)),
        pl.BlockSpec((1, K), lambda i, j: (0, 0)),
        pl.BlockSpec((K, tn), lambda i, j: (0, j)),
    ]
    out_shape = [jax.ShapeDtypeStruct((M, N), out_dtype)]
    out_specs = [pl.BlockSpec((tm, tn), lambda i, j: (i, j))]
    args = [x, gain.reshape(1, K), w]
    if has_aux:
        in_specs.append(pl.BlockSpec((K, LANES), lambda i, j: (0, 0)))
        out_shape.append(jax.ShapeDtypeStruct((M, LANES), F32))
        out_specs.append(pl.BlockSpec((tm, LANES), lambda i, j: (i, 0)))
        args.append(w_aux)
    res = pl.pallas_call(
        functools.partial(_norm_mm_kernel, has_aux=has_aux),
        grid=(M // tm, N // tn),
        in_specs=in_specs,
        out_specs=out_specs,
        out_shape=out_shape,
        scratch_shapes=[pltpu.VMEM((tm, K), BF16)],
        compiler_params=_params("parallel", "arbitrary"),
        name="norm_matmul",
    )(*args)
    return res if has_aux else res[0]


def _plus_projections(x, proj_refs, rows=slice(None)):
    for a_ref, w_ref in zip(proj_refs[0::2], proj_refs[1::2]):
        x = x + jnp.dot(a_ref[rows, :], w_ref[...].astype(BF16), preferred_element_type=F32)
    return x


def _projection_specs(pairs, tm):
    specs, args = [], []
    for a, w in pairs:
        specs.append(pl.BlockSpec((tm, a.shape[1]), lambda i, *_: (i, 0)))
        specs.append(pl.BlockSpec(w.shape, lambda i, *_: (0, 0)))
        args += [a, w]
    return specs, args


def _gate_kernel(zs_ref, par_ref, col_ref, row_ref, *, S, B):
    C = GDN_CHUNK
    bias = par_ref[0:1, :]
    neg_a = -jnp.exp(par_ref[1:2, :])
    r = lax.broadcasted_iota(jnp.int32, (C, C), 0)
    c = lax.broadcasted_iota(jnp.int32, (C, C), 1)
    tril = (r >= c).astype(F32)
    lane = lax.broadcasted_iota(jnp.int32, (C, LANES), 1)

    def body(n, carry):
        sl = pl.ds(pl.multiple_of(n * C, C), C)
        us, betas = [], []
        for bi in range(B):
            z = zs_ref[bi, sl, :]
            t = z + bias
            soft = jnp.log(1.0 + jnp.exp(-jnp.abs(t)))
            log_f = jnp.minimum(t, 0.0) - soft
            g = neg_a * (jnp.maximum(t, 0.0) + soft)
            betas.append(1.0 / (1.0 + jnp.exp(-z)))
            us.append(jnp.where(lane < LANE_A, log_f, jnp.where(lane < LANE_B, g, 0.0)))
        sums = jnp.dot(tril, jnp.concatenate(us, axis=1), preferred_element_type=F32, precision=lax.Precision.HIGHEST)
        last = []
        for bi in range(B):
            cs = sums[:, bi * LANES:(bi + 1) * LANES] + jnp.where(lane < LANE_A, carry[bi], 0.0)
            out = jnp.where(lane < LANE_B, cs, betas[bi])
            col_ref[bi, sl, :] = out
            out_t = out.T
            for hd in range(GDN_HEADS):
                row_ref[bi, hd, :, sl] = out_t[LANE_A + hd:LANE_A + hd + 1, :]
            last.append(cs[C - 1:C, :])
        return tuple(last)

    lax.fori_loop(0, S // C, body, tuple(jnp.zeros((1, LANES), F32) for _ in range(B)))


def even_gates(zs, par):
    B, S, _ = zs.shape
    return pl.pallas_call(
        functools.partial(_gate_kernel, S=S, B=B),
        grid=(1,),
        in_specs=[
            pl.BlockSpec((B, S, LANES), lambda i: (0, 0, 0)),
            pl.BlockSpec((8, LANES), lambda i: (0, 0)),
        ],
        out_specs=[
            pl.BlockSpec((B, S, LANES), lambda i: (0, 0, 0)),
            pl.BlockSpec((B, GDN_HEADS, 1, S), lambda i: (0, 0, 0, 0)),
        ],
        out_shape=[jax.ShapeDtypeStruct((B, S, LANES), F32), jax.ShapeDtypeStruct((B, GDN_HEADS, 1, S), F32)],
        compiler_params=_params("arbitrary"),
        name="even_gates",
    )(zs, par)


def _split3(x):
    hi = x.astype(BF16).astype(F32)
    mid = (x - hi).astype(BF16).astype(F32)
    lo = (x - hi - mid).astype(BF16).astype(F32)
    return hi, mid, lo


def _interleave(gens):
    out = [None] * len(gens)
    live = list(range(len(gens)))
    while live:
        for n in list(live):
            try:
                next(gens[n])
            except StopIteration as stop:
                out[n] = stop.value
                live.remove(n)
    return out


def _attend(state, c, qa, ka_ref, vt_ref, key0, nkeys, keep=None):
    ks = pl.ds(pl.multiple_of(key0, ATT_BLOCK), nkeys)
    st = _dot_nt(ka_ref[ks, :], qa)
    yield
    if keep is not None:
        st = jnp.where(keep, st, NEG)
    m_new = jnp.max(st, axis=0, keepdims=True)
    if state[c] is not None:
        m_old, acc_old = state[c]
        m_new = jnp.maximum(m_old, m_new)
    p = jnp.exp2(st - m_new).astype(BF16)
    pv = jnp.dot(vt_ref[:, ks], p, preferred_element_type=F32)
    state[c] = (m_new, pv if state[c] is None else acc_old * jnp.exp2(m_old - m_new) + pv)


def _attend_tile_pair(i, prep, ka_ref, vt_ref, m_ref, acc_ref, o_ref):
    t = ATT_BLOCK
    chains = [(hd, c) for c in range(2) for hd in range(ATT_HEADS)]
    num = lambda hd, c: 2 * hd + c
    qa = _interleave(prep)

    def step(state, hd, c, which, **kw):
        return _attend(state, num(hd, c), qa[num(hd, c)][which], ka_ref.at[hd], vt_ref.at[hd], **kw)

    def save(state):
        m_ref[...] = jnp.stack([state[n][0] for n in range(len(chains))])
        acc_ref[...] = jnp.stack([state[n][1] for n in range(len(chains))])

    def past_blocks(key0, n_steps):
        state = {n: (m_ref[n], acc_ref[n]) for n in range(len(chains))}
        _interleave([step(state, hd, c, 1, key0=key0 + s * 2 * t, nkeys=2 * t)
                     for s in range(n_steps) for hd, c in chains])
        save(state)

    state = {n: None for n in range(len(chains))}
    keep = _causal_keep()
    _interleave([step(state, hd, c, 0, key0=(2 * i + c) * t, nkeys=t, keep=keep) for hd, c in chains]
                + [step(state, hd, 1, 1, key0=(2 * i) * t, nkeys=t) for hd in range(ATT_HEADS)])
    save(state)

    def four_blocks(g, _):
        past_blocks(g * (4 * t), 2)
        return 0

    lax.fori_loop(0, i // 2, four_blocks, 0)

    @pl.when((i & 1) != 0)
    def _():
        past_blocks((i // 2) * (4 * t), 1)

    for hd, c in chains:
        acc = acc_ref[num(hd, c)]
        out_t = acc[:HEAD_DIM, :] * (1.0 / acc[HEAD_DIM:HEAD_DIM + 1, :])
        o_ref[c * t:(c + 1) * t, hd * HEAD_DIM:(hd + 1) * HEAD_DIM] = out_t.T.astype(o_ref.dtype)


def _transposed(v):
    r = lax.broadcasted_iota(jnp.int32, (HEAD_DIM, HEAD_DIM), 0)
    c = lax.broadcasted_iota(jnp.int32, (HEAD_DIM, HEAD_DIM), 1)
    v_t = _dot_nt(jnp.where(r == c, 1.0, 0.0).astype(BF16), v).astype(BF16)
    return jnp.concatenate([v_t, jnp.ones((V_ROWS - HEAD_DIM, ATT_BLOCK), BF16)], axis=0)


def _causal_keep():
    key = lax.broadcasted_iota(jnp.int32, (ATT_BLOCK, ATT_BLOCK), 0)
    qry = lax.broadcasted_iota(jnp.int32, (ATT_BLOCK, ATT_BLOCK), 1)
    return key <= qry


_ATT_SCRATCH = lambda S: [pltpu.VMEM((ATT_HEADS, S, 2 * HEAD_DIM), BF16), pltpu.VMEM((ATT_HEADS, V_ROWS, S), BF16),
                          pltpu.VMEM((2 * ATT_HEADS, 1, ATT_BLOCK), F32),
                          pltpu.VMEM((2 * ATT_HEADS, V_ROWS, ATT_BLOCK), F32)]


def _fox_kernel(q_ref, k_ref, v_ref, col_ref, qg_ref, kg_ref, o_ref, ka_ref, vt_ref, m_ref, acc_ref, *, S):
    h0 = pl.program_id(1) * ATT_HEADS
    i = pl.program_id(2)
    t = ATT_BLOCK
    D = HEAD_DIM
    lane = lax.broadcasted_iota(jnp.int32, (t, LANES), 1)

    @pl.when(i == 0)
    def _():
        def prep_keys(g, _):
            todo = []
            for u in range(KEY_BLOCKS_PER_STEP):
                sl = pl.ds(pl.multiple_of((g * KEY_BLOCKS_PER_STEP + u) * t, t), t)
                for hd in range(ATT_HEADS):
                    cols = slice(hd * D, (hd + 1) * D)
                    kn = _rms(k_ref[sl, cols].astype(F32), kg_ref[...]).astype(BF16)
                    hi, mid, lo = _split3(-LOG2E * _pick_lane(col_ref[sl, :], LANE_F + h0 + hd))
                    aug = jnp.where(lane < 3, 1.0, jnp.where(lane == 3, hi, jnp.where(lane == 4, mid,
                                                                                    jnp.where(lane == 5, lo, 0.0))))
                    todo.append((hd, sl, jnp.concatenate([kn, aug.astype(BF16)], axis=1), _transposed(v_ref[sl, cols])))
            for hd, sl, ka, vt in todo:
                ka_ref[hd, sl, :] = ka
                vt_ref[hd, :, sl] = vt
            return 0

        lax.fori_loop(0, S // (t * KEY_BLOCKS_PER_STEP), prep_keys, 0)

    def prep(hd, c):
        q = q_ref[c * t:(c + 1) * t, hd * D:(hd + 1) * D].astype(F32)
        qn = (_rms(q, qg_ref[...]) * (LOG2E * D ** -0.5)).astype(BF16)
        yield
        qsl = pl.ds(pl.multiple_of((2 * i + c) * t, t), t)
        hi, mid, lo = _split3(LOG2E * _pick_lane(col_ref[qsl, :], LANE_F + h0 + hd))
        aug = jnp.where(lane == 0, hi,
                        jnp.where(lane == 1, mid, jnp.where(lane == 2, lo, jnp.where(lane < 6, 1.0, 0.0))))
        qa = jnp.concatenate([qn, aug.astype(BF16)], axis=-1)
        return qa, qa

    _attend_tile_pair(i, [prep(hd, c) for hd in range(ATT_HEADS) for c in range(2)],
                      ka_ref, vt_ref, m_ref, acc_ref, o_ref)


def fox_attention(z, col, q_gain, k_gain):
    B, S, _ = z.shape
    t = 2 * ATT_BLOCK
    w = ATT_HEADS * HEAD_DIM
    return pl.pallas_call(
        functools.partial(_fox_kernel, S=S),
        grid=(B, FOX_HEADS // ATT_HEADS, S // t),
        in_specs=[
            pl.BlockSpec((None, t, w), lambda b, h, i: (b, i, FOX_Q0 // ATT_HEADS + h)),
            pl.BlockSpec((None, S, w), lambda b, h, i: (b, 0, FOX_K0 // ATT_HEADS + h)),
            pl.BlockSpec((None, S, w), lambda b, h, i: (b, 0, FOX_V0 // ATT_HEADS + h)),
            pl.BlockSpec((None, S, LANES), lambda b, h, i: (b, 0, 0)),
            pl.BlockSpec((1, HEAD_DIM), lambda b, h, i: (0, 0)),
            pl.BlockSpec((1, HEAD_DIM), lambda b, h, i: (0, 0)),
        ],
        out_specs=pl.BlockSpec((None, t, w), lambda b, h, i: (b, i, h)),
        out_shape=jax.ShapeDtypeStruct((B, S, FOX_HEADS * HEAD_DIM), BF16),
        scratch_shapes=_ATT_SCRATCH(S),
        compiler_params=_params("parallel", "parallel", "arbitrary"),
        name="fox_attention",
    )(z, z, z, col, q_gain.reshape(1, -1), k_gain.reshape(1, -1))


def _unit_lower_inverse(m):
    n = m.shape[0]
    r = lax.broadcasted_iota(jnp.int32, (n, n), 0)
    c = lax.broadcasted_iota(jnp.int32, (n, n), 1)
    eye = (r == c).astype(F32)

    def same_block(b):
        return (r // b) == (c // b)

    p = jnp.where(same_block(INV_BLOCK), m, 0.0)
    inv = eye - p
    k = 2
    while k < INV_BLOCK:
        pb = p.astype(BF16)
        p = jnp.dot(pb, pb, preferred_element_type=F32)
        yield
        inv = jnp.dot(inv.astype(BF16), (eye + p).astype(BF16), preferred_element_type=F32)
        yield
        k *= 2
    b = INV_BLOCK
    while b < n:
        off = jnp.where(same_block(2 * b), jnp.where(same_block(b), 0.0, m), 0.0).astype(BF16)
        ib = inv.astype(BF16)
        left = jnp.dot(ib, off, preferred_element_type=F32).astype(BF16)
        yield
        inv = inv - jnp.dot(left, ib, preferred_element_type=F32)
        yield
        b *= 2
    return inv


def _gdn_kernel(q_ref, k_ref, v_ref, gg_ref, wq_ref, wk_ref, wv_ref, col_ref, row_ref, on_ref,
                o_ref, qs_ref, ks_ref, vs_ref, *, S, rows, hb):
    h0 = pl.program_id(1) * hb
    C = GDN_CHUNK
    D = HEAD_DIM

    streams = [(x_ref, w_ref, dst_ref, hh, mode) for hh in range(hb)
               for x_ref, w_ref, dst_ref, mode in ((q_ref, wq_ref, qs_ref, "q"), (k_ref, wk_ref, ks_ref, "k"),
                                                   (v_ref, wv_ref, vs_ref, "v"))]
    halo = 16

    def conv_step(base, windows):
        outs = []
        for (x_ref, w_ref, dst_ref, hh, mode), win in zip(streams, windows):
            w = w_ref[:, hh * D:(hh + 1) * D]
            y = jnp.zeros((rows, D), F32)
            for tap in range(CONV_WIDTH):
                lead = halo - (CONV_WIDTH - 1) + tap
                y = y + w[tap:tap + 1, :] * pltpu.roll(win, rows + halo - lead, 0)[0:rows, :]
            y = y / (1.0 + jnp.exp(-y))
            if mode != "v":
                y = y * lax.rsqrt(jnp.sum(y * y, axis=-1, keepdims=True) + EPS)
            if mode == "q":
                y = y * D ** -0.5
            outs.append(y)
        for (x_ref, w_ref, dst_ref, hh, mode), y in zip(streams, outs):
            dst_ref[hh, pl.ds(base, rows), :] = y.astype(dst_ref.dtype)

    conv_step(0, [jnp.concatenate([jnp.zeros((halo, D), F32), x_ref[0:rows, hh * D:(hh + 1) * D].astype(F32)], axis=0)
                  for x_ref, _, _, hh, _ in streams])

    def conv_rest(n, _):
        base = pl.multiple_of(n * rows, rows)
        conv_step(base, [x_ref[pl.ds(pl.multiple_of(base - halo, halo), rows + halo), hh * D:(hh + 1) * D].astype(F32)
                         for x_ref, _, _, hh, _ in streams])
        return 0

    lax.fori_loop(1, S // rows, conv_rest, 0)

    r = lax.broadcasted_iota(jnp.int32, (C, C), 0)
    c = lax.broadcasted_iota(jnp.int32, (C, C), 1)
    incl = r >= c
    strict = r > c

    def chunk_local(hh, sl, tab):
        q = qs_ref[hh, sl, :].astype(F32)
        k = ks_ref[hh, sl, :].astype(F32)
        v = vs_ref[hh, sl, :].astype(F32)
        gcol = _pick_lane(tab, LANE_A + h0 + hh)
        beta = _pick_lane(tab, LANE_B + h0 + hh)
        grow = row_ref[hh, :, sl]
        glast = gcol[C - 1:C, :]
        decay = jnp.where(incl, jnp.exp(jnp.where(incl, gcol - grow, 0.0)), 0.0)
        eg = jnp.exp(gcol)
        kb = k * beta
        kbf = k.astype(BF16)
        m = jnp.where(strict, _dot_nt(kb.astype(BF16), kbf) * decay, 0.0)
        attn = (_dot_nt(q.astype(BF16), kbf) * decay).astype(BF16)
        yield
        tinv = (yield from _unit_lower_inverse(m)).astype(BF16)
        rhs = jnp.concatenate([v * beta, kb * eg], axis=-1).astype(BF16)
        sol = jnp.dot(tinv, rhs, preferred_element_type=F32)
        yield
        gate = gg_ref[sl, hh * D:(hh + 1) * D].astype(F32)
        return dict(u=sol[:, :D], w=sol[:, D:].astype(BF16), attn=attn, qg=(q * eg).astype(BF16),
                    kg_t=(k * jnp.exp(glast - gcol)).T.astype(BF16), keep=jnp.exp(glast),
                    gate=gate / (1.0 + jnp.exp(-gate)))

    def chunk_state(parts, state):
        outs = []
        for c in parts:
            sb = state.astype(BF16)
            v_new = c["u"] - jnp.dot(c["w"], sb, preferred_element_type=F32)
            o_state = jnp.dot(c["qg"], sb, preferred_element_type=F32)
            yield
            vb = v_new.astype(BF16)
            o = o_state + jnp.dot(c["attn"], vb, preferred_element_type=F32)
            state = state * c["keep"] + jnp.dot(c["kg_t"], vb, preferred_element_type=F32)
            yield
            outs.append((_rms(o, on_ref[...]) * c["gate"]).astype(o_ref.dtype))
        return state, outs

    def chunk_group(n, states):
        sls = [pl.ds(pl.multiple_of((n * GDN_CHUNKS_PER_STEP + g) * C, C), C) for g in range(GDN_CHUNKS_PER_STEP)]
        tabs = [col_ref[sl, :] for sl in sls]
        parts = _interleave([chunk_local(hh, sls[g], tabs[g]) for g in range(GDN_CHUNKS_PER_STEP) for hh in range(hb)])
        res = _interleave([chunk_state([parts[g * hb + hh] for g in range(GDN_CHUNKS_PER_STEP)], states[hh])
                           for hh in range(hb)])
        for g in range(GDN_CHUNKS_PER_STEP):
            o_ref[sls[g], :] = jnp.concatenate([outs[g] for _, outs in res], axis=-1)
        return tuple(s for s, _ in res)

    lax.fori_loop(0, S // (C * GDN_CHUNKS_PER_STEP), chunk_group, tuple(jnp.zeros((D, D), F32) for _ in range(hb)))


def gdn_mixer(z, conv_w, col, row, o_gain):
    B, S, _ = z.shape
    D = HEAD_DIM
    hb = GDN_HEADS_PER_STEP
    rows = 256
    seq = lambda off: pl.BlockSpec((None, S, hb * D), lambda b, h: (b, 0, off // hb + h),
                                   pipeline_mode=pl.Buffered(1))
    cw = lambda off: pl.BlockSpec((CONV_WIDTH, hb * D), lambda b, h: (0, off // hb + h))
    return pl.pallas_call(
        functools.partial(_gdn_kernel, S=S, rows=rows, hb=hb),
        grid=(B, GDN_HEADS // hb),
        in_specs=[
            seq(GDN_Q0), seq(GDN_K0), seq(GDN_V0), seq(GDN_G0),
            cw(0), cw(GDN_HEADS), cw(2 * GDN_HEADS),
            pl.BlockSpec((None, S, LANES), lambda b, h: (b, 0, 0)),
            pl.BlockSpec((None, hb, 1, S), lambda b, h: (b, h, 0, 0)),
            pl.BlockSpec((1, D), lambda b, h: (0, 0)),
        ],
        out_specs=pl.BlockSpec((None, S, hb * D), lambda b, h: (b, 0, h)),
        out_shape=jax.ShapeDtypeStruct((B, S, GDN_HEADS * D), BF16),
        scratch_shapes=[pltpu.VMEM((hb, S, D), BF16)] * 3,
        compiler_params=_params("parallel", "parallel"),
        name="gdn_mixer",
    )(z, z, z, z, conv_w, conv_w, conv_w, col, row, o_gain.reshape(1, D))


def _rope(x, cos, sin_signed):
    return x * cos + pltpu.roll(x, HEAD_DIM // 2, 1) * sin_signed


def _moba_kernel(q_ref, k_ref, v_ref, cos_ref, sin_ref, qg_ref, kg_ref, o_ref,
                 ka_ref, vt_ref, m_ref, acc_ref, kmean_ref, *, S):
    i = pl.program_id(2)
    t = ATT_BLOCK
    D = HEAD_DIM
    lane = lax.broadcasted_iota(jnp.int32, (t, LANES), 1)

    @pl.when(i == 0)
    def _():
        kmean_ref[...] = jnp.zeros(kmean_ref.shape, F32)

        def prep_keys(g, _):
            todo = []
            for u in range(KEY_BLOCKS_PER_STEP):
                n = g * KEY_BLOCKS_PER_STEP + u
                sl = pl.ds(pl.multiple_of(n * t, t), t)
                onehot = jnp.where(lane == n, 1.0, 0.0).astype(BF16)
                for hd in range(ATT_HEADS):
                    cols = slice(hd * D, (hd + 1) * D)
                    k = _rope(_rms(k_ref[sl, cols].astype(F32), kg_ref[...]), cos_ref[sl, :], sin_ref[sl, :])
                    todo.append((hd, n, sl, jnp.concatenate([k.astype(BF16), onehot], axis=1),
                                 jnp.mean(k, axis=0, keepdims=True), _transposed(v_ref[sl, cols])))
            for hd, n, sl, ka, kmean, vt in todo:
                ka_ref[hd, sl, :] = ka
                kmean_ref[hd, pl.ds(n, 1), :] = kmean
                vt_ref[hd, :, sl] = vt
            return 0

        lax.fori_loop(0, S // (t * KEY_BLOCKS_PER_STEP), prep_keys, 0)

    nb = -(-(S // t) // 8) * 8

    def prep(hd, c):
        cur = 2 * i + c
        qsl = pl.ds(pl.multiple_of(cur * t, t), t)
        q = _rope(_rms(q_ref[c * t:(c + 1) * t, hd * D:(hd + 1) * D].astype(F32), qg_ref[...]),
                  cos_ref[qsl, :], sin_ref[qsl, :])
        yield
        kmean = kmean_ref[hd]
        km_hi = kmean.astype(BF16)
        km_split = jnp.concatenate([km_hi, (kmean - km_hi.astype(F32)).astype(BF16)], axis=0)
        q_hi = q.astype(BF16)
        q_lo = (q - q_hi.astype(F32)).astype(BF16)
        part = _dot_nt(km_split, q_hi)
        gate = (part[:LANES] + part[LANES:] + _dot_nt(km_split[:LANES, :], q_lo))[0:nb]
        yield
        blk = lax.broadcasted_iota(jnp.int32, (nb, t), 0)
        blk_f = blk.astype(F32)
        gate = jnp.where(blk < cur, gate, -jnp.inf)
        sel = jnp.full((nb, t), NEG, F32)
        for _ in range(MOBA_TOPK):
            top = jnp.max(gate, axis=0, keepdims=True)
            first = jnp.min(jnp.where(gate == top, blk_f, float(LANES)), axis=0, keepdims=True)
            pick = blk_f == first
            sel = jnp.where(pick & (first < cur.astype(F32)), 0.0, sel)
            gate = jnp.where(pick, -jnp.inf, gate)
        sel_bias = jnp.concatenate([sel, jnp.full((LANES - nb, t), NEG, F32)], axis=0).T
        qs = (q * (LOG2E * D ** -0.5)).astype(BF16)
        return (jnp.concatenate([qs, jnp.zeros((t, LANES), BF16)], axis=-1),
                jnp.concatenate([qs, sel_bias.astype(BF16)], axis=-1))

    _attend_tile_pair(i, [prep(hd, c) for hd in range(ATT_HEADS) for c in range(2)],
                      ka_ref, vt_ref, m_ref, acc_ref, o_ref)


def moba_attention(z, cos, sin_signed, q_gain, k_gain):
    B, S, W = z.shape
    H = W // (3 * HEAD_DIM)
    assert ATT_BLOCK == MOBA_BLOCK
    t = 2 * ATT_BLOCK
    w = ATT_HEADS * HEAD_DIM
    hp = H // ATT_HEADS
    return pl.pallas_call(
        functools.partial(_moba_kernel, S=S),
        grid=(B, hp, S // t),
        in_specs=[
            pl.BlockSpec((None, t, w), lambda b, h, i: (b, i, h)),
            pl.BlockSpec((None, S, w), lambda b, h, i: (b, 0, hp + h)),
            pl.BlockSpec((None, S, w), lambda b, h, i: (b, 0, 2 * hp + h)),
            pl.BlockSpec((S, HEAD_DIM), lambda b, h, i: (0, 0)),
            pl.BlockSpec((S, HEAD_DIM), lambda b, h, i: (0, 0)),
            pl.BlockSpec((1, HEAD_DIM), lambda b, h, i: (0, 0)),
            pl.BlockSpec((1, HEAD_DIM), lambda b, h, i: (0, 0)),
        ],
        out_specs=pl.BlockSpec((None, t, w), lambda b, h, i: (b, i, h)),
        out_shape=jax.ShapeDtypeStruct((B, S, H * HEAD_DIM), BF16),
        scratch_shapes=_ATT_SCRATCH(S) + [pltpu.VMEM((ATT_HEADS, LANES, HEAD_DIM), F32)],
        compiler_params=_params("parallel", "parallel", "arbitrary"),
        name="moba_attention",
    )(z, z, z, cos, sin_signed, q_gain.reshape(1, -1), k_gain.reshape(1, -1))


META_IDX, META_RANK, META_GATE = 0, 2, 4


def _router_kernel(x_ref, *refs, n_proj):
    proj_refs = refs[:2 * n_proj]
    g_ref, w_ref, y_ref, xn_ref, meta_ref, cnt_ref, carry_ref = refs[2 * n_proj:]
    i = pl.program_id(0)
    tm = x_ref.shape[0]

    @pl.when(i == 0)
    def _():
        carry_ref[...] = jnp.zeros(carry_ref.shape, F32)

    w = w_ref[...]
    w_hi = w.astype(BF16)
    w_lo = (w - w_hi.astype(F32)).astype(BF16)
    part = tm // ROUTER_CHAINS
    lane = lax.broadcasted_iota(jnp.int32, (part, LANES), 1)
    lane_f = lane.astype(F32)

    def route(n):
        rows = slice(n * part, (n + 1) * part)
        y = _plus_projections(x_ref[rows, :], proj_refs, rows)
        y_ref[rows, :] = y
        yield
        xn = _rms(y, g_ref[...])
        xn_ref[rows, :] = _pack_bf16_pairs(xn)
        x_hi = xn.astype(BF16)
        x_lo = (xn - x_hi.astype(F32)).astype(BF16)
        logits = (jnp.dot(x_hi, w_hi, preferred_element_type=F32) + jnp.dot(x_hi, w_lo, preferred_element_type=F32)
                  + jnp.dot(x_lo, w_hi, preferred_element_type=F32))
        yield
        logits = jnp.where(lane < N_EXPERTS, logits, -jnp.inf)
        top1 = jnp.max(logits, axis=-1, keepdims=True)
        yield
        idx1 = jnp.min(jnp.where(logits == top1, lane_f, float(LANES)), axis=-1, keepdims=True)
        yield
        rest = jnp.where(lane_f == idx1, -jnp.inf, logits)
        top2 = jnp.max(rest, axis=-1, keepdims=True)
        yield
        idx2 = jnp.min(jnp.where(rest == top2, lane_f, float(LANES)), axis=-1, keepdims=True)
        e2 = jnp.exp(top2 - top1)
        denom = 1.0 + e2
        return idx1, idx2, 1.0 / denom, e2 / denom, jnp.where((lane_f == idx1) | (lane_f == idx2), 1.0, 0.0)

    picks = _interleave([route(n) for n in range(ROUTER_CHAINS)])
    chosen = jnp.concatenate([p[4] for p in picks], axis=0)
    r = lax.broadcasted_iota(jnp.int32, (tm, tm), 0)
    c = lax.broadcasted_iota(jnp.int32, (tm, tm), 1)
    ahead = jnp.dot(jnp.where(r > c, 1.0, 0.0).astype(BF16), chosen.astype(BF16), preferred_element_type=F32)
    carry = carry_ref[...]
    for n, (idx1, idx2, g1, g2, _) in enumerate(picks):
        rows = slice(n * part, (n + 1) * part)
        rank = ahead[rows, :] + carry
        rank1 = jnp.sum(jnp.where(lane_f == idx1, rank, 0.0), axis=-1, keepdims=True)
        rank2 = jnp.sum(jnp.where(lane_f == idx2, rank, 0.0), axis=-1, keepdims=True)
        meta = jnp.zeros((part, LANES), F32)
        for k, v in enumerate((idx1, idx2, rank1, rank2, g1, g2)):
            meta = jnp.where(lane == k, v, meta)
        meta_ref[rows, :] = meta
    carry = carry + jnp.sum(chosen, axis=0, keepdims=True)
    carry_ref[...] = carry
    cnt_ref[...] = jnp.broadcast_to(carry, cnt_ref.shape)


def moe_router(x, proj, gain, w_router, *, tm):
    M, K = x.shape
    w = jnp.zeros((K, LANES), F32).at[:, :N_EXPERTS].set(w_router)
    proj_specs, proj_args = _projection_specs(proj, tm)
    return pl.pallas_call(
        functools.partial(_router_kernel, n_proj=len(proj)),
        grid=(M // tm,),
        in_specs=[
            pl.BlockSpec((tm, K), lambda i: (i, 0)),
            *proj_specs,
            pl.BlockSpec((1, K), lambda i: (0, 0)),
            pl.BlockSpec((K, LANES), lambda i: (0, 0)),
        ],
        out_specs=[
            pl.BlockSpec((tm, K), lambda i: (i, 0)),
            pl.BlockSpec((tm, K // 2), lambda i: (i, 0)),
            pl.BlockSpec((tm, LANES), lambda i: (i, 0)),
            pl.BlockSpec((8, LANES), lambda i: (0, 0)),
        ],
        out_shape=[
            jax.ShapeDtypeStruct((M, K), F32),
            jax.ShapeDtypeStruct((M, K // 2), jnp.int32),
            jax.ShapeDtypeStruct((M, LANES), F32),
            jax.ShapeDtypeStruct((8, LANES), F32),
        ],
        scratch_shapes=[pltpu.VMEM((1, LANES), F32)],
        compiler_params=_params("arbitrary"),
        name="moe_router",
    )(x, *proj_args, gain.reshape(1, K), w)


def _sc_workers():
    info = plsc.get_sparse_core_info()
    return info.num_cores, info.num_cores * info.num_subcores


def scatter_rows(rows, dest, n_out):
    M, W = rows.shape
    nc, nw = _sc_workers()
    assert M % (nw * SC_ROWS) == 0
    per_w = M // nw
    mesh = plsc.VectorSubcoreMesh(core_axis_name="c", subcore_axis_name="s")

    n_chunks = per_w // SC_ROWS
    assert n_chunks % 2 == 0
    buf = [pltpu.VMEM((SC_ROWS, W), rows.dtype), pltpu.VMEM((SC_ROWS,), jnp.int32), pltpu.VMEM((SC_ROWS,), jnp.int32),
           pltpu.SemaphoreType.DMA]

    @functools.partial(pl.kernel, mesh=mesh, out_type=jax.ShapeDtypeStruct((n_out, W), rows.dtype),
                       scratch_types=buf + buf)
    def kern(rows_hbm, dest_hbm, out_hbm, rows0, ia0, ib0, sem0, rows1, ia1, ib1, sem1):
        wid = lax.axis_index("s") * nc + lax.axis_index("c")
        slots = ((rows0, (ia0, ib0), sem0), (rows1, (ia1, ib1), sem1))

        def send(g, slot):
            rows_v, idx, sem = slot
            base = wid * per_w + g * SC_ROWS
            pltpu.sync_copy(rows_hbm.at[pl.ds(base, SC_ROWS)], rows_v)
            for k in range(2):
                pltpu.sync_copy(dest_hbm.at[k, pl.ds(base, SC_ROWS)], idx[k])
                pltpu.async_copy(rows_v, out_hbm.at[idx[k]], sem)

        def drain(slot):
            rows_v, idx, sem = slot
            for k in range(2):
                pltpu.make_async_copy(rows_v, out_hbm.at[idx[k]], sem).wait()

        send(0, slots[0])

        @pl.loop(0, n_chunks, step=2)
        def _(g):
            send(g + 1, slots[1])
            drain(slots[0])

            @pl.when(g + 2 < n_chunks)
            def _():
                send(g + 2, slots[0])

            drain(slots[1])

    return kern(rows, dest)


def gather_rows(table, idx):
    N = idx.shape[0]
    W = table.shape[1]
    nc, nw = _sc_workers()
    assert N % (nw * SC_ROWS) == 0
    per_w = N // nw
    mesh = plsc.VectorSubcoreMesh(core_axis_name="c", subcore_axis_name="s")

    n_chunks = per_w // SC_ROWS
    assert n_chunks % 2 == 0
    buf = [pltpu.VMEM((SC_ROWS,), jnp.int32), pltpu.VMEM((SC_ROWS, W), table.dtype), pltpu.SemaphoreType.DMA]

    @functools.partial(pl.kernel, mesh=mesh, out_type=jax.ShapeDtypeStruct((N, W), table.dtype),
                       scratch_types=buf + buf)
    def kern(table_hbm, idx_hbm, out_hbm, idx0, rows0, sem0, idx1, rows1, sem1):
        wid = lax.axis_index("s") * nc + lax.axis_index("c")
        slots = ((idx0, rows0, sem0), (idx1, rows1, sem1))

        def fetch(g, slot):
            idx_v, rows_v, sem = slot
            pltpu.sync_copy(idx_hbm.at[pl.ds(wid * per_w + g * SC_ROWS, SC_ROWS)], idx_v)
            pltpu.async_copy(table_hbm.at[idx_v], rows_v, sem)

        def drain(g, slot):
            idx_v, rows_v, sem = slot
            pltpu.make_async_copy(table_hbm.at[idx_v], rows_v, sem).wait()
            pltpu.sync_copy(rows_v, out_hbm.at[pl.ds(wid * per_w + g * SC_ROWS, SC_ROWS)])

        fetch(0, slots[0])

        @pl.loop(0, n_chunks, step=2)
        def _(g):
            fetch(g + 1, slots[1])
            drain(g, slots[0])

            @pl.when(g + 2 < n_chunks)
            def _():
                fetch(g + 2, slots[0])

            drain(g + 1, slots[1])

    return kern(table, idx)


def _combine_kernel(h_ref, y1_ref, y2_ref, meta_ref, o_ref):
    g1 = meta_ref[:, META_GATE:META_GATE + 1]
    g2 = meta_ref[:, META_GATE + 1:META_GATE + 2]
    o_ref[...] = (h_ref[...] + g1 * _unpack_bf16_pairs(y1_ref[...]).astype(F32)
                  + g2 * _unpack_bf16_pairs(y2_ref[...]).astype(F32))


def moe_combine(h, y_pairs, meta):
    M, K = h.shape
    p = PERM_TILE
    nt = M // p
    return pl.pallas_call(
        _combine_kernel,
        grid=(nt,),
        in_specs=[
            pl.BlockSpec((p, K), lambda i: (i, 0)),
            pl.BlockSpec((p, K // 2), lambda i: (i, 0)),
            pl.BlockSpec((p, K // 2), lambda i: (nt + i, 0)),
            pl.BlockSpec((p, LANES), lambda i: (i, 0)),
        ],
        out_specs=pl.BlockSpec((p, K), lambda i: (i, 0)),
        out_shape=jax.ShapeDtypeStruct((M, K), F32),
        compiler_params=_params("parallel"),
        name="moe_combine",
    )(h, y_pairs, y_pairs, meta)


def _expert_ffn_kernel(te_ref, tv_ref, x_ref, wgu_hbm, wd_hbm, o_ref, acc_ref, wg_buf, wu_buf, wd_buf, sem, *, nf):
    i = pl.program_id(0)
    j = pl.program_id(1)
    tm = x_ref.shape[0]
    tf = wg_buf.shape[2]
    n_tiles = pl.num_programs(0)
    valid = tv_ref[i]
    step = i * nf + j

    def weight_copies(s):
        tile = s // nf
        c0 = pl.multiple_of((s - tile * nf) * tf, LANES)
        e = te_ref[tile]
        slot = s % EXPERT_WEIGHT_BUFFERS
        return [pltpu.make_async_copy(wgu_hbm.at[e, :, pl.ds(c0, tf)], wg_buf.at[slot], sem.at[slot]),
                pltpu.make_async_copy(wgu_hbm.at[e, :, pl.ds(nf * tf + c0, tf)], wu_buf.at[slot], sem.at[slot]),
                pltpu.make_async_copy(wd_hbm.at[e, pl.ds(c0, tf), :], wd_buf.at[slot], sem.at[slot])]

    def fetch(s):
        @pl.when((s < n_tiles * nf) & (tv_ref[jnp.minimum(s // nf, n_tiles - 1)] > 0))
        def _():
            for cp in weight_copies(s):
                cp.start()

    @pl.when(step == 0)
    def _():
        for ahead in range(EXPERT_WEIGHT_BUFFERS - 1):
            fetch(step + ahead)

    @pl.when(valid > 0)
    def _():
        fetch(step + EXPERT_WEIGHT_BUFFERS - 1)
        for cp in weight_copies(step):
            cp.wait()

    slot = step % EXPERT_WEIGHT_BUFFERS
    wg_ref, wu_ref, wd_ref = wg_buf.at[slot], wu_buf.at[slot], wd_buf.at[slot]

    def swiglu_rows(rows):
        x = _unpack_bf16_pairs(x_ref[0:rows, :])
        gt = jnp.dot(x, wg_ref[...].astype(BF16), preferred_element_type=F32)
        up = jnp.dot(x, wu_ref[...].astype(BF16), preferred_element_type=F32)
        act = (gt / (1.0 + jnp.exp(-gt)) * up).astype(BF16)
        part = jnp.dot(act, wd_ref[...].astype(BF16), preferred_element_type=F32)

        def finish(total):
            o_ref[0:rows, :] = _pack_bf16_pairs(total)
            if rows < tm:
                o_ref[rows:tm, :] = jnp.zeros((tm - rows, o_ref.shape[1]), o_ref.dtype)

        if nf == 1:
            finish(part)
            return

        @pl.when(j == 0)
        def _():
            acc_ref[0:rows, :] = part

        @pl.when((j > 0) & (j < nf - 1))
        def _():
            acc_ref[0:rows, :] += part

        @pl.when(j == nf - 1)
        def _():
            finish(acc_ref[0:rows, :] + part)

    part = tm // EXPERT_ROW_PARTS
    for n in range(1, EXPERT_ROW_PARTS + 1):
        pl.when((valid > (n - 1) * part) & (valid <= n * part))(functools.partial(swiglu_rows, n * part))

    @pl.when((valid == 0) & (j == 0))
    def _():
        o_ref[...] = jnp.zeros(o_ref.shape, o_ref.dtype)


def expert_ffn(xs, tile_expert, tile_valid, w_gate_up, w_down, *, tm, tf):
    R = xs.shape[0]
    E, F, K = w_down.shape
    nf = F // tf
    grid_spec = pltpu.PrefetchScalarGridSpec(
        num_scalar_prefetch=2,
        grid=(R // tm, nf),
        in_specs=[
            pl.BlockSpec((tm, K // 2), lambda i, j, te, tv: (te[R // tm + i], 0)),
            pl.BlockSpec(memory_space=pl.ANY),
            pl.BlockSpec(memory_space=pl.ANY),
        ],
        out_specs=pl.BlockSpec((tm, K // 2), lambda i, j, te, tv: (i, 0)),
        scratch_shapes=[pltpu.VMEM((tm, K), F32),
                        pltpu.VMEM((EXPERT_WEIGHT_BUFFERS, K, tf), F32), pltpu.VMEM((EXPERT_WEIGHT_BUFFERS, K, tf), F32),
                        pltpu.VMEM((EXPERT_WEIGHT_BUFFERS, tf, K), F32),
                        pltpu.SemaphoreType.DMA((EXPERT_WEIGHT_BUFFERS,))],
    )
    return pl.pallas_call(
        functools.partial(_expert_ffn_kernel, nf=nf),
        grid_spec=grid_spec,
        out_shape=jax.ShapeDtypeStruct((R, K // 2), jnp.int32),
        compiler_params=_params("arbitrary", "arbitrary"),
        name="moe_expert_ffn",
    )(tile_expert, tile_valid, xs, w_gate_up, w_down)


def moe_residual(h, proj, gain, w_router, w_gate_up, w_down):
    M, K = h.shape
    p = PERM_TILE
    tm = EXPERT_TILE
    n_rows = 2 * M + N_EXPERTS * tm
    h, xn, meta, cnt = moe_router(h, proj, gain, w_router, tm=p)

    counts = cnt[0, :N_EXPERTS].astype(jnp.int32)
    padded = (counts + tm - 1) // tm * tm
    ends = jnp.cumsum(padded)
    offsets = ends - padded
    n_tiles = n_rows // tm
    tile_row = jnp.arange(n_tiles) * tm
    n_used = ends[-1] // tm
    last_used = jnp.minimum(tile_row // tm, n_used - 1)
    expert_of = lambda row: jnp.minimum(jnp.sum(ends[None, :] <= row[:, None], axis=1), N_EXPERTS - 1)
    tile_expert = expert_of(last_used * tm)
    row_in_expert = tile_row - jnp.sum(jnp.where(tile_expert[:, None] == jnp.arange(N_EXPERTS), offsets, 0), axis=1)
    own_count = jnp.sum(jnp.where(tile_expert[:, None] == jnp.arange(N_EXPERTS), counts, 0), axis=1)
    tile_valid = jnp.where(tile_row // tm < n_used, jnp.clip(own_count - row_in_expert, 0, tm), 0)
    tile_tables = jnp.concatenate([tile_expert, last_used]).astype(jnp.int32)

    idx = meta[:, META_IDX:META_IDX + 2].astype(jnp.int32)
    rank = meta[:, META_RANK:META_RANK + 2].astype(jnp.int32)
    dest = (jnp.sum(jnp.where(idx[:, :, None] == jnp.arange(N_EXPERTS), offsets, 0), axis=-1) + rank).T

    xs = scatter_rows(xn, dest, n_rows)
    ys = expert_ffn(xs, tile_tables, tile_valid.astype(jnp.int32), w_gate_up, w_down, tm=tm, tf=EXPERT_COLS)
    return moe_combine(h, gather_rows(ys, dest.reshape(-1)), meta)


def _ffn_kernel(x_ref, *refs, n_proj):
    proj_refs = refs[:2 * n_proj]
    g_ref, wg_ref, wu_ref, wd_ref, o_ref, xn_ref, acc_ref = refs[2 * n_proj:]
    j = pl.program_id(1)

    @pl.when(j == 0)
    def _():
        x = _plus_projections(x_ref[...], proj_refs)
        xn_ref[...] = _rms(x, g_ref[...]).astype(BF16)
        acc_ref[...] = x

    xn = xn_ref[...]
    gt = jnp.dot(xn, wg_ref[...], preferred_element_type=F32)
    up = jnp.dot(xn, wu_ref[...], preferred_element_type=F32)
    act = gt / (1.0 + jnp.exp(-gt)) * up
    acc_ref[...] += jnp.dot(act.astype(BF16), wd_ref[...], preferred_element_type=F32)

    @pl.when(j == pl.num_programs(1) - 1)
    def _():
        o_ref[...] = acc_ref[...]


def ffn_residual(x, proj, gain, w_gate_up, w_down, *, tm, tf):
    M, K = x.shape
    F = w_down.shape[0]
    nf = F // tf
    proj_specs, proj_args = _projection_specs(proj, tm)
    return pl.pallas_call(
        functools.partial(_ffn_kernel, n_proj=len(proj)),
        grid=(M // tm, nf),
        in_specs=[
            pl.BlockSpec((tm, K), lambda i, j: (i, 0)),
            *proj_specs,
            pl.BlockSpec((1, K), lambda i, j: (0, 0)),
            pl.BlockSpec((K, tf), lambda i, j: (0, j)),
            pl.BlockSpec((K, tf), lambda i, j: (0, nf + j)),
            pl.BlockSpec((tf, K), lambda i, j: (j, 0)),
        ],
        out_specs=pl.BlockSpec((tm, K), lambda i, j: (i, 0)),
        out_shape=jax.ShapeDtypeStruct((M, K), F32),
        scratch_shapes=[pltpu.VMEM((tm, K), BF16), pltpu.VMEM((tm, K), F32)],
        compiler_params=_params("parallel", "arbitrary"),
        name="ffn_residual",
    )(x, *proj_args, gain.reshape(1, K), w_gate_up, w_gate_up, w_down)


def _even_mix(h, B, S, norm1, w_in, f_bias, q_norm, k_norm, conv_w, a_log, dt_bias, o_norm, w_out):
    M, D = h.shape
    fw, gw = FOX_HEADS * HEAD_DIM, GDN_HEADS * HEAD_DIM
    o_ff = 3 * fw
    o_gq = o_ff + FOX_HEADS
    o_ga = o_gq + 3 * gw
    o_gb = o_ga + GDN_HEADS
    o_gg = o_gb + GDN_HEADS
    w_big = jnp.concatenate([w_in[:, :o_ff], w_in[:, o_gq:o_ga], w_in[:, o_gg:]], axis=1)
    w_small = jnp.concatenate([w_in[:, o_ff:o_gq], w_in[:, o_ga:o_gg],
                               jnp.zeros((D, LANES - FOX_HEADS - 2 * GDN_HEADS), F32)], axis=1)
    w_big, w_small = (t.astype(BF16) for t in lax.optimization_barrier((w_big, w_small)))
    z, zs = norm_matmul(h, norm1, w_big, w_small, tm=ROW_TILE, tn=IN_PROJ_COLS)
    par = jnp.zeros((8, LANES), F32)
    par = par.at[0, LANE_F:LANE_F + FOX_HEADS].set(f_bias).at[0, LANE_A:LANE_A + GDN_HEADS].set(dt_bias)
    par = par.at[1, LANE_A:LANE_A + GDN_HEADS].set(a_log)
    col, row = even_gates(zs.reshape(B, S, LANES), par)
    z = z.reshape(B, S, -1)
    fox = fox_attention(z, col, q_norm, k_norm)
    gdn = gdn_mixer(z, conv_w, col, row, o_norm)
    return [(fox.reshape(M, fw), w_out[:fw]), (gdn.reshape(M, gw), w_out[fw:])]


def _odd_mix(h, B, S, norm1, w_qkv, q_norm, k_norm, w_out):
    M, D = h.shape
    z = norm_matmul(h, norm1, w_qkv.astype(BF16), tm=ROW_TILE, tn=QKV_COLS).reshape(B, S, -1)
    half = HEAD_DIM // 2
    inv = jnp.power(ROPE_THETA, -jnp.arange(half, dtype=F32) / half)
    ang = jnp.arange(S, dtype=F32)[:, None] * inv[None, :]
    cos, sin = jnp.cos(ang), jnp.sin(ang)
    cos_full = jnp.concatenate([cos, cos], axis=-1)
    sin_signed = jnp.concatenate([-sin, sin], axis=-1)
    att = moba_attention(z, cos_full, sin_signed, q_norm, k_norm)
    return [(att.reshape(M, -1), w_out)]


def kernel(x, e_norm1, e_w_in, e_fox_f_bias, e_fox_q_norm, e_fox_k_norm, e_gdn_conv, e_gdn_a_log,
           e_gdn_dt_bias, e_gdn_o_norm, e_w_out, e_norm2, e_ffn_w_gate_up, e_ffn_w_down,
           o_norm1, o_w_qkv, o_q_norm, o_k_norm, o_w_out, o_norm2, o_router, o_exp_w_gate_up, o_exp_w_down):
    B, S, D = x.shape
    h = x.reshape(B * S, D)
    depth = e_norm1.shape[0] + o_norm1.shape[0]
    for layer in range(depth):
        i = layer // 2
        if layer % 2 == 0:
            mix = _even_mix(h, B, S, e_norm1[i], e_w_in[i], e_fox_f_bias[i], e_fox_q_norm[i], e_fox_k_norm[i],
                            e_gdn_conv[i], e_gdn_a_log[i], e_gdn_dt_bias[i], e_gdn_o_norm[i], e_w_out[i])
            h = ffn_residual(h, mix, e_norm2[i], e_ffn_w_gate_up[i].astype(BF16), e_ffn_w_down[i].astype(BF16),
                             tm=FFN_ROWS, tf=FFN_COLS)
        else:
            mix = _odd_mix(h, B, S, o_norm1[i], o_w_qkv[i], o_q_norm[i], o_k_norm[i], o_w_out[i])
            h = moe_residual(h, mix, o_norm2[i], o_router[i], o_exp_w_gate_up[i], o_exp_w_down[i])
    return h.reshape(B, S, D)
```
